```python
import math
import jax
import jax.numpy as jnp
from jax import lax
import numpy as np

D_MODEL = 1024
BATCH = 16
SEQ = 2048
DEPTH = 2
DEC_BATCH = 8
DEC_SEQ = 16
PAST_LEN = 4096

CHUNK = 64
Q_BLOCK = 128
HEAD_DIM = 64
H_A = 8
H_B = 8
BAND_CHUNKS = 8
WINDOW_B = BAND_CHUNKS * CHUNK
MAX_REL = 128
H_C = 8
D_FF = 2816
N_EXPERTS = 8
TOP_K = 2
D_FF_EXPERT = 3584
MOE_GROUP = 256
N_EVEN = (DEPTH + 1) // 2
N_ODD = DEPTH // 2
W_A = H_A * HEAD_DIM
W_B = H_B * HEAD_DIM
D_IN_EVEN = 3 * W_A + H_A + 3 * W_B
W_C = H_C * 2 * HEAD_DIM
D_IN_ODD = 3 * W_C
FORGET_BIAS = 4.0
RMS_EPS = 1e-6
NEG_INF = -1e30

kernel_name = 'streaming_hybrid_encoder_step'


def rmsnorm(x, g):
    xf = x.astype(jnp.float32)
    y = xf * lax.rsqrt(jnp.mean(xf * xf, axis=-1, keepdims=True) + RMS_EPS)
    return (y * g.astype(jnp.float32)).astype(x.dtype)


def modulate(x, g, shift, scale):
    return rmsnorm(x, g) * (1 + scale[:, None, :]) + shift[:, None, :]


def alibi_slopes(n_heads):
    return jnp.asarray([2.0 ** (-8.0 * (h + 1) / n_heads) for h in range(n_heads)], jnp.float32)


def sweep_query_blocks(attend, q_arrays, q_pos):
    s = q_pos.shape[0]
    nb = s // Q_BLOCK

    def to_blocks(a):
        return jnp.swapaxes(a.reshape(a.shape[0], nb, Q_BLOCK, *a.shape[2:]), 0, 1)

    xs = tuple(to_blocks(a) for a in q_arrays) + (q_pos.reshape(nb, Q_BLOCK),)
    out = lax.map(lambda blk: attend(*blk), xs)
    out = jnp.swapaxes(out, 0, 1)
    return out.reshape(out.shape[0], s, *out.shape[3:])


def fox_attend(q, k, v, fq, fk, q_pos, k_pos):
    s = jnp.einsum('bqhd,bkhd->bhqk', q, k, preferred_element_type=jnp.float32) * (HEAD_DIM ** -0.5)
    s = s + (jnp.swapaxes(fq, 1, 2)[..., :, None] - jnp.swapaxes(fk, 1, 2)[..., None, :])
    s = jnp.where(k_pos[None, :] <= q_pos[:, None], s, NEG_INF)
    p = jax.nn.softmax(s, axis=-1)
    return jnp.einsum('bhqk,bkhd->bqhd', p.astype(v.dtype), v)


def band_attend_prompt(q, k, v, rel_bias):
    b, s, h, d = q.shape
    nc = s // CHUNK
    nband = BAND_CHUNKS + 1
    pad = ((0, 0), (WINDOW_B, 0), (0, 0), (0, 0))
    kc = jnp.pad(k, pad).reshape(b, nc + BAND_CHUNKS, CHUNK, h, d)
    vc = jnp.pad(v, pad).reshape(b, nc + BAND_CHUNKS, CHUNK, h, d)
    idx = jnp.arange(nc)[:, None] + jnp.arange(nband)[None, :]
    kb = kc[:, idx].reshape(b, nc, nband * CHUNK, h, d)
    vb = vc[:, idx].reshape(b, nc, nband * CHUNK, h, d)
    qc = q.reshape(b, nc, CHUNK, h, d)
    sc = jnp.einsum('bcqhd,bckhd->bchqk', qc, kb, preferred_element_type=jnp.float32) * (HEAD_DIM ** -0.5)
    rel = WINDOW_B + jnp.arange(CHUNK)[:, None] - jnp.arange(nband * CHUNK)[None, :]
    bias = rel_bias[:, jnp.clip(rel, -MAX_REL, MAX_REL) + MAX_REL].astype(jnp.float32)
    valid = jnp.repeat(idx >= BAND_CHUNKS, CHUNK, axis=1)
    sc = jnp.where(valid[None, :, None, None, :], sc + bias[None, None], NEG_INF)
    p = jax.nn.softmax(sc, axis=-1)
    o = jnp.einsum('bchqk,bckhd->bcqhd', p.astype(v.dtype), vb)
    return o.reshape(b, s, h, d)


def band_attend_sample(q, k_new, v_new, k_buf, v_buf, rel_bias):
    t = q.shape[1]
    lb = k_buf.shape[1]
    k = jnp.concatenate([k_buf, k_new], axis=1)
    v = jnp.concatenate([v_buf, v_new], axis=1)
    q_pos = jnp.arange(t)
    k_pos = jnp.concatenate([jnp.arange(-lb, 0), jnp.arange(t)])
    rel = q_pos[:, None] - k_pos[None, :]
    bias = rel_bias[:, jnp.clip(rel, -MAX_REL, MAX_REL) + MAX_REL].astype(jnp.float32)
    sc = jnp.einsum('bqhd,bkhd->bhqk', q, k, preferred_element_type=jnp.float32) * (HEAD_DIM ** -0.5)
    p = jax.nn.softmax(sc + bias[None], axis=-1)
    return jnp.einsum('bhqk,bkhd->bqhd', p.astype(v.dtype), v)


def diff_attend(q, k, v, q_pos, k_pos, lam, slopes):
    s = jnp.einsum('bqhid,bkhid->bihqk', q, k, preferred_element_type=jnp.float32) * (HEAD_DIM ** -0.5)
    dist = jnp.abs(q_pos[:, None] - k_pos[None, :]).astype(jnp.float32)
    s = s - slopes[:, None, None] * dist
    mask = (k_pos // CHUNK)[None, :] <= (q_pos // CHUNK)[:, None]
    s = jnp.where(mask, s, NEG_INF)
    p = jax.nn.softmax(s, axis=-1)
    a = p[:, 0] - lam * p[:, 1]
    return jnp.einsum('bhqk,bkhe->bqhe', a.astype(v.dtype), v)


def even_mixer(h, w_in, b_f, rel_bias, w_out, past):
    b, t, _ = h.shape
    proj = jnp.einsum('btd,de->bte', h, w_in)
    cuts = [W_A, 2 * W_A, 3 * W_A, 3 * W_A + H_A, 3 * W_A + H_A + W_B, 3 * W_A + H_A + 2 * W_B]
    qa, ka, va, fa, qb, kb, vb = jnp.split(proj, cuts, axis=-1)
    qa, ka, va = (a.reshape(b, t, H_A, HEAD_DIM) for a in (qa, ka, va))
    qb, kb, vb = (a.reshape(b, t, H_B, HEAD_DIM) for a in (qb, kb, vb))
    logf = jax.nn.log_sigmoid(fa.astype(jnp.float32) + b_f.astype(jnp.float32))
    if past is None:
        pos = jnp.arange(t)
        f_cum = jnp.cumsum(logf, axis=1)
        o_a = sweep_query_blocks(
            lambda q_blk, fq_blk, qp: fox_attend(q_blk, ka, va, fq_blk, f_cum, qp, pos), (qa, f_cum), pos)
        o_b = band_attend_prompt(qb, kb, vb, rel_bias)
        keep = min(WINDOW_B, t)
        new_bk, new_bv = kb[:, t - keep:], vb[:, t - keep:]
    else:
        ck, cv, clogf, bk, bv = past
        p_len = ck.shape[1]
        f_cum = jnp.cumsum(jnp.concatenate([clogf.astype(jnp.float32), logf], axis=1), axis=1)
        o_a = fox_attend(qa, jnp.concatenate([ck, ka], axis=1), jnp.concatenate([cv, va], axis=1),
                         f_cum[:, p_len:], f_cum, p_len + jnp.arange(t), jnp.arange(p_len + t))
        o_b = band_attend_sample(qb, kb, vb, bk, bv, rel_bias)
        new_bk = jnp.concatenate([bk, kb], axis=1)[:, t:]
        new_bv = jnp.concatenate([bv, vb], axis=1)[:, t:]
    o = jnp.concatenate([o_a.reshape(b, t, W_A), o_b.reshape(b, t, W_B)], axis=-1)
    y = jnp.einsum('bte,ed->btd', o, w_out)
    return y, (ka, va, logf.astype(h.dtype), new_bk, new_bv)


def odd_mixer(h, w_in, lq1, lk1, lq2, lk2, g_sub, w_out, lam_init, past):
    b, t, _ = h.shape
    proj = jnp.einsum('btd,de->bte', h, w_in)
    q, k, v = jnp.split(proj, 3, axis=-1)
    q = q.reshape(b, t, H_C, 2, HEAD_DIM)
    k = k.reshape(b, t, H_C, 2, HEAD_DIM)
    v = v.reshape(b, t, H_C, 2 * HEAD_DIM)
    f32 = jnp.float32
    lam = (jnp.exp(jnp.sum(lq1.astype(f32) * lk1.astype(f32)))
           - jnp.exp(jnp.sum(lq2.astype(f32) * lk2.astype(f32))) + lam_init)
    slopes = alibi_slopes(H_C)
    if past is None:
        pos = jnp.arange(t)
        o = sweep_query_blocks(lambda q_blk, qp: diff_attend(q_blk, k, v, qp, pos, lam, slopes), (q,), pos)
    else:
        ck, cv = past
        p_len = ck.shape[1]
        k_all = jnp.concatenate([ck.reshape(b, p_len, H_C, 2, HEAD_DIM), k], axis=1)
        v_all = jnp.concatenate([cv, v], axis=1)
        o = diff_attend(q, k_all, v_all, p_len + jnp.arange(t), jnp.arange(p_len + t), lam, slopes)
    o = rmsnorm(o, g_sub) * (1.0 - lam_init)
    y = jnp.einsum('bte,ed->btd', o.reshape(b, t, W_C), w_out)
    return y, (k.reshape(b, t, H_C, 2 * HEAD_DIM), v)


def swiglu(h, w1, w3, w2):
    a = jnp.einsum('btd,df->btf', h, w1)
    g = jnp.einsum('btd,df->btf', h, w3)
    return jnp.einsum('btf,fd->btd', jax.nn.silu(a) * g, w2)


def moe_swiglu(h, w_router, w1, w3, w2):
    b, t, d = h.shape
    x = h.reshape(b * t, d)
    n_tok = b * t
    logits = jnp.einsum('td,de->te', x, w_router, preferred_element_type=jnp.float32)
    probs = jax.nn.softmax(logits, axis=-1)
    top_p, top_e = lax.top_k(probs, TOP_K)
    top_p = top_p / jnp.sum(top_p, axis=-1, keepdims=True)
    n_slots = n_tok * TOP_K
    slot_e = top_e.reshape(n_slots).astype(jnp.int32)
    slot_tok = jnp.repeat(jnp.arange(n_tok, dtype=jnp.int32), TOP_K)
    slot_w = top_p.reshape(n_slots)
    order = jnp.argsort(slot_e)
    e_sorted, tok_sorted, w_sorted = slot_e[order], slot_tok[order], slot_w[order]
    counts = jnp.bincount(slot_e, length=N_EXPERTS).astype(jnp.int32)
    padded = ((counts + MOE_GROUP - 1) // MOE_GROUP) * MOE_GROUP
    start = jnp.cumsum(counts) - counts
    pad_end = jnp.cumsum(padded)
    pad_start = pad_end - padded
    dest = pad_start[e_sorted] + jnp.arange(n_slots, dtype=jnp.int32) - start[e_sorted]
    n_blocks = -(-n_slots // MOE_GROUP) + N_EXPERTS
    n_rows = n_blocks * MOE_GROUP
    row_tok = jnp.zeros((n_rows,), jnp.int32).at[dest].set(tok_sorted)
    row_w = jnp.zeros((n_rows,), jnp.float32).at[dest].set(w_sorted)
    block_start = jnp.arange(n_blocks, dtype=jnp.int32) * MOE_GROUP
    block_e = jnp.minimum(jnp.searchsorted(pad_end, block_start, side='right'), N_EXPERTS - 1)
    xb = x[row_tok].reshape(n_blocks, MOE_GROUP, d)

    def expert_block(args):
        xg, e = args
        return (jax.nn.silu(xg @ w1[e]) * (xg @ w3[e])) @ w2[e]

    yb = lax.map(expert_block, (xb, block_e)).reshape(n_rows, d)
    y = jnp.zeros_like(x).at[row_tok].add(yb * row_w[:, None].astype(yb.dtype))
    return y.reshape(b, t, d)


def setup_inputs(seed: int = 0) -> dict:
    key = jax.random.key(seed)
    keys = iter(jax.random.split(key, 40))

    def nrm(shape, scale):
        return jax.random.normal(next(keys), shape, jnp.float32) * scale

    lb = min(WINDOW_B, PAST_LEN)
    d = D_MODEL
    return {
        'x_prompt': nrm((BATCH, SEQ, d), 1.0),
        'x_sample': nrm((DEC_BATCH, DEC_SEQ, d), 1.0),
        'cache_a_k': nrm((N_EVEN, DEC_BATCH, PAST_LEN, H_A, HEAD_DIM), 1.0),
        'cache_a_v': nrm((N_EVEN, DEC_BATCH, PAST_LEN, H_A, HEAD_DIM), 1.0),
        'cache_a_logf': jax.nn.log_sigmoid(FORGET_BIAS + nrm((N_EVEN, DEC_BATCH, PAST_LEN, H_A), 1.0)),
        'cache_b_k': nrm((N_EVEN, DEC_BATCH, lb, H_B, HEAD_DIM), 1.0),
        'cache_b_v': nrm((N_EVEN, DEC_BATCH, lb, H_B, HEAD_DIM), 1.0),
        'cache_c_k': nrm((N_ODD, DEC_BATCH, PAST_LEN, H_C, 2 * HEAD_DIM), 1.0),
        'cache_c_v': nrm((N_ODD, DEC_BATCH, PAST_LEN, H_C, 2 * HEAD_DIM), 1.0),
        'c_prompt': nrm((BATCH, d), 1.0),
        'c_sample': nrm((DEC_BATCH, d), 1.0),
        'w_mod': nrm((DEPTH, d, 6 * d), 0.5 * d ** -0.5),
        'b_mod': nrm((DEPTH, 6 * d), 0.1),
        'g_mix': 1.0 + nrm((DEPTH, d), 0.05),
        'g_ffn': 1.0 + nrm((DEPTH, d), 0.05),
        'g_final': 1.0 + nrm((d,), 0.05),
        'w_in_even': nrm((N_EVEN, d, D_IN_EVEN), d ** -0.5),
        'b_forget': FORGET_BIAS + nrm((N_EVEN, H_A), 0.5),
        'rel_bias': nrm((N_EVEN, H_B, 2 * MAX_REL + 1), 0.5),
        'w_out_even': nrm((N_EVEN, W_A + W_B, d), (W_A + W_B) ** -0.5),
        'w_in_odd': nrm((N_ODD, d, D_IN_ODD), d ** -0.5),
        'lambda_q1': nrm((N_ODD, HEAD_DIM), 0.1),
        'lambda_k1': nrm((N_ODD, HEAD_DIM), 0.1),
        'lambda_q2': nrm((N_ODD, HEAD_DIM), 0.1),
        'lambda_k2': nrm((N_ODD, HEAD_DIM), 0.1),
        'g_subln': 1.0 + nrm((N_ODD, 2 * HEAD_DIM), 0.05),
        'w_out_odd': nrm((N_ODD, W_C, d), W_C ** -0.5),
        'w1_dense': nrm((N_EVEN, d, D_FF), d ** -0.5),
        'w3_dense': nrm((N_EVEN, d, D_FF), d ** -0.5),
        'w2_dense': nrm((N_EVEN, D_FF, d), D_FF ** -0.5),
        'w_router': nrm((N_ODD, d, N_EXPERTS), d ** -0.5),
        'w1_moe': nrm((N_ODD, N_EXPERTS, d, D_FF_EXPERT), d ** -0.5),
        'w3_moe': nrm((N_ODD, N_EXPERTS, d, D_FF_EXPERT), d ** -0.5),
        'w2_moe': nrm((N_ODD, N_EXPERTS, D_FF_EXPERT, d), D_FF_EXPERT ** -0.5),
    }


def reference(x_prompt, x_sample, cache_a_k, cache_a_v, cache_a_logf, cache_b_k, cache_b_v,
              cache_c_k, cache_c_v, c_prompt, c_sample,
              w_mod, b_mod, g_mix, g_ffn, g_final,
              w_in_even, b_forget, rel_bias, w_out_even,
              w_in_odd, lambda_q1, lambda_k1, lambda_q2, lambda_k2, g_subln, w_out_odd,
              w1_dense, w3_dense, w2_dense,
              w_router, w1_moe, w3_moe, w2_moe):

    def trunk(x, c, use_cache):
        even_states, odd_states = [], []
        for l in range(DEPTH):
            i = l // 2
            mod = jnp.einsum('bd,de->be', jax.nn.silu(c), w_mod[l]) + b_mod[l]
            sh_m, sc_m, gt_m, sh_f, sc_f, gt_f = jnp.split(mod, 6, axis=-1)
            h = modulate(x, g_mix[l], sh_m, sc_m)
            if l % 2 == 0:
                past = (cache_a_k[i], cache_a_v[i], cache_a_logf[i], cache_b_k[i], cache_b_v[i]) if use_cache else None
                o, st = even_mixer(h, w_in_even[i], b_forget[i], rel_bias[i], w_out_even[i], past)
                even_states.append(st)
            else:
                past = (cache_c_k[i], cache_c_v[i]) if use_cache else None
                lam_init = 0.8 - 0.6 * math.exp(-0.3 * l)
                o, st = odd_mixer(h, w_in_odd[i], lambda_q1[i], lambda_k1[i], lambda_q2[i], lambda_k2[i],
                                  g_subln[i], w_out_odd[i], lam_init, past)
                odd_states.append(st)
            x = x + gt_m[:, None, :] * o
            h = modulate(x, g_ffn[l], sh_f, sc_f)
            if l % 2 == 0:
                f = swiglu(h, w1_dense[i], w3_dense[i], w2_dense[i])
            else:
                f = moe_swiglu(h, w_router[i], w1_moe[i], w3_moe[i], w2_moe[i])
            x = x + gt_f[:, None, :] * f
        return rmsnorm(x, g_final), even_states, odd_states

    def stack(states, j):
        return jnp.stack([s[j] for s in states], axis=0)

    y_prompt, ev_p, od_p = trunk(x_prompt, c_prompt, False)
    y_sample, ev_s, od_s = trunk(x_sample, c_sample, True)

    new_a_k_prompt = stack(ev_p, 0)
    new_a_v_prompt = stack(ev_p, 1)
    new_a_logf_prompt = stack(ev_p, 2)
    new_b_k_prompt = stack(ev_p, 3)
    new_b_v_prompt = stack(ev_p, 4)
    new_c_k_prompt = stack(od_p, 0)
    new_c_v_prompt = stack(od_p, 1)
    new_a_k_sample = stack(ev_s, 0)
    new_a_v_sample = stack(ev_s, 1)
    new_a_logf_sample = stack(ev_s, 2)
    new_b_k_sample = stack(ev_s, 3)
    new_b_v_sample = stack(ev_s, 4)
    new_c_k_sample = stack(od_s, 0)
    new_c_v_sample = stack(od_s, 1)
    return (y_prompt, y_sample,
            new_a_k_prompt, new_a_v_prompt, new_a_logf_prompt, new_b_k_prompt, new_b_v_prompt,
            new_c_k_prompt, new_c_v_prompt,
            new_a_k_sample, new_a_v_sample, new_a_logf_sample, new_b_k_sample, new_b_v_sample,
            new_c_k_sample, new_c_v_sample)
```

```python
import functools
import math

import jax
import jax.numpy as jnp
from jax import lax
from jax.experimental import pallas as pl
from jax.experimental.pallas import tpu as pltpu

BF = jnp.bfloat16
F32 = jnp.float32

CHUNK = 64
CHUNK_SHIFT = 6
HEAD_DIM = 64
BAND_CHUNKS = 8
WINDOW_B = BAND_CHUNKS * CHUNK
MAX_REL = 128
TOP_K = 2
RMS_EPS = 1e-6
NEG_INF = -1e30
QK_SCALE = HEAD_DIM ** -0.5

LANES = 128
VMEM_LIMIT_BYTES = 56 * 1024 * 1024

ROW_TILE = 512
ATTN_TILE = 512
BAND_Q_TILE = WINDOW_B // 2
CACHE_TILE_A = 1024
CACHE_TILE_C = 512
MOE_ROW_TILE = 512
MOE_F_TILE = 1792
GATHER_ROWS = 512
COMBINE_TILE = 256


def _cparams(sem):
    return pltpu.CompilerParams(dimension_semantics=sem, vmem_limit_bytes=VMEM_LIMIT_BYTES)


def _dot(a, b):
    return jnp.dot(a, b, preferred_element_type=F32)


def _dot_nt(a, b):
    return lax.dot_general(a, b, (((1,), (1,)), ((), ())), preferred_element_type=F32)


def _sigmoid(x):
    return 1.0 / (1.0 + jnp.exp(-x))


def _modulate(x, g, shift, scale):
    y = x * lax.rsqrt(jnp.mean(x * x, axis=-1, keepdims=True) + RMS_EPS)
    return (y * g) * (1.0 + scale) + shift


def _split3(x):
    hi = x.astype(BF)
    r1 = x - hi.astype(F32)
    mid = r1.astype(BF)
    lo = (r1 - mid.astype(F32)).astype(BF)
    return hi, mid, lo


def _cumsum_lanes(x, upper):
    hi, mid, lo = _split3(x)
    return _dot(hi, upper) + _dot(mid, upper) + _dot(lo, upper)


def _upper_tri(t):
    r = jnp.arange(t)
    return (r[:, None] <= r[None, :]).astype(BF)


def _osm_update(s, v, m_ref, l_ref, acc_ref, idx):
    m_prev = m_ref[idx]
    m_new = jnp.maximum(m_prev, jnp.max(s, axis=-1, keepdims=True))
    alpha = jnp.exp(m_prev - m_new)
    p = jnp.exp(s - m_new)
    l_ref[idx] = alpha * l_ref[idx] + jnp.sum(p, axis=-1, keepdims=True)
    acc_ref[idx] = alpha * acc_ref[idx] + _dot(p.astype(BF), v)
    m_ref[idx] = m_new


def _osm_init(m_ref, l_ref, acc_ref):
    m_ref[...] = jnp.full(m_ref.shape, NEG_INF, F32)
    l_ref[...] = jnp.zeros(l_ref.shape, F32)
    acc_ref[...] = jnp.zeros(acc_ref.shape, F32)


def _mod_body(c_ref, w_ref, b_ref, o_ref):
    c = c_ref[...]
    s = (c * _sigmoid(c)).astype(BF)
    o_ref[0] = _dot(s, w_ref[0].astype(BF)) + b_ref[0]


def _mod_call(c_all, w_mod, b_mod):
    depth, d, n = w_mod.shape
    r = c_all.shape[0]
    tn = 1536 if n % 1536 == 0 else n
    return pl.pallas_call(
        _mod_body,
        grid=(depth, n // tn),
        in_specs=[pl.BlockSpec((r, d), lambda l, j: (0, 0)),
                  pl.BlockSpec((1, d, tn), lambda l, j: (l, 0, j)),
                  pl.BlockSpec((1, 1, tn), lambda l, j: (l, 0, j))],
        out_specs=pl.BlockSpec((1, r, tn), lambda l, j: (l, 0, j)),
        out_shape=jax.ShapeDtypeStruct((depth, r, n), F32),
        compiler_params=_cparams(("arbitrary", "arbitrary")),
        name="mod",
    )(c_all, w_mod, b_mod.reshape(depth, 1, n))


def _mod_spec(mod_arr, tm):
    d = mod_arr.shape[-1]
    if mod_arr.shape[1] == 1:
        return pl.BlockSpec((1, 1, d), lambda b, i: (b, 0, 0))
    return pl.BlockSpec((1, tm, d), lambda b, i: (b, i, 0))


def _const_spec(arr):
    nd = arr.ndim
    return pl.BlockSpec(arr.shape, lambda b, i: (0,) * nd, pipeline_mode=pl.Buffered(1))


def _inproj_even_body(*refs, wa, wb, cum):
    if cum:
        (x_ref, g_ref, sh_ref, sc_ref, w_ref, wf_ref, bf_ref, up_ref,
         qkv_ref, ka_ref, va_ref, kb_ref, vb_ref, logf_ref, lft_ref, ft_ref, carry_ref) = refs
    else:
        (x_ref, g_ref, sh_ref, sc_ref, w_ref, wf_ref, bf_ref,
         qkv_ref, ka_ref, va_ref, kb_ref, vb_ref, logf_ref, lft_ref) = refs
    i = pl.program_id(1)
    h = _modulate(x_ref[0], g_ref[...], sh_ref[0], sc_ref[0]).astype(BF)
    nh = logf_ref.shape[-1]

    qa = _dot(h, w_ref[:, 0:wa])
    ka = _dot(h, w_ref[:, wa:2 * wa])
    va = _dot(h, w_ref[:, 2 * wa:3 * wa])
    o = 3 * wa
    qb = _dot(h, w_ref[:, o:o + wb])
    kb = _dot(h, w_ref[:, o + wb:o + 2 * wb])
    vb = _dot(h, w_ref[:, o + 2 * wb:o + 3 * wb])
    ka_ref[0] = ka
    va_ref[0] = va
    kb_ref[0] = kb
    vb_ref[0] = vb
    qkv_ref[0] = jnp.concatenate(
        [(qa * QK_SCALE).astype(BF), ka.astype(BF), va.astype(BF),
         (qb * QK_SCALE).astype(BF), kb.astype(BF), vb.astype(BF)], axis=-1)

    z = _dot(h, wf_ref[...]) + bf_ref[...]
    logf = jnp.minimum(z, 0.0) - jnp.log1p(jnp.exp(-jnp.abs(z)))
    logf_ref[0] = logf[:, :nh]
    lft = logf.T[:nh, :]
    lft_ref[0] = lft
    if cum:
        @pl.when(i == 0)
        def _():
            carry_ref[...] = jnp.zeros(carry_ref.shape, F32)
        ft = _cumsum_lanes(lft, up_ref[...]) + carry_ref[:, 0:1]
        ft_ref[0] = ft
        carry_ref[...] = jnp.broadcast_to(ft[:, -1:], carry_ref.shape)


def _inproj_even_call(x, g, shift, scale, w_main, w_f, b_f, keep, cum):
    bx, sx, d = x.shape
    tm = min(ROW_TILE, sx)
    nt = sx // tm
    n = w_main.shape[1]
    wa = wb = n // 6
    nh = wa // HEAD_DIM
    nkeep = keep // tm
    tail = lambda b, i: (b, jnp.maximum(i - (nt - nkeep), 0), 0)
    row = lambda b, i: (b, i, 0)
    in_specs = [pl.BlockSpec((1, tm, d), row), _const_spec(g), _mod_spec(shift, tm), _mod_spec(scale, tm),
                _const_spec(w_main), _const_spec(w_f), _const_spec(b_f)]
    args = [x, g, shift, scale, w_main, w_f, b_f]
    out_specs = [pl.BlockSpec((1, tm, n), row),
                 pl.BlockSpec((1, tm, wa), row), pl.BlockSpec((1, tm, wa), row),
                 pl.BlockSpec((1, tm, wb), tail), pl.BlockSpec((1, tm, wb), tail),
                 pl.BlockSpec((1, tm, nh), row),
                 pl.BlockSpec((1, nh, tm), lambda b, i: (b, 0, i))]
    out_shape = [jax.ShapeDtypeStruct((bx, sx, n), BF),
                 jax.ShapeDtypeStruct((bx, sx, wa), F32), jax.ShapeDtypeStruct((bx, sx, wa), F32),
                 jax.ShapeDtypeStruct((bx, keep, wb), F32), jax.ShapeDtypeStruct((bx, keep, wb), F32),
                 jax.ShapeDtypeStruct((bx, sx, nh), F32),
                 jax.ShapeDtypeStruct((bx, nh, sx), F32)]
    scratch = []
    if cum:
        up = _upper_tri(tm)
        in_specs.append(_const_spec(up))
        args.append(up)
        out_specs.append(pl.BlockSpec((1, nh, tm), lambda b, i: (b, 0, i)))
        out_shape.append(jax.ShapeDtypeStruct((bx, nh, sx), F32))
        scratch.append(pltpu.VMEM((nh, LANES), F32))
    return pl.pallas_call(
        functools.partial(_inproj_even_body, wa=wa, wb=wb, cum=cum),
        grid=(bx, nt), in_specs=in_specs, out_specs=out_specs, out_shape=out_shape,
        scratch_shapes=scratch,
        compiler_params=_cparams(("arbitrary", "arbitrary")),
        name="inproj_even",
    )(*args)


def _inproj_odd_body(x_ref, g_ref, sh_ref, sc_ref, w_ref, qkv_ref, k_ref, v_ref, *, wc):
    h = _modulate(x_ref[0], g_ref[...], sh_ref[0], sc_ref[0]).astype(BF)
    q = _dot(h, w_ref[:, 0:wc])
    k = _dot(h, w_ref[:, wc:2 * wc])
    v = _dot(h, w_ref[:, 2 * wc:3 * wc])
    k_ref[0] = k
    v_ref[0] = v
    qkv_ref[0] = jnp.concatenate([(q * QK_SCALE).astype(BF), k.astype(BF), v.astype(BF)], axis=-1)


def _inproj_odd_call(x, g, shift, scale, w):
    bx, sx, d = x.shape
    tm = min(ROW_TILE, sx)
    n = w.shape[1]
    wc = n // 3
    row = lambda b, i: (b, i, 0)
    return pl.pallas_call(
        functools.partial(_inproj_odd_body, wc=wc),
        grid=(bx, sx // tm),
        in_specs=[pl.BlockSpec((1, tm, d), row), _const_spec(g), _mod_spec(shift, tm), _mod_spec(scale, tm),
                  _const_spec(w)],
        out_specs=[pl.BlockSpec((1, tm, n), row), pl.BlockSpec((1, tm, wc), row), pl.BlockSpec((1, tm, wc), row)],
        out_shape=[jax.ShapeDtypeStruct((bx, sx, n), BF),
                   jax.ShapeDtypeStruct((bx, sx, wc), F32), jax.ShapeDtypeStruct((bx, sx, wc), F32)],
        compiler_params=_cparams(("arbitrary", "arbitrary")),
        name="inproj_odd",
    )(x, g, shift, scale, w)


def _fox_prompt_body(q_ref, k_ref, v_ref, ft_ref, o_ref, m_ref, l_ref, acc_ref, *, nh):
    i = pl.program_id(1)
    j = pl.program_id(2)
    hd = HEAD_DIM

    @pl.when(j == 0)
    def _():
        _osm_init(m_ref, l_ref, acc_ref)

    def step(masked):
        q = q_ref[0]
        k = k_ref[0]
        v = v_ref[0]
        ft = ft_ref[0]
        tq, tk = q.shape[0], k.shape[0]
        if masked:
            visible = lax.broadcasted_iota(jnp.int32, (tq, tk), 1) <= lax.broadcasted_iota(jnp.int32, (tq, tk), 0)
        for h in range(nh):
            sl = slice(h * hd, (h + 1) * hd)
            s = _dot_nt(q[:, sl], k[:, sl]) - ft[h:h + 1, :]
            if masked:
                s = jnp.where(visible, s, NEG_INF)
            _osm_update(s, v[:, sl], m_ref, l_ref, acc_ref, h)

    @pl.when(j < i)
    def _():
        step(False)

    @pl.when(j == i)
    def _():
        step(True)

    @pl.when(j == pl.num_programs(2) - 1)
    def _():
        o_ref[0] = jnp.concatenate([(acc_ref[h] / l_ref[h]).astype(BF) for h in range(nh)], axis=-1)


def _fox_prompt_call(qkv, ft, wa):
    b, s, _ = qkv.shape
    nh = wa // HEAD_DIM
    t = min(ATTN_TILE, s)
    nq = s // t
    return pl.pallas_call(
        functools.partial(_fox_prompt_body, nh=nh),
        grid=(b, nq, nq),
        in_specs=[pl.BlockSpec((1, t, wa), lambda b, i, j: (b, i, 0)),
                  pl.BlockSpec((1, t, wa), lambda b, i, j: (b, jnp.minimum(j, i), 1)),
                  pl.BlockSpec((1, t, wa), lambda b, i, j: (b, jnp.minimum(j, i), 2)),
                  pl.BlockSpec((1, nh, t), lambda b, i, j: (b, 0, jnp.minimum(j, i)))],
        out_specs=pl.BlockSpec((1, t, wa), lambda b, i, j: (b, i, 0)),
        out_shape=jax.ShapeDtypeStruct((b, s, wa), BF),
        scratch_shapes=[pltpu.VMEM((nh, t, 1), F32), pltpu.VMEM((nh, t, 1), F32),
                        pltpu.VMEM((nh, t, HEAD_DIM), F32)],
        compiler_params=_cparams(("arbitrary", "arbitrary", "arbitrary")),
        name="fox_prompt",
    )(qkv, qkv, qkv, ft)


def _band_prompt_body(q_ref, k0_ref, k1_ref, k2_ref, v0_ref, v1_ref, v2_ref, bias_ref, o_ref, *, nh):
    i = pl.program_id(1)
    hd = HEAD_DIM
    q = q_ref[0]
    tq = q.shape[0]
    k = jnp.concatenate([k0_ref[0], k1_ref[0], k2_ref[0]], axis=0)
    v = jnp.concatenate([v0_ref[0], v1_ref[0], v2_ref[0]], axis=0)
    in_seq = lax.broadcasted_iota(jnp.int32, (tq, 3 * tq), 1) >= (2 - i) * tq
    outs = []
    for h in range(nh):
        sl = slice(h * hd, (h + 1) * hd)
        s = _dot_nt(q[:, sl], k[:, sl]) + bias_ref[h]
        s = jnp.where(in_seq, s, NEG_INF)
        m = jnp.max(s, axis=-1, keepdims=True)
        p = jnp.exp(s - m)
        l = jnp.sum(p, axis=-1, keepdims=True)
        outs.append((_dot(p.astype(BF), v[:, sl]) / l).astype(BF))
    o_ref[0] = jnp.concatenate(outs, axis=-1)


def _band_bias_prompt(rel_bias, tq):
    r = jnp.arange(tq)[:, None]
    c = jnp.arange(3 * tq)[None, :]
    rel = 2 * tq + r - c
    bias = rel_bias[:, jnp.clip(rel, -MAX_REL, MAX_REL) + MAX_REL].astype(F32)
    valid = (c // CHUNK >= r // CHUNK) & (c // CHUNK <= r // CHUNK + BAND_CHUNKS)
    return jnp.where(valid[None], bias, NEG_INF)


def _band_prompt_call(qkv, rel_bias, wa, wb):
    b, s, _ = qkv.shape
    nh = wb // HEAD_DIM
    tq = BAND_Q_TILE
    assert s % tq == 0 and wa == wb
    bias = _band_bias_prompt(rel_bias, tq)
    cq, ck, cv = 3, 4, 5

    def kv_spec(col, back):
        return pl.BlockSpec((1, tq, wb), lambda b, i: (b, jnp.maximum(i - back, 0), col))

    return pl.pallas_call(
        functools.partial(_band_prompt_body, nh=nh),
        grid=(b, s // tq),
        in_specs=[pl.BlockSpec((1, tq, wb), lambda b, i: (b, i, cq)),
                  kv_spec(ck, 2), kv_spec(ck, 1), kv_spec(ck, 0),
                  kv_spec(cv, 2), kv_spec(cv, 1), kv_spec(cv, 0),
                  _const_spec(bias)],
        out_specs=pl.BlockSpec((1, tq, wb), lambda b, i: (b, i, 0)),
        out_shape=jax.ShapeDtypeStruct((b, s, wb), BF),
        compiler_params=_cparams(("arbitrary", "arbitrary")),
        name="band_prompt",
    )(qkv, qkv, qkv, qkv, qkv, qkv, qkv, bias)


def _lambda(lam_ref, lam_init):
    lp = lam_ref[...]
    a = jnp.sum(lp[0:1] * lp[1:2], axis=-1, keepdims=True)
    b = jnp.sum(lp[2:3] * lp[3:4], axis=-1, keepdims=True)
    return jnp.exp(a) - jnp.exp(b) + lam_init


def _diff_finalize(l_ref, acc_ref, lam_ref, gsub_ref, nh, lam_init):
    lam = _lambda(lam_ref, lam_init)
    g = gsub_ref[...]
    outs = []
    for h in range(nh):
        o = acc_ref[2 * h] / l_ref[2 * h] - lam * (acc_ref[2 * h + 1] / l_ref[2 * h + 1])
        o = o * lax.rsqrt(jnp.mean(o * o, axis=-1, keepdims=True) + RMS_EPS)
        outs.append(((o * g) * (1.0 - lam_init)).astype(BF))
    return jnp.concatenate(outs, axis=-1)


def _diff_prompt_body(q_ref, k_ref, v_ref, lam_ref, gsub_ref, o_ref, m_ref, l_ref, acc_ref, *, nh, lam_init):
    i = pl.program_id(1)
    j = pl.program_id(2)
    hd = HEAD_DIM

    @pl.when(j == 0)
    def _():
        _osm_init(m_ref, l_ref, acc_ref)

    def step(masked):
        q = q_ref[0]
        k = k_ref[0]
        v = v_ref[0]
        tq, tk = q.shape[0], k.shape[0]
        row = lax.broadcasted_iota(jnp.int32, (tq, tk), 0)
        col = lax.broadcasted_iota(jnp.int32, (tq, tk), 1)
        dist = jnp.abs((i - j) * tq + row - col).astype(F32)
        if masked:
            visible = (col >> CHUNK_SHIFT) <= (row >> CHUNK_SHIFT)
        for h in range(nh):
            slope = 2.0 ** (-8.0 * (h + 1) / nh)
            bias = slope * dist
            vh = v[:, 2 * h * hd:(2 * h + 2) * hd]
            for t in range(2):
                sl = slice((2 * h + t) * hd, (2 * h + t + 1) * hd)
                s = _dot_nt(q[:, sl], k[:, sl]) - bias
                if masked:
                    s = jnp.where(visible, s, NEG_INF)
                _osm_update(s, vh, m_ref, l_ref, acc_ref, 2 * h + t)

    @pl.when(j < i)
    def _():
        step(False)

    @pl.when(j == i)
    def _():
        step(True)

    @pl.when(j == pl.num_programs(2) - 1)
    def _():
        o_ref[0] = _diff_finalize(l_ref, acc_ref, lam_ref, gsub_ref, nh, lam_init)


def _diff_prompt_call(qkv, lam_par, g_sub, wc, lam_init):
    b, s, _ = qkv.shape
    nh = wc // (2 * HEAD_DIM)
    t = min(ATTN_TILE, s)
    nq = s // t
    return pl.pallas_call(
        functools.partial(_diff_prompt_body, nh=nh, lam_init=lam_init),
        grid=(b, nq, nq),
        in_specs=[pl.BlockSpec((1, t, wc), lambda b, i, j: (b, i, 0)),
                  pl.BlockSpec((1, t, wc), lambda b, i, j: (b, jnp.minimum(j, i), 1)),
                  pl.BlockSpec((1, t, wc), lambda b, i, j: (b, jnp.minimum(j, i), 2)),
                  pl.BlockSpec(lam_par.shape, lambda b, i, j: (0, 0)),
                  pl.BlockSpec(g_sub.shape, lambda b, i, j: (0, 0))],
        out_specs=pl.BlockSpec((1, t, wc), lambda b, i, j: (b, i, 0)),
        out_shape=jax.ShapeDtypeStruct((b, s, wc), BF),
        scratch_shapes=[pltpu.VMEM((2 * nh, t, 1), F32), pltpu.VMEM((2 * nh, t, 1), F32),
                        pltpu.VMEM((2 * nh, t, 2 * HEAD_DIM), F32)],
        compiler_params=_cparams(("arbitrary", "arbitrary", "arbitrary")),
        name="diff_prompt",
    )(qkv, qkv, qkv, lam_par, g_sub)


def _cumsum_body(x_ref, up_ref, o_ref, carry_ref):
    @pl.when(pl.program_id(1) == 0)
    def _():
        carry_ref[...] = jnp.zeros(carry_ref.shape, F32)
    ft = _cumsum_lanes(x_ref[0], up_ref[...]) + carry_ref[:, 0:1]
    o_ref[0] = ft
    carry_ref[...] = jnp.broadcast_to(ft[:, -1:], carry_ref.shape)


def _cumsum_call(x):
    b, nh, length = x.shape
    t = 512
    up = _upper_tri(t)
    return pl.pallas_call(
        _cumsum_body,
        grid=(b, length // t),
        in_specs=[pl.BlockSpec((1, nh, t), lambda b, j: (b, 0, j)), pl.BlockSpec((t, t), lambda b, j: (0, 0))],
        out_specs=pl.BlockSpec((1, nh, t), lambda b, j: (b, 0, j)),
        out_shape=jax.ShapeDtypeStruct(x.shape, F32),
        scratch_shapes=[pltpu.VMEM((nh, LANES), F32)],
        compiler_params=_cparams(("arbitrary", "arbitrary")),
        name="logf_cumsum",
    )(x, up)


def _fox_sample_body(q_ref, kn_ref, vn_ref, ck_ref, cv_ref, ftc_ref, ftn_ref, o_ref, m_ref, l_ref, acc_ref, *, nh):
    j = pl.program_id(1)
    hd = HEAD_DIM
    q = q_ref[0]
    t = q.shape[0]

    @pl.when(j == 0)
    def _():
        _osm_init(m_ref, l_ref, acc_ref)

    k = ck_ref[0].astype(BF)
    v = cv_ref[0].astype(BF)
    ft = ftc_ref[0]
    for h in range(nh):
        sl = slice(h * hd, (h + 1) * hd)
        s = _dot_nt(q[:, sl], k[:, sl]) - ft[h:h + 1, :]
        _osm_update(s, v[:, sl], m_ref, l_ref, acc_ref, h)

    @pl.when(j == pl.num_programs(1) - 1)
    def _():
        kn = kn_ref[0]
        vn = vn_ref[0]
        ftn = ftn_ref[0][:, :t]
        visible = lax.broadcasted_iota(jnp.int32, (t, t), 1) <= lax.broadcasted_iota(jnp.int32, (t, t), 0)
        outs = []
        for h in range(nh):
            sl = slice(h * hd, (h + 1) * hd)
            s = _dot_nt(q[:, sl], kn[:, sl]) - ftn[h:h + 1, :]
            s = jnp.where(visible, s, NEG_INF)
            _osm_update(s, vn[:, sl], m_ref, l_ref, acc_ref, h)
            outs.append((acc_ref[h] / l_ref[h]).astype(BF))
        o_ref[0] = jnp.concatenate(outs, axis=-1)


def _fox_sample_call(qkv, cache_k, cache_v, ft_pad, wa):
    b, t, _ = qkv.shape
    p_len = cache_k.shape[1]
    nh = wa // HEAD_DIM
    tk = min(CACHE_TILE_A, p_len)
    assert p_len % tk == 0 and p_len % LANES == 0 and t <= LANES
    return pl.pallas_call(
        functools.partial(_fox_sample_body, nh=nh),
        grid=(b, p_len // tk),
        in_specs=[pl.BlockSpec((1, t, wa), lambda b, j: (b, 0, 0)),
                  pl.BlockSpec((1, t, wa), lambda b, j: (b, 0, 1)),
                  pl.BlockSpec((1, t, wa), lambda b, j: (b, 0, 2)),
                  pl.BlockSpec((1, tk, wa), lambda b, j: (b, j, 0)),
                  pl.BlockSpec((1, tk, wa), lambda b, j: (b, j, 0)),
                  pl.BlockSpec((1, nh, tk), lambda b, j: (b, 0, j)),
                  pl.BlockSpec((1, nh, LANES), lambda b, j: (b, 0, p_len // LANES))],
        out_specs=pl.BlockSpec((1, t, wa), lambda b, j: (b, 0, 0)),
        out_shape=jax.ShapeDtypeStruct((b, t, wa), BF),
        scratch_shapes=[pltpu.VMEM((nh, t, 1), F32), pltpu.VMEM((nh, t, 1), F32),
                        pltpu.VMEM((nh, t, HEAD_DIM), F32)],
        compiler_params=_cparams(("arbitrary", "arbitrary")),
        name="fox_sample",
    )(qkv, qkv, qkv, cache_k, cache_v, ft_pad, ft_pad)


def _band_sample_body(q_ref, kn_ref, vn_ref, bk_ref, bv_ref, bias_b_ref, bias_n_ref, o_ref, *, nh):
    hd = HEAD_DIM
    q = q_ref[0]
    kn = kn_ref[0]
    vn = vn_ref[0]
    kb = bk_ref[0].astype(BF)
    vb = bv_ref[0].astype(BF)
    outs = []
    for h in range(nh):
        sl = slice(h * hd, (h + 1) * hd)
        sb = _dot_nt(q[:, sl], kb[:, sl]) + bias_b_ref[h]
        sn = _dot_nt(q[:, sl], kn[:, sl]) + bias_n_ref[h]
        m = jnp.maximum(jnp.max(sb, axis=-1, keepdims=True), jnp.max(sn, axis=-1, keepdims=True))
        pb = jnp.exp(sb - m)
        pn = jnp.exp(sn - m)
        l = jnp.sum(pb, axis=-1, keepdims=True) + jnp.sum(pn, axis=-1, keepdims=True)
        o = _dot(pb.astype(BF), vb[:, sl]) + _dot(pn.astype(BF), vn[:, sl])
        outs.append((o / l).astype(BF))
    o_ref[0] = jnp.concatenate(outs, axis=-1)


def _band_sample_call(qkv, buf_k, buf_v, rel_bias, wb):
    b, t, _ = qkv.shape
    lb = buf_k.shape[1]
    nh = wb // HEAD_DIM
    k_pos = jnp.concatenate([jnp.arange(-lb, 0), jnp.arange(t)])
    rel = jnp.arange(t)[:, None] - k_pos[None, :]
    bias = rel_bias[:, jnp.clip(rel, -MAX_REL, MAX_REL) + MAX_REL].astype(F32)
    bias_b, bias_n = bias[:, :, :lb], bias[:, :, lb:]
    return pl.pallas_call(
        functools.partial(_band_sample_body, nh=nh),
        grid=(b,),
        in_specs=[pl.BlockSpec((1, t, wb), lambda b: (b, 0, 3)),
                  pl.BlockSpec((1, t, wb), lambda b: (b, 0, 4)),
                  pl.BlockSpec((1, t, wb), lambda b: (b, 0, 5)),
                  pl.BlockSpec((1, lb, wb), lambda b: (b, 0, 0)),
                  pl.BlockSpec((1, lb, wb), lambda b: (b, 0, 0)),
                  pl.BlockSpec(bias_b.shape, lambda b: (0, 0, 0)),
                  pl.BlockSpec(bias_n.shape, lambda b: (0, 0, 0))],
        out_specs=pl.BlockSpec((1, t, wb), lambda b: (b, 0, 0)),
        out_shape=jax.ShapeDtypeStruct((b, t, wb), BF),
        compiler_params=_cparams(("arbitrary",)),
        name="band_sample",
    )(qkv, qkv, qkv, buf_k, buf_v, bias_b, bias_n)


def _diff_sample_body(q_ref, kn_ref, vn_ref, ck_ref, cv_ref, lam_ref, gsub_ref, o_ref, m_ref, l_ref, acc_ref,
                      *, nh, lam_init, p_len):
    j = pl.program_id(1)
    hd = HEAD_DIM
    q = q_ref[0]
    t = q.shape[0]

    @pl.when(j == 0)
    def _():
        _osm_init(m_ref, l_ref, acc_ref)

    def attend(k, v, dist):
        for h in range(nh):
            slope = 2.0 ** (-8.0 * (h + 1) / nh)
            bias = slope * dist
            vh = v[:, 2 * h * hd:(2 * h + 2) * hd]
            for u in range(2):
                sl = slice((2 * h + u) * hd, (2 * h + u + 1) * hd)
                s = _dot_nt(q[:, sl], k[:, sl]) - bias
                _osm_update(s, vh, m_ref, l_ref, acc_ref, 2 * h + u)

    tk = ck_ref.shape[1]
    row = lax.broadcasted_iota(jnp.int32, (t, tk), 0)
    col = lax.broadcasted_iota(jnp.int32, (t, tk), 1)
    attend(ck_ref[0].astype(BF), cv_ref[0].astype(BF), (p_len + row - j * tk - col).astype(F32))

    @pl.when(j == pl.num_programs(1) - 1)
    def _():
        rown = lax.broadcasted_iota(jnp.int32, (t, t), 0)
        coln = lax.broadcasted_iota(jnp.int32, (t, t), 1)
        attend(kn_ref[0], vn_ref[0], jnp.abs(rown - coln).astype(F32))
        o_ref[0] = _diff_finalize(l_ref, acc_ref, lam_ref, gsub_ref, nh, lam_init)


def _diff_sample_call(qkv, cache_k, cache_v, lam_par, g_sub, wc, lam_init):
    b, t, _ = qkv.shape
    p_len = cache_k.shape[1]
    assert (p_len // CHUNK) * CHUNK == p_len and t <= CHUNK
    nh = wc // (2 * HEAD_DIM)
    tk = min(CACHE_TILE_C, p_len)
    return pl.pallas_call(
        functools.partial(_diff_sample_body, nh=nh, lam_init=lam_init, p_len=p_len),
        grid=(b, p_len // tk),
        in_specs=[pl.BlockSpec((1, t, wc), lambda b, j: (b, 0, 0)),
                  pl.BlockSpec((1, t, wc), lambda b, j: (b, 0, 1)),
                  pl.BlockSpec((1, t, wc), lambda b, j: (b, 0, 2)),
                  pl.BlockSpec((1, tk, wc), lambda b, j: (b, j, 0)),
                  pl.BlockSpec((1, tk, wc), lambda b, j: (b, j, 0)),
                  pl.BlockSpec(lam_par.shape, lambda b, j: (0, 0)),
                  pl.BlockSpec(g_sub.shape, lambda b, j: (0, 0))],
        out_specs=pl.BlockSpec((1, t, wc), lambda b, j: (b, 0, 0)),
        out_shape=jax.ShapeDtypeStruct((b, t, wc), BF),
        scratch_shapes=[pltpu.VMEM((2 * nh, t, 1), F32), pltpu.VMEM((2 * nh, t, 1), F32),
                        pltpu.VMEM((2 * nh, t, 2 * HEAD_DIM), F32)],
        compiler_params=_cparams(("arbitrary", "arbitrary")),
        name="diff_sample",
    )(qkv, qkv, qkv, cache_k, cache_v, lam_par, g_sub)


def _out_ffn_body(x_ref, o_ref, gm_ref, g_ref, sh_ref, sc_ref, gf_ref, wo_ref, w1_ref, w3_ref, w2_ref, y_ref,
                  *, f_chunk):
    x1 = x_ref[0] + gm_ref[0] * _dot(o_ref[0], wo_ref[...])
    h = _modulate(x1, g_ref[...], sh_ref[0], sc_ref[0]).astype(BF)
    f_total = w1_ref.shape[1]
    f = None
    for c0 in range(0, f_total, f_chunk):
        a = _dot(h, w1_ref[:, c0:c0 + f_chunk])
        g = _dot(h, w3_ref[:, c0:c0 + f_chunk])
        u = ((a * _sigmoid(a)) * g).astype(BF)
        part = _dot(u, w2_ref[c0:c0 + f_chunk, :])
        f = part if f is None else f + part
    y_ref[0] = x1 + gf_ref[0] * f


def _out_ffn_call(x, o, gate_m, g, shift, scale, gate_f, wo, w1, w3, w2):
    bx, sx, d = x.shape
    tm = min(ROW_TILE, sx)
    f_total = w1.shape[1]
    f_chunk = f_total // 2 if (f_total // 2) % LANES == 0 else f_total
    row = lambda b, i: (b, i, 0)
    return pl.pallas_call(
        functools.partial(_out_ffn_body, f_chunk=f_chunk),
        grid=(bx, sx // tm),
        in_specs=[pl.BlockSpec((1, tm, d), row), pl.BlockSpec((1, tm, o.shape[-1]), row),
                  _mod_spec(gate_m, tm), _const_spec(g), _mod_spec(shift, tm), _mod_spec(scale, tm),
                  _mod_spec(gate_f, tm), _const_spec(wo), _const_spec(w1), _const_spec(w3), _const_spec(w2)],
        out_specs=pl.BlockSpec((1, tm, d), row),
        out_shape=jax.ShapeDtypeStruct((bx, sx, d), F32),
        compiler_params=_cparams(("arbitrary", "arbitrary")),
        name="out_ffn",
    )(x, o, gate_m, g, shift, scale, gate_f, wo, w1, w3, w2)


def _out_router_body(x_ref, o_ref, gm_ref, g_ref, sh_ref, sc_ref, wo_ref, wr_ref, x1_ref, h_ref, route_ref,
                     *, n_exp):
    x1 = x_ref[0] + gm_ref[0] * _dot(o_ref[0], wo_ref[...])
    x1_ref[0] = x1
    h = _modulate(x1, g_ref[...], sh_ref[0], sc_ref[0])
    hb = h.astype(BF)
    h_ref[0] = hb.astype(F32)
    logits = _dot(hb, wr_ref[0]) + _dot(hb, wr_ref[1]) + _dot(hb, wr_ref[2])
    col = lax.broadcasted_iota(jnp.int32, logits.shape, 1)
    lg = jnp.where(col < n_exp, logits, NEG_INF)
    e = jnp.exp(lg - jnp.max(lg, axis=-1, keepdims=True))
    probs = e / jnp.sum(e, axis=-1, keepdims=True)
    p1 = jnp.max(probs, axis=-1, keepdims=True)
    i1 = jnp.min(jnp.where(probs == p1, col, LANES), axis=-1, keepdims=True)
    rest = jnp.where(col == i1, -1.0, probs)
    p2 = jnp.max(rest, axis=-1, keepdims=True)
    i2 = jnp.min(jnp.where(rest == p2, col, LANES), axis=-1, keepdims=True)
    tot = p1 + p2
    route = jnp.where(col == 0, i1.astype(F32),
                      jnp.where(col == 1, i2.astype(F32),
                                jnp.where(col == 2, p1 / tot, jnp.where(col == 3, p2 / tot, 0.0))))
    route_ref[0] = route


def _out_router_call(x, o, gate_m, g, shift, scale, wo, wr3, n_exp):
    bx, sx, d = x.shape
    tm = min(ROW_TILE, sx)
    row = lambda b, i: (b, i, 0)
    return pl.pallas_call(
        functools.partial(_out_router_body, n_exp=n_exp),
        grid=(bx, sx // tm),
        in_specs=[pl.BlockSpec((1, tm, d), row), pl.BlockSpec((1, tm, o.shape[-1]), row),
                  _mod_spec(gate_m, tm), _const_spec(g), _mod_spec(shift, tm), _mod_spec(scale, tm),
                  _const_spec(wo), _const_spec(wr3)],
        out_specs=[pl.BlockSpec((1, tm, d), row), pl.BlockSpec((1, tm, d), row),
                   pl.BlockSpec((1, tm, LANES), row)],
        out_shape=[jax.ShapeDtypeStruct((bx, sx, d), F32), jax.ShapeDtypeStruct((bx, sx, d), F32),
                   jax.ShapeDtypeStruct((bx, sx, LANES), F32)],
        compiler_params=_cparams(("arbitrary", "arbitrary")),
        name="out_router",
    )(x, o, gate_m, g, shift, scale, wo, wr3)


def _gather_body(idx_ref, src_ref, dst_ref, sem):
    t = pl.program_id(0)
    n = idx_ref.shape[-1]

    def row_copy(r):
        return pltpu.make_async_copy(src_ref.at[pl.ds(idx_ref[0, 0, r], 1)], dst_ref.at[pl.ds(t * n + r, 1)], sem)

    def issue(r, c):
        row_copy(r).start()
        return c

    def drain(r, c):
        row_copy(r).wait()
        return c

    lax.fori_loop(0, n, issue, 0)
    lax.fori_loop(0, n, drain, 0)


def _gather_call(src, row_idx):
    n_rows = row_idx.shape[0]
    g = GATHER_ROWS
    assert n_rows % g == 0
    idx3 = row_idx.reshape(n_rows // g, 1, g)
    return pl.pallas_call(
        _gather_body,
        grid=(n_rows // g,),
        in_specs=[pl.BlockSpec((1, 1, g), lambda t: (t, 0, 0), memory_space=pltpu.SMEM),
                  pl.BlockSpec(memory_space=pl.ANY)],
        out_specs=pl.BlockSpec(memory_space=pl.ANY),
        out_shape=jax.ShapeDtypeStruct((n_rows, src.shape[1]), src.dtype),
        scratch_shapes=[pltpu.SemaphoreType.DMA(())],
        compiler_params=_cparams(("arbitrary",)),
        name="moe_gather",
    )(idx3, src)


def _moe_ffn_body(te_ref, nv_ref, xs_ref, w1_ref, w3_ref, w2_ref, y_ref):
    t = pl.program_id(0)
    j = pl.program_id(1)

    @pl.when(t < nv_ref[0])
    def _():
        xb = xs_ref[...].astype(BF)
        a = _dot(xb, w1_ref[0])
        g = _dot(xb, w3_ref[0])
        u = ((a * _sigmoid(a)) * g).astype(BF)
        part = _dot(u, w2_ref[0])

        @pl.when(j == 0)
        def _():
            y_ref[...] = part

        @pl.when(j > 0)
        def _():
            y_ref[...] += part

    @pl.when(t >= nv_ref[0])
    def _():
        y_ref[...] = jnp.zeros(y_ref.shape, F32)


def _moe_ffn_call(xs, tile_e, n_valid, w1, w3, w2):
    n_rows, d = xs.shape
    f_total = w1.shape[2]
    tf = MOE_F_TILE if f_total % MOE_F_TILE == 0 else f_total
    nf = f_total // tf
    tm = MOE_ROW_TILE

    def fcol(t, j, te, nv):
        return jnp.where(t < nv[0], j, nf - 1)

    grid_spec = pltpu.PrefetchScalarGridSpec(
        num_scalar_prefetch=2,
        grid=(n_rows // tm, nf),
        in_specs=[pl.BlockSpec((tm, d), lambda t, j, te, nv: (t, 0)),
                  pl.BlockSpec((1, d, tf), lambda t, j, te, nv: (te[t], 0, fcol(t, j, te, nv))),
                  pl.BlockSpec((1, d, tf), lambda t, j, te, nv: (te[t], 0, fcol(t, j, te, nv))),
                  pl.BlockSpec((1, tf, d), lambda t, j, te, nv: (te[t], fcol(t, j, te, nv), 0))],
        out_specs=pl.BlockSpec((tm, d), lambda t, j, te, nv: (t, 0)),
    )
    return pl.pallas_call(
        _moe_ffn_body,
        grid_spec=grid_spec,
        out_shape=jax.ShapeDtypeStruct((n_rows, d), F32),
        compiler_params=_cparams(("arbitrary", "arbitrary")),
        name="moe_ffn",
    )(tile_e, n_valid, xs, w1, w3, w2)


def _combine_body(d0_ref, d1_ref, x_ref, gf_ref, w_ref, g_ref, yb_ref, o_ref, buf, sem):
    t = pl.program_id(0)
    nt = pl.num_programs(0)
    tc = x_ref.shape[1]

    def row_copy(idx_ref, r, slot, k):
        return pltpu.make_async_copy(yb_ref.at[pl.ds(idx_ref[0, 0, 2 * r + k], 1)],
                                     buf.at[slot, k, pl.ds(r, 1)], sem.at[slot])

    def issue(idx_ref, slot):
        def body(r, c):
            row_copy(idx_ref, r, slot, 0).start()
            row_copy(idx_ref, r, slot, 1).start()
            return c
        lax.fori_loop(0, tc, body, 0)

    def drain(idx_ref, slot):
        def body(r, c):
            row_copy(idx_ref, r, slot, 0).wait()
            row_copy(idx_ref, r, slot, 1).wait()
            return c
        lax.fori_loop(0, tc, body, 0)

    slot = lax.rem(t, 2)

    @pl.when(t == 0)
    def _():
        issue(d0_ref, 0)

    @pl.when(t + 1 < nt)
    def _():
        issue(d1_ref, 1 - slot)

    drain(d0_ref, slot)
    w = w_ref[0]
    f = w[:, 2:3] * buf[slot, 0] + w[:, 3:4] * buf[slot, 1]
    y = x_ref[0] + gf_ref[0] * f
    y = y * lax.rsqrt(jnp.mean(y * y, axis=-1, keepdims=True) + RMS_EPS)
    o_ref[0] = y * g_ref[...]


def _combine_call(x1, gate_f, route, g_final, yb, dest):
    bx, sx, d = x1.shape
    tc = min(COMBINE_TILE, sx)
    nps = sx // tc
    nt = bx * nps
    dest3 = dest.reshape(nt, 1, 2 * tc)
    row = lambda t: (t // nps, t % nps, 0)
    if gate_f.shape[1] == 1:
        gate_spec = pl.BlockSpec((1, 1, d), lambda t: (t // nps, 0, 0))
    else:
        gate_spec = pl.BlockSpec((1, tc, d), row)
    return pl.pallas_call(
        _combine_body,
        grid=(nt,),
        in_specs=[pl.BlockSpec((1, 1, 2 * tc), lambda t: (t, 0, 0), memory_space=pltpu.SMEM),
                  pl.BlockSpec((1, 1, 2 * tc), lambda t: (jnp.minimum(t + 1, nt - 1), 0, 0),
                               memory_space=pltpu.SMEM),
                  pl.BlockSpec((1, tc, d), row), gate_spec,
                  pl.BlockSpec((1, tc, LANES), row),
                  pl.BlockSpec(g_final.shape, lambda t: (0, 0)),
                  pl.BlockSpec(memory_space=pl.ANY)],
        out_specs=pl.BlockSpec((1, tc, d), row),
        out_shape=jax.ShapeDtypeStruct((bx, sx, d), F32),
        scratch_shapes=[pltpu.VMEM((2, 2, tc, d), F32), pltpu.SemaphoreType.DMA((2,))],
        compiler_params=_cparams(("arbitrary",)),
        name="moe_combine",
    )(dest3, dest3, x1, gate_f, route, g_final, yb)


def _route_plan(route, n_exp, tm):
    n_tok = route.shape[0]
    n_slots = n_tok * TOP_K
    slot_e = route[:, :TOP_K].astype(jnp.int32).reshape(n_slots)
    onehot = (slot_e[:, None] == jnp.arange(n_exp, dtype=jnp.int32)[None, :]).astype(jnp.int32)
    csum = jnp.cumsum(onehot, axis=0)
    rank = jnp.sum(csum * onehot, axis=1) - 1
    counts = csum[-1]
    padded = ((counts + tm - 1) // tm) * tm
    pad_end = jnp.cumsum(padded)
    pad_start = pad_end - padded
    dest = pad_start[slot_e] + rank
    n_tiles = -(-n_slots // tm) + n_exp
    n_rows = n_tiles * tm
    slot_tok = jnp.repeat(jnp.arange(n_tok, dtype=jnp.int32), TOP_K)
    row_tok = jnp.zeros((n_rows,), jnp.int32).at[dest].set(slot_tok)
    n_valid = (pad_end[-1] // tm).astype(jnp.int32)
    tile_start = jnp.arange(n_tiles, dtype=jnp.int32) * tm
    tile_e = jnp.minimum(jnp.searchsorted(pad_end, tile_start, side='right'), n_exp - 1).astype(jnp.int32)
    last_e = tile_e[jnp.maximum(n_valid - 1, 0)]
    tile_e = jnp.where(jnp.arange(n_tiles) < n_valid, tile_e, last_e)
    return dest, row_tok, tile_e, n_valid.reshape(1)


def _moe(x1, h, route, gate_f, g_final, w1, w3, w2):
    bx, sx, d = x1.shape
    n_exp = w1.shape[0]
    dest, row_tok, tile_e, n_valid = _route_plan(route.reshape(bx * sx, LANES), n_exp, MOE_ROW_TILE)
    xs = _gather_call(h.reshape(bx * sx, d), row_tok)
    yb = _moe_ffn_call(xs, tile_e, n_valid, w1, w3, w2)
    return _combine_call(x1, gate_f, route, g_final, yb, dest)


def _trunk(x, mods, p, cache):
    sh_m0, sc_m0, gt_m0, sh_f0, sc_f0, gt_f0 = mods[0]
    sh_m1, sc_m1, gt_m1, sh_f1, sc_f1, gt_f1 = mods[1]
    wa = p['wa']
    wb = p['wb']
    wc = p['wc']
    bx, sx, d = x.shape

    if cache is None:
        keep = min(WINDOW_B, sx)
        qkv, ka, va, kb, vb, logf, _, ft = _inproj_even_call(
            x, p['g_mix0'], sh_m0, sc_m0, p['w_in_even'], p['w_f'], p['b_f'], keep, True)
        o_a = _fox_prompt_call(qkv, ft, wa)
        o_b = _band_prompt_call(qkv, p['rel_bias'], wa, wb)
        x_l0_state = (ka, va, logf, kb, vb)
    else:
        ck, cv, clogf, bk, bv, cck, ccv, b, t = cache
        qkv, ka, va, kb, vb, logf, lft, = _inproj_even_call(
            x, p['g_mix0'], sh_m0, sc_m0, p['w_in_even'], p['w_f'], p['b_f'], sx, False)
        nh = wa // HEAD_DIM
        p_len = ck.shape[1]
        lft_new = lft.reshape(nh, b, t).transpose(1, 0, 2)
        lp = -(-(p_len + t) // 512) * 512
        lcat = jnp.concatenate([clogf.transpose(0, 2, 1), lft_new,
                                jnp.zeros((b, nh, lp - p_len - t), F32)], axis=-1)
        ft_pad = _cumsum_call(lcat)
        qkv_b = qkv.reshape(b, t, qkv.shape[-1])
        o_a = _fox_sample_call(qkv_b, ck, cv, ft_pad, wa).reshape(1, b * t, wa)
        o_b = _band_sample_call(qkv_b, bk, bv, p['rel_bias'], wb).reshape(1, b * t, wb)
        x_l0_state = (ka, va, logf, kb, vb)
    o = jnp.concatenate([o_a, o_b], axis=-1)
    x = _out_ffn_call(x, o, gt_m0, p['g_ffn0'], sh_f0, sc_f0, gt_f0,
                      p['w_out_even'], p['w1_dense'], p['w3_dense'], p['w2_dense'])

    qkv, kc, vc = _inproj_odd_call(x, p['g_mix1'], sh_m1, sc_m1, p['w_in_odd'])
    if cache is None:
        o = _diff_prompt_call(qkv, p['lam_par'], p['g_sub'], wc, p['lam_init'])
    else:
        qkv_b = qkv.reshape(b, t, qkv.shape[-1])
        o = _diff_sample_call(qkv_b, cck, ccv, p['lam_par'], p['g_sub'], wc, p['lam_init']).reshape(1, b * t, wc)
    x1, h, route = _out_router_call(x, o, gt_m1, p['g_ffn1'], sh_f1, sc_f1, p['w_out_odd'], p['w_router3'],
                                    p['n_exp'])
    y = _moe(x1, h, route, gt_f1, p['g_final'], p['w1_moe'], p['w3_moe'], p['w2_moe'])
    return y, x_l0_state, (kc, vc)


def kernel(x_prompt, x_sample, cache_a_k, cache_a_v, cache_a_logf, cache_b_k, cache_b_v, cache_c_k, cache_c_v,
           c_prompt, c_sample, w_mod, b_mod, g_mix, g_ffn, g_final, w_in_even, b_forget, rel_bias, w_out_even,
           w_in_odd, lambda_q1, lambda_k1, lambda_q2, lambda_k2, g_subln, w_out_odd, w1_dense, w3_dense, w2_dense,
           w_router, w1_moe, w3_moe, w2_moe):
    bp, sp, d = x_prompt.shape
    bs, ts, _ = x_sample.shape
    h_a = cache_a_k.shape[3]
    h_b = cache_b_k.shape[3]
    h_c = cache_c_k.shape[3]
    wa, wb, wc = h_a * HEAD_DIM, h_b * HEAD_DIM, h_c * 2 * HEAD_DIM
    n_exp = w_router.shape[-1]
    assert w_mod.shape[0] == 2, "kernel is written for the 2-layer trunk"

    we = w_in_even[0]
    w_main = jnp.concatenate([we[:, :3 * wa], we[:, 3 * wa + h_a:]], axis=1).astype(BF)
    w_f = jnp.pad(we[:, 3 * wa:3 * wa + h_a], ((0, 0), (0, LANES - h_a))).astype(BF)
    b_f = jnp.pad(b_forget[0], (0, LANES - h_a)).reshape(1, LANES).astype(F32)
    wr = jnp.pad(w_router[0], ((0, 0), (0, LANES - n_exp)))
    wr_hi = wr.astype(BF)
    wr_r1 = wr - wr_hi.astype(F32)
    wr_mid = wr_r1.astype(BF)
    wr_lo = (wr_r1 - wr_mid.astype(F32)).astype(BF)
    params = dict(
        wa=wa, wb=wb, wc=wc, n_exp=n_exp, lam_init=0.8 - 0.6 * math.exp(-0.3 * 1),
        g_mix0=g_mix[0:1], g_mix1=g_mix[1:2], g_ffn0=g_ffn[0:1], g_ffn1=g_ffn[1:2],
        g_final=g_final.reshape(1, d),
        w_in_even=w_main, w_f=w_f, b_f=b_f, rel_bias=rel_bias[0],
        w_out_even=w_out_even[0].astype(BF), w_in_odd=w_in_odd[0].astype(BF),
        lam_par=jnp.concatenate([lambda_q1, lambda_k1, lambda_q2, lambda_k2], axis=0),
        g_sub=g_subln[0:1], w_out_odd=w_out_odd[0].astype(BF),
        w1_dense=w1_dense[0].astype(BF), w3_dense=w3_dense[0].astype(BF), w2_dense=w2_dense[0].astype(BF),
        w_router3=jnp.stack([wr_hi, wr_mid, wr_lo]),
        w1_moe=w1_moe[0].astype(BF), w3_moe=w3_moe[0].astype(BF), w2_moe=w2_moe[0].astype(BF),
    )

    mod = _mod_call(jnp.concatenate([c_prompt, c_sample], axis=0), w_mod, b_mod)

    def mods_for(rows, per_row_len):
        out = []
        for l in range(2):
            parts = []
            for k in range(6):
                m = mod[l, rows, k * d:(k + 1) * d]
                if per_row_len:
                    m = jnp.repeat(m, per_row_len, axis=0).reshape(1, -1, d)
                else:
                    m = m.reshape(-1, 1, d)
                parts.append(m)
            out.append(parts)
        return out

    y_p, ev_p, od_p = _trunk(x_prompt, mods_for(slice(0, bp), 0), params, None)
    cache = (cache_a_k[0].reshape(bs, -1, wa), cache_a_v[0].reshape(bs, -1, wa), cache_a_logf[0],
             cache_b_k[0].reshape(bs, -1, wb), cache_b_v[0].reshape(bs, -1, wb),
             cache_c_k[0].reshape(bs, -1, wc), cache_c_v[0].reshape(bs, -1, wc), bs, ts)
    y_s, ev_s, od_s = _trunk(x_sample.reshape(1, bs * ts, d), mods_for(slice(bp, bp + bs), ts), params, cache)

    ka, va, logf, kb, vb = ev_p
    keep = kb.shape[1]
    out_p = (ka.reshape(1, bp, sp, h_a, HEAD_DIM), va.reshape(1, bp, sp, h_a, HEAD_DIM),
             logf.reshape(1, bp, sp, h_a),
             kb.reshape(1, bp, keep, h_b, HEAD_DIM), vb.reshape(1, bp, keep, h_b, HEAD_DIM),
             od_p[0].reshape(1, bp, sp, h_c, 2 * HEAD_DIM), od_p[1].reshape(1, bp, sp, h_c, 2 * HEAD_DIM))
    ka, va, logf, kb, vb = ev_s
    new_bk = jnp.concatenate([cache_b_k[0], kb.reshape(bs, ts, h_b, HEAD_DIM)], axis=1)[:, ts:]
    new_bv = jnp.concatenate([cache_b_v[0], vb.reshape(bs, ts, h_b, HEAD_DIM)], axis=1)[:, ts:]
    out_s = (ka.reshape(1, bs, ts, h_a, HEAD_DIM), va.reshape(1, bs, ts, h_a, HEAD_DIM),
             logf.reshape(1, bs, ts, h_a), new_bk[None], new_bv[None],
             od_s[0].reshape(1, bs, ts, h_c, 2 * HEAD_DIM), od_s[1].reshape(1, bs, ts, h_c, 2 * HEAD_DIM))
    return (y_p, y_s.reshape(bs, ts, d)) + out_p + out_s
```

```python
import functools
import math

import jax
import jax.numpy as jnp
from jax import lax
from jax.experimental import pallas as pl
from jax.experimental.pallas import tpu as pltpu

BF = jnp.bfloat16
F32 = jnp.float32

CHUNK = 64
CHUNK_SHIFT = 6
HEAD_DIM = 64
BAND_CHUNKS = 8
WINDOW_B = BAND_CHUNKS * CHUNK
MAX_REL = 128
TOP_K = 2
RMS_EPS = 1e-6
NEG_INF = -1e30
QK_SCALE = HEAD_DIM ** -0.5

LANES = 128
VMEM_LIMIT_BYTES = 56 * 1024 * 1024

ROW_TILE = 512
ATTN_TILE = 512
BAND_Q_TILE = WINDOW_B // 2
CACHE_TILE_A = 1024
CACHE_TILE_C = 512
MOE_ROW_TILE = 512
MOE_F_TILE = 1792
COMBINE_TILE = 256


def _cparams(sem):
    return pltpu.CompilerParams(dimension_semantics=sem, vmem_limit_bytes=VMEM_LIMIT_BYTES)


def _dot(a, b):
    return jnp.dot(a, b, preferred_element_type=F32)


def _dot_nt(a, b):
    return lax.dot_general(a, b, (((1,), (1,)), ((), ())), preferred_element_type=F32)


def _sigmoid(x):
    return 1.0 / (1.0 + jnp.exp(-x))


def _modulate(x, g, shift, scale):
    y = x * lax.rsqrt(jnp.mean(x * x, axis=-1, keepdims=True) + RMS_EPS)
    return (y * g) * (1.0 + scale) + shift


def _split3(x):
    hi = x.astype(BF)
    r1 = x - hi.astype(F32)
    mid = r1.astype(BF)
    lo = (r1 - mid.astype(F32)).astype(BF)
    return hi, mid, lo


def _cumsum_lanes(x, upper):
    hi, mid, lo = _split3(x)
    return _dot(hi, upper) + _dot(mid, upper) + _dot(lo, upper)


def _upper_tri(t):
    r = jnp.arange(t)
    return (r[:, None] <= r[None, :]).astype(BF)


def _osm_update(s, v, m_ref, l_ref, acc_ref, idx):
    m_prev = m_ref[idx]
    m_new = jnp.maximum(m_prev, jnp.max(s, axis=-1, keepdims=True))
    alpha = jnp.exp(m_prev - m_new)
    p = jnp.exp(s - m_new)
    l_ref[idx] = alpha * l_ref[idx] + jnp.sum(p, axis=-1, keepdims=True)
    acc_ref[idx] = alpha * acc_ref[idx] + _dot(p.astype(BF), v)
    m_ref[idx] = m_new


def _osm_init(m_ref, l_ref, acc_ref):
    m_ref[...] = jnp.full(m_ref.shape, NEG_INF, F32)
    l_ref[...] = jnp.zeros(l_ref.shape, F32)
    acc_ref[...] = jnp.zeros(acc_ref.shape, F32)


def _mod_body(c_ref, w_ref, b_ref, o_ref):
    c = c_ref[...]
    s = (c * _sigmoid(c)).astype(BF)
    o_ref[0] = _dot(s, w_ref[0].astype(BF)) + b_ref[0]


def _mod_call(c_all, w_mod, b_mod):
    depth, d, n = w_mod.shape
    r = c_all.shape[0]
    tn = 1536 if n % 1536 == 0 else n
    return pl.pallas_call(
        _mod_body,
        grid=(depth, n // tn),
        in_specs=[pl.BlockSpec((r, d), lambda l, j: (0, 0)),
                  pl.BlockSpec((1, d, tn), lambda l, j: (l, 0, j)),
                  pl.BlockSpec((1, 1, tn), lambda l, j: (l, 0, j))],
        out_specs=pl.BlockSpec((1, r, tn), lambda l, j: (l, 0, j)),
        out_shape=jax.ShapeDtypeStruct((depth, r, n), F32),
        compiler_params=_cparams(("arbitrary", "arbitrary")),
        name="mod",
    )(c_all, w_mod, b_mod.reshape(depth, 1, n))


def _mod_spec(mod_arr, tm):
    d = mod_arr.shape[-1]
    if mod_arr.shape[1] == 1:
        return pl.BlockSpec((1, 1, d), lambda b, i: (b, 0, 0))
    return pl.BlockSpec((1, tm, d), lambda b, i: (b, i, 0))


def _const_spec(arr):
    nd = arr.ndim
    return pl.BlockSpec(arr.shape, lambda b, i: (0,) * nd, pipeline_mode=pl.Buffered(1))


def _inproj_even_body(*refs, wa, wb, cum):
    if cum:
        (x_ref, g_ref, sh_ref, sc_ref, w_ref, wf_ref, bf_ref, up_ref,
         qkv_ref, ka_ref, va_ref, kb_ref, vb_ref, logf_ref, lft_ref, ft_ref, carry_ref) = refs
    else:
        (x_ref, g_ref, sh_ref, sc_ref, w_ref, wf_ref, bf_ref,
         qkv_ref, ka_ref, va_ref, kb_ref, vb_ref, logf_ref, lft_ref) = refs
    i = pl.program_id(1)
    h = _modulate(x_ref[0], g_ref[...], sh_ref[0], sc_ref[0]).astype(BF)
    nh = logf_ref.shape[-1]

    qa = _dot(h, w_ref[:, 0:wa])
    ka = _dot(h, w_ref[:, wa:2 * wa])
    va = _dot(h, w_ref[:, 2 * wa:3 * wa])
    o = 3 * wa
    qb = _dot(h, w_ref[:, o:o + wb])
    kb = _dot(h, w_ref[:, o + wb:o + 2 * wb])
    vb = _dot(h, w_ref[:, o + 2 * wb:o + 3 * wb])
    ka_ref[0] = ka
    va_ref[0] = va
    kb_ref[0] = kb
    vb_ref[0] = vb
    qkv_ref[0] = jnp.concatenate(
        [(qa * QK_SCALE).astype(BF), ka.astype(BF), va.astype(BF),
         (qb * QK_SCALE).astype(BF), kb.astype(BF), vb.astype(BF)], axis=-1)

    z = _dot(h, wf_ref[...]) + bf_ref[...]
    logf = jnp.minimum(z, 0.0) - jnp.log1p(jnp.exp(-jnp.abs(z)))
    logf_ref[0] = logf[:, :nh]
    lft = logf.T[:nh, :]
    lft_ref[0] = lft
    if cum:
        @pl.when(i == 0)
        def _():
            carry_ref[...] = jnp.zeros(carry_ref.shape, F32)
        ft = _cumsum_lanes(lft, up_ref[...]) + carry_ref[:, 0:1]
        ft_ref[0] = ft
        carry_ref[...] = jnp.broadcast_to(ft[:, -1:], carry_ref.shape)


def _inproj_even_call(x, g, shift, scale, w_main, w_f, b_f, keep, cum):
    bx, sx, d = x.shape
    tm = min(ROW_TILE, sx)
    nt = sx // tm
    n = w_main.shape[1]
    wa = wb = n // 6
    nh = wa // HEAD_DIM
    nkeep = keep // tm
    tail = lambda b, i: (b, jnp.maximum(i - (nt - nkeep), 0), 0)
    row = lambda b, i: (b, i, 0)
    in_specs = [pl.BlockSpec((1, tm, d), row), _const_spec(g), _mod_spec(shift, tm), _mod_spec(scale, tm),
                _const_spec(w_main), _const_spec(w_f), _const_spec(b_f)]
    args = [x, g, shift, scale, w_main, w_f, b_f]
    out_specs = [pl.BlockSpec((1, tm, n), row),
                 pl.BlockSpec((1, tm, wa), row), pl.BlockSpec((1, tm, wa), row),
                 pl.BlockSpec((1, tm, wb), tail), pl.BlockSpec((1, tm, wb), tail),
                 pl.BlockSpec((1, tm, nh), row),
                 pl.BlockSpec((1, nh, tm), lambda b, i: (b, 0, i))]
    out_shape = [jax.ShapeDtypeStruct((bx, sx, n), BF),
                 jax.ShapeDtypeStruct((bx, sx, wa), F32), jax.ShapeDtypeStruct((bx, sx, wa), F32),
                 jax.ShapeDtypeStruct((bx, keep, wb), F32), jax.ShapeDtypeStruct((bx, keep, wb), F32),
                 jax.ShapeDtypeStruct((bx, sx, nh), F32),
                 jax.ShapeDtypeStruct((bx, nh, sx), F32)]
    scratch = []
    if cum:
        up = _upper_tri(tm)
        in_specs.append(_const_spec(up))
        args.append(up)
        out_specs.append(pl.BlockSpec((1, nh, tm), lambda b, i: (b, 0, i)))
        out_shape.append(jax.ShapeDtypeStruct((bx, nh, sx), F32))
        scratch.append(pltpu.VMEM((nh, LANES), F32))
    return pl.pallas_call(
        functools.partial(_inproj_even_body, wa=wa, wb=wb, cum=cum),
        grid=(bx, nt), in_specs=in_specs, out_specs=out_specs, out_shape=out_shape,
        scratch_shapes=scratch,
        compiler_params=_cparams(("arbitrary", "arbitrary")),
        name="inproj_even",
    )(*args)


def _inproj_odd_body(x_ref, g_ref, sh_ref, sc_ref, w_ref, qkv_ref, k_ref, v_ref, *, wc):
    h = _modulate(x_ref[0], g_ref[...], sh_ref[0], sc_ref[0]).astype(BF)
    q = _dot(h, w_ref[:, 0:wc])
    k = _dot(h, w_ref[:, wc:2 * wc])
    v = _dot(h, w_ref[:, 2 * wc:3 * wc])
    k_ref[0] = k
    v_ref[0] = v
    qkv_ref[0] = jnp.concatenate([(q * QK_SCALE).astype(BF), k.astype(BF), v.astype(BF)], axis=-1)


def _inproj_odd_call(x, g, shift, scale, w):
    bx, sx, d = x.shape
    tm = min(ROW_TILE, sx)
    n = w.shape[1]
    wc = n // 3
    row = lambda b, i: (b, i, 0)
    return pl.pallas_call(
        functools.partial(_inproj_odd_body, wc=wc),
        grid=(bx, sx // tm),
        in_specs=[pl.BlockSpec((1, tm, d), row), _const_spec(g), _mod_spec(shift, tm), _mod_spec(scale, tm),
                  _const_spec(w)],
        out_specs=[pl.BlockSpec((1, tm, n), row), pl.BlockSpec((1, tm, wc), row), pl.BlockSpec((1, tm, wc), row)],
        out_shape=[jax.ShapeDtypeStruct((bx, sx, n), BF),
                   jax.ShapeDtypeStruct((bx, sx, wc), F32), jax.ShapeDtypeStruct((bx, sx, wc), F32)],
        compiler_params=_cparams(("arbitrary", "arbitrary")),
        name="inproj_odd",
    )(x, g, shift, scale, w)


def _fox_prompt_body(q_ref, k_ref, v_ref, ft_ref, o_ref, m_ref, l_ref, acc_ref, *, nh):
    i = pl.program_id(1)
    j = pl.program_id(2)
    hd = HEAD_DIM

    @pl.when(j == 0)
    def _():
        _osm_init(m_ref, l_ref, acc_ref)

    def step(masked):
        q = q_ref[0]
        k = k_ref[0]
        v = v_ref[0]
        ft = ft_ref[0]
        tq, tk = q.shape[0], k.shape[0]
        if masked:
            visible = lax.broadcasted_iota(jnp.int32, (tq, tk), 1) <= lax.broadcasted_iota(jnp.int32, (tq, tk), 0)
        for h in range(nh):
            sl = slice(h * hd, (h + 1) * hd)
            s = _dot_nt(q[:, sl], k[:, sl]) - ft[h:h + 1, :]
            if masked:
                s = jnp.where(visible, s, NEG_INF)
            _osm_update(s, v[:, sl], m_ref, l_ref, acc_ref, h)

    @pl.when(j < i)
    def _():
        step(False)

    @pl.when(j == i)
    def _():
        step(True)

    @pl.when(j == pl.num_programs(2) - 1)
    def _():
        o_ref[0] = jnp.concatenate([(acc_ref[h] / l_ref[h]).astype(BF) for h in range(nh)], axis=-1)


def _fox_prompt_call(qkv, ft, wa):
    b, s, _ = qkv.shape
    nh = wa // HEAD_DIM
    t = min(ATTN_TILE, s)
    nq = s // t
    return pl.pallas_call(
        functools.partial(_fox_prompt_body, nh=nh),
        grid=(b, nq, nq),
        in_specs=[pl.BlockSpec((1, t, wa), lambda b, i, j: (b, i, 0)),
                  pl.BlockSpec((1, t, wa), lambda b, i, j: (b, jnp.minimum(j, i), 1)),
                  pl.BlockSpec((1, t, wa), lambda b, i, j: (b, jnp.minimum(j, i), 2)),
                  pl.BlockSpec((1, nh, t), lambda b, i, j: (b, 0, jnp.minimum(j, i)))],
        out_specs=pl.BlockSpec((1, t, wa), lambda b, i, j: (b, i, 0)),
        out_shape=jax.ShapeDtypeStruct((b, s, wa), BF),
        scratch_shapes=[pltpu.VMEM((nh, t, 1), F32), pltpu.VMEM((nh, t, 1), F32),
                        pltpu.VMEM((nh, t, HEAD_DIM), F32)],
        compiler_params=_cparams(("arbitrary", "arbitrary", "arbitrary")),
        name="fox_prompt",
    )(qkv, qkv, qkv, ft)


def _band_prompt_body(q_ref, k0_ref, k1_ref, k2_ref, v0_ref, v1_ref, v2_ref, bias_ref, o_ref, *, nh):
    i = pl.program_id(1)
    hd = HEAD_DIM
    q = q_ref[0]
    tq = q.shape[0]
    k = jnp.concatenate([k0_ref[0], k1_ref[0], k2_ref[0]], axis=0)
    v = jnp.concatenate([v0_ref[0], v1_ref[0], v2_ref[0]], axis=0)
    in_seq = lax.broadcasted_iota(jnp.int32, (tq, 3 * tq), 1) >= (2 - i) * tq
    outs = []
    for h in range(nh):
        sl = slice(h * hd, (h + 1) * hd)
        s = _dot_nt(q[:, sl], k[:, sl]) + bias_ref[h]
        s = jnp.where(in_seq, s, NEG_INF)
        m = jnp.max(s, axis=-1, keepdims=True)
        p = jnp.exp(s - m)
        l = jnp.sum(p, axis=-1, keepdims=True)
        outs.append((_dot(p.astype(BF), v[:, sl]) / l).astype(BF))
    o_ref[0] = jnp.concatenate(outs, axis=-1)


def _band_bias_prompt(rel_bias, tq):
    nh = rel_bias.shape[0]
    period = 4 * tq
    t = jnp.arange(period)
    t = jnp.where(t < 3 * tq, t, t - period)
    u = rel_bias[:, jnp.clip(2 * tq - t, -MAX_REL, MAX_REL) + MAX_REL].astype(F32)
    skew = jnp.tile(u, (1, tq))[:, :tq * (period - 1)].reshape(nh, tq, period - 1)
    bias = skew[:, :, :3 * tq]
    r = jnp.arange(tq)[:, None]
    c = jnp.arange(3 * tq)[None, :]
    valid = (c // CHUNK >= r // CHUNK) & (c // CHUNK <= r // CHUNK + BAND_CHUNKS)
    return jnp.where(valid[None], bias, NEG_INF)


def _band_prompt_call(qkv, rel_bias, wa, wb):
    b, s, _ = qkv.shape
    nh = wb // HEAD_DIM
    tq = BAND_Q_TILE
    assert s % tq == 0 and wa == wb
    bias = _band_bias_prompt(rel_bias, tq)
    cq, ck, cv = 3, 4, 5

    def kv_spec(col, back):
        return pl.BlockSpec((1, tq, wb), lambda b, i: (b, jnp.maximum(i - back, 0), col))

    return pl.pallas_call(
        functools.partial(_band_prompt_body, nh=nh),
        grid=(b, s // tq),
        in_specs=[pl.BlockSpec((1, tq, wb), lambda b, i: (b, i, cq)),
                  kv_spec(ck, 2), kv_spec(ck, 1), kv_spec(ck, 0),
                  kv_spec(cv, 2), kv_spec(cv, 1), kv_spec(cv, 0),
                  _const_spec(bias)],
        out_specs=pl.BlockSpec((1, tq, wb), lambda b, i: (b, i, 0)),
        out_shape=jax.ShapeDtypeStruct((b, s, wb), BF),
        compiler_params=_cparams(("arbitrary", "arbitrary")),
        name="band_prompt",
    )(qkv, qkv, qkv, qkv, qkv, qkv, qkv, bias)


def _lambda(lam_ref, lam_init):
    lp = lam_ref[...]
    a = jnp.sum(lp[0:1] * lp[1:2], axis=-1, keepdims=True)
    b = jnp.sum(lp[2:3] * lp[3:4], axis=-1, keepdims=True)
    return jnp.exp(a) - jnp.exp(b) + lam_init


def _diff_finalize(l_ref, acc_ref, lam_ref, gsub_ref, nh, lam_init):
    lam = _lambda(lam_ref, lam_init)
    g = gsub_ref[...]
    outs = []
    for h in range(nh):
        o = acc_ref[2 * h] / l_ref[2 * h] - lam * (acc_ref[2 * h + 1] / l_ref[2 * h + 1])
        o = o * lax.rsqrt(jnp.mean(o * o, axis=-1, keepdims=True) + RMS_EPS)
        outs.append(((o * g) * (1.0 - lam_init)).astype(BF))
    return jnp.concatenate(outs, axis=-1)


def _diff_prompt_body(q_ref, k_ref, v_ref, lam_ref, gsub_ref, o_ref, m_ref, l_ref, acc_ref, *, nh, lam_init):
    i = pl.program_id(1)
    j = pl.program_id(2)
    hd = HEAD_DIM

    @pl.when(j == 0)
    def _():
        _osm_init(m_ref, l_ref, acc_ref)

    def step(masked):
        q = q_ref[0]
        k = k_ref[0]
        v = v_ref[0]
        tq, tk = q.shape[0], k.shape[0]
        row = lax.broadcasted_iota(jnp.int32, (tq, tk), 0)
        col = lax.broadcasted_iota(jnp.int32, (tq, tk), 1)
        dist = jnp.abs((i - j) * tq + row - col).astype(F32)
        if masked:
            visible = (col >> CHUNK_SHIFT) <= (row >> CHUNK_SHIFT)
        for h in range(nh):
            slope = 2.0 ** (-8.0 * (h + 1) / nh)
            bias = slope * dist
            vh = v[:, 2 * h * hd:(2 * h + 2) * hd]
            for t in range(2):
                sl = slice((2 * h + t) * hd, (2 * h + t + 1) * hd)
                s = _dot_nt(q[:, sl], k[:, sl]) - bias
                if masked:
                    s = jnp.where(visible, s, NEG_INF)
                _osm_update(s, vh, m_ref, l_ref, acc_ref, 2 * h + t)

    @pl.when(j < i)
    def _():
        step(False)

    @pl.when(j == i)
    def _():
        step(True)

    @pl.when(j == pl.num_programs(2) - 1)
    def _():
        o_ref[0] = _diff_finalize(l_ref, acc_ref, lam_ref, gsub_ref, nh, lam_init)


def _diff_prompt_call(qkv, lam_par, g_sub, wc, lam_init):
    b, s, _ = qkv.shape
    nh = wc // (2 * HEAD_DIM)
    t = min(ATTN_TILE, s)
    nq = s // t
    return pl.pallas_call(
        functools.partial(_diff_prompt_body, nh=nh, lam_init=lam_init),
        grid=(b, nq, nq),
        in_specs=[pl.BlockSpec((1, t, wc), lambda b, i, j: (b, i, 0)),
                  pl.BlockSpec((1, t, wc), lambda b, i, j: (b, jnp.minimum(j, i), 1)),
                  pl.BlockSpec((1, t, wc), lambda b, i, j: (b, jnp.minimum(j, i), 2)),
                  pl.BlockSpec(lam_par.shape, lambda b, i, j: (0, 0)),
                  pl.BlockSpec(g_sub.shape, lambda b, i, j: (0, 0))],
        out_specs=pl.BlockSpec((1, t, wc), lambda b, i, j: (b, i, 0)),
        out_shape=jax.ShapeDtypeStruct((b, s, wc), BF),
        scratch_shapes=[pltpu.VMEM((2 * nh, t, 1), F32), pltpu.VMEM((2 * nh, t, 1), F32),
                        pltpu.VMEM((2 * nh, t, 2 * HEAD_DIM), F32)],
        compiler_params=_cparams(("arbitrary", "arbitrary", "arbitrary")),
        name="diff_prompt",
    )(qkv, qkv, qkv, lam_par, g_sub)


def _cumsum_body(x_ref, up_ref, o_ref, carry_ref):
    @pl.when(pl.program_id(1) == 0)
    def _():
        carry_ref[...] = jnp.zeros(carry_ref.shape, F32)
    ft = _cumsum_lanes(x_ref[0], up_ref[...]) + carry_ref[:, 0:1]
    o_ref[0] = ft
    carry_ref[...] = jnp.broadcast_to(ft[:, -1:], carry_ref.shape)


def _cumsum_call(x):
    b, nh, length = x.shape
    t = 512
    up = _upper_tri(t)
    return pl.pallas_call(
        _cumsum_body,
        grid=(b, length // t),
        in_specs=[pl.BlockSpec((1, nh, t), lambda b, j: (b, 0, j)), pl.BlockSpec((t, t), lambda b, j: (0, 0))],
        out_specs=pl.BlockSpec((1, nh, t), lambda b, j: (b, 0, j)),
        out_shape=jax.ShapeDtypeStruct(x.shape, F32),
        scratch_shapes=[pltpu.VMEM((nh, LANES), F32)],
        compiler_params=_cparams(("arbitrary", "arbitrary")),
        name="logf_cumsum",
    )(x, up)


def _fox_sample_body(q_ref, kn_ref, vn_ref, ck_ref, cv_ref, ftc_ref, ftn_ref, o_ref, m_ref, l_ref, acc_ref, *, nh):
    j = pl.program_id(1)
    hd = HEAD_DIM
    q = q_ref[0]
    t = q.shape[0]

    @pl.when(j == 0)
    def _():
        _osm_init(m_ref, l_ref, acc_ref)

    k = ck_ref[0].astype(BF)
    v = cv_ref[0].astype(BF)
    ft = ftc_ref[0]
    for h in range(nh):
        sl = slice(h * hd, (h + 1) * hd)
        s = _dot_nt(q[:, sl], k[:, sl]) - ft[h:h + 1, :]
        _osm_update(s, v[:, sl], m_ref, l_ref, acc_ref, h)

    @pl.when(j == pl.num_programs(1) - 1)
    def _():
        kn = kn_ref[0]
        vn = vn_ref[0]
        ftn = ftn_ref[0][:, :t]
        visible = lax.broadcasted_iota(jnp.int32, (t, t), 1) <= lax.broadcasted_iota(jnp.int32, (t, t), 0)
        outs = []
        for h in range(nh):
            sl = slice(h * hd, (h + 1) * hd)
            s = _dot_nt(q[:, sl], kn[:, sl]) - ftn[h:h + 1, :]
            s = jnp.where(visible, s, NEG_INF)
            _osm_update(s, vn[:, sl], m_ref, l_ref, acc_ref, h)
            outs.append((acc_ref[h] / l_ref[h]).astype(BF))
        o_ref[0] = jnp.concatenate(outs, axis=-1)


def _fox_sample_call(qkv, cache_k, cache_v, ft_pad, wa):
    b, t, _ = qkv.shape
    p_len = cache_k.shape[1]
    nh = wa // HEAD_DIM
    tk = min(CACHE_TILE_A, p_len)
    assert p_len % tk == 0 and p_len % LANES == 0 and t <= LANES
    return pl.pallas_call(
        functools.partial(_fox_sample_body, nh=nh),
        grid=(b, p_len // tk),
        in_specs=[pl.BlockSpec((1, t, wa), lambda b, j: (b, 0, 0)),
                  pl.BlockSpec((1, t, wa), lambda b, j: (b, 0, 1)),
                  pl.BlockSpec((1, t, wa), lambda b, j: (b, 0, 2)),
                  pl.BlockSpec((1, tk, wa), lambda b, j: (b, j, 0)),
                  pl.BlockSpec((1, tk, wa), lambda b, j: (b, j, 0)),
                  pl.BlockSpec((1, nh, tk), lambda b, j: (b, 0, j)),
                  pl.BlockSpec((1, nh, LANES), lambda b, j: (b, 0, p_len // LANES))],
        out_specs=pl.BlockSpec((1, t, wa), lambda b, j: (b, 0, 0)),
        out_shape=jax.ShapeDtypeStruct((b, t, wa), BF),
        scratch_shapes=[pltpu.VMEM((nh, t, 1), F32), pltpu.VMEM((nh, t, 1), F32),
                        pltpu.VMEM((nh, t, HEAD_DIM), F32)],
        compiler_params=_cparams(("arbitrary", "arbitrary")),
        name="fox_sample",
    )(qkv, qkv, qkv, cache_k, cache_v, ft_pad, ft_pad)


def _band_sample_body(q_ref, kn_ref, vn_ref, bk_ref, bv_ref, bias_b_ref, bias_n_ref, o_ref, *, nh):
    hd = HEAD_DIM
    q = q_ref[0]
    kn = kn_ref[0]
    vn = vn_ref[0]
    kb = bk_ref[0].astype(BF)
    vb = bv_ref[0].astype(BF)
    outs = []
    for h in range(nh):
        sl = slice(h * hd, (h + 1) * hd)
        sb = _dot_nt(q[:, sl], kb[:, sl]) + bias_b_ref[h]
        sn = _dot_nt(q[:, sl], kn[:, sl]) + bias_n_ref[h]
        m = jnp.maximum(jnp.max(sb, axis=-1, keepdims=True), jnp.max(sn, axis=-1, keepdims=True))
        pb = jnp.exp(sb - m)
        pn = jnp.exp(sn - m)
        l = jnp.sum(pb, axis=-1, keepdims=True) + jnp.sum(pn, axis=-1, keepdims=True)
        o = _dot(pb.astype(BF), vb[:, sl]) + _dot(pn.astype(BF), vn[:, sl])
        outs.append((o / l).astype(BF))
    o_ref[0] = jnp.concatenate(outs, axis=-1)


def _band_sample_call(qkv, buf_k, buf_v, rel_bias, wb):
    b, t, _ = qkv.shape
    lb = buf_k.shape[1]
    nh = wb // HEAD_DIM
    k_pos = jnp.concatenate([jnp.arange(-lb, 0), jnp.arange(t)])
    rel = jnp.arange(t)[:, None] - k_pos[None, :]
    bias = rel_bias[:, jnp.clip(rel, -MAX_REL, MAX_REL) + MAX_REL].astype(F32)
    bias_b, bias_n = bias[:, :, :lb], bias[:, :, lb:]
    return pl.pallas_call(
        functools.partial(_band_sample_body, nh=nh),
        grid=(b,),
        in_specs=[pl.BlockSpec((1, t, wb), lambda b: (b, 0, 3)),
                  pl.BlockSpec((1, t, wb), lambda b: (b, 0, 4)),
                  pl.BlockSpec((1, t, wb), lambda b: (b, 0, 5)),
                  pl.BlockSpec((1, lb, wb), lambda b: (b, 0, 0)),
                  pl.BlockSpec((1, lb, wb), lambda b: (b, 0, 0)),
                  pl.BlockSpec(bias_b.shape, lambda b: (0, 0, 0)),
                  pl.BlockSpec(bias_n.shape, lambda b: (0, 0, 0))],
        out_specs=pl.BlockSpec((1, t, wb), lambda b: (b, 0, 0)),
        out_shape=jax.ShapeDtypeStruct((b, t, wb), BF),
        compiler_params=_cparams(("arbitrary",)),
        name="band_sample",
    )(qkv, qkv, qkv, buf_k, buf_v, bias_b, bias_n)


def _diff_sample_body(q_ref, kn_ref, vn_ref, ck_ref, cv_ref, lam_ref, gsub_ref, o_ref, m_ref, l_ref, acc_ref,
                      *, nh, lam_init, p_len):
    j = pl.program_id(1)
    hd = HEAD_DIM
    q = q_ref[0]
    t = q.shape[0]

    @pl.when(j == 0)
    def _():
        _osm_init(m_ref, l_ref, acc_ref)

    def attend(k, v, dist):
        for h in range(nh):
            slope = 2.0 ** (-8.0 * (h + 1) / nh)
            bias = slope * dist
            vh = v[:, 2 * h * hd:(2 * h + 2) * hd]
            for u in range(2):
                sl = slice((2 * h + u) * hd, (2 * h + u + 1) * hd)
                s = _dot_nt(q[:, sl], k[:, sl]) - bias
                _osm_update(s, vh, m_ref, l_ref, acc_ref, 2 * h + u)

    tk = ck_ref.shape[1]
    row = lax.broadcasted_iota(jnp.int32, (t, tk), 0)
    col = lax.broadcasted_iota(jnp.int32, (t, tk), 1)
    attend(ck_ref[0].astype(BF), cv_ref[0].astype(BF), (p_len + row - j * tk - col).astype(F32))

    @pl.when(j == pl.num_programs(1) - 1)
    def _():
        rown = lax.broadcasted_iota(jnp.int32, (t, t), 0)
        coln = lax.broadcasted_iota(jnp.int32, (t, t), 1)
        attend(kn_ref[0], vn_ref[0], jnp.abs(rown - coln).astype(F32))
        o_ref[0] = _diff_finalize(l_ref, acc_ref, lam_ref, gsub_ref, nh, lam_init)


def _diff_sample_call(qkv, cache_k, cache_v, lam_par, g_sub, wc, lam_init):
    b, t, _ = qkv.shape
    p_len = cache_k.shape[1]
    assert (p_len // CHUNK) * CHUNK == p_len and t <= CHUNK
    nh = wc // (2 * HEAD_DIM)
    tk = min(CACHE_TILE_C, p_len)
    return pl.pallas_call(
        functools.partial(_diff_sample_body, nh=nh, lam_init=lam_init, p_len=p_len),
        grid=(b, p_len // tk),
        in_specs=[pl.BlockSpec((1, t, wc), lambda b, j: (b, 0, 0)),
                  pl.BlockSpec((1, t, wc), lambda b, j: (b, 0, 1)),
                  pl.BlockSpec((1, t, wc), lambda b, j: (b, 0, 2)),
                  pl.BlockSpec((1, tk, wc), lambda b, j: (b, j, 0)),
                  pl.BlockSpec((1, tk, wc), lambda b, j: (b, j, 0)),
                  pl.BlockSpec(lam_par.shape, lambda b, j: (0, 0)),
                  pl.BlockSpec(g_sub.shape, lambda b, j: (0, 0))],
        out_specs=pl.BlockSpec((1, t, wc), lambda b, j: (b, 0, 0)),
        out_shape=jax.ShapeDtypeStruct((b, t, wc), BF),
        scratch_shapes=[pltpu.VMEM((2 * nh, t, 1), F32), pltpu.VMEM((2 * nh, t, 1), F32),
                        pltpu.VMEM((2 * nh, t, 2 * HEAD_DIM), F32)],
        compiler_params=_cparams(("arbitrary", "arbitrary")),
        name="diff_sample",
    )(qkv, qkv, qkv, cache_k, cache_v, lam_par, g_sub)


def _out_ffn_body(x_ref, o_ref, gm_ref, g_ref, sh_ref, sc_ref, gf_ref, wo_ref, w1_ref, w3_ref, w2_ref, y_ref,
                  *, f_chunk):
    x1 = x_ref[0] + gm_ref[0] * _dot(o_ref[0], wo_ref[...])
    h = _modulate(x1, g_ref[...], sh_ref[0], sc_ref[0]).astype(BF)
    f_total = w1_ref.shape[1]
    f = None
    for c0 in range(0, f_total, f_chunk):
        a = _dot(h, w1_ref[:, c0:c0 + f_chunk])
        g = _dot(h, w3_ref[:, c0:c0 + f_chunk])
        u = ((a * _sigmoid(a)) * g).astype(BF)
        part = _dot(u, w2_ref[c0:c0 + f_chunk, :])
        f = part if f is None else f + part
    y_ref[0] = x1 + gf_ref[0] * f


def _out_ffn_call(x, o, gate_m, g, shift, scale, gate_f, wo, w1, w3, w2):
    bx, sx, d = x.shape
    tm = min(ROW_TILE, sx)
    f_total = w1.shape[1]
    f_chunk = f_total // 2 if (f_total // 2) % LANES == 0 else f_total
    row = lambda b, i: (b, i, 0)
    return pl.pallas_call(
        functools.partial(_out_ffn_body, f_chunk=f_chunk),
        grid=(bx, sx // tm),
        in_specs=[pl.BlockSpec((1, tm, d), row), pl.BlockSpec((1, tm, o.shape[-1]), row),
                  _mod_spec(gate_m, tm), _const_spec(g), _mod_spec(shift, tm), _mod_spec(scale, tm),
                  _mod_spec(gate_f, tm), _const_spec(wo), _const_spec(w1), _const_spec(w3), _const_spec(w2)],
        out_specs=pl.BlockSpec((1, tm, d), row),
        out_shape=jax.ShapeDtypeStruct((bx, sx, d), F32),
        compiler_params=_cparams(("arbitrary", "arbitrary")),
        name="out_ffn",
    )(x, o, gate_m, g, shift, scale, gate_f, wo, w1, w3, w2)


def _out_router_body(x_ref, o_ref, gm_ref, g_ref, sh_ref, sc_ref, wo_ref, wr_ref, x1_ref, h_ref, route_ref,
                     *, n_exp):
    x1 = x_ref[0] + gm_ref[0] * _dot(o_ref[0], wo_ref[...])
    x1_ref[0] = x1
    h = _modulate(x1, g_ref[...], sh_ref[0], sc_ref[0])
    hb = h.astype(BF)
    h_ref[0] = hb.astype(F32)
    logits = _dot(hb, wr_ref[0]) + _dot(hb, wr_ref[1]) + _dot(hb, wr_ref[2])
    col = lax.broadcasted_iota(jnp.int32, logits.shape, 1)
    lg = jnp.where(col < n_exp, logits, NEG_INF)
    e = jnp.exp(lg - jnp.max(lg, axis=-1, keepdims=True))
    probs = e / jnp.sum(e, axis=-1, keepdims=True)
    p1 = jnp.max(probs, axis=-1, keepdims=True)
    i1 = jnp.min(jnp.where(probs == p1, col, LANES), axis=-1, keepdims=True)
    rest = jnp.where(col == i1, -1.0, probs)
    p2 = jnp.max(rest, axis=-1, keepdims=True)
    i2 = jnp.min(jnp.where(rest == p2, col, LANES), axis=-1, keepdims=True)
    tot = p1 + p2
    route = jnp.where(col == 0, i1.astype(F32),
                      jnp.where(col == 1, i2.astype(F32),
                                jnp.where(col == 2, p1 / tot, jnp.where(col == 3, p2 / tot, 0.0))))
    route_ref[0] = route


def _out_router_call(x, o, gate_m, g, shift, scale, wo, wr3, n_exp):
    bx, sx, d = x.shape
    tm = min(ROW_TILE, sx)
    row = lambda b, i: (b, i, 0)
    return pl.pallas_call(
        functools.partial(_out_router_body, n_exp=n_exp),
        grid=(bx, sx // tm),
        in_specs=[pl.BlockSpec((1, tm, d), row), pl.BlockSpec((1, tm, o.shape[-1]), row),
                  _mod_spec(gate_m, tm), _const_spec(g), _mod_spec(shift, tm), _mod_spec(scale, tm),
                  _const_spec(wo), _const_spec(wr3)],
        out_specs=[pl.BlockSpec((1, tm, d), row), pl.BlockSpec((1, tm, d), row),
                   pl.BlockSpec((1, tm, LANES), row)],
        out_shape=[jax.ShapeDtypeStruct((bx, sx, d), F32), jax.ShapeDtypeStruct((bx, sx, d), F32),
                   jax.ShapeDtypeStruct((bx, sx, LANES), F32)],
        compiler_params=_cparams(("arbitrary", "arbitrary")),
        name="out_router",
    )(x, o, gate_m, g, shift, scale, wo, wr3)


def _moe_ffn_body(te_ref, nv_ref, idx_ref, idx_next_ref, h_ref, w1_ref, w3_ref, w2_ref, y_ref, xbuf, xb16, sem):
    t = pl.program_id(0)
    j = pl.program_id(1)
    nv = nv_ref[0]
    tm = xbuf.shape[1]
    slot = lax.rem(t, 2)

    def issue(rows_ref, s):
        def body(r, c):
            pltpu.make_async_copy(h_ref.at[pl.ds(rows_ref[0, 0, r], 1)], xbuf.at[s, pl.ds(r, 1)], sem.at[s]).start()
            return c
        lax.fori_loop(0, tm, body, 0, unroll=8)

    @pl.when((j == 0) & (t == 0))
    def _():
        issue(idx_ref, 0)

    @pl.when((j == 0) & (t + 1 < nv))
    def _():
        issue(idx_next_ref, 1 - slot)

    @pl.when((j == 0) & (t < nv))
    def _():
        pltpu.make_async_copy(h_ref.at[pl.ds(0, tm)], xbuf.at[slot], sem.at[slot]).wait()
        xb16[...] = xbuf[slot].astype(BF)

    @pl.when(t < nv)
    def _():
        xb = xb16[...]
        a = _dot(xb, w1_ref[0])
        g = _dot(xb, w3_ref[0])
        u = ((a * _sigmoid(a)) * g).astype(BF)
        part = _dot(u, w2_ref[0])

        @pl.when(j == 0)
        def _():
            y_ref[...] = part

        @pl.when(j > 0)
        def _():
            y_ref[...] += part

    @pl.when(t >= nv)
    def _():
        y_ref[...] = jnp.zeros(y_ref.shape, F32)


def _moe_ffn_call(h, row_tok, tile_e, n_valid, w1, w3, w2):
    n_rows = row_tok.shape[0]
    d = h.shape[1]
    f_total = w1.shape[2]
    tf = MOE_F_TILE if f_total % MOE_F_TILE == 0 else f_total
    nf = f_total // tf
    tm = MOE_ROW_TILE
    n_tiles = n_rows // tm
    idx3 = row_tok.reshape(n_tiles, 1, tm)

    def fcol(t, j, te, nv):
        return jnp.where(t < nv[0], j, nf - 1)

    grid_spec = pltpu.PrefetchScalarGridSpec(
        num_scalar_prefetch=2,
        grid=(n_tiles, nf),
        in_specs=[pl.BlockSpec((1, 1, tm), lambda t, j, te, nv: (t, 0, 0), memory_space=pltpu.SMEM),
                  pl.BlockSpec((1, 1, tm), lambda t, j, te, nv: (jnp.minimum(t + 1, n_tiles - 1), 0, 0),
                               memory_space=pltpu.SMEM),
                  pl.BlockSpec(memory_space=pl.ANY),
                  pl.BlockSpec((1, d, tf), lambda t, j, te, nv: (te[t], 0, fcol(t, j, te, nv))),
                  pl.BlockSpec((1, d, tf), lambda t, j, te, nv: (te[t], 0, fcol(t, j, te, nv))),
                  pl.BlockSpec((1, tf, d), lambda t, j, te, nv: (te[t], fcol(t, j, te, nv), 0))],
        out_specs=pl.BlockSpec((tm, d), lambda t, j, te, nv: (t, 0)),
        scratch_shapes=[pltpu.VMEM((2, tm, d), F32), pltpu.VMEM((tm, d), BF), pltpu.SemaphoreType.DMA((2,))],
    )
    return pl.pallas_call(
        _moe_ffn_body,
        grid_spec=grid_spec,
        out_shape=jax.ShapeDtypeStruct((n_rows, d), F32),
        compiler_params=_cparams(("arbitrary", "arbitrary")),
        name="moe_ffn",
    )(tile_e, n_valid, idx3, idx3, h, w1, w3, w2)


def _combine_body(d0_ref, d1_ref, x_ref, gf_ref, w_ref, g_ref, yb_ref, o_ref, buf, sem):
    t = pl.program_id(0)
    nt = pl.num_programs(0)
    tc = x_ref.shape[1]

    def row_copy(idx_ref, r, slot, k):
        return pltpu.make_async_copy(yb_ref.at[pl.ds(idx_ref[0, 0, 2 * r + k], 1)],
                                     buf.at[slot, k, pl.ds(r, 1)], sem.at[slot])

    def issue(idx_ref, slot):
        def body(r, c):
            row_copy(idx_ref, r, slot, 0).start()
            row_copy(idx_ref, r, slot, 1).start()
            return c
        lax.fori_loop(0, tc, body, 0, unroll=4)

    slot = lax.rem(t, 2)

    @pl.when(t == 0)
    def _():
        issue(d0_ref, 0)

    @pl.when(t + 1 < nt)
    def _():
        issue(d1_ref, 1 - slot)

    for k in range(TOP_K):
        pltpu.make_async_copy(yb_ref.at[pl.ds(0, tc)], buf.at[slot, k], sem.at[slot]).wait()
    w = w_ref[0]
    f = w[:, 2:3] * buf[slot, 0] + w[:, 3:4] * buf[slot, 1]
    y = x_ref[0] + gf_ref[0] * f
    y = y * lax.rsqrt(jnp.mean(y * y, axis=-1, keepdims=True) + RMS_EPS)
    o_ref[0] = y * g_ref[...]


def _combine_call(x1, gate_f, route, g_final, yb, dest):
    bx, sx, d = x1.shape
    tc = min(COMBINE_TILE, sx)
    nps = sx // tc
    nt = bx * nps
    dest3 = dest.reshape(nt, 1, 2 * tc)
    row = lambda t: (t // nps, t % nps, 0)
    if gate_f.shape[1] == 1:
        gate_spec = pl.BlockSpec((1, 1, d), lambda t: (t // nps, 0, 0))
    else:
        gate_spec = pl.BlockSpec((1, tc, d), row)
    return pl.pallas_call(
        _combine_body,
        grid=(nt,),
        in_specs=[pl.BlockSpec((1, 1, 2 * tc), lambda t: (t, 0, 0), memory_space=pltpu.SMEM),
                  pl.BlockSpec((1, 1, 2 * tc), lambda t: (jnp.minimum(t + 1, nt - 1), 0, 0),
                               memory_space=pltpu.SMEM),
                  pl.BlockSpec((1, tc, d), row), gate_spec,
                  pl.BlockSpec((1, tc, LANES), row),
                  pl.BlockSpec(g_final.shape, lambda t: (0, 0)),
                  pl.BlockSpec(memory_space=pl.ANY)],
        out_specs=pl.BlockSpec((1, tc, d), row),
        out_shape=jax.ShapeDtypeStruct((bx, sx, d), F32),
        scratch_shapes=[pltpu.VMEM((2, 2, tc, d), F32), pltpu.SemaphoreType.DMA((2,))],
        compiler_params=_cparams(("arbitrary",)),
        name="moe_combine",
    )(dest3, dest3, x1, gate_f, route, g_final, yb)


def _route_plan(route, n_exp, tm):
    n_tok = route.shape[0]
    n_slots = n_tok * TOP_K
    slot_e = route[:, :TOP_K].astype(jnp.int32).reshape(n_slots)
    onehot = (slot_e[:, None] == jnp.arange(n_exp, dtype=jnp.int32)[None, :]).astype(jnp.int32)
    csum = jnp.cumsum(onehot, axis=0)
    rank = jnp.sum(csum * onehot, axis=1) - 1
    counts = csum[-1]
    padded = ((counts + tm - 1) // tm) * tm
    pad_end = jnp.cumsum(padded)
    pad_start = pad_end - padded
    dest = jnp.sum(onehot * pad_start[None, :], axis=1) + rank
    n_tiles = -(-n_slots // tm) + n_exp
    n_rows = n_tiles * tm
    slot_tok = jnp.repeat(jnp.arange(n_tok, dtype=jnp.int32), TOP_K)
    row_tok = jnp.zeros((n_rows,), jnp.int32).at[dest].set(slot_tok)
    n_valid = (pad_end[-1] // tm).astype(jnp.int32)
    tile_start = jnp.arange(n_tiles, dtype=jnp.int32) * tm
    tile_e = jnp.minimum(jnp.searchsorted(pad_end, tile_start, side='right'), n_exp - 1).astype(jnp.int32)
    last_e = tile_e[jnp.maximum(n_valid - 1, 0)]
    tile_e = jnp.where(jnp.arange(n_tiles) < n_valid, tile_e, last_e)
    return dest, row_tok, tile_e, n_valid.reshape(1)


def _moe(x1, h, route, gate_f, g_final, w1, w3, w2):
    bx, sx, d = x1.shape
    n_exp = w1.shape[0]
    dest, row_tok, tile_e, n_valid = _route_plan(route.reshape(bx * sx, LANES), n_exp, MOE_ROW_TILE)
    yb = _moe_ffn_call(h.reshape(bx * sx, d), row_tok, tile_e, n_valid, w1, w3, w2)
    return _combine_call(x1, gate_f, route, g_final, yb, dest)


def _trunk(x, mods, p, cache):
    sh_m0, sc_m0, gt_m0, sh_f0, sc_f0, gt_f0 = mods[0]
    sh_m1, sc_m1, gt_m1, sh_f1, sc_f1, gt_f1 = mods[1]
    wa = p['wa']
    wb = p['wb']
    wc = p['wc']
    bx, sx, d = x.shape

    if cache is None:
        keep = min(WINDOW_B, sx)
        qkv, ka, va, kb, vb, logf, _, ft = _inproj_even_call(
            x, p['g_mix0'], sh_m0, sc_m0, p['w_in_even'], p['w_f'], p['b_f'], keep, True)
        o_a = _fox_prompt_call(qkv, ft, wa)
        o_b = _band_prompt_call(qkv, p['rel_bias'], wa, wb)
        x_l0_state = (ka, va, logf, kb, vb)
    else:
        ck, cv, clogf, bk, bv, cck, ccv, b, t = cache
        qkv, ka, va, kb, vb, logf, lft, = _inproj_even_call(
            x, p['g_mix0'], sh_m0, sc_m0, p['w_in_even'], p['w_f'], p['b_f'], sx, False)
        nh = wa // HEAD_DIM
        p_len = ck.shape[1]
        lft_new = lft.reshape(nh, b, t).transpose(1, 0, 2)
        lp = -(-(p_len + t) // 512) * 512
        lcat = jnp.concatenate([clogf.transpose(0, 2, 1), lft_new,
                                jnp.zeros((b, nh, lp - p_len - t), F32)], axis=-1)
        ft_pad = _cumsum_call(lcat)
        qkv_b = qkv.reshape(b, t, qkv.shape[-1])
        o_a = _fox_sample_call(qkv_b, ck, cv, ft_pad, wa).reshape(1, b * t, wa)
        o_b = _band_sample_call(qkv_b, bk, bv, p['rel_bias'], wb).reshape(1, b * t, wb)
        x_l0_state = (ka, va, logf, kb, vb)
    o = jnp.concatenate([o_a, o_b], axis=-1)
    x = _out_ffn_call(x, o, gt_m0, p['g_ffn0'], sh_f0, sc_f0, gt_f0,
                      p['w_out_even'], p['w1_dense'], p['w3_dense'], p['w2_dense'])

    qkv, kc, vc = _inproj_odd_call(x, p['g_mix1'], sh_m1, sc_m1, p['w_in_odd'])
    if cache is None:
        o = _diff_prompt_call(qkv, p['lam_par'], p['g_sub'], wc, p['lam_init'])
    else:
        qkv_b = qkv.reshape(b, t, qkv.shape[-1])
        o = _diff_sample_call(qkv_b, cck, ccv, p['lam_par'], p['g_sub'], wc, p['lam_init']).reshape(1, b * t, wc)
    x1, h, route = _out_router_call(x, o, gt_m1, p['g_ffn1'], sh_f1, sc_f1, p['w_out_odd'], p['w_router3'],
                                    p['n_exp'])
    y = _moe(x1, h, route, gt_f1, p['g_final'], p['w1_moe'], p['w3_moe'], p['w2_moe'])
    return y, x_l0_state, (kc, vc)


def kernel(x_prompt, x_sample, cache_a_k, cache_a_v, cache_a_logf, cache_b_k, cache_b_v, cache_c_k, cache_c_v,
           c_prompt, c_sample, w_mod, b_mod, g_mix, g_ffn, g_final, w_in_even, b_forget, rel_bias, w_out_even,
           w_in_odd, lambda_q1, lambda_k1, lambda_q2, lambda_k2, g_subln, w_out_odd, w1_dense, w3_dense, w2_dense,
           w_router, w1_moe, w3_moe, w2_moe):
    bp, sp, d = x_prompt.shape
    bs, ts, _ = x_sample.shape
    h_a = cache_a_k.shape[3]
    h_b = cache_b_k.shape[3]
    h_c = cache_c_k.shape[3]
    wa, wb, wc = h_a * HEAD_DIM, h_b * HEAD_DIM, h_c * 2 * HEAD_DIM
    n_exp = w_router.shape[-1]
    assert w_mod.shape[0] == 2, "kernel is written for the 2-layer trunk"

    we = w_in_even[0]
    w_main = jnp.concatenate([we[:, :3 * wa], we[:, 3 * wa + h_a:]], axis=1).astype(BF)
    w_f = jnp.pad(we[:, 3 * wa:3 * wa + h_a], ((0, 0), (0, LANES - h_a))).astype(BF)
    b_f = jnp.pad(b_forget[0], (0, LANES - h_a)).reshape(1, LANES).astype(F32)
    wr = jnp.pad(w_router[0], ((0, 0), (0, LANES - n_exp)))
    wr_hi = wr.astype(BF)
    wr_r1 = wr - wr_hi.astype(F32)
    wr_mid = wr_r1.astype(BF)
    wr_lo = (wr_r1 - wr_mid.astype(F32)).astype(BF)
    params = dict(
        wa=wa, wb=wb, wc=wc, n_exp=n_exp, lam_init=0.8 - 0.6 * math.exp(-0.3 * 1),
        g_mix0=g_mix[0:1], g_mix1=g_mix[1:2], g_ffn0=g_ffn[0:1], g_ffn1=g_ffn[1:2],
        g_final=g_final.reshape(1, d),
        w_in_even=w_main, w_f=w_f, b_f=b_f, rel_bias=rel_bias[0],
        w_out_even=w_out_even[0].astype(BF), w_in_odd=w_in_odd[0].astype(BF),
        lam_par=jnp.concatenate([lambda_q1, lambda_k1, lambda_q2, lambda_k2], axis=0),
        g_sub=g_subln[0:1], w_out_odd=w_out_odd[0].astype(BF),
        w1_dense=w1_dense[0].astype(BF), w3_dense=w3_dense[0].astype(BF), w2_dense=w2_dense[0].astype(BF),
        w_router3=jnp.stack([wr_hi, wr_mid, wr_lo]),
        w1_moe=w1_moe[0].astype(BF), w3_moe=w3_moe[0].astype(BF), w2_moe=w2_moe[0].astype(BF),
    )

    mod = _mod_call(jnp.concatenate([c_prompt, c_sample], axis=0), w_mod, b_mod)

    def mods_for(rows, per_row_len):
        out = []
        for l in range(2):
            parts = []
            for k in range(6):
                m = mod[l, rows, k * d:(k + 1) * d]
                if per_row_len:
                    m = jnp.repeat(m, per_row_len, axis=0).reshape(1, -1, d)
                else:
                    m = m.reshape(-1, 1, d)
                parts.append(m)
            out.append(parts)
        return out

    y_p, ev_p, od_p = _trunk(x_prompt, mods_for(slice(0, bp), 0), params, None)
    cache = (cache_a_k[0].reshape(bs, -1, wa), cache_a_v[0].reshape(bs, -1, wa), cache_a_logf[0],
             cache_b_k[0].reshape(bs, -1, wb), cache_b_v[0].reshape(bs, -1, wb),
             cache_c_k[0].reshape(bs, -1, wc), cache_c_v[0].reshape(bs, -1, wc), bs, ts)
    y_s, ev_s, od_s = _trunk(x_sample.reshape(1, bs * ts, d), mods_for(slice(bp, bp + bs), ts), params, cache)

    ka, va, logf, kb, vb = ev_p
    keep = kb.shape[1]
    out_p = (ka.reshape(1, bp, sp, h_a, HEAD_DIM), va.reshape(1, bp, sp, h_a, HEAD_DIM),
             logf.reshape(1, bp, sp, h_a),
             kb.reshape(1, bp, keep, h_b, HEAD_DIM), vb.reshape(1, bp, keep, h_b, HEAD_DIM),
             od_p[0].reshape(1, bp, sp, h_c, 2 * HEAD_DIM), od_p[1].reshape(1, bp, sp, h_c, 2 * HEAD_DIM))
    ka, va, logf, kb, vb = ev_s
    new_bk = jnp.concatenate([cache_b_k[0], kb.reshape(bs, ts, h_b, HEAD_DIM)], axis=1)[:, ts:]
    new_bv = jnp.concatenate([cache_b_v[0], vb.reshape(bs, ts, h_b, HEAD_DIM)], axis=1)[:, ts:]
    out_s = (ka.reshape(1, bs, ts, h_a, HEAD_DIM), va.reshape(1, bs, ts, h_a, HEAD_DIM),
             logf.reshape(1, bs, ts, h_a), new_bk[None], new_bv[None],
             od_s[0].reshape(1, bs, ts, h_c, 2 * HEAD_DIM), od_s[1].reshape(1, bs, ts, h_c, 2 * HEAD_DIM))
    return (y_p, y_s.reshape(bs, ts, d)) + out_p + out_s
```

```python
import functools
import math

import jax
import jax.numpy as jnp
from jax import lax
from jax.experimental import pallas as pl
from jax.experimental.pallas import tpu as pltpu

BF = jnp.bfloat16
F32 = jnp.float32

CHUNK = 64
CHUNK_SHIFT = 6
HEAD_DIM = 64
BAND_CHUNKS = 8
WINDOW_B = BAND_CHUNKS * CHUNK
MAX_REL = 128
TOP_K = 2
RMS_EPS = 1e-6
NEG_INF = -1e30
QK_SCALE = HEAD_DIM ** -0.5

LANES = 128
VMEM_LIMIT_BYTES = 56 * 1024 * 1024

ROW_TILE = 512
ATTN_TILE = 512
BAND_Q_TILE = WINDOW_B // 2
CACHE_TILE_A = 1024
CACHE_TILE_C = 512
MOE_ROW_TILE = 512
MOE_F_TILE = 1792
COMBINE_TILE = 256


def _cparams(sem):
    return pltpu.CompilerParams(dimension_semantics=sem, vmem_limit_bytes=VMEM_LIMIT_BYTES)


def _dot(a, b):
    return jnp.dot(a, b, preferred_element_type=F32)


def _dot_nt(a, b):
    return lax.dot_general(a, b, (((1,), (1,)), ((), ())), preferred_element_type=F32)


def _sigmoid(x):
    return 1.0 / (1.0 + jnp.exp(-x))


def _modulate(x, g, shift, scale):
    y = x * lax.rsqrt(jnp.mean(x * x, axis=-1, keepdims=True) + RMS_EPS)
    return (y * g) * (1.0 + scale) + shift


def _split3(x):
    hi = x.astype(BF)
    r1 = x - hi.astype(F32)
    mid = r1.astype(BF)
    lo = (r1 - mid.astype(F32)).astype(BF)
    return hi, mid, lo


def _cumsum_lanes(x, upper):
    hi, mid, lo = _split3(x)
    return _dot(hi, upper) + _dot(mid, upper) + _dot(lo, upper)


def _upper_tri(t):
    r = jnp.arange(t)
    return (r[:, None] <= r[None, :]).astype(BF)


def _osm_update(s, v, m_ref, l_ref, acc_ref, idx):
    m_prev = m_ref[idx]
    m_new = jnp.maximum(m_prev, jnp.max(s, axis=-1, keepdims=True))
    alpha = jnp.exp(m_prev - m_new)
    p = jnp.exp(s - m_new)
    l_ref[idx] = alpha * l_ref[idx] + jnp.sum(p, axis=-1, keepdims=True)
    acc_ref[idx] = alpha * acc_ref[idx] + _dot(p.astype(BF), v)
    m_ref[idx] = m_new


def _osm_init(m_ref, l_ref, acc_ref):
    m_ref[...] = jnp.full(m_ref.shape, NEG_INF, F32)
    l_ref[...] = jnp.zeros(l_ref.shape, F32)
    acc_ref[...] = jnp.zeros(acc_ref.shape, F32)


def _mod_body(c_ref, w_ref, b_ref, o_ref):
    c = c_ref[...]
    s = (c * _sigmoid(c)).astype(BF)
    o_ref[0] = _dot(s, w_ref[0].astype(BF)) + b_ref[0]


def _mod_call(c_all, w_mod, b_mod):
    depth, d, n = w_mod.shape
    r = c_all.shape[0]
    tn = 1536 if n % 1536 == 0 else n
    return pl.pallas_call(
        _mod_body,
        grid=(depth, n // tn),
        in_specs=[pl.BlockSpec((r, d), lambda l, j: (0, 0)),
                  pl.BlockSpec((1, d, tn), lambda l, j: (l, 0, j)),
                  pl.BlockSpec((1, 1, tn), lambda l, j: (l, 0, j))],
        out_specs=pl.BlockSpec((1, r, tn), lambda l, j: (l, 0, j)),
        out_shape=jax.ShapeDtypeStruct((depth, r, n), F32),
        compiler_params=_cparams(("arbitrary", "arbitrary")),
        name="mod",
    )(c_all, w_mod, b_mod.reshape(depth, 1, n))


def _mod_spec(mod_arr, tm):
    d = mod_arr.shape[-1]
    if mod_arr.shape[1] == 1:
        return pl.BlockSpec((1, 1, d), lambda b, i: (b, 0, 0))
    return pl.BlockSpec((1, tm, d), lambda b, i: (b, i, 0))


def _const_spec(arr):
    nd = arr.ndim
    return pl.BlockSpec(arr.shape, lambda b, i: (0,) * nd, pipeline_mode=pl.Buffered(1))


def _head_blocks(x, augs):
    tm, w = x.shape
    first = lax.broadcasted_iota(jnp.int32, (tm, LANES), 1) < HEAD_DIM
    out = []
    for p in range(w // LANES):
        pair = x[:, p * LANES:(p + 1) * LANES]
        out.append(jnp.where(first, pair, augs[2 * p]).astype(BF))
        out.append(jnp.where(first, pltpu.roll(pair, HEAD_DIM, 1), augs[2 * p + 1]).astype(BF))
    return out


def _lane_range_row(lo, hi, value):
    lane = lax.broadcasted_iota(jnp.int32, (1, LANES), 1)
    return jnp.where((lane >= lo) & (lane < hi), value, 0.0).astype(F32)


def _ones_rows(t):
    return jnp.where(lax.broadcasted_iota(jnp.int32, (16, t), 0) == 0, 1.0, 0.0).astype(BF)


def _log_forget(h, wf_ref, bf_ref):
    z = _dot(h, wf_ref[...]) + bf_ref[...]
    return jnp.minimum(z, 0.0) - jnp.log1p(jnp.exp(-jnp.abs(z)))


def _inproj_even_body(*refs, wa, wb, prompt):
    if prompt:
        (x_ref, g_ref, sh_ref, sc_ref, w_ref, wf_ref, bf_ref, low_ref, place_ref,
         qk_ref, vt_ref, ka_ref, va_ref, kb_ref, vb_ref, logf_ref, carry_ref) = refs
    else:
        (x_ref, g_ref, sh_ref, sc_ref, w_ref, wf_ref, bf_ref,
         qkv_ref, ka_ref, va_ref, kb_ref, vb_ref, logf_ref, lft_ref) = refs
    i = pl.program_id(1)
    h = _modulate(x_ref[0], g_ref[...], sh_ref[0], sc_ref[0]).astype(BF)
    tm = h.shape[0]
    nh = logf_ref.shape[-1]

    qa = _dot(h, w_ref[:, 0:wa])
    ka = _dot(h, w_ref[:, wa:2 * wa])
    va = _dot(h, w_ref[:, 2 * wa:3 * wa])
    o = 3 * wa
    qb = _dot(h, w_ref[:, o:o + wb])
    kb = _dot(h, w_ref[:, o + wb:o + 2 * wb])
    vb = _dot(h, w_ref[:, o + 2 * wb:o + 3 * wb])
    ka_ref[0] = ka
    va_ref[0] = va
    kb_ref[0] = kb
    vb_ref[0] = vb
    logf = _log_forget(h, wf_ref, bf_ref)
    logf_ref[0] = logf[:, :nh]

    if not prompt:
        qkv_ref[0] = jnp.concatenate(
            [(qa * QK_SCALE).astype(BF), ka.astype(BF), va.astype(BF),
             (qb * QK_SCALE).astype(BF), kb.astype(BF), vb.astype(BF)], axis=-1)
        lft_ref[0] = logf.T[:nh, :]
        return

    @pl.when(i == 0)
    def _():
        carry_ref[...] = jnp.zeros(carry_ref.shape, F32)

    low = low_ref[...]
    hi, mid, lo = _split3(logf)
    f_cum = _dot(low, hi) + _dot(low, mid) + _dot(low, lo) + carry_ref[0:1, :]
    carry_ref[...] = jnp.broadcast_to(f_cum[tm - 1:tm, :], carry_ref.shape)
    nhi, nmid, nlo = _split3(-f_cum)
    k_aug = _dot(nhi, place_ref[0]) + _dot(nmid, place_ref[1]) + _dot(nlo, place_ref[2])
    q_ones = _lane_range_row(HEAD_DIM, HEAD_DIM + 3, 1.0)
    zero = jnp.zeros((1, LANES), F32)
    nhb = wb // HEAD_DIM
    blocks = (_head_blocks(qa * QK_SCALE, [q_ones] * nh)
              + _head_blocks(ka, [k_aug[:, n * LANES:(n + 1) * LANES] for n in range(nh)])
              + _head_blocks(qb * QK_SCALE, [zero] * nhb)
              + _head_blocks(kb, [zero] * nhb))
    qk_ref[0] = jnp.concatenate(blocks, axis=-1)
    ones = _ones_rows(tm)
    vt_ref[0] = jnp.concatenate([va.T.astype(BF), ones, vb.T.astype(BF), ones], axis=0)


def _lower_tri(t):
    r = jnp.arange(t)
    return (r[:, None] >= r[None, :]).astype(BF)


def _place_matrices(nh):
    s = jnp.arange(3)[:, None, None]
    r = jnp.arange(LANES)[None, :, None]
    c = jnp.arange(nh * LANES)[None, None, :]
    return ((r < nh) & (c == r * LANES + HEAD_DIM + s)).astype(BF)


def _inproj_even_call(x, g, shift, scale, w_main, w_f, b_f, keep, prompt):
    bx, sx, d = x.shape
    tm = min(ROW_TILE, sx)
    nt = sx // tm
    n = w_main.shape[1]
    wa = wb = n // 6
    nh = wa // HEAD_DIM
    nkeep = keep // tm
    tail = lambda b, i: (b, jnp.maximum(i - (nt - nkeep), 0), 0)
    row = lambda b, i: (b, i, 0)
    in_specs = [pl.BlockSpec((1, tm, d), row), _const_spec(g), _mod_spec(shift, tm), _mod_spec(scale, tm),
                _const_spec(w_main), _const_spec(w_f), _const_spec(b_f)]
    args = [x, g, shift, scale, w_main, w_f, b_f]
    f32_specs = [pl.BlockSpec((1, tm, wa), row), pl.BlockSpec((1, tm, wa), row),
                 pl.BlockSpec((1, tm, wb), tail), pl.BlockSpec((1, tm, wb), tail),
                 pl.BlockSpec((1, tm, nh), row)]
    f32_shapes = [jax.ShapeDtypeStruct((bx, sx, wa), F32), jax.ShapeDtypeStruct((bx, sx, wa), F32),
                  jax.ShapeDtypeStruct((bx, keep, wb), F32), jax.ShapeDtypeStruct((bx, keep, wb), F32),
                  jax.ShapeDtypeStruct((bx, sx, nh), F32)]
    scratch = []
    if prompt:
        low, place = _lower_tri(tm), _place_matrices(nh)
        in_specs += [_const_spec(low), _const_spec(place)]
        args += [low, place]
        nqk = 2 * (wa + wb) // HEAD_DIM * LANES
        nvt = wa + wb + 32
        out_specs = [pl.BlockSpec((1, tm, nqk), row), pl.BlockSpec((1, nvt, tm), lambda b, i: (b, 0, i))] + f32_specs
        out_shape = [jax.ShapeDtypeStruct((bx, sx, nqk), BF), jax.ShapeDtypeStruct((bx, nvt, sx), BF)] + f32_shapes
        scratch.append(pltpu.VMEM((8, LANES), F32))
    else:
        out_specs = [pl.BlockSpec((1, tm, n), row)] + f32_specs + [pl.BlockSpec((1, nh, tm), lambda b, i: (b, 0, i))]
        out_shape = [jax.ShapeDtypeStruct((bx, sx, n), BF)] + f32_shapes + [jax.ShapeDtypeStruct((bx, nh, sx), F32)]
    return pl.pallas_call(
        functools.partial(_inproj_even_body, wa=wa, wb=wb, prompt=prompt),
        grid=(bx, nt), in_specs=in_specs, out_specs=out_specs, out_shape=out_shape,
        scratch_shapes=scratch,
        compiler_params=_cparams(("arbitrary", "arbitrary")),
        name="inproj_even",
    )(*args)


def _inproj_odd_body(*refs, wc, prompt):
    if prompt:
        x_ref, g_ref, sh_ref, sc_ref, w_ref, qk_ref, vt_ref, k_ref, v_ref = refs
    else:
        x_ref, g_ref, sh_ref, sc_ref, w_ref, qkv_ref, k_ref, v_ref = refs
    i = pl.program_id(1)
    h = _modulate(x_ref[0], g_ref[...], sh_ref[0], sc_ref[0]).astype(BF)
    tm = h.shape[0]
    q = _dot(h, w_ref[:, 0:wc])
    k = _dot(h, w_ref[:, wc:2 * wc])
    v = _dot(h, w_ref[:, 2 * wc:3 * wc])
    k_ref[0] = k
    v_ref[0] = v
    if not prompt:
        qkv_ref[0] = jnp.concatenate([(q * QK_SCALE).astype(BF), k.astype(BF), v.astype(BF)], axis=-1)
        return
    nh = wc // LANES
    pos = (i * tm + lax.broadcasted_iota(jnp.int32, (tm, LANES), 0)).astype(F32)
    pos_hi = pos.astype(BF).astype(F32)
    lane = lax.broadcasted_iota(jnp.int32, (tm, LANES), 1)
    k_aug = jnp.where(lane == HEAD_DIM, pos_hi, jnp.where(lane == HEAD_DIM + 1, pos - pos_hi, 0.0))
    q_augs = []
    for hh in range(nh):
        q_augs += [_lane_range_row(HEAD_DIM, HEAD_DIM + 2, _alibi_slope(hh, nh))] * 2
    blocks = _head_blocks(q * QK_SCALE, q_augs) + _head_blocks(k, [k_aug] * (2 * nh))
    qk_ref[0] = jnp.concatenate(blocks, axis=-1)
    vt_ref[0] = jnp.concatenate([v.T.astype(BF), _ones_rows(tm)], axis=0)


def _alibi_slope(h, nh):
    assert 8 % nh == 0, "slopes must be exact powers of two to ride in bf16"
    return 2.0 ** (-8.0 * (h + 1) / nh)


def _inproj_odd_call(x, g, shift, scale, w, prompt):
    bx, sx, d = x.shape
    tm = min(ROW_TILE, sx)
    n = w.shape[1]
    wc = n // 3
    row = lambda b, i: (b, i, 0)
    f32_specs = [pl.BlockSpec((1, tm, wc), row), pl.BlockSpec((1, tm, wc), row)]
    f32_shapes = [jax.ShapeDtypeStruct((bx, sx, wc), F32), jax.ShapeDtypeStruct((bx, sx, wc), F32)]
    if prompt:
        nqk = 2 * wc // HEAD_DIM * LANES
        out_specs = [pl.BlockSpec((1, tm, nqk), row), pl.BlockSpec((1, wc + 16, tm), lambda b, i: (b, 0, i))]
        out_shape = [jax.ShapeDtypeStruct((bx, sx, nqk), BF), jax.ShapeDtypeStruct((bx, wc + 16, sx), BF)]
    else:
        out_specs = [pl.BlockSpec((1, tm, n), row)]
        out_shape = [jax.ShapeDtypeStruct((bx, sx, n), BF)]
    return pl.pallas_call(
        functools.partial(_inproj_odd_body, wc=wc, prompt=prompt),
        grid=(bx, sx // tm),
        in_specs=[pl.BlockSpec((1, tm, d), row), _const_spec(g), _mod_spec(shift, tm), _mod_spec(scale, tm),
                  _const_spec(w)],
        out_specs=out_specs + f32_specs, out_shape=out_shape + f32_shapes,
        compiler_params=_cparams(("arbitrary", "arbitrary")),
        name="inproj_odd",
    )(x, g, shift, scale, w)


def _tflash_update(k_blk, q_blk, v_aug, mask, extra, m_ref, acc_ref, idx):
    st = _dot_nt(k_blk, q_blk)
    if extra is not None:
        st = st - extra
    if mask is not None:
        st = jnp.where(mask, st, NEG_INF)
    m_prev = m_ref[idx]
    m_new = jnp.maximum(m_prev, jnp.max(st, axis=0, keepdims=True))
    alpha = jnp.exp(m_prev - m_new)
    p = jnp.exp(st - m_new).astype(BF)
    acc_ref[idx] = alpha * acc_ref[idx] + _dot(v_aug, p)
    m_ref[idx] = m_new


def _tflash_init(m_ref, acc_ref):
    m_ref[...] = jnp.full(m_ref.shape, NEG_INF, F32)
    acc_ref[...] = jnp.zeros(acc_ref.shape, F32)


def _fox_prompt_body(q_ref, k_ref, vt_ref, o_ref, m_ref, acc_ref, *, nh):
    i = pl.program_id(1)
    j = pl.program_id(2)
    hd = HEAD_DIM

    @pl.when(j == 0)
    def _():
        _tflash_init(m_ref, acc_ref)

    def step(masked):
        q = q_ref[0]
        k = k_ref[0]
        vt = vt_ref[0]
        tq, tk = q.shape[0], k.shape[0]
        ones = vt[nh * hd:, :]
        mask = None
        if masked:
            mask = lax.broadcasted_iota(jnp.int32, (tk, tq), 0) <= lax.broadcasted_iota(jnp.int32, (tk, tq), 1)
        for h in range(nh):
            blk = slice(h * LANES, (h + 1) * LANES)
            v_aug = jnp.concatenate([vt[h * hd:(h + 1) * hd, :], ones], axis=0)
            _tflash_update(k[:, blk], q[:, blk], v_aug, mask, None, m_ref, acc_ref, h)

    @pl.when(j < i)
    def _():
        step(False)

    @pl.when(j == i)
    def _():
        step(True)

    @pl.when(j == pl.num_programs(2) - 1)
    def _():
        ot = jnp.concatenate([acc_ref[h][:hd] / acc_ref[h][hd:hd + 1] for h in range(nh)], axis=0)
        o_ref[0] = ot.T.astype(BF)


def _fox_prompt_call(qk, vt, wa):
    b, s, _ = qk.shape
    nh = wa // HEAD_DIM
    t = min(ATTN_TILE, s)
    nq = s // t
    wq = nh * LANES
    return pl.pallas_call(
        functools.partial(_fox_prompt_body, nh=nh),
        grid=(b, nq, nq),
        in_specs=[pl.BlockSpec((1, t, wq), lambda b, i, j: (b, i, 0)),
                  pl.BlockSpec((1, t, wq), lambda b, i, j: (b, jnp.minimum(j, i), 1)),
                  pl.BlockSpec((1, wa + 16, t), lambda b, i, j: (b, 0, jnp.minimum(j, i)))],
        out_specs=pl.BlockSpec((1, t, wa), lambda b, i, j: (b, i, 0)),
        out_shape=jax.ShapeDtypeStruct((b, s, wa), BF),
        scratch_shapes=[pltpu.VMEM((nh, 1, t), F32), pltpu.VMEM((nh, HEAD_DIM + 16, t), F32)],
        compiler_params=_cparams(("arbitrary", "arbitrary", "arbitrary")),
        name="fox_prompt",
    )(qk, qk, vt)


def _band_prompt_body(q_ref, k0_ref, k1_ref, k2_ref, v0_ref, v1_ref, v2_ref, bias_ref, o_ref, *, nh):
    i = pl.program_id(1)
    hd = HEAD_DIM
    q = q_ref[0]
    tq = q.shape[0]
    k = jnp.concatenate([k0_ref[0], k1_ref[0], k2_ref[0]], axis=0)
    vts = [v0_ref[0], v1_ref[0], v2_ref[0]]
    in_seq = lax.broadcasted_iota(jnp.int32, (3 * tq, tq), 0) >= (2 - i) * tq
    outs = []
    for h in range(nh):
        blk = slice(h * LANES, (h + 1) * LANES)
        st = _dot_nt(k[:, blk], q[:, blk]) + bias_ref[h]
        st = jnp.where(in_seq, st, NEG_INF)
        p = jnp.exp(st - jnp.max(st, axis=0, keepdims=True)).astype(BF)
        acc = None
        for w, vt in enumerate(vts):
            v_aug = jnp.concatenate([vt[h * hd:(h + 1) * hd, :], vt[nh * hd:, :]], axis=0)
            part = _dot(v_aug, p[w * tq:(w + 1) * tq, :])
            acc = part if acc is None else acc + part
        outs.append(acc[:hd] / acc[hd:hd + 1])
    o_ref[0] = jnp.concatenate(outs, axis=0).T.astype(BF)


def _band_bias_prompt(rel_bias, tq):
    nh = rel_bias.shape[0]
    period = 4 * tq
    t = jnp.arange(period)
    t = jnp.where(t < 3 * tq, t, t - period)
    u = rel_bias[:, jnp.clip(2 * tq - t, -MAX_REL, MAX_REL) + MAX_REL].astype(F32)
    skew = jnp.tile(u, (1, tq))[:, :tq * (period - 1)].reshape(nh, tq, period - 1)
    bias = skew[:, :, :3 * tq]
    r = jnp.arange(tq)[:, None]
    c = jnp.arange(3 * tq)[None, :]
    valid = (c // CHUNK >= r // CHUNK) & (c // CHUNK <= r // CHUNK + BAND_CHUNKS)
    return jnp.where(valid[None], bias, NEG_INF)


def _band_prompt_call(qk, vt, rel_bias, wa, wb):
    b, s, _ = qk.shape
    nh = wb // HEAD_DIM
    tq = BAND_Q_TILE
    assert s % tq == 0 and wa == wb
    bias_t = _band_bias_prompt(rel_bias, tq).transpose(0, 2, 1)
    wq = nh * LANES
    cq, ck = 2, 3

    def k_spec(back):
        return pl.BlockSpec((1, tq, wq), lambda b, i: (b, jnp.maximum(i - back, 0), ck))

    def v_spec(back):
        return pl.BlockSpec((1, wb + 16, tq), lambda b, i: (b, 1, jnp.maximum(i - back, 0)))

    return pl.pallas_call(
        functools.partial(_band_prompt_body, nh=nh),
        grid=(b, s // tq),
        in_specs=[pl.BlockSpec((1, tq, wq), lambda b, i: (b, i, cq)),
                  k_spec(2), k_spec(1), k_spec(0), v_spec(2), v_spec(1), v_spec(0),
                  _const_spec(bias_t)],
        out_specs=pl.BlockSpec((1, tq, wb), lambda b, i: (b, i, 0)),
        out_shape=jax.ShapeDtypeStruct((b, s, wb), BF),
        compiler_params=_cparams(("arbitrary", "arbitrary")),
        name="band_prompt",
    )(qk, qk, qk, qk, vt, vt, vt, bias_t)


def _lambda(lam_ref, lam_init):
    lp = lam_ref[...]
    a = jnp.sum(lp[0:1] * lp[1:2], axis=-1, keepdims=True)
    b = jnp.sum(lp[2:3] * lp[3:4], axis=-1, keepdims=True)
    return jnp.exp(a) - jnp.exp(b) + lam_init


def _diff_finalize(l_ref, acc_ref, lam_ref, gsub_ref, nh, lam_init):
    lam = _lambda(lam_ref, lam_init)
    g = gsub_ref[...]
    outs = []
    for h in range(nh):
        o = acc_ref[2 * h] / l_ref[2 * h] - lam * (acc_ref[2 * h + 1] / l_ref[2 * h + 1])
        o = o * lax.rsqrt(jnp.mean(o * o, axis=-1, keepdims=True) + RMS_EPS)
        outs.append(((o * g) * (1.0 - lam_init)).astype(BF))
    return jnp.concatenate(outs, axis=-1)


def _diff_prompt_body(q_ref, k_ref, vt_ref, lam_ref, gcol_ref, o_ref, m_ref, acc_ref, *, nh, lam_init):
    i = pl.program_id(1)
    j = pl.program_id(2)
    dv = 2 * HEAD_DIM

    @pl.when(j == 0)
    def _():
        _tflash_init(m_ref, acc_ref)

    def step(masked):
        q = q_ref[0]
        k = k_ref[0]
        vt = vt_ref[0]
        tq, tk = q.shape[0], k.shape[0]
        ones = vt[nh * dv:, :]
        mask = None
        if masked:
            key = lax.broadcasted_iota(jnp.int32, (tk, tq), 0)
            qry = lax.broadcasted_iota(jnp.int32, (tk, tq), 1)
            mask = (key >> CHUNK_SHIFT) <= (qry >> CHUNK_SHIFT)
            ahead = jnp.maximum(key - qry, 0).astype(F32)
        for h in range(nh):
            v_aug = jnp.concatenate([vt[h * dv:(h + 1) * dv, :], ones], axis=0)
            extra = (2.0 * _alibi_slope(h, nh)) * ahead if masked else None
            for u in range(2):
                n = 2 * h + u
                blk = slice(n * LANES, (n + 1) * LANES)
                _tflash_update(k[:, blk], q[:, blk], v_aug, mask, extra, m_ref, acc_ref, n)

    @pl.when(j < i)
    def _():
        step(False)

    @pl.when(j == i)
    def _():
        step(True)

    @pl.when(j == pl.num_programs(2) - 1)
    def _():
        lam = _lambda(lam_ref, lam_init)
        g = gcol_ref[...]
        outs = []
        for h in range(nh):
            a1 = acc_ref[2 * h]
            a2 = acc_ref[2 * h + 1]
            o = a1[:dv] / a1[dv:dv + 1] - lam * (a2[:dv] / a2[dv:dv + 1])
            o = o * lax.rsqrt(jnp.mean(o * o, axis=0, keepdims=True) + RMS_EPS)
            outs.append((o * g) * (1.0 - lam_init))
        o_ref[0] = jnp.concatenate(outs, axis=0).T.astype(BF)


def _diff_prompt_call(qk, vt, lam_par, g_col, wc, lam_init):
    b, s, _ = qk.shape
    nh = wc // (2 * HEAD_DIM)
    t = min(ATTN_TILE, s)
    nq = s // t
    wq = 2 * nh * LANES
    return pl.pallas_call(
        functools.partial(_diff_prompt_body, nh=nh, lam_init=lam_init),
        grid=(b, nq, nq),
        in_specs=[pl.BlockSpec((1, t, wq), lambda b, i, j: (b, i, 0)),
                  pl.BlockSpec((1, t, wq), lambda b, i, j: (b, jnp.minimum(j, i), 1)),
                  pl.BlockSpec((1, wc + 16, t), lambda b, i, j: (b, 0, jnp.minimum(j, i))),
                  pl.BlockSpec(lam_par.shape, lambda b, i, j: (0, 0)),
                  pl.BlockSpec(g_col.shape, lambda b, i, j: (0, 0))],
        out_specs=pl.BlockSpec((1, t, wc), lambda b, i, j: (b, i, 0)),
        out_shape=jax.ShapeDtypeStruct((b, s, wc), BF),
        scratch_shapes=[pltpu.VMEM((2 * nh, 1, t), F32), pltpu.VMEM((2 * nh, 2 * HEAD_DIM + 16, t), F32)],
        compiler_params=_cparams(("arbitrary", "arbitrary", "arbitrary")),
        name="diff_prompt",
    )(qk, qk, vt, lam_par, g_col)


def _cumsum_body(x_ref, up_ref, o_ref, carry_ref):
    @pl.when(pl.program_id(1) == 0)
    def _():
        carry_ref[...] = jnp.zeros(carry_ref.shape, F32)
    ft = _cumsum_lanes(x_ref[0], up_ref[...]) + carry_ref[:, 0:1]
    o_ref[0] = ft
    carry_ref[...] = jnp.broadcast_to(ft[:, -1:], carry_ref.shape)


def _cumsum_call(x):
    b, nh, length = x.shape
    t = 512
    up = _upper_tri(t)
    return pl.pallas_call(
        _cumsum_body,
        grid=(b, length // t),
        in_specs=[pl.BlockSpec((1, nh, t), lambda b, j: (b, 0, j)), pl.BlockSpec((t, t), lambda b, j: (0, 0))],
        out_specs=pl.BlockSpec((1, nh, t), lambda b, j: (b, 0, j)),
        out_shape=jax.ShapeDtypeStruct(x.shape, F32),
        scratch_shapes=[pltpu.VMEM((nh, LANES), F32)],
        compiler_params=_cparams(("arbitrary", "arbitrary")),
        name="logf_cumsum",
    )(x, up)


def _fox_sample_body(q_ref, kn_ref, vn_ref, ck_ref, cv_ref, ftc_ref, ftn_ref, o_ref, m_ref, l_ref, acc_ref, *, nh):
    j = pl.program_id(1)
    hd = HEAD_DIM
    q = q_ref[0]
    t = q.shape[0]

    @pl.when(j == 0)
    def _():
        _osm_init(m_ref, l_ref, acc_ref)

    k = ck_ref[0].astype(BF)
    v = cv_ref[0].astype(BF)
    ft = ftc_ref[0]
    for h in range(nh):
        sl = slice(h * hd, (h + 1) * hd)
        s = _dot_nt(q[:, sl], k[:, sl]) - ft[h:h + 1, :]
        _osm_update(s, v[:, sl], m_ref, l_ref, acc_ref, h)

    @pl.when(j == pl.num_programs(1) - 1)
    def _():
        kn = kn_ref[0]
        vn = vn_ref[0]
        ftn = ftn_ref[0][:, :t]
        visible = lax.broadcasted_iota(jnp.int32, (t, t), 1) <= lax.broadcasted_iota(jnp.int32, (t, t), 0)
        outs = []
        for h in range(nh):
            sl = slice(h * hd, (h + 1) * hd)
            s = _dot_nt(q[:, sl], kn[:, sl]) - ftn[h:h + 1, :]
            s = jnp.where(visible, s, NEG_INF)
            _osm_update(s, vn[:, sl], m_ref, l_ref, acc_ref, h)
            outs.append((acc_ref[h] / l_ref[h]).astype(BF))
        o_ref[0] = jnp.concatenate(outs, axis=-1)


def _fox_sample_call(qkv, cache_k, cache_v, ft_pad, wa):
    b, t, _ = qkv.shape
    p_len = cache_k.shape[1]
    nh = wa // HEAD_DIM
    tk = min(CACHE_TILE_A, p_len)
    assert p_len % tk == 0 and p_len % LANES == 0 and t <= LANES
    return pl.pallas_call(
        functools.partial(_fox_sample_body, nh=nh),
        grid=(b, p_len // tk),
        in_specs=[pl.BlockSpec((1, t, wa), lambda b, j: (b, 0, 0)),
                  pl.BlockSpec((1, t, wa), lambda b, j: (b, 0, 1)),
                  pl.BlockSpec((1, t, wa), lambda b, j: (b, 0, 2)),
                  pl.BlockSpec((1, tk, wa), lambda b, j: (b, j, 0)),
                  pl.BlockSpec((1, tk, wa), lambda b, j: (b, j, 0)),
                  pl.BlockSpec((1, nh, tk), lambda b, j: (b, 0, j)),
                  pl.BlockSpec((1, nh, LANES), lambda b, j: (b, 0, p_len // LANES))],
        out_specs=pl.BlockSpec((1, t, wa), lambda b, j: (b, 0, 0)),
        out_shape=jax.ShapeDtypeStruct((b, t, wa), BF),
        scratch_shapes=[pltpu.VMEM((nh, t, 1), F32), pltpu.VMEM((nh, t, 1), F32),
                        pltpu.VMEM((nh, t, HEAD_DIM), F32)],
        compiler_params=_cparams(("arbitrary", "arbitrary")),
        name="fox_sample",
    )(qkv, qkv, qkv, cache_k, cache_v, ft_pad, ft_pad)


def _band_sample_body(q_ref, kn_ref, vn_ref, bk_ref, bv_ref, bias_b_ref, bias_n_ref, o_ref, *, nh):
    hd = HEAD_DIM
    q = q_ref[0]
    kn = kn_ref[0]
    vn = vn_ref[0]
    kb = bk_ref[0].astype(BF)
    vb = bv_ref[0].astype(BF)
    outs = []
    for h in range(nh):
        sl = slice(h * hd, (h + 1) * hd)
        sb = _dot_nt(q[:, sl], kb[:, sl]) + bias_b_ref[h]
        sn = _dot_nt(q[:, sl], kn[:, sl]) + bias_n_ref[h]
        m = jnp.maximum(jnp.max(sb, axis=-1, keepdims=True), jnp.max(sn, axis=-1, keepdims=True))
        pb = jnp.exp(sb - m)
        pn = jnp.exp(sn - m)
        l = jnp.sum(pb, axis=-1, keepdims=True) + jnp.sum(pn, axis=-1, keepdims=True)
        o = _dot(pb.astype(BF), vb[:, sl]) + _dot(pn.astype(BF), vn[:, sl])
        outs.append((o / l).astype(BF))
    o_ref[0] = jnp.concatenate(outs, axis=-1)


def _band_sample_call(qkv, buf_k, buf_v, rel_bias, wb):
    b, t, _ = qkv.shape
    lb = buf_k.shape[1]
    nh = wb // HEAD_DIM
    k_pos = jnp.concatenate([jnp.arange(-lb, 0), jnp.arange(t)])
    rel = jnp.arange(t)[:, None] - k_pos[None, :]
    bias = rel_bias[:, jnp.clip(rel, -MAX_REL, MAX_REL) + MAX_REL].astype(F32)
    bias_b, bias_n = bias[:, :, :lb], bias[:, :, lb:]
    return pl.pallas_call(
        functools.partial(_band_sample_body, nh=nh),
        grid=(b,),
        in_specs=[pl.BlockSpec((1, t, wb), lambda b: (b, 0, 3)),
                  pl.BlockSpec((1, t, wb), lambda b: (b, 0, 4)),
                  pl.BlockSpec((1, t, wb), lambda b: (b, 0, 5)),
                  pl.BlockSpec((1, lb, wb), lambda b: (b, 0, 0)),
                  pl.BlockSpec((1, lb, wb), lambda b: (b, 0, 0)),
                  pl.BlockSpec(bias_b.shape, lambda b: (0, 0, 0)),
                  pl.BlockSpec(bias_n.shape, lambda b: (0, 0, 0))],
        out_specs=pl.BlockSpec((1, t, wb), lambda b: (b, 0, 0)),
        out_shape=jax.ShapeDtypeStruct((b, t, wb), BF),
        compiler_params=_cparams(("arbitrary",)),
        name="band_sample",
    )(qkv, qkv, qkv, buf_k, buf_v, bias_b, bias_n)


def _diff_sample_body(q_ref, kn_ref, vn_ref, ck_ref, cv_ref, lam_ref, gsub_ref, o_ref, m_ref, l_ref, acc_ref,
                      *, nh, lam_init, p_len):
    j = pl.program_id(1)
    hd = HEAD_DIM
    q = q_ref[0]
    t = q.shape[0]

    @pl.when(j == 0)
    def _():
        _osm_init(m_ref, l_ref, acc_ref)

    def attend(k, v, dist):
        for h in range(nh):
            slope = 2.0 ** (-8.0 * (h + 1) / nh)
            bias = slope * dist
            vh = v[:, 2 * h * hd:(2 * h + 2) * hd]
            for u in range(2):
                sl = slice((2 * h + u) * hd, (2 * h + u + 1) * hd)
                s = _dot_nt(q[:, sl], k[:, sl]) - bias
                _osm_update(s, vh, m_ref, l_ref, acc_ref, 2 * h + u)

    tk = ck_ref.shape[1]
    row = lax.broadcasted_iota(jnp.int32, (t, tk), 0)
    col = lax.broadcasted_iota(jnp.int32, (t, tk), 1)
    attend(ck_ref[0].astype(BF), cv_ref[0].astype(BF), (p_len + row - j * tk - col).astype(F32))

    @pl.when(j == pl.num_programs(1) - 1)
    def _():
        rown = lax.broadcasted_iota(jnp.int32, (t, t), 0)
        coln = lax.broadcasted_iota(jnp.int32, (t, t), 1)
        attend(kn_ref[0], vn_ref[0], jnp.abs(rown - coln).astype(F32))
        o_ref[0] = _diff_finalize(l_ref, acc_ref, lam_ref, gsub_ref, nh, lam_init)


def _diff_sample_call(qkv, cache_k, cache_v, lam_par, g_sub, wc, lam_init):
    b, t, _ = qkv.shape
    p_len = cache_k.shape[1]
    assert (p_len // CHUNK) * CHUNK == p_len and t <= CHUNK
    nh = wc // (2 * HEAD_DIM)
    tk = min(CACHE_TILE_C, p_len)
    return pl.pallas_call(
        functools.partial(_diff_sample_body, nh=nh, lam_init=lam_init, p_len=p_len),
        grid=(b, p_len // tk),
        in_specs=[pl.BlockSpec((1, t, wc), lambda b, j: (b, 0, 0)),
                  pl.BlockSpec((1, t, wc), lambda b, j: (b, 0, 1)),
                  pl.BlockSpec((1, t, wc), lambda b, j: (b, 0, 2)),
                  pl.BlockSpec((1, tk, wc), lambda b, j: (b, j, 0)),
                  pl.BlockSpec((1, tk, wc), lambda b, j: (b, j, 0)),
                  pl.BlockSpec(lam_par.shape, lambda b, j: (0, 0)),
                  pl.BlockSpec(g_sub.shape, lambda b, j: (0, 0))],
        out_specs=pl.BlockSpec((1, t, wc), lambda b, j: (b, 0, 0)),
        out_shape=jax.ShapeDtypeStruct((b, t, wc), BF),
        scratch_shapes=[pltpu.VMEM((2 * nh, t, 1), F32), pltpu.VMEM((2 * nh, t, 1), F32),
                        pltpu.VMEM((2 * nh, t, 2 * HEAD_DIM), F32)],
        compiler_params=_cparams(("arbitrary", "arbitrary")),
        name="diff_sample",
    )(qkv, qkv, qkv, cache_k, cache_v, lam_par, g_sub)


def _out_ffn_body(x_ref, o_ref, gm_ref, g_ref, sh_ref, sc_ref, gf_ref, wo_ref, w1_ref, w3_ref, w2_ref, y_ref,
                  *, f_chunk):
    x1 = x_ref[0] + gm_ref[0] * _dot(o_ref[0], wo_ref[...])
    h = _modulate(x1, g_ref[...], sh_ref[0], sc_ref[0]).astype(BF)
    f_total = w1_ref.shape[1]
    f = None
    for c0 in range(0, f_total, f_chunk):
        a = _dot(h, w1_ref[:, c0:c0 + f_chunk])
        g = _dot(h, w3_ref[:, c0:c0 + f_chunk])
        u = ((a * _sigmoid(a)) * g).astype(BF)
        part = _dot(u, w2_ref[c0:c0 + f_chunk, :])
        f = part if f is None else f + part
    y_ref[0] = x1 + gf_ref[0] * f


def _out_ffn_call(x, o, gate_m, g, shift, scale, gate_f, wo, w1, w3, w2):
    bx, sx, d = x.shape
    tm = min(ROW_TILE, sx)
    f_total = w1.shape[1]
    f_chunk = f_total // 2 if (f_total // 2) % LANES == 0 else f_total
    row = lambda b, i: (b, i, 0)
    return pl.pallas_call(
        functools.partial(_out_ffn_body, f_chunk=f_chunk),
        grid=(bx, sx // tm),
        in_specs=[pl.BlockSpec((1, tm, d), row), pl.BlockSpec((1, tm, o.shape[-1]), row),
                  _mod_spec(gate_m, tm), _const_spec(g), _mod_spec(shift, tm), _mod_spec(scale, tm),
                  _mod_spec(gate_f, tm), _const_spec(wo), _const_spec(w1), _const_spec(w3), _const_spec(w2)],
        out_specs=pl.BlockSpec((1, tm, d), row),
        out_shape=jax.ShapeDtypeStruct((bx, sx, d), F32),
        compiler_params=_cparams(("arbitrary", "arbitrary")),
        name="out_ffn",
    )(x, o, gate_m, g, shift, scale, gate_f, wo, w1, w3, w2)


def _out_router_body(x_ref, o_ref, gm_ref, g_ref, sh_ref, sc_ref, wo_ref, wr_ref, x1_ref, h_ref, route_ref,
                     *, n_exp):
    x1 = x_ref[0] + gm_ref[0] * _dot(o_ref[0], wo_ref[...])
    x1_ref[0] = x1
    h = _modulate(x1, g_ref[...], sh_ref[0], sc_ref[0])
    hb = h.astype(BF)
    h_ref[0] = hb.astype(F32)
    logits = _dot(hb, wr_ref[0]) + _dot(hb, wr_ref[1]) + _dot(hb, wr_ref[2])
    col = lax.broadcasted_iota(jnp.int32, logits.shape, 1)
    lg = jnp.where(col < n_exp, logits, NEG_INF)
    e = jnp.exp(lg - jnp.max(lg, axis=-1, keepdims=True))
    probs = e / jnp.sum(e, axis=-1, keepdims=True)
    p1 = jnp.max(probs, axis=-1, keepdims=True)
    i1 = jnp.min(jnp.where(probs == p1, col, LANES), axis=-1, keepdims=True)
    rest = jnp.where(col == i1, -1.0, probs)
    p2 = jnp.max(rest, axis=-1, keepdims=True)
    i2 = jnp.min(jnp.where(rest == p2, col, LANES), axis=-1, keepdims=True)
    tot = p1 + p2
    route = jnp.where(col == 0, i1.astype(F32),
                      jnp.where(col == 1, i2.astype(F32),
                                jnp.where(col == 2, p1 / tot, jnp.where(col == 3, p2 / tot, 0.0))))
    route_ref[0] = route


def _out_router_call(x, o, gate_m, g, shift, scale, wo, wr3, n_exp):
    bx, sx, d = x.shape
    tm = min(ROW_TILE, sx)
    row = lambda b, i: (b, i, 0)
    return pl.pallas_call(
        functools.partial(_out_router_body, n_exp=n_exp),
        grid=(bx, sx // tm),
        in_specs=[pl.BlockSpec((1, tm, d), row), pl.BlockSpec((1, tm, o.shape[-1]), row),
                  _mod_spec(gate_m, tm), _const_spec(g), _mod_spec(shift, tm), _mod_spec(scale, tm),
                  _const_spec(wo), _const_spec(wr3)],
        out_specs=[pl.BlockSpec((1, tm, d), row), pl.BlockSpec((1, tm, d), row),
                   pl.BlockSpec((1, tm, LANES), row)],
        out_shape=[jax.ShapeDtypeStruct((bx, sx, d), F32), jax.ShapeDtypeStruct((bx, sx, d), F32),
                   jax.ShapeDtypeStruct((bx, sx, LANES), F32)],
        compiler_params=_cparams(("arbitrary", "arbitrary")),
        name="out_router",
    )(x, o, gate_m, g, shift, scale, wo, wr3)


def _moe_ffn_body(te_ref, nv_ref, idx_ref, idx_next_ref, h_ref, w1_ref, w3_ref, w2_ref, y_ref, xbuf, xb16, sem):
    t = pl.program_id(0)
    j = pl.program_id(1)
    nv = nv_ref[0]
    tm = xbuf.shape[1]
    slot = lax.rem(t, 2)

    def issue(rows_ref, s):
        def body(r, c):
            pltpu.make_async_copy(h_ref.at[pl.ds(rows_ref[0, 0, r], 1)], xbuf.at[s, pl.ds(r, 1)], sem.at[s]).start()
            return c
        lax.fori_loop(0, tm, body, 0, unroll=8)

    @pl.when((j == 0) & (t == 0))
    def _():
        issue(idx_ref, 0)

    @pl.when((j == 0) & (t + 1 < nv))
    def _():
        issue(idx_next_ref, 1 - slot)

    @pl.when((j == 0) & (t < nv))
    def _():
        pltpu.make_async_copy(h_ref.at[pl.ds(0, tm)], xbuf.at[slot], sem.at[slot]).wait()
        xb16[...] = xbuf[slot].astype(BF)

    @pl.when(t < nv)
    def _():
        xb = xb16[...]
        a = _dot(xb, w1_ref[0])
        g = _dot(xb, w3_ref[0])
        u = ((a * _sigmoid(a)) * g).astype(BF)
        part = _dot(u, w2_ref[0])

        @pl.when(j == 0)
        def _():
            y_ref[...] = part

        @pl.when(j > 0)
        def _():
            y_ref[...] += part

    @pl.when(t >= nv)
    def _():
        y_ref[...] = jnp.zeros(y_ref.shape, F32)


def _moe_ffn_call(h, row_tok, tile_e, n_valid, w1, w3, w2):
    n_rows = row_tok.shape[0]
    d = h.shape[1]
    f_total = w1.shape[2]
    tf = MOE_F_TILE if f_total % MOE_F_TILE == 0 else f_total
    nf = f_total // tf
    tm = MOE_ROW_TILE
    n_tiles = n_rows // tm
    idx3 = row_tok.reshape(n_tiles, 1, tm)

    def fcol(t, j, te, nv):
        return jnp.where(t < nv[0], j, nf - 1)

    grid_spec = pltpu.PrefetchScalarGridSpec(
        num_scalar_prefetch=2,
        grid=(n_tiles, nf),
        in_specs=[pl.BlockSpec((1, 1, tm), lambda t, j, te, nv: (t, 0, 0), memory_space=pltpu.SMEM),
                  pl.BlockSpec((1, 1, tm), lambda t, j, te, nv: (jnp.minimum(t + 1, n_tiles - 1), 0, 0),
                               memory_space=pltpu.SMEM),
                  pl.BlockSpec(memory_space=pl.ANY),
                  pl.BlockSpec((1, d, tf), lambda t, j, te, nv: (te[t], 0, fcol(t, j, te, nv))),
                  pl.BlockSpec((1, d, tf), lambda t, j, te, nv: (te[t], 0, fcol(t, j, te, nv))),
                  pl.BlockSpec((1, tf, d), lambda t, j, te, nv: (te[t], fcol(t, j, te, nv), 0))],
        out_specs=pl.BlockSpec((tm, d), lambda t, j, te, nv: (t, 0)),
        scratch_shapes=[pltpu.VMEM((2, tm, d), F32), pltpu.VMEM((tm, d), BF), pltpu.SemaphoreType.DMA((2,))],
    )
    return pl.pallas_call(
        _moe_ffn_body,
        grid_spec=grid_spec,
        out_shape=jax.ShapeDtypeStruct((n_rows, d), F32),
        compiler_params=_cparams(("arbitrary", "arbitrary")),
        name="moe_ffn",
    )(tile_e, n_valid, idx3, idx3, h, w1, w3, w2)


def _combine_body(d0_ref, d1_ref, x_ref, gf_ref, w_ref, g_ref, yb_ref, o_ref, buf, sem):
    t = pl.program_id(0)
    nt = pl.num_programs(0)
    tc = x_ref.shape[1]

    def row_copy(idx_ref, r, slot, k):
        return pltpu.make_async_copy(yb_ref.at[pl.ds(idx_ref[0, 0, 2 * r + k], 1)],
                                     buf.at[slot, k, pl.ds(r, 1)], sem.at[slot])

    def issue(idx_ref, slot):
        def body(r, c):
            row_copy(idx_ref, r, slot, 0).start()
            row_copy(idx_ref, r, slot, 1).start()
            return c
        lax.fori_loop(0, tc, body, 0, unroll=4)

    slot = lax.rem(t, 2)

    @pl.when(t == 0)
    def _():
        issue(d0_ref, 0)

    @pl.when(t + 1 < nt)
    def _():
        issue(d1_ref, 1 - slot)

    for k in range(TOP_K):
        pltpu.make_async_copy(yb_ref.at[pl.ds(0, tc)], buf.at[slot, k], sem.at[slot]).wait()
    w = w_ref[0]
    f = w[:, 2:3] * buf[slot, 0] + w[:, 3:4] * buf[slot, 1]
    y = x_ref[0] + gf_ref[0] * f
    y = y * lax.rsqrt(jnp.mean(y * y, axis=-1, keepdims=True) + RMS_EPS)
    o_ref[0] = y * g_ref[...]


def _combine_call(x1, gate_f, route, g_final, yb, dest):
    bx, sx, d = x1.shape
    tc = min(COMBINE_TILE, sx)
    nps = sx // tc
    nt = bx * nps
    dest3 = dest.reshape(nt, 1, 2 * tc)
    row = lambda t: (t // nps, t % nps, 0)
    if gate_f.shape[1] == 1:
        gate_spec = pl.BlockSpec((1, 1, d), lambda t: (t // nps, 0, 0))
    else:
        gate_spec = pl.BlockSpec((1, tc, d), row)
    return pl.pallas_call(
        _combine_body,
        grid=(nt,),
        in_specs=[pl.BlockSpec((1, 1, 2 * tc), lambda t: (t, 0, 0), memory_space=pltpu.SMEM),
                  pl.BlockSpec((1, 1, 2 * tc), lambda t: (jnp.minimum(t + 1, nt - 1), 0, 0),
                               memory_space=pltpu.SMEM),
                  pl.BlockSpec((1, tc, d), row), gate_spec,
                  pl.BlockSpec((1, tc, LANES), row),
                  pl.BlockSpec(g_final.shape, lambda t: (0, 0)),
                  pl.BlockSpec(memory_space=pl.ANY)],
        out_specs=pl.BlockSpec((1, tc, d), row),
        out_shape=jax.ShapeDtypeStruct((bx, sx, d), F32),
        scratch_shapes=[pltpu.VMEM((2, 2, tc, d), F32), pltpu.SemaphoreType.DMA((2,))],
        compiler_params=_cparams(("arbitrary",)),
        name="moe_combine",
    )(dest3, dest3, x1, gate_f, route, g_final, yb)


def _route_plan(route, n_exp, tm):
    n_tok = route.shape[0]
    n_slots = n_tok * TOP_K
    slot_e = route[:, :TOP_K].astype(jnp.int32).reshape(n_slots)
    onehot = (slot_e[:, None] == jnp.arange(n_exp, dtype=jnp.int32)[None, :]).astype(jnp.int32)
    csum = jnp.cumsum(onehot, axis=0)
    rank = jnp.sum(csum * onehot, axis=1) - 1
    counts = csum[-1]
    padded = ((counts + tm - 1) // tm) * tm
    pad_end = jnp.cumsum(padded)
    pad_start = pad_end - padded
    dest = jnp.sum(onehot * pad_start[None, :], axis=1) + rank
    n_tiles = -(-n_slots // tm) + n_exp
    n_rows = n_tiles * tm
    slot_tok = jnp.repeat(jnp.arange(n_tok, dtype=jnp.int32), TOP_K)
    row_tok = jnp.zeros((n_rows,), jnp.int32).at[dest].set(slot_tok)
    n_valid = (pad_end[-1] // tm).astype(jnp.int32)
    tile_start = jnp.arange(n_tiles, dtype=jnp.int32) * tm
    tile_e = jnp.minimum(jnp.searchsorted(pad_end, tile_start, side='right'), n_exp - 1).astype(jnp.int32)
    last_e = tile_e[jnp.maximum(n_valid - 1, 0)]
    tile_e = jnp.where(jnp.arange(n_tiles) < n_valid, tile_e, last_e)
    return dest, row_tok, tile_e, n_valid.reshape(1)


def _moe(x1, h, route, gate_f, g_final, w1, w3, w2):
    bx, sx, d = x1.shape
    n_exp = w1.shape[0]
    dest, row_tok, tile_e, n_valid = _route_plan(route.reshape(bx * sx, LANES), n_exp, MOE_ROW_TILE)
    yb = _moe_ffn_call(h.reshape(bx * sx, d), row_tok, tile_e, n_valid, w1, w3, w2)
    return _combine_call(x1, gate_f, route, g_final, yb, dest)


def _trunk(x, mods, p, cache):
    sh_m0, sc_m0, gt_m0, sh_f0, sc_f0, gt_f0 = mods[0]
    sh_m1, sc_m1, gt_m1, sh_f1, sc_f1, gt_f1 = mods[1]
    wa = p['wa']
    wb = p['wb']
    wc = p['wc']
    bx, sx, d = x.shape

    if cache is None:
        keep = min(WINDOW_B, sx)
        qk, vt, ka, va, kb, vb, logf = _inproj_even_call(
            x, p['g_mix0'], sh_m0, sc_m0, p['w_in_even'], p['w_f'], p['b_f'], keep, True)
        o_a = _fox_prompt_call(qk, vt, wa)
        o_b = _band_prompt_call(qk, vt, p['rel_bias'], wa, wb)
        x_l0_state = (ka, va, logf, kb, vb)
    else:
        ck, cv, clogf, bk, bv, cck, ccv, b, t = cache
        qkv, ka, va, kb, vb, logf, lft, = _inproj_even_call(
            x, p['g_mix0'], sh_m0, sc_m0, p['w_in_even'], p['w_f'], p['b_f'], sx, False)
        nh = wa // HEAD_DIM
        p_len = ck.shape[1]
        lft_new = lft.reshape(nh, b, t).transpose(1, 0, 2)
        lp = -(-(p_len + t) // 512) * 512
        lcat = jnp.concatenate([clogf.transpose(0, 2, 1), lft_new,
                                jnp.zeros((b, nh, lp - p_len - t), F32)], axis=-1)
        ft_pad = _cumsum_call(lcat)
        qkv_b = qkv.reshape(b, t, qkv.shape[-1])
        o_a = _fox_sample_call(qkv_b, ck, cv, ft_pad, wa).reshape(1, b * t, wa)
        o_b = _band_sample_call(qkv_b, bk, bv, p['rel_bias'], wb).reshape(1, b * t, wb)
        x_l0_state = (ka, va, logf, kb, vb)
    o = jnp.concatenate([o_a, o_b], axis=-1)
    x = _out_ffn_call(x, o, gt_m0, p['g_ffn0'], sh_f0, sc_f0, gt_f0,
                      p['w_out_even'], p['w1_dense'], p['w3_dense'], p['w2_dense'])

    if cache is None:
        qk, vt, kc, vc = _inproj_odd_call(x, p['g_mix1'], sh_m1, sc_m1, p['w_in_odd'], True)
        o = _diff_prompt_call(qk, vt, p['lam_par'], p['g_sub'].reshape(-1, 1), wc, p['lam_init'])
    else:
        qkv, kc, vc = _inproj_odd_call(x, p['g_mix1'], sh_m1, sc_m1, p['w_in_odd'], False)
        qkv_b = qkv.reshape(b, t, qkv.shape[-1])
        o = _diff_sample_call(qkv_b, cck, ccv, p['lam_par'], p['g_sub'], wc, p['lam_init']).reshape(1, b * t, wc)
    x1, h, route = _out_router_call(x, o, gt_m1, p['g_ffn1'], sh_f1, sc_f1, p['w_out_odd'], p['w_router3'],
                                    p['n_exp'])
    y = _moe(x1, h, route, gt_f1, p['g_final'], p['w1_moe'], p['w3_moe'], p['w2_moe'])
    return y, x_l0_state, (kc, vc)


def kernel(x_prompt, x_sample, cache_a_k, cache_a_v, cache_a_logf, cache_b_k, cache_b_v, cache_c_k, cache_c_v,
           c_prompt, c_sample, w_mod, b_mod, g_mix, g_ffn, g_final, w_in_even, b_forget, rel_bias, w_out_even,
           w_in_odd, lambda_q1, lambda_k1, lambda_q2, lambda_k2, g_subln, w_out_odd, w1_dense, w3_dense, w2_dense,
           w_router, w1_moe, w3_moe, w2_moe):
    bp, sp, d = x_prompt.shape
    bs, ts, _ = x_sample.shape
    h_a = cache_a_k.shape[3]
    h_b = cache_b_k.shape[3]
    h_c = cache_c_k.shape[3]
    wa, wb, wc = h_a * HEAD_DIM, h_b * HEAD_DIM, h_c * 2 * HEAD_DIM
    n_exp = w_router.shape[-1]
    assert w_mod.shape[0] == 2, "kernel is written for the 2-layer trunk"

    we = w_in_even[0]
    w_main = jnp.concatenate([we[:, :3 * wa], we[:, 3 * wa + h_a:]], axis=1).astype(BF)
    w_f = jnp.pad(we[:, 3 * wa:3 * wa + h_a], ((0, 0), (0, LANES - h_a))).astype(BF)
    b_f = jnp.pad(b_forget[0], (0, LANES - h_a)).reshape(1, LANES).astype(F32)
    wr = jnp.pad(w_router[0], ((0, 0), (0, LANES - n_exp)))
    wr_hi = wr.astype(BF)
    wr_r1 = wr - wr_hi.astype(F32)
    wr_mid = wr_r1.astype(BF)
    wr_lo = (wr_r1 - wr_mid.astype(F32)).astype(BF)
    params = dict(
        wa=wa, wb=wb, wc=wc, n_exp=n_exp, lam_init=0.8 - 0.6 * math.exp(-0.3 * 1),
        g_mix0=g_mix[0:1], g_mix1=g_mix[1:2], g_ffn0=g_ffn[0:1], g_ffn1=g_ffn[1:2],
        g_final=g_final.reshape(1, d),
        w_in_even=w_main, w_f=w_f, b_f=b_f, rel_bias=rel_bias[0],
        w_out_even=w_out_even[0].astype(BF), w_in_odd=w_in_odd[0].astype(BF),
        lam_par=jnp.concatenate([lambda_q1, lambda_k1, lambda_q2, lambda_k2], axis=0),
        g_sub=g_subln[0:1], w_out_odd=w_out_odd[0].astype(BF),
        w1_dense=w1_dense[0].astype(BF), w3_dense=w3_dense[0].astype(BF), w2_dense=w2_dense[0].astype(BF),
        w_router3=jnp.stack([wr_hi, wr_mid, wr_lo]),
        w1_moe=w1_moe[0].astype(BF), w3_moe=w3_moe[0].astype(BF), w2_moe=w2_moe[0].astype(BF),
    )

    mod = _mod_call(jnp.concatenate([c_prompt, c_sample], axis=0), w_mod, b_mod)

    def mods_for(rows, per_row_len):
        out = []
        for l in range(2):
            parts = []
            for k in range(6):
                m = mod[l, rows, k * d:(k + 1) * d]
                if per_row_len:
                    m = jnp.repeat(m, per_row_len, axis=0).reshape(1, -1, d)
                else:
                    m = m.reshape(-1, 1, d)
                parts.append(m)
            out.append(parts)
        return out

    y_p, ev_p, od_p = _trunk(x_prompt, mods_for(slice(0, bp), 0), params, None)
    cache = (cache_a_k[0].reshape(bs, -1, wa), cache_a_v[0].reshape(bs, -1, wa), cache_a_logf[0],
             cache_b_k[0].reshape(bs, -1, wb), cache_b_v[0].reshape(bs, -1, wb),
             cache_c_k[0].reshape(bs, -1, wc), cache_c_v[0].reshape(bs, -1, wc), bs, ts)
    y_s, ev_s, od_s = _trunk(x_sample.reshape(1, bs * ts, d), mods_for(slice(bp, bp + bs), ts), params, cache)

    ka, va, logf, kb, vb = ev_p
    keep = kb.shape[1]
    out_p = (ka.reshape(1, bp, sp, h_a, HEAD_DIM), va.reshape(1, bp, sp, h_a, HEAD_DIM),
             logf.reshape(1, bp, sp, h_a),
             kb.reshape(1, bp, keep, h_b, HEAD_DIM), vb.reshape(1, bp, keep, h_b, HEAD_DIM),
             od_p[0].reshape(1, bp, sp, h_c, 2 * HEAD_DIM), od_p[1].reshape(1, bp, sp, h_c, 2 * HEAD_DIM))
    ka, va, logf, kb, vb = ev_s
    new_bk = jnp.concatenate([cache_b_k[0], kb.reshape(bs, ts, h_b, HEAD_DIM)], axis=1)[:, ts:]
    new_bv = jnp.concatenate([cache_b_v[0], vb.reshape(bs, ts, h_b, HEAD_DIM)], axis=1)[:, ts:]
    out_s = (ka.reshape(1, bs, ts, h_a, HEAD_DIM), va.reshape(1, bs, ts, h_a, HEAD_DIM),
             logf.reshape(1, bs, ts, h_a), new_bk[None], new_bv[None],
             od_s[0].reshape(1, bs, ts, h_c, 2 * HEAD_DIM), od_s[1].reshape(1, bs, ts, h_c, 2 * HEAD_DIM))
    return (y_p, y_s.reshape(bs, ts, d)) + out_p + out_s
```

```python
import functools
import math

import jax
import jax.numpy as jnp
from jax import lax
from jax.experimental import pallas as pl
from jax.experimental.pallas import tpu as pltpu

BF = jnp.bfloat16
F32 = jnp.float32

CHUNK = 64
CHUNK_SHIFT = 6
HEAD_DIM = 64
BAND_CHUNKS = 8
WINDOW_B = BAND_CHUNKS * CHUNK
MAX_REL = 128
TOP_K = 2
RMS_EPS = 1e-6
NEG_INF = -1e30
QK_SCALE = HEAD_DIM ** -0.5

LANES = 128
VMEM_LIMIT_BYTES = 56 * 1024 * 1024

ROW_TILE = 512
ATTN_TILE = 512
BAND_Q_TILE = WINDOW_B // 2
CACHE_TILE_A = 1024
CACHE_TILE_C = 512
MOE_ROW_TILE = 512
MOE_F_TILE = 1792
COMBINE_TILE = 256


def _cparams(sem):
    return pltpu.CompilerParams(dimension_semantics=sem, vmem_limit_bytes=VMEM_LIMIT_BYTES)


def _dot(a, b):
    return jnp.dot(a, b, preferred_element_type=F32)


def _dot_nt(a, b):
    return lax.dot_general(a, b, (((1,), (1,)), ((), ())), preferred_element_type=F32)


def _sigmoid(x):
    return 1.0 / (1.0 + jnp.exp(-x))


def _modulate(x, g, shift, scale):
    y = x * lax.rsqrt(jnp.mean(x * x, axis=-1, keepdims=True) + RMS_EPS)
    return (y * g) * (1.0 + scale) + shift


def _split3(x):
    hi = x.astype(BF)
    r1 = x - hi.astype(F32)
    mid = r1.astype(BF)
    lo = (r1 - mid.astype(F32)).astype(BF)
    return hi, mid, lo


def _cumsum_lanes(x, upper):
    hi, mid, lo = _split3(x)
    return _dot(hi, upper) + _dot(mid, upper) + _dot(lo, upper)


def _upper_tri(t):
    r = jnp.arange(t)
    return (r[:, None] <= r[None, :]).astype(BF)


def _osm_update(s, v, m_ref, l_ref, acc_ref, idx):
    m_prev = m_ref[idx]
    m_new = jnp.maximum(m_prev, jnp.max(s, axis=-1, keepdims=True))
    alpha = jnp.exp(m_prev - m_new)
    p = jnp.exp(s - m_new)
    l_ref[idx] = alpha * l_ref[idx] + jnp.sum(p, axis=-1, keepdims=True)
    acc_ref[idx] = alpha * acc_ref[idx] + _dot(p.astype(BF), v)
    m_ref[idx] = m_new


def _osm_init(m_ref, l_ref, acc_ref):
    m_ref[...] = jnp.full(m_ref.shape, NEG_INF, F32)
    l_ref[...] = jnp.zeros(l_ref.shape, F32)
    acc_ref[...] = jnp.zeros(acc_ref.shape, F32)


def _mod_body(c_ref, w_ref, b_ref, o_ref):
    c = c_ref[...]
    s = (c * _sigmoid(c)).astype(BF)
    o_ref[0] = _dot(s, w_ref[0].astype(BF)) + b_ref[0]


def _mod_call(c_all, w_mod, b_mod):
    depth, d, n = w_mod.shape
    r = c_all.shape[0]
    tn = 1536 if n % 1536 == 0 else n
    return pl.pallas_call(
        _mod_body,
        grid=(depth, n // tn),
        in_specs=[pl.BlockSpec((r, d), lambda l, j: (0, 0)),
                  pl.BlockSpec((1, d, tn), lambda l, j: (l, 0, j)),
                  pl.BlockSpec((1, 1, tn), lambda l, j: (l, 0, j))],
        out_specs=pl.BlockSpec((1, r, tn), lambda l, j: (l, 0, j)),
        out_shape=jax.ShapeDtypeStruct((depth, r, n), F32),
        compiler_params=_cparams(("arbitrary", "arbitrary")),
        name="mod",
    )(c_all, w_mod, b_mod.reshape(depth, 1, n))


def _mod_spec(mod_arr, tm):
    d = mod_arr.shape[-1]
    if mod_arr.shape[1] == 1:
        return pl.BlockSpec((1, 1, d), lambda b, i: (b, 0, 0))
    return pl.BlockSpec((1, tm, d), lambda b, i: (b, i, 0))


def _const_spec(arr):
    nd = arr.ndim
    return pl.BlockSpec(arr.shape, lambda b, i: (0,) * nd, pipeline_mode=pl.Buffered(1))


def _head_blocks(x, augs):
    tm, w = x.shape
    first = lax.broadcasted_iota(jnp.int32, (tm, LANES), 1) < HEAD_DIM
    out = []
    for p in range(w // LANES):
        pair = x[:, p * LANES:(p + 1) * LANES]
        out.append(jnp.where(first, pair, augs[2 * p]).astype(BF))
        out.append(jnp.where(first, pltpu.roll(pair, HEAD_DIM, 1), augs[2 * p + 1]).astype(BF))
    return out


def _lane_range_row(lo, hi, value):
    lane = lax.broadcasted_iota(jnp.int32, (1, LANES), 1)
    return jnp.where((lane >= lo) & (lane < hi), value, 0.0).astype(F32)


def _ones_rows(t):
    return jnp.where(lax.broadcasted_iota(jnp.int32, (16, t), 0) == 0, 1.0, 0.0).astype(BF)


def _log_forget(h, wf_ref, bf_ref):
    z = _dot(h, wf_ref[...]) + bf_ref[...]
    return jnp.minimum(z, 0.0) - jnp.log1p(jnp.exp(-jnp.abs(z)))


def _inproj_even_body(*refs, wa, wb, prompt):
    if prompt:
        (x_ref, g_ref, sh_ref, sc_ref, w_ref, wf_ref, bf_ref, low_ref, place_ref,
         qk_ref, vt_ref, ka_ref, va_ref, kb_ref, vb_ref, logf_ref, carry_ref) = refs
    else:
        (x_ref, g_ref, sh_ref, sc_ref, w_ref, wf_ref, bf_ref,
         qkv_ref, ka_ref, va_ref, kb_ref, vb_ref, logf_ref, lft_ref) = refs
    i = pl.program_id(1)
    h = _modulate(x_ref[0], g_ref[...], sh_ref[0], sc_ref[0]).astype(BF)
    tm = h.shape[0]
    nh = logf_ref.shape[-1]

    qa = _dot(h, w_ref[:, 0:wa])
    ka = _dot(h, w_ref[:, wa:2 * wa])
    va = _dot(h, w_ref[:, 2 * wa:3 * wa])
    o = 3 * wa
    qb = _dot(h, w_ref[:, o:o + wb])
    kb = _dot(h, w_ref[:, o + wb:o + 2 * wb])
    vb = _dot(h, w_ref[:, o + 2 * wb:o + 3 * wb])
    ka_ref[0] = ka
    va_ref[0] = va
    kb_ref[0] = kb
    vb_ref[0] = vb
    logf = _log_forget(h, wf_ref, bf_ref)
    logf_ref[0] = logf[:, :nh]

    if not prompt:
        qkv_ref[0] = jnp.concatenate(
            [(qa * QK_SCALE).astype(BF), ka.astype(BF), va.astype(BF),
             (qb * QK_SCALE).astype(BF), kb.astype(BF), vb.astype(BF)], axis=-1)
        lft_ref[0] = logf.T[:nh, :]
        return

    @pl.when(i == 0)
    def _():
        carry_ref[...] = jnp.zeros(carry_ref.shape, F32)

    low = low_ref[...]
    hi, mid, lo = _split3(logf)
    f_cum = _dot(low, hi) + _dot(low, mid) + _dot(low, lo) + carry_ref[0:1, :]
    carry_ref[...] = jnp.broadcast_to(f_cum[tm - 1:tm, :], carry_ref.shape)
    nhi, nmid, nlo = _split3(-f_cum)
    k_aug = _dot(nhi, place_ref[0]) + _dot(nmid, place_ref[1]) + _dot(nlo, place_ref[2])
    q_ones = _lane_range_row(HEAD_DIM, HEAD_DIM + 3, 1.0)
    zero = jnp.zeros((1, LANES), F32)
    nhb = wb // HEAD_DIM
    blocks = (_head_blocks(qa * QK_SCALE, [q_ones] * nh)
              + _head_blocks(ka, [k_aug[:, n * LANES:(n + 1) * LANES] for n in range(nh)])
              + _head_blocks(qb * QK_SCALE, [zero] * nhb)
              + _head_blocks(kb, [zero] * nhb))
    qk_ref[0] = jnp.concatenate(blocks, axis=-1)
    ones = _ones_rows(tm)
    vt_ref[0] = jnp.concatenate([va.T.astype(BF), ones, vb.T.astype(BF), ones], axis=0)


def _lower_tri(t):
    r = jnp.arange(t)
    return (r[:, None] >= r[None, :]).astype(BF)


def _place_matrices(nh):
    s = jnp.arange(3)[:, None, None]
    r = jnp.arange(LANES)[None, :, None]
    c = jnp.arange(nh * LANES)[None, None, :]
    return ((r < nh) & (c == r * LANES + HEAD_DIM + s)).astype(BF)


def _inproj_even_call(x, g, shift, scale, w_main, w_f, b_f, keep, prompt):
    bx, sx, d = x.shape
    tm = min(ROW_TILE, sx)
    nt = sx // tm
    n = w_main.shape[1]
    wa = wb = n // 6
    nh = wa // HEAD_DIM
    nkeep = keep // tm
    tail = lambda b, i: (b, jnp.maximum(i - (nt - nkeep), 0), 0)
    row = lambda b, i: (b, i, 0)
    in_specs = [pl.BlockSpec((1, tm, d), row), _const_spec(g), _mod_spec(shift, tm), _mod_spec(scale, tm),
                _const_spec(w_main), _const_spec(w_f), _const_spec(b_f)]
    args = [x, g, shift, scale, w_main, w_f, b_f]
    f32_specs = [pl.BlockSpec((1, tm, wa), row), pl.BlockSpec((1, tm, wa), row),
                 pl.BlockSpec((1, tm, wb), tail), pl.BlockSpec((1, tm, wb), tail),
                 pl.BlockSpec((1, tm, nh), row)]
    f32_shapes = [jax.ShapeDtypeStruct((bx, sx, wa), F32), jax.ShapeDtypeStruct((bx, sx, wa), F32),
                  jax.ShapeDtypeStruct((bx, keep, wb), F32), jax.ShapeDtypeStruct((bx, keep, wb), F32),
                  jax.ShapeDtypeStruct((bx, sx, nh), F32)]
    scratch = []
    if prompt:
        low, place = _lower_tri(tm), _place_matrices(nh)
        in_specs += [_const_spec(low), _const_spec(place)]
        args += [low, place]
        nqk = 2 * (wa + wb) // HEAD_DIM * LANES
        nvt = wa + wb + 32
        out_specs = [pl.BlockSpec((1, tm, nqk), row), pl.BlockSpec((1, nvt, tm), lambda b, i: (b, 0, i))] + f32_specs
        out_shape = [jax.ShapeDtypeStruct((bx, sx, nqk), BF), jax.ShapeDtypeStruct((bx, nvt, sx), BF)] + f32_shapes
        scratch.append(pltpu.VMEM((8, LANES), F32))
    else:
        out_specs = [pl.BlockSpec((1, tm, n), row)] + f32_specs + [pl.BlockSpec((1, nh, tm), lambda b, i: (b, 0, i))]
        out_shape = [jax.ShapeDtypeStruct((bx, sx, n), BF)] + f32_shapes + [jax.ShapeDtypeStruct((bx, nh, sx), F32)]
    return pl.pallas_call(
        functools.partial(_inproj_even_body, wa=wa, wb=wb, prompt=prompt),
        grid=(bx, nt), in_specs=in_specs, out_specs=out_specs, out_shape=out_shape,
        scratch_shapes=scratch,
        compiler_params=_cparams(("arbitrary", "arbitrary")),
        name="inproj_even",
    )(*args)


def _inproj_odd_body(*refs, wc, prompt):
    if prompt:
        x_ref, g_ref, sh_ref, sc_ref, w_ref, qk_ref, vt_ref, k_ref, v_ref = refs
    else:
        x_ref, g_ref, sh_ref, sc_ref, w_ref, qkv_ref, k_ref, v_ref = refs
    i = pl.program_id(1)
    h = _modulate(x_ref[0], g_ref[...], sh_ref[0], sc_ref[0]).astype(BF)
    tm = h.shape[0]
    q = _dot(h, w_ref[:, 0:wc])
    k = _dot(h, w_ref[:, wc:2 * wc])
    v = _dot(h, w_ref[:, 2 * wc:3 * wc])
    k_ref[0] = k
    v_ref[0] = v
    if not prompt:
        qkv_ref[0] = jnp.concatenate([(q * QK_SCALE).astype(BF), k.astype(BF), v.astype(BF)], axis=-1)
        return
    nh = wc // LANES
    pos = (i * tm + lax.broadcasted_iota(jnp.int32, (tm, LANES), 0)).astype(F32)
    pos_hi = pos.astype(BF).astype(F32)
    lane = lax.broadcasted_iota(jnp.int32, (tm, LANES), 1)
    k_aug = jnp.where(lane == HEAD_DIM, pos_hi, jnp.where(lane == HEAD_DIM + 1, pos - pos_hi, 0.0))
    q_augs = []
    for hh in range(nh):
        q_augs += [_lane_range_row(HEAD_DIM, HEAD_DIM + 2, _alibi_slope(hh, nh))] * 2
    blocks = _head_blocks(q * QK_SCALE, q_augs) + _head_blocks(k, [k_aug] * (2 * nh))
    qk_ref[0] = jnp.concatenate(blocks, axis=-1)
    vt_ref[0] = jnp.concatenate([v.T.astype(BF), _ones_rows(tm)], axis=0)


def _alibi_slope(h, nh):
    assert 8 % nh == 0, "slopes must be exact powers of two to ride in bf16"
    return 2.0 ** (-8.0 * (h + 1) / nh)


def _inproj_odd_call(x, g, shift, scale, w, prompt):
    bx, sx, d = x.shape
    tm = min(ROW_TILE, sx)
    n = w.shape[1]
    wc = n // 3
    row = lambda b, i: (b, i, 0)
    f32_specs = [pl.BlockSpec((1, tm, wc), row), pl.BlockSpec((1, tm, wc), row)]
    f32_shapes = [jax.ShapeDtypeStruct((bx, sx, wc), F32), jax.ShapeDtypeStruct((bx, sx, wc), F32)]
    if prompt:
        nqk = 2 * wc // HEAD_DIM * LANES
        out_specs = [pl.BlockSpec((1, tm, nqk), row), pl.BlockSpec((1, wc + 16, tm), lambda b, i: (b, 0, i))]
        out_shape = [jax.ShapeDtypeStruct((bx, sx, nqk), BF), jax.ShapeDtypeStruct((bx, wc + 16, sx), BF)]
    else:
        out_specs = [pl.BlockSpec((1, tm, n), row)]
        out_shape = [jax.ShapeDtypeStruct((bx, sx, n), BF)]
    return pl.pallas_call(
        functools.partial(_inproj_odd_body, wc=wc, prompt=prompt),
        grid=(bx, sx // tm),
        in_specs=[pl.BlockSpec((1, tm, d), row), _const_spec(g), _mod_spec(shift, tm), _mod_spec(scale, tm),
                  _const_spec(w)],
        out_specs=out_specs + f32_specs, out_shape=out_shape + f32_shapes,
        compiler_params=_cparams(("arbitrary", "arbitrary")),
        name="inproj_odd",
    )(x, g, shift, scale, w)


def _tflash_update(st, v_aug, mask, extra, m_ref, acc_ref, idx):
    if extra is not None:
        st = st - extra
    if mask is not None:
        st = jnp.where(mask, st, NEG_INF)
    m_prev = m_ref[idx]
    m_new = jnp.maximum(m_prev, jnp.max(st, axis=0, keepdims=True))
    alpha = jnp.exp(m_prev - m_new)
    p = jnp.exp(st - m_new).astype(BF)
    acc_ref[idx] = alpha * acc_ref[idx] + _dot(v_aug, p)
    m_ref[idx] = m_new


def _tflash_init(m_ref, acc_ref):
    m_ref[...] = jnp.full(m_ref.shape, NEG_INF, F32)
    acc_ref[...] = jnp.zeros(acc_ref.shape, F32)


def _fox_prompt_body(q_ref, k_ref, vt_ref, o_ref, m_ref, acc_ref, st_ref, *, nh):
    i = pl.program_id(1)
    j = pl.program_id(2)
    hd = HEAD_DIM

    @pl.when(j == 0)
    def _():
        _tflash_init(m_ref, acc_ref)

    def step(masked):
        q = q_ref[0]
        k = k_ref[0]
        vt = vt_ref[0]
        tq, tk = q.shape[0], k.shape[0]
        ones = vt[nh * hd:, :]
        mask = None
        if masked:
            mask = lax.broadcasted_iota(jnp.int32, (tk, tq), 0) <= lax.broadcasted_iota(jnp.int32, (tk, tq), 1)
        def scores(h):
            st_ref[h % 2] = _dot_nt(k[:, h * LANES:(h + 1) * LANES], q[:, h * LANES:(h + 1) * LANES])

        scores(0)
        for h in range(nh):
            if h + 1 < nh:
                scores(h + 1)
            v_aug = jnp.concatenate([vt[h * hd:(h + 1) * hd, :], ones], axis=0)
            _tflash_update(st_ref[h % 2], v_aug, mask, None, m_ref, acc_ref, h)

    @pl.when(j < i)
    def _():
        step(False)

    @pl.when(j == i)
    def _():
        step(True)

    @pl.when(j == pl.num_programs(2) - 1)
    def _():
        ot = jnp.concatenate([acc_ref[h][:hd] / acc_ref[h][hd:hd + 1] for h in range(nh)], axis=0)
        o_ref[0] = ot.T.astype(BF)


def _fox_prompt_call(qk, vt, wa):
    b, s, _ = qk.shape
    nh = wa // HEAD_DIM
    t = min(ATTN_TILE, s)
    nq = s // t
    wq = nh * LANES
    return pl.pallas_call(
        functools.partial(_fox_prompt_body, nh=nh),
        grid=(b, nq, nq),
        in_specs=[pl.BlockSpec((1, t, wq), lambda b, i, j: (b, i, 0)),
                  pl.BlockSpec((1, t, wq), lambda b, i, j: (b, jnp.minimum(j, i), 1)),
                  pl.BlockSpec((1, wa + 16, t), lambda b, i, j: (b, 0, jnp.minimum(j, i)))],
        out_specs=pl.BlockSpec((1, t, wa), lambda b, i, j: (b, i, 0)),
        out_shape=jax.ShapeDtypeStruct((b, s, wa), BF),
        scratch_shapes=[pltpu.VMEM((nh, 1, t), F32), pltpu.VMEM((nh, HEAD_DIM + 16, t), F32),
                        pltpu.VMEM((2, t, t), F32)],
        compiler_params=_cparams(("arbitrary", "arbitrary", "arbitrary")),
        name="fox_prompt",
    )(qk, qk, vt)


def _band_prompt_body(q_ref, k0_ref, k1_ref, k2_ref, v0_ref, v1_ref, v2_ref, bias_ref, o_ref, st_ref, *, nh):
    i = pl.program_id(1)
    hd = HEAD_DIM
    q = q_ref[0]
    tq = q.shape[0]
    k = jnp.concatenate([k0_ref[0], k1_ref[0], k2_ref[0]], axis=0)
    vts = [v0_ref[0], v1_ref[0], v2_ref[0]]
    in_seq = lax.broadcasted_iota(jnp.int32, (3 * tq, tq), 0) >= (2 - i) * tq
    outs = []
    def scores(h):
        st_ref[h % 2] = _dot_nt(k[:, h * LANES:(h + 1) * LANES], q[:, h * LANES:(h + 1) * LANES])

    scores(0)
    for h in range(nh):
        if h + 1 < nh:
            scores(h + 1)
        st = st_ref[h % 2] + bias_ref[h]
        st = jnp.where(in_seq, st, NEG_INF)
        p = jnp.exp(st - jnp.max(st, axis=0, keepdims=True)).astype(BF)
        acc = None
        for w, vt in enumerate(vts):
            v_aug = jnp.concatenate([vt[h * hd:(h + 1) * hd, :], vt[nh * hd:, :]], axis=0)
            part = _dot(v_aug, p[w * tq:(w + 1) * tq, :])
            acc = part if acc is None else acc + part
        outs.append(acc[:hd] / acc[hd:hd + 1])
    o_ref[0] = jnp.concatenate(outs, axis=0).T.astype(BF)


def _band_bias_prompt(rel_bias, tq):
    nh = rel_bias.shape[0]
    period = 4 * tq
    t = jnp.arange(period)
    t = jnp.where(t < 3 * tq, t, t - period)
    u = rel_bias[:, jnp.clip(2 * tq - t, -MAX_REL, MAX_REL) + MAX_REL].astype(F32)
    skew = jnp.tile(u, (1, tq))[:, :tq * (period - 1)].reshape(nh, tq, period - 1)
    bias = skew[:, :, :3 * tq]
    r = jnp.arange(tq)[:, None]
    c = jnp.arange(3 * tq)[None, :]
    valid = (c // CHUNK >= r // CHUNK) & (c // CHUNK <= r // CHUNK + BAND_CHUNKS)
    return jnp.where(valid[None], bias, NEG_INF)


def _band_prompt_call(qk, vt, rel_bias, wa, wb):
    b, s, _ = qk.shape
    nh = wb // HEAD_DIM
    tq = BAND_Q_TILE
    assert s % tq == 0 and wa == wb
    bias_t = _band_bias_prompt(rel_bias, tq).transpose(0, 2, 1)
    wq = nh * LANES
    cq, ck = 2, 3

    def k_spec(back):
        return pl.BlockSpec((1, tq, wq), lambda b, i: (b, jnp.maximum(i - back, 0), ck))

    def v_spec(back):
        return pl.BlockSpec((1, wb + 16, tq), lambda b, i: (b, 1, jnp.maximum(i - back, 0)))

    return pl.pallas_call(
        functools.partial(_band_prompt_body, nh=nh),
        grid=(b, s // tq),
        in_specs=[pl.BlockSpec((1, tq, wq), lambda b, i: (b, i, cq)),
                  k_spec(2), k_spec(1), k_spec(0), v_spec(2), v_spec(1), v_spec(0),
                  _const_spec(bias_t)],
        out_specs=pl.BlockSpec((1, tq, wb), lambda b, i: (b, i, 0)),
        out_shape=jax.ShapeDtypeStruct((b, s, wb), BF),
        scratch_shapes=[pltpu.VMEM((2, 3 * tq, tq), F32)],
        compiler_params=_cparams(("arbitrary", "arbitrary")),
        name="band_prompt",
    )(qk, qk, qk, qk, vt, vt, vt, bias_t)


def _lambda(lam_ref, lam_init):
    lp = lam_ref[...]
    a = jnp.sum(lp[0:1] * lp[1:2], axis=-1, keepdims=True)
    b = jnp.sum(lp[2:3] * lp[3:4], axis=-1, keepdims=True)
    return jnp.exp(a) - jnp.exp(b) + lam_init


def _diff_finalize(l_ref, acc_ref, lam_ref, gsub_ref, nh, lam_init):
    lam = _lambda(lam_ref, lam_init)
    g = gsub_ref[...]
    outs = []
    for h in range(nh):
        o = acc_ref[2 * h] / l_ref[2 * h] - lam * (acc_ref[2 * h + 1] / l_ref[2 * h + 1])
        o = o * lax.rsqrt(jnp.mean(o * o, axis=-1, keepdims=True) + RMS_EPS)
        outs.append(((o * g) * (1.0 - lam_init)).astype(BF))
    return jnp.concatenate(outs, axis=-1)


def _diff_prompt_body(q_ref, k_ref, vt_ref, lam_ref, gcol_ref, o_ref, m_ref, acc_ref, st_ref, *, nh, lam_init):
    i = pl.program_id(1)
    j = pl.program_id(2)
    dv = 2 * HEAD_DIM

    @pl.when(j == 0)
    def _():
        _tflash_init(m_ref, acc_ref)

    def step(masked):
        q = q_ref[0]
        k = k_ref[0]
        vt = vt_ref[0]
        tq, tk = q.shape[0], k.shape[0]
        ones = vt[nh * dv:, :]
        mask = None
        if masked:
            key = lax.broadcasted_iota(jnp.int32, (tk, tq), 0)
            qry = lax.broadcasted_iota(jnp.int32, (tk, tq), 1)
            mask = (key >> CHUNK_SHIFT) <= (qry >> CHUNK_SHIFT)
            ahead = jnp.maximum(key - qry, 0).astype(F32)
        def scores(n):
            st_ref[n % 2] = _dot_nt(k[:, n * LANES:(n + 1) * LANES], q[:, n * LANES:(n + 1) * LANES])

        scores(0)
        for h in range(nh):
            v_aug = jnp.concatenate([vt[h * dv:(h + 1) * dv, :], ones], axis=0)
            extra = (2.0 * _alibi_slope(h, nh)) * ahead if masked else None
            for u in range(2):
                n = 2 * h + u
                if n + 1 < 2 * nh:
                    scores(n + 1)
                _tflash_update(st_ref[n % 2], v_aug, mask, extra, m_ref, acc_ref, n)

    @pl.when(j < i)
    def _():
        step(False)

    @pl.when(j == i)
    def _():
        step(True)

    @pl.when(j == pl.num_programs(2) - 1)
    def _():
        lam = _lambda(lam_ref, lam_init)
        g = gcol_ref[...]
        outs = []
        for h in range(nh):
            a1 = acc_ref[2 * h]
            a2 = acc_ref[2 * h + 1]
            o = a1[:dv] / a1[dv:dv + 1] - lam * (a2[:dv] / a2[dv:dv + 1])
            o = o * lax.rsqrt(jnp.mean(o * o, axis=0, keepdims=True) + RMS_EPS)
            outs.append((o * g) * (1.0 - lam_init))
        o_ref[0] = jnp.concatenate(outs, axis=0).T.astype(BF)


def _diff_prompt_call(qk, vt, lam_par, g_col, wc, lam_init):
    b, s, _ = qk.shape
    nh = wc // (2 * HEAD_DIM)
    t = min(ATTN_TILE, s)
    nq = s // t
    wq = 2 * nh * LANES
    return pl.pallas_call(
        functools.partial(_diff_prompt_body, nh=nh, lam_init=lam_init),
        grid=(b, nq, nq),
        in_specs=[pl.BlockSpec((1, t, wq), lambda b, i, j: (b, i, 0)),
                  pl.BlockSpec((1, t, wq), lambda b, i, j: (b, jnp.minimum(j, i), 1)),
                  pl.BlockSpec((1, wc + 16, t), lambda b, i, j: (b, 0, jnp.minimum(j, i))),
                  pl.BlockSpec(lam_par.shape, lambda b, i, j: (0, 0)),
                  pl.BlockSpec(g_col.shape, lambda b, i, j: (0, 0))],
        out_specs=pl.BlockSpec((1, t, wc), lambda b, i, j: (b, i, 0)),
        out_shape=jax.ShapeDtypeStruct((b, s, wc), BF),
        scratch_shapes=[pltpu.VMEM((2 * nh, 1, t), F32), pltpu.VMEM((2 * nh, 2 * HEAD_DIM + 16, t), F32),
                        pltpu.VMEM((2, t, t), F32)],
        compiler_params=_cparams(("arbitrary", "arbitrary", "arbitrary")),
        name="diff_prompt",
    )(qk, qk, vt, lam_par, g_col)


def _cumsum_body(x_ref, up_ref, o_ref, carry_ref):
    @pl.when(pl.program_id(1) == 0)
    def _():
        carry_ref[...] = jnp.zeros(carry_ref.shape, F32)
    ft = _cumsum_lanes(x_ref[0], up_ref[...]) + carry_ref[:, 0:1]
    o_ref[0] = ft
    carry_ref[...] = jnp.broadcast_to(ft[:, -1:], carry_ref.shape)


def _cumsum_call(x):
    b, nh, length = x.shape
    t = 512
    up = _upper_tri(t)
    return pl.pallas_call(
        _cumsum_body,
        grid=(b, length // t),
        in_specs=[pl.BlockSpec((1, nh, t), lambda b, j: (b, 0, j)), pl.BlockSpec((t, t), lambda b, j: (0, 0))],
        out_specs=pl.BlockSpec((1, nh, t), lambda b, j: (b, 0, j)),
        out_shape=jax.ShapeDtypeStruct(x.shape, F32),
        scratch_shapes=[pltpu.VMEM((nh, LANES), F32)],
        compiler_params=_cparams(("arbitrary", "arbitrary")),
        name="logf_cumsum",
    )(x, up)


def _fox_sample_body(q_ref, kn_ref, vn_ref, ck_ref, cv_ref, ftc_ref, ftn_ref, o_ref, m_ref, l_ref, acc_ref, *, nh):
    j = pl.program_id(1)
    hd = HEAD_DIM
    q = q_ref[0]
    t = q.shape[0]

    @pl.when(j == 0)
    def _():
        _osm_init(m_ref, l_ref, acc_ref)

    k = ck_ref[0].astype(BF)
    v = cv_ref[0].astype(BF)
    ft = ftc_ref[0]
    for h in range(nh):
        sl = slice(h * hd, (h + 1) * hd)
        s = _dot_nt(q[:, sl], k[:, sl]) - ft[h:h + 1, :]
        _osm_update(s, v[:, sl], m_ref, l_ref, acc_ref, h)

    @pl.when(j == pl.num_programs(1) - 1)
    def _():
        kn = kn_ref[0]
        vn = vn_ref[0]
        ftn = ftn_ref[0][:, :t]
        visible = lax.broadcasted_iota(jnp.int32, (t, t), 1) <= lax.broadcasted_iota(jnp.int32, (t, t), 0)
        outs = []
        for h in range(nh):
            sl = slice(h * hd, (h + 1) * hd)
            s = _dot_nt(q[:, sl], kn[:, sl]) - ftn[h:h + 1, :]
            s = jnp.where(visible, s, NEG_INF)
            _osm_update(s, vn[:, sl], m_ref, l_ref, acc_ref, h)
            outs.append((acc_ref[h] / l_ref[h]).astype(BF))
        o_ref[0] = jnp.concatenate(outs, axis=-1)


def _fox_sample_call(qkv, cache_k, cache_v, ft_pad, wa):
    b, t, _ = qkv.shape
    p_len = cache_k.shape[1]
    nh = wa // HEAD_DIM
    tk = min(CACHE_TILE_A, p_len)
    assert p_len % tk == 0 and p_len % LANES == 0 and t <= LANES
    return pl.pallas_call(
        functools.partial(_fox_sample_body, nh=nh),
        grid=(b, p_len // tk),
        in_specs=[pl.BlockSpec((1, t, wa), lambda b, j: (b, 0, 0)),
                  pl.BlockSpec((1, t, wa), lambda b, j: (b, 0, 1)),
                  pl.BlockSpec((1, t, wa), lambda b, j: (b, 0, 2)),
                  pl.BlockSpec((1, tk, wa), lambda b, j: (b, j, 0)),
                  pl.BlockSpec((1, tk, wa), lambda b, j: (b, j, 0)),
                  pl.BlockSpec((1, nh, tk), lambda b, j: (b, 0, j)),
                  pl.BlockSpec((1, nh, LANES), lambda b, j: (b, 0, p_len // LANES))],
        out_specs=pl.BlockSpec((1, t, wa), lambda b, j: (b, 0, 0)),
        out_shape=jax.ShapeDtypeStruct((b, t, wa), BF),
        scratch_shapes=[pltpu.VMEM((nh, t, 1), F32), pltpu.VMEM((nh, t, 1), F32),
                        pltpu.VMEM((nh, t, HEAD_DIM), F32)],
        compiler_params=_cparams(("arbitrary", "arbitrary")),
        name="fox_sample",
    )(qkv, qkv, qkv, cache_k, cache_v, ft_pad, ft_pad)


def _band_sample_body(q_ref, kn_ref, vn_ref, bk_ref, bv_ref, bias_b_ref, bias_n_ref, o_ref, *, nh):
    hd = HEAD_DIM
    q = q_ref[0]
    kn = kn_ref[0]
    vn = vn_ref[0]
    kb = bk_ref[0].astype(BF)
    vb = bv_ref[0].astype(BF)
    outs = []
    for h in range(nh):
        sl = slice(h * hd, (h + 1) * hd)
        sb = _dot_nt(q[:, sl], kb[:, sl]) + bias_b_ref[h]
        sn = _dot_nt(q[:, sl], kn[:, sl]) + bias_n_ref[h]
        m = jnp.maximum(jnp.max(sb, axis=-1, keepdims=True), jnp.max(sn, axis=-1, keepdims=True))
        pb = jnp.exp(sb - m)
        pn = jnp.exp(sn - m)
        l = jnp.sum(pb, axis=-1, keepdims=True) + jnp.sum(pn, axis=-1, keepdims=True)
        o = _dot(pb.astype(BF), vb[:, sl]) + _dot(pn.astype(BF), vn[:, sl])
        outs.append((o / l).astype(BF))
    o_ref[0] = jnp.concatenate(outs, axis=-1)


def _band_sample_call(qkv, buf_k, buf_v, rel_bias, wb):
    b, t, _ = qkv.shape
    lb = buf_k.shape[1]
    nh = wb // HEAD_DIM
    k_pos = jnp.concatenate([jnp.arange(-lb, 0), jnp.arange(t)])
    rel = jnp.arange(t)[:, None] - k_pos[None, :]
    bias = rel_bias[:, jnp.clip(rel, -MAX_REL, MAX_REL) + MAX_REL].astype(F32)
    bias_b, bias_n = bias[:, :, :lb], bias[:, :, lb:]
    return pl.pallas_call(
        functools.partial(_band_sample_body, nh=nh),
        grid=(b,),
        in_specs=[pl.BlockSpec((1, t, wb), lambda b: (b, 0, 3)),
                  pl.BlockSpec((1, t, wb), lambda b: (b, 0, 4)),
                  pl.BlockSpec((1, t, wb), lambda b: (b, 0, 5)),
                  pl.BlockSpec((1, lb, wb), lambda b: (b, 0, 0)),
                  pl.BlockSpec((1, lb, wb), lambda b: (b, 0, 0)),
                  pl.BlockSpec(bias_b.shape, lambda b: (0, 0, 0)),
                  pl.BlockSpec(bias_n.shape, lambda b: (0, 0, 0))],
        out_specs=pl.BlockSpec((1, t, wb), lambda b: (b, 0, 0)),
        out_shape=jax.ShapeDtypeStruct((b, t, wb), BF),
        compiler_params=_cparams(("arbitrary",)),
        name="band_sample",
    )(qkv, qkv, qkv, buf_k, buf_v, bias_b, bias_n)


def _diff_sample_body(q_ref, kn_ref, vn_ref, ck_ref, cv_ref, lam_ref, gsub_ref, o_ref, m_ref, l_ref, acc_ref,
                      *, nh, lam_init, p_len):
    j = pl.program_id(1)
    hd = HEAD_DIM
    q = q_ref[0]
    t = q.shape[0]

    @pl.when(j == 0)
    def _():
        _osm_init(m_ref, l_ref, acc_ref)

    def attend(k, v, dist):
        for h in range(nh):
            slope = 2.0 ** (-8.0 * (h + 1) / nh)
            bias = slope * dist
            vh = v[:, 2 * h * hd:(2 * h + 2) * hd]
            for u in range(2):
                sl = slice((2 * h + u) * hd, (2 * h + u + 1) * hd)
                s = _dot_nt(q[:, sl], k[:, sl]) - bias
                _osm_update(s, vh, m_ref, l_ref, acc_ref, 2 * h + u)

    tk = ck_ref.shape[1]
    row = lax.broadcasted_iota(jnp.int32, (t, tk), 0)
    col = lax.broadcasted_iota(jnp.int32, (t, tk), 1)
    attend(ck_ref[0].astype(BF), cv_ref[0].astype(BF), (p_len + row - j * tk - col).astype(F32))

    @pl.when(j == pl.num_programs(1) - 1)
    def _():
        rown = lax.broadcasted_iota(jnp.int32, (t, t), 0)
        coln = lax.broadcasted_iota(jnp.int32, (t, t), 1)
        attend(kn_ref[0], vn_ref[0], jnp.abs(rown - coln).astype(F32))
        o_ref[0] = _diff_finalize(l_ref, acc_ref, lam_ref, gsub_ref, nh, lam_init)


def _diff_sample_call(qkv, cache_k, cache_v, lam_par, g_sub, wc, lam_init):
    b, t, _ = qkv.shape
    p_len = cache_k.shape[1]
    assert (p_len // CHUNK) * CHUNK == p_len and t <= CHUNK
    nh = wc // (2 * HEAD_DIM)
    tk = min(CACHE_TILE_C, p_len)
    return pl.pallas_call(
        functools.partial(_diff_sample_body, nh=nh, lam_init=lam_init, p_len=p_len),
        grid=(b, p_len // tk),
        in_specs=[pl.BlockSpec((1, t, wc), lambda b, j: (b, 0, 0)),
                  pl.BlockSpec((1, t, wc), lambda b, j: (b, 0, 1)),
                  pl.BlockSpec((1, t, wc), lambda b, j: (b, 0, 2)),
                  pl.BlockSpec((1, tk, wc), lambda b, j: (b, j, 0)),
                  pl.BlockSpec((1, tk, wc), lambda b, j: (b, j, 0)),
                  pl.BlockSpec(lam_par.shape, lambda b, j: (0, 0)),
                  pl.BlockSpec(g_sub.shape, lambda b, j: (0, 0))],
        out_specs=pl.BlockSpec((1, t, wc), lambda b, j: (b, 0, 0)),
        out_shape=jax.ShapeDtypeStruct((b, t, wc), BF),
        scratch_shapes=[pltpu.VMEM((2 * nh, t, 1), F32), pltpu.VMEM((2 * nh, t, 1), F32),
                        pltpu.VMEM((2 * nh, t, 2 * HEAD_DIM), F32)],
        compiler_params=_cparams(("arbitrary", "arbitrary")),
        name="diff_sample",
    )(qkv, qkv, qkv, cache_k, cache_v, lam_par, g_sub)


def _out_ffn_body(x_ref, o_ref, gm_ref, g_ref, sh_ref, sc_ref, gf_ref, wo_ref, w1_ref, w3_ref, w2_ref, y_ref,
                  *, f_chunk):
    x1 = x_ref[0] + gm_ref[0] * _dot(o_ref[0], wo_ref[...])
    h = _modulate(x1, g_ref[...], sh_ref[0], sc_ref[0]).astype(BF)
    f_total = w1_ref.shape[1]
    f = None
    for c0 in range(0, f_total, f_chunk):
        a = _dot(h, w1_ref[:, c0:c0 + f_chunk])
        g = _dot(h, w3_ref[:, c0:c0 + f_chunk])
        u = ((a * _sigmoid(a)) * g).astype(BF)
        part = _dot(u, w2_ref[c0:c0 + f_chunk, :])
        f = part if f is None else f + part
    y_ref[0] = x1 + gf_ref[0] * f


def _out_ffn_call(x, o, gate_m, g, shift, scale, gate_f, wo, w1, w3, w2):
    bx, sx, d = x.shape
    tm = min(ROW_TILE, sx)
    f_total = w1.shape[1]
    f_chunk = f_total // 2 if (f_total // 2) % LANES == 0 else f_total
    row = lambda b, i: (b, i, 0)
    return pl.pallas_call(
        functools.partial(_out_ffn_body, f_chunk=f_chunk),
        grid=(bx, sx // tm),
        in_specs=[pl.BlockSpec((1, tm, d), row), pl.BlockSpec((1, tm, o.shape[-1]), row),
                  _mod_spec(gate_m, tm), _const_spec(g), _mod_spec(shift, tm), _mod_spec(scale, tm),
                  _mod_spec(gate_f, tm), _const_spec(wo), _const_spec(w1), _const_spec(w3), _const_spec(w2)],
        out_specs=pl.BlockSpec((1, tm, d), row),
        out_shape=jax.ShapeDtypeStruct((bx, sx, d), F32),
        compiler_params=_cparams(("arbitrary", "arbitrary")),
        name="out_ffn",
    )(x, o, gate_m, g, shift, scale, gate_f, wo, w1, w3, w2)


def _out_router_body(x_ref, o_ref, gm_ref, g_ref, sh_ref, sc_ref, wo_ref, wr_ref, x1_ref, h_ref, route_ref,
                     *, n_exp):
    x1 = x_ref[0] + gm_ref[0] * _dot(o_ref[0], wo_ref[...])
    x1_ref[0] = x1
    h = _modulate(x1, g_ref[...], sh_ref[0], sc_ref[0])
    hb = h.astype(BF)
    h_ref[0] = hb.astype(F32)
    logits = _dot(hb, wr_ref[0]) + _dot(hb, wr_ref[1]) + _dot(hb, wr_ref[2])
    col = lax.broadcasted_iota(jnp.int32, logits.shape, 1)
    lg = jnp.where(col < n_exp, logits, NEG_INF)
    e = jnp.exp(lg - jnp.max(lg, axis=-1, keepdims=True))
    probs = e / jnp.sum(e, axis=-1, keepdims=True)
    p1 = jnp.max(probs, axis=-1, keepdims=True)
    i1 = jnp.min(jnp.where(probs == p1, col, LANES), axis=-1, keepdims=True)
    rest = jnp.where(col == i1, -1.0, probs)
    p2 = jnp.max(rest, axis=-1, keepdims=True)
    i2 = jnp.min(jnp.where(rest == p2, col, LANES), axis=-1, keepdims=True)
    tot = p1 + p2
    route = jnp.where(col == 0, i1.astype(F32),
                      jnp.where(col == 1, i2.astype(F32),
                                jnp.where(col == 2, p1 / tot, jnp.where(col == 3, p2 / tot, 0.0))))
    route_ref[0] = route


def _out_router_call(x, o, gate_m, g, shift, scale, wo, wr3, n_exp):
    bx, sx, d = x.shape
    tm = min(ROW_TILE, sx)
    row = lambda b, i: (b, i, 0)
    return pl.pallas_call(
        functools.partial(_out_router_body, n_exp=n_exp),
        grid=(bx, sx // tm),
        in_specs=[pl.BlockSpec((1, tm, d), row), pl.BlockSpec((1, tm, o.shape[-1]), row),
                  _mod_spec(gate_m, tm), _const_spec(g), _mod_spec(shift, tm), _mod_spec(scale, tm),
                  _const_spec(wo), _const_spec(wr3)],
        out_specs=[pl.BlockSpec((1, tm, d), row), pl.BlockSpec((1, tm, d), row),
                   pl.BlockSpec((1, tm, LANES), row)],
        out_shape=[jax.ShapeDtypeStruct((bx, sx, d), F32), jax.ShapeDtypeStruct((bx, sx, d), F32),
                   jax.ShapeDtypeStruct((bx, sx, LANES), F32)],
        compiler_params=_cparams(("arbitrary", "arbitrary")),
        name="out_router",
    )(x, o, gate_m, g, shift, scale, wo, wr3)


def _moe_ffn_body(te_ref, nv_ref, idx_ref, idx_next_ref, h_ref, w1_ref, w3_ref, w2_ref, y_ref, xbuf, xb16, sem):
    t = pl.program_id(0)
    j = pl.program_id(1)
    nv = nv_ref[0]
    tm = xbuf.shape[1]
    slot = lax.rem(t, 2)

    def issue(rows_ref, s):
        def body(r, c):
            pltpu.make_async_copy(h_ref.at[pl.ds(rows_ref[0, 0, r], 1)], xbuf.at[s, pl.ds(r, 1)], sem.at[s]).start()
            return c
        lax.fori_loop(0, tm, body, 0, unroll=8)

    @pl.when((j == 0) & (t == 0))
    def _():
        issue(idx_ref, 0)

    @pl.when((j == 0) & (t + 1 < nv))
    def _():
        issue(idx_next_ref, 1 - slot)

    @pl.when((j == 0) & (t < nv))
    def _():
        pltpu.make_async_copy(h_ref.at[pl.ds(0, tm)], xbuf.at[slot], sem.at[slot]).wait()
        xb16[...] = xbuf[slot].astype(BF)

    @pl.when(t < nv)
    def _():
        xb = xb16[...]
        a = _dot(xb, w1_ref[0])
        g = _dot(xb, w3_ref[0])
        u = ((a * _sigmoid(a)) * g).astype(BF)
        part = _dot(u, w2_ref[0])

        @pl.when(j == 0)
        def _():
            y_ref[...] = part

        @pl.when(j > 0)
        def _():
            y_ref[...] += part

    @pl.when(t >= nv)
    def _():
        y_ref[...] = jnp.zeros(y_ref.shape, F32)


def _moe_ffn_call(h, row_tok, tile_e, n_valid, w1, w3, w2):
    n_rows = row_tok.shape[0]
    d = h.shape[1]
    f_total = w1.shape[2]
    tf = MOE_F_TILE if f_total % MOE_F_TILE == 0 else f_total
    nf = f_total // tf
    tm = MOE_ROW_TILE
    n_tiles = n_rows // tm
    idx3 = row_tok.reshape(n_tiles, 1, tm)

    def fcol(t, j, te, nv):
        return jnp.where(t < nv[0], j, nf - 1)

    grid_spec = pltpu.PrefetchScalarGridSpec(
        num_scalar_prefetch=2,
        grid=(n_tiles, nf),
        in_specs=[pl.BlockSpec((1, 1, tm), lambda t, j, te, nv: (t, 0, 0), memory_space=pltpu.SMEM),
                  pl.BlockSpec((1, 1, tm), lambda t, j, te, nv: (jnp.minimum(t + 1, n_tiles - 1), 0, 0),
                               memory_space=pltpu.SMEM),
                  pl.BlockSpec(memory_space=pl.ANY),
                  pl.BlockSpec((1, d, tf), lambda t, j, te, nv: (te[t], 0, fcol(t, j, te, nv))),
                  pl.BlockSpec((1, d, tf), lambda t, j, te, nv: (te[t], 0, fcol(t, j, te, nv))),
                  pl.BlockSpec((1, tf, d), lambda t, j, te, nv: (te[t], fcol(t, j, te, nv), 0))],
        out_specs=pl.BlockSpec((tm, d), lambda t, j, te, nv: (t, 0)),
        scratch_shapes=[pltpu.VMEM((2, tm, d), F32), pltpu.VMEM((tm, d), BF), pltpu.SemaphoreType.DMA((2,))],
    )
    return pl.pallas_call(
        _moe_ffn_body,
        grid_spec=grid_spec,
        out_shape=jax.ShapeDtypeStruct((n_rows, d), F32),
        compiler_params=_cparams(("arbitrary", "arbitrary")),
        name="moe_ffn",
    )(tile_e, n_valid, idx3, idx3, h, w1, w3, w2)


def _combine_body(d0_ref, d1_ref, x_ref, gf_ref, w_ref, g_ref, yb_ref, o_ref, buf, sem):
    t = pl.program_id(0)
    nt = pl.num_programs(0)
    tc = x_ref.shape[1]

    def row_copy(idx_ref, r, slot, k):
        return pltpu.make_async_copy(yb_ref.at[pl.ds(idx_ref[0, 0, 2 * r + k], 1)],
                                     buf.at[slot, k, pl.ds(r, 1)], sem.at[slot])

    def issue(idx_ref, slot):
        def body(r, c):
            row_copy(idx_ref, r, slot, 0).start()
            row_copy(idx_ref, r, slot, 1).start()
            return c
        lax.fori_loop(0, tc, body, 0, unroll=4)

    slot = lax.rem(t, 2)

    @pl.when(t == 0)
    def _():
        issue(d0_ref, 0)

    @pl.when(t + 1 < nt)
    def _():
        issue(d1_ref, 1 - slot)

    for k in range(TOP_K):
        pltpu.make_async_copy(yb_ref.at[pl.ds(0, tc)], buf.at[slot, k], sem.at[slot]).wait()
    w = w_ref[0]
    f = w[:, 2:3] * buf[slot, 0] + w[:, 3:4] * buf[slot, 1]
    y = x_ref[0] + gf_ref[0] * f
    y = y * lax.rsqrt(jnp.mean(y * y, axis=-1, keepdims=True) + RMS_EPS)
    o_ref[0] = y * g_ref[...]


def _combine_call(x1, gate_f, route, g_final, yb, dest):
    bx, sx, d = x1.shape
    tc = min(COMBINE_TILE, sx)
    nps = sx // tc
    nt = bx * nps
    dest3 = dest.reshape(nt, 1, 2 * tc)
    row = lambda t: (t // nps, t % nps, 0)
    if gate_f.shape[1] == 1:
        gate_spec = pl.BlockSpec((1, 1, d), lambda t: (t // nps, 0, 0))
    else:
        gate_spec = pl.BlockSpec((1, tc, d), row)
    return pl.pallas_call(
        _combine_body,
        grid=(nt,),
        in_specs=[pl.BlockSpec((1, 1, 2 * tc), lambda t: (t, 0, 0), memory_space=pltpu.SMEM),
                  pl.BlockSpec((1, 1, 2 * tc), lambda t: (jnp.minimum(t + 1, nt - 1), 0, 0),
                               memory_space=pltpu.SMEM),
                  pl.BlockSpec((1, tc, d), row), gate_spec,
                  pl.BlockSpec((1, tc, LANES), row),
                  pl.BlockSpec(g_final.shape, lambda t: (0, 0)),
                  pl.BlockSpec(memory_space=pl.ANY)],
        out_specs=pl.BlockSpec((1, tc, d), row),
        out_shape=jax.ShapeDtypeStruct((bx, sx, d), F32),
        scratch_shapes=[pltpu.VMEM((2, 2, tc, d), F32), pltpu.SemaphoreType.DMA((2,))],
        compiler_params=_cparams(("arbitrary",)),
        name="moe_combine",
    )(dest3, dest3, x1, gate_f, route, g_final, yb)


def _route_plan(route, n_exp, tm):
    n_tok = route.shape[0]
    n_slots = n_tok * TOP_K
    slot_e = route[:, :TOP_K].astype(jnp.int32).reshape(n_slots)
    onehot = (slot_e[:, None] == jnp.arange(n_exp, dtype=jnp.int32)[None, :]).astype(jnp.int32)
    csum = jnp.cumsum(onehot, axis=0)
    rank = jnp.sum(csum * onehot, axis=1) - 1
    counts = csum[-1]
    padded = ((counts + tm - 1) // tm) * tm
    pad_end = jnp.cumsum(padded)
    pad_start = pad_end - padded
    dest = jnp.sum(onehot * pad_start[None, :], axis=1) + rank
    n_tiles = -(-n_slots // tm) + n_exp
    n_rows = n_tiles * tm
    slot_tok = jnp.repeat(jnp.arange(n_tok, dtype=jnp.int32), TOP_K)
    row_tok = jnp.zeros((n_rows,), jnp.int32).at[dest].set(slot_tok)
    n_valid = (pad_end[-1] // tm).astype(jnp.int32)
    tile_start = jnp.arange(n_tiles, dtype=jnp.int32) * tm
    tile_e = jnp.minimum(jnp.searchsorted(pad_end, tile_start, side='right'), n_exp - 1).astype(jnp.int32)
    last_e = tile_e[jnp.maximum(n_valid - 1, 0)]
    tile_e = jnp.where(jnp.arange(n_tiles) < n_valid, tile_e, last_e)
    return dest, row_tok, tile_e, n_valid.reshape(1)


def _moe(x1, h, route, gate_f, g_final, w1, w3, w2):
    bx, sx, d = x1.shape
    n_exp = w1.shape[0]
    dest, row_tok, tile_e, n_valid = _route_plan(route.reshape(bx * sx, LANES), n_exp, MOE_ROW_TILE)
    yb = _moe_ffn_call(h.reshape(bx * sx, d), row_tok, tile_e, n_valid, w1, w3, w2)
    return _combine_call(x1, gate_f, route, g_final, yb, dest)


def _trunk(x, mods, p, cache):
    sh_m0, sc_m0, gt_m0, sh_f0, sc_f0, gt_f0 = mods[0]
    sh_m1, sc_m1, gt_m1, sh_f1, sc_f1, gt_f1 = mods[1]
    wa = p['wa']
    wb = p['wb']
    wc = p['wc']
    bx, sx, d = x.shape

    if cache is None:
        keep = min(WINDOW_B, sx)
        qk, vt, ka, va, kb, vb, logf = _inproj_even_call(
            x, p['g_mix0'], sh_m0, sc_m0, p['w_in_even'], p['w_f'], p['b_f'], keep, True)
        o_a = _fox_prompt_call(qk, vt, wa)
        o_b = _band_prompt_call(qk, vt, p['rel_bias'], wa, wb)
        x_l0_state = (ka, va, logf, kb, vb)
    else:
        ck, cv, clogf, bk, bv, cck, ccv, b, t = cache
        qkv, ka, va, kb, vb, logf, lft, = _inproj_even_call(
            x, p['g_mix0'], sh_m0, sc_m0, p['w_in_even'], p['w_f'], p['b_f'], sx, False)
        nh = wa // HEAD_DIM
        p_len = ck.shape[1]
        lft_new = lft.reshape(nh, b, t).transpose(1, 0, 2)
        lp = -(-(p_len + t) // 512) * 512
        lcat = jnp.concatenate([clogf.transpose(0, 2, 1), lft_new,
                                jnp.zeros((b, nh, lp - p_len - t), F32)], axis=-1)
        ft_pad = _cumsum_call(lcat)
        qkv_b = qkv.reshape(b, t, qkv.shape[-1])
        o_a = _fox_sample_call(qkv_b, ck, cv, ft_pad, wa).reshape(1, b * t, wa)
        o_b = _band_sample_call(qkv_b, bk, bv, p['rel_bias'], wb).reshape(1, b * t, wb)
        x_l0_state = (ka, va, logf, kb, vb)
    o = jnp.concatenate([o_a, o_b], axis=-1)
    x = _out_ffn_call(x, o, gt_m0, p['g_ffn0'], sh_f0, sc_f0, gt_f0,
                      p['w_out_even'], p['w1_dense'], p['w3_dense'], p['w2_dense'])

    if cache is None:
        qk, vt, kc, vc = _inproj_odd_call(x, p['g_mix1'], sh_m1, sc_m1, p['w_in_odd'], True)
        o = _diff_prompt_call(qk, vt, p['lam_par'], p['g_sub'].reshape(-1, 1), wc, p['lam_init'])
    else:
        qkv, kc, vc = _inproj_odd_call(x, p['g_mix1'], sh_m1, sc_m1, p['w_in_odd'], False)
        qkv_b = qkv.reshape(b, t, qkv.shape[-1])
        o = _diff_sample_call(qkv_b, cck, ccv, p['lam_par'], p['g_sub'], wc, p['lam_init']).reshape(1, b * t, wc)
    x1, h, route = _out_router_call(x, o, gt_m1, p['g_ffn1'], sh_f1, sc_f1, p['w_out_odd'], p['w_router3'],
                                    p['n_exp'])
    y = _moe(x1, h, route, gt_f1, p['g_final'], p['w1_moe'], p['w3_moe'], p['w2_moe'])
    return y, x_l0_state, (kc, vc)


def kernel(x_prompt, x_sample, cache_a_k, cache_a_v, cache_a_logf, cache_b_k, cache_b_v, cache_c_k, cache_c_v,
           c_prompt, c_sample, w_mod, b_mod, g_mix, g_ffn, g_final, w_in_even, b_forget, rel_bias, w_out_even,
           w_in_odd, lambda_q1, lambda_k1, lambda_q2, lambda_k2, g_subln, w_out_odd, w1_dense, w3_dense, w2_dense,
           w_router, w1_moe, w3_moe, w2_moe):
    bp, sp, d = x_prompt.shape
    bs, ts, _ = x_sample.shape
    h_a = cache_a_k.shape[3]
    h_b = cache_b_k.shape[3]
    h_c = cache_c_k.shape[3]
    wa, wb, wc = h_a * HEAD_DIM, h_b * HEAD_DIM, h_c * 2 * HEAD_DIM
    n_exp = w_router.shape[-1]
    assert w_mod.shape[0] == 2, "kernel is written for the 2-layer trunk"

    we = w_in_even[0]
    w_main = jnp.concatenate([we[:, :3 * wa], we[:, 3 * wa + h_a:]], axis=1).astype(BF)
    w_f = jnp.pad(we[:, 3 * wa:3 * wa + h_a], ((0, 0), (0, LANES - h_a))).astype(BF)
    b_f = jnp.pad(b_forget[0], (0, LANES - h_a)).reshape(1, LANES).astype(F32)
    wr = jnp.pad(w_router[0], ((0, 0), (0, LANES - n_exp)))
    wr_hi = wr.astype(BF)
    wr_r1 = wr - wr_hi.astype(F32)
    wr_mid = wr_r1.astype(BF)
    wr_lo = (wr_r1 - wr_mid.astype(F32)).astype(BF)
    params = dict(
        wa=wa, wb=wb, wc=wc, n_exp=n_exp, lam_init=0.8 - 0.6 * math.exp(-0.3 * 1),
        g_mix0=g_mix[0:1], g_mix1=g_mix[1:2], g_ffn0=g_ffn[0:1], g_ffn1=g_ffn[1:2],
        g_final=g_final.reshape(1, d),
        w_in_even=w_main, w_f=w_f, b_f=b_f, rel_bias=rel_bias[0],
        w_out_even=w_out_even[0].astype(BF), w_in_odd=w_in_odd[0].astype(BF),
        lam_par=jnp.concatenate([lambda_q1, lambda_k1, lambda_q2, lambda_k2], axis=0),
        g_sub=g_subln[0:1], w_out_odd=w_out_odd[0].astype(BF),
        w1_dense=w1_dense[0].astype(BF), w3_dense=w3_dense[0].astype(BF), w2_dense=w2_dense[0].astype(BF),
        w_router3=jnp.stack([wr_hi, wr_mid, wr_lo]),
        w1_moe=w1_moe[0].astype(BF), w3_moe=w3_moe[0].astype(BF), w2_moe=w2_moe[0].astype(BF),
    )

    mod = _mod_call(jnp.concatenate([c_prompt, c_sample], axis=0), w_mod, b_mod)

    def mods_for(rows, per_row_len):
        out = []
        for l in range(2):
            parts = []
            for k in range(6):
                m = mod[l, rows, k * d:(k + 1) * d]
                if per_row_len:
                    m = jnp.repeat(m, per_row_len, axis=0).reshape(1, -1, d)
                else:
                    m = m.reshape(-1, 1, d)
                parts.append(m)
            out.append(parts)
        return out

    y_p, ev_p, od_p = _trunk(x_prompt, mods_for(slice(0, bp), 0), params, None)
    cache = (cache_a_k[0].reshape(bs, -1, wa), cache_a_v[0].reshape(bs, -1, wa), cache_a_logf[0],
             cache_b_k[0].reshape(bs, -1, wb), cache_b_v[0].reshape(bs, -1, wb),
             cache_c_k[0].reshape(bs, -1, wc), cache_c_v[0].reshape(bs, -1, wc), bs, ts)
    y_s, ev_s, od_s = _trunk(x_sample.reshape(1, bs * ts, d), mods_for(slice(bp, bp + bs), ts), params, cache)

    ka, va, logf, kb, vb = ev_p
    keep = kb.shape[1]
    out_p = (ka.reshape(1, bp, sp, h_a, HEAD_DIM), va.reshape(1, bp, sp, h_a, HEAD_DIM),
             logf.reshape(1, bp, sp, h_a),
             kb.reshape(1, bp, keep, h_b, HEAD_DIM), vb.reshape(1, bp, keep, h_b, HEAD_DIM),
             od_p[0].reshape(1, bp, sp, h_c, 2 * HEAD_DIM), od_p[1].reshape(1, bp, sp, h_c, 2 * HEAD_DIM))
    ka, va, logf, kb, vb = ev_s
    new_bk = jnp.concatenate([cache_b_k[0], kb.reshape(bs, ts, h_b, HEAD_DIM)], axis=1)[:, ts:]
    new_bv = jnp.concatenate([cache_b_v[0], vb.reshape(bs, ts, h_b, HEAD_DIM)], axis=1)[:, ts:]
    out_s = (ka.reshape(1, bs, ts, h_a, HEAD_DIM), va.reshape(1, bs, ts, h_a, HEAD_DIM),
             logf.reshape(1, bs, ts, h_a), new_bk[None], new_bv[None],
             od_s[0].reshape(1, bs, ts, h_c, 2 * HEAD_DIM), od_s[1].reshape(1, bs, ts, h_c, 2 * HEAD_DIM))
    return (y_p, y_s.reshape(bs, ts, d)) + out_p + out_s
```

```python
import functools
import math

import jax
import jax.numpy as jnp
from jax import lax
from jax.experimental import pallas as pl
from jax.experimental.pallas import tpu as pltpu

BF = jnp.bfloat16
F32 = jnp.float32

CHUNK = 64
CHUNK_SHIFT = 6
HEAD_DIM = 64
BAND_CHUNKS = 8
WINDOW_B = BAND_CHUNKS * CHUNK
MAX_REL = 128
TOP_K = 2
RMS_EPS = 1e-6
NEG_INF = -1e30
QK_SCALE = HEAD_DIM ** -0.5

LANES = 128
VMEM_LIMIT_BYTES = 56 * 1024 * 1024

ROW_TILE = 512
ATTN_TILE = 512
BAND_Q_TILE = WINDOW_B // 2
CACHE_TILE_A = 1024
CACHE_TILE_C = 512
MOE_ROW_TILE = 512
MOE_F_TILE = 1792
COMBINE_TILE = 256


def _cparams(sem):
    return pltpu.CompilerParams(dimension_semantics=sem, vmem_limit_bytes=VMEM_LIMIT_BYTES)


def _dot(a, b):
    return jnp.dot(a, b, preferred_element_type=F32)


def _dot_nt(a, b):
    return lax.dot_general(a, b, (((1,), (1,)), ((), ())), preferred_element_type=F32)


def _sigmoid(x):
    return 1.0 / (1.0 + jnp.exp(-x))


def _modulate(x, g, shift, scale):
    y = x * lax.rsqrt(jnp.mean(x * x, axis=-1, keepdims=True) + RMS_EPS)
    return (y * g) * (1.0 + scale) + shift


def _split3(x):
    hi = x.astype(BF)
    r1 = x - hi.astype(F32)
    mid = r1.astype(BF)
    lo = (r1 - mid.astype(F32)).astype(BF)
    return hi, mid, lo


def _cumsum_lanes(x, upper):
    hi, mid, lo = _split3(x)
    return _dot(hi, upper) + _dot(mid, upper) + _dot(lo, upper)


def _upper_tri(t):
    r = jnp.arange(t)
    return (r[:, None] <= r[None, :]).astype(BF)


def _osm_update(s, v, m_ref, l_ref, acc_ref, idx):
    m_prev = m_ref[idx]
    m_new = jnp.maximum(m_prev, jnp.max(s, axis=-1, keepdims=True))
    alpha = jnp.exp(m_prev - m_new)
    p = jnp.exp(s - m_new)
    l_ref[idx] = alpha * l_ref[idx] + jnp.sum(p, axis=-1, keepdims=True)
    acc_ref[idx] = alpha * acc_ref[idx] + _dot(p.astype(BF), v)
    m_ref[idx] = m_new


def _osm_init(m_ref, l_ref, acc_ref):
    m_ref[...] = jnp.full(m_ref.shape, NEG_INF, F32)
    l_ref[...] = jnp.zeros(l_ref.shape, F32)
    acc_ref[...] = jnp.zeros(acc_ref.shape, F32)


def _mod_body(c_ref, w_ref, b_ref, o_ref):
    c = c_ref[...]
    s = (c * _sigmoid(c)).astype(BF)
    o_ref[0] = _dot(s, w_ref[0].astype(BF)) + b_ref[0]


def _mod_call(c_all, w_mod, b_mod):
    depth, d, n = w_mod.shape
    r = c_all.shape[0]
    tn = 1536 if n % 1536 == 0 else n
    return pl.pallas_call(
        _mod_body,
        grid=(depth, n // tn),
        in_specs=[pl.BlockSpec((r, d), lambda l, j: (0, 0)),
                  pl.BlockSpec((1, d, tn), lambda l, j: (l, 0, j)),
                  pl.BlockSpec((1, 1, tn), lambda l, j: (l, 0, j))],
        out_specs=pl.BlockSpec((1, r, tn), lambda l, j: (l, 0, j)),
        out_shape=jax.ShapeDtypeStruct((depth, r, n), F32),
        compiler_params=_cparams(("arbitrary", "arbitrary")),
        name="mod",
    )(c_all, w_mod, b_mod.reshape(depth, 1, n))


def _mod_spec(mod_arr, tm):
    d = mod_arr.shape[-1]
    if mod_arr.shape[1] == 1:
        return pl.BlockSpec((1, 1, d), lambda b, i: (b, 0, 0))
    return pl.BlockSpec((1, tm, d), lambda b, i: (b, i, 0))


def _const_spec(arr):
    nd = arr.ndim
    return pl.BlockSpec(arr.shape, lambda b, i: (0,) * nd, pipeline_mode=pl.Buffered(1))


def _head_blocks(x, augs):
    tm, w = x.shape
    first = lax.broadcasted_iota(jnp.int32, (tm, LANES), 1) < HEAD_DIM
    out = []
    for p in range(w // LANES):
        pair = x[:, p * LANES:(p + 1) * LANES]
        out.append(jnp.where(first, pair, augs[2 * p]).astype(BF))
        out.append(jnp.where(first, pltpu.roll(pair, HEAD_DIM, 1), augs[2 * p + 1]).astype(BF))
    return out


def _lane_range_row(lo, hi, value):
    lane = lax.broadcasted_iota(jnp.int32, (1, LANES), 1)
    return jnp.where((lane >= lo) & (lane < hi), value, 0.0).astype(F32)


def _ones_rows(t):
    return jnp.where(lax.broadcasted_iota(jnp.int32, (16, t), 0) == 0, 1.0, 0.0).astype(BF)


def _log_forget(h, wf_ref, bf_ref):
    z = _dot(h, wf_ref[...]) + bf_ref[...]
    return jnp.minimum(z, 0.0) - jnp.log1p(jnp.exp(-jnp.abs(z)))


def _inproj_even_body(*refs, wa, wb, prompt):
    if prompt:
        (x_ref, g_ref, sh_ref, sc_ref, w_ref, wf_ref, bf_ref, low_ref, place_ref,
         qk_ref, vt_ref, ka_ref, va_ref, kb_ref, vb_ref, logf_ref, carry_ref) = refs
    else:
        (x_ref, g_ref, sh_ref, sc_ref, w_ref, wf_ref, bf_ref,
         qkv_ref, ka_ref, va_ref, kb_ref, vb_ref, logf_ref, lft_ref) = refs
    i = pl.program_id(1)
    h = _modulate(x_ref[0], g_ref[...], sh_ref[0], sc_ref[0]).astype(BF)
    tm = h.shape[0]
    nh = logf_ref.shape[-1]

    qa = _dot(h, w_ref[:, 0:wa])
    ka = _dot(h, w_ref[:, wa:2 * wa])
    va = _dot(h, w_ref[:, 2 * wa:3 * wa])
    o = 3 * wa
    qb = _dot(h, w_ref[:, o:o + wb])
    kb = _dot(h, w_ref[:, o + wb:o + 2 * wb])
    vb = _dot(h, w_ref[:, o + 2 * wb:o + 3 * wb])
    ka_ref[0] = ka
    va_ref[0] = va
    kb_ref[0] = kb
    vb_ref[0] = vb
    logf = _log_forget(h, wf_ref, bf_ref)
    logf_ref[0] = logf[:, :nh]

    if not prompt:
        qkv_ref[0] = jnp.concatenate(
            [(qa * QK_SCALE).astype(BF), ka.astype(BF), va.astype(BF),
             (qb * QK_SCALE).astype(BF), kb.astype(BF), vb.astype(BF)], axis=-1)
        lft_ref[0] = logf.T[:nh, :]
        return

    @pl.when(i == 0)
    def _():
        carry_ref[...] = jnp.zeros(carry_ref.shape, F32)

    low = low_ref[...]
    hi, mid, lo = _split3(logf)
    f_cum = _dot(low, hi) + _dot(low, mid) + _dot(low, lo) + carry_ref[0:1, :]
    carry_ref[...] = jnp.broadcast_to(f_cum[tm - 1:tm, :], carry_ref.shape)
    nhi, nmid, nlo = _split3(-f_cum)
    k_aug = _dot(nhi, place_ref[0]) + _dot(nmid, place_ref[1]) + _dot(nlo, place_ref[2])
    q_ones = _lane_range_row(HEAD_DIM, HEAD_DIM + 3, 1.0)
    zero = jnp.zeros((1, LANES), F32)
    nhb = wb // HEAD_DIM
    blocks = (_head_blocks(qa * QK_SCALE, [q_ones] * nh)
              + _head_blocks(ka, [k_aug[:, n * LANES:(n + 1) * LANES] for n in range(nh)])
              + _head_blocks(qb * QK_SCALE, [zero] * nhb)
              + _head_blocks(kb, [zero] * nhb))
    qk_ref[0] = jnp.concatenate(blocks, axis=-1)
    ones = _ones_rows(tm)
    vt_ref[0] = jnp.concatenate([va.T.astype(BF), ones, vb.T.astype(BF), ones], axis=0)


def _lower_tri(t):
    r = jnp.arange(t)
    return (r[:, None] >= r[None, :]).astype(BF)


def _place_matrices(nh):
    s = jnp.arange(3)[:, None, None]
    r = jnp.arange(LANES)[None, :, None]
    c = jnp.arange(nh * LANES)[None, None, :]
    return ((r < nh) & (c == r * LANES + HEAD_DIM + s)).astype(BF)


def _inproj_even_call(x, g, shift, scale, w_main, w_f, b_f, keep, prompt):
    bx, sx, d = x.shape
    tm = min(ROW_TILE, sx)
    nt = sx // tm
    n = w_main.shape[1]
    wa = wb = n // 6
    nh = wa // HEAD_DIM
    nkeep = keep // tm
    tail = lambda b, i: (b, jnp.maximum(i - (nt - nkeep), 0), 0)
    row = lambda b, i: (b, i, 0)
    in_specs = [pl.BlockSpec((1, tm, d), row), _const_spec(g), _mod_spec(shift, tm), _mod_spec(scale, tm),
                _const_spec(w_main), _const_spec(w_f), _const_spec(b_f)]
    args = [x, g, shift, scale, w_main, w_f, b_f]
    f32_specs = [pl.BlockSpec((1, tm, wa), row), pl.BlockSpec((1, tm, wa), row),
                 pl.BlockSpec((1, tm, wb), tail), pl.BlockSpec((1, tm, wb), tail),
                 pl.BlockSpec((1, tm, nh), row)]
    f32_shapes = [jax.ShapeDtypeStruct((bx, sx, wa), F32), jax.ShapeDtypeStruct((bx, sx, wa), F32),
                  jax.ShapeDtypeStruct((bx, keep, wb), F32), jax.ShapeDtypeStruct((bx, keep, wb), F32),
                  jax.ShapeDtypeStruct((bx, sx, nh), F32)]
    scratch = []
    if prompt:
        low, place = _lower_tri(tm), _place_matrices(nh)
        in_specs += [_const_spec(low), _const_spec(place)]
        args += [low, place]
        nqk = 2 * (wa + wb) // HEAD_DIM * LANES
        nvt = wa + wb + 32
        out_specs = [pl.BlockSpec((1, tm, nqk), row), pl.BlockSpec((1, nvt, tm), lambda b, i: (b, 0, i))] + f32_specs
        out_shape = [jax.ShapeDtypeStruct((bx, sx, nqk), BF), jax.ShapeDtypeStruct((bx, nvt, sx), BF)] + f32_shapes
        scratch.append(pltpu.VMEM((8, LANES), F32))
    else:
        out_specs = [pl.BlockSpec((1, tm, n), row)] + f32_specs + [pl.BlockSpec((1, nh, tm), lambda b, i: (b, 0, i))]
        out_shape = [jax.ShapeDtypeStruct((bx, sx, n), BF)] + f32_shapes + [jax.ShapeDtypeStruct((bx, nh, sx), F32)]
    return pl.pallas_call(
        functools.partial(_inproj_even_body, wa=wa, wb=wb, prompt=prompt),
        grid=(bx, nt), in_specs=in_specs, out_specs=out_specs, out_shape=out_shape,
        scratch_shapes=scratch,
        compiler_params=_cparams(("arbitrary", "arbitrary")),
        name="inproj_even",
    )(*args)


def _inproj_odd_body(*refs, wc, prompt):
    if prompt:
        x_ref, g_ref, sh_ref, sc_ref, w_ref, qk_ref, vt_ref, k_ref, v_ref = refs
    else:
        x_ref, g_ref, sh_ref, sc_ref, w_ref, qkv_ref, k_ref, v_ref = refs
    i = pl.program_id(1)
    h = _modulate(x_ref[0], g_ref[...], sh_ref[0], sc_ref[0]).astype(BF)
    tm = h.shape[0]
    q = _dot(h, w_ref[:, 0:wc])
    k = _dot(h, w_ref[:, wc:2 * wc])
    v = _dot(h, w_ref[:, 2 * wc:3 * wc])
    k_ref[0] = k
    v_ref[0] = v
    if not prompt:
        qkv_ref[0] = jnp.concatenate([(q * QK_SCALE).astype(BF), k.astype(BF), v.astype(BF)], axis=-1)
        return
    nh = wc // LANES
    pos = (i * tm + lax.broadcasted_iota(jnp.int32, (tm, LANES), 0)).astype(F32)
    pos_hi = pos.astype(BF).astype(F32)
    lane = lax.broadcasted_iota(jnp.int32, (tm, LANES), 1)
    k_aug = jnp.where(lane == HEAD_DIM, pos_hi, jnp.where(lane == HEAD_DIM + 1, pos - pos_hi, 0.0))
    q_augs = []
    for hh in range(nh):
        q_augs += [_lane_range_row(HEAD_DIM, HEAD_DIM + 2, _alibi_slope(hh, nh))] * 2
    blocks = _head_blocks(q * QK_SCALE, q_augs) + _head_blocks(k, [k_aug] * (2 * nh))
    qk_ref[0] = jnp.concatenate(blocks, axis=-1)
    vt_ref[0] = jnp.concatenate([v.T.astype(BF), _ones_rows(tm)], axis=0)


def _alibi_slope(h, nh):
    assert 8 % nh == 0, "slopes must be exact powers of two to ride in bf16"
    return 2.0 ** (-8.0 * (h + 1) / nh)


def _inproj_odd_call(x, g, shift, scale, w, prompt):
    bx, sx, d = x.shape
    tm = min(ROW_TILE, sx)
    n = w.shape[1]
    wc = n // 3
    row = lambda b, i: (b, i, 0)
    f32_specs = [pl.BlockSpec((1, tm, wc), row), pl.BlockSpec((1, tm, wc), row)]
    f32_shapes = [jax.ShapeDtypeStruct((bx, sx, wc), F32), jax.ShapeDtypeStruct((bx, sx, wc), F32)]
    if prompt:
        nqk = 2 * wc // HEAD_DIM * LANES
        out_specs = [pl.BlockSpec((1, tm, nqk), row), pl.BlockSpec((1, wc + 16, tm), lambda b, i: (b, 0, i))]
        out_shape = [jax.ShapeDtypeStruct((bx, sx, nqk), BF), jax.ShapeDtypeStruct((bx, wc + 16, sx), BF)]
    else:
        out_specs = [pl.BlockSpec((1, tm, n), row)]
        out_shape = [jax.ShapeDtypeStruct((bx, sx, n), BF)]
    return pl.pallas_call(
        functools.partial(_inproj_odd_body, wc=wc, prompt=prompt),
        grid=(bx, sx // tm),
        in_specs=[pl.BlockSpec((1, tm, d), row), _const_spec(g), _mod_spec(shift, tm), _mod_spec(scale, tm),
                  _const_spec(w)],
        out_specs=out_specs + f32_specs, out_shape=out_shape + f32_shapes,
        compiler_params=_cparams(("arbitrary", "arbitrary")),
        name="inproj_odd",
    )(x, g, shift, scale, w)


def _tflash_update(st, v_aug, mask, extra, m_ref, acc_ref, idx):
    if extra is not None:
        st = st - extra
    if mask is not None:
        st = jnp.where(mask, st, NEG_INF)
    m_prev = m_ref[idx]
    m_new = jnp.maximum(m_prev, jnp.max(st, axis=0, keepdims=True))
    alpha = jnp.exp(m_prev - m_new)
    p = jnp.exp(st - m_new).astype(BF)
    acc_ref[idx] = alpha * acc_ref[idx] + _dot(v_aug, p)
    m_ref[idx] = m_new


def _tflash_init(m_ref, acc_ref):
    m_ref[...] = jnp.full(m_ref.shape, NEG_INF, F32)
    acc_ref[...] = jnp.zeros(acc_ref.shape, F32)


def _fox_prompt_body(q_ref, k_ref, vt_ref, o_ref, m_ref, acc_ref, st_ref, *, nh):
    i = pl.program_id(1)
    j = pl.program_id(2)
    hd = HEAD_DIM

    @pl.when(j == 0)
    def _():
        _tflash_init(m_ref, acc_ref)

    def step(masked):
        q = q_ref[0]
        k = k_ref[0]
        vt = vt_ref[0]
        tq, tk = q.shape[0], k.shape[0]
        ones = vt[nh * hd:, :]
        mask = None
        if masked:
            mask = lax.broadcasted_iota(jnp.int32, (tk, tq), 0) <= lax.broadcasted_iota(jnp.int32, (tk, tq), 1)
        def scores(h):
            st_ref[h % 2] = _dot_nt(k[:, h * LANES:(h + 1) * LANES], q[:, h * LANES:(h + 1) * LANES])

        scores(0)
        for h in range(nh):
            if h + 1 < nh:
                scores(h + 1)
            v_aug = jnp.concatenate([vt[h * hd:(h + 1) * hd, :], ones], axis=0)
            _tflash_update(st_ref[h % 2], v_aug, mask, None, m_ref, acc_ref, h)

    @pl.when(j < i)
    def _():
        step(False)

    @pl.when(j == i)
    def _():
        step(True)

    @pl.when(j == pl.num_programs(2) - 1)
    def _():
        ot = jnp.concatenate([acc_ref[h][:hd] / acc_ref[h][hd:hd + 1] for h in range(nh)], axis=0)
        o_ref[0] = ot.T.astype(BF)


def _fox_prompt_call(qk, vt, wa):
    b, s, _ = qk.shape
    nh = wa // HEAD_DIM
    t = min(ATTN_TILE, s)
    nq = s // t
    wq = nh * LANES
    return pl.pallas_call(
        functools.partial(_fox_prompt_body, nh=nh),
        grid=(b, nq, nq),
        in_specs=[pl.BlockSpec((1, t, wq), lambda b, i, j: (b, i, 0)),
                  pl.BlockSpec((1, t, wq), lambda b, i, j: (b, jnp.minimum(j, i), 1)),
                  pl.BlockSpec((1, wa + 16, t), lambda b, i, j: (b, 0, jnp.minimum(j, i)))],
        out_specs=pl.BlockSpec((1, t, wa), lambda b, i, j: (b, i, 0)),
        out_shape=jax.ShapeDtypeStruct((b, s, wa), BF),
        scratch_shapes=[pltpu.VMEM((nh, 1, t), F32), pltpu.VMEM((nh, HEAD_DIM + 16, t), F32),
                        pltpu.VMEM((2, t, t), F32)],
        compiler_params=_cparams(("arbitrary", "arbitrary", "arbitrary")),
        name="fox_prompt",
    )(qk, qk, vt)


def _band_prompt_body(q_ref, k0_ref, k1_ref, k2_ref, v0_ref, v1_ref, v2_ref, bias_ref, o_ref, st_ref, *, nh):
    i = pl.program_id(1)
    hd = HEAD_DIM
    q = q_ref[0]
    tq = q.shape[0]
    k = jnp.concatenate([k0_ref[0], k1_ref[0], k2_ref[0]], axis=0)
    vts = [v0_ref[0], v1_ref[0], v2_ref[0]]
    in_seq = lax.broadcasted_iota(jnp.int32, (3 * tq, tq), 0) >= (2 - i) * tq
    outs = []
    def scores(h):
        st_ref[h % 2] = _dot_nt(k[:, h * LANES:(h + 1) * LANES], q[:, h * LANES:(h + 1) * LANES])

    scores(0)
    for h in range(nh):
        if h + 1 < nh:
            scores(h + 1)
        st = st_ref[h % 2] + bias_ref[h]
        st = jnp.where(in_seq, st, NEG_INF)
        p = jnp.exp(st - jnp.max(st, axis=0, keepdims=True)).astype(BF)
        acc = None
        for w, vt in enumerate(vts):
            v_aug = jnp.concatenate([vt[h * hd:(h + 1) * hd, :], vt[nh * hd:, :]], axis=0)
            part = _dot(v_aug, p[w * tq:(w + 1) * tq, :])
            acc = part if acc is None else acc + part
        outs.append(acc[:hd] / acc[hd:hd + 1])
    o_ref[0] = jnp.concatenate(outs, axis=0).T.astype(BF)


def _band_bias_prompt(rel_bias, tq):
    nh = rel_bias.shape[0]
    period = 4 * tq
    t = jnp.arange(period)
    t = jnp.where(t < 3 * tq, t, t - period)
    u = rel_bias[:, jnp.clip(2 * tq - t, -MAX_REL, MAX_REL) + MAX_REL].astype(F32)
    skew = jnp.tile(u, (1, tq))[:, :tq * (period - 1)].reshape(nh, tq, period - 1)
    bias = skew[:, :, :3 * tq]
    r = jnp.arange(tq)[:, None]
    c = jnp.arange(3 * tq)[None, :]
    valid = (c // CHUNK >= r // CHUNK) & (c // CHUNK <= r // CHUNK + BAND_CHUNKS)
    return jnp.where(valid[None], bias, NEG_INF)


def _band_prompt_call(qk, vt, rel_bias, wa, wb):
    b, s, _ = qk.shape
    nh = wb // HEAD_DIM
    tq = BAND_Q_TILE
    assert s % tq == 0 and wa == wb
    bias_t = _band_bias_prompt(rel_bias, tq).transpose(0, 2, 1)
    wq = nh * LANES
    cq, ck = 2, 3

    def k_spec(back):
        return pl.BlockSpec((1, tq, wq), lambda b, i: (b, jnp.maximum(i - back, 0), ck))

    def v_spec(back):
        return pl.BlockSpec((1, wb + 16, tq), lambda b, i: (b, 1, jnp.maximum(i - back, 0)))

    return pl.pallas_call(
        functools.partial(_band_prompt_body, nh=nh),
        grid=(b, s // tq),
        in_specs=[pl.BlockSpec((1, tq, wq), lambda b, i: (b, i, cq)),
                  k_spec(2), k_spec(1), k_spec(0), v_spec(2), v_spec(1), v_spec(0),
                  _const_spec(bias_t)],
        out_specs=pl.BlockSpec((1, tq, wb), lambda b, i: (b, i, 0)),
        out_shape=jax.ShapeDtypeStruct((b, s, wb), BF),
        scratch_shapes=[pltpu.VMEM((2, 3 * tq, tq), F32)],
        compiler_params=_cparams(("arbitrary", "arbitrary")),
        name="band_prompt",
    )(qk, qk, qk, qk, vt, vt, vt, bias_t)


def _lambda(lam_ref, lam_init):
    lp = lam_ref[...]
    a = jnp.sum(lp[0:1] * lp[1:2], axis=-1, keepdims=True)
    b = jnp.sum(lp[2:3] * lp[3:4], axis=-1, keepdims=True)
    return jnp.exp(a) - jnp.exp(b) + lam_init


def _diff_finalize(l_ref, acc_ref, lam_ref, gsub_ref, nh, lam_init):
    lam = _lambda(lam_ref, lam_init)
    g = gsub_ref[...]
    outs = []
    for h in range(nh):
        o = acc_ref[2 * h] / l_ref[2 * h] - lam * (acc_ref[2 * h + 1] / l_ref[2 * h + 1])
        o = o * lax.rsqrt(jnp.mean(o * o, axis=-1, keepdims=True) + RMS_EPS)
        outs.append(((o * g) * (1.0 - lam_init)).astype(BF))
    return jnp.concatenate(outs, axis=-1)


def _diff_prompt_body(q_ref, k_ref, vt_ref, lam_ref, gcol_ref, o_ref, m_ref, acc_ref, st_ref, *, nh, lam_init):
    i = pl.program_id(1)
    j = pl.program_id(2)
    dv = 2 * HEAD_DIM

    @pl.when(j == 0)
    def _():
        _tflash_init(m_ref, acc_ref)

    def step(masked):
        q = q_ref[0]
        k = k_ref[0]
        vt = vt_ref[0]
        tq, tk = q.shape[0], k.shape[0]
        ones = vt[nh * dv:, :]
        mask = None
        if masked:
            key = lax.broadcasted_iota(jnp.int32, (tk, tq), 0)
            qry = lax.broadcasted_iota(jnp.int32, (tk, tq), 1)
            mask = (key >> CHUNK_SHIFT) <= (qry >> CHUNK_SHIFT)
            ahead = jnp.maximum(key - qry, 0).astype(F32)
        def scores(n):
            st_ref[n % 2] = _dot_nt(k[:, n * LANES:(n + 1) * LANES], q[:, n * LANES:(n + 1) * LANES])

        scores(0)
        for h in range(nh):
            v_aug = jnp.concatenate([vt[h * dv:(h + 1) * dv, :], ones], axis=0)
            extra = (2.0 * _alibi_slope(h, nh)) * ahead if masked else None
            for u in range(2):
                n = 2 * h + u
                if n + 1 < 2 * nh:
                    scores(n + 1)
                _tflash_update(st_ref[n % 2], v_aug, mask, extra, m_ref, acc_ref, n)

    @pl.when(j < i)
    def _():
        step(False)

    @pl.when(j == i)
    def _():
        step(True)

    @pl.when(j == pl.num_programs(2) - 1)
    def _():
        lam = _lambda(lam_ref, lam_init)
        g = gcol_ref[...]
        outs = []
        for h in range(nh):
            a1 = acc_ref[2 * h]
            a2 = acc_ref[2 * h + 1]
            o = a1[:dv] / a1[dv:dv + 1] - lam * (a2[:dv] / a2[dv:dv + 1])
            o = o * lax.rsqrt(jnp.mean(o * o, axis=0, keepdims=True) + RMS_EPS)
            outs.append((o * g) * (1.0 - lam_init))
        o_ref[0] = jnp.concatenate(outs, axis=0).T.astype(BF)


def _diff_prompt_call(qk, vt, lam_par, g_col, wc, lam_init):
    b, s, _ = qk.shape
    nh = wc // (2 * HEAD_DIM)
    t = min(ATTN_TILE, s)
    nq = s // t
    wq = 2 * nh * LANES
    return pl.pallas_call(
        functools.partial(_diff_prompt_body, nh=nh, lam_init=lam_init),
        grid=(b, nq, nq),
        in_specs=[pl.BlockSpec((1, t, wq), lambda b, i, j: (b, i, 0)),
                  pl.BlockSpec((1, t, wq), lambda b, i, j: (b, jnp.minimum(j, i), 1)),
                  pl.BlockSpec((1, wc + 16, t), lambda b, i, j: (b, 0, jnp.minimum(j, i))),
                  pl.BlockSpec(lam_par.shape, lambda b, i, j: (0, 0)),
                  pl.BlockSpec(g_col.shape, lambda b, i, j: (0, 0))],
        out_specs=pl.BlockSpec((1, t, wc), lambda b, i, j: (b, i, 0)),
        out_shape=jax.ShapeDtypeStruct((b, s, wc), BF),
        scratch_shapes=[pltpu.VMEM((2 * nh, 1, t), F32), pltpu.VMEM((2 * nh, 2 * HEAD_DIM + 16, t), F32),
                        pltpu.VMEM((2, t, t), F32)],
        compiler_params=_cparams(("arbitrary", "arbitrary", "arbitrary")),
        name="diff_prompt",
    )(qk, qk, vt, lam_par, g_col)


def _cumsum_body(x_ref, up_ref, o_ref, carry_ref):
    @pl.when(pl.program_id(1) == 0)
    def _():
        carry_ref[...] = jnp.zeros(carry_ref.shape, F32)
    ft = _cumsum_lanes(x_ref[0], up_ref[...]) + carry_ref[:, 0:1]
    o_ref[0] = ft
    carry_ref[...] = jnp.broadcast_to(ft[:, -1:], carry_ref.shape)


def _cumsum_call(x):
    b, nh, length = x.shape
    t = 512
    up = _upper_tri(t)
    return pl.pallas_call(
        _cumsum_body,
        grid=(b, length // t),
        in_specs=[pl.BlockSpec((1, nh, t), lambda b, j: (b, 0, j)), pl.BlockSpec((t, t), lambda b, j: (0, 0))],
        out_specs=pl.BlockSpec((1, nh, t), lambda b, j: (b, 0, j)),
        out_shape=jax.ShapeDtypeStruct(x.shape, F32),
        scratch_shapes=[pltpu.VMEM((nh, LANES), F32)],
        compiler_params=_cparams(("arbitrary", "arbitrary")),
        name="logf_cumsum",
    )(x, up)


def _fox_sample_body(q_ref, kn_ref, vn_ref, ck_ref, cv_ref, ftc_ref, ftn_ref, o_ref, m_ref, l_ref, acc_ref, *, nh):
    j = pl.program_id(1)
    hd = HEAD_DIM
    q = q_ref[0]
    t = q.shape[0]

    @pl.when(j == 0)
    def _():
        _osm_init(m_ref, l_ref, acc_ref)

    k = ck_ref[0].astype(BF)
    v = cv_ref[0].astype(BF)
    ft = ftc_ref[0]
    for h in range(nh):
        sl = slice(h * hd, (h + 1) * hd)
        s = _dot_nt(q[:, sl], k[:, sl]) - ft[h:h + 1, :]
        _osm_update(s, v[:, sl], m_ref, l_ref, acc_ref, h)

    @pl.when(j == pl.num_programs(1) - 1)
    def _():
        kn = kn_ref[0]
        vn = vn_ref[0]
        ftn = ftn_ref[0][:, :t]
        visible = lax.broadcasted_iota(jnp.int32, (t, t), 1) <= lax.broadcasted_iota(jnp.int32, (t, t), 0)
        outs = []
        for h in range(nh):
            sl = slice(h * hd, (h + 1) * hd)
            s = _dot_nt(q[:, sl], kn[:, sl]) - ftn[h:h + 1, :]
            s = jnp.where(visible, s, NEG_INF)
            _osm_update(s, vn[:, sl], m_ref, l_ref, acc_ref, h)
            outs.append((acc_ref[h] / l_ref[h]).astype(BF))
        o_ref[0] = jnp.concatenate(outs, axis=-1)


def _fox_sample_call(qkv, cache_k, cache_v, ft_pad, wa):
    b, t, _ = qkv.shape
    p_len = cache_k.shape[1]
    nh = wa // HEAD_DIM
    tk = min(CACHE_TILE_A, p_len)
    assert p_len % tk == 0 and p_len % LANES == 0 and t <= LANES
    return pl.pallas_call(
        functools.partial(_fox_sample_body, nh=nh),
        grid=(b, p_len // tk),
        in_specs=[pl.BlockSpec((1, t, wa), lambda b, j: (b, 0, 0)),
                  pl.BlockSpec((1, t, wa), lambda b, j: (b, 0, 1)),
                  pl.BlockSpec((1, t, wa), lambda b, j: (b, 0, 2)),
                  pl.BlockSpec((1, tk, wa), lambda b, j: (b, j, 0)),
                  pl.BlockSpec((1, tk, wa), lambda b, j: (b, j, 0)),
                  pl.BlockSpec((1, nh, tk), lambda b, j: (b, 0, j)),
                  pl.BlockSpec((1, nh, LANES), lambda b, j: (b, 0, p_len // LANES))],
        out_specs=pl.BlockSpec((1, t, wa), lambda b, j: (b, 0, 0)),
        out_shape=jax.ShapeDtypeStruct((b, t, wa), BF),
        scratch_shapes=[pltpu.VMEM((nh, t, 1), F32), pltpu.VMEM((nh, t, 1), F32),
                        pltpu.VMEM((nh, t, HEAD_DIM), F32)],
        compiler_params=_cparams(("arbitrary", "arbitrary")),
        name="fox_sample",
    )(qkv, qkv, qkv, cache_k, cache_v, ft_pad, ft_pad)


def _band_sample_body(q_ref, kn_ref, vn_ref, bk_ref, bv_ref, bias_b_ref, bias_n_ref, o_ref, *, nh):
    hd = HEAD_DIM
    q = q_ref[0]
    kn = kn_ref[0]
    vn = vn_ref[0]
    kb = bk_ref[0].astype(BF)
    vb = bv_ref[0].astype(BF)
    outs = []
    for h in range(nh):
        sl = slice(h * hd, (h + 1) * hd)
        sb = _dot_nt(q[:, sl], kb[:, sl]) + bias_b_ref[h]
        sn = _dot_nt(q[:, sl], kn[:, sl]) + bias_n_ref[h]
        m = jnp.maximum(jnp.max(sb, axis=-1, keepdims=True), jnp.max(sn, axis=-1, keepdims=True))
        pb = jnp.exp(sb - m)
        pn = jnp.exp(sn - m)
        l = jnp.sum(pb, axis=-1, keepdims=True) + jnp.sum(pn, axis=-1, keepdims=True)
        o = _dot(pb.astype(BF), vb[:, sl]) + _dot(pn.astype(BF), vn[:, sl])
        outs.append((o / l).astype(BF))
    o_ref[0] = jnp.concatenate(outs, axis=-1)


def _band_sample_call(qkv, buf_k, buf_v, rel_bias, wb):
    b, t, _ = qkv.shape
    lb = buf_k.shape[1]
    nh = wb // HEAD_DIM
    k_pos = jnp.concatenate([jnp.arange(-lb, 0), jnp.arange(t)])
    rel = jnp.arange(t)[:, None] - k_pos[None, :]
    bias = rel_bias[:, jnp.clip(rel, -MAX_REL, MAX_REL) + MAX_REL].astype(F32)
    bias_b, bias_n = bias[:, :, :lb], bias[:, :, lb:]
    return pl.pallas_call(
        functools.partial(_band_sample_body, nh=nh),
        grid=(b,),
        in_specs=[pl.BlockSpec((1, t, wb), lambda b: (b, 0, 3)),
                  pl.BlockSpec((1, t, wb), lambda b: (b, 0, 4)),
                  pl.BlockSpec((1, t, wb), lambda b: (b, 0, 5)),
                  pl.BlockSpec((1, lb, wb), lambda b: (b, 0, 0)),
                  pl.BlockSpec((1, lb, wb), lambda b: (b, 0, 0)),
                  pl.BlockSpec(bias_b.shape, lambda b: (0, 0, 0)),
                  pl.BlockSpec(bias_n.shape, lambda b: (0, 0, 0))],
        out_specs=pl.BlockSpec((1, t, wb), lambda b: (b, 0, 0)),
        out_shape=jax.ShapeDtypeStruct((b, t, wb), BF),
        compiler_params=_cparams(("arbitrary",)),
        name="band_sample",
    )(qkv, qkv, qkv, buf_k, buf_v, bias_b, bias_n)


def _diff_sample_body(q_ref, kn_ref, vn_ref, ck_ref, cv_ref, lam_ref, gsub_ref, o_ref, m_ref, l_ref, acc_ref,
                      *, nh, lam_init, p_len):
    j = pl.program_id(1)
    hd = HEAD_DIM
    q = q_ref[0]
    t = q.shape[0]

    @pl.when(j == 0)
    def _():
        _osm_init(m_ref, l_ref, acc_ref)

    def attend(k, v, dist):
        for h in range(nh):
            slope = 2.0 ** (-8.0 * (h + 1) / nh)
            bias = slope * dist
            vh = v[:, 2 * h * hd:(2 * h + 2) * hd]
            for u in range(2):
                sl = slice((2 * h + u) * hd, (2 * h + u + 1) * hd)
                s = _dot_nt(q[:, sl], k[:, sl]) - bias
                _osm_update(s, vh, m_ref, l_ref, acc_ref, 2 * h + u)

    tk = ck_ref.shape[1]
    row = lax.broadcasted_iota(jnp.int32, (t, tk), 0)
    col = lax.broadcasted_iota(jnp.int32, (t, tk), 1)
    attend(ck_ref[0].astype(BF), cv_ref[0].astype(BF), (p_len + row - j * tk - col).astype(F32))

    @pl.when(j == pl.num_programs(1) - 1)
    def _():
        rown = lax.broadcasted_iota(jnp.int32, (t, t), 0)
        coln = lax.broadcasted_iota(jnp.int32, (t, t), 1)
        attend(kn_ref[0], vn_ref[0], jnp.abs(rown - coln).astype(F32))
        o_ref[0] = _diff_finalize(l_ref, acc_ref, lam_ref, gsub_ref, nh, lam_init)


def _diff_sample_call(qkv, cache_k, cache_v, lam_par, g_sub, wc, lam_init):
    b, t, _ = qkv.shape
    p_len = cache_k.shape[1]
    assert (p_len // CHUNK) * CHUNK == p_len and t <= CHUNK
    nh = wc // (2 * HEAD_DIM)
    tk = min(CACHE_TILE_C, p_len)
    return pl.pallas_call(
        functools.partial(_diff_sample_body, nh=nh, lam_init=lam_init, p_len=p_len),
        grid=(b, p_len // tk),
        in_specs=[pl.BlockSpec((1, t, wc), lambda b, j: (b, 0, 0)),
                  pl.BlockSpec((1, t, wc), lambda b, j: (b, 0, 1)),
                  pl.BlockSpec((1, t, wc), lambda b, j: (b, 0, 2)),
                  pl.BlockSpec((1, tk, wc), lambda b, j: (b, j, 0)),
                  pl.BlockSpec((1, tk, wc), lambda b, j: (b, j, 0)),
                  pl.BlockSpec(lam_par.shape, lambda b, j: (0, 0)),
                  pl.BlockSpec(g_sub.shape, lambda b, j: (0, 0))],
        out_specs=pl.BlockSpec((1, t, wc), lambda b, j: (b, 0, 0)),
        out_shape=jax.ShapeDtypeStruct((b, t, wc), BF),
        scratch_shapes=[pltpu.VMEM((2 * nh, t, 1), F32), pltpu.VMEM((2 * nh, t, 1), F32),
                        pltpu.VMEM((2 * nh, t, 2 * HEAD_DIM), F32)],
        compiler_params=_cparams(("arbitrary", "arbitrary")),
        name="diff_sample",
    )(qkv, qkv, qkv, cache_k, cache_v, lam_par, g_sub)


def _out_ffn_body(x_ref, o_ref, gm_ref, g_ref, sh_ref, sc_ref, gf_ref, wo_ref, w1_ref, w3_ref, w2_ref, y_ref,
                  *, f_chunk):
    x1 = x_ref[0] + gm_ref[0] * _dot(o_ref[0], wo_ref[...])
    h = _modulate(x1, g_ref[...], sh_ref[0], sc_ref[0]).astype(BF)
    f_total = w1_ref.shape[1]
    f = None
    for c0 in range(0, f_total, f_chunk):
        a = _dot(h, w1_ref[:, c0:c0 + f_chunk])
        g = _dot(h, w3_ref[:, c0:c0 + f_chunk])
        u = ((a * _sigmoid(a)) * g).astype(BF)
        part = _dot(u, w2_ref[c0:c0 + f_chunk, :])
        f = part if f is None else f + part
    y_ref[0] = x1 + gf_ref[0] * f


def _out_ffn_call(x, o, gate_m, g, shift, scale, gate_f, wo, w1, w3, w2):
    bx, sx, d = x.shape
    tm = min(ROW_TILE, sx)
    f_total = w1.shape[1]
    f_chunk = f_total // 2 if (f_total // 2) % LANES == 0 else f_total
    row = lambda b, i: (b, i, 0)
    return pl.pallas_call(
        functools.partial(_out_ffn_body, f_chunk=f_chunk),
        grid=(bx, sx // tm),
        in_specs=[pl.BlockSpec((1, tm, d), row), pl.BlockSpec((1, tm, o.shape[-1]), row),
                  _mod_spec(gate_m, tm), _const_spec(g), _mod_spec(shift, tm), _mod_spec(scale, tm),
                  _mod_spec(gate_f, tm), _const_spec(wo), _const_spec(w1), _const_spec(w3), _const_spec(w2)],
        out_specs=pl.BlockSpec((1, tm, d), row),
        out_shape=jax.ShapeDtypeStruct((bx, sx, d), F32),
        compiler_params=_cparams(("arbitrary", "arbitrary")),
        name="out_ffn",
    )(x, o, gate_m, g, shift, scale, gate_f, wo, w1, w3, w2)


def _store_tile_rows(ref, x):
    rows, d = x.shape
    nseg = d // LANES
    for s in range(nseg):
        ref[pl.ds(s, rows, stride=nseg), :] = x[:, s * LANES:(s + 1) * LANES]


def _load_tile_rows(ref, rows, d):
    nseg = d // LANES
    return jnp.concatenate([ref[pl.ds(s, rows, stride=nseg), :] for s in range(nseg)], axis=-1)


def _out_router_body(x_ref, o_ref, gm_ref, g_ref, sh_ref, sc_ref, wo_ref, wr_ref, h_in_ref,
                     x1_ref, h_ref, route_ref, *, n_exp):
    del h_in_ref
    x1 = x_ref[0] + gm_ref[0] * _dot(o_ref[0], wo_ref[...])
    x1_ref[0] = x1
    h = _modulate(x1, g_ref[...], sh_ref[0], sc_ref[0])
    hb = h.astype(BF)
    _store_tile_rows(h_ref, hb.astype(F32))
    logits = _dot(hb, wr_ref[0]) + _dot(hb, wr_ref[1]) + _dot(hb, wr_ref[2])
    col = lax.broadcasted_iota(jnp.int32, logits.shape, 1)
    lg = jnp.where(col < n_exp, logits, NEG_INF)
    e = jnp.exp(lg - jnp.max(lg, axis=-1, keepdims=True))
    probs = e / jnp.sum(e, axis=-1, keepdims=True)
    p1 = jnp.max(probs, axis=-1, keepdims=True)
    i1 = jnp.min(jnp.where(probs == p1, col, LANES), axis=-1, keepdims=True)
    rest = jnp.where(col == i1, -1.0, probs)
    p2 = jnp.max(rest, axis=-1, keepdims=True)
    i2 = jnp.min(jnp.where(rest == p2, col, LANES), axis=-1, keepdims=True)
    tot = p1 + p2
    route = jnp.where(col == 0, i1.astype(F32),
                      jnp.where(col == 1, i2.astype(F32),
                                jnp.where(col == 2, p1 / tot, jnp.where(col == 3, p2 / tot, 0.0))))
    route_ref[0] = route


def _out_router_call(x, o, gate_m, g, shift, scale, wo, wr3, n_exp, tok_offset, h_all):
    bx, sx, d = x.shape
    tm = min(ROW_TILE, sx)
    nt = sx // tm
    nseg = d // LANES
    assert tok_offset % tm == 0
    row = lambda b, i: (b, i, 0)
    return pl.pallas_call(
        functools.partial(_out_router_body, n_exp=n_exp),
        grid=(bx, nt),
        in_specs=[pl.BlockSpec((1, tm, d), row), pl.BlockSpec((1, tm, o.shape[-1]), row),
                  _mod_spec(gate_m, tm), _const_spec(g), _mod_spec(shift, tm), _mod_spec(scale, tm),
                  _const_spec(wo), _const_spec(wr3), pl.BlockSpec(memory_space=pl.ANY)],
        out_specs=[pl.BlockSpec((1, tm, d), row),
                   pl.BlockSpec((tm * nseg, LANES), lambda b, i: (tok_offset // tm + b * nt + i, 0)),
                   pl.BlockSpec((1, tm, LANES), row)],
        out_shape=[jax.ShapeDtypeStruct((bx, sx, d), F32),
                   jax.ShapeDtypeStruct(h_all.shape, F32),
                   jax.ShapeDtypeStruct((bx, sx, LANES), F32)],
        input_output_aliases={8: 1},
        compiler_params=_cparams(("arbitrary", "arbitrary")),
        name="out_router",
    )(x, o, gate_m, g, shift, scale, wo, wr3, h_all)


def _moe_ffn_body(te_ref, nv_ref, idx_ref, idx_next_ref, h_ref, w1_ref, w3_ref, w2_ref, y_ref, xbuf, xb16, acc, sem,
                  *, nseg, nf):
    t = pl.program_id(0)
    j = pl.program_id(1)
    nv = nv_ref[0]
    tm, d = xb16.shape
    slot = lax.rem(t, 2)

    def issue(rows_ref, s):
        def body(r, c):
            src = pl.multiple_of(rows_ref[0, 0, r], nseg)
            dst = pl.multiple_of(r * nseg, nseg)
            pltpu.make_async_copy(h_ref.at[pl.ds(src, nseg)], xbuf.at[s, pl.ds(dst, nseg)], sem.at[s]).start()
            return c
        lax.fori_loop(0, tm, body, 0, unroll=8)

    @pl.when((j == 0) & (t == 0))
    def _():
        issue(idx_ref, 0)

    @pl.when((j == 0) & (t + 1 < nv))
    def _():
        issue(idx_next_ref, 1 - slot)

    @pl.when((j == 0) & (t < nv))
    def _():
        pltpu.make_async_copy(h_ref.at[pl.ds(0, tm * nseg)], xbuf.at[slot], sem.at[slot]).wait()
        xb16[...] = _load_tile_rows(xbuf.at[slot], tm, d).astype(BF)

    @pl.when(t < nv)
    def _():
        xb = xb16[...]
        a = _dot(xb, w1_ref[0])
        g = _dot(xb, w3_ref[0])
        u = ((a * _sigmoid(a)) * g).astype(BF)
        part = _dot(u, w2_ref[0])

        if nf == 1:
            _store_tile_rows(y_ref, part)
        else:
            @pl.when(j == 0)
            def _():
                acc[...] = part

            @pl.when((j > 0) & (j < nf - 1))
            def _():
                acc[...] += part

            @pl.when(j == nf - 1)
            def _():
                _store_tile_rows(y_ref, acc[...] + part)

    @pl.when((t >= nv) & (j == nf - 1))
    def _():
        y_ref[...] = jnp.zeros(y_ref.shape, F32)


def _moe_ffn_call(h_all, row_tok, tile_e, n_valid, w1, w3, w2):
    n_rows = row_tok.shape[0]
    d = w1.shape[1]
    nseg = d // LANES
    f_total = w1.shape[2]
    tf = MOE_F_TILE if f_total % MOE_F_TILE == 0 else f_total
    nf = f_total // tf
    tm = MOE_ROW_TILE
    n_tiles = n_rows // tm
    idx3 = (row_tok * nseg).reshape(n_tiles, 1, tm)

    def fcol(t, j, te, nv):
        return jnp.where(t < nv[0], j, nf - 1)

    grid_spec = pltpu.PrefetchScalarGridSpec(
        num_scalar_prefetch=2,
        grid=(n_tiles, nf),
        in_specs=[pl.BlockSpec((1, 1, tm), lambda t, j, te, nv: (t, 0, 0), memory_space=pltpu.SMEM),
                  pl.BlockSpec((1, 1, tm), lambda t, j, te, nv: (jnp.minimum(t + 1, n_tiles - 1), 0, 0),
                               memory_space=pltpu.SMEM),
                  pl.BlockSpec(memory_space=pl.ANY),
                  pl.BlockSpec((1, d, tf), lambda t, j, te, nv: (te[t], 0, fcol(t, j, te, nv))),
                  pl.BlockSpec((1, d, tf), lambda t, j, te, nv: (te[t], 0, fcol(t, j, te, nv))),
                  pl.BlockSpec((1, tf, d), lambda t, j, te, nv: (te[t], fcol(t, j, te, nv), 0))],
        out_specs=pl.BlockSpec((tm * nseg, LANES), lambda t, j, te, nv: (t, 0)),
        scratch_shapes=[pltpu.VMEM((2, tm * nseg, LANES), F32), pltpu.VMEM((tm, d), BF), pltpu.VMEM((tm, d), F32),
                        pltpu.SemaphoreType.DMA((2,))],
    )
    return pl.pallas_call(
        functools.partial(_moe_ffn_body, nseg=nseg, nf=nf),
        grid_spec=grid_spec,
        out_shape=jax.ShapeDtypeStruct((n_rows * nseg, LANES), F32),
        compiler_params=_cparams(("arbitrary", "arbitrary")),
        name="moe_ffn",
    )(tile_e, n_valid, idx3, idx3, h_all, w1, w3, w2)


def _combine_body(d0_ref, d1_ref, x_ref, gf_ref, w_ref, g_ref, yb_ref, o_ref, buf, sem):
    t = pl.program_id(0)
    nt = pl.num_programs(0)
    _, tc, d = x_ref.shape
    nseg = d // LANES

    def row_copy(idx_ref, r, slot, k):
        src = pl.multiple_of(idx_ref[0, 0, 2 * r + k], nseg)
        dst = pl.multiple_of(r * nseg, nseg)
        return pltpu.make_async_copy(yb_ref.at[pl.ds(src, nseg)], buf.at[slot, k, pl.ds(dst, nseg)], sem.at[slot])

    def issue(idx_ref, slot):
        def body(r, c):
            row_copy(idx_ref, r, slot, 0).start()
            row_copy(idx_ref, r, slot, 1).start()
            return c
        lax.fori_loop(0, tc, body, 0, unroll=4)

    slot = lax.rem(t, 2)

    @pl.when(t == 0)
    def _():
        issue(d0_ref, 0)

    @pl.when(t + 1 < nt)
    def _():
        issue(d1_ref, 1 - slot)

    for k in range(TOP_K):
        pltpu.make_async_copy(yb_ref.at[pl.ds(0, tc * nseg)], buf.at[slot, k], sem.at[slot]).wait()
    w = w_ref[0]
    f = (w[:, 2:3] * _load_tile_rows(buf.at[slot, 0], tc, d)
         + w[:, 3:4] * _load_tile_rows(buf.at[slot, 1], tc, d))
    y = x_ref[0] + gf_ref[0] * f
    y = y * lax.rsqrt(jnp.mean(y * y, axis=-1, keepdims=True) + RMS_EPS)
    o_ref[0] = y * g_ref[...]


def _combine_call(x1, gate_f, route, g_final, yb, dest):
    bx, sx, d = x1.shape
    nseg = d // LANES
    tc = min(COMBINE_TILE, sx)
    nps = sx // tc
    nt = bx * nps
    dest3 = (dest * nseg).reshape(nt, 1, 2 * tc)
    row = lambda t: (t // nps, t % nps, 0)
    if gate_f.shape[1] == 1:
        gate_spec = pl.BlockSpec((1, 1, d), lambda t: (t // nps, 0, 0))
    else:
        gate_spec = pl.BlockSpec((1, tc, d), row)
    return pl.pallas_call(
        _combine_body,
        grid=(nt,),
        in_specs=[pl.BlockSpec((1, 1, 2 * tc), lambda t: (t, 0, 0), memory_space=pltpu.SMEM),
                  pl.BlockSpec((1, 1, 2 * tc), lambda t: (jnp.minimum(t + 1, nt - 1), 0, 0),
                               memory_space=pltpu.SMEM),
                  pl.BlockSpec((1, tc, d), row), gate_spec,
                  pl.BlockSpec((1, tc, LANES), row),
                  pl.BlockSpec(g_final.shape, lambda t: (0, 0)),
                  pl.BlockSpec(memory_space=pl.ANY)],
        out_specs=pl.BlockSpec((1, tc, d), row),
        out_shape=jax.ShapeDtypeStruct((bx, sx, d), F32),
        scratch_shapes=[pltpu.VMEM((2, 2, tc * nseg, LANES), F32), pltpu.SemaphoreType.DMA((2,))],
        compiler_params=_cparams(("arbitrary",)),
        name="moe_combine",
    )(dest3, dest3, x1, gate_f, route, g_final, yb)


def _route_plan(route, n_exp, tm):
    n_tok = route.shape[0]
    n_slots = n_tok * TOP_K
    slot_e = route[:, :TOP_K].astype(jnp.int32).reshape(n_slots)
    onehot = (slot_e[:, None] == jnp.arange(n_exp, dtype=jnp.int32)[None, :]).astype(jnp.int32)
    csum = jnp.cumsum(onehot, axis=0)
    rank = jnp.sum(csum * onehot, axis=1) - 1
    counts = csum[-1]
    padded = ((counts + tm - 1) // tm) * tm
    pad_end = jnp.cumsum(padded)
    pad_start = pad_end - padded
    dest = jnp.sum(onehot * pad_start[None, :], axis=1) + rank
    n_tiles = -(-n_slots // tm) + n_exp
    n_rows = n_tiles * tm
    slot_tok = jnp.repeat(jnp.arange(n_tok, dtype=jnp.int32), TOP_K)
    row_tok = jnp.zeros((n_rows,), jnp.int32).at[dest].set(slot_tok)
    n_valid = (pad_end[-1] // tm).astype(jnp.int32)
    tile_start = jnp.arange(n_tiles, dtype=jnp.int32) * tm
    tile_e = jnp.minimum(jnp.searchsorted(pad_end, tile_start, side='right'), n_exp - 1).astype(jnp.int32)
    last_e = tile_e[jnp.maximum(n_valid - 1, 0)]
    tile_e = jnp.where(jnp.arange(n_tiles) < n_valid, tile_e, last_e)
    return dest, row_tok, tile_e, n_valid.reshape(1)


def _moe(parts, h_all, g_final, w1, w3, w2):
    n_exp = w1.shape[0]
    route_all = jnp.concatenate([r.reshape(-1, LANES)[:, :2 * TOP_K] for _, r, _ in parts], axis=0)
    dest, row_tok, tile_e, n_valid = _route_plan(route_all, n_exp, MOE_ROW_TILE)
    yb = _moe_ffn_call(h_all, row_tok, tile_e, n_valid, w1, w3, w2)
    outs = []
    first = 0
    for x1, route, gate_f in parts:
        n_slots = x1.shape[0] * x1.shape[1] * TOP_K
        outs.append(_combine_call(x1, gate_f, route, g_final, yb, dest[first:first + n_slots]))
        first += n_slots
    return outs


def _trunk(x, mods, p, cache, tok_offset, h_all):
    sh_m0, sc_m0, gt_m0, sh_f0, sc_f0, gt_f0 = mods[0]
    sh_m1, sc_m1, gt_m1, sh_f1, sc_f1, gt_f1 = mods[1]
    wa = p['wa']
    wb = p['wb']
    wc = p['wc']
    bx, sx, d = x.shape

    if cache is None:
        keep = min(WINDOW_B, sx)
        qk, vt, ka, va, kb, vb, logf = _inproj_even_call(
            x, p['g_mix0'], sh_m0, sc_m0, p['w_in_even'], p['w_f'], p['b_f'], keep, True)
        o_a = _fox_prompt_call(qk, vt, wa)
        o_b = _band_prompt_call(qk, vt, p['rel_bias'], wa, wb)
        x_l0_state = (ka, va, logf, kb, vb)
    else:
        ck, cv, clogf, bk, bv, cck, ccv, b, t = cache
        qkv, ka, va, kb, vb, logf, lft, = _inproj_even_call(
            x, p['g_mix0'], sh_m0, sc_m0, p['w_in_even'], p['w_f'], p['b_f'], sx, False)
        nh = wa // HEAD_DIM
        p_len = ck.shape[1]
        lft_new = lft.reshape(nh, b, t).transpose(1, 0, 2)
        lp = -(-(p_len + t) // 512) * 512
        lcat = jnp.concatenate([clogf.transpose(0, 2, 1), lft_new,
                                jnp.zeros((b, nh, lp - p_len - t), F32)], axis=-1)
        ft_pad = _cumsum_call(lcat)
        qkv_b = qkv.reshape(b, t, qkv.shape[-1])
        o_a = _fox_sample_call(qkv_b, ck, cv, ft_pad, wa).reshape(1, b * t, wa)
        o_b = _band_sample_call(qkv_b, bk, bv, p['rel_bias'], wb).reshape(1, b * t, wb)
        x_l0_state = (ka, va, logf, kb, vb)
    o = jnp.concatenate([o_a, o_b], axis=-1)
    x = _out_ffn_call(x, o, gt_m0, p['g_ffn0'], sh_f0, sc_f0, gt_f0,
                      p['w_out_even'], p['w1_dense'], p['w3_dense'], p['w2_dense'])

    if cache is None:
        qk, vt, kc, vc = _inproj_odd_call(x, p['g_mix1'], sh_m1, sc_m1, p['w_in_odd'], True)
        o = _diff_prompt_call(qk, vt, p['lam_par'], p['g_sub'].reshape(-1, 1), wc, p['lam_init'])
    else:
        qkv, kc, vc = _inproj_odd_call(x, p['g_mix1'], sh_m1, sc_m1, p['w_in_odd'], False)
        qkv_b = qkv.reshape(b, t, qkv.shape[-1])
        o = _diff_sample_call(qkv_b, cck, ccv, p['lam_par'], p['g_sub'], wc, p['lam_init']).reshape(1, b * t, wc)
    x1, h_all, route = _out_router_call(x, o, gt_m1, p['g_ffn1'], sh_f1, sc_f1, p['w_out_odd'], p['w_router3'],
                                        p['n_exp'], tok_offset, h_all)
    return (x1, route, gt_f1), h_all, x_l0_state, (kc, vc)


def kernel(x_prompt, x_sample, cache_a_k, cache_a_v, cache_a_logf, cache_b_k, cache_b_v, cache_c_k, cache_c_v,
           c_prompt, c_sample, w_mod, b_mod, g_mix, g_ffn, g_final, w_in_even, b_forget, rel_bias, w_out_even,
           w_in_odd, lambda_q1, lambda_k1, lambda_q2, lambda_k2, g_subln, w_out_odd, w1_dense, w3_dense, w2_dense,
           w_router, w1_moe, w3_moe, w2_moe):
    bp, sp, d = x_prompt.shape
    bs, ts, _ = x_sample.shape
    h_a = cache_a_k.shape[3]
    h_b = cache_b_k.shape[3]
    h_c = cache_c_k.shape[3]
    wa, wb, wc = h_a * HEAD_DIM, h_b * HEAD_DIM, h_c * 2 * HEAD_DIM
    n_exp = w_router.shape[-1]
    assert w_mod.shape[0] == 2, "kernel is written for the 2-layer trunk"

    we = w_in_even[0]
    w_main = jnp.concatenate([we[:, :3 * wa], we[:, 3 * wa + h_a:]], axis=1).astype(BF)
    w_f = jnp.pad(we[:, 3 * wa:3 * wa + h_a], ((0, 0), (0, LANES - h_a))).astype(BF)
    b_f = jnp.pad(b_forget[0], (0, LANES - h_a)).reshape(1, LANES).astype(F32)
    wr = jnp.pad(w_router[0], ((0, 0), (0, LANES - n_exp)))
    wr_hi = wr.astype(BF)
    wr_r1 = wr - wr_hi.astype(F32)
    wr_mid = wr_r1.astype(BF)
    wr_lo = (wr_r1 - wr_mid.astype(F32)).astype(BF)
    params = dict(
        wa=wa, wb=wb, wc=wc, n_exp=n_exp, lam_init=0.8 - 0.6 * math.exp(-0.3 * 1),
        g_mix0=g_mix[0:1], g_mix1=g_mix[1:2], g_ffn0=g_ffn[0:1], g_ffn1=g_ffn[1:2],
        g_final=g_final.reshape(1, d),
        w_in_even=w_main, w_f=w_f, b_f=b_f, rel_bias=rel_bias[0],
        w_out_even=w_out_even[0].astype(BF), w_in_odd=w_in_odd[0].astype(BF),
        lam_par=jnp.concatenate([lambda_q1, lambda_k1, lambda_q2, lambda_k2], axis=0),
        g_sub=g_subln[0:1], w_out_odd=w_out_odd[0].astype(BF),
        w1_dense=w1_dense[0].astype(BF), w3_dense=w3_dense[0].astype(BF), w2_dense=w2_dense[0].astype(BF),
        w_router3=jnp.stack([wr_hi, wr_mid, wr_lo]),
        w1_moe=w1_moe[0].astype(BF), w3_moe=w3_moe[0].astype(BF), w2_moe=w2_moe[0].astype(BF),
    )

    mod = _mod_call(jnp.concatenate([c_prompt, c_sample], axis=0), w_mod, b_mod)

    def mods_for(rows, per_row_len):
        out = []
        for l in range(2):
            parts = []
            for k in range(6):
                m = mod[l, rows, k * d:(k + 1) * d]
                if per_row_len:
                    m = jnp.repeat(m, per_row_len, axis=0).reshape(1, -1, d)
                else:
                    m = m.reshape(-1, 1, d)
                parts.append(m)
            out.append(parts)
        return out

    h_all = jnp.zeros(((bp * sp + bs * ts) * (d // LANES), LANES), F32)
    moe_p, h_all, ev_p, od_p = _trunk(x_prompt, mods_for(slice(0, bp), 0), params, None, 0, h_all)
    cache = (cache_a_k[0].reshape(bs, -1, wa), cache_a_v[0].reshape(bs, -1, wa), cache_a_logf[0],
             cache_b_k[0].reshape(bs, -1, wb), cache_b_v[0].reshape(bs, -1, wb),
             cache_c_k[0].reshape(bs, -1, wc), cache_c_v[0].reshape(bs, -1, wc), bs, ts)
    moe_s, h_all, ev_s, od_s = _trunk(x_sample.reshape(1, bs * ts, d), mods_for(slice(bp, bp + bs), ts), params,
                                      cache, bp * sp, h_all)
    y_p, y_s = _moe([moe_p, moe_s], h_all, params['g_final'], params['w1_moe'], params['w3_moe'], params['w2_moe'])

    ka, va, logf, kb, vb = ev_p
    keep = kb.shape[1]
    out_p = (ka.reshape(1, bp, sp, h_a, HEAD_DIM), va.reshape(1, bp, sp, h_a, HEAD_DIM),
             logf.reshape(1, bp, sp, h_a),
             kb.reshape(1, bp, keep, h_b, HEAD_DIM), vb.reshape(1, bp, keep, h_b, HEAD_DIM),
             od_p[0].reshape(1, bp, sp, h_c, 2 * HEAD_DIM), od_p[1].reshape(1, bp, sp, h_c, 2 * HEAD_DIM))
    ka, va, logf, kb, vb = ev_s
    new_bk = jnp.concatenate([cache_b_k[0], kb.reshape(bs, ts, h_b, HEAD_DIM)], axis=1)[:, ts:]
    new_bv = jnp.concatenate([cache_b_v[0], vb.reshape(bs, ts, h_b, HEAD_DIM)], axis=1)[:, ts:]
    out_s = (ka.reshape(1, bs, ts, h_a, HEAD_DIM), va.reshape(1, bs, ts, h_a, HEAD_DIM),
             logf.reshape(1, bs, ts, h_a), new_bk[None], new_bv[None],
             od_s[0].reshape(1, bs, ts, h_c, 2 * HEAD_DIM), od_s[1].reshape(1, bs, ts, h_c, 2 * HEAD_DIM))
    return (y_p, y_s.reshape(bs, ts, d)) + out_p + out_s
```

```python
import functools
import math

import jax
import jax.numpy as jnp
from jax import lax
from jax.experimental import pallas as pl
from jax.experimental.pallas import tpu as pltpu

BF = jnp.bfloat16
F32 = jnp.float32

CHUNK = 64
CHUNK_SHIFT = 6
HEAD_DIM = 64
BAND_CHUNKS = 8
WINDOW_B = BAND_CHUNKS * CHUNK
MAX_REL = 128
TOP_K = 2
RMS_EPS = 1e-6
NEG_INF = -1e30
QK_SCALE = HEAD_DIM ** -0.5

LANES = 128
VMEM_LIMIT_BYTES = 56 * 1024 * 1024

ROW_TILE = 512
ATTN_TILE = 512
BAND_Q_TILE = WINDOW_B // 2
CACHE_TILE_A = 1024
CACHE_TILE_C = 512
MOE_ROW_TILE = 512
MOE_F_TILE = 1792
COMBINE_TILE = 256


def _cparams(sem):
    return pltpu.CompilerParams(dimension_semantics=sem, vmem_limit_bytes=VMEM_LIMIT_BYTES)


def _dot(a, b):
    return jnp.dot(a, b, preferred_element_type=F32)


def _dot_nt(a, b):
    return lax.dot_general(a, b, (((1,), (1,)), ((), ())), preferred_element_type=F32)


def _sigmoid(x):
    return 1.0 / (1.0 + jnp.exp(-x))


def _modulate(x, g, shift, scale):
    y = x * lax.rsqrt(jnp.mean(x * x, axis=-1, keepdims=True) + RMS_EPS)
    return (y * g) * (1.0 + scale) + shift


def _split3(x):
    hi = x.astype(BF)
    r1 = x - hi.astype(F32)
    mid = r1.astype(BF)
    lo = (r1 - mid.astype(F32)).astype(BF)
    return hi, mid, lo


def _cumsum_lanes(x, upper):
    hi, mid, lo = _split3(x)
    return _dot(hi, upper) + _dot(mid, upper) + _dot(lo, upper)


def _upper_tri(t):
    r = jnp.arange(t)
    return (r[:, None] <= r[None, :]).astype(BF)


def _osm_update(s, v, m_ref, l_ref, acc_ref, idx):
    m_prev = m_ref[idx]
    m_new = jnp.maximum(m_prev, jnp.max(s, axis=-1, keepdims=True))
    alpha = jnp.exp(m_prev - m_new)
    p = jnp.exp(s - m_new)
    l_ref[idx] = alpha * l_ref[idx] + jnp.sum(p, axis=-1, keepdims=True)
    acc_ref[idx] = alpha * acc_ref[idx] + _dot(p.astype(BF), v)
    m_ref[idx] = m_new


def _osm_init(m_ref, l_ref, acc_ref):
    m_ref[...] = jnp.full(m_ref.shape, NEG_INF, F32)
    l_ref[...] = jnp.zeros(l_ref.shape, F32)
    acc_ref[...] = jnp.zeros(acc_ref.shape, F32)


def _mod_body(c_ref, w_ref, b_ref, o_ref):
    c = c_ref[...]
    s = (c * _sigmoid(c)).astype(BF)
    o_ref[0] = _dot(s, w_ref[0].astype(BF)) + b_ref[0]


def _mod_call(c_all, w_mod, b_mod):
    depth, d, n = w_mod.shape
    r = c_all.shape[0]
    tn = 1536 if n % 1536 == 0 else n
    return pl.pallas_call(
        _mod_body,
        grid=(depth, n // tn),
        in_specs=[pl.BlockSpec((r, d), lambda l, j: (0, 0)),
                  pl.BlockSpec((1, d, tn), lambda l, j: (l, 0, j)),
                  pl.BlockSpec((1, 1, tn), lambda l, j: (l, 0, j))],
        out_specs=pl.BlockSpec((1, r, tn), lambda l, j: (l, 0, j)),
        out_shape=jax.ShapeDtypeStruct((depth, r, n), F32),
        compiler_params=_cparams(("arbitrary", "arbitrary")),
        name="mod",
    )(c_all, w_mod, b_mod.reshape(depth, 1, n))


def _mod_spec(mod_arr, tm):
    d = mod_arr.shape[-1]
    if mod_arr.shape[1] == 1:
        return pl.BlockSpec((1, 1, d), lambda b, i: (b, 0, 0))
    return pl.BlockSpec((1, tm, d), lambda b, i: (b, i, 0))


def _const_spec(arr):
    nd = arr.ndim
    return pl.BlockSpec(arr.shape, lambda b, i: (0,) * nd, pipeline_mode=pl.Buffered(1))


def _head_blocks(x, augs):
    tm, w = x.shape
    first = lax.broadcasted_iota(jnp.int32, (tm, LANES), 1) < HEAD_DIM
    out = []
    for p in range(w // LANES):
        pair = x[:, p * LANES:(p + 1) * LANES]
        out.append(jnp.where(first, pair, augs[2 * p]).astype(BF))
        out.append(jnp.where(first, pltpu.roll(pair, HEAD_DIM, 1), augs[2 * p + 1]).astype(BF))
    return out


def _lane_range_row(lo, hi, value):
    lane = lax.broadcasted_iota(jnp.int32, (1, LANES), 1)
    return jnp.where((lane >= lo) & (lane < hi), value, 0.0).astype(F32)


def _ones_rows(t):
    return jnp.where(lax.broadcasted_iota(jnp.int32, (16, t), 0) == 0, 1.0, 0.0).astype(BF)


def _log_forget(h, wf_ref, bf_ref):
    z = _dot(h, wf_ref[...]) + bf_ref[...]
    return jnp.minimum(z, 0.0) - jnp.log1p(jnp.exp(-jnp.abs(z)))


def _inproj_even_body(*refs, wa, wb, prompt):
    if prompt:
        (x_ref, g_ref, sh_ref, sc_ref, w_ref, wf_ref, bf_ref, low_ref, place_ref,
         qk_ref, vt_ref, ka_ref, va_ref, kb_ref, vb_ref, logf_ref, carry_ref) = refs
    else:
        (x_ref, g_ref, sh_ref, sc_ref, w_ref, wf_ref, bf_ref,
         qkv_ref, ka_ref, va_ref, kb_ref, vb_ref, logf_ref, lft_ref) = refs
    i = pl.program_id(1)
    h = _modulate(x_ref[0], g_ref[...], sh_ref[0], sc_ref[0]).astype(BF)
    tm = h.shape[0]
    nh = logf_ref.shape[-1]

    qa = _dot(h, w_ref[:, 0:wa])
    ka = _dot(h, w_ref[:, wa:2 * wa])
    va = _dot(h, w_ref[:, 2 * wa:3 * wa])
    o = 3 * wa
    qb = _dot(h, w_ref[:, o:o + wb])
    kb = _dot(h, w_ref[:, o + wb:o + 2 * wb])
    vb = _dot(h, w_ref[:, o + 2 * wb:o + 3 * wb])
    ka_ref[0] = ka
    va_ref[0] = va
    kb_ref[0] = kb
    vb_ref[0] = vb
    logf = _log_forget(h, wf_ref, bf_ref)
    logf_ref[0] = logf[:, :nh]

    if not prompt:
        qkv_ref[0] = jnp.concatenate(
            [(qa * QK_SCALE).astype(BF), ka.astype(BF), va.astype(BF),
             (qb * QK_SCALE).astype(BF), kb.astype(BF), vb.astype(BF)], axis=-1)
        lft_ref[0] = logf.T[:nh, :]
        return

    @pl.when(i == 0)
    def _():
        carry_ref[...] = jnp.zeros(carry_ref.shape, F32)

    low = low_ref[...]
    hi, mid, lo = _split3(logf)
    f_cum = _dot(low, hi) + _dot(low, mid) + _dot(low, lo) + carry_ref[0:1, :]
    carry_ref[...] = jnp.broadcast_to(f_cum[tm - 1:tm, :], carry_ref.shape)
    nhi, nmid, nlo = _split3(-f_cum)
    k_aug = _dot(nhi, place_ref[0]) + _dot(nmid, place_ref[1]) + _dot(nlo, place_ref[2])
    q_ones = _lane_range_row(HEAD_DIM, HEAD_DIM + 3, 1.0)
    zero = jnp.zeros((1, LANES), F32)
    nhb = wb // HEAD_DIM
    blocks = (_head_blocks(qa * QK_SCALE, [q_ones] * nh)
              + _head_blocks(ka, [k_aug[:, n * LANES:(n + 1) * LANES] for n in range(nh)])
              + _head_blocks(qb * QK_SCALE, [zero] * nhb)
              + _head_blocks(kb, [zero] * nhb))
    qk_ref[0] = jnp.concatenate(blocks, axis=-1)
    ones = _ones_rows(tm)
    vt_ref[0] = jnp.concatenate([va.T.astype(BF), ones, vb.T.astype(BF), ones], axis=0)


def _lower_tri(t):
    r = jnp.arange(t)
    return (r[:, None] >= r[None, :]).astype(BF)


def _place_matrices(nh):
    s = jnp.arange(3)[:, None, None]
    r = jnp.arange(LANES)[None, :, None]
    c = jnp.arange(nh * LANES)[None, None, :]
    return ((r < nh) & (c == r * LANES + HEAD_DIM + s)).astype(BF)


def _inproj_even_call(x, g, shift, scale, w_main, w_f, b_f, keep, prompt):
    bx, sx, d = x.shape
    tm = min(ROW_TILE, sx)
    nt = sx // tm
    n = w_main.shape[1]
    wa = wb = n // 6
    nh = wa // HEAD_DIM
    nkeep = keep // tm
    tail = lambda b, i: (b, jnp.maximum(i - (nt - nkeep), 0), 0)
    row = lambda b, i: (b, i, 0)
    in_specs = [pl.BlockSpec((1, tm, d), row), _const_spec(g), _mod_spec(shift, tm), _mod_spec(scale, tm),
                _const_spec(w_main), _const_spec(w_f), _const_spec(b_f)]
    args = [x, g, shift, scale, w_main, w_f, b_f]
    f32_specs = [pl.BlockSpec((1, tm, wa), row), pl.BlockSpec((1, tm, wa), row),
                 pl.BlockSpec((1, tm, wb), tail), pl.BlockSpec((1, tm, wb), tail),
                 pl.BlockSpec((1, tm, nh), row)]
    f32_shapes = [jax.ShapeDtypeStruct((bx, sx, wa), F32), jax.ShapeDtypeStruct((bx, sx, wa), F32),
                  jax.ShapeDtypeStruct((bx, keep, wb), F32), jax.ShapeDtypeStruct((bx, keep, wb), F32),
                  jax.ShapeDtypeStruct((bx, sx, nh), F32)]
    scratch = []
    if prompt:
        low, place = _lower_tri(tm), _place_matrices(nh)
        in_specs += [_const_spec(low), _const_spec(place)]
        args += [low, place]
        nqk = 2 * (wa + wb) // HEAD_DIM * LANES
        nvt = wa + wb + 32
        out_specs = [pl.BlockSpec((1, tm, nqk), row), pl.BlockSpec((1, nvt, tm), lambda b, i: (b, 0, i))] + f32_specs
        out_shape = [jax.ShapeDtypeStruct((bx, sx, nqk), BF), jax.ShapeDtypeStruct((bx, nvt, sx), BF)] + f32_shapes
        scratch.append(pltpu.VMEM((8, LANES), F32))
    else:
        out_specs = [pl.BlockSpec((1, tm, n), row)] + f32_specs + [pl.BlockSpec((1, nh, tm), lambda b, i: (b, 0, i))]
        out_shape = [jax.ShapeDtypeStruct((bx, sx, n), BF)] + f32_shapes + [jax.ShapeDtypeStruct((bx, nh, sx), F32)]
    return pl.pallas_call(
        functools.partial(_inproj_even_body, wa=wa, wb=wb, prompt=prompt),
        grid=(bx, nt), in_specs=in_specs, out_specs=out_specs, out_shape=out_shape,
        scratch_shapes=scratch,
        compiler_params=_cparams(("arbitrary", "arbitrary")),
        name="inproj_even",
    )(*args)


def _inproj_odd_body(*refs, wc, prompt):
    if prompt:
        x_ref, g_ref, sh_ref, sc_ref, w_ref, qk_ref, vt_ref, k_ref, v_ref = refs
    else:
        x_ref, g_ref, sh_ref, sc_ref, w_ref, qkv_ref, k_ref, v_ref = refs
    i = pl.program_id(1)
    h = _modulate(x_ref[0], g_ref[...], sh_ref[0], sc_ref[0]).astype(BF)
    tm = h.shape[0]
    q = _dot(h, w_ref[:, 0:wc])
    k = _dot(h, w_ref[:, wc:2 * wc])
    v = _dot(h, w_ref[:, 2 * wc:3 * wc])
    k_ref[0] = k
    v_ref[0] = v
    if not prompt:
        qkv_ref[0] = jnp.concatenate([(q * QK_SCALE).astype(BF), k.astype(BF), v.astype(BF)], axis=-1)
        return
    nh = wc // LANES
    pos = (i * tm + lax.broadcasted_iota(jnp.int32, (tm, LANES), 0)).astype(F32)
    pos_hi = pos.astype(BF).astype(F32)
    lane = lax.broadcasted_iota(jnp.int32, (tm, LANES), 1)
    k_aug = jnp.where(lane == HEAD_DIM, pos_hi, jnp.where(lane == HEAD_DIM + 1, pos - pos_hi, 0.0))
    q_augs = []
    for hh in range(nh):
        q_augs += [_lane_range_row(HEAD_DIM, HEAD_DIM + 2, _alibi_slope(hh, nh))] * 2
    blocks = _head_blocks(q * QK_SCALE, q_augs) + _head_blocks(k, [k_aug] * (2 * nh))
    qk_ref[0] = jnp.concatenate(blocks, axis=-1)
    vt_ref[0] = jnp.concatenate([v.T.astype(BF), _ones_rows(tm)], axis=0)


def _alibi_slope(h, nh):
    assert 8 % nh == 0, "slopes must be exact powers of two to ride in bf16"
    return 2.0 ** (-8.0 * (h + 1) / nh)


def _inproj_odd_call(x, g, shift, scale, w, prompt):
    bx, sx, d = x.shape
    tm = min(ROW_TILE, sx)
    n = w.shape[1]
    wc = n // 3
    row = lambda b, i: (b, i, 0)
    f32_specs = [pl.BlockSpec((1, tm, wc), row), pl.BlockSpec((1, tm, wc), row)]
    f32_shapes = [jax.ShapeDtypeStruct((bx, sx, wc), F32), jax.ShapeDtypeStruct((bx, sx, wc), F32)]
    if prompt:
        nqk = 2 * wc // HEAD_DIM * LANES
        out_specs = [pl.BlockSpec((1, tm, nqk), row), pl.BlockSpec((1, wc + 16, tm), lambda b, i: (b, 0, i))]
        out_shape = [jax.ShapeDtypeStruct((bx, sx, nqk), BF), jax.ShapeDtypeStruct((bx, wc + 16, sx), BF)]
    else:
        out_specs = [pl.BlockSpec((1, tm, n), row)]
        out_shape = [jax.ShapeDtypeStruct((bx, sx, n), BF)]
    return pl.pallas_call(
        functools.partial(_inproj_odd_body, wc=wc, prompt=prompt),
        grid=(bx, sx // tm),
        in_specs=[pl.BlockSpec((1, tm, d), row), _const_spec(g), _mod_spec(shift, tm), _mod_spec(scale, tm),
                  _const_spec(w)],
        out_specs=out_specs + f32_specs, out_shape=out_shape + f32_shapes,
        compiler_params=_cparams(("arbitrary", "arbitrary")),
        name="inproj_odd",
    )(x, g, shift, scale, w)


def _tflash_update(st, v_aug, mask, extra, m_ref, acc_ref, idx):
    if extra is not None:
        st = st - extra
    if mask is not None:
        st = jnp.where(mask, st, NEG_INF)
    m_prev = m_ref[idx]
    m_new = jnp.maximum(m_prev, jnp.max(st, axis=0, keepdims=True))
    alpha = jnp.exp(m_prev - m_new)
    p = jnp.exp(st - m_new).astype(BF)
    acc_ref[idx] = alpha * acc_ref[idx] + _dot(v_aug, p)
    m_ref[idx] = m_new


def _tflash_init(m_ref, acc_ref):
    m_ref[...] = jnp.full(m_ref.shape, NEG_INF, F32)
    acc_ref[...] = jnp.zeros(acc_ref.shape, F32)


def _fox_prompt_body(q_ref, k_ref, vt_ref, o_ref, m_ref, acc_ref, st_ref, *, nh):
    i = pl.program_id(1)
    j = pl.program_id(2)
    hd = HEAD_DIM

    @pl.when(j == 0)
    def _():
        _tflash_init(m_ref, acc_ref)

    def step(masked):
        q = q_ref[0]
        k = k_ref[0]
        vt = vt_ref[0]
        tq, tk = q.shape[0], k.shape[0]
        ones = vt[nh * hd:, :]
        mask = None
        if masked:
            mask = lax.broadcasted_iota(jnp.int32, (tk, tq), 0) <= lax.broadcasted_iota(jnp.int32, (tk, tq), 1)
        def scores(h):
            st_ref[h % 2] = _dot_nt(k[:, h * LANES:(h + 1) * LANES], q[:, h * LANES:(h + 1) * LANES])

        scores(0)
        for h in range(nh):
            if h + 1 < nh:
                scores(h + 1)
            v_aug = jnp.concatenate([vt[h * hd:(h + 1) * hd, :], ones], axis=0)
            _tflash_update(st_ref[h % 2], v_aug, mask, None, m_ref, acc_ref, h)

    @pl.when(j < i)
    def _():
        step(False)

    @pl.when(j == i)
    def _():
        step(True)

    @pl.when(j == pl.num_programs(2) - 1)
    def _():
        ot = jnp.concatenate([acc_ref[h][:hd] / acc_ref[h][hd:hd + 1] for h in range(nh)], axis=0)
        o_ref[0] = ot.T.astype(BF)


def _fox_prompt_call(qk, vt, wa):
    b, s, _ = qk.shape
    nh = wa // HEAD_DIM
    t = min(ATTN_TILE, s)
    nq = s // t
    wq = nh * LANES
    return pl.pallas_call(
        functools.partial(_fox_prompt_body, nh=nh),
        grid=(b, nq, nq),
        in_specs=[pl.BlockSpec((1, t, wq), lambda b, i, j: (b, i, 0)),
                  pl.BlockSpec((1, t, wq), lambda b, i, j: (b, jnp.minimum(j, i), 1)),
                  pl.BlockSpec((1, wa + 16, t), lambda b, i, j: (b, 0, jnp.minimum(j, i)))],
        out_specs=pl.BlockSpec((1, t, wa), lambda b, i, j: (b, i, 0)),
        out_shape=jax.ShapeDtypeStruct((b, s, wa), BF),
        scratch_shapes=[pltpu.VMEM((nh, 1, t), F32), pltpu.VMEM((nh, HEAD_DIM + 16, t), F32),
                        pltpu.VMEM((2, t, t), F32)],
        compiler_params=_cparams(("arbitrary", "arbitrary", "arbitrary")),
        name="fox_prompt",
    )(qk, qk, vt)


def _band_prompt_body(q_ref, k0_ref, k1_ref, k2_ref, v0_ref, v1_ref, v2_ref, bias_ref, o_ref, st_ref, *, nh):
    i = pl.program_id(1)
    hd = HEAD_DIM
    q = q_ref[0]
    tq = q.shape[0]
    k = jnp.concatenate([k0_ref[0], k1_ref[0], k2_ref[0]], axis=0)
    vts = [v0_ref[0], v1_ref[0], v2_ref[0]]
    in_seq = lax.broadcasted_iota(jnp.int32, (3 * tq, tq), 0) >= (2 - i) * tq
    outs = []
    def scores(h):
        st_ref[h % 2] = _dot_nt(k[:, h * LANES:(h + 1) * LANES], q[:, h * LANES:(h + 1) * LANES])

    scores(0)
    for h in range(nh):
        if h + 1 < nh:
            scores(h + 1)
        st = st_ref[h % 2] + bias_ref[h]
        st = jnp.where(in_seq, st, NEG_INF)
        p = jnp.exp(st - jnp.max(st, axis=0, keepdims=True)).astype(BF)
        acc = None
        for w, vt in enumerate(vts):
            v_aug = jnp.concatenate([vt[h * hd:(h + 1) * hd, :], vt[nh * hd:, :]], axis=0)
            part = _dot(v_aug, p[w * tq:(w + 1) * tq, :])
            acc = part if acc is None else acc + part
        outs.append(acc[:hd] / acc[hd:hd + 1])
    o_ref[0] = jnp.concatenate(outs, axis=0).T.astype(BF)


def _band_bias_prompt(rel_bias, tq):
    nh = rel_bias.shape[0]
    period = 4 * tq
    t = jnp.arange(period)
    t = jnp.where(t < 3 * tq, t, t - period)
    u = rel_bias[:, jnp.clip(2 * tq - t, -MAX_REL, MAX_REL) + MAX_REL].astype(F32)
    skew = jnp.tile(u, (1, tq))[:, :tq * (period - 1)].reshape(nh, tq, period - 1)
    bias = skew[:, :, :3 * tq]
    r = jnp.arange(tq)[:, None]
    c = jnp.arange(3 * tq)[None, :]
    valid = (c // CHUNK >= r // CHUNK) & (c // CHUNK <= r // CHUNK + BAND_CHUNKS)
    return jnp.where(valid[None], bias, NEG_INF)


def _band_prompt_call(qk, vt, rel_bias, wa, wb):
    b, s, _ = qk.shape
    nh = wb // HEAD_DIM
    tq = BAND_Q_TILE
    assert s % tq == 0 and wa == wb
    bias_t = _band_bias_prompt(rel_bias, tq).transpose(0, 2, 1)
    wq = nh * LANES
    cq, ck = 2, 3

    def k_spec(back):
        return pl.BlockSpec((1, tq, wq), lambda b, i: (b, jnp.maximum(i - back, 0), ck))

    def v_spec(back):
        return pl.BlockSpec((1, wb + 16, tq), lambda b, i: (b, 1, jnp.maximum(i - back, 0)))

    return pl.pallas_call(
        functools.partial(_band_prompt_body, nh=nh),
        grid=(b, s // tq),
        in_specs=[pl.BlockSpec((1, tq, wq), lambda b, i: (b, i, cq)),
                  k_spec(2), k_spec(1), k_spec(0), v_spec(2), v_spec(1), v_spec(0),
                  _const_spec(bias_t)],
        out_specs=pl.BlockSpec((1, tq, wb), lambda b, i: (b, i, 0)),
        out_shape=jax.ShapeDtypeStruct((b, s, wb), BF),
        scratch_shapes=[pltpu.VMEM((2, 3 * tq, tq), F32)],
        compiler_params=_cparams(("arbitrary", "arbitrary")),
        name="band_prompt",
    )(qk, qk, qk, qk, vt, vt, vt, bias_t)


def _lambda(lam_ref, lam_init):
    lp = lam_ref[...]
    a = jnp.sum(lp[0:1] * lp[1:2], axis=-1, keepdims=True)
    b = jnp.sum(lp[2:3] * lp[3:4], axis=-1, keepdims=True)
    return jnp.exp(a) - jnp.exp(b) + lam_init


def _diff_prompt_body(q_ref, k_ref, vt_ref, lam_ref, gcol_ref, o_ref, m_ref, acc_ref, st_ref, *, nh, lam_init):
    i = pl.program_id(1)
    j = pl.program_id(2)
    dv = 2 * HEAD_DIM

    @pl.when(j == 0)
    def _():
        _tflash_init(m_ref, acc_ref)

    def step(masked):
        q = q_ref[0]
        k = k_ref[0]
        vt = vt_ref[0]
        tq, tk = q.shape[0], k.shape[0]
        ones = vt[nh * dv:, :]
        mask = None
        if masked:
            key = lax.broadcasted_iota(jnp.int32, (tk, tq), 0)
            qry = lax.broadcasted_iota(jnp.int32, (tk, tq), 1)
            mask = (key >> CHUNK_SHIFT) <= (qry >> CHUNK_SHIFT)
            ahead = jnp.maximum(key - qry, 0).astype(F32)
        def scores(n):
            st_ref[n % 2] = _dot_nt(k[:, n * LANES:(n + 1) * LANES], q[:, n * LANES:(n + 1) * LANES])

        scores(0)
        for h in range(nh):
            v_aug = jnp.concatenate([vt[h * dv:(h + 1) * dv, :], ones], axis=0)
            extra = (2.0 * _alibi_slope(h, nh)) * ahead if masked else None
            for u in range(2):
                n = 2 * h + u
                if n + 1 < 2 * nh:
                    scores(n + 1)
                _tflash_update(st_ref[n % 2], v_aug, mask, extra, m_ref, acc_ref, n)

    @pl.when(j < i)
    def _():
        step(False)

    @pl.when(j == i)
    def _():
        step(True)

    @pl.when(j == pl.num_programs(2) - 1)
    def _():
        lam = _lambda(lam_ref, lam_init)
        g = gcol_ref[...]
        outs = []
        for h in range(nh):
            a1 = acc_ref[2 * h]
            a2 = acc_ref[2 * h + 1]
            o = a1[:dv] / a1[dv:dv + 1] - lam * (a2[:dv] / a2[dv:dv + 1])
            o = o * lax.rsqrt(jnp.mean(o * o, axis=0, keepdims=True) + RMS_EPS)
            outs.append((o * g) * (1.0 - lam_init))
        o_ref[0] = jnp.concatenate(outs, axis=0).T.astype(BF)


def _diff_prompt_call(qk, vt, lam_par, g_col, wc, lam_init):
    b, s, _ = qk.shape
    nh = wc // (2 * HEAD_DIM)
    t = min(ATTN_TILE, s)
    nq = s // t
    wq = 2 * nh * LANES
    return pl.pallas_call(
        functools.partial(_diff_prompt_body, nh=nh, lam_init=lam_init),
        grid=(b, nq, nq),
        in_specs=[pl.BlockSpec((1, t, wq), lambda b, i, j: (b, i, 0)),
                  pl.BlockSpec((1, t, wq), lambda b, i, j: (b, jnp.minimum(j, i), 1)),
                  pl.BlockSpec((1, wc + 16, t), lambda b, i, j: (b, 0, jnp.minimum(j, i))),
                  pl.BlockSpec(lam_par.shape, lambda b, i, j: (0, 0)),
                  pl.BlockSpec(g_col.shape, lambda b, i, j: (0, 0))],
        out_specs=pl.BlockSpec((1, t, wc), lambda b, i, j: (b, i, 0)),
        out_shape=jax.ShapeDtypeStruct((b, s, wc), BF),
        scratch_shapes=[pltpu.VMEM((2 * nh, 1, t), F32), pltpu.VMEM((2 * nh, 2 * HEAD_DIM + 16, t), F32),
                        pltpu.VMEM((2, t, t), F32)],
        compiler_params=_cparams(("arbitrary", "arbitrary", "arbitrary")),
        name="diff_prompt",
    )(qk, qk, vt, lam_par, g_col)


def _cumsum_body(x_ref, up_ref, o_ref, carry_ref):
    @pl.when(pl.program_id(1) == 0)
    def _():
        carry_ref[...] = jnp.zeros(carry_ref.shape, F32)
    ft = _cumsum_lanes(x_ref[0], up_ref[...]) + carry_ref[:, 0:1]
    o_ref[0] = ft
    carry_ref[...] = jnp.broadcast_to(ft[:, -1:], carry_ref.shape)


def _cumsum_call(x):
    b, nh, length = x.shape
    t = 512
    up = _upper_tri(t)
    return pl.pallas_call(
        _cumsum_body,
        grid=(b, length // t),
        in_specs=[pl.BlockSpec((1, nh, t), lambda b, j: (b, 0, j)), pl.BlockSpec((t, t), lambda b, j: (0, 0))],
        out_specs=pl.BlockSpec((1, nh, t), lambda b, j: (b, 0, j)),
        out_shape=jax.ShapeDtypeStruct(x.shape, F32),
        scratch_shapes=[pltpu.VMEM((nh, LANES), F32)],
        compiler_params=_cparams(("arbitrary", "arbitrary")),
        name="logf_cumsum",
    )(x, up)


def _fox_sample_body(q_ref, kn_ref, vn_ref, ck_ref, cv_ref, ftc_ref, ftn_ref, o_ref, m_ref, l_ref, acc_ref, *, nh):
    j = pl.program_id(1)
    hd = HEAD_DIM
    q = q_ref[0]
    t = q.shape[0]

    @pl.when(j == 0)
    def _():
        _osm_init(m_ref, l_ref, acc_ref)

    k = ck_ref[0].astype(BF)
    v = cv_ref[0].astype(BF)
    ft = ftc_ref[0]
    for h in range(nh):
        sl = slice(h * hd, (h + 1) * hd)
        s = _dot_nt(q[:, sl], k[:, sl]) - ft[h:h + 1, :]
        _osm_update(s, v[:, sl], m_ref, l_ref, acc_ref, h)

    @pl.when(j == pl.num_programs(1) - 1)
    def _():
        kn = kn_ref[0]
        vn = vn_ref[0]
        ftn = ftn_ref[0][:, :t]
        visible = lax.broadcasted_iota(jnp.int32, (t, t), 1) <= lax.broadcasted_iota(jnp.int32, (t, t), 0)
        outs = []
        for h in range(nh):
            sl = slice(h * hd, (h + 1) * hd)
            s = _dot_nt(q[:, sl], kn[:, sl]) - ftn[h:h + 1, :]
            s = jnp.where(visible, s, NEG_INF)
            _osm_update(s, vn[:, sl], m_ref, l_ref, acc_ref, h)
            outs.append((acc_ref[h] / l_ref[h]).astype(BF))
        o_ref[0] = jnp.concatenate(outs, axis=-1)


def _fox_sample_call(qkv, cache_k, cache_v, ft_pad, wa):
    b, t, _ = qkv.shape
    p_len = cache_k.shape[1]
    nh = wa // HEAD_DIM
    tk = min(CACHE_TILE_A, p_len)
    assert p_len % tk == 0 and p_len % LANES == 0 and t <= LANES
    return pl.pallas_call(
        functools.partial(_fox_sample_body, nh=nh),
        grid=(b, p_len // tk),
        in_specs=[pl.BlockSpec((1, t, wa), lambda b, j: (b, 0, 0)),
                  pl.BlockSpec((1, t, wa), lambda b, j: (b, 0, 1)),
                  pl.BlockSpec((1, t, wa), lambda b, j: (b, 0, 2)),
                  pl.BlockSpec((1, tk, wa), lambda b, j: (b, j, 0)),
                  pl.BlockSpec((1, tk, wa), lambda b, j: (b, j, 0)),
                  pl.BlockSpec((1, nh, tk), lambda b, j: (b, 0, j)),
                  pl.BlockSpec((1, nh, LANES), lambda b, j: (b, 0, p_len // LANES))],
        out_specs=pl.BlockSpec((1, t, wa), lambda b, j: (b, 0, 0)),
        out_shape=jax.ShapeDtypeStruct((b, t, wa), BF),
        scratch_shapes=[pltpu.VMEM((nh, t, 1), F32), pltpu.VMEM((nh, t, 1), F32),
                        pltpu.VMEM((nh, t, HEAD_DIM), F32)],
        compiler_params=_cparams(("arbitrary", "arbitrary")),
        name="fox_sample",
    )(qkv, qkv, qkv, cache_k, cache_v, ft_pad, ft_pad)


def _band_sample_body(q_ref, kn_ref, vn_ref, bk_ref, bv_ref, bias_b_ref, bias_n_ref, o_ref, *, nh):
    hd = HEAD_DIM
    q = q_ref[0]
    kn = kn_ref[0]
    vn = vn_ref[0]
    kb = bk_ref[0].astype(BF)
    vb = bv_ref[0].astype(BF)
    outs = []
    for h in range(nh):
        sl = slice(h * hd, (h + 1) * hd)
        sb = _dot_nt(q[:, sl], kb[:, sl]) + bias_b_ref[h]
        sn = _dot_nt(q[:, sl], kn[:, sl]) + bias_n_ref[h]
        m = jnp.maximum(jnp.max(sb, axis=-1, keepdims=True), jnp.max(sn, axis=-1, keepdims=True))
        pb = jnp.exp(sb - m)
        pn = jnp.exp(sn - m)
        l = jnp.sum(pb, axis=-1, keepdims=True) + jnp.sum(pn, axis=-1, keepdims=True)
        o = _dot(pb.astype(BF), vb[:, sl]) + _dot(pn.astype(BF), vn[:, sl])
        outs.append((o / l).astype(BF))
    o_ref[0] = jnp.concatenate(outs, axis=-1)


def _band_sample_call(qkv, buf_k, buf_v, rel_bias, wb):
    b, t, _ = qkv.shape
    lb = buf_k.shape[1]
    nh = wb // HEAD_DIM
    k_pos = jnp.concatenate([jnp.arange(-lb, 0), jnp.arange(t)])
    rel = jnp.arange(t)[:, None] - k_pos[None, :]
    bias = rel_bias[:, jnp.clip(rel, -MAX_REL, MAX_REL) + MAX_REL].astype(F32)
    bias_b, bias_n = bias[:, :, :lb], bias[:, :, lb:]
    return pl.pallas_call(
        functools.partial(_band_sample_body, nh=nh),
        grid=(b,),
        in_specs=[pl.BlockSpec((1, t, wb), lambda b: (b, 0, 3)),
                  pl.BlockSpec((1, t, wb), lambda b: (b, 0, 4)),
                  pl.BlockSpec((1, t, wb), lambda b: (b, 0, 5)),
                  pl.BlockSpec((1, lb, wb), lambda b: (b, 0, 0)),
                  pl.BlockSpec((1, lb, wb), lambda b: (b, 0, 0)),
                  pl.BlockSpec(bias_b.shape, lambda b: (0, 0, 0)),
                  pl.BlockSpec(bias_n.shape, lambda b: (0, 0, 0))],
        out_specs=pl.BlockSpec((1, t, wb), lambda b: (b, 0, 0)),
        out_shape=jax.ShapeDtypeStruct((b, t, wb), BF),
        compiler_params=_cparams(("arbitrary",)),
        name="band_sample",
    )(qkv, qkv, qkv, buf_k, buf_v, bias_b, bias_n)


def _diff_sample_body(q_ref, kn_ref, vn_ref, ck_ref, cv_ref, lam_ref, gsub_ref, o_ref, m_ref, l_ref, acc_ref,
                      *, nh, lam_init, p_len):
    j = pl.program_id(1)
    q = q_ref[0]
    t = q.shape[0]
    dv = 2 * HEAD_DIM
    lane = lax.broadcasted_iota(jnp.int32, (t, dv), 1)

    @pl.when(j == 0)
    def _():
        _osm_init(m_ref, l_ref, acc_ref)

    def attend(k_of, v_of, dist):
        dist2 = jnp.concatenate([dist, dist], axis=0)
        for h in range(nh):
            x = q[:, h * dv:(h + 1) * dv]
            zero = jnp.zeros_like(x)
            q2 = jnp.concatenate([jnp.where(lane < HEAD_DIM, x, zero), jnp.where(lane >= HEAD_DIM, x, zero)], axis=0)
            s = _dot_nt(q2, k_of(h)) - _alibi_slope(h, nh) * dist2
            _osm_update(s, v_of(h), m_ref, l_ref, acc_ref, h)

    tk = ck_ref.shape[1] // nh
    row = lax.broadcasted_iota(jnp.int32, (t, tk), 0)
    col = lax.broadcasted_iota(jnp.int32, (t, tk), 1)
    attend(lambda h: ck_ref[0, pl.ds(h, tk, stride=nh), :].astype(BF),
           lambda h: cv_ref[0, pl.ds(h, tk, stride=nh), :].astype(BF),
           (p_len + row - j * tk - col).astype(F32))

    @pl.when(j == pl.num_programs(1) - 1)
    def _():
        kn = kn_ref[0]
        vn = vn_ref[0]
        rown = lax.broadcasted_iota(jnp.int32, (t, t), 0)
        coln = lax.broadcasted_iota(jnp.int32, (t, t), 1)
        attend(lambda h: kn[:, h * dv:(h + 1) * dv], lambda h: vn[:, h * dv:(h + 1) * dv],
               jnp.abs(rown - coln).astype(F32))
        lam = _lambda(lam_ref, lam_init)
        g = gsub_ref[...]
        outs = []
        for h in range(nh):
            on = acc_ref[h] / l_ref[h]
            o = on[:t] - lam * on[t:]
            o = o * lax.rsqrt(jnp.mean(o * o, axis=-1, keepdims=True) + RMS_EPS)
            outs.append(((o * g) * (1.0 - lam_init)).astype(BF))
        o_ref[0] = jnp.concatenate(outs, axis=-1)


def _diff_sample_call(qkv, cache_k, cache_v, lam_par, g_sub, wc, lam_init):
    b, t, _ = qkv.shape
    nh = wc // (2 * HEAD_DIM)
    p_len = cache_k.shape[1] // nh
    assert (p_len // CHUNK) * CHUNK == p_len and t <= CHUNK
    tk = min(CACHE_TILE_C, p_len)
    return pl.pallas_call(
        functools.partial(_diff_sample_body, nh=nh, lam_init=lam_init, p_len=p_len),
        grid=(b, p_len // tk),
        in_specs=[pl.BlockSpec((1, t, wc), lambda b, j: (b, 0, 0)),
                  pl.BlockSpec((1, t, wc), lambda b, j: (b, 0, 1)),
                  pl.BlockSpec((1, t, wc), lambda b, j: (b, 0, 2)),
                  pl.BlockSpec((1, tk * nh, 2 * HEAD_DIM), lambda b, j: (b, j, 0)),
                  pl.BlockSpec((1, tk * nh, 2 * HEAD_DIM), lambda b, j: (b, j, 0)),
                  pl.BlockSpec(lam_par.shape, lambda b, j: (0, 0)),
                  pl.BlockSpec(g_sub.shape, lambda b, j: (0, 0))],
        out_specs=pl.BlockSpec((1, t, wc), lambda b, j: (b, 0, 0)),
        out_shape=jax.ShapeDtypeStruct((b, t, wc), BF),
        scratch_shapes=[pltpu.VMEM((nh, 2 * t, 1), F32), pltpu.VMEM((nh, 2 * t, 1), F32),
                        pltpu.VMEM((nh, 2 * t, 2 * HEAD_DIM), F32)],
        compiler_params=_cparams(("arbitrary", "arbitrary")),
        name="diff_sample",
    )(qkv, qkv, qkv, cache_k, cache_v, lam_par, g_sub)


def _out_ffn_body(x_ref, o_ref, gm_ref, g_ref, sh_ref, sc_ref, gf_ref, wo_ref, w1_ref, w3_ref, w2_ref, y_ref,
                  *, f_chunk):
    x1 = x_ref[0] + gm_ref[0] * _dot(o_ref[0], wo_ref[...])
    h = _modulate(x1, g_ref[...], sh_ref[0], sc_ref[0]).astype(BF)
    f_total = w1_ref.shape[1]
    f = None
    for c0 in range(0, f_total, f_chunk):
        a = _dot(h, w1_ref[:, c0:c0 + f_chunk])
        g = _dot(h, w3_ref[:, c0:c0 + f_chunk])
        u = ((a * _sigmoid(a)) * g).astype(BF)
        part = _dot(u, w2_ref[c0:c0 + f_chunk, :])
        f = part if f is None else f + part
    y_ref[0] = x1 + gf_ref[0] * f


def _out_ffn_call(x, o, gate_m, g, shift, scale, gate_f, wo, w1, w3, w2):
    bx, sx, d = x.shape
    tm = min(ROW_TILE, sx)
    f_total = w1.shape[1]
    f_chunk = f_total // 2 if (f_total // 2) % LANES == 0 else f_total
    row = lambda b, i: (b, i, 0)
    return pl.pallas_call(
        functools.partial(_out_ffn_body, f_chunk=f_chunk),
        grid=(bx, sx // tm),
        in_specs=[pl.BlockSpec((1, tm, d), row), pl.BlockSpec((1, tm, o.shape[-1]), row),
                  _mod_spec(gate_m, tm), _const_spec(g), _mod_spec(shift, tm), _mod_spec(scale, tm),
                  _mod_spec(gate_f, tm), _const_spec(wo), _const_spec(w1), _const_spec(w3), _const_spec(w2)],
        out_specs=pl.BlockSpec((1, tm, d), row),
        out_shape=jax.ShapeDtypeStruct((bx, sx, d), F32),
        compiler_params=_cparams(("arbitrary", "arbitrary")),
        name="out_ffn",
    )(x, o, gate_m, g, shift, scale, gate_f, wo, w1, w3, w2)


def _store_tile_rows(ref, x):
    rows, d = x.shape
    nseg = d // LANES
    for s in range(nseg):
        ref[pl.ds(s, rows, stride=nseg), :] = x[:, s * LANES:(s + 1) * LANES]


def _load_tile_rows(ref, rows, d):
    nseg = d // LANES
    return jnp.concatenate([ref[pl.ds(s, rows, stride=nseg), :] for s in range(nseg)], axis=-1)


def _out_router_body(x_ref, o_ref, gm_ref, g_ref, sh_ref, sc_ref, wo_ref, wr_ref, h_in_ref,
                     x1_ref, h_ref, route_ref, *, n_exp):
    del h_in_ref
    x1 = x_ref[0] + gm_ref[0] * _dot(o_ref[0], wo_ref[...])
    x1_ref[0] = x1
    h = _modulate(x1, g_ref[...], sh_ref[0], sc_ref[0])
    hb = h.astype(BF)
    _store_tile_rows(h_ref, hb.astype(F32))
    logits = _dot(hb, wr_ref[0]) + _dot(hb, wr_ref[1]) + _dot(hb, wr_ref[2])
    col = lax.broadcasted_iota(jnp.int32, logits.shape, 1)
    lg = jnp.where(col < n_exp, logits, NEG_INF)
    e = jnp.exp(lg - jnp.max(lg, axis=-1, keepdims=True))
    probs = e / jnp.sum(e, axis=-1, keepdims=True)
    p1 = jnp.max(probs, axis=-1, keepdims=True)
    i1 = jnp.min(jnp.where(probs == p1, col, LANES), axis=-1, keepdims=True)
    rest = jnp.where(col == i1, -1.0, probs)
    p2 = jnp.max(rest, axis=-1, keepdims=True)
    i2 = jnp.min(jnp.where(rest == p2, col, LANES), axis=-1, keepdims=True)
    tot = p1 + p2
    route = jnp.where(col == 0, i1.astype(F32),
                      jnp.where(col == 1, i2.astype(F32),
                                jnp.where(col == 2, p1 / tot, jnp.where(col == 3, p2 / tot, 0.0))))
    route_ref[0] = route


def _out_router_call(x, o, gate_m, g, shift, scale, wo, wr3, n_exp, tok_offset, h_all):
    bx, sx, d = x.shape
    tm = min(ROW_TILE, sx)
    nt = sx // tm
    nseg = d // LANES
    assert tok_offset % tm == 0
    row = lambda b, i: (b, i, 0)
    return pl.pallas_call(
        functools.partial(_out_router_body, n_exp=n_exp),
        grid=(bx, nt),
        in_specs=[pl.BlockSpec((1, tm, d), row), pl.BlockSpec((1, tm, o.shape[-1]), row),
                  _mod_spec(gate_m, tm), _const_spec(g), _mod_spec(shift, tm), _mod_spec(scale, tm),
                  _const_spec(wo), _const_spec(wr3), pl.BlockSpec(memory_space=pl.ANY)],
        out_specs=[pl.BlockSpec((1, tm, d), row),
                   pl.BlockSpec((tm * nseg, LANES), lambda b, i: (tok_offset // tm + b * nt + i, 0)),
                   pl.BlockSpec((1, tm, LANES), row)],
        out_shape=[jax.ShapeDtypeStruct((bx, sx, d), F32),
                   jax.ShapeDtypeStruct(h_all.shape, F32),
                   jax.ShapeDtypeStruct((bx, sx, LANES), F32)],
        input_output_aliases={8: 1},
        compiler_params=_cparams(("arbitrary", "arbitrary")),
        name="out_router",
    )(x, o, gate_m, g, shift, scale, wo, wr3, h_all)


def _moe_ffn_body(te_ref, nv_ref, idx_ref, idx_next_ref, h_ref, w1_ref, w3_ref, w2_ref, y_ref, xbuf, xb16, acc, sem,
                  *, nseg, nf):
    t = pl.program_id(0)
    j = pl.program_id(1)
    nv = nv_ref[0]
    tm, d = xb16.shape
    slot = lax.rem(t, 2)

    def issue(rows_ref, s):
        def body(r, c):
            src = pl.multiple_of(rows_ref[0, 0, r], nseg)
            dst = pl.multiple_of(r * nseg, nseg)
            pltpu.make_async_copy(h_ref.at[pl.ds(src, nseg)], xbuf.at[s, pl.ds(dst, nseg)], sem.at[s]).start()
            return c
        lax.fori_loop(0, tm, body, 0, unroll=8)

    @pl.when((j == 0) & (t == 0))
    def _():
        issue(idx_ref, 0)

    @pl.when((j == 0) & (t + 1 < nv))
    def _():
        issue(idx_next_ref, 1 - slot)

    @pl.when((j == 0) & (t < nv))
    def _():
        pltpu.make_async_copy(h_ref.at[pl.ds(0, tm * nseg)], xbuf.at[slot], sem.at[slot]).wait()
        xb16[...] = _load_tile_rows(xbuf.at[slot], tm, d).astype(BF)

    @pl.when(t < nv)
    def _():
        xb = xb16[...]
        a = _dot(xb, w1_ref[0])
        g = _dot(xb, w3_ref[0])
        u = ((a * _sigmoid(a)) * g).astype(BF)
        part = _dot(u, w2_ref[0])

        if nf == 1:
            _store_tile_rows(y_ref, part)
        else:
            @pl.when(j == 0)
            def _():
                acc[...] = part

            @pl.when((j > 0) & (j < nf - 1))
            def _():
                acc[...] += part

            @pl.when(j == nf - 1)
            def _():
                _store_tile_rows(y_ref, acc[...] + part)

    @pl.when((t >= nv) & (j == nf - 1))
    def _():
        y_ref[...] = jnp.zeros(y_ref.shape, F32)


def _moe_ffn_call(h_all, row_tok, tile_e, n_valid, w1, w3, w2):
    n_rows = row_tok.shape[0]
    d = w1.shape[1]
    nseg = d // LANES
    f_total = w1.shape[2]
    tf = MOE_F_TILE if f_total % MOE_F_TILE == 0 else f_total
    nf = f_total // tf
    tm = MOE_ROW_TILE
    n_tiles = n_rows // tm
    idx3 = (row_tok * nseg).reshape(n_tiles, 1, tm)

    def fcol(t, j, te, nv):
        return jnp.where(t < nv[0], j, nf - 1)

    grid_spec = pltpu.PrefetchScalarGridSpec(
        num_scalar_prefetch=2,
        grid=(n_tiles, nf),
        in_specs=[pl.BlockSpec((1, 1, tm), lambda t, j, te, nv: (t, 0, 0), memory_space=pltpu.SMEM),
                  pl.BlockSpec((1, 1, tm), lambda t, j, te, nv: (jnp.minimum(t + 1, n_tiles - 1), 0, 0),
                               memory_space=pltpu.SMEM),
                  pl.BlockSpec(memory_space=pl.ANY),
                  pl.BlockSpec((1, d, tf), lambda t, j, te, nv: (te[t], 0, fcol(t, j, te, nv))),
                  pl.BlockSpec((1, d, tf), lambda t, j, te, nv: (te[t], 0, fcol(t, j, te, nv))),
                  pl.BlockSpec((1, tf, d), lambda t, j, te, nv: (te[t], fcol(t, j, te, nv), 0))],
        out_specs=pl.BlockSpec((tm * nseg, LANES), lambda t, j, te, nv: (t, 0)),
        scratch_shapes=[pltpu.VMEM((2, tm * nseg, LANES), F32), pltpu.VMEM((tm, d), BF), pltpu.VMEM((tm, d), F32),
                        pltpu.SemaphoreType.DMA((2,))],
    )
    return pl.pallas_call(
        functools.partial(_moe_ffn_body, nseg=nseg, nf=nf),
        grid_spec=grid_spec,
        out_shape=jax.ShapeDtypeStruct((n_rows * nseg, LANES), F32),
        compiler_params=_cparams(("arbitrary", "arbitrary")),
        name="moe_ffn",
    )(tile_e, n_valid, idx3, idx3, h_all, w1, w3, w2)


def _combine_body(d0_ref, d1_ref, x_ref, gf_ref, w_ref, g_ref, yb_ref, o_ref, buf, sem):
    t = pl.program_id(0)
    nt = pl.num_programs(0)
    _, tc, d = x_ref.shape
    nseg = d // LANES

    def row_copy(idx_ref, r, slot, k):
        src = pl.multiple_of(idx_ref[0, 0, 2 * r + k], nseg)
        dst = pl.multiple_of(r * nseg, nseg)
        return pltpu.make_async_copy(yb_ref.at[pl.ds(src, nseg)], buf.at[slot, k, pl.ds(dst, nseg)], sem.at[slot])

    def issue(idx_ref, slot):
        def body(r, c):
            row_copy(idx_ref, r, slot, 0).start()
            row_copy(idx_ref, r, slot, 1).start()
            return c
        lax.fori_loop(0, tc, body, 0, unroll=4)

    slot = lax.rem(t, 2)

    @pl.when(t == 0)
    def _():
        issue(d0_ref, 0)

    @pl.when(t + 1 < nt)
    def _():
        issue(d1_ref, 1 - slot)

    for k in range(TOP_K):
        pltpu.make_async_copy(yb_ref.at[pl.ds(0, tc * nseg)], buf.at[slot, k], sem.at[slot]).wait()
    w = w_ref[0]
    f = (w[:, 2:3] * _load_tile_rows(buf.at[slot, 0], tc, d)
         + w[:, 3:4] * _load_tile_rows(buf.at[slot, 1], tc, d))
    y = x_ref[0] + gf_ref[0] * f
    y = y * lax.rsqrt(jnp.mean(y * y, axis=-1, keepdims=True) + RMS_EPS)
    o_ref[0] = y * g_ref[...]


def _combine_call(x1, gate_f, route, g_final, yb, dest):
    bx, sx, d = x1.shape
    nseg = d // LANES
    tc = min(COMBINE_TILE, sx)
    nps = sx // tc
    nt = bx * nps
    dest3 = (dest * nseg).reshape(nt, 1, 2 * tc)
    row = lambda t: (t // nps, t % nps, 0)
    if gate_f.shape[1] == 1:
        gate_spec = pl.BlockSpec((1, 1, d), lambda t: (t // nps, 0, 0))
    else:
        gate_spec = pl.BlockSpec((1, tc, d), row)
    return pl.pallas_call(
        _combine_body,
        grid=(nt,),
        in_specs=[pl.BlockSpec((1, 1, 2 * tc), lambda t: (t, 0, 0), memory_space=pltpu.SMEM),
                  pl.BlockSpec((1, 1, 2 * tc), lambda t: (jnp.minimum(t + 1, nt - 1), 0, 0),
                               memory_space=pltpu.SMEM),
                  pl.BlockSpec((1, tc, d), row), gate_spec,
                  pl.BlockSpec((1, tc, LANES), row),
                  pl.BlockSpec(g_final.shape, lambda t: (0, 0)),
                  pl.BlockSpec(memory_space=pl.ANY)],
        out_specs=pl.BlockSpec((1, tc, d), row),
        out_shape=jax.ShapeDtypeStruct((bx, sx, d), F32),
        scratch_shapes=[pltpu.VMEM((2, 2, tc * nseg, LANES), F32), pltpu.SemaphoreType.DMA((2,))],
        compiler_params=_cparams(("arbitrary",)),
        name="moe_combine",
    )(dest3, dest3, x1, gate_f, route, g_final, yb)


def _route_plan(route, n_exp, tm):
    n_tok = route.shape[0]
    n_slots = n_tok * TOP_K
    slot_e = route[:, :TOP_K].astype(jnp.int32).reshape(n_slots)
    onehot = (slot_e[:, None] == jnp.arange(n_exp, dtype=jnp.int32)[None, :]).astype(jnp.int32)
    csum = jnp.cumsum(onehot, axis=0)
    rank = jnp.sum(csum * onehot, axis=1) - 1
    counts = csum[-1]
    padded = ((counts + tm - 1) // tm) * tm
    pad_end = jnp.cumsum(padded)
    pad_start = pad_end - padded
    dest = jnp.sum(onehot * pad_start[None, :], axis=1) + rank
    n_tiles = -(-n_slots // tm) + n_exp
    n_rows = n_tiles * tm
    n_valid = (pad_end[-1] // tm).astype(jnp.int32)
    tile_start = jnp.arange(n_tiles, dtype=jnp.int32) * tm
    tile_e = jnp.minimum(jnp.searchsorted(pad_end, tile_start, side='right'), n_exp - 1).astype(jnp.int32)
    last_e = tile_e[jnp.maximum(n_valid - 1, 0)]
    tile_e = jnp.where(jnp.arange(n_tiles) < n_valid, tile_e, last_e)
    sorted_tok = (jnp.argsort(slot_e, stable=True) // TOP_K).astype(jnp.int32)
    sorted_tok = jnp.concatenate([sorted_tok, jnp.zeros((tm,), jnp.int32)])
    first = jnp.cumsum(counts) - counts - pad_start
    tile_first = jnp.minimum(tile_start + first[tile_e], n_slots)
    row_tok = jax.vmap(lambda o: lax.dynamic_slice(sorted_tok, (o,), (tm,)))(tile_first).reshape(n_rows)
    return dest, row_tok, tile_e, n_valid.reshape(1)


def _moe(parts, h_all, g_final, w1, w3, w2):
    n_exp = w1.shape[0]
    route_all = jnp.concatenate([r.reshape(-1, LANES)[:, :2 * TOP_K] for _, r, _ in parts], axis=0)
    dest, row_tok, tile_e, n_valid = _route_plan(route_all, n_exp, MOE_ROW_TILE)
    yb = _moe_ffn_call(h_all, row_tok, tile_e, n_valid, w1, w3, w2)
    outs = []
    first = 0
    for x1, route, gate_f in parts:
        n_slots = x1.shape[0] * x1.shape[1] * TOP_K
        outs.append(_combine_call(x1, gate_f, route, g_final, yb, dest[first:first + n_slots]))
        first += n_slots
    return outs


def _trunk(x, mods, p, cache, tok_offset, h_all):
    sh_m0, sc_m0, gt_m0, sh_f0, sc_f0, gt_f0 = mods[0]
    sh_m1, sc_m1, gt_m1, sh_f1, sc_f1, gt_f1 = mods[1]
    wa = p['wa']
    wb = p['wb']
    wc = p['wc']
    bx, sx, d = x.shape

    if cache is None:
        keep = min(WINDOW_B, sx)
        qk, vt, ka, va, kb, vb, logf = _inproj_even_call(
            x, p['g_mix0'], sh_m0, sc_m0, p['w_in_even'], p['w_f'], p['b_f'], keep, True)
        o_a = _fox_prompt_call(qk, vt, wa)
        o_b = _band_prompt_call(qk, vt, p['rel_bias'], wa, wb)
        x_l0_state = (ka, va, logf, kb, vb)
    else:
        ck, cv, clogf, bk, bv, cck, ccv, b, t = cache
        qkv, ka, va, kb, vb, logf, lft, = _inproj_even_call(
            x, p['g_mix0'], sh_m0, sc_m0, p['w_in_even'], p['w_f'], p['b_f'], sx, False)
        nh = wa // HEAD_DIM
        p_len = ck.shape[1]
        lft_new = lft.reshape(nh, b, t).transpose(1, 0, 2)
        lp = -(-(p_len + t) // 512) * 512
        lcat = jnp.concatenate([clogf.transpose(0, 2, 1), lft_new,
                                jnp.zeros((b, nh, lp - p_len - t), F32)], axis=-1)
        ft_pad = _cumsum_call(lcat)
        qkv_b = qkv.reshape(b, t, qkv.shape[-1])
        o_a = _fox_sample_call(qkv_b, ck, cv, ft_pad, wa).reshape(1, b * t, wa)
        o_b = _band_sample_call(qkv_b, bk, bv, p['rel_bias'], wb).reshape(1, b * t, wb)
        x_l0_state = (ka, va, logf, kb, vb)
    o = jnp.concatenate([o_a, o_b], axis=-1)
    x = _out_ffn_call(x, o, gt_m0, p['g_ffn0'], sh_f0, sc_f0, gt_f0,
                      p['w_out_even'], p['w1_dense'], p['w3_dense'], p['w2_dense'])

    if cache is None:
        qk, vt, kc, vc = _inproj_odd_call(x, p['g_mix1'], sh_m1, sc_m1, p['w_in_odd'], True)
        o = _diff_prompt_call(qk, vt, p['lam_par'], p['g_sub'].reshape(-1, 1), wc, p['lam_init'])
    else:
        qkv, kc, vc = _inproj_odd_call(x, p['g_mix1'], sh_m1, sc_m1, p['w_in_odd'], False)
        qkv_b = qkv.reshape(b, t, qkv.shape[-1])
        o = _diff_sample_call(qkv_b, cck, ccv, p['lam_par'], p['g_sub'], wc, p['lam_init']).reshape(1, b * t, wc)
    x1, h_all, route = _out_router_call(x, o, gt_m1, p['g_ffn1'], sh_f1, sc_f1, p['w_out_odd'], p['w_router3'],
                                        p['n_exp'], tok_offset, h_all)
    return (x1, route, gt_f1), h_all, x_l0_state, (kc, vc)


def kernel(x_prompt, x_sample, cache_a_k, cache_a_v, cache_a_logf, cache_b_k, cache_b_v, cache_c_k, cache_c_v,
           c_prompt, c_sample, w_mod, b_mod, g_mix, g_ffn, g_final, w_in_even, b_forget, rel_bias, w_out_even,
           w_in_odd, lambda_q1, lambda_k1, lambda_q2, lambda_k2, g_subln, w_out_odd, w1_dense, w3_dense, w2_dense,
           w_router, w1_moe, w3_moe, w2_moe):
    bp, sp, d = x_prompt.shape
    bs, ts, _ = x_sample.shape
    h_a = cache_a_k.shape[3]
    h_b = cache_b_k.shape[3]
    h_c = cache_c_k.shape[3]
    wa, wb, wc = h_a * HEAD_DIM, h_b * HEAD_DIM, h_c * 2 * HEAD_DIM
    n_exp = w_router.shape[-1]
    assert w_mod.shape[0] == 2, "kernel is written for the 2-layer trunk"

    we = w_in_even[0]
    w_main = jnp.concatenate([we[:, :3 * wa], we[:, 3 * wa + h_a:]], axis=1).astype(BF)
    w_f = jnp.pad(we[:, 3 * wa:3 * wa + h_a], ((0, 0), (0, LANES - h_a))).astype(BF)
    b_f = jnp.pad(b_forget[0], (0, LANES - h_a)).reshape(1, LANES).astype(F32)
    wr = jnp.pad(w_router[0], ((0, 0), (0, LANES - n_exp)))
    wr_hi = wr.astype(BF)
    wr_r1 = wr - wr_hi.astype(F32)
    wr_mid = wr_r1.astype(BF)
    wr_lo = (wr_r1 - wr_mid.astype(F32)).astype(BF)
    params = dict(
        wa=wa, wb=wb, wc=wc, n_exp=n_exp, lam_init=0.8 - 0.6 * math.exp(-0.3 * 1),
        g_mix0=g_mix[0:1], g_mix1=g_mix[1:2], g_ffn0=g_ffn[0:1], g_ffn1=g_ffn[1:2],
        g_final=g_final.reshape(1, d),
        w_in_even=w_main, w_f=w_f, b_f=b_f, rel_bias=rel_bias[0],
        w_out_even=w_out_even[0].astype(BF), w_in_odd=w_in_odd[0].astype(BF),
        lam_par=jnp.concatenate([lambda_q1, lambda_k1, lambda_q2, lambda_k2], axis=0),
        g_sub=g_subln[0:1], w_out_odd=w_out_odd[0].astype(BF),
        w1_dense=w1_dense[0].astype(BF), w3_dense=w3_dense[0].astype(BF), w2_dense=w2_dense[0].astype(BF),
        w_router3=jnp.stack([wr_hi, wr_mid, wr_lo]),
        w1_moe=w1_moe[0].astype(BF), w3_moe=w3_moe[0].astype(BF), w2_moe=w2_moe[0].astype(BF),
    )

    mod = _mod_call(jnp.concatenate([c_prompt, c_sample], axis=0), w_mod, b_mod)

    def mods_for(rows, per_row_len):
        out = []
        for l in range(2):
            parts = []
            for k in range(6):
                m = mod[l, rows, k * d:(k + 1) * d]
                if per_row_len:
                    m = jnp.repeat(m, per_row_len, axis=0).reshape(1, -1, d)
                else:
                    m = m.reshape(-1, 1, d)
                parts.append(m)
            out.append(parts)
        return out

    h_all = jnp.zeros(((bp * sp + bs * ts) * (d // LANES), LANES), F32)
    moe_p, h_all, ev_p, od_p = _trunk(x_prompt, mods_for(slice(0, bp), 0), params, None, 0, h_all)
    cache = (cache_a_k[0].reshape(bs, -1, wa), cache_a_v[0].reshape(bs, -1, wa), cache_a_logf[0],
             cache_b_k[0].reshape(bs, -1, wb), cache_b_v[0].reshape(bs, -1, wb),
             cache_c_k[0].reshape(bs, -1, 2 * HEAD_DIM), cache_c_v[0].reshape(bs, -1, 2 * HEAD_DIM), bs, ts)
    moe_s, h_all, ev_s, od_s = _trunk(x_sample.reshape(1, bs * ts, d), mods_for(slice(bp, bp + bs), ts), params,
                                      cache, bp * sp, h_all)
    y_p, y_s = _moe([moe_p, moe_s], h_all, params['g_final'], params['w1_moe'], params['w3_moe'], params['w2_moe'])

    ka, va, logf, kb, vb = ev_p
    keep = kb.shape[1]
    out_p = (ka.reshape(1, bp, sp, h_a, HEAD_DIM), va.reshape(1, bp, sp, h_a, HEAD_DIM),
             logf.reshape(1, bp, sp, h_a),
             kb.reshape(1, bp, keep, h_b, HEAD_DIM), vb.reshape(1, bp, keep, h_b, HEAD_DIM),
             od_p[0].reshape(1, bp, sp, h_c, 2 * HEAD_DIM), od_p[1].reshape(1, bp, sp, h_c, 2 * HEAD_DIM))
    ka, va, logf, kb, vb = ev_s
    new_bk = jnp.concatenate([cache_b_k[0], kb.reshape(bs, ts, h_b, HEAD_DIM)], axis=1)[:, ts:]
    new_bv = jnp.concatenate([cache_b_v[0], vb.reshape(bs, ts, h_b, HEAD_DIM)], axis=1)[:, ts:]
    out_s = (ka.reshape(1, bs, ts, h_a, HEAD_DIM), va.reshape(1, bs, ts, h_a, HEAD_DIM),
             logf.reshape(1, bs, ts, h_a), new_bk[None], new_bv[None],
             od_s[0].reshape(1, bs, ts, h_c, 2 * HEAD_DIM), od_s[1].reshape(1, bs, ts, h_c, 2 * HEAD_DIM))
    return (y_p, y_s.reshape(bs, ts, d)) + out_p + out_s
```

```python
import functools
import math

import jax
import jax.numpy as jnp
from jax import lax
from jax.experimental import pallas as pl
from jax.experimental.pallas import tpu as pltpu

BF = jnp.bfloat16
F32 = jnp.float32

CHUNK = 64
CHUNK_SHIFT = 6
HEAD_DIM = 64
BAND_CHUNKS = 8
WINDOW_B = BAND_CHUNKS * CHUNK
MAX_REL = 128
TOP_K = 2
RMS_EPS = 1e-6
NEG_INF = -1e30
QK_SCALE = HEAD_DIM ** -0.5

LANES = 128
SUBLANES = 8
VMEM_LIMIT_BYTES = 56 * 1024 * 1024
ROUTE_ROWS = SUBLANES

ROW_TILE = 512
ATTN_TILE = 512
BAND_Q_TILE = WINDOW_B // 2
CACHE_TILE_A = 1024
CACHE_TILE_C = 512
MOE_ROW_TILE = 512
MOE_F_TILE = 1792
COMBINE_TILE = 256


def _cparams(sem):
    return pltpu.CompilerParams(dimension_semantics=sem, vmem_limit_bytes=VMEM_LIMIT_BYTES)


def _dot(a, b):
    return jnp.dot(a, b, preferred_element_type=F32)


def _dot_nt(a, b):
    return lax.dot_general(a, b, (((1,), (1,)), ((), ())), preferred_element_type=F32)


def _sigmoid(x):
    return 1.0 / (1.0 + jnp.exp(-x))


def _modulate(x, g, shift, scale):
    y = x * lax.rsqrt(jnp.mean(x * x, axis=-1, keepdims=True) + RMS_EPS)
    return (y * g) * (1.0 + scale) + shift


def _split3(x):
    hi = x.astype(BF)
    r1 = x - hi.astype(F32)
    mid = r1.astype(BF)
    lo = (r1 - mid.astype(F32)).astype(BF)
    return hi, mid, lo


def _cumsum_lanes(x, upper):
    hi, mid, lo = _split3(x)
    return _dot(hi, upper) + _dot(mid, upper) + _dot(lo, upper)


def _upper_tri(t):
    r = jnp.arange(t)
    return (r[:, None] <= r[None, :]).astype(BF)


def _osm_update(s, v, m_ref, l_ref, acc_ref, idx):
    m_prev = m_ref[idx]
    m_new = jnp.maximum(m_prev, jnp.max(s, axis=-1, keepdims=True))
    alpha = jnp.exp(m_prev - m_new)
    p = jnp.exp(s - m_new)
    l_ref[idx] = alpha * l_ref[idx] + jnp.sum(p, axis=-1, keepdims=True)
    acc_ref[idx] = alpha * acc_ref[idx] + _dot(p.astype(BF), v)
    m_ref[idx] = m_new


def _osm_init(m_ref, l_ref, acc_ref):
    m_ref[...] = jnp.full(m_ref.shape, NEG_INF, F32)
    l_ref[...] = jnp.zeros(l_ref.shape, F32)
    acc_ref[...] = jnp.zeros(acc_ref.shape, F32)


def _mod_body(c_ref, w_ref, b_ref, o_ref):
    c = c_ref[...]
    s = (c * _sigmoid(c)).astype(BF)
    o_ref[0] = _dot(s, w_ref[0].astype(BF)) + b_ref[0]


def _mod_call(c_all, w_mod, b_mod):
    depth, d, n = w_mod.shape
    r = c_all.shape[0]
    tn = 1536 if n % 1536 == 0 else n
    return pl.pallas_call(
        _mod_body,
        grid=(depth, n // tn),
        in_specs=[pl.BlockSpec((r, d), lambda l, j: (0, 0)),
                  pl.BlockSpec((1, d, tn), lambda l, j: (l, 0, j)),
                  pl.BlockSpec((1, 1, tn), lambda l, j: (l, 0, j))],
        out_specs=pl.BlockSpec((1, r, tn), lambda l, j: (l, 0, j)),
        out_shape=jax.ShapeDtypeStruct((depth, r, n), F32),
        compiler_params=_cparams(("arbitrary", "arbitrary")),
        name="mod",
    )(c_all, w_mod, b_mod.reshape(depth, 1, n))


def _mod_spec(m, tm):
    arr, l, k = m
    d = arr.shape[-1]
    if arr.shape[3] == 1:
        return pl.BlockSpec((1, 1, 1, 1, d), lambda b, i: (l, k, b, 0, 0))
    return pl.BlockSpec((1, 1, 1, tm, d), lambda b, i: (l, k, b, i, 0))


def _const_spec(arr):
    nd = arr.ndim
    return pl.BlockSpec(arr.shape, lambda b, i: (0,) * nd, pipeline_mode=pl.Buffered(1))


def _head_blocks(x, augs):
    tm, w = x.shape
    first = lax.broadcasted_iota(jnp.int32, (tm, LANES), 1) < HEAD_DIM
    out = []
    for p in range(w // LANES):
        pair = x[:, p * LANES:(p + 1) * LANES]
        out.append(jnp.where(first, pair, augs[2 * p]).astype(BF))
        out.append(jnp.where(first, pltpu.roll(pair, HEAD_DIM, 1), augs[2 * p + 1]).astype(BF))
    return out


def _lane_range_row(lo, hi, value):
    lane = lax.broadcasted_iota(jnp.int32, (1, LANES), 1)
    return jnp.where((lane >= lo) & (lane < hi), value, 0.0).astype(F32)


def _ones_rows(t):
    return jnp.where(lax.broadcasted_iota(jnp.int32, (16, t), 0) == 0, 1.0, 0.0).astype(BF)


def _log_forget(h, wf_ref, bf_ref):
    z = _dot(h, wf_ref[...]) + bf_ref[...]
    return jnp.minimum(z, 0.0) - jnp.log1p(jnp.exp(-jnp.abs(z)))


def _inproj_even_body(*refs, wa, wb, prompt):
    if prompt:
        (x_ref, g_ref, sh_ref, sc_ref, w_ref, wf_ref, bf_ref, low_ref, place_ref,
         qk_ref, vt_ref, ka_ref, va_ref, kb_ref, vb_ref, logf_ref, carry_ref) = refs
    else:
        (x_ref, g_ref, sh_ref, sc_ref, w_ref, wf_ref, bf_ref,
         qkv_ref, ka_ref, va_ref, kb_ref, vb_ref, logf_ref, lft_ref) = refs
    i = pl.program_id(1)
    h = _modulate(x_ref[0], g_ref[...], sh_ref[0, 0, 0], sc_ref[0, 0, 0]).astype(BF)
    tm = h.shape[0]
    nh = logf_ref.shape[-1]

    qa = _dot(h, w_ref[:, 0:wa])
    ka = _dot(h, w_ref[:, wa:2 * wa])
    va = _dot(h, w_ref[:, 2 * wa:3 * wa])
    o = 3 * wa
    qb = _dot(h, w_ref[:, o:o + wb])
    kb = _dot(h, w_ref[:, o + wb:o + 2 * wb])
    vb = _dot(h, w_ref[:, o + 2 * wb:o + 3 * wb])
    ka_ref[0] = ka
    va_ref[0] = va
    kb_ref[0] = kb
    vb_ref[0] = vb
    logf = _log_forget(h, wf_ref, bf_ref)
    logf_ref[0] = logf[:, :nh]

    if not prompt:
        qkv_ref[0] = jnp.concatenate(
            [(qa * QK_SCALE).astype(BF), ka.astype(BF), va.astype(BF),
             (qb * QK_SCALE).astype(BF), kb.astype(BF), vb.astype(BF)], axis=-1)
        lft_ref[0] = logf.T[:nh, :]
        return

    @pl.when(i == 0)
    def _():
        carry_ref[...] = jnp.zeros(carry_ref.shape, F32)

    low = low_ref[...]
    hi, mid, lo = _split3(logf)
    f_cum = _dot(low, hi) + _dot(low, mid) + _dot(low, lo) + carry_ref[0:1, :]
    carry_ref[...] = jnp.broadcast_to(f_cum[tm - 1:tm, :], carry_ref.shape)
    nhi, nmid, nlo = _split3(-f_cum)
    k_aug = _dot(nhi, place_ref[0]) + _dot(nmid, place_ref[1]) + _dot(nlo, place_ref[2])
    q_ones = _lane_range_row(HEAD_DIM, HEAD_DIM + 3, 1.0)
    zero = jnp.zeros((1, LANES), F32)
    nhb = wb // HEAD_DIM
    blocks = (_head_blocks(qa * QK_SCALE, [q_ones] * nh)
              + _head_blocks(ka, [k_aug[:, n * LANES:(n + 1) * LANES] for n in range(nh)])
              + _head_blocks(qb * QK_SCALE, [zero] * nhb)
              + _head_blocks(kb, [zero] * nhb))
    qk_ref[0] = jnp.concatenate(blocks, axis=-1)
    ones = _ones_rows(tm)
    vt_ref[0] = jnp.concatenate([va.T.astype(BF), ones, vb.T.astype(BF), ones], axis=0)


def _lower_tri(t):
    r = jnp.arange(t)
    return (r[:, None] >= r[None, :]).astype(BF)


def _place_matrices(nh):
    s = jnp.arange(3)[:, None, None]
    r = jnp.arange(LANES)[None, :, None]
    c = jnp.arange(nh * LANES)[None, None, :]
    return ((r < nh) & (c == r * LANES + HEAD_DIM + s)).astype(BF)


def _inproj_even_call(x, g, shift, scale, w_main, w_f, b_f, keep, prompt):
    bx, sx, d = x.shape
    tm = min(ROW_TILE, sx)
    nt = sx // tm
    n = w_main.shape[1]
    wa = wb = n // 6
    nh = wa // HEAD_DIM
    nkeep = keep // tm
    tail = lambda b, i: (b, jnp.maximum(i - (nt - nkeep), 0), 0)
    row = lambda b, i: (b, i, 0)
    in_specs = [pl.BlockSpec((1, tm, d), row), _const_spec(g), _mod_spec(shift, tm), _mod_spec(scale, tm),
                _const_spec(w_main), _const_spec(w_f), _const_spec(b_f)]
    args = [x, g, shift[0], scale[0], w_main, w_f, b_f]
    f32_specs = [pl.BlockSpec((1, tm, wa), row), pl.BlockSpec((1, tm, wa), row),
                 pl.BlockSpec((1, tm, wb), tail), pl.BlockSpec((1, tm, wb), tail),
                 pl.BlockSpec((1, tm, nh), row)]
    f32_shapes = [jax.ShapeDtypeStruct((bx, sx, wa), F32), jax.ShapeDtypeStruct((bx, sx, wa), F32),
                  jax.ShapeDtypeStruct((bx, keep, wb), F32), jax.ShapeDtypeStruct((bx, keep, wb), F32),
                  jax.ShapeDtypeStruct((bx, sx, nh), F32)]
    scratch = []
    if prompt:
        low, place = _lower_tri(tm), _place_matrices(nh)
        in_specs += [_const_spec(low), _const_spec(place)]
        args += [low, place]
        nqk = 2 * (wa + wb) // HEAD_DIM * LANES
        nvt = wa + wb + 32
        out_specs = [pl.BlockSpec((1, tm, nqk), row), pl.BlockSpec((1, nvt, tm), lambda b, i: (b, 0, i))] + f32_specs
        out_shape = [jax.ShapeDtypeStruct((bx, sx, nqk), BF), jax.ShapeDtypeStruct((bx, nvt, sx), BF)] + f32_shapes
        scratch.append(pltpu.VMEM((8, LANES), F32))
    else:
        out_specs = [pl.BlockSpec((1, tm, n), row)] + f32_specs + [pl.BlockSpec((1, nh, tm), lambda b, i: (b, 0, i))]
        out_shape = [jax.ShapeDtypeStruct((bx, sx, n), BF)] + f32_shapes + [jax.ShapeDtypeStruct((bx, nh, sx), F32)]
    return pl.pallas_call(
        functools.partial(_inproj_even_body, wa=wa, wb=wb, prompt=prompt),
        grid=(bx, nt), in_specs=in_specs, out_specs=out_specs, out_shape=out_shape,
        scratch_shapes=scratch,
        compiler_params=_cparams(("arbitrary", "arbitrary")),
        name="inproj_even",
    )(*args)


def _inproj_odd_body(*refs, wc, prompt):
    if prompt:
        x_ref, g_ref, sh_ref, sc_ref, w_ref, qk_ref, vt_ref, k_ref, v_ref = refs
    else:
        x_ref, g_ref, sh_ref, sc_ref, w_ref, qkv_ref, k_ref, v_ref = refs
    i = pl.program_id(1)
    h = _modulate(x_ref[0], g_ref[...], sh_ref[0, 0, 0], sc_ref[0, 0, 0]).astype(BF)
    tm = h.shape[0]
    q = _dot(h, w_ref[:, 0:wc])
    k = _dot(h, w_ref[:, wc:2 * wc])
    v = _dot(h, w_ref[:, 2 * wc:3 * wc])
    k_ref[0] = k
    v_ref[0] = v
    if not prompt:
        qkv_ref[0] = jnp.concatenate([(q * QK_SCALE).astype(BF), k.astype(BF), v.astype(BF)], axis=-1)
        return
    nh = wc // LANES
    pos = (i * tm + lax.broadcasted_iota(jnp.int32, (tm, LANES), 0)).astype(F32)
    pos_hi = pos.astype(BF).astype(F32)
    lane = lax.broadcasted_iota(jnp.int32, (tm, LANES), 1)
    k_aug = jnp.where(lane == HEAD_DIM, pos_hi, jnp.where(lane == HEAD_DIM + 1, pos - pos_hi, 0.0))
    q_augs = []
    for hh in range(nh):
        q_augs += [_lane_range_row(HEAD_DIM, HEAD_DIM + 2, _alibi_slope(hh, nh))] * 2
    blocks = _head_blocks(q * QK_SCALE, q_augs) + _head_blocks(k, [k_aug] * (2 * nh))
    qk_ref[0] = jnp.concatenate(blocks, axis=-1)
    vt_ref[0] = jnp.concatenate([v.T.astype(BF), _ones_rows(tm)], axis=0)


def _alibi_slope(h, nh):
    assert 8 % nh == 0, "slopes must be exact powers of two to ride in bf16"
    return 2.0 ** (-8.0 * (h + 1) / nh)


def _inproj_odd_call(x, g, shift, scale, w, prompt):
    bx, sx, d = x.shape
    tm = min(ROW_TILE, sx)
    n = w.shape[1]
    wc = n // 3
    row = lambda b, i: (b, i, 0)
    f32_specs = [pl.BlockSpec((1, tm, wc), row), pl.BlockSpec((1, tm, wc), row)]
    f32_shapes = [jax.ShapeDtypeStruct((bx, sx, wc), F32), jax.ShapeDtypeStruct((bx, sx, wc), F32)]
    if prompt:
        nqk = 2 * wc // HEAD_DIM * LANES
        out_specs = [pl.BlockSpec((1, tm, nqk), row), pl.BlockSpec((1, wc + 16, tm), lambda b, i: (b, 0, i))]
        out_shape = [jax.ShapeDtypeStruct((bx, sx, nqk), BF), jax.ShapeDtypeStruct((bx, wc + 16, sx), BF)]
    else:
        out_specs = [pl.BlockSpec((1, tm, n), row)]
        out_shape = [jax.ShapeDtypeStruct((bx, sx, n), BF)]
    return pl.pallas_call(
        functools.partial(_inproj_odd_body, wc=wc, prompt=prompt),
        grid=(bx, sx // tm),
        in_specs=[pl.BlockSpec((1, tm, d), row), _const_spec(g), _mod_spec(shift, tm), _mod_spec(scale, tm),
                  _const_spec(w)],
        out_specs=out_specs + f32_specs, out_shape=out_shape + f32_shapes,
        compiler_params=_cparams(("arbitrary", "arbitrary")),
        name="inproj_odd",
    )(x, g, shift[0], scale[0], w)


def _tflash_update(st, v_aug, mask, extra, m_ref, acc_ref, idx):
    if extra is not None:
        st = st - extra
    if mask is not None:
        st = jnp.where(mask, st, NEG_INF)
    m_prev = m_ref[idx]
    m_new = jnp.maximum(m_prev, jnp.max(st, axis=0, keepdims=True))
    alpha = jnp.exp(m_prev - m_new)
    p = jnp.exp(st - m_new).astype(BF)
    acc_ref[idx] = alpha * acc_ref[idx] + _dot(v_aug, p)
    m_ref[idx] = m_new


def _tflash_init(m_ref, acc_ref):
    m_ref[...] = jnp.full(m_ref.shape, NEG_INF, F32)
    acc_ref[...] = jnp.zeros(acc_ref.shape, F32)


def _fox_prompt_body(q_ref, k_ref, vt_ref, o_ref, m_ref, acc_ref, st_ref, *, nh):
    i = pl.program_id(1)
    j = pl.program_id(2)
    hd = HEAD_DIM

    @pl.when(j == 0)
    def _():
        _tflash_init(m_ref, acc_ref)

    def step(masked):
        q = q_ref[0]
        k = k_ref[0]
        vt = vt_ref[0]
        tq, tk = q.shape[0], k.shape[0]
        ones = vt[nh * hd:, :]
        mask = None
        if masked:
            mask = lax.broadcasted_iota(jnp.int32, (tk, tq), 0) <= lax.broadcasted_iota(jnp.int32, (tk, tq), 1)
        def scores(h):
            st_ref[h % 2] = _dot_nt(k[:, h * LANES:(h + 1) * LANES], q[:, h * LANES:(h + 1) * LANES])

        scores(0)
        for h in range(nh):
            if h + 1 < nh:
                scores(h + 1)
            v_aug = jnp.concatenate([vt[h * hd:(h + 1) * hd, :], ones], axis=0)
            _tflash_update(st_ref[h % 2], v_aug, mask, None, m_ref, acc_ref, h)

    @pl.when(j < i)
    def _():
        step(False)

    @pl.when(j == i)
    def _():
        step(True)

    @pl.when(j == pl.num_programs(2) - 1)
    def _():
        ot = jnp.concatenate([acc_ref[h][:hd] / acc_ref[h][hd:hd + 1] for h in range(nh)], axis=0)
        o_ref[0] = ot.T.astype(BF)


def _fox_prompt_call(qk, vt, wa):
    b, s, _ = qk.shape
    nh = wa // HEAD_DIM
    t = min(ATTN_TILE, s)
    nq = s // t
    wq = nh * LANES
    return pl.pallas_call(
        functools.partial(_fox_prompt_body, nh=nh),
        grid=(b, nq, nq),
        in_specs=[pl.BlockSpec((1, t, wq), lambda b, i, j: (b, i, 0)),
                  pl.BlockSpec((1, t, wq), lambda b, i, j: (b, jnp.minimum(j, i), 1)),
                  pl.BlockSpec((1, wa + 16, t), lambda b, i, j: (b, 0, jnp.minimum(j, i)))],
        out_specs=pl.BlockSpec((1, t, wa), lambda b, i, j: (b, i, 0)),
        out_shape=jax.ShapeDtypeStruct((b, s, wa), BF),
        scratch_shapes=[pltpu.VMEM((nh, 1, t), F32), pltpu.VMEM((nh, HEAD_DIM + 16, t), F32),
                        pltpu.VMEM((2, t, t), F32)],
        compiler_params=_cparams(("arbitrary", "arbitrary", "arbitrary")),
        name="fox_prompt",
    )(qk, qk, vt)


def _band_prompt_body(q_ref, k0_ref, k1_ref, k2_ref, v0_ref, v1_ref, v2_ref, bias_ref, o_ref, st_ref, *, nh):
    i = pl.program_id(1)
    hd = HEAD_DIM
    q = q_ref[0]
    tq = q.shape[0]
    k = jnp.concatenate([k0_ref[0], k1_ref[0], k2_ref[0]], axis=0)
    vts = [v0_ref[0], v1_ref[0], v2_ref[0]]
    in_seq = lax.broadcasted_iota(jnp.int32, (3 * tq, tq), 0) >= (2 - i) * tq
    outs = []
    def scores(h):
        st_ref[h % 2] = _dot_nt(k[:, h * LANES:(h + 1) * LANES], q[:, h * LANES:(h + 1) * LANES])

    scores(0)
    for h in range(nh):
        if h + 1 < nh:
            scores(h + 1)
        st = st_ref[h % 2] + bias_ref[h]
        st = jnp.where(in_seq, st, NEG_INF)
        p = jnp.exp(st - jnp.max(st, axis=0, keepdims=True)).astype(BF)
        acc = None
        for w, vt in enumerate(vts):
            v_aug = jnp.concatenate([vt[h * hd:(h + 1) * hd, :], vt[nh * hd:, :]], axis=0)
            part = _dot(v_aug, p[w * tq:(w + 1) * tq, :])
            acc = part if acc is None else acc + part
        outs.append(acc[:hd] / acc[hd:hd + 1])
    o_ref[0] = jnp.concatenate(outs, axis=0).T.astype(BF)


def _band_bias_prompt(rel_bias, tq):
    nh = rel_bias.shape[0]
    period = 4 * tq
    t = jnp.arange(period)
    t = jnp.where(t < 3 * tq, t, t - period)
    u = rel_bias[:, jnp.clip(2 * tq - t, -MAX_REL, MAX_REL) + MAX_REL].astype(F32)
    skew = jnp.tile(u, (1, tq))[:, :tq * (period - 1)].reshape(nh, tq, period - 1)
    bias = skew[:, :, :3 * tq]
    r = jnp.arange(tq)[:, None]
    c = jnp.arange(3 * tq)[None, :]
    valid = (c // CHUNK >= r // CHUNK) & (c // CHUNK <= r // CHUNK + BAND_CHUNKS)
    return jnp.where(valid[None], bias, NEG_INF)


def _band_prompt_call(qk, vt, rel_bias, wa, wb):
    b, s, _ = qk.shape
    nh = wb // HEAD_DIM
    tq = BAND_Q_TILE
    assert s % tq == 0 and wa == wb
    bias_t = _band_bias_prompt(rel_bias, tq).transpose(0, 2, 1)
    wq = nh * LANES
    cq, ck = 2, 3

    def k_spec(back):
        return pl.BlockSpec((1, tq, wq), lambda b, i: (b, jnp.maximum(i - back, 0), ck))

    def v_spec(back):
        return pl.BlockSpec((1, wb + 16, tq), lambda b, i: (b, 1, jnp.maximum(i - back, 0)))

    return pl.pallas_call(
        functools.partial(_band_prompt_body, nh=nh),
        grid=(b, s // tq),
        in_specs=[pl.BlockSpec((1, tq, wq), lambda b, i: (b, i, cq)),
                  k_spec(2), k_spec(1), k_spec(0), v_spec(2), v_spec(1), v_spec(0),
                  _const_spec(bias_t)],
        out_specs=pl.BlockSpec((1, tq, wb), lambda b, i: (b, i, 0)),
        out_shape=jax.ShapeDtypeStruct((b, s, wb), BF),
        scratch_shapes=[pltpu.VMEM((2, 3 * tq, tq), F32)],
        compiler_params=_cparams(("arbitrary", "arbitrary")),
        name="band_prompt",
    )(qk, qk, qk, qk, vt, vt, vt, bias_t)


def _lambda(lam_ref, lam_init):
    lp = lam_ref[...]
    a = jnp.sum(lp[0:1] * lp[1:2], axis=-1, keepdims=True)
    b = jnp.sum(lp[2:3] * lp[3:4], axis=-1, keepdims=True)
    return jnp.exp(a) - jnp.exp(b) + lam_init


def _diff_prompt_body(q_ref, k_ref, vt_ref, lam_ref, gcol_ref, o_ref, m_ref, acc_ref, st_ref, *, nh, lam_init):
    i = pl.program_id(1)
    j = pl.program_id(2)
    dv = 2 * HEAD_DIM

    @pl.when(j == 0)
    def _():
        _tflash_init(m_ref, acc_ref)

    def step(masked):
        q = q_ref[0]
        k = k_ref[0]
        vt = vt_ref[0]
        tq, tk = q.shape[0], k.shape[0]
        ones = vt[nh * dv:, :]
        mask = None
        if masked:
            key = lax.broadcasted_iota(jnp.int32, (tk, tq), 0)
            qry = lax.broadcasted_iota(jnp.int32, (tk, tq), 1)
            mask = (key >> CHUNK_SHIFT) <= (qry >> CHUNK_SHIFT)
            ahead = jnp.maximum(key - qry, 0).astype(F32)
        def scores(n):
            st_ref[n % 2] = _dot_nt(k[:, n * LANES:(n + 1) * LANES], q[:, n * LANES:(n + 1) * LANES])

        scores(0)
        for h in range(nh):
            v_aug = jnp.concatenate([vt[h * dv:(h + 1) * dv, :], ones], axis=0)
            extra = (2.0 * _alibi_slope(h, nh)) * ahead if masked else None
            for u in range(2):
                n = 2 * h + u
                if n + 1 < 2 * nh:
                    scores(n + 1)
                _tflash_update(st_ref[n % 2], v_aug, mask, extra, m_ref, acc_ref, n)

    @pl.when(j < i)
    def _():
        step(False)

    @pl.when(j == i)
    def _():
        step(True)

    @pl.when(j == pl.num_programs(2) - 1)
    def _():
        lam = _lambda(lam_ref, lam_init)
        g = gcol_ref[...]
        outs = []
        for h in range(nh):
            a1 = acc_ref[2 * h]
            a2 = acc_ref[2 * h + 1]
            o = a1[:dv] / a1[dv:dv + 1] - lam * (a2[:dv] / a2[dv:dv + 1])
            o = o * lax.rsqrt(jnp.mean(o * o, axis=0, keepdims=True) + RMS_EPS)
            outs.append((o * g) * (1.0 - lam_init))
        o_ref[0] = jnp.concatenate(outs, axis=0).T.astype(BF)


def _diff_prompt_call(qk, vt, lam_par, g_col, wc, lam_init):
    b, s, _ = qk.shape
    nh = wc // (2 * HEAD_DIM)
    t = min(ATTN_TILE, s)
    nq = s // t
    wq = 2 * nh * LANES
    return pl.pallas_call(
        functools.partial(_diff_prompt_body, nh=nh, lam_init=lam_init),
        grid=(b, nq, nq),
        in_specs=[pl.BlockSpec((1, t, wq), lambda b, i, j: (b, i, 0)),
                  pl.BlockSpec((1, t, wq), lambda b, i, j: (b, jnp.minimum(j, i), 1)),
                  pl.BlockSpec((1, wc + 16, t), lambda b, i, j: (b, 0, jnp.minimum(j, i))),
                  pl.BlockSpec(lam_par.shape, lambda b, i, j: (0, 0)),
                  pl.BlockSpec(g_col.shape, lambda b, i, j: (0, 0))],
        out_specs=pl.BlockSpec((1, t, wc), lambda b, i, j: (b, i, 0)),
        out_shape=jax.ShapeDtypeStruct((b, s, wc), BF),
        scratch_shapes=[pltpu.VMEM((2 * nh, 1, t), F32), pltpu.VMEM((2 * nh, 2 * HEAD_DIM + 16, t), F32),
                        pltpu.VMEM((2, t, t), F32)],
        compiler_params=_cparams(("arbitrary", "arbitrary", "arbitrary")),
        name="diff_prompt",
    )(qk, qk, vt, lam_par, g_col)


def _cumsum_body(x_ref, up_ref, o_ref, carry_ref):
    @pl.when(pl.program_id(1) == 0)
    def _():
        carry_ref[...] = jnp.zeros(carry_ref.shape, F32)
    ft = _cumsum_lanes(x_ref[0], up_ref[...]) + carry_ref[:, 0:1]
    o_ref[0] = ft
    carry_ref[...] = jnp.broadcast_to(ft[:, -1:], carry_ref.shape)


def _cumsum_call(x):
    b, nh, length = x.shape
    t = 512
    up = _upper_tri(t)
    return pl.pallas_call(
        _cumsum_body,
        grid=(b, length // t),
        in_specs=[pl.BlockSpec((1, nh, t), lambda b, j: (b, 0, j)), pl.BlockSpec((t, t), lambda b, j: (0, 0))],
        out_specs=pl.BlockSpec((1, nh, t), lambda b, j: (b, 0, j)),
        out_shape=jax.ShapeDtypeStruct(x.shape, F32),
        scratch_shapes=[pltpu.VMEM((nh, LANES), F32)],
        compiler_params=_cparams(("arbitrary", "arbitrary")),
        name="logf_cumsum",
    )(x, up)


def _fox_sample_body(q_ref, kn_ref, vn_ref, ck_ref, cv_ref, ftc_ref, ftn_ref, o_ref, m_ref, l_ref, acc_ref, *, nh):
    j = pl.program_id(1)
    hd = HEAD_DIM
    q = q_ref[0]
    t = q.shape[0]

    @pl.when(j == 0)
    def _():
        _osm_init(m_ref, l_ref, acc_ref)

    k = ck_ref[0].astype(BF)
    v = cv_ref[0].astype(BF)
    ft = ftc_ref[0]
    for h in range(nh):
        sl = slice(h * hd, (h + 1) * hd)
        s = _dot_nt(q[:, sl], k[:, sl]) - ft[h:h + 1, :]
        _osm_update(s, v[:, sl], m_ref, l_ref, acc_ref, h)

    @pl.when(j == pl.num_programs(1) - 1)
    def _():
        kn = kn_ref[0]
        vn = vn_ref[0]
        ftn = ftn_ref[0][:, :t]
        visible = lax.broadcasted_iota(jnp.int32, (t, t), 1) <= lax.broadcasted_iota(jnp.int32, (t, t), 0)
        outs = []
        for h in range(nh):
            sl = slice(h * hd, (h + 1) * hd)
            s = _dot_nt(q[:, sl], kn[:, sl]) - ftn[h:h + 1, :]
            s = jnp.where(visible, s, NEG_INF)
            _osm_update(s, vn[:, sl], m_ref, l_ref, acc_ref, h)
            outs.append((acc_ref[h] / l_ref[h]).astype(BF))
        o_ref[0] = jnp.concatenate(outs, axis=-1)


def _fox_sample_call(qkv, cache_k, cache_v, ft_pad, wa):
    b, t, _ = qkv.shape
    p_len = cache_k.shape[1]
    nh = wa // HEAD_DIM
    tk = min(CACHE_TILE_A, p_len)
    assert p_len % tk == 0 and p_len % LANES == 0 and t <= LANES
    return pl.pallas_call(
        functools.partial(_fox_sample_body, nh=nh),
        grid=(b, p_len // tk),
        in_specs=[pl.BlockSpec((1, t, wa), lambda b, j: (b, 0, 0)),
                  pl.BlockSpec((1, t, wa), lambda b, j: (b, 0, 1)),
                  pl.BlockSpec((1, t, wa), lambda b, j: (b, 0, 2)),
                  pl.BlockSpec((1, tk, wa), lambda b, j: (b, j, 0)),
                  pl.BlockSpec((1, tk, wa), lambda b, j: (b, j, 0)),
                  pl.BlockSpec((1, nh, tk), lambda b, j: (b, 0, j)),
                  pl.BlockSpec((1, nh, LANES), lambda b, j: (b, 0, p_len // LANES))],
        out_specs=pl.BlockSpec((1, t, wa), lambda b, j: (b, 0, 0)),
        out_shape=jax.ShapeDtypeStruct((b, t, wa), BF),
        scratch_shapes=[pltpu.VMEM((nh, t, 1), F32), pltpu.VMEM((nh, t, 1), F32),
                        pltpu.VMEM((nh, t, HEAD_DIM), F32)],
        compiler_params=_cparams(("arbitrary", "arbitrary")),
        name="fox_sample",
    )(qkv, qkv, qkv, cache_k, cache_v, ft_pad, ft_pad)


def _band_sample_body(q_ref, kn_ref, vn_ref, bk_ref, bv_ref, bias_b_ref, bias_n_ref, o_ref, *, nh):
    hd = HEAD_DIM
    q = q_ref[0]
    kn = kn_ref[0]
    vn = vn_ref[0]
    kb = bk_ref[0].astype(BF)
    vb = bv_ref[0].astype(BF)
    outs = []
    for h in range(nh):
        sl = slice(h * hd, (h + 1) * hd)
        sb = _dot_nt(q[:, sl], kb[:, sl]) + bias_b_ref[h]
        sn = _dot_nt(q[:, sl], kn[:, sl]) + bias_n_ref[h]
        m = jnp.maximum(jnp.max(sb, axis=-1, keepdims=True), jnp.max(sn, axis=-1, keepdims=True))
        pb = jnp.exp(sb - m)
        pn = jnp.exp(sn - m)
        l = jnp.sum(pb, axis=-1, keepdims=True) + jnp.sum(pn, axis=-1, keepdims=True)
        o = _dot(pb.astype(BF), vb[:, sl]) + _dot(pn.astype(BF), vn[:, sl])
        outs.append((o / l).astype(BF))
    o_ref[0] = jnp.concatenate(outs, axis=-1)


def _band_sample_call(qkv, buf_k, buf_v, rel_bias, wb):
    b, t, _ = qkv.shape
    lb = buf_k.shape[1]
    nh = wb // HEAD_DIM
    k_pos = jnp.concatenate([jnp.arange(-lb, 0), jnp.arange(t)])
    rel = jnp.arange(t)[:, None] - k_pos[None, :]
    bias = rel_bias[:, jnp.clip(rel, -MAX_REL, MAX_REL) + MAX_REL].astype(F32)
    bias_b, bias_n = bias[:, :, :lb], bias[:, :, lb:]
    return pl.pallas_call(
        functools.partial(_band_sample_body, nh=nh),
        grid=(b,),
        in_specs=[pl.BlockSpec((1, t, wb), lambda b: (b, 0, 3)),
                  pl.BlockSpec((1, t, wb), lambda b: (b, 0, 4)),
                  pl.BlockSpec((1, t, wb), lambda b: (b, 0, 5)),
                  pl.BlockSpec((1, lb, wb), lambda b: (b, 0, 0)),
                  pl.BlockSpec((1, lb, wb), lambda b: (b, 0, 0)),
                  pl.BlockSpec(bias_b.shape, lambda b: (0, 0, 0)),
                  pl.BlockSpec(bias_n.shape, lambda b: (0, 0, 0))],
        out_specs=pl.BlockSpec((1, t, wb), lambda b: (b, 0, 0)),
        out_shape=jax.ShapeDtypeStruct((b, t, wb), BF),
        compiler_params=_cparams(("arbitrary",)),
        name="band_sample",
    )(qkv, qkv, qkv, buf_k, buf_v, bias_b, bias_n)


def _diff_sample_body(q_ref, kn_ref, vn_ref, ck_ref, cv_ref, lam_ref, gsub_ref, o_ref, m_ref, l_ref, acc_ref,
                      *, nh, lam_init, p_len):
    j = pl.program_id(1)
    q = q_ref[0]
    t = q.shape[0]
    dv = 2 * HEAD_DIM
    lane = lax.broadcasted_iota(jnp.int32, (t, dv), 1)

    @pl.when(j == 0)
    def _():
        _osm_init(m_ref, l_ref, acc_ref)

    def attend(k_of, v_of, dist):
        dist2 = jnp.concatenate([dist, dist], axis=0)
        for h in range(nh):
            x = q[:, h * dv:(h + 1) * dv]
            zero = jnp.zeros_like(x)
            q2 = jnp.concatenate([jnp.where(lane < HEAD_DIM, x, zero), jnp.where(lane >= HEAD_DIM, x, zero)], axis=0)
            s = _dot_nt(q2, k_of(h)) - _alibi_slope(h, nh) * dist2
            _osm_update(s, v_of(h), m_ref, l_ref, acc_ref, h)

    tk = ck_ref.shape[1] // nh
    row = lax.broadcasted_iota(jnp.int32, (t, tk), 0)
    col = lax.broadcasted_iota(jnp.int32, (t, tk), 1)
    attend(lambda h: ck_ref[0, pl.ds(h, tk, stride=nh), :].astype(BF),
           lambda h: cv_ref[0, pl.ds(h, tk, stride=nh), :].astype(BF),
           (p_len + row - j * tk - col).astype(F32))

    @pl.when(j == pl.num_programs(1) - 1)
    def _():
        kn = kn_ref[0]
        vn = vn_ref[0]
        rown = lax.broadcasted_iota(jnp.int32, (t, t), 0)
        coln = lax.broadcasted_iota(jnp.int32, (t, t), 1)
        attend(lambda h: kn[:, h * dv:(h + 1) * dv], lambda h: vn[:, h * dv:(h + 1) * dv],
               jnp.abs(rown - coln).astype(F32))
        lam = _lambda(lam_ref, lam_init)
        g = gsub_ref[...]
        outs = []
        for h in range(nh):
            on = acc_ref[h] / l_ref[h]
            o = on[:t] - lam * on[t:]
            o = o * lax.rsqrt(jnp.mean(o * o, axis=-1, keepdims=True) + RMS_EPS)
            outs.append(((o * g) * (1.0 - lam_init)).astype(BF))
        o_ref[0] = jnp.concatenate(outs, axis=-1)


def _diff_sample_call(qkv, cache_k, cache_v, lam_par, g_sub, wc, lam_init):
    b, t, _ = qkv.shape
    nh = wc // (2 * HEAD_DIM)
    p_len = cache_k.shape[1] // nh
    assert (p_len // CHUNK) * CHUNK == p_len and t <= CHUNK
    tk = min(CACHE_TILE_C, p_len)
    return pl.pallas_call(
        functools.partial(_diff_sample_body, nh=nh, lam_init=lam_init, p_len=p_len),
        grid=(b, p_len // tk),
        in_specs=[pl.BlockSpec((1, t, wc), lambda b, j: (b, 0, 0)),
                  pl.BlockSpec((1, t, wc), lambda b, j: (b, 0, 1)),
                  pl.BlockSpec((1, t, wc), lambda b, j: (b, 0, 2)),
                  pl.BlockSpec((1, tk * nh, 2 * HEAD_DIM), lambda b, j: (b, j, 0)),
                  pl.BlockSpec((1, tk * nh, 2 * HEAD_DIM), lambda b, j: (b, j, 0)),
                  pl.BlockSpec(lam_par.shape, lambda b, j: (0, 0)),
                  pl.BlockSpec(g_sub.shape, lambda b, j: (0, 0))],
        out_specs=pl.BlockSpec((1, t, wc), lambda b, j: (b, 0, 0)),
        out_shape=jax.ShapeDtypeStruct((b, t, wc), BF),
        scratch_shapes=[pltpu.VMEM((nh, 2 * t, 1), F32), pltpu.VMEM((nh, 2 * t, 1), F32),
                        pltpu.VMEM((nh, 2 * t, 2 * HEAD_DIM), F32)],
        compiler_params=_cparams(("arbitrary", "arbitrary")),
        name="diff_sample",
    )(qkv, qkv, qkv, cache_k, cache_v, lam_par, g_sub)


def _out_ffn_body(x_ref, oa_ref, ob_ref, gm_ref, g_ref, sh_ref, sc_ref, gf_ref, wo_ref, w1_ref, w3_ref, w2_ref, y_ref,
                  *, f_chunk):
    wa = oa_ref.shape[-1]
    attn = _dot(oa_ref[0], wo_ref[:wa, :]) + _dot(ob_ref[0], wo_ref[wa:, :])
    x1 = x_ref[0] + gm_ref[0, 0, 0] * attn
    h = _modulate(x1, g_ref[...], sh_ref[0, 0, 0], sc_ref[0, 0, 0]).astype(BF)
    f_total = w1_ref.shape[1]
    f = None
    for c0 in range(0, f_total, f_chunk):
        a = _dot(h, w1_ref[:, c0:c0 + f_chunk])
        g = _dot(h, w3_ref[:, c0:c0 + f_chunk])
        u = ((a * _sigmoid(a)) * g).astype(BF)
        part = _dot(u, w2_ref[c0:c0 + f_chunk, :])
        f = part if f is None else f + part
    y_ref[0] = x1 + gf_ref[0, 0, 0] * f


def _out_ffn_call(x, o_a, o_b, gate_m, g, shift, scale, gate_f, wo, w1, w3, w2):
    bx, sx, d = x.shape
    tm = min(ROW_TILE, sx)
    f_total = w1.shape[1]
    f_chunk = f_total // 2 if (f_total // 2) % LANES == 0 else f_total
    row = lambda b, i: (b, i, 0)
    return pl.pallas_call(
        functools.partial(_out_ffn_body, f_chunk=f_chunk),
        grid=(bx, sx // tm),
        in_specs=[pl.BlockSpec((1, tm, d), row),
                  pl.BlockSpec((1, tm, o_a.shape[-1]), row), pl.BlockSpec((1, tm, o_b.shape[-1]), row),
                  _mod_spec(gate_m, tm), _const_spec(g), _mod_spec(shift, tm), _mod_spec(scale, tm),
                  _mod_spec(gate_f, tm), _const_spec(wo), _const_spec(w1), _const_spec(w3), _const_spec(w2)],
        out_specs=pl.BlockSpec((1, tm, d), row),
        out_shape=jax.ShapeDtypeStruct((bx, sx, d), F32),
        compiler_params=_cparams(("arbitrary", "arbitrary")),
        name="out_ffn",
    )(x, o_a, o_b, gate_m[0], g, shift[0], scale[0], gate_f[0], wo, w1, w3, w2)


def _store_tile_rows(ref, x):
    rows, d = x.shape
    nseg = d // LANES
    for s in range(nseg):
        ref[pl.ds(s, rows, stride=nseg), :] = x[:, s * LANES:(s + 1) * LANES]


def _load_tile_rows(ref, rows, d):
    nseg = d // LANES
    return jnp.concatenate([ref[pl.ds(s, rows, stride=nseg), :] for s in range(nseg)], axis=-1)


def _out_router_body(x_ref, o_ref, gm_ref, g_ref, sh_ref, sc_ref, wo_ref, wr_ref, h_in_ref,
                     x1_ref, h_ref, route_ref, *, n_exp):
    del h_in_ref
    x1 = x_ref[0] + gm_ref[0, 0, 0] * _dot(o_ref[0], wo_ref[...])
    x1_ref[0] = x1
    h = _modulate(x1, g_ref[...], sh_ref[0, 0, 0], sc_ref[0, 0, 0])
    hb = h.astype(BF)
    _store_tile_rows(h_ref, hb.astype(F32))
    logits = _dot(hb, wr_ref[0]) + _dot(hb, wr_ref[1]) + _dot(hb, wr_ref[2])
    lt = logits.T[:ROUTE_ROWS, :]
    row = lax.broadcasted_iota(jnp.int32, lt.shape, 0)
    lg = jnp.where(row < n_exp, lt, NEG_INF)
    e = jnp.exp(lg - jnp.max(lg, axis=0, keepdims=True))
    probs = e / jnp.sum(e, axis=0, keepdims=True)
    p1 = jnp.max(probs, axis=0, keepdims=True)
    i1 = jnp.min(jnp.where(probs == p1, row, ROUTE_ROWS), axis=0, keepdims=True)
    rest = jnp.where(row == i1, -1.0, probs)
    p2 = jnp.max(rest, axis=0, keepdims=True)
    i2 = jnp.min(jnp.where(rest == p2, row, ROUTE_ROWS), axis=0, keepdims=True)
    tot = p1 + p2
    route_ref[0] = jnp.where(row == 0, i1.astype(F32),
                             jnp.where(row == 1, i2.astype(F32),
                                       jnp.where(row == 2, p1 / tot, jnp.where(row == 3, p2 / tot, 0.0))))


def _out_router_call(x, o, gate_m, g, shift, scale, wo, wr3, n_exp, tok_offset, h_all):
    bx, sx, d = x.shape
    tm = min(ROW_TILE, sx)
    nt = sx // tm
    nseg = d // LANES
    assert tok_offset % tm == 0 and n_exp <= ROUTE_ROWS
    row = lambda b, i: (b, i, 0)
    return pl.pallas_call(
        functools.partial(_out_router_body, n_exp=n_exp),
        grid=(bx, nt),
        in_specs=[pl.BlockSpec((1, tm, d), row), pl.BlockSpec((1, tm, o.shape[-1]), row),
                  _mod_spec(gate_m, tm), _const_spec(g), _mod_spec(shift, tm), _mod_spec(scale, tm),
                  _const_spec(wo), _const_spec(wr3), pl.BlockSpec(memory_space=pl.ANY)],
        out_specs=[pl.BlockSpec((1, tm, d), row),
                   pl.BlockSpec((tm * nseg, LANES), lambda b, i: (tok_offset // tm + b * nt + i, 0)),
                   pl.BlockSpec((1, ROUTE_ROWS, tm), lambda b, i: (b, 0, i))],
        out_shape=[jax.ShapeDtypeStruct((bx, sx, d), F32),
                   jax.ShapeDtypeStruct(h_all.shape, F32),
                   jax.ShapeDtypeStruct((bx, ROUTE_ROWS, sx), F32)],
        input_output_aliases={8: 1},
        compiler_params=_cparams(("arbitrary", "arbitrary")),
        name="out_router",
    )(x, o, gate_m[0], g, shift[0], scale[0], wo, wr3, h_all)


def _moe_ffn_body(te_ref, nv_ref, idx_ref, idx_next_ref, h_ref, w1_ref, w3_ref, w2_ref, y_ref, xbuf, xb16, acc, sem,
                  *, nseg, nf):
    t = pl.program_id(0)
    j = pl.program_id(1)
    nv = nv_ref[0]
    tm, d = xb16.shape
    slot = lax.rem(t, 2)

    def issue(rows_ref, s):
        def body(r, c):
            src = pl.multiple_of(rows_ref[0, 0, r], nseg)
            dst = pl.multiple_of(r * nseg, nseg)
            pltpu.make_async_copy(h_ref.at[pl.ds(src, nseg)], xbuf.at[s, pl.ds(dst, nseg)], sem.at[s]).start()
            return c
        lax.fori_loop(0, tm, body, 0, unroll=8)

    @pl.when((j == 0) & (t == 0))
    def _():
        issue(idx_ref, 0)

    @pl.when((j == 0) & (t + 1 < nv))
    def _():
        issue(idx_next_ref, 1 - slot)

    @pl.when((j == 0) & (t < nv))
    def _():
        pltpu.make_async_copy(h_ref.at[pl.ds(0, tm * nseg)], xbuf.at[slot], sem.at[slot]).wait()
        xb16[...] = _load_tile_rows(xbuf.at[slot], tm, d).astype(BF)

    @pl.when(t < nv)
    def _():
        xb = xb16[...]
        a = _dot(xb, w1_ref[0])
        g = _dot(xb, w3_ref[0])
        u = ((a * _sigmoid(a)) * g).astype(BF)
        part = _dot(u, w2_ref[0])

        if nf == 1:
            _store_tile_rows(y_ref, part)
        else:
            @pl.when(j == 0)
            def _():
                acc[...] = part

            @pl.when((j > 0) & (j < nf - 1))
            def _():
                acc[...] += part

            @pl.when(j == nf - 1)
            def _():
                _store_tile_rows(y_ref, acc[...] + part)

    @pl.when((t >= nv) & (j == nf - 1))
    def _():
        y_ref[...] = jnp.zeros(y_ref.shape, F32)


def _moe_ffn_call(h_all, row_tok, tile_e, n_valid, w1, w3, w2):
    n_rows = row_tok.shape[0]
    d = w1.shape[1]
    nseg = d // LANES
    f_total = w1.shape[2]
    tf = MOE_F_TILE if f_total % MOE_F_TILE == 0 else f_total
    nf = f_total // tf
    tm = MOE_ROW_TILE
    n_tiles = n_rows // tm
    idx3 = (row_tok * nseg).reshape(n_tiles, 1, tm)

    def fcol(t, j, te, nv):
        return jnp.where(t < nv[0], j, nf - 1)

    grid_spec = pltpu.PrefetchScalarGridSpec(
        num_scalar_prefetch=2,
        grid=(n_tiles, nf),
        in_specs=[pl.BlockSpec((1, 1, tm), lambda t, j, te, nv: (t, 0, 0), memory_space=pltpu.SMEM),
                  pl.BlockSpec((1, 1, tm), lambda t, j, te, nv: (jnp.minimum(t + 1, n_tiles - 1), 0, 0),
                               memory_space=pltpu.SMEM),
                  pl.BlockSpec(memory_space=pl.ANY),
                  pl.BlockSpec((1, d, tf), lambda t, j, te, nv: (te[t], 0, fcol(t, j, te, nv))),
                  pl.BlockSpec((1, d, tf), lambda t, j, te, nv: (te[t], 0, fcol(t, j, te, nv))),
                  pl.BlockSpec((1, tf, d), lambda t, j, te, nv: (te[t], fcol(t, j, te, nv), 0))],
        out_specs=pl.BlockSpec((tm * nseg, LANES), lambda t, j, te, nv: (t, 0)),
        scratch_shapes=[pltpu.VMEM((2, tm * nseg, LANES), F32), pltpu.VMEM((tm, d), BF), pltpu.VMEM((tm, d), F32),
                        pltpu.SemaphoreType.DMA((2,))],
    )
    return pl.pallas_call(
        functools.partial(_moe_ffn_body, nseg=nseg, nf=nf),
        grid_spec=grid_spec,
        out_shape=jax.ShapeDtypeStruct((n_rows * nseg, LANES), F32),
        compiler_params=_cparams(("arbitrary", "arbitrary")),
        name="moe_ffn",
    )(tile_e, n_valid, idx3, idx3, h_all, w1, w3, w2)


def _combine_body(d0_ref, d1_ref, x_ref, gf_ref, w_ref, g_ref, yb_ref, o_ref, buf, sem):
    t = pl.program_id(0)
    nt = pl.num_programs(0)
    _, tc, d = x_ref.shape
    nseg = d // LANES

    def row_copy(idx_ref, r, slot, k):
        src = pl.multiple_of(idx_ref[0, 0, 2 * r + k], nseg)
        dst = pl.multiple_of(r * nseg, nseg)
        return pltpu.make_async_copy(yb_ref.at[pl.ds(src, nseg)], buf.at[slot, k, pl.ds(dst, nseg)], sem.at[slot])

    def issue(idx_ref, slot):
        def body(r, c):
            row_copy(idx_ref, r, slot, 0).start()
            row_copy(idx_ref, r, slot, 1).start()
            return c
        lax.fori_loop(0, tc, body, 0, unroll=4)

    slot = lax.rem(t, 2)

    @pl.when(t == 0)
    def _():
        issue(d0_ref, 0)

    @pl.when(t + 1 < nt)
    def _():
        issue(d1_ref, 1 - slot)

    for k in range(TOP_K):
        pltpu.make_async_copy(yb_ref.at[pl.ds(0, tc * nseg)], buf.at[slot, k], sem.at[slot]).wait()
    w = jnp.concatenate([w_ref[0], jnp.zeros((LANES - ROUTE_ROWS, tc), F32)], axis=0).T
    f = (w[:, 2:3] * _load_tile_rows(buf.at[slot, 0], tc, d)
         + w[:, 3:4] * _load_tile_rows(buf.at[slot, 1], tc, d))
    y = x_ref[0] + gf_ref[0, 0, 0] * f
    y = y * lax.rsqrt(jnp.mean(y * y, axis=-1, keepdims=True) + RMS_EPS)
    o_ref[0] = y * g_ref[...]


def _combine_call(x1, gate_f, route, g_final, yb, dest):
    bx, sx, d = x1.shape
    nseg = d // LANES
    tc = min(COMBINE_TILE, sx)
    nps = sx // tc
    nt = bx * nps
    dest3 = (dest * nseg).reshape(nt, 1, 2 * tc)
    row = lambda t: (t // nps, t % nps, 0)
    gate_arr, gl, gk = gate_f
    if gate_arr.shape[3] == 1:
        gate_spec = pl.BlockSpec((1, 1, 1, 1, d), lambda t: (gl, gk, t // nps, 0, 0))
    else:
        gate_spec = pl.BlockSpec((1, 1, 1, tc, d), lambda t: (gl, gk, t // nps, t % nps, 0))
    return pl.pallas_call(
        _combine_body,
        grid=(nt,),
        in_specs=[pl.BlockSpec((1, 1, 2 * tc), lambda t: (t, 0, 0), memory_space=pltpu.SMEM),
                  pl.BlockSpec((1, 1, 2 * tc), lambda t: (jnp.minimum(t + 1, nt - 1), 0, 0),
                               memory_space=pltpu.SMEM),
                  pl.BlockSpec((1, tc, d), row), gate_spec,
                  pl.BlockSpec((1, ROUTE_ROWS, tc), lambda t: (t // nps, 0, t % nps)),
                  pl.BlockSpec(g_final.shape, lambda t: (0, 0)),
                  pl.BlockSpec(memory_space=pl.ANY)],
        out_specs=pl.BlockSpec((1, tc, d), row),
        out_shape=jax.ShapeDtypeStruct((bx, sx, d), F32),
        scratch_shapes=[pltpu.VMEM((2, 2, tc * nseg, LANES), F32), pltpu.SemaphoreType.DMA((2,))],
        compiler_params=_cparams(("arbitrary",)),
        name="moe_combine",
    )(dest3, dest3, x1, gate_arr, route, g_final, yb)


def _route_plan(slot_e, n_exp, tm):
    n_slots = slot_e.shape[0]
    onehot = (slot_e[:, None] == jnp.arange(n_exp, dtype=jnp.int32)[None, :]).astype(jnp.int32)
    csum = jnp.cumsum(onehot, axis=0)
    rank = jnp.sum(csum * onehot, axis=1) - 1
    counts = csum[-1]
    padded = ((counts + tm - 1) // tm) * tm
    pad_end = jnp.cumsum(padded)
    pad_start = pad_end - padded
    dest = jnp.sum(onehot * pad_start[None, :], axis=1) + rank
    n_tiles = -(-n_slots // tm) + n_exp
    n_rows = n_tiles * tm
    n_valid = (pad_end[-1] // tm).astype(jnp.int32)
    tile_start = jnp.arange(n_tiles, dtype=jnp.int32) * tm
    tile_e = jnp.minimum(jnp.searchsorted(pad_end, tile_start, side='right'), n_exp - 1).astype(jnp.int32)
    last_e = tile_e[jnp.maximum(n_valid - 1, 0)]
    tile_e = jnp.where(jnp.arange(n_tiles) < n_valid, tile_e, last_e)
    sorted_tok = (jnp.argsort(slot_e, stable=True) // TOP_K).astype(jnp.int32)
    sorted_tok = jnp.concatenate([sorted_tok, jnp.zeros((tm,), jnp.int32)])
    first = jnp.cumsum(counts) - counts - pad_start
    tile_first = jnp.minimum(tile_start + first[tile_e], n_slots)
    row_tok = sorted_tok[(tile_first[:, None] + jnp.arange(tm, dtype=jnp.int32)[None, :]).reshape(n_rows)]
    return dest, row_tok, tile_e, n_valid.reshape(1)


def _moe(parts, h_all, g_final, w1, w3, w2):
    n_exp = w1.shape[0]
    slot_e = jnp.concatenate([r[:, :TOP_K, :].transpose(0, 2, 1).reshape(-1) for _, r, _ in parts], axis=0)
    dest, row_tok, tile_e, n_valid = _route_plan(slot_e.astype(jnp.int32), n_exp, MOE_ROW_TILE)
    yb = _moe_ffn_call(h_all, row_tok, tile_e, n_valid, w1, w3, w2)
    outs = []
    first = 0
    for x1, route, gate_f in parts:
        n_slots = x1.shape[0] * x1.shape[1] * TOP_K
        outs.append(_combine_call(x1, gate_f, route, g_final, yb, dest[first:first + n_slots]))
        first += n_slots
    return outs


def _trunk(x, mods, p, cache, tok_offset, h_all):
    sh_m0, sc_m0, gt_m0, sh_f0, sc_f0, gt_f0 = mods[0]
    sh_m1, sc_m1, gt_m1, sh_f1, sc_f1, gt_f1 = mods[1]
    wa = p['wa']
    wb = p['wb']
    wc = p['wc']
    bx, sx, d = x.shape

    if cache is None:
        keep = min(WINDOW_B, sx)
        qk, vt, ka, va, kb, vb, logf = _inproj_even_call(
            x, p['g_mix0'], sh_m0, sc_m0, p['w_in_even'], p['w_f'], p['b_f'], keep, True)
        o_a = _fox_prompt_call(qk, vt, wa)
        o_b = _band_prompt_call(qk, vt, p['rel_bias'], wa, wb)
        x_l0_state = (ka, va, logf, kb, vb)
    else:
        ck, cv, clogf, bk, bv, cck, ccv, b, t = cache
        qkv, ka, va, kb, vb, logf, lft, = _inproj_even_call(
            x, p['g_mix0'], sh_m0, sc_m0, p['w_in_even'], p['w_f'], p['b_f'], sx, False)
        nh = wa // HEAD_DIM
        p_len = ck.shape[1]
        lft_new = lft.reshape(nh, b, t).transpose(1, 0, 2)
        lp = -(-(p_len + t) // 512) * 512
        lcat = jnp.concatenate([clogf.transpose(0, 2, 1), lft_new,
                                jnp.zeros((b, nh, lp - p_len - t), F32)], axis=-1)
        ft_pad = _cumsum_call(lcat)
        qkv_b = qkv.reshape(b, t, qkv.shape[-1])
        o_a = _fox_sample_call(qkv_b, ck, cv, ft_pad, wa).reshape(1, b * t, wa)
        o_b = _band_sample_call(qkv_b, bk, bv, p['rel_bias'], wb).reshape(1, b * t, wb)
        x_l0_state = (ka, va, logf, kb, vb)
    x = _out_ffn_call(x, o_a, o_b, gt_m0, p['g_ffn0'], sh_f0, sc_f0, gt_f0,
                      p['w_out_even'], p['w1_dense'], p['w3_dense'], p['w2_dense'])

    if cache is None:
        qk, vt, kc, vc = _inproj_odd_call(x, p['g_mix1'], sh_m1, sc_m1, p['w_in_odd'], True)
        o = _diff_prompt_call(qk, vt, p['lam_par'], p['g_sub'].reshape(-1, 1), wc, p['lam_init'])
    else:
        qkv, kc, vc = _inproj_odd_call(x, p['g_mix1'], sh_m1, sc_m1, p['w_in_odd'], False)
        qkv_b = qkv.reshape(b, t, qkv.shape[-1])
        o = _diff_sample_call(qkv_b, cck, ccv, p['lam_par'], p['g_sub'], wc, p['lam_init']).reshape(1, b * t, wc)
    x1, h_all, route = _out_router_call(x, o, gt_m1, p['g_ffn1'], sh_f1, sc_f1, p['w_out_odd'], p['w_router3'],
                                        p['n_exp'], tok_offset, h_all)
    return (x1, route, gt_f1), h_all, x_l0_state, (kc, vc)


def kernel(x_prompt, x_sample, cache_a_k, cache_a_v, cache_a_logf, cache_b_k, cache_b_v, cache_c_k, cache_c_v,
           c_prompt, c_sample, w_mod, b_mod, g_mix, g_ffn, g_final, w_in_even, b_forget, rel_bias, w_out_even,
           w_in_odd, lambda_q1, lambda_k1, lambda_q2, lambda_k2, g_subln, w_out_odd, w1_dense, w3_dense, w2_dense,
           w_router, w1_moe, w3_moe, w2_moe):
    bp, sp, d = x_prompt.shape
    bs, ts, _ = x_sample.shape
    h_a = cache_a_k.shape[3]
    h_b = cache_b_k.shape[3]
    h_c = cache_c_k.shape[3]
    wa, wb, wc = h_a * HEAD_DIM, h_b * HEAD_DIM, h_c * 2 * HEAD_DIM
    n_exp = w_router.shape[-1]
    assert w_mod.shape[0] == 2, "kernel is written for the 2-layer trunk"

    we = w_in_even[0]
    w_main = jnp.concatenate([we[:, :3 * wa], we[:, 3 * wa + h_a:]], axis=1).astype(BF)
    w_f = jnp.pad(we[:, 3 * wa:3 * wa + h_a], ((0, 0), (0, LANES - h_a))).astype(BF)
    b_f = jnp.pad(b_forget[0], (0, LANES - h_a)).reshape(1, LANES).astype(F32)
    wr = jnp.pad(w_router[0], ((0, 0), (0, LANES - n_exp)))
    wr_hi = wr.astype(BF)
    wr_r1 = wr - wr_hi.astype(F32)
    wr_mid = wr_r1.astype(BF)
    wr_lo = (wr_r1 - wr_mid.astype(F32)).astype(BF)
    params = dict(
        wa=wa, wb=wb, wc=wc, n_exp=n_exp, lam_init=0.8 - 0.6 * math.exp(-0.3 * 1),
        g_mix0=g_mix[0:1], g_mix1=g_mix[1:2], g_ffn0=g_ffn[0:1], g_ffn1=g_ffn[1:2],
        g_final=g_final.reshape(1, d),
        w_in_even=w_main, w_f=w_f, b_f=b_f, rel_bias=rel_bias[0],
        w_out_even=w_out_even[0].astype(BF), w_in_odd=w_in_odd[0].astype(BF),
        lam_par=jnp.concatenate([lambda_q1, lambda_k1, lambda_q2, lambda_k2], axis=0),
        g_sub=g_subln[0:1], w_out_odd=w_out_odd[0].astype(BF),
        w1_dense=w1_dense[0].astype(BF), w3_dense=w3_dense[0].astype(BF), w2_dense=w2_dense[0].astype(BF),
        w_router3=jnp.stack([wr_hi, wr_mid, wr_lo]),
        w1_moe=w1_moe[0].astype(BF), w3_moe=w3_moe[0].astype(BF), w2_moe=w2_moe[0].astype(BF),
    )

    mod = _mod_call(jnp.concatenate([c_prompt, c_sample], axis=0), w_mod, b_mod)

    def mods_for(rows, per_row_len):
        m = mod[:, rows].reshape(2, -1, 6, d).transpose(0, 2, 1, 3)
        if per_row_len:
            m = jnp.repeat(m, per_row_len, axis=2)[:, :, None]
        else:
            m = m[:, :, :, None]
        return [[(m, l, k) for k in range(6)] for l in range(2)]

    h_all = jnp.zeros(((bp * sp + bs * ts) * (d // LANES), LANES), F32)
    moe_p, h_all, ev_p, od_p = _trunk(x_prompt, mods_for(slice(0, bp), 0), params, None, 0, h_all)
    cache = (cache_a_k[0].reshape(bs, -1, wa), cache_a_v[0].reshape(bs, -1, wa), cache_a_logf[0],
             cache_b_k[0].reshape(bs, -1, wb), cache_b_v[0].reshape(bs, -1, wb),
             cache_c_k[0].reshape(bs, -1, 2 * HEAD_DIM), cache_c_v[0].reshape(bs, -1, 2 * HEAD_DIM), bs, ts)
    moe_s, h_all, ev_s, od_s = _trunk(x_sample.reshape(1, bs * ts, d), mods_for(slice(bp, bp + bs), ts), params,
                                      cache, bp * sp, h_all)
    y_p, y_s = _moe([moe_p, moe_s], h_all, params['g_final'], params['w1_moe'], params['w3_moe'], params['w2_moe'])

    ka, va, logf, kb, vb = ev_p
    keep = kb.shape[1]
    out_p = (ka.reshape(1, bp, sp, h_a, HEAD_DIM), va.reshape(1, bp, sp, h_a, HEAD_DIM),
             logf.reshape(1, bp, sp, h_a),
             kb.reshape(1, bp, keep, h_b, HEAD_DIM), vb.reshape(1, bp, keep, h_b, HEAD_DIM),
             od_p[0].reshape(1, bp, sp, h_c, 2 * HEAD_DIM), od_p[1].reshape(1, bp, sp, h_c, 2 * HEAD_DIM))
    ka, va, logf, kb, vb = ev_s
    new_bk = jnp.concatenate([cache_b_k[0], kb.reshape(bs, ts, h_b, HEAD_DIM)], axis=1)[:, ts:]
    new_bv = jnp.concatenate([cache_b_v[0], vb.reshape(bs, ts, h_b, HEAD_DIM)], axis=1)[:, ts:]
    out_s = (ka.reshape(1, bs, ts, h_a, HEAD_DIM), va.reshape(1, bs, ts, h_a, HEAD_DIM),
             logf.reshape(1, bs, ts, h_a), new_bk[None], new_bv[None],
             od_s[0].reshape(1, bs, ts, h_c, 2 * HEAD_DIM), od_s[1].reshape(1, bs, ts, h_c, 2 * HEAD_DIM))
    return (y_p, y_s.reshape(bs, ts, d)) + out_p + out_s
```

```python
import functools
import math

import jax
import jax.numpy as jnp
import numpy as np
from jax import lax
from jax.experimental import pallas as pl
from jax.experimental.pallas import tpu as pltpu

BF = jnp.bfloat16
F32 = jnp.float32

CHUNK = 64
CHUNK_SHIFT = 6
HEAD_DIM = 64
BAND_CHUNKS = 8
WINDOW_B = BAND_CHUNKS * CHUNK
MAX_REL = 128
TOP_K = 2
RMS_EPS = 1e-6
NEG_INF = -1e30
QK_SCALE = HEAD_DIM ** -0.5
LOG2E = math.log2(math.e)
PROMPT_Q_SCALE = QK_SCALE * LOG2E


def _bf16_terms(c, n):
    terms = []
    for _ in range(n):
        t = float(np.asarray(c, dtype=jnp.bfloat16))
        terms.append(t)
        c -= t
    return terms


LOG2E_TERMS = _bf16_terms(LOG2E, 3)

LANES = 128
SUBLANES = 8
VMEM_LIMIT_BYTES = 56 * 1024 * 1024
ROUTE_ROWS = SUBLANES

ROW_TILE = 512
ATTN_TILE = 512
BAND_Q_TILE = WINDOW_B // 2
CACHE_TILE_A = 1024
CACHE_TILE_C = 512
MOE_ROW_TILE = 512
MOE_F_TILE = 1792
COMBINE_TILE = 256


def _cparams(sem):
    return pltpu.CompilerParams(dimension_semantics=sem, vmem_limit_bytes=VMEM_LIMIT_BYTES)


def _dot(a, b):
    return jnp.dot(a, b, preferred_element_type=F32)


def _dot_nt(a, b):
    return lax.dot_general(a, b, (((1,), (1,)), ((), ())), preferred_element_type=F32)


def _sigmoid(x):
    return 1.0 / (1.0 + jnp.exp(-x))


def _modulate(x, g, shift, scale):
    y = x * lax.rsqrt(jnp.mean(x * x, axis=-1, keepdims=True) + RMS_EPS)
    return (y * g) * (1.0 + scale) + shift


def _split3(x):
    hi = x.astype(BF)
    r1 = x - hi.astype(F32)
    mid = r1.astype(BF)
    lo = (r1 - mid.astype(F32)).astype(BF)
    return hi, mid, lo


def _cumsum_lanes(x, upper):
    hi, mid, lo = _split3(x)
    return _dot(hi, upper) + _dot(mid, upper) + _dot(lo, upper)


def _upper_tri(t):
    r = jnp.arange(t)
    return (r[:, None] <= r[None, :]).astype(BF)


def _osm_update(s, v, m_ref, l_ref, acc_ref, idx):
    m_prev = m_ref[idx]
    m_new = jnp.maximum(m_prev, jnp.max(s, axis=-1, keepdims=True))
    alpha = jnp.exp(m_prev - m_new)
    p = jnp.exp(s - m_new)
    l_ref[idx] = alpha * l_ref[idx] + jnp.sum(p, axis=-1, keepdims=True)
    acc_ref[idx] = alpha * acc_ref[idx] + _dot(p.astype(BF), v)
    m_ref[idx] = m_new


def _osm_init(m_ref, l_ref, acc_ref):
    m_ref[...] = jnp.full(m_ref.shape, NEG_INF, F32)
    l_ref[...] = jnp.zeros(l_ref.shape, F32)
    acc_ref[...] = jnp.zeros(acc_ref.shape, F32)


def _mod_body(c_ref, w_ref, b_ref, o_ref):
    c = c_ref[...]
    s = (c * _sigmoid(c)).astype(BF)
    o_ref[0] = _dot(s, w_ref[0].astype(BF)) + b_ref[0]


def _mod_call(c_all, w_mod, b_mod):
    depth, d, n = w_mod.shape
    r = c_all.shape[0]
    tn = 1536 if n % 1536 == 0 else n
    return pl.pallas_call(
        _mod_body,
        grid=(depth, n // tn),
        in_specs=[pl.BlockSpec((r, d), lambda l, j: (0, 0)),
                  pl.BlockSpec((1, d, tn), lambda l, j: (l, 0, j)),
                  pl.BlockSpec((1, 1, tn), lambda l, j: (l, 0, j))],
        out_specs=pl.BlockSpec((1, r, tn), lambda l, j: (l, 0, j)),
        out_shape=jax.ShapeDtypeStruct((depth, r, n), F32),
        compiler_params=_cparams(("arbitrary", "arbitrary")),
        name="mod",
    )(c_all, w_mod, b_mod.reshape(depth, 1, n))


def _mod_spec(m, tm):
    arr, l, k = m
    d = arr.shape[-1]
    if arr.shape[3] == 1:
        return pl.BlockSpec((1, 1, 1, 1, d), lambda b, i: (l, k, b, 0, 0))
    return pl.BlockSpec((1, 1, 1, tm, d), lambda b, i: (l, k, b, i, 0))


def _const_spec(arr):
    nd = arr.ndim
    return pl.BlockSpec(arr.shape, lambda b, i: (0,) * nd, pipeline_mode=pl.Buffered(1))


def _head_blocks(x, augs):
    tm, w = x.shape
    first = lax.broadcasted_iota(jnp.int32, (tm, LANES), 1) < HEAD_DIM
    out = []
    for p in range(w // LANES):
        pair = x[:, p * LANES:(p + 1) * LANES]
        out.append(jnp.where(first, pair, augs[2 * p]).astype(BF))
        out.append(jnp.where(first, pltpu.roll(pair, HEAD_DIM, 1), augs[2 * p + 1]).astype(BF))
    return out


def _lane_range_row(lo, hi, value):
    lane = lax.broadcasted_iota(jnp.int32, (1, LANES), 1)
    return jnp.where((lane >= lo) & (lane < hi), value, 0.0).astype(F32)


def _ones_rows(t):
    return jnp.where(lax.broadcasted_iota(jnp.int32, (16, t), 0) == 0, 1.0, 0.0).astype(BF)


def _log_forget(h, wf_ref, bf_ref):
    z = _dot(h, wf_ref[...]) + bf_ref[...]
    return jnp.minimum(z, 0.0) - jnp.log1p(jnp.exp(-jnp.abs(z)))


def _inproj_even_body(*refs, wa, wb, prompt):
    if prompt:
        (x_ref, g_ref, sh_ref, sc_ref, w_ref, wf_ref, bf_ref, low_ref, place_ref,
         qk_ref, vt_ref, ka_ref, va_ref, kb_ref, vb_ref, logf_ref, carry_ref) = refs
    else:
        (x_ref, g_ref, sh_ref, sc_ref, w_ref, wf_ref, bf_ref,
         qkv_ref, ka_ref, va_ref, kb_ref, vb_ref, logf_ref, lft_ref) = refs
    i = pl.program_id(1)
    h = _modulate(x_ref[0], g_ref[...], sh_ref[0, 0, 0], sc_ref[0, 0, 0]).astype(BF)
    tm = h.shape[0]
    nh = logf_ref.shape[-1]

    qa = _dot(h, w_ref[:, 0:wa])
    ka = _dot(h, w_ref[:, wa:2 * wa])
    va = _dot(h, w_ref[:, 2 * wa:3 * wa])
    o = 3 * wa
    qb = _dot(h, w_ref[:, o:o + wb])
    kb = _dot(h, w_ref[:, o + wb:o + 2 * wb])
    vb = _dot(h, w_ref[:, o + 2 * wb:o + 3 * wb])
    ka_ref[0] = ka
    va_ref[0] = va
    kb_ref[0] = kb
    vb_ref[0] = vb
    logf = _log_forget(h, wf_ref, bf_ref)
    logf_ref[0] = logf[:, :nh]

    if not prompt:
        qkv_ref[0] = jnp.concatenate(
            [(qa * QK_SCALE).astype(BF), ka.astype(BF), va.astype(BF),
             (qb * QK_SCALE).astype(BF), kb.astype(BF), vb.astype(BF)], axis=-1)
        lft_ref[0] = logf.T[:nh, :]
        return

    @pl.when(i == 0)
    def _():
        carry_ref[...] = jnp.zeros(carry_ref.shape, F32)

    low = low_ref[...]
    hi, mid, lo = _split3(logf)
    f_cum = _dot(low, hi) + _dot(low, mid) + _dot(low, lo) + carry_ref[0:1, :]
    carry_ref[...] = jnp.broadcast_to(f_cum[tm - 1:tm, :], carry_ref.shape)
    nhi, nmid, nlo = _split3(-f_cum)
    k_aug = _dot(nhi, place_ref[0]) + _dot(nmid, place_ref[1]) + _dot(nlo, place_ref[2])
    q_log2e = sum(_lane_range_row(HEAD_DIM + 3 * r, HEAD_DIM + 3 * r + 3, c) for r, c in enumerate(LOG2E_TERMS))
    zero = jnp.zeros((1, LANES), F32)
    nhb = wb // HEAD_DIM
    blocks = (_head_blocks(qa * PROMPT_Q_SCALE, [q_log2e] * nh)
              + _head_blocks(ka, [k_aug[:, n * LANES:(n + 1) * LANES] for n in range(nh)])
              + _head_blocks(qb * PROMPT_Q_SCALE, [zero] * nhb)
              + _head_blocks(kb, [zero] * nhb))
    qk_ref[0] = jnp.concatenate(blocks, axis=-1)
    ones = _ones_rows(tm)
    vt_ref[0] = jnp.concatenate([va.T.astype(BF), ones, vb.T.astype(BF), ones], axis=0)


def _lower_tri(t):
    r = jnp.arange(t)
    return (r[:, None] >= r[None, :]).astype(BF)


def _place_matrices(nh):
    s = jnp.arange(3)[:, None, None]
    r = jnp.arange(LANES)[None, :, None]
    c = jnp.arange(nh * LANES)[None, None, :]
    off = c - r * LANES - HEAD_DIM - s
    return ((r < nh) & ((off == 0) | (off == 3) | (off == 6))).astype(BF)


def _inproj_even_call(x, g, shift, scale, w_main, w_f, b_f, keep, prompt):
    bx, sx, d = x.shape
    tm = min(ROW_TILE, sx)
    nt = sx // tm
    n = w_main.shape[1]
    wa = wb = n // 6
    nh = wa // HEAD_DIM
    nkeep = keep // tm
    tail = lambda b, i: (b, jnp.maximum(i - (nt - nkeep), 0), 0)
    row = lambda b, i: (b, i, 0)
    in_specs = [pl.BlockSpec((1, tm, d), row), _const_spec(g), _mod_spec(shift, tm), _mod_spec(scale, tm),
                _const_spec(w_main), _const_spec(w_f), _const_spec(b_f)]
    args = [x, g, shift[0], scale[0], w_main, w_f, b_f]
    f32_specs = [pl.BlockSpec((1, tm, wa), row), pl.BlockSpec((1, tm, wa), row),
                 pl.BlockSpec((1, tm, wb), tail), pl.BlockSpec((1, tm, wb), tail),
                 pl.BlockSpec((1, tm, nh), row)]
    f32_shapes = [jax.ShapeDtypeStruct((bx, sx, wa), F32), jax.ShapeDtypeStruct((bx, sx, wa), F32),
                  jax.ShapeDtypeStruct((bx, keep, wb), F32), jax.ShapeDtypeStruct((bx, keep, wb), F32),
                  jax.ShapeDtypeStruct((bx, sx, nh), F32)]
    scratch = []
    if prompt:
        low, place = _lower_tri(tm), _place_matrices(nh)
        in_specs += [_const_spec(low), _const_spec(place)]
        args += [low, place]
        nqk = 2 * (wa + wb) // HEAD_DIM * LANES
        nvt = wa + wb + 32
        out_specs = [pl.BlockSpec((1, tm, nqk), row), pl.BlockSpec((1, nvt, tm), lambda b, i: (b, 0, i))] + f32_specs
        out_shape = [jax.ShapeDtypeStruct((bx, sx, nqk), BF), jax.ShapeDtypeStruct((bx, nvt, sx), BF)] + f32_shapes
        scratch.append(pltpu.VMEM((8, LANES), F32))
    else:
        out_specs = [pl.BlockSpec((1, tm, n), row)] + f32_specs + [pl.BlockSpec((1, nh, tm), lambda b, i: (b, 0, i))]
        out_shape = [jax.ShapeDtypeStruct((bx, sx, n), BF)] + f32_shapes + [jax.ShapeDtypeStruct((bx, nh, sx), F32)]
    return pl.pallas_call(
        functools.partial(_inproj_even_body, wa=wa, wb=wb, prompt=prompt),
        grid=(bx, nt), in_specs=in_specs, out_specs=out_specs, out_shape=out_shape,
        scratch_shapes=scratch,
        compiler_params=_cparams(("arbitrary", "arbitrary")),
        name="inproj_even",
    )(*args)


def _inproj_odd_body(*refs, wc, prompt):
    if prompt:
        x_ref, g_ref, sh_ref, sc_ref, w_ref, qk_ref, vt_ref, k_ref, v_ref = refs
    else:
        x_ref, g_ref, sh_ref, sc_ref, w_ref, qkv_ref, k_ref, v_ref = refs
    i = pl.program_id(1)
    h = _modulate(x_ref[0], g_ref[...], sh_ref[0, 0, 0], sc_ref[0, 0, 0]).astype(BF)
    tm = h.shape[0]
    q = _dot(h, w_ref[:, 0:wc])
    k = _dot(h, w_ref[:, wc:2 * wc])
    v = _dot(h, w_ref[:, 2 * wc:3 * wc])
    k_ref[0] = k
    v_ref[0] = v
    if not prompt:
        qkv_ref[0] = jnp.concatenate([(q * QK_SCALE).astype(BF), k.astype(BF), v.astype(BF)], axis=-1)
        return
    nh = wc // LANES
    pos = (i * tm + lax.broadcasted_iota(jnp.int32, (tm, LANES), 0)).astype(F32)
    pos_hi = pos.astype(BF).astype(F32)
    lane = lax.broadcasted_iota(jnp.int32, (tm, LANES), 1) - HEAD_DIM
    in_aug = (lane >= 0) & (lane < 2 * len(LOG2E_TERMS))
    k_aug = jnp.where(in_aug, jnp.where((lane & 1) == 0, pos_hi, pos - pos_hi), 0.0)
    q_augs = []
    for hh in range(nh):
        slope = _alibi_slope(hh, nh)
        row = sum(_lane_range_row(HEAD_DIM + 2 * r, HEAD_DIM + 2 * r + 2, slope * c) for r, c in enumerate(LOG2E_TERMS))
        q_augs += [row] * 2
    blocks = _head_blocks(q * PROMPT_Q_SCALE, q_augs) + _head_blocks(k, [k_aug] * (2 * nh))
    qk_ref[0] = jnp.concatenate(blocks, axis=-1)
    vt_ref[0] = jnp.concatenate([v.T.astype(BF), _ones_rows(tm)], axis=0)


def _alibi_slope(h, nh):
    assert 8 % nh == 0, "slopes must be exact powers of two to ride in bf16"
    return 2.0 ** (-8.0 * (h + 1) / nh)


def _inproj_odd_call(x, g, shift, scale, w, prompt):
    bx, sx, d = x.shape
    tm = min(ROW_TILE, sx)
    n = w.shape[1]
    wc = n // 3
    row = lambda b, i: (b, i, 0)
    f32_specs = [pl.BlockSpec((1, tm, wc), row), pl.BlockSpec((1, tm, wc), row)]
    f32_shapes = [jax.ShapeDtypeStruct((bx, sx, wc), F32), jax.ShapeDtypeStruct((bx, sx, wc), F32)]
    if prompt:
        nqk = 2 * wc // HEAD_DIM * LANES
        out_specs = [pl.BlockSpec((1, tm, nqk), row), pl.BlockSpec((1, wc + 16, tm), lambda b, i: (b, 0, i))]
        out_shape = [jax.ShapeDtypeStruct((bx, sx, nqk), BF), jax.ShapeDtypeStruct((bx, wc + 16, sx), BF)]
    else:
        out_specs = [pl.BlockSpec((1, tm, n), row)]
        out_shape = [jax.ShapeDtypeStruct((bx, sx, n), BF)]
    return pl.pallas_call(
        functools.partial(_inproj_odd_body, wc=wc, prompt=prompt),
        grid=(bx, sx // tm),
        in_specs=[pl.BlockSpec((1, tm, d), row), _const_spec(g), _mod_spec(shift, tm), _mod_spec(scale, tm),
                  _const_spec(w)],
        out_specs=out_specs + f32_specs, out_shape=out_shape + f32_shapes,
        compiler_params=_cparams(("arbitrary", "arbitrary")),
        name="inproj_odd",
    )(x, g, shift[0], scale[0], w)


def _tflash_update(st, v_aug, mask, extra, m_ref, acc_ref, idx):
    if extra is not None:
        st = st - extra
    if mask is not None:
        st = jnp.where(mask, st, NEG_INF)
    m_prev = m_ref[idx]
    m_new = jnp.maximum(m_prev, jnp.max(st, axis=0, keepdims=True))
    alpha = jnp.exp2(m_prev - m_new)
    p = jnp.exp2(st - m_new).astype(BF)
    acc_ref[idx] = alpha * acc_ref[idx] + _dot(v_aug, p)
    m_ref[idx] = m_new


def _tflash_init(m_ref, acc_ref):
    m_ref[...] = jnp.full(m_ref.shape, NEG_INF, F32)
    acc_ref[...] = jnp.zeros(acc_ref.shape, F32)


def _causal_pairs(nq):
    pairs = [(i, j) for i in range(nq) for j in range(i + 1)]
    return jnp.asarray([p[0] for p in pairs], jnp.int32), jnp.asarray([p[1] for p in pairs], jnp.int32)


def _fox_prompt_body(it_ref, jt_ref, q_ref, k_ref, vt_ref, o_ref, m_ref, acc_ref, st_ref, *, nh):
    i = it_ref[pl.program_id(1)]
    j = jt_ref[pl.program_id(1)]
    hd = HEAD_DIM

    @pl.when(j == 0)
    def _():
        _tflash_init(m_ref, acc_ref)

    def step(masked):
        q = q_ref[0]
        k = k_ref[0]
        vt = vt_ref[0]
        tq, tk = q.shape[0], k.shape[0]
        ones = vt[nh * hd:, :]
        mask = None
        if masked:
            mask = lax.broadcasted_iota(jnp.int32, (tk, tq), 0) <= lax.broadcasted_iota(jnp.int32, (tk, tq), 1)
        def scores(h):
            st_ref[h % 2] = _dot_nt(k[:, h * LANES:(h + 1) * LANES], q[:, h * LANES:(h + 1) * LANES])

        scores(0)
        for h in range(nh):
            if h + 1 < nh:
                scores(h + 1)
            v_aug = jnp.concatenate([vt[h * hd:(h + 1) * hd, :], ones], axis=0)
            _tflash_update(st_ref[h % 2], v_aug, mask, None, m_ref, acc_ref, h)

    @pl.when(j < i)
    def _():
        step(False)

    @pl.when(j == i)
    def _():
        step(True)
        ot = jnp.concatenate([acc_ref[h][:hd] / acc_ref[h][hd:hd + 1] for h in range(nh)], axis=0)
        o_ref[0] = ot.T.astype(BF)


def _fox_prompt_call(qk, vt, wa):
    b, s, _ = qk.shape
    nh = wa // HEAD_DIM
    t = min(ATTN_TILE, s)
    it, jt = _causal_pairs(s // t)
    wq = nh * LANES
    grid_spec = pltpu.PrefetchScalarGridSpec(
        num_scalar_prefetch=2,
        grid=(b, it.shape[0]),
        in_specs=[pl.BlockSpec((1, t, wq), lambda b, s, it, jt: (b, it[s], 0)),
                  pl.BlockSpec((1, t, wq), lambda b, s, it, jt: (b, jt[s], 1)),
                  pl.BlockSpec((1, wa + 16, t), lambda b, s, it, jt: (b, 0, jt[s]))],
        out_specs=pl.BlockSpec((1, t, wa), lambda b, s, it, jt: (b, it[s], 0)),
        scratch_shapes=[pltpu.VMEM((nh, 1, t), F32), pltpu.VMEM((nh, HEAD_DIM + 16, t), F32),
                        pltpu.VMEM((2, t, t), F32)],
    )
    return pl.pallas_call(
        functools.partial(_fox_prompt_body, nh=nh),
        grid_spec=grid_spec,
        out_shape=jax.ShapeDtypeStruct((b, s, wa), BF),
        compiler_params=_cparams(("arbitrary", "arbitrary")),
        name="fox_prompt",
    )(it, jt, qk, qk, vt)


def _band_prompt_body(q_ref, k0_ref, k1_ref, k2_ref, v0_ref, v1_ref, v2_ref, bias_ref, o_ref, st_ref, *, nh):
    i = pl.program_id(1)
    hd = HEAD_DIM
    q = q_ref[0]
    tq = q.shape[0]
    k = jnp.concatenate([k0_ref[0], k1_ref[0], k2_ref[0]], axis=0)
    vts = [v0_ref[0], v1_ref[0], v2_ref[0]]
    in_seq = lax.broadcasted_iota(jnp.int32, (3 * tq, tq), 0) >= (2 - i) * tq
    outs = []
    def scores(h):
        st_ref[h % 2] = _dot_nt(k[:, h * LANES:(h + 1) * LANES], q[:, h * LANES:(h + 1) * LANES])

    scores(0)
    for h in range(nh):
        if h + 1 < nh:
            scores(h + 1)
        st = st_ref[h % 2] + bias_ref[h]
        st = jnp.where(in_seq, st, NEG_INF)
        p = jnp.exp2(st - jnp.max(st, axis=0, keepdims=True)).astype(BF)
        acc = None
        for w, vt in enumerate(vts):
            v_aug = jnp.concatenate([vt[h * hd:(h + 1) * hd, :], vt[nh * hd:, :]], axis=0)
            part = _dot(v_aug, p[w * tq:(w + 1) * tq, :])
            acc = part if acc is None else acc + part
        outs.append(acc[:hd] / acc[hd:hd + 1])
    o_ref[0] = jnp.concatenate(outs, axis=0).T.astype(BF)


def _band_bias_prompt(rel_bias, tq):
    nh = rel_bias.shape[0]
    period = 4 * tq
    t = jnp.arange(period)
    t = jnp.where(t < 3 * tq, t, t - period)
    u = rel_bias[:, jnp.clip(2 * tq - t, -MAX_REL, MAX_REL) + MAX_REL].astype(F32)
    skew = jnp.tile(u, (1, tq))[:, :tq * (period - 1)].reshape(nh, tq, period - 1)
    bias = skew[:, :, :3 * tq]
    r = jnp.arange(tq)[:, None]
    c = jnp.arange(3 * tq)[None, :]
    valid = (c // CHUNK >= r // CHUNK) & (c // CHUNK <= r // CHUNK + BAND_CHUNKS)
    return jnp.where(valid[None], bias, NEG_INF)


def _band_prompt_call(qk, vt, rel_bias, wa, wb):
    b, s, _ = qk.shape
    nh = wb // HEAD_DIM
    tq = BAND_Q_TILE
    assert s % tq == 0 and wa == wb
    bias_t = (_band_bias_prompt(rel_bias, tq) * LOG2E).transpose(0, 2, 1)
    wq = nh * LANES
    cq, ck = 2, 3

    def k_spec(back):
        return pl.BlockSpec((1, tq, wq), lambda b, i: (b, jnp.maximum(i - back, 0), ck))

    def v_spec(back):
        return pl.BlockSpec((1, wb + 16, tq), lambda b, i: (b, 1, jnp.maximum(i - back, 0)))

    return pl.pallas_call(
        functools.partial(_band_prompt_body, nh=nh),
        grid=(b, s // tq),
        in_specs=[pl.BlockSpec((1, tq, wq), lambda b, i: (b, i, cq)),
                  k_spec(2), k_spec(1), k_spec(0), v_spec(2), v_spec(1), v_spec(0),
                  _const_spec(bias_t)],
        out_specs=pl.BlockSpec((1, tq, wb), lambda b, i: (b, i, 0)),
        out_shape=jax.ShapeDtypeStruct((b, s, wb), BF),
        scratch_shapes=[pltpu.VMEM((2, 3 * tq, tq), F32)],
        compiler_params=_cparams(("arbitrary", "arbitrary")),
        name="band_prompt",
    )(qk, qk, qk, qk, vt, vt, vt, bias_t)


def _lambda(lam_ref, lam_init):
    lp = lam_ref[...]
    a = jnp.sum(lp[0:1] * lp[1:2], axis=-1, keepdims=True)
    b = jnp.sum(lp[2:3] * lp[3:4], axis=-1, keepdims=True)
    return jnp.exp(a) - jnp.exp(b) + lam_init


def _diff_prompt_body(it_ref, jt_ref, q_ref, k_ref, vt_ref, lam_ref, gcol_ref, o_ref, m_ref, acc_ref, st_ref,
                      *, nh, lam_init):
    i = it_ref[pl.program_id(1)]
    j = jt_ref[pl.program_id(1)]
    dv = 2 * HEAD_DIM

    @pl.when(j == 0)
    def _():
        _tflash_init(m_ref, acc_ref)

    def step(masked):
        q = q_ref[0]
        k = k_ref[0]
        vt = vt_ref[0]
        tq, tk = q.shape[0], k.shape[0]
        ones = vt[nh * dv:, :]
        if masked:
            key = lax.broadcasted_iota(jnp.int32, (tk, tq), 0)
            qry = lax.broadcasted_iota(jnp.int32, (tk, tq), 1)
            visible = (key >> CHUNK_SHIFT) <= (qry >> CHUNK_SHIFT)
            ahead = jnp.where(visible, 2.0 * jnp.maximum(key - qry, 0).astype(F32), -NEG_INF)
        def scores(n):
            st_ref[n % 2] = _dot_nt(k[:, n * LANES:(n + 1) * LANES], q[:, n * LANES:(n + 1) * LANES])

        scores(0)
        for h in range(nh):
            v_aug = jnp.concatenate([vt[h * dv:(h + 1) * dv, :], ones], axis=0)
            extra = (LOG2E * _alibi_slope(h, nh)) * ahead if masked else None
            for u in range(2):
                n = 2 * h + u
                if n + 1 < 2 * nh:
                    scores(n + 1)
                _tflash_update(st_ref[n % 2], v_aug, None, extra, m_ref, acc_ref, n)

    @pl.when(j < i)
    def _():
        step(False)

    @pl.when(j == i)
    def _():
        step(True)
        lam = _lambda(lam_ref, lam_init)
        g = gcol_ref[...]
        outs = []
        for h in range(nh):
            a1 = acc_ref[2 * h]
            a2 = acc_ref[2 * h + 1]
            o = a1[:dv] / a1[dv:dv + 1] - lam * (a2[:dv] / a2[dv:dv + 1])
            o = o * lax.rsqrt(jnp.mean(o * o, axis=0, keepdims=True) + RMS_EPS)
            outs.append((o * g) * (1.0 - lam_init))
        o_ref[0] = jnp.concatenate(outs, axis=0).T.astype(BF)


def _diff_prompt_call(qk, vt, lam_par, g_col, wc, lam_init):
    b, s, _ = qk.shape
    nh = wc // (2 * HEAD_DIM)
    t = min(ATTN_TILE, s)
    it, jt = _causal_pairs(s // t)
    wq = 2 * nh * LANES
    grid_spec = pltpu.PrefetchScalarGridSpec(
        num_scalar_prefetch=2,
        grid=(b, it.shape[0]),
        in_specs=[pl.BlockSpec((1, t, wq), lambda b, s, it, jt: (b, it[s], 0)),
                  pl.BlockSpec((1, t, wq), lambda b, s, it, jt: (b, jt[s], 1)),
                  pl.BlockSpec((1, wc + 16, t), lambda b, s, it, jt: (b, 0, jt[s])),
                  pl.BlockSpec(lam_par.shape, lambda b, s, it, jt: (0, 0)),
                  pl.BlockSpec(g_col.shape, lambda b, s, it, jt: (0, 0))],
        out_specs=pl.BlockSpec((1, t, wc), lambda b, s, it, jt: (b, it[s], 0)),
        scratch_shapes=[pltpu.VMEM((2 * nh, 1, t), F32), pltpu.VMEM((2 * nh, 2 * HEAD_DIM + 16, t), F32),
                        pltpu.VMEM((2, t, t), F32)],
    )
    return pl.pallas_call(
        functools.partial(_diff_prompt_body, nh=nh, lam_init=lam_init),
        grid_spec=grid_spec,
        out_shape=jax.ShapeDtypeStruct((b, s, wc), BF),
        compiler_params=_cparams(("arbitrary", "arbitrary")),
        name="diff_prompt",
    )(it, jt, qk, qk, vt, lam_par, g_col)


def _cumsum_body(x_ref, up_ref, o_ref, carry_ref):
    @pl.when(pl.program_id(1) == 0)
    def _():
        carry_ref[...] = jnp.zeros(carry_ref.shape, F32)
    ft = _cumsum_lanes(x_ref[0], up_ref[...]) + carry_ref[:, 0:1]
    o_ref[0] = ft
    carry_ref[...] = jnp.broadcast_to(ft[:, -1:], carry_ref.shape)


def _cumsum_call(x):
    b, nh, length = x.shape
    t = 512
    up = _upper_tri(t)
    return pl.pallas_call(
        _cumsum_body,
        grid=(b, length // t),
        in_specs=[pl.BlockSpec((1, nh, t), lambda b, j: (b, 0, j)), pl.BlockSpec((t, t), lambda b, j: (0, 0))],
        out_specs=pl.BlockSpec((1, nh, t), lambda b, j: (b, 0, j)),
        out_shape=jax.ShapeDtypeStruct(x.shape, F32),
        scratch_shapes=[pltpu.VMEM((nh, LANES), F32)],
        compiler_params=_cparams(("arbitrary", "arbitrary")),
        name="logf_cumsum",
    )(x, up)


def _fox_sample_body(q_ref, kn_ref, vn_ref, ck_ref, cv_ref, ftc_ref, ftn_ref, o_ref, m_ref, l_ref, acc_ref, *, nh):
    j = pl.program_id(1)
    hd = HEAD_DIM
    q = q_ref[0]
    t = q.shape[0]

    @pl.when(j == 0)
    def _():
        _osm_init(m_ref, l_ref, acc_ref)

    k = ck_ref[0].astype(BF)
    v = cv_ref[0].astype(BF)
    ft = ftc_ref[0]
    for h in range(nh):
        sl = slice(h * hd, (h + 1) * hd)
        s = _dot_nt(q[:, sl], k[:, sl]) - ft[h:h + 1, :]
        _osm_update(s, v[:, sl], m_ref, l_ref, acc_ref, h)

    @pl.when(j == pl.num_programs(1) - 1)
    def _():
        kn = kn_ref[0]
        vn = vn_ref[0]
        ftn = ftn_ref[0][:, :t]
        visible = lax.broadcasted_iota(jnp.int32, (t, t), 1) <= lax.broadcasted_iota(jnp.int32, (t, t), 0)
        outs = []
        for h in range(nh):
            sl = slice(h * hd, (h + 1) * hd)
            s = _dot_nt(q[:, sl], kn[:, sl]) - ftn[h:h + 1, :]
            s = jnp.where(visible, s, NEG_INF)
            _osm_update(s, vn[:, sl], m_ref, l_ref, acc_ref, h)
            outs.append((acc_ref[h] / l_ref[h]).astype(BF))
        o_ref[0] = jnp.concatenate(outs, axis=-1)


def _fox_sample_call(qkv, cache_k, cache_v, ft_pad, wa):
    b, t, _ = qkv.shape
    p_len = cache_k.shape[1]
    nh = wa // HEAD_DIM
    tk = min(CACHE_TILE_A, p_len)
    assert p_len % tk == 0 and p_len % LANES == 0 and t <= LANES
    return pl.pallas_call(
        functools.partial(_fox_sample_body, nh=nh),
        grid=(b, p_len // tk),
        in_specs=[pl.BlockSpec((1, t, wa), lambda b, j: (b, 0, 0)),
                  pl.BlockSpec((1, t, wa), lambda b, j: (b, 0, 1)),
                  pl.BlockSpec((1, t, wa), lambda b, j: (b, 0, 2)),
                  pl.BlockSpec((1, tk, wa), lambda b, j: (b, j, 0)),
                  pl.BlockSpec((1, tk, wa), lambda b, j: (b, j, 0)),
                  pl.BlockSpec((1, nh, tk), lambda b, j: (b, 0, j)),
                  pl.BlockSpec((1, nh, LANES), lambda b, j: (b, 0, p_len // LANES))],
        out_specs=pl.BlockSpec((1, t, wa), lambda b, j: (b, 0, 0)),
        out_shape=jax.ShapeDtypeStruct((b, t, wa), BF),
        scratch_shapes=[pltpu.VMEM((nh, t, 1), F32), pltpu.VMEM((nh, t, 1), F32),
                        pltpu.VMEM((nh, t, HEAD_DIM), F32)],
        compiler_params=_cparams(("arbitrary", "arbitrary")),
        name="fox_sample",
    )(qkv, qkv, qkv, cache_k, cache_v, ft_pad, ft_pad)


def _band_sample_body(q_ref, kn_ref, vn_ref, bk_ref, bv_ref, bias_b_ref, bias_n_ref, o_ref, *, nh):
    hd = HEAD_DIM
    q = q_ref[0]
    kn = kn_ref[0]
    vn = vn_ref[0]
    kb = bk_ref[0].astype(BF)
    vb = bv_ref[0].astype(BF)
    outs = []
    for h in range(nh):
        sl = slice(h * hd, (h + 1) * hd)
        sb = _dot_nt(q[:, sl], kb[:, sl]) + bias_b_ref[h]
        sn = _dot_nt(q[:, sl], kn[:, sl]) + bias_n_ref[h]
        m = jnp.maximum(jnp.max(sb, axis=-1, keepdims=True), jnp.max(sn, axis=-1, keepdims=True))
        pb = jnp.exp(sb - m)
        pn = jnp.exp(sn - m)
        l = jnp.sum(pb, axis=-1, keepdims=True) + jnp.sum(pn, axis=-1, keepdims=True)
        o = _dot(pb.astype(BF), vb[:, sl]) + _dot(pn.astype(BF), vn[:, sl])
        outs.append((o / l).astype(BF))
    o_ref[0] = jnp.concatenate(outs, axis=-1)


def _band_sample_call(qkv, buf_k, buf_v, rel_bias, wb):
    b, t, _ = qkv.shape
    lb = buf_k.shape[1]
    nh = wb // HEAD_DIM
    k_pos = jnp.concatenate([jnp.arange(-lb, 0), jnp.arange(t)])
    rel = jnp.arange(t)[:, None] - k_pos[None, :]
    bias = rel_bias[:, jnp.clip(rel, -MAX_REL, MAX_REL) + MAX_REL].astype(F32)
    bias_b, bias_n = bias[:, :, :lb], bias[:, :, lb:]
    return pl.pallas_call(
        functools.partial(_band_sample_body, nh=nh),
        grid=(b,),
        in_specs=[pl.BlockSpec((1, t, wb), lambda b: (b, 0, 3)),
                  pl.BlockSpec((1, t, wb), lambda b: (b, 0, 4)),
                  pl.BlockSpec((1, t, wb), lambda b: (b, 0, 5)),
                  pl.BlockSpec((1, lb, wb), lambda b: (b, 0, 0)),
                  pl.BlockSpec((1, lb, wb), lambda b: (b, 0, 0)),
                  pl.BlockSpec(bias_b.shape, lambda b: (0, 0, 0)),
                  pl.BlockSpec(bias_n.shape, lambda b: (0, 0, 0))],
        out_specs=pl.BlockSpec((1, t, wb), lambda b: (b, 0, 0)),
        out_shape=jax.ShapeDtypeStruct((b, t, wb), BF),
        compiler_params=_cparams(("arbitrary",)),
        name="band_sample",
    )(qkv, qkv, qkv, buf_k, buf_v, bias_b, bias_n)


def _diff_sample_body(q_ref, kn_ref, vn_ref, ck_ref, cv_ref, lam_ref, gsub_ref, o_ref, m_ref, l_ref, acc_ref,
                      *, nh, lam_init, p_len):
    j = pl.program_id(1)
    q = q_ref[0]
    t = q.shape[0]
    dv = 2 * HEAD_DIM
    lane = lax.broadcasted_iota(jnp.int32, (t, dv), 1)

    @pl.when(j == 0)
    def _():
        _osm_init(m_ref, l_ref, acc_ref)

    def attend(k_of, v_of, dist):
        dist2 = jnp.concatenate([dist, dist], axis=0)
        for h in range(nh):
            x = q[:, h * dv:(h + 1) * dv]
            zero = jnp.zeros_like(x)
            q2 = jnp.concatenate([jnp.where(lane < HEAD_DIM, x, zero), jnp.where(lane >= HEAD_DIM, x, zero)], axis=0)
            s = _dot_nt(q2, k_of(h)) - _alibi_slope(h, nh) * dist2
            _osm_update(s, v_of(h), m_ref, l_ref, acc_ref, h)

    tk = ck_ref.shape[1] // nh
    row = lax.broadcasted_iota(jnp.int32, (t, tk), 0)
    col = lax.broadcasted_iota(jnp.int32, (t, tk), 1)
    attend(lambda h: ck_ref[0, pl.ds(h, tk, stride=nh), :].astype(BF),
           lambda h: cv_ref[0, pl.ds(h, tk, stride=nh), :].astype(BF),
           (p_len + row - j * tk - col).astype(F32))

    @pl.when(j == pl.num_programs(1) - 1)
    def _():
        kn = kn_ref[0]
        vn = vn_ref[0]
        rown = lax.broadcasted_iota(jnp.int32, (t, t), 0)
        coln = lax.broadcasted_iota(jnp.int32, (t, t), 1)
        attend(lambda h: kn[:, h * dv:(h + 1) * dv], lambda h: vn[:, h * dv:(h + 1) * dv],
               jnp.abs(rown - coln).astype(F32))
        lam = _lambda(lam_ref, lam_init)
        g = gsub_ref[...]
        outs = []
        for h in range(nh):
            on = acc_ref[h] / l_ref[h]
            o = on[:t] - lam * on[t:]
            o = o * lax.rsqrt(jnp.mean(o * o, axis=-1, keepdims=True) + RMS_EPS)
            outs.append(((o * g) * (1.0 - lam_init)).astype(BF))
        o_ref[0] = jnp.concatenate(outs, axis=-1)


def _diff_sample_call(qkv, cache_k, cache_v, lam_par, g_sub, wc, lam_init):
    b, t, _ = qkv.shape
    nh = wc // (2 * HEAD_DIM)
    p_len = cache_k.shape[1] // nh
    assert (p_len // CHUNK) * CHUNK == p_len and t <= CHUNK
    tk = min(CACHE_TILE_C, p_len)
    return pl.pallas_call(
        functools.partial(_diff_sample_body, nh=nh, lam_init=lam_init, p_len=p_len),
        grid=(b, p_len // tk),
        in_specs=[pl.BlockSpec((1, t, wc), lambda b, j: (b, 0, 0)),
                  pl.BlockSpec((1, t, wc), lambda b, j: (b, 0, 1)),
                  pl.BlockSpec((1, t, wc), lambda b, j: (b, 0, 2)),
                  pl.BlockSpec((1, tk * nh, 2 * HEAD_DIM), lambda b, j: (b, j, 0)),
                  pl.BlockSpec((1, tk * nh, 2 * HEAD_DIM), lambda b, j: (b, j, 0)),
                  pl.BlockSpec(lam_par.shape, lambda b, j: (0, 0)),
                  pl.BlockSpec(g_sub.shape, lambda b, j: (0, 0))],
        out_specs=pl.BlockSpec((1, t, wc), lambda b, j: (b, 0, 0)),
        out_shape=jax.ShapeDtypeStruct((b, t, wc), BF),
        scratch_shapes=[pltpu.VMEM((nh, 2 * t, 1), F32), pltpu.VMEM((nh, 2 * t, 1), F32),
                        pltpu.VMEM((nh, 2 * t, 2 * HEAD_DIM), F32)],
        compiler_params=_cparams(("arbitrary", "arbitrary")),
        name="diff_sample",
    )(qkv, qkv, qkv, cache_k, cache_v, lam_par, g_sub)


def _out_ffn_body(x_ref, oa_ref, ob_ref, gm_ref, g_ref, sh_ref, sc_ref, gf_ref, wo_ref, w1_ref, w3_ref, w2_ref, y_ref,
                  *, f_chunk):
    wa = oa_ref.shape[-1]
    attn = _dot(oa_ref[0], wo_ref[:wa, :]) + _dot(ob_ref[0], wo_ref[wa:, :])
    x1 = x_ref[0] + gm_ref[0, 0, 0] * attn
    h = _modulate(x1, g_ref[...], sh_ref[0, 0, 0], sc_ref[0, 0, 0]).astype(BF)
    f_total = w1_ref.shape[1]
    f = None
    for c0 in range(0, f_total, f_chunk):
        a = _dot(h, w1_ref[:, c0:c0 + f_chunk])
        g = _dot(h, w3_ref[:, c0:c0 + f_chunk])
        u = ((a * _sigmoid(a)) * g).astype(BF)
        part = _dot(u, w2_ref[c0:c0 + f_chunk, :])
        f = part if f is None else f + part
    y_ref[0] = x1 + gf_ref[0, 0, 0] * f


def _out_ffn_call(x, o_a, o_b, gate_m, g, shift, scale, gate_f, wo, w1, w3, w2):
    bx, sx, d = x.shape
    tm = min(ROW_TILE, sx)
    f_total = w1.shape[1]
    f_chunk = f_total // 2 if (f_total // 2) % LANES == 0 else f_total
    row = lambda b, i: (b, i, 0)
    return pl.pallas_call(
        functools.partial(_out_ffn_body, f_chunk=f_chunk),
        grid=(bx, sx // tm),
        in_specs=[pl.BlockSpec((1, tm, d), row),
                  pl.BlockSpec((1, tm, o_a.shape[-1]), row), pl.BlockSpec((1, tm, o_b.shape[-1]), row),
                  _mod_spec(gate_m, tm), _const_spec(g), _mod_spec(shift, tm), _mod_spec(scale, tm),
                  _mod_spec(gate_f, tm), _const_spec(wo), _const_spec(w1), _const_spec(w3), _const_spec(w2)],
        out_specs=pl.BlockSpec((1, tm, d), row),
        out_shape=jax.ShapeDtypeStruct((bx, sx, d), F32),
        compiler_params=_cparams(("arbitrary", "arbitrary")),
        name="out_ffn",
    )(x, o_a, o_b, gate_m[0], g, shift[0], scale[0], gate_f[0], wo, w1, w3, w2)


def _store_tile_rows(ref, x):
    rows, d = x.shape
    nseg = d // LANES
    for s in range(nseg):
        ref[pl.ds(s, rows, stride=nseg), :] = x[:, s * LANES:(s + 1) * LANES]


def _load_tile_rows(ref, rows, d):
    nseg = d // LANES
    return jnp.concatenate([ref[pl.ds(s, rows, stride=nseg), :] for s in range(nseg)], axis=-1)


def _out_router_body(x_ref, o_ref, gm_ref, g_ref, sh_ref, sc_ref, wo_ref, wr_ref, h_in_ref,
                     x1_ref, h_ref, route_ref, *, n_exp):
    del h_in_ref
    x1 = x_ref[0] + gm_ref[0, 0, 0] * _dot(o_ref[0], wo_ref[...])
    x1_ref[0] = x1
    h = _modulate(x1, g_ref[...], sh_ref[0, 0, 0], sc_ref[0, 0, 0])
    hb = h.astype(BF)
    _store_tile_rows(h_ref, hb.astype(F32))
    logits = _dot(hb, wr_ref[0]) + _dot(hb, wr_ref[1]) + _dot(hb, wr_ref[2])
    lt = logits.T[:ROUTE_ROWS, :]
    row = lax.broadcasted_iota(jnp.int32, lt.shape, 0)
    lg = jnp.where(row < n_exp, lt, NEG_INF)
    e = jnp.exp(lg - jnp.max(lg, axis=0, keepdims=True))
    probs = e / jnp.sum(e, axis=0, keepdims=True)
    p1 = jnp.max(probs, axis=0, keepdims=True)
    i1 = jnp.min(jnp.where(probs == p1, row, ROUTE_ROWS), axis=0, keepdims=True)
    rest = jnp.where(row == i1, -1.0, probs)
    p2 = jnp.max(rest, axis=0, keepdims=True)
    i2 = jnp.min(jnp.where(rest == p2, row, ROUTE_ROWS), axis=0, keepdims=True)
    tot = p1 + p2
    route_ref[0] = jnp.where(row == 0, i1.astype(F32),
                             jnp.where(row == 1, i2.astype(F32),
                                       jnp.where(row == 2, p1 / tot, jnp.where(row == 3, p2 / tot, 0.0))))


def _out_router_call(x, o, gate_m, g, shift, scale, wo, wr3, n_exp, tok_offset, h_all):
    bx, sx, d = x.shape
    tm = min(ROW_TILE, sx)
    nt = sx // tm
    nseg = d // LANES
    assert tok_offset % tm == 0 and n_exp <= ROUTE_ROWS
    row = lambda b, i: (b, i, 0)
    return pl.pallas_call(
        functools.partial(_out_router_body, n_exp=n_exp),
        grid=(bx, nt),
        in_specs=[pl.BlockSpec((1, tm, d), row), pl.BlockSpec((1, tm, o.shape[-1]), row),
                  _mod_spec(gate_m, tm), _const_spec(g), _mod_spec(shift, tm), _mod_spec(scale, tm),
                  _const_spec(wo), _const_spec(wr3), pl.BlockSpec(memory_space=pl.ANY)],
        out_specs=[pl.BlockSpec((1, tm, d), row),
                   pl.BlockSpec((tm * nseg, LANES), lambda b, i: (tok_offset // tm + b * nt + i, 0)),
                   pl.BlockSpec((1, ROUTE_ROWS, tm), lambda b, i: (b, 0, i))],
        out_shape=[jax.ShapeDtypeStruct((bx, sx, d), F32),
                   jax.ShapeDtypeStruct(h_all.shape, F32),
                   jax.ShapeDtypeStruct((bx, ROUTE_ROWS, sx), F32)],
        input_output_aliases={8: 1},
        compiler_params=_cparams(("arbitrary", "arbitrary")),
        name="out_router",
    )(x, o, gate_m[0], g, shift[0], scale[0], wo, wr3, h_all)


def _moe_ffn_body(te_ref, nv_ref, idx_ref, idx_next_ref, h_ref, w1_ref, w3_ref, w2_ref, y_ref, xbuf, xb16, acc, sem,
                  *, nseg, nf):
    t = pl.program_id(0)
    j = pl.program_id(1)
    nv = nv_ref[0]
    tm, d = xb16.shape
    slot = lax.rem(t, 2)

    def issue(rows_ref, s):
        def body(r, c):
            src = pl.multiple_of(rows_ref[0, 0, r], nseg)
            dst = pl.multiple_of(r * nseg, nseg)
            pltpu.make_async_copy(h_ref.at[pl.ds(src, nseg)], xbuf.at[s, pl.ds(dst, nseg)], sem.at[s]).start()
            return c
        lax.fori_loop(0, tm, body, 0, unroll=8)

    @pl.when((j == 0) & (t == 0))
    def _():
        issue(idx_ref, 0)

    @pl.when((j == 0) & (t + 1 < nv))
    def _():
        issue(idx_next_ref, 1 - slot)

    @pl.when((j == 0) & (t < nv))
    def _():
        pltpu.make_async_copy(h_ref.at[pl.ds(0, tm * nseg)], xbuf.at[slot], sem.at[slot]).wait()
        xb16[...] = _load_tile_rows(xbuf.at[slot], tm, d).astype(BF)

    @pl.when(t < nv)
    def _():
        xb = xb16[...]
        a = _dot(xb, w1_ref[0])
        g = _dot(xb, w3_ref[0])
        u = ((a * _sigmoid(a)) * g).astype(BF)
        part = _dot(u, w2_ref[0])

        if nf == 1:
            _store_tile_rows(y_ref, part)
        else:
            @pl.when(j == 0)
            def _():
                acc[...] = part

            @pl.when((j > 0) & (j < nf - 1))
            def _():
                acc[...] += part

            @pl.when(j == nf - 1)
            def _():
                _store_tile_rows(y_ref, acc[...] + part)

    @pl.when((t >= nv) & (j == nf - 1))
    def _():
        y_ref[...] = jnp.zeros(y_ref.shape, F32)


def _moe_ffn_call(h_all, row_tok, tile_e, n_valid, w1, w3, w2):
    n_rows = row_tok.shape[0]
    d = w1.shape[1]
    nseg = d // LANES
    f_total = w1.shape[2]
    tf = MOE_F_TILE if f_total % MOE_F_TILE == 0 else f_total
    nf = f_total // tf
    tm = MOE_ROW_TILE
    n_tiles = n_rows // tm
    idx3 = (row_tok * nseg).reshape(n_tiles, 1, tm)

    def fcol(t, j, te, nv):
        return jnp.where(t < nv[0], j, nf - 1)

    grid_spec = pltpu.PrefetchScalarGridSpec(
        num_scalar_prefetch=2,
        grid=(n_tiles, nf),
        in_specs=[pl.BlockSpec((1, 1, tm), lambda t, j, te, nv: (t, 0, 0), memory_space=pltpu.SMEM),
                  pl.BlockSpec((1, 1, tm), lambda t, j, te, nv: (jnp.minimum(t + 1, n_tiles - 1), 0, 0),
                               memory_space=pltpu.SMEM),
                  pl.BlockSpec(memory_space=pl.ANY),
                  pl.BlockSpec((1, d, tf), lambda t, j, te, nv: (te[t], 0, fcol(t, j, te, nv))),
                  pl.BlockSpec((1, d, tf), lambda t, j, te, nv: (te[t], 0, fcol(t, j, te, nv))),
                  pl.BlockSpec((1, tf, d), lambda t, j, te, nv: (te[t], fcol(t, j, te, nv), 0))],
        out_specs=pl.BlockSpec((tm * nseg, LANES), lambda t, j, te, nv: (t, 0)),
        scratch_shapes=[pltpu.VMEM((2, tm * nseg, LANES), F32), pltpu.VMEM((tm, d), BF), pltpu.VMEM((tm, d), F32),
                        pltpu.SemaphoreType.DMA((2,))],
    )
    return pl.pallas_call(
        functools.partial(_moe_ffn_body, nseg=nseg, nf=nf),
        grid_spec=grid_spec,
        out_shape=jax.ShapeDtypeStruct((n_rows * nseg, LANES), F32),
        compiler_params=_cparams(("arbitrary", "arbitrary")),
        name="moe_ffn",
    )(tile_e, n_valid, idx3, idx3, h_all, w1, w3, w2)


def _combine_body(d0_ref, d1_ref, x_ref, gf_ref, w_ref, g_ref, yb_ref, o_ref, buf, sem):
    t = pl.program_id(0)
    nt = pl.num_programs(0)
    _, tc, d = x_ref.shape
    nseg = d // LANES

    def row_copy(idx_ref, r, slot, k):
        src = pl.multiple_of(idx_ref[0, 0, 2 * r + k], nseg)
        dst = pl.multiple_of(r * nseg, nseg)
        return pltpu.make_async_copy(yb_ref.at[pl.ds(src, nseg)], buf.at[slot, k, pl.ds(dst, nseg)], sem.at[slot])

    def issue(idx_ref, slot):
        def body(r, c):
            row_copy(idx_ref, r, slot, 0).start()
            row_copy(idx_ref, r, slot, 1).start()
            return c
        lax.fori_loop(0, tc, body, 0, unroll=4)

    slot = lax.rem(t, 2)

    @pl.when(t == 0)
    def _():
        issue(d0_ref, 0)

    @pl.when(t + 1 < nt)
    def _():
        issue(d1_ref, 1 - slot)

    for k in range(TOP_K):
        pltpu.make_async_copy(yb_ref.at[pl.ds(0, tc * nseg)], buf.at[slot, k], sem.at[slot]).wait()
    w = jnp.concatenate([w_ref[0], jnp.zeros((LANES - ROUTE_ROWS, tc), F32)], axis=0).T
    f = (w[:, 2:3] * _load_tile_rows(buf.at[slot, 0], tc, d)
         + w[:, 3:4] * _load_tile_rows(buf.at[slot, 1], tc, d))
    y = x_ref[0] + gf_ref[0, 0, 0] * f
    y = y * lax.rsqrt(jnp.mean(y * y, axis=-1, keepdims=True) + RMS_EPS)
    o_ref[0] = y * g_ref[...]


def _combine_call(x1, gate_f, route, g_final, yb, dest):
    bx, sx, d = x1.shape
    nseg = d // LANES
    tc = min(COMBINE_TILE, sx)
    nps = sx // tc
    nt = bx * nps
    dest3 = (dest * nseg).reshape(nt, 1, 2 * tc)
    row = lambda t: (t // nps, t % nps, 0)
    gate_arr, gl, gk = gate_f
    if gate_arr.shape[3] == 1:
        gate_spec = pl.BlockSpec((1, 1, 1, 1, d), lambda t: (gl, gk, t // nps, 0, 0))
    else:
        gate_spec = pl.BlockSpec((1, 1, 1, tc, d), lambda t: (gl, gk, t // nps, t % nps, 0))
    return pl.pallas_call(
        _combine_body,
        grid=(nt,),
        in_specs=[pl.BlockSpec((1, 1, 2 * tc), lambda t: (t, 0, 0), memory_space=pltpu.SMEM),
                  pl.BlockSpec((1, 1, 2 * tc), lambda t: (jnp.minimum(t + 1, nt - 1), 0, 0),
                               memory_space=pltpu.SMEM),
                  pl.BlockSpec((1, tc, d), row), gate_spec,
                  pl.BlockSpec((1, ROUTE_ROWS, tc), lambda t: (t // nps, 0, t % nps)),
                  pl.BlockSpec(g_final.shape, lambda t: (0, 0)),
                  pl.BlockSpec(memory_space=pl.ANY)],
        out_specs=pl.BlockSpec((1, tc, d), row),
        out_shape=jax.ShapeDtypeStruct((bx, sx, d), F32),
        scratch_shapes=[pltpu.VMEM((2, 2, tc * nseg, LANES), F32), pltpu.SemaphoreType.DMA((2,))],
        compiler_params=_cparams(("arbitrary",)),
        name="moe_combine",
    )(dest3, dest3, x1, gate_arr, route, g_final, yb)


def _route_plan(slot_e, n_exp, tm):
    n_slots = slot_e.shape[0]
    onehot = (slot_e[:, None] == jnp.arange(n_exp, dtype=jnp.int32)[None, :]).astype(jnp.int32)
    csum = jnp.cumsum(onehot, axis=0)
    rank = jnp.sum(csum * onehot, axis=1) - 1
    counts = csum[-1]
    padded = ((counts + tm - 1) // tm) * tm
    pad_end = jnp.cumsum(padded)
    pad_start = pad_end - padded
    dest = jnp.sum(onehot * pad_start[None, :], axis=1) + rank
    n_tiles = -(-n_slots // tm) + n_exp
    n_rows = n_tiles * tm
    n_valid = (pad_end[-1] // tm).astype(jnp.int32)
    tile_start = jnp.arange(n_tiles, dtype=jnp.int32) * tm
    tile_e = jnp.minimum(jnp.searchsorted(pad_end, tile_start, side='right'), n_exp - 1).astype(jnp.int32)
    last_e = tile_e[jnp.maximum(n_valid - 1, 0)]
    tile_e = jnp.where(jnp.arange(n_tiles) < n_valid, tile_e, last_e)
    sorted_tok = (jnp.argsort(slot_e, stable=True) // TOP_K).astype(jnp.int32)
    sorted_tok = jnp.concatenate([sorted_tok, jnp.zeros((tm,), jnp.int32)])
    first = jnp.cumsum(counts) - counts - pad_start
    tile_first = jnp.minimum(tile_start + first[tile_e], n_slots)
    row_tok = sorted_tok[(tile_first[:, None] + jnp.arange(tm, dtype=jnp.int32)[None, :]).reshape(n_rows)]
    return dest, row_tok, tile_e, n_valid.reshape(1)


def _moe(parts, h_all, g_final, w1, w3, w2):
    n_exp = w1.shape[0]
    slot_e = jnp.concatenate([r[:, :TOP_K, :].transpose(0, 2, 1).reshape(-1) for _, r, _ in parts], axis=0)
    dest, row_tok, tile_e, n_valid = _route_plan(slot_e.astype(jnp.int32), n_exp, MOE_ROW_TILE)
    yb = _moe_ffn_call(h_all, row_tok, tile_e, n_valid, w1, w3, w2)
    outs = []
    first = 0
    for x1, route, gate_f in parts:
        n_slots = x1.shape[0] * x1.shape[1] * TOP_K
        outs.append(_combine_call(x1, gate_f, route, g_final, yb, dest[first:first + n_slots]))
        first += n_slots
    return outs


def _trunk(x, mods, p, cache, tok_offset, h_all):
    sh_m0, sc_m0, gt_m0, sh_f0, sc_f0, gt_f0 = mods[0]
    sh_m1, sc_m1, gt_m1, sh_f1, sc_f1, gt_f1 = mods[1]
    wa = p['wa']
    wb = p['wb']
    wc = p['wc']
    bx, sx, d = x.shape

    if cache is None:
        keep = min(WINDOW_B, sx)
        qk, vt, ka, va, kb, vb, logf = _inproj_even_call(
            x, p['g_mix0'], sh_m0, sc_m0, p['w_in_even'], p['w_f'], p['b_f'], keep, True)
        o_a = _fox_prompt_call(qk, vt, wa)
        o_b = _band_prompt_call(qk, vt, p['rel_bias'], wa, wb)
        x_l0_state = (ka, va, logf, kb, vb)
    else:
        ck, cv, clogf, bk, bv, cck, ccv, b, t = cache
        qkv, ka, va, kb, vb, logf, lft, = _inproj_even_call(
            x, p['g_mix0'], sh_m0, sc_m0, p['w_in_even'], p['w_f'], p['b_f'], sx, False)
        nh = wa // HEAD_DIM
        p_len = ck.shape[1]
        lft_new = lft.reshape(nh, b, t).transpose(1, 0, 2)
        lp = -(-(p_len + t) // 512) * 512
        lcat = jnp.concatenate([clogf.transpose(0, 2, 1), lft_new,
                                jnp.zeros((b, nh, lp - p_len - t), F32)], axis=-1)
        ft_pad = _cumsum_call(lcat)
        qkv_b = qkv.reshape(b, t, qkv.shape[-1])
        o_a = _fox_sample_call(qkv_b, ck, cv, ft_pad, wa).reshape(1, b * t, wa)
        o_b = _band_sample_call(qkv_b, bk, bv, p['rel_bias'], wb).reshape(1, b * t, wb)
        x_l0_state = (ka, va, logf, kb, vb)
    x = _out_ffn_call(x, o_a, o_b, gt_m0, p['g_ffn0'], sh_f0, sc_f0, gt_f0,
                      p['w_out_even'], p['w1_dense'], p['w3_dense'], p['w2_dense'])

    if cache is None:
        qk, vt, kc, vc = _inproj_odd_call(x, p['g_mix1'], sh_m1, sc_m1, p['w_in_odd'], True)
        o = _diff_prompt_call(qk, vt, p['lam_par'], p['g_sub'].reshape(-1, 1), wc, p['lam_init'])
    else:
        qkv, kc, vc = _inproj_odd_call(x, p['g_mix1'], sh_m1, sc_m1, p['w_in_odd'], False)
        qkv_b = qkv.reshape(b, t, qkv.shape[-1])
        o = _diff_sample_call(qkv_b, cck, ccv, p['lam_par'], p['g_sub'], wc, p['lam_init']).reshape(1, b * t, wc)
    x1, h_all, route = _out_router_call(x, o, gt_m1, p['g_ffn1'], sh_f1, sc_f1, p['w_out_odd'], p['w_router3'],
                                        p['n_exp'], tok_offset, h_all)
    return (x1, route, gt_f1), h_all, x_l0_state, (kc, vc)


def kernel(x_prompt, x_sample, cache_a_k, cache_a_v, cache_a_logf, cache_b_k, cache_b_v, cache_c_k, cache_c_v,
           c_prompt, c_sample, w_mod, b_mod, g_mix, g_ffn, g_final, w_in_even, b_forget, rel_bias, w_out_even,
           w_in_odd, lambda_q1, lambda_k1, lambda_q2, lambda_k2, g_subln, w_out_odd, w1_dense, w3_dense, w2_dense,
           w_router, w1_moe, w3_moe, w2_moe):
    bp, sp, d = x_prompt.shape
    bs, ts, _ = x_sample.shape
    h_a = cache_a_k.shape[3]
    h_b = cache_b_k.shape[3]
    h_c = cache_c_k.shape[3]
    wa, wb, wc = h_a * HEAD_DIM, h_b * HEAD_DIM, h_c * 2 * HEAD_DIM
    n_exp = w_router.shape[-1]
    assert w_mod.shape[0] == 2, "kernel is written for the 2-layer trunk"

    we = w_in_even[0]
    w_main = jnp.concatenate([we[:, :3 * wa], we[:, 3 * wa + h_a:]], axis=1).astype(BF)
    w_f = jnp.pad(we[:, 3 * wa:3 * wa + h_a], ((0, 0), (0, LANES - h_a))).astype(BF)
    b_f = jnp.pad(b_forget[0], (0, LANES - h_a)).reshape(1, LANES).astype(F32)
    wr = jnp.pad(w_router[0], ((0, 0), (0, LANES - n_exp)))
    wr_hi = wr.astype(BF)
    wr_r1 = wr - wr_hi.astype(F32)
    wr_mid = wr_r1.astype(BF)
    wr_lo = (wr_r1 - wr_mid.astype(F32)).astype(BF)
    params = dict(
        wa=wa, wb=wb, wc=wc, n_exp=n_exp, lam_init=0.8 - 0.6 * math.exp(-0.3 * 1),
        g_mix0=g_mix[0:1], g_mix1=g_mix[1:2], g_ffn0=g_ffn[0:1], g_ffn1=g_ffn[1:2],
        g_final=g_final.reshape(1, d),
        w_in_even=w_main, w_f=w_f, b_f=b_f, rel_bias=rel_bias[0],
        w_out_even=w_out_even[0].astype(BF), w_in_odd=w_in_odd[0].astype(BF),
        lam_par=jnp.concatenate([lambda_q1, lambda_k1, lambda_q2, lambda_k2], axis=0),
        g_sub=g_subln[0:1], w_out_odd=w_out_odd[0].astype(BF),
        w1_dense=w1_dense[0].astype(BF), w3_dense=w3_dense[0].astype(BF), w2_dense=w2_dense[0].astype(BF),
        w_router3=jnp.stack([wr_hi, wr_mid, wr_lo]),
        w1_moe=w1_moe[0].astype(BF), w3_moe=w3_moe[0].astype(BF), w2_moe=w2_moe[0].astype(BF),
    )

    mod = _mod_call(jnp.concatenate([c_prompt, c_sample], axis=0), w_mod, b_mod)

    def mods_for(rows, per_row_len):
        m = mod[:, rows].reshape(2, -1, 6, d).transpose(0, 2, 1, 3)
        if per_row_len:
            m = jnp.repeat(m, per_row_len, axis=2)[:, :, None]
        else:
            m = m[:, :, :, None]
        return [[(m, l, k) for k in range(6)] for l in range(2)]

    h_all = jnp.zeros(((bp * sp + bs * ts) * (d // LANES), LANES), F32)
    moe_p, h_all, ev_p, od_p = _trunk(x_prompt, mods_for(slice(0, bp), 0), params, None, 0, h_all)
    cache = (cache_a_k[0].reshape(bs, -1, wa), cache_a_v[0].reshape(bs, -1, wa), cache_a_logf[0],
             cache_b_k[0].reshape(bs, -1, wb), cache_b_v[0].reshape(bs, -1, wb),
             cache_c_k[0].reshape(bs, -1, 2 * HEAD_DIM), cache_c_v[0].reshape(bs, -1, 2 * HEAD_DIM), bs, ts)
    moe_s, h_all, ev_s, od_s = _trunk(x_sample.reshape(1, bs * ts, d), mods_for(slice(bp, bp + bs), ts), params,
                                      cache, bp * sp, h_all)
    y_p, y_s = _moe([moe_p, moe_s], h_all, params['g_final'], params['w1_moe'], params['w3_moe'], params['w2_moe'])

    ka, va, logf, kb, vb = ev_p
    keep = kb.shape[1]
    out_p = (ka.reshape(1, bp, sp, h_a, HEAD_DIM), va.reshape(1, bp, sp, h_a, HEAD_DIM),
             logf.reshape(1, bp, sp, h_a),
             kb.reshape(1, bp, keep, h_b, HEAD_DIM), vb.reshape(1, bp, keep, h_b, HEAD_DIM),
             od_p[0].reshape(1, bp, sp, h_c, 2 * HEAD_DIM), od_p[1].reshape(1, bp, sp, h_c, 2 * HEAD_DIM))
    ka, va, logf, kb, vb = ev_s
    new_bk = jnp.concatenate([cache_b_k[0], kb.reshape(bs, ts, h_b, HEAD_DIM)], axis=1)[:, ts:]
    new_bv = jnp.concatenate([cache_b_v[0], vb.reshape(bs, ts, h_b, HEAD_DIM)], axis=1)[:, ts:]
    out_s = (ka.reshape(1, bs, ts, h_a, HEAD_DIM), va.reshape(1, bs, ts, h_a, HEAD_DIM),
             logf.reshape(1, bs, ts, h_a), new_bk[None], new_bv[None],
             od_s[0].reshape(1, bs, ts, h_c, 2 * HEAD_DIM), od_s[1].reshape(1, bs, ts, h_c, 2 * HEAD_DIM))
    return (y_p, y_s.reshape(bs, ts, d)) + out_p + out_s
```

```python
import functools
import math

import jax
import jax.numpy as jnp
import numpy as np
from jax import lax
from jax.experimental import pallas as pl
from jax.experimental.pallas import tpu as pltpu

BF = jnp.bfloat16
F32 = jnp.float32

CHUNK = 64
CHUNK_SHIFT = 6
HEAD_DIM = 64
BAND_CHUNKS = 8
WINDOW_B = BAND_CHUNKS * CHUNK
MAX_REL = 128
TOP_K = 2
RMS_EPS = 1e-6
NEG_INF = -1e30
QK_SCALE = HEAD_DIM ** -0.5
LOG2E = math.log2(math.e)
PROMPT_Q_SCALE = QK_SCALE * LOG2E


def _bf16_terms(c, n):
    terms = []
    for _ in range(n):
        t = float(np.asarray(c, dtype=jnp.bfloat16))
        terms.append(t)
        c -= t
    return terms


LOG2E_TERMS = _bf16_terms(LOG2E, 3)

LANES = 128
SUBLANES = 8
VMEM_LIMIT_BYTES = 56 * 1024 * 1024
ROUTE_ROWS = SUBLANES

ROW_TILE = 512
ATTN_TILE = 512
BAND_Q_TILE = WINDOW_B // 2
CACHE_TILE_A = 1024
CACHE_TILE_C = 512
MOE_ROW_TILE = 512
MOE_F_TILE = 1792
COMBINE_TILE = 256
CAST_ROWS = 512


def _cparams(sem):
    return pltpu.CompilerParams(dimension_semantics=sem, vmem_limit_bytes=VMEM_LIMIT_BYTES)


def _dot(a, b):
    return jnp.dot(a, b, preferred_element_type=F32)


def _dot_nt(a, b):
    return lax.dot_general(a, b, (((1,), (1,)), ((), ())), preferred_element_type=F32)


def _sigmoid(x):
    return 1.0 / (1.0 + jnp.exp(-x))


def _modulate(x, g, shift, scale):
    y = x * lax.rsqrt(jnp.mean(x * x, axis=-1, keepdims=True) + RMS_EPS)
    return (y * g) * (1.0 + scale) + shift


def _split3(x):
    hi = x.astype(BF)
    r1 = x - hi.astype(F32)
    mid = r1.astype(BF)
    lo = (r1 - mid.astype(F32)).astype(BF)
    return hi, mid, lo


def _cumsum_lanes(x, upper):
    hi, mid, lo = _split3(x)
    return _dot(hi, upper) + _dot(mid, upper) + _dot(lo, upper)


def _upper_tri(t):
    r = jnp.arange(t)
    return (r[:, None] <= r[None, :]).astype(BF)


def _osm_update(s, v, m_ref, l_ref, acc_ref, idx):
    m_prev = m_ref[idx]
    m_new = jnp.maximum(m_prev, jnp.max(s, axis=-1, keepdims=True))
    alpha = jnp.exp(m_prev - m_new)
    p = jnp.exp(s - m_new)
    l_ref[idx] = alpha * l_ref[idx] + jnp.sum(p, axis=-1, keepdims=True)
    acc_ref[idx] = alpha * acc_ref[idx] + _dot(p.astype(BF), v)
    m_ref[idx] = m_new


def _osm_init(m_ref, l_ref, acc_ref):
    m_ref[...] = jnp.full(m_ref.shape, NEG_INF, F32)
    l_ref[...] = jnp.zeros(l_ref.shape, F32)
    acc_ref[...] = jnp.zeros(acc_ref.shape, F32)


def _cast_body(x_ref, o_ref):
    o_ref[...] = x_ref[...].astype(BF)


def _cast_bf16_call(w):
    e, k, n = w.shape
    tk = CAST_ROWS if k % CAST_ROWS == 0 else k
    spec = pl.BlockSpec((1, tk, n), lambda e, i: (e, i, 0))
    return pl.pallas_call(
        _cast_body, grid=(e, k // tk), in_specs=[spec], out_specs=spec,
        out_shape=jax.ShapeDtypeStruct(w.shape, BF),
        compiler_params=_cparams(("arbitrary", "arbitrary")),
        name="cast_bf16",
    )(w)


def _mod_body(c_ref, w_ref, b_ref, o_ref):
    c = c_ref[...]
    s = (c * _sigmoid(c)).astype(BF)
    o_ref[0] = _dot(s, w_ref[0].astype(BF)) + b_ref[0]


def _mod_call(c_all, w_mod, b_mod):
    depth, d, n = w_mod.shape
    r = c_all.shape[0]
    tn = 1536 if n % 1536 == 0 else n
    return pl.pallas_call(
        _mod_body,
        grid=(depth, n // tn),
        in_specs=[pl.BlockSpec((r, d), lambda l, j: (0, 0)),
                  pl.BlockSpec((1, d, tn), lambda l, j: (l, 0, j)),
                  pl.BlockSpec((1, 1, tn), lambda l, j: (l, 0, j))],
        out_specs=pl.BlockSpec((1, r, tn), lambda l, j: (l, 0, j)),
        out_shape=jax.ShapeDtypeStruct((depth, r, n), F32),
        compiler_params=_cparams(("arbitrary", "arbitrary")),
        name="mod",
    )(c_all, w_mod, b_mod.reshape(depth, 1, n))


def _mod_spec(m, tm):
    arr, l, k = m
    d = arr.shape[-1]
    if arr.shape[3] == 1:
        return pl.BlockSpec((1, 1, 1, 1, d), lambda b, i: (l, k, b, 0, 0))
    return pl.BlockSpec((1, 1, 1, tm, d), lambda b, i: (l, k, b, i, 0))


def _const_spec(arr):
    nd = arr.ndim
    return pl.BlockSpec(arr.shape, lambda b, i: (0,) * nd, pipeline_mode=pl.Buffered(1))


def _head_blocks(x, augs):
    tm, w = x.shape
    first = lax.broadcasted_iota(jnp.int32, (tm, LANES), 1) < HEAD_DIM
    out = []
    for p in range(w // LANES):
        pair = x[:, p * LANES:(p + 1) * LANES]
        out.append(jnp.where(first, pair, augs[2 * p]).astype(BF))
        out.append(jnp.where(first, pltpu.roll(pair, HEAD_DIM, 1), augs[2 * p + 1]).astype(BF))
    return out


def _lane_range_row(lo, hi, value):
    lane = lax.broadcasted_iota(jnp.int32, (1, LANES), 1)
    return jnp.where((lane >= lo) & (lane < hi), value, 0.0).astype(F32)


def _ones_rows(t):
    return jnp.where(lax.broadcasted_iota(jnp.int32, (16, t), 0) == 0, 1.0, 0.0).astype(BF)


def _log_forget(h, wf_ref, bf_ref):
    z = _dot(h, wf_ref[...]) + bf_ref[...]
    return jnp.minimum(z, 0.0) - jnp.log1p(jnp.exp(-jnp.abs(z)))


def _inproj_even_body(*refs, wa, wb, prompt):
    if prompt:
        (x_ref, g_ref, sh_ref, sc_ref, w_ref, wf_ref, bf_ref, low_ref, place_ref,
         qk_ref, vt_ref, ka_ref, va_ref, kb_ref, vb_ref, logf_ref, carry_ref) = refs
    else:
        (x_ref, g_ref, sh_ref, sc_ref, w_ref, wf_ref, bf_ref,
         qkv_ref, ka_ref, va_ref, kb_ref, vb_ref, logf_ref, lft_ref) = refs
    i = pl.program_id(1)
    h = _modulate(x_ref[0], g_ref[...], sh_ref[0, 0, 0], sc_ref[0, 0, 0]).astype(BF)
    tm = h.shape[0]
    nh = logf_ref.shape[-1]

    qa = _dot(h, w_ref[:, 0:wa])
    ka = _dot(h, w_ref[:, wa:2 * wa])
    va = _dot(h, w_ref[:, 2 * wa:3 * wa])
    o = 3 * wa
    qb = _dot(h, w_ref[:, o:o + wb])
    kb = _dot(h, w_ref[:, o + wb:o + 2 * wb])
    vb = _dot(h, w_ref[:, o + 2 * wb:o + 3 * wb])
    ka_ref[0] = ka
    va_ref[0] = va
    kb_ref[0] = kb
    vb_ref[0] = vb
    logf = _log_forget(h, wf_ref, bf_ref)
    logf_ref[0] = logf[:, :nh]

    if not prompt:
        qkv_ref[0] = jnp.concatenate(
            [(qa * QK_SCALE).astype(BF), ka.astype(BF), va.astype(BF),
             (qb * QK_SCALE).astype(BF), kb.astype(BF), vb.astype(BF)], axis=-1)
        lft_ref[0] = logf.T[:nh, :]
        return

    @pl.when(i == 0)
    def _():
        carry_ref[...] = jnp.zeros(carry_ref.shape, F32)

    low = low_ref[...]
    hi, mid, lo = _split3(logf)
    f_cum = _dot(low, hi) + _dot(low, mid) + _dot(low, lo) + carry_ref[0:1, :]
    carry_ref[...] = jnp.broadcast_to(f_cum[tm - 1:tm, :], carry_ref.shape)
    nhi, nmid, nlo = _split3(-f_cum)
    k_aug = _dot(nhi, place_ref[0]) + _dot(nmid, place_ref[1]) + _dot(nlo, place_ref[2])
    q_log2e = sum(_lane_range_row(HEAD_DIM + 3 * r, HEAD_DIM + 3 * r + 3, c) for r, c in enumerate(LOG2E_TERMS))
    zero = jnp.zeros((1, LANES), F32)
    nhb = wb // HEAD_DIM
    blocks = (_head_blocks(qa * PROMPT_Q_SCALE, [q_log2e] * nh)
              + _head_blocks(ka, [k_aug[:, n * LANES:(n + 1) * LANES] for n in range(nh)])
              + _head_blocks(qb * PROMPT_Q_SCALE, [zero] * nhb)
              + _head_blocks(kb, [zero] * nhb))
    qk_ref[0] = jnp.concatenate(blocks, axis=-1)
    ones = _ones_rows(tm)
    vt_ref[0] = jnp.concatenate([va.T.astype(BF), ones, vb.T.astype(BF), ones], axis=0)


def _lower_tri(t):
    r = jnp.arange(t)
    return (r[:, None] >= r[None, :]).astype(BF)


def _place_matrices(nh):
    s = jnp.arange(3)[:, None, None]
    r = jnp.arange(LANES)[None, :, None]
    c = jnp.arange(nh * LANES)[None, None, :]
    off = c - r * LANES - HEAD_DIM - s
    return ((r < nh) & ((off == 0) | (off == 3) | (off == 6))).astype(BF)


def _inproj_even_call(x, g, shift, scale, w_main, w_f, b_f, keep, prompt):
    bx, sx, d = x.shape
    tm = min(ROW_TILE, sx)
    nt = sx // tm
    n = w_main.shape[1]
    wa = wb = n // 6
    nh = wa // HEAD_DIM
    nkeep = keep // tm
    tail = lambda b, i: (b, jnp.maximum(i - (nt - nkeep), 0), 0)
    row = lambda b, i: (b, i, 0)
    in_specs = [pl.BlockSpec((1, tm, d), row), _const_spec(g), _mod_spec(shift, tm), _mod_spec(scale, tm),
                _const_spec(w_main), _const_spec(w_f), _const_spec(b_f)]
    args = [x, g, shift[0], scale[0], w_main, w_f, b_f]
    f32_specs = [pl.BlockSpec((1, tm, wa), row), pl.BlockSpec((1, tm, wa), row),
                 pl.BlockSpec((1, tm, wb), tail), pl.BlockSpec((1, tm, wb), tail),
                 pl.BlockSpec((1, tm, nh), row)]
    f32_shapes = [jax.ShapeDtypeStruct((bx, sx, wa), F32), jax.ShapeDtypeStruct((bx, sx, wa), F32),
                  jax.ShapeDtypeStruct((bx, keep, wb), F32), jax.ShapeDtypeStruct((bx, keep, wb), F32),
                  jax.ShapeDtypeStruct((bx, sx, nh), F32)]
    scratch = []
    if prompt:
        low, place = _lower_tri(tm), _place_matrices(nh)
        in_specs += [_const_spec(low), _const_spec(place)]
        args += [low, place]
        nqk = 2 * (wa + wb) // HEAD_DIM * LANES
        nvt = wa + wb + 32
        out_specs = [pl.BlockSpec((1, tm, nqk), row), pl.BlockSpec((1, nvt, tm), lambda b, i: (b, 0, i))] + f32_specs
        out_shape = [jax.ShapeDtypeStruct((bx, sx, nqk), BF), jax.ShapeDtypeStruct((bx, nvt, sx), BF)] + f32_shapes
        scratch.append(pltpu.VMEM((8, LANES), F32))
    else:
        out_specs = [pl.BlockSpec((1, tm, n), row)] + f32_specs + [pl.BlockSpec((1, nh, tm), lambda b, i: (b, 0, i))]
        out_shape = [jax.ShapeDtypeStruct((bx, sx, n), BF)] + f32_shapes + [jax.ShapeDtypeStruct((bx, nh, sx), F32)]
    return pl.pallas_call(
        functools.partial(_inproj_even_body, wa=wa, wb=wb, prompt=prompt),
        grid=(bx, nt), in_specs=in_specs, out_specs=out_specs, out_shape=out_shape,
        scratch_shapes=scratch,
        compiler_params=_cparams(("arbitrary", "arbitrary")),
        name="inproj_even",
    )(*args)


def _inproj_odd_body(*refs, wc, prompt):
    if prompt:
        x_ref, g_ref, sh_ref, sc_ref, w_ref, qk_ref, vt_ref, k_ref, v_ref = refs
    else:
        x_ref, g_ref, sh_ref, sc_ref, w_ref, qkv_ref, k_ref, v_ref = refs
    i = pl.program_id(1)
    h = _modulate(x_ref[0], g_ref[...], sh_ref[0, 0, 0], sc_ref[0, 0, 0]).astype(BF)
    tm = h.shape[0]
    q = _dot(h, w_ref[:, 0:wc])
    k = _dot(h, w_ref[:, wc:2 * wc])
    v = _dot(h, w_ref[:, 2 * wc:3 * wc])
    k_ref[0] = k
    v_ref[0] = v
    if not prompt:
        qkv_ref[0] = jnp.concatenate([(q * QK_SCALE).astype(BF), k.astype(BF), v.astype(BF)], axis=-1)
        return
    nh = wc // LANES
    pos = (i * tm + lax.broadcasted_iota(jnp.int32, (tm, LANES), 0)).astype(F32)
    pos_hi = pos.astype(BF).astype(F32)
    lane = lax.broadcasted_iota(jnp.int32, (tm, LANES), 1) - HEAD_DIM
    in_aug = (lane >= 0) & (lane < 2 * len(LOG2E_TERMS))
    k_aug = jnp.where(in_aug, jnp.where((lane & 1) == 0, pos_hi, pos - pos_hi), 0.0)
    q_augs = []
    for hh in range(nh):
        slope = _alibi_slope(hh, nh)
        row = sum(_lane_range_row(HEAD_DIM + 2 * r, HEAD_DIM + 2 * r + 2, slope * c) for r, c in enumerate(LOG2E_TERMS))
        q_augs += [row] * 2
    blocks = _head_blocks(q * PROMPT_Q_SCALE, q_augs) + _head_blocks(k, [k_aug] * (2 * nh))
    qk_ref[0] = jnp.concatenate(blocks, axis=-1)
    vt_ref[0] = jnp.concatenate([v.T.astype(BF), _ones_rows(tm)], axis=0)


def _alibi_slope(h, nh):
    assert 8 % nh == 0, "slopes must be exact powers of two to ride in bf16"
    return 2.0 ** (-8.0 * (h + 1) / nh)


def _inproj_odd_call(x, g, shift, scale, w, prompt):
    bx, sx, d = x.shape
    tm = min(ROW_TILE, sx)
    n = w.shape[1]
    wc = n // 3
    row = lambda b, i: (b, i, 0)
    f32_specs = [pl.BlockSpec((1, tm, wc), row), pl.BlockSpec((1, tm, wc), row)]
    f32_shapes = [jax.ShapeDtypeStruct((bx, sx, wc), F32), jax.ShapeDtypeStruct((bx, sx, wc), F32)]
    if prompt:
        nqk = 2 * wc // HEAD_DIM * LANES
        out_specs = [pl.BlockSpec((1, tm, nqk), row), pl.BlockSpec((1, wc + 16, tm), lambda b, i: (b, 0, i))]
        out_shape = [jax.ShapeDtypeStruct((bx, sx, nqk), BF), jax.ShapeDtypeStruct((bx, wc + 16, sx), BF)]
    else:
        out_specs = [pl.BlockSpec((1, tm, n), row)]
        out_shape = [jax.ShapeDtypeStruct((bx, sx, n), BF)]
    return pl.pallas_call(
        functools.partial(_inproj_odd_body, wc=wc, prompt=prompt),
        grid=(bx, sx // tm),
        in_specs=[pl.BlockSpec((1, tm, d), row), _const_spec(g), _mod_spec(shift, tm), _mod_spec(scale, tm),
                  _const_spec(w)],
        out_specs=out_specs + f32_specs, out_shape=out_shape + f32_shapes,
        compiler_params=_cparams(("arbitrary", "arbitrary")),
        name="inproj_odd",
    )(x, g, shift[0], scale[0], w)


def _tflash_update(st, v_aug, mask, extra, m_ref, acc_ref, idx):
    if extra is not None:
        st = st - extra
    if mask is not None:
        st = jnp.where(mask, st, NEG_INF)
    m_prev = m_ref[idx]
    m_new = jnp.maximum(m_prev, jnp.max(st, axis=0, keepdims=True))
    alpha = jnp.exp2(m_prev - m_new)
    p = jnp.exp2(st - m_new).astype(BF)
    acc_ref[idx] = alpha * acc_ref[idx] + _dot(v_aug, p)
    m_ref[idx] = m_new


def _tflash_init(m_ref, acc_ref):
    m_ref[...] = jnp.full(m_ref.shape, NEG_INF, F32)
    acc_ref[...] = jnp.zeros(acc_ref.shape, F32)


def _causal_pairs(nq):
    pairs = [(i, j) for i in range(nq) for j in range(i + 1)]
    return jnp.asarray([p[0] for p in pairs], jnp.int32), jnp.asarray([p[1] for p in pairs], jnp.int32)


def _fox_prompt_body(it_ref, jt_ref, q_ref, k_ref, vt_ref, o_ref, m_ref, acc_ref, st_ref, *, nh):
    i = it_ref[pl.program_id(1)]
    j = jt_ref[pl.program_id(1)]
    hd = HEAD_DIM

    @pl.when(j == 0)
    def _():
        _tflash_init(m_ref, acc_ref)

    def step(masked):
        q = q_ref[0]
        k = k_ref[0]
        vt = vt_ref[0]
        tq, tk = q.shape[0], k.shape[0]
        ones = vt[nh * hd:, :]
        mask = None
        if masked:
            mask = lax.broadcasted_iota(jnp.int32, (tk, tq), 0) <= lax.broadcasted_iota(jnp.int32, (tk, tq), 1)
        def scores(h):
            st_ref[h % 2] = _dot_nt(k[:, h * LANES:(h + 1) * LANES], q[:, h * LANES:(h + 1) * LANES])

        scores(0)
        for h in range(nh):
            if h + 1 < nh:
                scores(h + 1)
            v_aug = jnp.concatenate([vt[h * hd:(h + 1) * hd, :], ones], axis=0)
            _tflash_update(st_ref[h % 2], v_aug, mask, None, m_ref, acc_ref, h)

    @pl.when(j < i)
    def _():
        step(False)

    @pl.when(j == i)
    def _():
        step(True)
        ot = jnp.concatenate([acc_ref[h][:hd] / acc_ref[h][hd:hd + 1] for h in range(nh)], axis=0)
        o_ref[0] = ot.T.astype(BF)


def _fox_prompt_call(qk, vt, wa):
    b, s, _ = qk.shape
    nh = wa // HEAD_DIM
    t = min(ATTN_TILE, s)
    it, jt = _causal_pairs(s // t)
    wq = nh * LANES
    grid_spec = pltpu.PrefetchScalarGridSpec(
        num_scalar_prefetch=2,
        grid=(b, it.shape[0]),
        in_specs=[pl.BlockSpec((1, t, wq), lambda b, s, it, jt: (b, it[s], 0)),
                  pl.BlockSpec((1, t, wq), lambda b, s, it, jt: (b, jt[s], 1)),
                  pl.BlockSpec((1, wa + 16, t), lambda b, s, it, jt: (b, 0, jt[s]))],
        out_specs=pl.BlockSpec((1, t, wa), lambda b, s, it, jt: (b, it[s], 0)),
        scratch_shapes=[pltpu.VMEM((nh, 1, t), F32), pltpu.VMEM((nh, HEAD_DIM + 16, t), F32),
                        pltpu.VMEM((2, t, t), F32)],
    )
    return pl.pallas_call(
        functools.partial(_fox_prompt_body, nh=nh),
        grid_spec=grid_spec,
        out_shape=jax.ShapeDtypeStruct((b, s, wa), BF),
        compiler_params=_cparams(("arbitrary", "arbitrary")),
        name="fox_prompt",
    )(it, jt, qk, qk, vt)


def _band_prompt_body(q_ref, k0_ref, k1_ref, k2_ref, v0_ref, v1_ref, v2_ref, bias_ref, o_ref, st_ref, *, nh):
    i = pl.program_id(1)
    hd = HEAD_DIM
    q = q_ref[0]
    tq = q.shape[0]
    k = jnp.concatenate([k0_ref[0], k1_ref[0], k2_ref[0]], axis=0)
    vts = [v0_ref[0], v1_ref[0], v2_ref[0]]

    def scores(h):
        st_ref[h % 2] = _dot_nt(k[:, h * LANES:(h + 1) * LANES], q[:, h * LANES:(h + 1) * LANES])

    def attend(at_start):
        if at_start:
            in_seq = lax.broadcasted_iota(jnp.int32, (3 * tq, tq), 0) >= (2 - i) * tq
        outs = []
        scores(0)
        for h in range(nh):
            if h + 1 < nh:
                scores(h + 1)
            st = st_ref[h % 2] + bias_ref[h]
            if at_start:
                st = jnp.where(in_seq, st, NEG_INF)
            p = jnp.exp2(st - jnp.max(st, axis=0, keepdims=True)).astype(BF)
            acc = None
            for w, vt in enumerate(vts):
                v_aug = jnp.concatenate([vt[h * hd:(h + 1) * hd, :], vt[nh * hd:, :]], axis=0)
                part = _dot(v_aug, p[w * tq:(w + 1) * tq, :])
                acc = part if acc is None else acc + part
            outs.append(acc[:hd] / acc[hd:hd + 1])
        o_ref[0] = jnp.concatenate(outs, axis=0).T.astype(BF)

    @pl.when(i < 2)
    def _():
        attend(True)

    @pl.when(i >= 2)
    def _():
        attend(False)


def _band_bias_prompt(rel_bias, tq):
    nh = rel_bias.shape[0]
    period = 4 * tq
    t = jnp.arange(period)
    t = jnp.where(t < 3 * tq, t, t - period)
    u = rel_bias[:, jnp.clip(2 * tq - t, -MAX_REL, MAX_REL) + MAX_REL].astype(F32)
    skew = jnp.tile(u, (1, tq))[:, :tq * (period - 1)].reshape(nh, tq, period - 1)
    bias = skew[:, :, :3 * tq]
    r = jnp.arange(tq)[:, None]
    c = jnp.arange(3 * tq)[None, :]
    valid = (c // CHUNK >= r // CHUNK) & (c // CHUNK <= r // CHUNK + BAND_CHUNKS)
    return jnp.where(valid[None], bias, NEG_INF)


def _band_prompt_call(qk, vt, rel_bias, wa, wb):
    b, s, _ = qk.shape
    nh = wb // HEAD_DIM
    tq = BAND_Q_TILE
    assert s % tq == 0 and wa == wb
    bias_t = (_band_bias_prompt(rel_bias, tq) * LOG2E).transpose(0, 2, 1)
    wq = nh * LANES
    cq, ck = 2, 3

    def k_spec(back):
        return pl.BlockSpec((1, tq, wq), lambda b, i: (b, jnp.maximum(i - back, 0), ck))

    def v_spec(back):
        return pl.BlockSpec((1, wb + 16, tq), lambda b, i: (b, 1, jnp.maximum(i - back, 0)))

    return pl.pallas_call(
        functools.partial(_band_prompt_body, nh=nh),
        grid=(b, s // tq),
        in_specs=[pl.BlockSpec((1, tq, wq), lambda b, i: (b, i, cq)),
                  k_spec(2), k_spec(1), k_spec(0), v_spec(2), v_spec(1), v_spec(0),
                  _const_spec(bias_t)],
        out_specs=pl.BlockSpec((1, tq, wb), lambda b, i: (b, i, 0)),
        out_shape=jax.ShapeDtypeStruct((b, s, wb), BF),
        scratch_shapes=[pltpu.VMEM((2, 3 * tq, tq), F32)],
        compiler_params=_cparams(("arbitrary", "arbitrary")),
        name="band_prompt",
    )(qk, qk, qk, qk, vt, vt, vt, bias_t)


def _lambda(lam_ref, lam_init):
    lp = lam_ref[...]
    a = jnp.sum(lp[0:1] * lp[1:2], axis=-1, keepdims=True)
    b = jnp.sum(lp[2:3] * lp[3:4], axis=-1, keepdims=True)
    return jnp.exp(a) - jnp.exp(b) + lam_init


def _diff_prompt_body(it_ref, jt_ref, q_ref, k_ref, vt_ref, lam_ref, gcol_ref, o_ref, m_ref, acc_ref, st_ref,
                      *, nh, lam_init):
    i = it_ref[pl.program_id(1)]
    j = jt_ref[pl.program_id(1)]
    dv = 2 * HEAD_DIM

    @pl.when(j == 0)
    def _():
        _tflash_init(m_ref, acc_ref)

    def step(masked):
        q = q_ref[0]
        k = k_ref[0]
        vt = vt_ref[0]
        tq, tk = q.shape[0], k.shape[0]
        ones = vt[nh * dv:, :]
        if masked:
            key = lax.broadcasted_iota(jnp.int32, (tk, tq), 0)
            qry = lax.broadcasted_iota(jnp.int32, (tk, tq), 1)
            visible = (key >> CHUNK_SHIFT) <= (qry >> CHUNK_SHIFT)
            ahead = jnp.where(visible, 2.0 * jnp.maximum(key - qry, 0).astype(F32), -NEG_INF)
        def scores(n):
            st_ref[n % 2] = _dot_nt(k[:, n * LANES:(n + 1) * LANES], q[:, n * LANES:(n + 1) * LANES])

        scores(0)
        for h in range(nh):
            v_aug = jnp.concatenate([vt[h * dv:(h + 1) * dv, :], ones], axis=0)
            extra = (LOG2E * _alibi_slope(h, nh)) * ahead if masked else None
            for u in range(2):
                n = 2 * h + u
                if n + 1 < 2 * nh:
                    scores(n + 1)
                _tflash_update(st_ref[n % 2], v_aug, None, extra, m_ref, acc_ref, n)

    @pl.when(j < i)
    def _():
        step(False)

    @pl.when(j == i)
    def _():
        step(True)
        lam = _lambda(lam_ref, lam_init)
        g = gcol_ref[...]
        outs = []
        for h in range(nh):
            a1 = acc_ref[2 * h]
            a2 = acc_ref[2 * h + 1]
            o = a1[:dv] / a1[dv:dv + 1] - lam * (a2[:dv] / a2[dv:dv + 1])
            o = o * lax.rsqrt(jnp.mean(o * o, axis=0, keepdims=True) + RMS_EPS)
            outs.append((o * g) * (1.0 - lam_init))
        o_ref[0] = jnp.concatenate(outs, axis=0).T.astype(BF)


def _diff_prompt_call(qk, vt, lam_par, g_col, wc, lam_init):
    b, s, _ = qk.shape
    nh = wc // (2 * HEAD_DIM)
    t = min(ATTN_TILE, s)
    it, jt = _causal_pairs(s // t)
    wq = 2 * nh * LANES
    grid_spec = pltpu.PrefetchScalarGridSpec(
        num_scalar_prefetch=2,
        grid=(b, it.shape[0]),
        in_specs=[pl.BlockSpec((1, t, wq), lambda b, s, it, jt: (b, it[s], 0)),
                  pl.BlockSpec((1, t, wq), lambda b, s, it, jt: (b, jt[s], 1)),
                  pl.BlockSpec((1, wc + 16, t), lambda b, s, it, jt: (b, 0, jt[s])),
                  pl.BlockSpec(lam_par.shape, lambda b, s, it, jt: (0, 0)),
                  pl.BlockSpec(g_col.shape, lambda b, s, it, jt: (0, 0))],
        out_specs=pl.BlockSpec((1, t, wc), lambda b, s, it, jt: (b, it[s], 0)),
        scratch_shapes=[pltpu.VMEM((2 * nh, 1, t), F32), pltpu.VMEM((2 * nh, 2 * HEAD_DIM + 16, t), F32),
                        pltpu.VMEM((2, t, t), F32)],
    )
    return pl.pallas_call(
        functools.partial(_diff_prompt_body, nh=nh, lam_init=lam_init),
        grid_spec=grid_spec,
        out_shape=jax.ShapeDtypeStruct((b, s, wc), BF),
        compiler_params=_cparams(("arbitrary", "arbitrary")),
        name="diff_prompt",
    )(it, jt, qk, qk, vt, lam_par, g_col)


def _cumsum_body(x_ref, up_ref, o_ref, carry_ref):
    @pl.when(pl.program_id(1) == 0)
    def _():
        carry_ref[...] = jnp.zeros(carry_ref.shape, F32)
    ft = _cumsum_lanes(x_ref[0], up_ref[...]) + carry_ref[:, 0:1]
    o_ref[0] = ft
    carry_ref[...] = jnp.broadcast_to(ft[:, -1:], carry_ref.shape)


def _cumsum_call(x):
    b, nh, length = x.shape
    t = 512
    up = _upper_tri(t)
    return pl.pallas_call(
        _cumsum_body,
        grid=(b, length // t),
        in_specs=[pl.BlockSpec((1, nh, t), lambda b, j: (b, 0, j)), pl.BlockSpec((t, t), lambda b, j: (0, 0))],
        out_specs=pl.BlockSpec((1, nh, t), lambda b, j: (b, 0, j)),
        out_shape=jax.ShapeDtypeStruct(x.shape, F32),
        scratch_shapes=[pltpu.VMEM((nh, LANES), F32)],
        compiler_params=_cparams(("arbitrary", "arbitrary")),
        name="logf_cumsum",
    )(x, up)


def _fox_sample_body(q_ref, kn_ref, vn_ref, ck_ref, cv_ref, ftc_ref, ftn_ref, o_ref, m_ref, l_ref, acc_ref, *, nh):
    j = pl.program_id(1)
    hd = HEAD_DIM
    q = q_ref[0]
    t = q.shape[0]

    @pl.when(j == 0)
    def _():
        _osm_init(m_ref, l_ref, acc_ref)

    k = ck_ref[0].astype(BF)
    v = cv_ref[0].astype(BF)
    ft = ftc_ref[0]
    for h in range(nh):
        sl = slice(h * hd, (h + 1) * hd)
        s = _dot_nt(q[:, sl], k[:, sl]) - ft[h:h + 1, :]
        _osm_update(s, v[:, sl], m_ref, l_ref, acc_ref, h)

    @pl.when(j == pl.num_programs(1) - 1)
    def _():
        kn = kn_ref[0]
        vn = vn_ref[0]
        ftn = ftn_ref[0][:, :t]
        visible = lax.broadcasted_iota(jnp.int32, (t, t), 1) <= lax.broadcasted_iota(jnp.int32, (t, t), 0)
        outs = []
        for h in range(nh):
            sl = slice(h * hd, (h + 1) * hd)
            s = _dot_nt(q[:, sl], kn[:, sl]) - ftn[h:h + 1, :]
            s = jnp.where(visible, s, NEG_INF)
            _osm_update(s, vn[:, sl], m_ref, l_ref, acc_ref, h)
            outs.append((acc_ref[h] / l_ref[h]).astype(BF))
        o_ref[0] = jnp.concatenate(outs, axis=-1)


def _fox_sample_call(qkv, cache_k, cache_v, ft_pad, wa):
    b, t, _ = qkv.shape
    p_len = cache_k.shape[1]
    nh = wa // HEAD_DIM
    tk = min(CACHE_TILE_A, p_len)
    assert p_len % tk == 0 and p_len % LANES == 0 and t <= LANES
    return pl.pallas_call(
        functools.partial(_fox_sample_body, nh=nh),
        grid=(b, p_len // tk),
        in_specs=[pl.BlockSpec((1, t, wa), lambda b, j: (b, 0, 0)),
                  pl.BlockSpec((1, t, wa), lambda b, j: (b, 0, 1)),
                  pl.BlockSpec((1, t, wa), lambda b, j: (b, 0, 2)),
                  pl.BlockSpec((1, tk, wa), lambda b, j: (b, j, 0)),
                  pl.BlockSpec((1, tk, wa), lambda b, j: (b, j, 0)),
                  pl.BlockSpec((1, nh, tk), lambda b, j: (b, 0, j)),
                  pl.BlockSpec((1, nh, LANES), lambda b, j: (b, 0, p_len // LANES))],
        out_specs=pl.BlockSpec((1, t, wa), lambda b, j: (b, 0, 0)),
        out_shape=jax.ShapeDtypeStruct((b, t, wa), BF),
        scratch_shapes=[pltpu.VMEM((nh, t, 1), F32), pltpu.VMEM((nh, t, 1), F32),
                        pltpu.VMEM((nh, t, HEAD_DIM), F32)],
        compiler_params=_cparams(("arbitrary", "arbitrary")),
        name="fox_sample",
    )(qkv, qkv, qkv, cache_k, cache_v, ft_pad, ft_pad)


def _band_sample_body(q_ref, kn_ref, vn_ref, bk_ref, bv_ref, bias_b_ref, bias_n_ref, o_ref, *, nh):
    hd = HEAD_DIM
    q = q_ref[0]
    kn = kn_ref[0]
    vn = vn_ref[0]
    kb = bk_ref[0].astype(BF)
    vb = bv_ref[0].astype(BF)
    outs = []
    for h in range(nh):
        sl = slice(h * hd, (h + 1) * hd)
        sb = _dot_nt(q[:, sl], kb[:, sl]) + bias_b_ref[h]
        sn = _dot_nt(q[:, sl], kn[:, sl]) + bias_n_ref[h]
        m = jnp.maximum(jnp.max(sb, axis=-1, keepdims=True), jnp.max(sn, axis=-1, keepdims=True))
        pb = jnp.exp(sb - m)
        pn = jnp.exp(sn - m)
        l = jnp.sum(pb, axis=-1, keepdims=True) + jnp.sum(pn, axis=-1, keepdims=True)
        o = _dot(pb.astype(BF), vb[:, sl]) + _dot(pn.astype(BF), vn[:, sl])
        outs.append((o / l).astype(BF))
    o_ref[0] = jnp.concatenate(outs, axis=-1)


def _band_sample_call(qkv, buf_k, buf_v, rel_bias, wb):
    b, t, _ = qkv.shape
    lb = buf_k.shape[1]
    nh = wb // HEAD_DIM
    k_pos = jnp.concatenate([jnp.arange(-lb, 0), jnp.arange(t)])
    rel = jnp.arange(t)[:, None] - k_pos[None, :]
    bias = rel_bias[:, jnp.clip(rel, -MAX_REL, MAX_REL) + MAX_REL].astype(F32)
    bias_b, bias_n = bias[:, :, :lb], bias[:, :, lb:]
    return pl.pallas_call(
        functools.partial(_band_sample_body, nh=nh),
        grid=(b,),
        in_specs=[pl.BlockSpec((1, t, wb), lambda b: (b, 0, 3)),
                  pl.BlockSpec((1, t, wb), lambda b: (b, 0, 4)),
                  pl.BlockSpec((1, t, wb), lambda b: (b, 0, 5)),
                  pl.BlockSpec((1, lb, wb), lambda b: (b, 0, 0)),
                  pl.BlockSpec((1, lb, wb), lambda b: (b, 0, 0)),
                  pl.BlockSpec(bias_b.shape, lambda b: (0, 0, 0)),
                  pl.BlockSpec(bias_n.shape, lambda b: (0, 0, 0))],
        out_specs=pl.BlockSpec((1, t, wb), lambda b: (b, 0, 0)),
        out_shape=jax.ShapeDtypeStruct((b, t, wb), BF),
        compiler_params=_cparams(("arbitrary",)),
        name="band_sample",
    )(qkv, qkv, qkv, buf_k, buf_v, bias_b, bias_n)


def _diff_sample_body(q_ref, kn_ref, vn_ref, ck_ref, cv_ref, lam_ref, gsub_ref, o_ref, m_ref, l_ref, acc_ref,
                      *, nh, lam_init, p_len):
    j = pl.program_id(1)
    q = q_ref[0]
    t = q.shape[0]
    dv = 2 * HEAD_DIM
    lane = lax.broadcasted_iota(jnp.int32, (t, dv), 1)

    @pl.when(j == 0)
    def _():
        _osm_init(m_ref, l_ref, acc_ref)

    def attend(k_of, v_of, dist):
        dist2 = jnp.concatenate([dist, dist], axis=0)
        for h in range(nh):
            x = q[:, h * dv:(h + 1) * dv]
            zero = jnp.zeros_like(x)
            q2 = jnp.concatenate([jnp.where(lane < HEAD_DIM, x, zero), jnp.where(lane >= HEAD_DIM, x, zero)], axis=0)
            s = _dot_nt(q2, k_of(h)) - _alibi_slope(h, nh) * dist2
            _osm_update(s, v_of(h), m_ref, l_ref, acc_ref, h)

    tk = ck_ref.shape[1] // nh
    row = lax.broadcasted_iota(jnp.int32, (t, tk), 0)
    col = lax.broadcasted_iota(jnp.int32, (t, tk), 1)
    attend(lambda h: ck_ref[0, pl.ds(h, tk, stride=nh), :].astype(BF),
           lambda h: cv_ref[0, pl.ds(h, tk, stride=nh), :].astype(BF),
           (p_len + row - j * tk - col).astype(F32))

    @pl.when(j == pl.num_programs(1) - 1)
    def _():
        kn = kn_ref[0]
        vn = vn_ref[0]
        rown = lax.broadcasted_iota(jnp.int32, (t, t), 0)
        coln = lax.broadcasted_iota(jnp.int32, (t, t), 1)
        attend(lambda h: kn[:, h * dv:(h + 1) * dv], lambda h: vn[:, h * dv:(h + 1) * dv],
               jnp.abs(rown - coln).astype(F32))
        lam = _lambda(lam_ref, lam_init)
        g = gsub_ref[...]
        outs = []
        for h in range(nh):
            on = acc_ref[h] / l_ref[h]
            o = on[:t] - lam * on[t:]
            o = o * lax.rsqrt(jnp.mean(o * o, axis=-1, keepdims=True) + RMS_EPS)
            outs.append(((o * g) * (1.0 - lam_init)).astype(BF))
        o_ref[0] = jnp.concatenate(outs, axis=-1)


def _diff_sample_call(qkv, cache_k, cache_v, lam_par, g_sub, wc, lam_init):
    b, t, _ = qkv.shape
    nh = wc // (2 * HEAD_DIM)
    p_len = cache_k.shape[1] // nh
    assert (p_len // CHUNK) * CHUNK == p_len and t <= CHUNK
    tk = min(CACHE_TILE_C, p_len)
    return pl.pallas_call(
        functools.partial(_diff_sample_body, nh=nh, lam_init=lam_init, p_len=p_len),
        grid=(b, p_len // tk),
        in_specs=[pl.BlockSpec((1, t, wc), lambda b, j: (b, 0, 0)),
                  pl.BlockSpec((1, t, wc), lambda b, j: (b, 0, 1)),
                  pl.BlockSpec((1, t, wc), lambda b, j: (b, 0, 2)),
                  pl.BlockSpec((1, tk * nh, 2 * HEAD_DIM), lambda b, j: (b, j, 0)),
                  pl.BlockSpec((1, tk * nh, 2 * HEAD_DIM), lambda b, j: (b, j, 0)),
                  pl.BlockSpec(lam_par.shape, lambda b, j: (0, 0)),
                  pl.BlockSpec(g_sub.shape, lambda b, j: (0, 0))],
        out_specs=pl.BlockSpec((1, t, wc), lambda b, j: (b, 0, 0)),
        out_shape=jax.ShapeDtypeStruct((b, t, wc), BF),
        scratch_shapes=[pltpu.VMEM((nh, 2 * t, 1), F32), pltpu.VMEM((nh, 2 * t, 1), F32),
                        pltpu.VMEM((nh, 2 * t, 2 * HEAD_DIM), F32)],
        compiler_params=_cparams(("arbitrary", "arbitrary")),
        name="diff_sample",
    )(qkv, qkv, qkv, cache_k, cache_v, lam_par, g_sub)


def _out_ffn_body(x_ref, oa_ref, ob_ref, gm_ref, g_ref, sh_ref, sc_ref, gf_ref, wo_ref, w1_ref, w3_ref, w2_ref, y_ref,
                  *, f_chunk):
    wa = oa_ref.shape[-1]
    attn = _dot(oa_ref[0], wo_ref[:wa, :]) + _dot(ob_ref[0], wo_ref[wa:, :])
    x1 = x_ref[0] + gm_ref[0, 0, 0] * attn
    h = _modulate(x1, g_ref[...], sh_ref[0, 0, 0], sc_ref[0, 0, 0]).astype(BF)
    f_total = w1_ref.shape[1]
    f = None
    for c0 in range(0, f_total, f_chunk):
        a = _dot(h, w1_ref[:, c0:c0 + f_chunk])
        g = _dot(h, w3_ref[:, c0:c0 + f_chunk])
        u = ((a * _sigmoid(a)) * g).astype(BF)
        part = _dot(u, w2_ref[c0:c0 + f_chunk, :])
        f = part if f is None else f + part
    y_ref[0] = x1 + gf_ref[0, 0, 0] * f


def _out_ffn_call(x, o_a, o_b, gate_m, g, shift, scale, gate_f, wo, w1, w3, w2):
    bx, sx, d = x.shape
    tm = min(ROW_TILE, sx)
    f_total = w1.shape[1]
    f_chunk = f_total // 2 if (f_total // 2) % LANES == 0 else f_total
    row = lambda b, i: (b, i, 0)
    return pl.pallas_call(
        functools.partial(_out_ffn_body, f_chunk=f_chunk),
        grid=(bx, sx // tm),
        in_specs=[pl.BlockSpec((1, tm, d), row),
                  pl.BlockSpec((1, tm, o_a.shape[-1]), row), pl.BlockSpec((1, tm, o_b.shape[-1]), row),
                  _mod_spec(gate_m, tm), _const_spec(g), _mod_spec(shift, tm), _mod_spec(scale, tm),
                  _mod_spec(gate_f, tm), _const_spec(wo), _const_spec(w1), _const_spec(w3), _const_spec(w2)],
        out_specs=pl.BlockSpec((1, tm, d), row),
        out_shape=jax.ShapeDtypeStruct((bx, sx, d), F32),
        compiler_params=_cparams(("arbitrary", "arbitrary")),
        name="out_ffn",
    )(x, o_a, o_b, gate_m[0], g, shift[0], scale[0], gate_f[0], wo, w1, w3, w2)


def _store_tile_rows(ref, x):
    rows, d = x.shape
    nseg = d // LANES
    for s in range(nseg):
        ref[pl.ds(s, rows, stride=nseg), :] = x[:, s * LANES:(s + 1) * LANES]


def _load_tile_rows(ref, rows, d):
    nseg = d // LANES
    return jnp.concatenate([ref[pl.ds(s, rows, stride=nseg), :] for s in range(nseg)], axis=-1)


def _out_router_body(x_ref, o_ref, gm_ref, g_ref, sh_ref, sc_ref, wo_ref, wr_ref, h_in_ref,
                     x1_ref, h_ref, route_ref, *, n_exp):
    del h_in_ref
    x1 = x_ref[0] + gm_ref[0, 0, 0] * _dot(o_ref[0], wo_ref[...])
    x1_ref[0] = x1
    h = _modulate(x1, g_ref[...], sh_ref[0, 0, 0], sc_ref[0, 0, 0])
    hb = h.astype(BF)
    _store_tile_rows(h_ref, hb.astype(F32))
    logits = _dot(hb, wr_ref[0]) + _dot(hb, wr_ref[1]) + _dot(hb, wr_ref[2])
    lt = logits.T[:ROUTE_ROWS, :]
    row = lax.broadcasted_iota(jnp.int32, lt.shape, 0)
    lg = jnp.where(row < n_exp, lt, NEG_INF)
    e = jnp.exp(lg - jnp.max(lg, axis=0, keepdims=True))
    probs = e / jnp.sum(e, axis=0, keepdims=True)
    p1 = jnp.max(probs, axis=0, keepdims=True)
    i1 = jnp.min(jnp.where(probs == p1, row, ROUTE_ROWS), axis=0, keepdims=True)
    rest = jnp.where(row == i1, -1.0, probs)
    p2 = jnp.max(rest, axis=0, keepdims=True)
    i2 = jnp.min(jnp.where(rest == p2, row, ROUTE_ROWS), axis=0, keepdims=True)
    tot = p1 + p2
    route_ref[0] = jnp.where(row == 0, i1.astype(F32),
                             jnp.where(row == 1, i2.astype(F32),
                                       jnp.where(row == 2, p1 / tot, jnp.where(row == 3, p2 / tot, 0.0))))


def _out_router_call(x, o, gate_m, g, shift, scale, wo, wr3, n_exp, tok_offset, h_all):
    bx, sx, d = x.shape
    tm = min(ROW_TILE, sx)
    nt = sx // tm
    nseg = d // LANES
    assert tok_offset % tm == 0 and n_exp <= ROUTE_ROWS
    row = lambda b, i: (b, i, 0)
    return pl.pallas_call(
        functools.partial(_out_router_body, n_exp=n_exp),
        grid=(bx, nt),
        in_specs=[pl.BlockSpec((1, tm, d), row), pl.BlockSpec((1, tm, o.shape[-1]), row),
                  _mod_spec(gate_m, tm), _const_spec(g), _mod_spec(shift, tm), _mod_spec(scale, tm),
                  _const_spec(wo), _const_spec(wr3), pl.BlockSpec(memory_space=pl.ANY)],
        out_specs=[pl.BlockSpec((1, tm, d), row),
                   pl.BlockSpec((tm * nseg, LANES), lambda b, i: (tok_offset // tm + b * nt + i, 0)),
                   pl.BlockSpec((1, ROUTE_ROWS, tm), lambda b, i: (b, 0, i))],
        out_shape=[jax.ShapeDtypeStruct((bx, sx, d), F32),
                   jax.ShapeDtypeStruct(h_all.shape, F32),
                   jax.ShapeDtypeStruct((bx, ROUTE_ROWS, sx), F32)],
        input_output_aliases={8: 1},
        compiler_params=_cparams(("arbitrary", "arbitrary")),
        name="out_router",
    )(x, o, gate_m[0], g, shift[0], scale[0], wo, wr3, h_all)


def _moe_ffn_body(te_ref, nv_ref, idx_ref, idx_next_ref, h_ref, w1_ref, w3_ref, w2_ref, y_ref, xbuf, acc, sem,
                  *, nseg, nf):
    t = pl.program_id(0)
    j = pl.program_id(1)
    nv = nv_ref[0]
    tm, d = acc.shape
    slot = lax.rem(t, 2)

    def issue(rows_ref, s):
        def body(r, c):
            src = pl.multiple_of(rows_ref[0, 0, r], nseg)
            dst = pl.multiple_of(r * nseg, nseg)
            pltpu.make_async_copy(h_ref.at[pl.ds(src, nseg)], xbuf.at[s, pl.ds(dst, nseg)], sem.at[s]).start()
            return c
        lax.fori_loop(0, tm, body, 0, unroll=8)

    @pl.when((j == 0) & (t == 0))
    def _():
        issue(idx_ref, 0)

    @pl.when((j == 0) & (t + 1 < nv))
    def _():
        issue(idx_next_ref, 1 - slot)

    @pl.when((j == 0) & (t < nv))
    def _():
        pltpu.make_async_copy(h_ref.at[pl.ds(0, tm * nseg)], xbuf.at[slot], sem.at[slot]).wait()

    @pl.when(t < nv)
    def _():
        xb = _load_tile_rows(xbuf.at[slot], tm, d).astype(BF)
        a = _dot(xb, w1_ref[0])
        g = _dot(xb, w3_ref[0])
        u = ((a * _sigmoid(a)) * g).astype(BF)
        part = _dot(u, w2_ref[0])

        if nf == 1:
            _store_tile_rows(y_ref, part)
        else:
            @pl.when(j == 0)
            def _():
                acc[...] = part

            @pl.when((j > 0) & (j < nf - 1))
            def _():
                acc[...] += part

            @pl.when(j == nf - 1)
            def _():
                _store_tile_rows(y_ref, acc[...] + part)

    @pl.when((t >= nv) & (j == nf - 1))
    def _():
        y_ref[...] = jnp.zeros(y_ref.shape, F32)


def _moe_ffn_call(h_all, row_tok, tile_e, n_valid, w1, w3, w2):
    n_rows = row_tok.shape[0]
    d = w1.shape[1]
    nseg = d // LANES
    f_total = w1.shape[2]
    tf = MOE_F_TILE if f_total % MOE_F_TILE == 0 else f_total
    nf = f_total // tf
    tm = MOE_ROW_TILE
    n_tiles = n_rows // tm
    idx3 = (row_tok * nseg).reshape(n_tiles, 1, tm)

    def fcol(t, j, te, nv):
        return jnp.where(t < nv[0], j, nf - 1)

    grid_spec = pltpu.PrefetchScalarGridSpec(
        num_scalar_prefetch=2,
        grid=(n_tiles, nf),
        in_specs=[pl.BlockSpec((1, 1, tm), lambda t, j, te, nv: (t, 0, 0), memory_space=pltpu.SMEM),
                  pl.BlockSpec((1, 1, tm), lambda t, j, te, nv: (jnp.minimum(t + 1, n_tiles - 1), 0, 0),
                               memory_space=pltpu.SMEM),
                  pl.BlockSpec(memory_space=pl.ANY),
                  pl.BlockSpec((1, d, tf), lambda t, j, te, nv: (te[t], 0, fcol(t, j, te, nv))),
                  pl.BlockSpec((1, d, tf), lambda t, j, te, nv: (te[t], 0, fcol(t, j, te, nv))),
                  pl.BlockSpec((1, tf, d), lambda t, j, te, nv: (te[t], fcol(t, j, te, nv), 0))],
        out_specs=pl.BlockSpec((tm * nseg, LANES), lambda t, j, te, nv: (t, 0)),
        scratch_shapes=[pltpu.VMEM((2, tm * nseg, LANES), F32), pltpu.VMEM((tm, d), F32),
                        pltpu.SemaphoreType.DMA((2,))],
    )
    return pl.pallas_call(
        functools.partial(_moe_ffn_body, nseg=nseg, nf=nf),
        grid_spec=grid_spec,
        out_shape=jax.ShapeDtypeStruct((n_rows * nseg, LANES), F32),
        compiler_params=_cparams(("arbitrary", "arbitrary")),
        name="moe_ffn",
    )(tile_e, n_valid, idx3, idx3, h_all, w1, w3, w2)


def _combine_body(d0_ref, d1_ref, x_ref, gf_ref, w_ref, g_ref, yb_ref, o_ref, buf, sem):
    t = pl.program_id(0)
    nt = pl.num_programs(0)
    _, tc, d = x_ref.shape
    nseg = d // LANES

    def row_copy(idx_ref, r, slot, k):
        src = pl.multiple_of(idx_ref[0, 0, 2 * r + k], nseg)
        dst = pl.multiple_of(r * nseg, nseg)
        return pltpu.make_async_copy(yb_ref.at[pl.ds(src, nseg)], buf.at[slot, k, pl.ds(dst, nseg)], sem.at[slot])

    def issue(idx_ref, slot):
        def body(r, c):
            row_copy(idx_ref, r, slot, 0).start()
            row_copy(idx_ref, r, slot, 1).start()
            return c
        lax.fori_loop(0, tc, body, 0, unroll=4)

    slot = lax.rem(t, 2)

    @pl.when(t == 0)
    def _():
        issue(d0_ref, 0)

    @pl.when(t + 1 < nt)
    def _():
        issue(d1_ref, 1 - slot)

    for k in range(TOP_K):
        pltpu.make_async_copy(yb_ref.at[pl.ds(0, tc * nseg)], buf.at[slot, k], sem.at[slot]).wait()
    w = jnp.concatenate([w_ref[0], jnp.zeros((LANES - ROUTE_ROWS, tc), F32)], axis=0).T
    f = (w[:, 2:3] * _load_tile_rows(buf.at[slot, 0], tc, d)
         + w[:, 3:4] * _load_tile_rows(buf.at[slot, 1], tc, d))
    y = x_ref[0] + gf_ref[0, 0, 0] * f
    y = y * lax.rsqrt(jnp.mean(y * y, axis=-1, keepdims=True) + RMS_EPS)
    o_ref[0] = y * g_ref[...]


def _combine_call(x1, gate_f, route, g_final, yb, dest):
    bx, sx, d = x1.shape
    nseg = d // LANES
    tc = min(COMBINE_TILE, sx)
    nps = sx // tc
    nt = bx * nps
    dest3 = (dest * nseg).reshape(nt, 1, 2 * tc)
    row = lambda t: (t // nps, t % nps, 0)
    gate_arr, gl, gk = gate_f
    if gate_arr.shape[3] == 1:
        gate_spec = pl.BlockSpec((1, 1, 1, 1, d), lambda t: (gl, gk, t // nps, 0, 0))
    else:
        gate_spec = pl.BlockSpec((1, 1, 1, tc, d), lambda t: (gl, gk, t // nps, t % nps, 0))
    return pl.pallas_call(
        _combine_body,
        grid=(nt,),
        in_specs=[pl.BlockSpec((1, 1, 2 * tc), lambda t: (t, 0, 0), memory_space=pltpu.SMEM),
                  pl.BlockSpec((1, 1, 2 * tc), lambda t: (jnp.minimum(t + 1, nt - 1), 0, 0),
                               memory_space=pltpu.SMEM),
                  pl.BlockSpec((1, tc, d), row), gate_spec,
                  pl.BlockSpec((1, ROUTE_ROWS, tc), lambda t: (t // nps, 0, t % nps)),
                  pl.BlockSpec(g_final.shape, lambda t: (0, 0)),
                  pl.BlockSpec(memory_space=pl.ANY)],
        out_specs=pl.BlockSpec((1, tc, d), row),
        out_shape=jax.ShapeDtypeStruct((bx, sx, d), F32),
        scratch_shapes=[pltpu.VMEM((2, 2, tc * nseg, LANES), F32), pltpu.SemaphoreType.DMA((2,))],
        compiler_params=_cparams(("arbitrary",)),
        name="moe_combine",
    )(dest3, dest3, x1, gate_arr, route, g_final, yb)


def _route_plan(slot_e, n_exp, tm):
    n_slots = slot_e.shape[0]
    onehot = (slot_e[:, None] == jnp.arange(n_exp, dtype=jnp.int32)[None, :]).astype(jnp.int32)
    csum = jnp.cumsum(onehot, axis=0)
    rank = jnp.sum(csum * onehot, axis=1) - 1
    counts = csum[-1]
    padded = ((counts + tm - 1) // tm) * tm
    pad_end = jnp.cumsum(padded)
    pad_start = pad_end - padded
    dest = jnp.sum(onehot * pad_start[None, :], axis=1) + rank
    n_tiles = -(-n_slots // tm) + n_exp
    n_rows = n_tiles * tm
    n_valid = (pad_end[-1] // tm).astype(jnp.int32)
    tile_start = jnp.arange(n_tiles, dtype=jnp.int32) * tm
    tile_e = jnp.minimum(jnp.searchsorted(pad_end, tile_start, side='right'), n_exp - 1).astype(jnp.int32)
    last_e = tile_e[jnp.maximum(n_valid - 1, 0)]
    tile_e = jnp.where(jnp.arange(n_tiles) < n_valid, tile_e, last_e)
    sorted_tok = (jnp.argsort(slot_e, stable=True) // TOP_K).astype(jnp.int32)
    sorted_tok = jnp.concatenate([sorted_tok, jnp.zeros((tm,), jnp.int32)])
    first = jnp.cumsum(counts) - counts - pad_start
    tile_first = jnp.minimum(tile_start + first[tile_e], n_slots)
    row_tok = sorted_tok[(tile_first[:, None] + jnp.arange(tm, dtype=jnp.int32)[None, :]).reshape(n_rows)]
    return dest, row_tok, tile_e, n_valid.reshape(1)


def _moe(parts, h_all, g_final, w1, w3, w2):
    n_exp = w1.shape[0]
    slot_e = jnp.concatenate([r[:, :TOP_K, :].transpose(0, 2, 1).reshape(-1) for _, r, _ in parts], axis=0)
    dest, row_tok, tile_e, n_valid = _route_plan(slot_e.astype(jnp.int32), n_exp, MOE_ROW_TILE)
    yb = _moe_ffn_call(h_all, row_tok, tile_e, n_valid, w1, w3, w2)
    outs = []
    first = 0
    for x1, route, gate_f in parts:
        n_slots = x1.shape[0] * x1.shape[1] * TOP_K
        outs.append(_combine_call(x1, gate_f, route, g_final, yb, dest[first:first + n_slots]))
        first += n_slots
    return outs


def _trunk(x, mods, p, cache, tok_offset, h_all):
    sh_m0, sc_m0, gt_m0, sh_f0, sc_f0, gt_f0 = mods[0]
    sh_m1, sc_m1, gt_m1, sh_f1, sc_f1, gt_f1 = mods[1]
    wa = p['wa']
    wb = p['wb']
    wc = p['wc']
    bx, sx, d = x.shape

    if cache is None:
        keep = min(WINDOW_B, sx)
        qk, vt, ka, va, kb, vb, logf = _inproj_even_call(
            x, p['g_mix0'], sh_m0, sc_m0, p['w_in_even'], p['w_f'], p['b_f'], keep, True)
        o_a = _fox_prompt_call(qk, vt, wa)
        o_b = _band_prompt_call(qk, vt, p['rel_bias'], wa, wb)
        x_l0_state = (ka, va, logf, kb, vb)
    else:
        ck, cv, clogf, bk, bv, cck, ccv, b, t = cache
        qkv, ka, va, kb, vb, logf, lft, = _inproj_even_call(
            x, p['g_mix0'], sh_m0, sc_m0, p['w_in_even'], p['w_f'], p['b_f'], sx, False)
        nh = wa // HEAD_DIM
        p_len = ck.shape[1]
        lft_new = lft.reshape(nh, b, t).transpose(1, 0, 2)
        lp = -(-(p_len + t) // 512) * 512
        lcat = jnp.concatenate([clogf.transpose(0, 2, 1), lft_new,
                                jnp.zeros((b, nh, lp - p_len - t), F32)], axis=-1)
        ft_pad = _cumsum_call(lcat)
        qkv_b = qkv.reshape(b, t, qkv.shape[-1])
        o_a = _fox_sample_call(qkv_b, ck, cv, ft_pad, wa).reshape(1, b * t, wa)
        o_b = _band_sample_call(qkv_b, bk, bv, p['rel_bias'], wb).reshape(1, b * t, wb)
        x_l0_state = (ka, va, logf, kb, vb)
    x = _out_ffn_call(x, o_a, o_b, gt_m0, p['g_ffn0'], sh_f0, sc_f0, gt_f0,
                      p['w_out_even'], p['w1_dense'], p['w3_dense'], p['w2_dense'])

    if cache is None:
        qk, vt, kc, vc = _inproj_odd_call(x, p['g_mix1'], sh_m1, sc_m1, p['w_in_odd'], True)
        o = _diff_prompt_call(qk, vt, p['lam_par'], p['g_sub'].reshape(-1, 1), wc, p['lam_init'])
    else:
        qkv, kc, vc = _inproj_odd_call(x, p['g_mix1'], sh_m1, sc_m1, p['w_in_odd'], False)
        qkv_b = qkv.reshape(b, t, qkv.shape[-1])
        o = _diff_sample_call(qkv_b, cck, ccv, p['lam_par'], p['g_sub'], wc, p['lam_init']).reshape(1, b * t, wc)
    x1, h_all, route = _out_router_call(x, o, gt_m1, p['g_ffn1'], sh_f1, sc_f1, p['w_out_odd'], p['w_router3'],
                                        p['n_exp'], tok_offset, h_all)
    return (x1, route, gt_f1), h_all, x_l0_state, (kc, vc)


def kernel(x_prompt, x_sample, cache_a_k, cache_a_v, cache_a_logf, cache_b_k, cache_b_v, cache_c_k, cache_c_v,
           c_prompt, c_sample, w_mod, b_mod, g_mix, g_ffn, g_final, w_in_even, b_forget, rel_bias, w_out_even,
           w_in_odd, lambda_q1, lambda_k1, lambda_q2, lambda_k2, g_subln, w_out_odd, w1_dense, w3_dense, w2_dense,
           w_router, w1_moe, w3_moe, w2_moe):
    bp, sp, d = x_prompt.shape
    bs, ts, _ = x_sample.shape
    h_a = cache_a_k.shape[3]
    h_b = cache_b_k.shape[3]
    h_c = cache_c_k.shape[3]
    wa, wb, wc = h_a * HEAD_DIM, h_b * HEAD_DIM, h_c * 2 * HEAD_DIM
    n_exp = w_router.shape[-1]
    assert w_mod.shape[0] == 2, "kernel is written for the 2-layer trunk"

    we = w_in_even[0]
    w_main = jnp.concatenate([we[:, :3 * wa], we[:, 3 * wa + h_a:]], axis=1).astype(BF)
    w_f = jnp.pad(we[:, 3 * wa:3 * wa + h_a], ((0, 0), (0, LANES - h_a))).astype(BF)
    b_f = jnp.pad(b_forget[0], (0, LANES - h_a)).reshape(1, LANES).astype(F32)
    wr = jnp.pad(w_router[0], ((0, 0), (0, LANES - n_exp)))
    wr_hi = wr.astype(BF)
    wr_r1 = wr - wr_hi.astype(F32)
    wr_mid = wr_r1.astype(BF)
    wr_lo = (wr_r1 - wr_mid.astype(F32)).astype(BF)
    params = dict(
        wa=wa, wb=wb, wc=wc, n_exp=n_exp, lam_init=0.8 - 0.6 * math.exp(-0.3 * 1),
        g_mix0=g_mix[0:1], g_mix1=g_mix[1:2], g_ffn0=g_ffn[0:1], g_ffn1=g_ffn[1:2],
        g_final=g_final.reshape(1, d),
        w_in_even=w_main, w_f=w_f, b_f=b_f, rel_bias=rel_bias[0],
        w_out_even=w_out_even[0].astype(BF), w_in_odd=w_in_odd[0].astype(BF),
        lam_par=jnp.concatenate([lambda_q1, lambda_k1, lambda_q2, lambda_k2], axis=0),
        g_sub=g_subln[0:1], w_out_odd=w_out_odd[0].astype(BF),
        w1_dense=w1_dense[0].astype(BF), w3_dense=w3_dense[0].astype(BF), w2_dense=w2_dense[0].astype(BF),
        w_router3=jnp.stack([wr_hi, wr_mid, wr_lo]),
        w1_moe=_cast_bf16_call(w1_moe[0]), w3_moe=_cast_bf16_call(w3_moe[0]), w2_moe=_cast_bf16_call(w2_moe[0]),
    )

    mod = _mod_call(jnp.concatenate([c_prompt, c_sample], axis=0), w_mod, b_mod)

    def mods_for(rows, per_row_len):
        m = mod[:, rows].reshape(2, -1, 6, d).transpose(0, 2, 1, 3)
        if per_row_len:
            m = jnp.repeat(m, per_row_len, axis=2)[:, :, None]
        else:
            m = m[:, :, :, None]
        return [[(m, l, k) for k in range(6)] for l in range(2)]

    h_all = jnp.zeros(((bp * sp + bs * ts) * (d // LANES), LANES), F32)
    moe_p, h_all, ev_p, od_p = _trunk(x_prompt, mods_for(slice(0, bp), 0), params, None, 0, h_all)
    cache = (cache_a_k[0].reshape(bs, -1, wa), cache_a_v[0].reshape(bs, -1, wa), cache_a_logf[0],
             cache_b_k[0].reshape(bs, -1, wb), cache_b_v[0].reshape(bs, -1, wb),
             cache_c_k[0].reshape(bs, -1, 2 * HEAD_DIM), cache_c_v[0].reshape(bs, -1, 2 * HEAD_DIM), bs, ts)
    moe_s, h_all, ev_s, od_s = _trunk(x_sample.reshape(1, bs * ts, d), mods_for(slice(bp, bp + bs), ts), params,
                                      cache, bp * sp, h_all)
    y_p, y_s = _moe([moe_p, moe_s], h_all, params['g_final'], params['w1_moe'], params['w3_moe'], params['w2_moe'])

    ka, va, logf, kb, vb = ev_p
    keep = kb.shape[1]
    out_p = (ka.reshape(1, bp, sp, h_a, HEAD_DIM), va.reshape(1, bp, sp, h_a, HEAD_DIM),
             logf.reshape(1, bp, sp, h_a),
             kb.reshape(1, bp, keep, h_b, HEAD_DIM), vb.reshape(1, bp, keep, h_b, HEAD_DIM),
             od_p[0].reshape(1, bp, sp, h_c, 2 * HEAD_DIM), od_p[1].reshape(1, bp, sp, h_c, 2 * HEAD_DIM))
    ka, va, logf, kb, vb = ev_s
    new_bk = jnp.concatenate([cache_b_k[0], kb.reshape(bs, ts, h_b, HEAD_DIM)], axis=1)[:, ts:]
    new_bv = jnp.concatenate([cache_b_v[0], vb.reshape(bs, ts, h_b, HEAD_DIM)], axis=1)[:, ts:]
    out_s = (ka.reshape(1, bs, ts, h_a, HEAD_DIM), va.reshape(1, bs, ts, h_a, HEAD_DIM),
             logf.reshape(1, bs, ts, h_a), new_bk[None], new_bv[None],
             od_s[0].reshape(1, bs, ts, h_c, 2 * HEAD_DIM), od_s[1].reshape(1, bs, ts, h_c, 2 * HEAD_DIM))
    return (y_p, y_s.reshape(bs, ts, d)) + out_p + out_s
```

```python
import functools
import math

import jax
import jax.numpy as jnp
import numpy as np
from jax import lax
from jax.experimental import pallas as pl
from jax.experimental.pallas import tpu as pltpu

BF = jnp.bfloat16
F32 = jnp.float32

CHUNK = 64
CHUNK_SHIFT = 6
HEAD_DIM = 64
BAND_CHUNKS = 8
WINDOW_B = BAND_CHUNKS * CHUNK
MAX_REL = 128
TOP_K = 2
RMS_EPS = 1e-6
NEG_INF = -1e30
QK_SCALE = HEAD_DIM ** -0.5
LOG2E = math.log2(math.e)
PROMPT_Q_SCALE = QK_SCALE * LOG2E


def _bf16_terms(c, n):
    terms = []
    for _ in range(n):
        t = float(np.asarray(c, dtype=jnp.bfloat16))
        terms.append(t)
        c -= t
    return terms


LOG2E_TERMS = _bf16_terms(LOG2E, 3)

LANES = 128
SUBLANES = 8
VMEM_LIMIT_BYTES = 56 * 1024 * 1024
ROUTE_ROWS = SUBLANES

ROW_TILE = 512
ATTN_TILE = 512
BAND_Q_TILE = WINDOW_B // 2
CACHE_TILE_A = 1024
CACHE_TILE_C = 512
MOE_ROW_TILE = 512
MOE_F_TILE = 1792
COMBINE_TILE = 256
CAST_ROWS = 512


def _cparams(sem):
    return pltpu.CompilerParams(dimension_semantics=sem, vmem_limit_bytes=VMEM_LIMIT_BYTES)


def _dot(a, b):
    return jnp.dot(a, b, preferred_element_type=F32)


def _dot_nt(a, b):
    return lax.dot_general(a, b, (((1,), (1,)), ((), ())), preferred_element_type=F32)


def _sigmoid(x):
    return 1.0 / (1.0 + jnp.exp(-x))


def _modulate(x, g, shift, scale):
    y = x * lax.rsqrt(jnp.mean(x * x, axis=-1, keepdims=True) + RMS_EPS)
    return (y * g) * (1.0 + scale) + shift


def _split3(x):
    hi = x.astype(BF)
    r1 = x - hi.astype(F32)
    mid = r1.astype(BF)
    lo = (r1 - mid.astype(F32)).astype(BF)
    return hi, mid, lo


def _cumsum_lanes(x, upper):
    hi, mid, lo = _split3(x)
    return _dot(hi, upper) + _dot(mid, upper) + _dot(lo, upper)


def _upper_tri(t):
    r = jnp.arange(t)
    return (r[:, None] <= r[None, :]).astype(BF)


def _osm_update(s, v, m_ref, l_ref, acc_ref, idx):
    m_prev = m_ref[idx]
    m_new = jnp.maximum(m_prev, jnp.max(s, axis=-1, keepdims=True))
    alpha = jnp.exp(m_prev - m_new)
    p = jnp.exp(s - m_new)
    l_ref[idx] = alpha * l_ref[idx] + jnp.sum(p, axis=-1, keepdims=True)
    acc_ref[idx] = alpha * acc_ref[idx] + _dot(p.astype(BF), v)
    m_ref[idx] = m_new


def _osm_init(m_ref, l_ref, acc_ref):
    m_ref[...] = jnp.full(m_ref.shape, NEG_INF, F32)
    l_ref[...] = jnp.zeros(l_ref.shape, F32)
    acc_ref[...] = jnp.zeros(acc_ref.shape, F32)


def _cast_body(x_ref, o_ref):
    o_ref[...] = x_ref[...].astype(BF)


def _cast_bf16_call(w):
    e, k, n = w.shape
    tk = CAST_ROWS if k % CAST_ROWS == 0 else k
    spec = pl.BlockSpec((1, tk, n), lambda e, i: (e, i, 0))
    return pl.pallas_call(
        _cast_body, grid=(e, k // tk), in_specs=[spec], out_specs=spec,
        out_shape=jax.ShapeDtypeStruct(w.shape, BF),
        compiler_params=_cparams(("arbitrary", "arbitrary")),
        name="cast_bf16",
    )(w)


def _mod_body(c_ref, w_ref, b_ref, o_ref):
    c = c_ref[...]
    s = (c * _sigmoid(c)).astype(BF)
    o_ref[0] = _dot(s, w_ref[0].astype(BF)) + b_ref[0]


def _mod_call(c_all, w_mod, b_mod):
    depth, d, n = w_mod.shape
    r = c_all.shape[0]
    tn = 1536 if n % 1536 == 0 else n
    return pl.pallas_call(
        _mod_body,
        grid=(depth, n // tn),
        in_specs=[pl.BlockSpec((r, d), lambda l, j: (0, 0)),
                  pl.BlockSpec((1, d, tn), lambda l, j: (l, 0, j)),
                  pl.BlockSpec((1, 1, tn), lambda l, j: (l, 0, j))],
        out_specs=pl.BlockSpec((1, r, tn), lambda l, j: (l, 0, j)),
        out_shape=jax.ShapeDtypeStruct((depth, r, n), F32),
        compiler_params=_cparams(("arbitrary", "arbitrary")),
        name="mod",
    )(c_all, w_mod, b_mod.reshape(depth, 1, n))


def _mod_spec(m, tm):
    arr, l, k = m
    d = arr.shape[-1]
    if arr.shape[3] == 1:
        return pl.BlockSpec((1, 1, 1, 1, d), lambda b, i: (l, k, b, 0, 0))
    return pl.BlockSpec((1, 1, 1, tm, d), lambda b, i: (l, k, b, i, 0))


def _const_spec(arr):
    nd = arr.ndim
    return pl.BlockSpec(arr.shape, lambda b, i: (0,) * nd, pipeline_mode=pl.Buffered(1))


def _head_blocks(x, augs):
    tm, w = x.shape
    first = lax.broadcasted_iota(jnp.int32, (tm, LANES), 1) < HEAD_DIM
    out = []
    for p in range(w // LANES):
        pair = x[:, p * LANES:(p + 1) * LANES]
        out.append(jnp.where(first, pair, augs[2 * p]).astype(BF))
        out.append(jnp.where(first, pltpu.roll(pair, HEAD_DIM, 1), augs[2 * p + 1]).astype(BF))
    return out


def _lane_range_row(lo, hi, value):
    lane = lax.broadcasted_iota(jnp.int32, (1, LANES), 1)
    return jnp.where((lane >= lo) & (lane < hi), value, 0.0).astype(F32)


def _ones_rows(t):
    return jnp.where(lax.broadcasted_iota(jnp.int32, (16, t), 0) == 0, 1.0, 0.0).astype(BF)


def _log_forget(h, wf_ref, bf_ref):
    z = _dot(h, wf_ref[...]) + bf_ref[...]
    return jnp.minimum(z, 0.0) - jnp.log1p(jnp.exp(-jnp.abs(z)))


def _inproj_even_body(*refs, wa, wb, prompt):
    if prompt:
        (x_ref, g_ref, sh_ref, sc_ref, w_ref, wf_ref, bf_ref, low_ref, place_ref,
         qk_ref, vt_ref, ka_ref, va_ref, kb_ref, vb_ref, logf_ref, carry_ref) = refs
    else:
        (x_ref, g_ref, sh_ref, sc_ref, w_ref, wf_ref, bf_ref,
         qkv_ref, ka_ref, va_ref, kb_ref, vb_ref, logf_ref, lft_ref) = refs
    i = pl.program_id(1)
    h = _modulate(x_ref[0], g_ref[...], sh_ref[0, 0, 0], sc_ref[0, 0, 0]).astype(BF)
    tm = h.shape[0]
    nh = logf_ref.shape[-1]

    qa = _dot(h, w_ref[:, 0:wa])
    ka = _dot(h, w_ref[:, wa:2 * wa])
    va = _dot(h, w_ref[:, 2 * wa:3 * wa])
    o = 3 * wa
    qb = _dot(h, w_ref[:, o:o + wb])
    kb = _dot(h, w_ref[:, o + wb:o + 2 * wb])
    vb = _dot(h, w_ref[:, o + 2 * wb:o + 3 * wb])
    ka_ref[0] = ka
    va_ref[0] = va
    kb_ref[0] = kb
    vb_ref[0] = vb
    logf = _log_forget(h, wf_ref, bf_ref)
    logf_ref[0] = logf[:, :nh]

    if not prompt:
        qkv_ref[0] = jnp.concatenate(
            [(qa * QK_SCALE).astype(BF), ka.astype(BF), va.astype(BF),
             (qb * QK_SCALE).astype(BF), kb.astype(BF), vb.astype(BF)], axis=-1)
        lft_ref[0] = logf.T[:nh, :]
        return

    @pl.when(i == 0)
    def _():
        carry_ref[...] = jnp.zeros(carry_ref.shape, F32)

    low = low_ref[...]
    hi, mid, lo = _split3(logf)
    f_cum = _dot(low, hi) + _dot(low, mid) + _dot(low, lo) + carry_ref[0:1, :]
    carry_ref[...] = jnp.broadcast_to(f_cum[tm - 1:tm, :], carry_ref.shape)
    nhi, nmid, nlo = _split3(-f_cum)
    k_aug = _dot(nhi, place_ref[0]) + _dot(nmid, place_ref[1]) + _dot(nlo, place_ref[2])
    q_log2e = sum(_lane_range_row(HEAD_DIM + 3 * r, HEAD_DIM + 3 * r + 3, c) for r, c in enumerate(LOG2E_TERMS))
    zero = jnp.zeros((1, LANES), F32)
    nhb = wb // HEAD_DIM
    blocks = (_head_blocks(qa * PROMPT_Q_SCALE, [q_log2e] * nh)
              + _head_blocks(ka, [k_aug[:, n * LANES:(n + 1) * LANES] for n in range(nh)])
              + _head_blocks(qb * PROMPT_Q_SCALE, [zero] * nhb)
              + _head_blocks(kb, [zero] * nhb))
    qk_ref[0] = jnp.concatenate(blocks, axis=-1)
    ones = _ones_rows(tm)
    vt_ref[0] = jnp.concatenate([va.T.astype(BF), ones, vb.T.astype(BF), ones], axis=0)


def _lower_tri(t):
    r = jnp.arange(t)
    return (r[:, None] >= r[None, :]).astype(BF)


def _place_matrices(nh):
    s = jnp.arange(3)[:, None, None]
    r = jnp.arange(LANES)[None, :, None]
    c = jnp.arange(nh * LANES)[None, None, :]
    off = c - r * LANES - HEAD_DIM - s
    return ((r < nh) & ((off == 0) | (off == 3) | (off == 6))).astype(BF)


def _inproj_even_call(x, g, shift, scale, w_main, w_f, b_f, keep, prompt):
    bx, sx, d = x.shape
    tm = min(ROW_TILE, sx)
    nt = sx // tm
    n = w_main.shape[1]
    wa = wb = n // 6
    nh = wa // HEAD_DIM
    nkeep = keep // tm
    tail = lambda b, i: (b, jnp.maximum(i - (nt - nkeep), 0), 0)
    row = lambda b, i: (b, i, 0)
    in_specs = [pl.BlockSpec((1, tm, d), row), _const_spec(g), _mod_spec(shift, tm), _mod_spec(scale, tm),
                _const_spec(w_main), _const_spec(w_f), _const_spec(b_f)]
    args = [x, g, shift[0], scale[0], w_main, w_f, b_f]
    f32_specs = [pl.BlockSpec((1, tm, wa), row), pl.BlockSpec((1, tm, wa), row),
                 pl.BlockSpec((1, tm, wb), tail), pl.BlockSpec((1, tm, wb), tail),
                 pl.BlockSpec((1, tm, nh), row)]
    f32_shapes = [jax.ShapeDtypeStruct((bx, sx, wa), F32), jax.ShapeDtypeStruct((bx, sx, wa), F32),
                  jax.ShapeDtypeStruct((bx, keep, wb), F32), jax.ShapeDtypeStruct((bx, keep, wb), F32),
                  jax.ShapeDtypeStruct((bx, sx, nh), F32)]
    scratch = []
    if prompt:
        low, place = _lower_tri(tm), _place_matrices(nh)
        in_specs += [_const_spec(low), _const_spec(place)]
        args += [low, place]
        nqk = 2 * (wa + wb) // HEAD_DIM * LANES
        nvt = wa + wb + 32
        out_specs = [pl.BlockSpec((1, tm, nqk), row), pl.BlockSpec((1, nvt, tm), lambda b, i: (b, 0, i))] + f32_specs
        out_shape = [jax.ShapeDtypeStruct((bx, sx, nqk), BF), jax.ShapeDtypeStruct((bx, nvt, sx), BF)] + f32_shapes
        scratch.append(pltpu.VMEM((8, LANES), F32))
    else:
        out_specs = [pl.BlockSpec((1, tm, n), row)] + f32_specs + [pl.BlockSpec((1, nh, tm), lambda b, i: (b, 0, i))]
        out_shape = [jax.ShapeDtypeStruct((bx, sx, n), BF)] + f32_shapes + [jax.ShapeDtypeStruct((bx, nh, sx), F32)]
    return pl.pallas_call(
        functools.partial(_inproj_even_body, wa=wa, wb=wb, prompt=prompt),
        grid=(bx, nt), in_specs=in_specs, out_specs=out_specs, out_shape=out_shape,
        scratch_shapes=scratch,
        compiler_params=_cparams(("arbitrary", "arbitrary")),
        name="inproj_even",
    )(*args)


def _inproj_odd_body(*refs, wc, prompt):
    if prompt:
        x_ref, g_ref, sh_ref, sc_ref, w_ref, qk_ref, vt_ref, k_ref, v_ref = refs
    else:
        x_ref, g_ref, sh_ref, sc_ref, w_ref, qkv_ref, k_ref, v_ref = refs
    i = pl.program_id(1)
    h = _modulate(x_ref[0], g_ref[...], sh_ref[0, 0, 0], sc_ref[0, 0, 0]).astype(BF)
    tm = h.shape[0]
    q = _dot(h, w_ref[:, 0:wc])
    k = _dot(h, w_ref[:, wc:2 * wc])
    v = _dot(h, w_ref[:, 2 * wc:3 * wc])
    k_ref[0] = k
    v_ref[0] = v
    if not prompt:
        qkv_ref[0] = jnp.concatenate([(q * QK_SCALE).astype(BF), k.astype(BF), v.astype(BF)], axis=-1)
        return
    nh = wc // LANES
    pos = (i * tm + lax.broadcasted_iota(jnp.int32, (tm, LANES), 0)).astype(F32)
    pos_hi = pos.astype(BF).astype(F32)
    lane = lax.broadcasted_iota(jnp.int32, (tm, LANES), 1) - HEAD_DIM
    in_aug = (lane >= 0) & (lane < 2 * len(LOG2E_TERMS))
    k_aug = jnp.where(in_aug, jnp.where((lane & 1) == 0, pos_hi, pos - pos_hi), 0.0)
    q_augs = []
    for hh in range(nh):
        slope = _alibi_slope(hh, nh)
        row = sum(_lane_range_row(HEAD_DIM + 2 * r, HEAD_DIM + 2 * r + 2, slope * c) for r, c in enumerate(LOG2E_TERMS))
        q_augs += [row] * 2
    blocks = _head_blocks(q * PROMPT_Q_SCALE, q_augs) + _head_blocks(k, [k_aug] * (2 * nh))
    qk_ref[0] = jnp.concatenate(blocks, axis=-1)
    vt_ref[0] = jnp.concatenate([v.T.astype(BF), _ones_rows(tm)], axis=0)


def _alibi_slope(h, nh):
    assert 8 % nh == 0, "slopes must be exact powers of two to ride in bf16"
    return 2.0 ** (-8.0 * (h + 1) / nh)


def _inproj_odd_call(x, g, shift, scale, w, prompt):
    bx, sx, d = x.shape
    tm = min(ROW_TILE, sx)
    n = w.shape[1]
    wc = n // 3
    row = lambda b, i: (b, i, 0)
    f32_specs = [pl.BlockSpec((1, tm, wc), row), pl.BlockSpec((1, tm, wc), row)]
    f32_shapes = [jax.ShapeDtypeStruct((bx, sx, wc), F32), jax.ShapeDtypeStruct((bx, sx, wc), F32)]
    if prompt:
        nqk = 2 * wc // HEAD_DIM * LANES
        out_specs = [pl.BlockSpec((1, tm, nqk), row), pl.BlockSpec((1, wc + 16, tm), lambda b, i: (b, 0, i))]
        out_shape = [jax.ShapeDtypeStruct((bx, sx, nqk), BF), jax.ShapeDtypeStruct((bx, wc + 16, sx), BF)]
    else:
        out_specs = [pl.BlockSpec((1, tm, n), row)]
        out_shape = [jax.ShapeDtypeStruct((bx, sx, n), BF)]
    return pl.pallas_call(
        functools.partial(_inproj_odd_body, wc=wc, prompt=prompt),
        grid=(bx, sx // tm),
        in_specs=[pl.BlockSpec((1, tm, d), row), _const_spec(g), _mod_spec(shift, tm), _mod_spec(scale, tm),
                  _const_spec(w)],
        out_specs=out_specs + f32_specs, out_shape=out_shape + f32_shapes,
        compiler_params=_cparams(("arbitrary", "arbitrary")),
        name="inproj_odd",
    )(x, g, shift[0], scale[0], w)


def _tflash_update(st, v_aug, m_ref, acc_ref, idx):
    m_prev = m_ref[idx]
    m_new = jnp.maximum(m_prev, jnp.max(st, axis=0, keepdims=True))
    alpha = jnp.exp2(m_prev - m_new)
    p = jnp.exp2(st - m_new).astype(BF)
    acc_ref[idx] = alpha * acc_ref[idx] + _dot(v_aug, p)
    m_ref[idx] = m_new


def _diag_scores(st_ref, slot, k_blk, q_blk):
    hf = k_blk.shape[0] // 2
    st_ref[slot, :hf, :] = _dot_nt(k_blk[:hf], q_blk)
    st_ref[slot, hf:, hf:] = _dot_nt(k_blk[hf:], q_blk[hf:])


def _tflash_diag_update(st_ref, slot, v_aug, fix_lo, fix_hi, m_ref, acc_ref, idx):
    hf = st_ref.shape[1] // 2
    lo = fix_lo(st_ref[slot, :hf, :])
    hi = fix_hi(st_ref[slot, hf:, hf:])
    m_prev = m_ref[idx]
    m_lo = jnp.maximum(m_prev, jnp.max(lo, axis=0, keepdims=True))
    m_new = jnp.concatenate([m_lo[:, :hf], jnp.maximum(m_lo[:, hf:], jnp.max(hi, axis=0, keepdims=True))], axis=1)
    alpha = jnp.exp2(m_prev - m_new)
    p_lo = jnp.exp2(lo - m_new).astype(BF)
    p_hi = jnp.exp2(hi - m_new[:, hf:]).astype(BF)
    upd = _dot(v_aug[:, :hf], p_lo)
    upd = jnp.concatenate([upd[:, :hf], upd[:, hf:] + _dot(v_aug[:, hf:], p_hi)], axis=1)
    acc_ref[idx] = alpha * acc_ref[idx] + upd
    m_ref[idx] = m_new


def _tflash_init(m_ref, acc_ref):
    m_ref[...] = jnp.full(m_ref.shape, NEG_INF, F32)
    acc_ref[...] = jnp.zeros(acc_ref.shape, F32)


def _causal_pairs(nq):
    pairs = [(i, j) for i in range(nq) for j in range(i + 1)]
    return jnp.asarray([p[0] for p in pairs], jnp.int32), jnp.asarray([p[1] for p in pairs], jnp.int32)


def _fox_prompt_body(it_ref, jt_ref, q_ref, k_ref, vt_ref, o_ref, m_ref, acc_ref, st_ref, *, nh):
    i = it_ref[pl.program_id(1)]
    j = jt_ref[pl.program_id(1)]
    hd = HEAD_DIM

    @pl.when(j == 0)
    def _():
        _tflash_init(m_ref, acc_ref)

    def step(diag):
        q = q_ref[0]
        k = k_ref[0]
        vt = vt_ref[0]
        t = q.shape[0]
        ones = vt[nh * hd:, :]
        if diag:
            causal = lambda s: jnp.where(lax.broadcasted_iota(jnp.int32, s.shape, 0)
                                         <= lax.broadcasted_iota(jnp.int32, s.shape, 1), s, NEG_INF)

        def scores(h):
            k_blk, q_blk = k[:, h * LANES:(h + 1) * LANES], q[:, h * LANES:(h + 1) * LANES]
            if diag:
                _diag_scores(st_ref, h % 2, k_blk, q_blk)
            else:
                st_ref[h % 2] = _dot_nt(k_blk, q_blk)

        scores(0)
        for h in range(nh):
            if h + 1 < nh:
                scores(h + 1)
            v_aug = jnp.concatenate([vt[h * hd:(h + 1) * hd, :], ones], axis=0)
            if diag:
                _tflash_diag_update(st_ref, h % 2, v_aug, causal, causal, m_ref, acc_ref, h)
            else:
                _tflash_update(st_ref[h % 2], v_aug, m_ref, acc_ref, h)

    @pl.when(j < i)
    def _():
        step(False)

    @pl.when(j == i)
    def _():
        step(True)
        ot = jnp.concatenate([acc_ref[h][:hd] / acc_ref[h][hd:hd + 1] for h in range(nh)], axis=0)
        o_ref[0] = ot.T.astype(BF)


def _fox_prompt_call(qk, vt, wa):
    b, s, _ = qk.shape
    nh = wa // HEAD_DIM
    t = min(ATTN_TILE, s)
    it, jt = _causal_pairs(s // t)
    wq = nh * LANES
    grid_spec = pltpu.PrefetchScalarGridSpec(
        num_scalar_prefetch=2,
        grid=(b, it.shape[0]),
        in_specs=[pl.BlockSpec((1, t, wq), lambda b, s, it, jt: (b, it[s], 0)),
                  pl.BlockSpec((1, t, wq), lambda b, s, it, jt: (b, jt[s], 1)),
                  pl.BlockSpec((1, wa + 16, t), lambda b, s, it, jt: (b, 0, jt[s]))],
        out_specs=pl.BlockSpec((1, t, wa), lambda b, s, it, jt: (b, it[s], 0)),
        scratch_shapes=[pltpu.VMEM((nh, 1, t), F32), pltpu.VMEM((nh, HEAD_DIM + 16, t), F32),
                        pltpu.VMEM((2, t, t), F32)],
    )
    return pl.pallas_call(
        functools.partial(_fox_prompt_body, nh=nh),
        grid_spec=grid_spec,
        out_shape=jax.ShapeDtypeStruct((b, s, wa), BF),
        compiler_params=_cparams(("arbitrary", "arbitrary")),
        name="fox_prompt",
    )(it, jt, qk, qk, vt)


def _band_prompt_body(q_ref, k0_ref, k1_ref, k2_ref, v0_ref, v1_ref, v2_ref, bias_ref, o_ref, st_ref, *, nh):
    i = pl.program_id(1)
    hd = HEAD_DIM
    q = q_ref[0]
    tq = q.shape[0]
    k = jnp.concatenate([k0_ref[0], k1_ref[0], k2_ref[0]], axis=0)
    vts = [v0_ref[0], v1_ref[0], v2_ref[0]]

    def scores(h):
        st_ref[h % 2] = _dot_nt(k[:, h * LANES:(h + 1) * LANES], q[:, h * LANES:(h + 1) * LANES])

    def attend(at_start):
        if at_start:
            in_seq = lax.broadcasted_iota(jnp.int32, (3 * tq, tq), 0) >= (2 - i) * tq
        outs = []
        scores(0)
        for h in range(nh):
            if h + 1 < nh:
                scores(h + 1)
            st = st_ref[h % 2] + bias_ref[h]
            if at_start:
                st = jnp.where(in_seq, st, NEG_INF)
            p = jnp.exp2(st - jnp.max(st, axis=0, keepdims=True)).astype(BF)
            acc = None
            for w, vt in enumerate(vts):
                v_aug = jnp.concatenate([vt[h * hd:(h + 1) * hd, :], vt[nh * hd:, :]], axis=0)
                part = _dot(v_aug, p[w * tq:(w + 1) * tq, :])
                acc = part if acc is None else acc + part
            outs.append(acc[:hd] / acc[hd:hd + 1])
        o_ref[0] = jnp.concatenate(outs, axis=0).T.astype(BF)

    @pl.when(i < 2)
    def _():
        attend(True)

    @pl.when(i >= 2)
    def _():
        attend(False)


def _band_bias_prompt(rel_bias, tq):
    nh = rel_bias.shape[0]
    period = 4 * tq
    t = jnp.arange(period)
    t = jnp.where(t < 3 * tq, t, t - period)
    u = rel_bias[:, jnp.clip(2 * tq - t, -MAX_REL, MAX_REL) + MAX_REL].astype(F32)
    skew = jnp.tile(u, (1, tq))[:, :tq * (period - 1)].reshape(nh, tq, period - 1)
    bias = skew[:, :, :3 * tq]
    r = jnp.arange(tq)[:, None]
    c = jnp.arange(3 * tq)[None, :]
    valid = (c // CHUNK >= r // CHUNK) & (c // CHUNK <= r // CHUNK + BAND_CHUNKS)
    return jnp.where(valid[None], bias, NEG_INF)


def _band_prompt_call(qk, vt, rel_bias, wa, wb):
    b, s, _ = qk.shape
    nh = wb // HEAD_DIM
    tq = BAND_Q_TILE
    assert s % tq == 0 and wa == wb
    bias_t = (_band_bias_prompt(rel_bias, tq) * LOG2E).transpose(0, 2, 1)
    wq = nh * LANES
    cq, ck = 2, 3

    def k_spec(back):
        return pl.BlockSpec((1, tq, wq), lambda b, i: (b, jnp.maximum(i - back, 0), ck))

    def v_spec(back):
        return pl.BlockSpec((1, wb + 16, tq), lambda b, i: (b, 1, jnp.maximum(i - back, 0)))

    return pl.pallas_call(
        functools.partial(_band_prompt_body, nh=nh),
        grid=(b, s // tq),
        in_specs=[pl.BlockSpec((1, tq, wq), lambda b, i: (b, i, cq)),
                  k_spec(2), k_spec(1), k_spec(0), v_spec(2), v_spec(1), v_spec(0),
                  _const_spec(bias_t)],
        out_specs=pl.BlockSpec((1, tq, wb), lambda b, i: (b, i, 0)),
        out_shape=jax.ShapeDtypeStruct((b, s, wb), BF),
        scratch_shapes=[pltpu.VMEM((2, 3 * tq, tq), F32)],
        compiler_params=_cparams(("arbitrary", "arbitrary")),
        name="band_prompt",
    )(qk, qk, qk, qk, vt, vt, vt, bias_t)


def _lambda(lam_ref, lam_init):
    lp = lam_ref[...]
    a = jnp.sum(lp[0:1] * lp[1:2], axis=-1, keepdims=True)
    b = jnp.sum(lp[2:3] * lp[3:4], axis=-1, keepdims=True)
    return jnp.exp(a) - jnp.exp(b) + lam_init


def _diff_prompt_body(it_ref, jt_ref, q_ref, k_ref, vt_ref, lam_ref, gcol_ref, o_ref, m_ref, acc_ref, st_ref,
                      *, nh, lam_init):
    i = it_ref[pl.program_id(1)]
    j = jt_ref[pl.program_id(1)]
    dv = 2 * HEAD_DIM

    @pl.when(j == 0)
    def _():
        _tflash_init(m_ref, acc_ref)

    def step(diag):
        q = q_ref[0]
        k = k_ref[0]
        vt = vt_ref[0]
        t = q.shape[0]
        ones = vt[nh * dv:, :]
        if diag:
            def penalty(shape):
                key = lax.broadcasted_iota(jnp.int32, shape, 0)
                qry = lax.broadcasted_iota(jnp.int32, shape, 1)
                visible = (key >> CHUNK_SHIFT) <= (qry >> CHUNK_SHIFT)
                return jnp.where(visible, 2.0 * jnp.maximum(key - qry, 0).astype(F32), -NEG_INF)
            pen_lo, pen_hi = penalty((t // 2, t)), penalty((t // 2, t // 2))

        def scores(n):
            k_blk, q_blk = k[:, n * LANES:(n + 1) * LANES], q[:, n * LANES:(n + 1) * LANES]
            if diag:
                _diag_scores(st_ref, n % 2, k_blk, q_blk)
            else:
                st_ref[n % 2] = _dot_nt(k_blk, q_blk)

        scores(0)
        for h in range(nh):
            v_aug = jnp.concatenate([vt[h * dv:(h + 1) * dv, :], ones], axis=0)
            if diag:
                c = LOG2E * _alibi_slope(h, nh)
                ex_lo, ex_hi = c * pen_lo, c * pen_hi
            for u in range(2):
                n = 2 * h + u
                if n + 1 < 2 * nh:
                    scores(n + 1)
                if diag:
                    _tflash_diag_update(st_ref, n % 2, v_aug, lambda s: s - ex_lo, lambda s: s - ex_hi,
                                        m_ref, acc_ref, n)
                else:
                    _tflash_update(st_ref[n % 2], v_aug, m_ref, acc_ref, n)

    @pl.when(j < i)
    def _():
        step(False)

    @pl.when(j == i)
    def _():
        step(True)
        lam = _lambda(lam_ref, lam_init)
        g = gcol_ref[...]
        outs = []
        for h in range(nh):
            a1 = acc_ref[2 * h]
            a2 = acc_ref[2 * h + 1]
            o = a1[:dv] / a1[dv:dv + 1] - lam * (a2[:dv] / a2[dv:dv + 1])
            o = o * lax.rsqrt(jnp.mean(o * o, axis=0, keepdims=True) + RMS_EPS)
            outs.append((o * g) * (1.0 - lam_init))
        o_ref[0] = jnp.concatenate(outs, axis=0).T.astype(BF)


def _diff_prompt_call(qk, vt, lam_par, g_col, wc, lam_init):
    b, s, _ = qk.shape
    nh = wc // (2 * HEAD_DIM)
    t = min(ATTN_TILE, s)
    it, jt = _causal_pairs(s // t)
    wq = 2 * nh * LANES
    grid_spec = pltpu.PrefetchScalarGridSpec(
        num_scalar_prefetch=2,
        grid=(b, it.shape[0]),
        in_specs=[pl.BlockSpec((1, t, wq), lambda b, s, it, jt: (b, it[s], 0)),
                  pl.BlockSpec((1, t, wq), lambda b, s, it, jt: (b, jt[s], 1)),
                  pl.BlockSpec((1, wc + 16, t), lambda b, s, it, jt: (b, 0, jt[s])),
                  pl.BlockSpec(lam_par.shape, lambda b, s, it, jt: (0, 0)),
                  pl.BlockSpec(g_col.shape, lambda b, s, it, jt: (0, 0))],
        out_specs=pl.BlockSpec((1, t, wc), lambda b, s, it, jt: (b, it[s], 0)),
        scratch_shapes=[pltpu.VMEM((2 * nh, 1, t), F32), pltpu.VMEM((2 * nh, 2 * HEAD_DIM + 16, t), F32),
                        pltpu.VMEM((2, t, t), F32)],
    )
    return pl.pallas_call(
        functools.partial(_diff_prompt_body, nh=nh, lam_init=lam_init),
        grid_spec=grid_spec,
        out_shape=jax.ShapeDtypeStruct((b, s, wc), BF),
        compiler_params=_cparams(("arbitrary", "arbitrary")),
        name="diff_prompt",
    )(it, jt, qk, qk, vt, lam_par, g_col)


def _cumsum_body(x_ref, up_ref, o_ref, carry_ref):
    @pl.when(pl.program_id(1) == 0)
    def _():
        carry_ref[...] = jnp.zeros(carry_ref.shape, F32)
    ft = _cumsum_lanes(x_ref[0], up_ref[...]) + carry_ref[:, 0:1]
    o_ref[0] = ft
    carry_ref[...] = jnp.broadcast_to(ft[:, -1:], carry_ref.shape)


def _cumsum_call(x):
    b, nh, length = x.shape
    t = 512
    up = _upper_tri(t)
    return pl.pallas_call(
        _cumsum_body,
        grid=(b, length // t),
        in_specs=[pl.BlockSpec((1, nh, t), lambda b, j: (b, 0, j)), pl.BlockSpec((t, t), lambda b, j: (0, 0))],
        out_specs=pl.BlockSpec((1, nh, t), lambda b, j: (b, 0, j)),
        out_shape=jax.ShapeDtypeStruct(x.shape, F32),
        scratch_shapes=[pltpu.VMEM((nh, LANES), F32)],
        compiler_params=_cparams(("arbitrary", "arbitrary")),
        name="logf_cumsum",
    )(x, up)


def _fox_sample_body(q_ref, kn_ref, vn_ref, ck_ref, cv_ref, ftc_ref, ftn_ref, o_ref, m_ref, l_ref, acc_ref, *, nh):
    j = pl.program_id(1)
    hd = HEAD_DIM
    q = q_ref[0]
    t = q.shape[0]

    @pl.when(j == 0)
    def _():
        _osm_init(m_ref, l_ref, acc_ref)

    k = ck_ref[0].astype(BF)
    v = cv_ref[0].astype(BF)
    ft = ftc_ref[0]
    for h in range(nh):
        sl = slice(h * hd, (h + 1) * hd)
        s = _dot_nt(q[:, sl], k[:, sl]) - ft[h:h + 1, :]
        _osm_update(s, v[:, sl], m_ref, l_ref, acc_ref, h)

    @pl.when(j == pl.num_programs(1) - 1)
    def _():
        kn = kn_ref[0]
        vn = vn_ref[0]
        ftn = ftn_ref[0][:, :t]
        visible = lax.broadcasted_iota(jnp.int32, (t, t), 1) <= lax.broadcasted_iota(jnp.int32, (t, t), 0)
        outs = []
        for h in range(nh):
            sl = slice(h * hd, (h + 1) * hd)
            s = _dot_nt(q[:, sl], kn[:, sl]) - ftn[h:h + 1, :]
            s = jnp.where(visible, s, NEG_INF)
            _osm_update(s, vn[:, sl], m_ref, l_ref, acc_ref, h)
            outs.append((acc_ref[h] / l_ref[h]).astype(BF))
        o_ref[0] = jnp.concatenate(outs, axis=-1)


def _fox_sample_call(qkv, cache_k, cache_v, ft_pad, wa):
    b, t, _ = qkv.shape
    p_len = cache_k.shape[1]
    nh = wa // HEAD_DIM
    tk = min(CACHE_TILE_A, p_len)
    assert p_len % tk == 0 and p_len % LANES == 0 and t <= LANES
    return pl.pallas_call(
        functools.partial(_fox_sample_body, nh=nh),
        grid=(b, p_len // tk),
        in_specs=[pl.BlockSpec((1, t, wa), lambda b, j: (b, 0, 0)),
                  pl.BlockSpec((1, t, wa), lambda b, j: (b, 0, 1)),
                  pl.BlockSpec((1, t, wa), lambda b, j: (b, 0, 2)),
                  pl.BlockSpec((1, tk, wa), lambda b, j: (b, j, 0)),
                  pl.BlockSpec((1, tk, wa), lambda b, j: (b, j, 0)),
                  pl.BlockSpec((1, nh, tk), lambda b, j: (b, 0, j)),
                  pl.BlockSpec((1, nh, LANES), lambda b, j: (b, 0, p_len // LANES))],
        out_specs=pl.BlockSpec((1, t, wa), lambda b, j: (b, 0, 0)),
        out_shape=jax.ShapeDtypeStruct((b, t, wa), BF),
        scratch_shapes=[pltpu.VMEM((nh, t, 1), F32), pltpu.VMEM((nh, t, 1), F32),
                        pltpu.VMEM((nh, t, HEAD_DIM), F32)],
        compiler_params=_cparams(("arbitrary", "arbitrary")),
        name="fox_sample",
    )(qkv, qkv, qkv, cache_k, cache_v, ft_pad, ft_pad)


def _band_sample_body(q_ref, kn_ref, vn_ref, bk_ref, bv_ref, bias_b_ref, bias_n_ref, o_ref, *, nh):
    hd = HEAD_DIM
    q = q_ref[0]
    kn = kn_ref[0]
    vn = vn_ref[0]
    kb = bk_ref[0].astype(BF)
    vb = bv_ref[0].astype(BF)
    outs = []
    for h in range(nh):
        sl = slice(h * hd, (h + 1) * hd)
        sb = _dot_nt(q[:, sl], kb[:, sl]) + bias_b_ref[h]
        sn = _dot_nt(q[:, sl], kn[:, sl]) + bias_n_ref[h]
        m = jnp.maximum(jnp.max(sb, axis=-1, keepdims=True), jnp.max(sn, axis=-1, keepdims=True))
        pb = jnp.exp(sb - m)
        pn = jnp.exp(sn - m)
        l = jnp.sum(pb, axis=-1, keepdims=True) + jnp.sum(pn, axis=-1, keepdims=True)
        o = _dot(pb.astype(BF), vb[:, sl]) + _dot(pn.astype(BF), vn[:, sl])
        outs.append((o / l).astype(BF))
    o_ref[0] = jnp.concatenate(outs, axis=-1)


def _band_sample_call(qkv, buf_k, buf_v, rel_bias, wb):
    b, t, _ = qkv.shape
    lb = buf_k.shape[1]
    nh = wb // HEAD_DIM
    k_pos = jnp.concatenate([jnp.arange(-lb, 0), jnp.arange(t)])
    rel = jnp.arange(t)[:, None] - k_pos[None, :]
    bias = rel_bias[:, jnp.clip(rel, -MAX_REL, MAX_REL) + MAX_REL].astype(F32)
    bias_b, bias_n = bias[:, :, :lb], bias[:, :, lb:]
    return pl.pallas_call(
        functools.partial(_band_sample_body, nh=nh),
        grid=(b,),
        in_specs=[pl.BlockSpec((1, t, wb), lambda b: (b, 0, 3)),
                  pl.BlockSpec((1, t, wb), lambda b: (b, 0, 4)),
                  pl.BlockSpec((1, t, wb), lambda b: (b, 0, 5)),
                  pl.BlockSpec((1, lb, wb), lambda b: (b, 0, 0)),
                  pl.BlockSpec((1, lb, wb), lambda b: (b, 0, 0)),
                  pl.BlockSpec(bias_b.shape, lambda b: (0, 0, 0)),
                  pl.BlockSpec(bias_n.shape, lambda b: (0, 0, 0))],
        out_specs=pl.BlockSpec((1, t, wb), lambda b: (b, 0, 0)),
        out_shape=jax.ShapeDtypeStruct((b, t, wb), BF),
        compiler_params=_cparams(("arbitrary",)),
        name="band_sample",
    )(qkv, qkv, qkv, buf_k, buf_v, bias_b, bias_n)


def _diff_sample_body(q_ref, kn_ref, vn_ref, ck_ref, cv_ref, lam_ref, gsub_ref, o_ref, m_ref, l_ref, acc_ref,
                      *, nh, lam_init, p_len):
    j = pl.program_id(1)
    q = q_ref[0]
    t = q.shape[0]
    dv = 2 * HEAD_DIM
    lane = lax.broadcasted_iota(jnp.int32, (t, dv), 1)

    @pl.when(j == 0)
    def _():
        _osm_init(m_ref, l_ref, acc_ref)

    def attend(k_of, v_of, dist):
        dist2 = jnp.concatenate([dist, dist], axis=0)
        for h in range(nh):
            x = q[:, h * dv:(h + 1) * dv]
            zero = jnp.zeros_like(x)
            q2 = jnp.concatenate([jnp.where(lane < HEAD_DIM, x, zero), jnp.where(lane >= HEAD_DIM, x, zero)], axis=0)
            s = _dot_nt(q2, k_of(h)) - _alibi_slope(h, nh) * dist2
            _osm_update(s, v_of(h), m_ref, l_ref, acc_ref, h)

    tk = ck_ref.shape[1] // nh
    row = lax.broadcasted_iota(jnp.int32, (t, tk), 0)
    col = lax.broadcasted_iota(jnp.int32, (t, tk), 1)
    attend(lambda h: ck_ref[0, pl.ds(h, tk, stride=nh), :].astype(BF),
           lambda h: cv_ref[0, pl.ds(h, tk, stride=nh), :].astype(BF),
           (p_len + row - j * tk - col).astype(F32))

    @pl.when(j == pl.num_programs(1) - 1)
    def _():
        kn = kn_ref[0]
        vn = vn_ref[0]
        rown = lax.broadcasted_iota(jnp.int32, (t, t), 0)
        coln = lax.broadcasted_iota(jnp.int32, (t, t), 1)
        attend(lambda h: kn[:, h * dv:(h + 1) * dv], lambda h: vn[:, h * dv:(h + 1) * dv],
               jnp.abs(rown - coln).astype(F32))
        lam = _lambda(lam_ref, lam_init)
        g = gsub_ref[...]
        outs = []
        for h in range(nh):
            on = acc_ref[h] / l_ref[h]
            o = on[:t] - lam * on[t:]
            o = o * lax.rsqrt(jnp.mean(o * o, axis=-1, keepdims=True) + RMS_EPS)
            outs.append(((o * g) * (1.0 - lam_init)).astype(BF))
        o_ref[0] = jnp.concatenate(outs, axis=-1)


def _diff_sample_call(qkv, cache_k, cache_v, lam_par, g_sub, wc, lam_init):
    b, t, _ = qkv.shape
    nh = wc // (2 * HEAD_DIM)
    p_len = cache_k.shape[1] // nh
    assert (p_len // CHUNK) * CHUNK == p_len and t <= CHUNK
    tk = min(CACHE_TILE_C, p_len)
    return pl.pallas_call(
        functools.partial(_diff_sample_body, nh=nh, lam_init=lam_init, p_len=p_len),
        grid=(b, p_len // tk),
        in_specs=[pl.BlockSpec((1, t, wc), lambda b, j: (b, 0, 0)),
                  pl.BlockSpec((1, t, wc), lambda b, j: (b, 0, 1)),
                  pl.BlockSpec((1, t, wc), lambda b, j: (b, 0, 2)),
                  pl.BlockSpec((1, tk * nh, 2 * HEAD_DIM), lambda b, j: (b, j, 0)),
                  pl.BlockSpec((1, tk * nh, 2 * HEAD_DIM), lambda b, j: (b, j, 0)),
                  pl.BlockSpec(lam_par.shape, lambda b, j: (0, 0)),
                  pl.BlockSpec(g_sub.shape, lambda b, j: (0, 0))],
        out_specs=pl.BlockSpec((1, t, wc), lambda b, j: (b, 0, 0)),
        out_shape=jax.ShapeDtypeStruct((b, t, wc), BF),
        scratch_shapes=[pltpu.VMEM((nh, 2 * t, 1), F32), pltpu.VMEM((nh, 2 * t, 1), F32),
                        pltpu.VMEM((nh, 2 * t, 2 * HEAD_DIM), F32)],
        compiler_params=_cparams(("arbitrary", "arbitrary")),
        name="diff_sample",
    )(qkv, qkv, qkv, cache_k, cache_v, lam_par, g_sub)


def _out_ffn_body(x_ref, oa_ref, ob_ref, gm_ref, g_ref, sh_ref, sc_ref, gf_ref, wo_ref, w1_ref, w3_ref, w2_ref, y_ref,
                  *, f_chunk):
    wa = oa_ref.shape[-1]
    attn = _dot(oa_ref[0], wo_ref[:wa, :]) + _dot(ob_ref[0], wo_ref[wa:, :])
    x1 = x_ref[0] + gm_ref[0, 0, 0] * attn
    h = _modulate(x1, g_ref[...], sh_ref[0, 0, 0], sc_ref[0, 0, 0]).astype(BF)
    f_total = w1_ref.shape[1]
    f = None
    for c0 in range(0, f_total, f_chunk):
        a = _dot(h, w1_ref[:, c0:c0 + f_chunk])
        g = _dot(h, w3_ref[:, c0:c0 + f_chunk])
        u = ((a * _sigmoid(a)) * g).astype(BF)
        part = _dot(u, w2_ref[c0:c0 + f_chunk, :])
        f = part if f is None else f + part
    y_ref[0] = x1 + gf_ref[0, 0, 0] * f


def _out_ffn_call(x, o_a, o_b, gate_m, g, shift, scale, gate_f, wo, w1, w3, w2):
    bx, sx, d = x.shape
    tm = min(ROW_TILE, sx)
    f_total = w1.shape[1]
    f_chunk = f_total // 2 if (f_total // 2) % LANES == 0 else f_total
    row = lambda b, i: (b, i, 0)
    return pl.pallas_call(
        functools.partial(_out_ffn_body, f_chunk=f_chunk),
        grid=(bx, sx // tm),
        in_specs=[pl.BlockSpec((1, tm, d), row),
                  pl.BlockSpec((1, tm, o_a.shape[-1]), row), pl.BlockSpec((1, tm, o_b.shape[-1]), row),
                  _mod_spec(gate_m, tm), _const_spec(g), _mod_spec(shift, tm), _mod_spec(scale, tm),
                  _mod_spec(gate_f, tm), _const_spec(wo), _const_spec(w1), _const_spec(w3), _const_spec(w2)],
        out_specs=pl.BlockSpec((1, tm, d), row),
        out_shape=jax.ShapeDtypeStruct((bx, sx, d), F32),
        compiler_params=_cparams(("arbitrary", "arbitrary")),
        name="out_ffn",
    )(x, o_a, o_b, gate_m[0], g, shift[0], scale[0], gate_f[0], wo, w1, w3, w2)


def _store_tile_rows(ref, x):
    rows, d = x.shape
    nseg = d // LANES
    for s in range(nseg):
        ref[pl.ds(s, rows, stride=nseg), :] = x[:, s * LANES:(s + 1) * LANES]


def _load_tile_rows(ref, rows, d):
    nseg = d // LANES
    return jnp.concatenate([ref[pl.ds(s, rows, stride=nseg), :] for s in range(nseg)], axis=-1)


def _out_router_body(x_ref, o_ref, gm_ref, g_ref, sh_ref, sc_ref, wo_ref, wr_ref, h_in_ref,
                     x1_ref, h_ref, route_ref, *, n_exp):
    del h_in_ref
    x1 = x_ref[0] + gm_ref[0, 0, 0] * _dot(o_ref[0], wo_ref[...])
    x1_ref[0] = x1
    h = _modulate(x1, g_ref[...], sh_ref[0, 0, 0], sc_ref[0, 0, 0])
    hb = h.astype(BF)
    _store_tile_rows(h_ref, hb.astype(F32))
    logits = _dot(hb, wr_ref[0]) + _dot(hb, wr_ref[1]) + _dot(hb, wr_ref[2])
    lt = logits.T[:ROUTE_ROWS, :]
    row = lax.broadcasted_iota(jnp.int32, lt.shape, 0)
    lg = jnp.where(row < n_exp, lt, NEG_INF)
    e = jnp.exp(lg - jnp.max(lg, axis=0, keepdims=True))
    probs = e / jnp.sum(e, axis=0, keepdims=True)
    p1 = jnp.max(probs, axis=0, keepdims=True)
    i1 = jnp.min(jnp.where(probs == p1, row, ROUTE_ROWS), axis=0, keepdims=True)
    rest = jnp.where(row == i1, -1.0, probs)
    p2 = jnp.max(rest, axis=0, keepdims=True)
    i2 = jnp.min(jnp.where(rest == p2, row, ROUTE_ROWS), axis=0, keepdims=True)
    tot = p1 + p2
    route_ref[0] = jnp.where(row == 0, i1.astype(F32),
                             jnp.where(row == 1, i2.astype(F32),
                                       jnp.where(row == 2, p1 / tot, jnp.where(row == 3, p2 / tot, 0.0))))


def _out_router_call(x, o, gate_m, g, shift, scale, wo, wr3, n_exp, tok_offset, h_all):
    bx, sx, d = x.shape
    tm = min(ROW_TILE, sx)
    nt = sx // tm
    nseg = d // LANES
    assert tok_offset % tm == 0 and n_exp <= ROUTE_ROWS
    row = lambda b, i: (b, i, 0)
    return pl.pallas_call(
        functools.partial(_out_router_body, n_exp=n_exp),
        grid=(bx, nt),
        in_specs=[pl.BlockSpec((1, tm, d), row), pl.BlockSpec((1, tm, o.shape[-1]), row),
                  _mod_spec(gate_m, tm), _const_spec(g), _mod_spec(shift, tm), _mod_spec(scale, tm),
                  _const_spec(wo), _const_spec(wr3), pl.BlockSpec(memory_space=pl.ANY)],
        out_specs=[pl.BlockSpec((1, tm, d), row),
                   pl.BlockSpec((tm * nseg, LANES), lambda b, i: (tok_offset // tm + b * nt + i, 0)),
                   pl.BlockSpec((1, ROUTE_ROWS, tm), lambda b, i: (b, 0, i))],
        out_shape=[jax.ShapeDtypeStruct((bx, sx, d), F32),
                   jax.ShapeDtypeStruct(h_all.shape, F32),
                   jax.ShapeDtypeStruct((bx, ROUTE_ROWS, sx), F32)],
        input_output_aliases={8: 1},
        compiler_params=_cparams(("arbitrary", "arbitrary")),
        name="out_router",
    )(x, o, gate_m[0], g, shift[0], scale[0], wo, wr3, h_all)


def _moe_ffn_body(te_ref, nv_ref, idx_ref, idx_next_ref, h_ref, w1_ref, w3_ref, w2_ref, y_ref, xbuf, acc, sem,
                  *, nseg, nf):
    t = pl.program_id(0)
    j = pl.program_id(1)
    nv = nv_ref[0]
    tm, d = acc.shape
    slot = lax.rem(t, 2)

    def issue(rows_ref, s):
        def body(r, c):
            src = pl.multiple_of(rows_ref[0, 0, r], nseg)
            dst = pl.multiple_of(r * nseg, nseg)
            pltpu.make_async_copy(h_ref.at[pl.ds(src, nseg)], xbuf.at[s, pl.ds(dst, nseg)], sem.at[s]).start()
            return c
        lax.fori_loop(0, tm, body, 0, unroll=8)

    @pl.when((j == 0) & (t == 0))
    def _():
        issue(idx_ref, 0)

    @pl.when((j == 0) & (t + 1 < nv))
    def _():
        issue(idx_next_ref, 1 - slot)

    @pl.when((j == 0) & (t < nv))
    def _():
        pltpu.make_async_copy(h_ref.at[pl.ds(0, tm * nseg)], xbuf.at[slot], sem.at[slot]).wait()

    @pl.when(t < nv)
    def _():
        xb = _load_tile_rows(xbuf.at[slot], tm, d).astype(BF)
        a = _dot(xb, w1_ref[0])
        g = _dot(xb, w3_ref[0])
        u = ((a * _sigmoid(a)) * g).astype(BF)
        part = _dot(u, w2_ref[0])

        if nf == 1:
            _store_tile_rows(y_ref, part)
        else:
            @pl.when(j == 0)
            def _():
                acc[...] = part

            @pl.when((j > 0) & (j < nf - 1))
            def _():
                acc[...] += part

            @pl.when(j == nf - 1)
            def _():
                _store_tile_rows(y_ref, acc[...] + part)

    @pl.when((t >= nv) & (j == nf - 1))
    def _():
        y_ref[...] = jnp.zeros(y_ref.shape, F32)


def _moe_ffn_call(h_all, row_tok, tile_e, n_valid, w1, w3, w2):
    n_rows = row_tok.shape[0]
    d = w1.shape[1]
    nseg = d // LANES
    f_total = w1.shape[2]
    tf = MOE_F_TILE if f_total % MOE_F_TILE == 0 else f_total
    nf = f_total // tf
    tm = MOE_ROW_TILE
    n_tiles = n_rows // tm
    idx3 = (row_tok * nseg).reshape(n_tiles, 1, tm)

    def fcol(t, j, te, nv):
        return jnp.where(t < nv[0], j, nf - 1)

    grid_spec = pltpu.PrefetchScalarGridSpec(
        num_scalar_prefetch=2,
        grid=(n_tiles, nf),
        in_specs=[pl.BlockSpec((1, 1, tm), lambda t, j, te, nv: (t, 0, 0), memory_space=pltpu.SMEM),
                  pl.BlockSpec((1, 1, tm), lambda t, j, te, nv: (jnp.minimum(t + 1, n_tiles - 1), 0, 0),
                               memory_space=pltpu.SMEM),
                  pl.BlockSpec(memory_space=pl.ANY),
                  pl.BlockSpec((1, d, tf), lambda t, j, te, nv: (te[t], 0, fcol(t, j, te, nv))),
                  pl.BlockSpec((1, d, tf), lambda t, j, te, nv: (te[t], 0, fcol(t, j, te, nv))),
                  pl.BlockSpec((1, tf, d), lambda t, j, te, nv: (te[t], fcol(t, j, te, nv), 0))],
        out_specs=pl.BlockSpec((tm * nseg, LANES), lambda t, j, te, nv: (t, 0)),
        scratch_shapes=[pltpu.VMEM((2, tm * nseg, LANES), F32), pltpu.VMEM((tm, d), F32),
                        pltpu.SemaphoreType.DMA((2,))],
    )
    return pl.pallas_call(
        functools.partial(_moe_ffn_body, nseg=nseg, nf=nf),
        grid_spec=grid_spec,
        out_shape=jax.ShapeDtypeStruct((n_rows * nseg, LANES), F32),
        compiler_params=_cparams(("arbitrary", "arbitrary")),
        name="moe_ffn",
    )(tile_e, n_valid, idx3, idx3, h_all, w1, w3, w2)


def _combine_body(d0_ref, d1_ref, x_ref, gf_ref, w_ref, g_ref, yb_ref, o_ref, buf, sem):
    t = pl.program_id(0)
    nt = pl.num_programs(0)
    _, tc, d = x_ref.shape
    nseg = d // LANES

    def row_copy(idx_ref, r, slot, k):
        src = pl.multiple_of(idx_ref[0, 0, 2 * r + k], nseg)
        dst = pl.multiple_of(r * nseg, nseg)
        return pltpu.make_async_copy(yb_ref.at[pl.ds(src, nseg)], buf.at[slot, k, pl.ds(dst, nseg)], sem.at[slot])

    def issue(idx_ref, slot):
        def body(r, c):
            row_copy(idx_ref, r, slot, 0).start()
            row_copy(idx_ref, r, slot, 1).start()
            return c
        lax.fori_loop(0, tc, body, 0, unroll=4)

    slot = lax.rem(t, 2)

    @pl.when(t == 0)
    def _():
        issue(d0_ref, 0)

    @pl.when(t + 1 < nt)
    def _():
        issue(d1_ref, 1 - slot)

    for k in range(TOP_K):
        pltpu.make_async_copy(yb_ref.at[pl.ds(0, tc * nseg)], buf.at[slot, k], sem.at[slot]).wait()
    w = jnp.concatenate([w_ref[0], jnp.zeros((LANES - ROUTE_ROWS, tc), F32)], axis=0).T
    f = (w[:, 2:3] * _load_tile_rows(buf.at[slot, 0], tc, d)
         + w[:, 3:4] * _load_tile_rows(buf.at[slot, 1], tc, d))
    y = x_ref[0] + gf_ref[0, 0, 0] * f
    y = y * lax.rsqrt(jnp.mean(y * y, axis=-1, keepdims=True) + RMS_EPS)
    o_ref[0] = y * g_ref[...]


def _combine_call(x1, gate_f, route, g_final, yb, dest):
    bx, sx, d = x1.shape
    nseg = d // LANES
    tc = min(COMBINE_TILE, sx)
    nps = sx // tc
    nt = bx * nps
    dest3 = (dest * nseg).reshape(nt, 1, 2 * tc)
    row = lambda t: (t // nps, t % nps, 0)
    gate_arr, gl, gk = gate_f
    if gate_arr.shape[3] == 1:
        gate_spec = pl.BlockSpec((1, 1, 1, 1, d), lambda t: (gl, gk, t // nps, 0, 0))
    else:
        gate_spec = pl.BlockSpec((1, 1, 1, tc, d), lambda t: (gl, gk, t // nps, t % nps, 0))
    return pl.pallas_call(
        _combine_body,
        grid=(nt,),
        in_specs=[pl.BlockSpec((1, 1, 2 * tc), lambda t: (t, 0, 0), memory_space=pltpu.SMEM),
                  pl.BlockSpec((1, 1, 2 * tc), lambda t: (jnp.minimum(t + 1, nt - 1), 0, 0),
                               memory_space=pltpu.SMEM),
                  pl.BlockSpec((1, tc, d), row), gate_spec,
                  pl.BlockSpec((1, ROUTE_ROWS, tc), lambda t: (t // nps, 0, t % nps)),
                  pl.BlockSpec(g_final.shape, lambda t: (0, 0)),
                  pl.BlockSpec(memory_space=pl.ANY)],
        out_specs=pl.BlockSpec((1, tc, d), row),
        out_shape=jax.ShapeDtypeStruct((bx, sx, d), F32),
        scratch_shapes=[pltpu.VMEM((2, 2, tc * nseg, LANES), F32), pltpu.SemaphoreType.DMA((2,))],
        compiler_params=_cparams(("arbitrary",)),
        name="moe_combine",
    )(dest3, dest3, x1, gate_arr, route, g_final, yb)


def _route_plan(slot_e, n_exp, tm):
    n_slots = slot_e.shape[0]
    onehot = (slot_e[:, None] == jnp.arange(n_exp, dtype=jnp.int32)[None, :]).astype(jnp.int32)
    csum = jnp.cumsum(onehot, axis=0)
    rank = jnp.sum(csum * onehot, axis=1) - 1
    counts = csum[-1]
    padded = ((counts + tm - 1) // tm) * tm
    pad_end = jnp.cumsum(padded)
    pad_start = pad_end - padded
    dest = jnp.sum(onehot * pad_start[None, :], axis=1) + rank
    n_tiles = -(-n_slots // tm) + n_exp
    n_rows = n_tiles * tm
    n_valid = (pad_end[-1] // tm).astype(jnp.int32)
    tile_start = jnp.arange(n_tiles, dtype=jnp.int32) * tm
    tile_e = jnp.minimum(jnp.searchsorted(pad_end, tile_start, side='right'), n_exp - 1).astype(jnp.int32)
    last_e = tile_e[jnp.maximum(n_valid - 1, 0)]
    tile_e = jnp.where(jnp.arange(n_tiles) < n_valid, tile_e, last_e)
    sorted_tok = (jnp.argsort(slot_e, stable=True) // TOP_K).astype(jnp.int32)
    sorted_tok = jnp.concatenate([sorted_tok, jnp.zeros((tm,), jnp.int32)])
    first = jnp.cumsum(counts) - counts - pad_start
    tile_first = jnp.minimum(tile_start + first[tile_e], n_slots)
    row_tok = sorted_tok[(tile_first[:, None] + jnp.arange(tm, dtype=jnp.int32)[None, :]).reshape(n_rows)]
    return dest, row_tok, tile_e, n_valid.reshape(1)


def _moe(parts, h_all, g_final, w1, w3, w2):
    n_exp = w1.shape[0]
    slot_e = jnp.concatenate([r[:, :TOP_K, :].transpose(0, 2, 1).reshape(-1) for _, r, _ in parts], axis=0)
    dest, row_tok, tile_e, n_valid = _route_plan(slot_e.astype(jnp.int32), n_exp, MOE_ROW_TILE)
    yb = _moe_ffn_call(h_all, row_tok, tile_e, n_valid, w1, w3, w2)
    outs = []
    first = 0
    for x1, route, gate_f in parts:
        n_slots = x1.shape[0] * x1.shape[1] * TOP_K
        outs.append(_combine_call(x1, gate_f, route, g_final, yb, dest[first:first + n_slots]))
        first += n_slots
    return outs


def _trunk(x, mods, p, cache, tok_offset, h_all):
    sh_m0, sc_m0, gt_m0, sh_f0, sc_f0, gt_f0 = mods[0]
    sh_m1, sc_m1, gt_m1, sh_f1, sc_f1, gt_f1 = mods[1]
    wa = p['wa']
    wb = p['wb']
    wc = p['wc']
    bx, sx, d = x.shape

    if cache is None:
        keep = min(WINDOW_B, sx)
        qk, vt, ka, va, kb, vb, logf = _inproj_even_call(
            x, p['g_mix0'], sh_m0, sc_m0, p['w_in_even'], p['w_f'], p['b_f'], keep, True)
        o_a = _fox_prompt_call(qk, vt, wa)
        o_b = _band_prompt_call(qk, vt, p['rel_bias'], wa, wb)
        x_l0_state = (ka, va, logf, kb, vb)
    else:
        ck, cv, clogf, bk, bv, cck, ccv, b, t = cache
        qkv, ka, va, kb, vb, logf, lft, = _inproj_even_call(
            x, p['g_mix0'], sh_m0, sc_m0, p['w_in_even'], p['w_f'], p['b_f'], sx, False)
        nh = wa // HEAD_DIM
        p_len = ck.shape[1]
        lft_new = lft.reshape(nh, b, t).transpose(1, 0, 2)
        lp = -(-(p_len + t) // 512) * 512
        lcat = jnp.concatenate([clogf.transpose(0, 2, 1), lft_new,
                                jnp.zeros((b, nh, lp - p_len - t), F32)], axis=-1)
        ft_pad = _cumsum_call(lcat)
        qkv_b = qkv.reshape(b, t, qkv.shape[-1])
        o_a = _fox_sample_call(qkv_b, ck, cv, ft_pad, wa).reshape(1, b * t, wa)
        o_b = _band_sample_call(qkv_b, bk, bv, p['rel_bias'], wb).reshape(1, b * t, wb)
        x_l0_state = (ka, va, logf, kb, vb)
    x = _out_ffn_call(x, o_a, o_b, gt_m0, p['g_ffn0'], sh_f0, sc_f0, gt_f0,
                      p['w_out_even'], p['w1_dense'], p['w3_dense'], p['w2_dense'])

    if cache is None:
        qk, vt, kc, vc = _inproj_odd_call(x, p['g_mix1'], sh_m1, sc_m1, p['w_in_odd'], True)
        o = _diff_prompt_call(qk, vt, p['lam_par'], p['g_sub'].reshape(-1, 1), wc, p['lam_init'])
    else:
        qkv, kc, vc = _inproj_odd_call(x, p['g_mix1'], sh_m1, sc_m1, p['w_in_odd'], False)
        qkv_b = qkv.reshape(b, t, qkv.shape[-1])
        o = _diff_sample_call(qkv_b, cck, ccv, p['lam_par'], p['g_sub'], wc, p['lam_init']).reshape(1, b * t, wc)
    x1, h_all, route = _out_router_call(x, o, gt_m1, p['g_ffn1'], sh_f1, sc_f1, p['w_out_odd'], p['w_router3'],
                                        p['n_exp'], tok_offset, h_all)
    return (x1, route, gt_f1), h_all, x_l0_state, (kc, vc)


def kernel(x_prompt, x_sample, cache_a_k, cache_a_v, cache_a_logf, cache_b_k, cache_b_v, cache_c_k, cache_c_v,
           c_prompt, c_sample, w_mod, b_mod, g_mix, g_ffn, g_final, w_in_even, b_forget, rel_bias, w_out_even,
           w_in_odd, lambda_q1, lambda_k1, lambda_q2, lambda_k2, g_subln, w_out_odd, w1_dense, w3_dense, w2_dense,
           w_router, w1_moe, w3_moe, w2_moe):
    bp, sp, d = x_prompt.shape
    bs, ts, _ = x_sample.shape
    h_a = cache_a_k.shape[3]
    h_b = cache_b_k.shape[3]
    h_c = cache_c_k.shape[3]
    wa, wb, wc = h_a * HEAD_DIM, h_b * HEAD_DIM, h_c * 2 * HEAD_DIM
    n_exp = w_router.shape[-1]
    assert w_mod.shape[0] == 2, "kernel is written for the 2-layer trunk"

    we = w_in_even[0]
    w_main = jnp.concatenate([we[:, :3 * wa], we[:, 3 * wa + h_a:]], axis=1).astype(BF)
    w_f = jnp.pad(we[:, 3 * wa:3 * wa + h_a], ((0, 0), (0, LANES - h_a))).astype(BF)
    b_f = jnp.pad(b_forget[0], (0, LANES - h_a)).reshape(1, LANES).astype(F32)
    wr = jnp.pad(w_router[0], ((0, 0), (0, LANES - n_exp)))
    wr_hi = wr.astype(BF)
    wr_r1 = wr - wr_hi.astype(F32)
    wr_mid = wr_r1.astype(BF)
    wr_lo = (wr_r1 - wr_mid.astype(F32)).astype(BF)
    params = dict(
        wa=wa, wb=wb, wc=wc, n_exp=n_exp, lam_init=0.8 - 0.6 * math.exp(-0.3 * 1),
        g_mix0=g_mix[0:1], g_mix1=g_mix[1:2], g_ffn0=g_ffn[0:1], g_ffn1=g_ffn[1:2],
        g_final=g_final.reshape(1, d),
        w_in_even=w_main, w_f=w_f, b_f=b_f, rel_bias=rel_bias[0],
        w_out_even=w_out_even[0].astype(BF), w_in_odd=w_in_odd[0].astype(BF),
        lam_par=jnp.concatenate([lambda_q1, lambda_k1, lambda_q2, lambda_k2], axis=0),
        g_sub=g_subln[0:1], w_out_odd=w_out_odd[0].astype(BF),
        w1_dense=w1_dense[0].astype(BF), w3_dense=w3_dense[0].astype(BF), w2_dense=w2_dense[0].astype(BF),
        w_router3=jnp.stack([wr_hi, wr_mid, wr_lo]),
        w1_moe=_cast_bf16_call(w1_moe[0]), w3_moe=_cast_bf16_call(w3_moe[0]), w2_moe=_cast_bf16_call(w2_moe[0]),
    )

    mod = _mod_call(jnp.concatenate([c_prompt, c_sample], axis=0), w_mod, b_mod)

    def mods_for(rows, per_row_len):
        m = mod[:, rows].reshape(2, -1, 6, d).transpose(0, 2, 1, 3)
        if per_row_len:
            m = jnp.repeat(m, per_row_len, axis=2)[:, :, None]
        else:
            m = m[:, :, :, None]
        return [[(m, l, k) for k in range(6)] for l in range(2)]

    h_all = jnp.zeros(((bp * sp + bs * ts) * (d // LANES), LANES), F32)
    moe_p, h_all, ev_p, od_p = _trunk(x_prompt, mods_for(slice(0, bp), 0), params, None, 0, h_all)
    cache = (cache_a_k[0].reshape(bs, -1, wa), cache_a_v[0].reshape(bs, -1, wa), cache_a_logf[0],
             cache_b_k[0].reshape(bs, -1, wb), cache_b_v[0].reshape(bs, -1, wb),
             cache_c_k[0].reshape(bs, -1, 2 * HEAD_DIM), cache_c_v[0].reshape(bs, -1, 2 * HEAD_DIM), bs, ts)
    moe_s, h_all, ev_s, od_s = _trunk(x_sample.reshape(1, bs * ts, d), mods_for(slice(bp, bp + bs), ts), params,
                                      cache, bp * sp, h_all)
    y_p, y_s = _moe([moe_p, moe_s], h_all, params['g_final'], params['w1_moe'], params['w3_moe'], params['w2_moe'])

    ka, va, logf, kb, vb = ev_p
    keep = kb.shape[1]
    out_p = (ka.reshape(1, bp, sp, h_a, HEAD_DIM), va.reshape(1, bp, sp, h_a, HEAD_DIM),
             logf.reshape(1, bp, sp, h_a),
             kb.reshape(1, bp, keep, h_b, HEAD_DIM), vb.reshape(1, bp, keep, h_b, HEAD_DIM),
             od_p[0].reshape(1, bp, sp, h_c, 2 * HEAD_DIM), od_p[1].reshape(1, bp, sp, h_c, 2 * HEAD_DIM))
    ka, va, logf, kb, vb = ev_s
    new_bk = jnp.concatenate([cache_b_k[0], kb.reshape(bs, ts, h_b, HEAD_DIM)], axis=1)[:, ts:]
    new_bv = jnp.concatenate([cache_b_v[0], vb.reshape(bs, ts, h_b, HEAD_DIM)], axis=1)[:, ts:]
    out_s = (ka.reshape(1, bs, ts, h_a, HEAD_DIM), va.reshape(1, bs, ts, h_a, HEAD_DIM),
             logf.reshape(1, bs, ts, h_a), new_bk[None], new_bv[None],
             od_s[0].reshape(1, bs, ts, h_c, 2 * HEAD_DIM), od_s[1].reshape(1, bs, ts, h_c, 2 * HEAD_DIM))
    return (y_p, y_s.reshape(bs, ts, d)) + out_p + out_s
```

```python
import functools
import math

import jax
import jax.numpy as jnp
import numpy as np
from jax import lax
from jax.experimental import pallas as pl
from jax.experimental.pallas import tpu as pltpu

BF = jnp.bfloat16
F32 = jnp.float32

CHUNK = 64
CHUNK_SHIFT = 6
HEAD_DIM = 64
BAND_CHUNKS = 8
WINDOW_B = BAND_CHUNKS * CHUNK
MAX_REL = 128
TOP_K = 2
RMS_EPS = 1e-6
NEG_INF = -1e30
QK_SCALE = HEAD_DIM ** -0.5
LOG2E = math.log2(math.e)
PROMPT_Q_SCALE = QK_SCALE * LOG2E


def _bf16_terms(c, n):
    terms = []
    for _ in range(n):
        t = float(np.asarray(c, dtype=jnp.bfloat16))
        terms.append(t)
        c -= t
    return terms


LOG2E_TERMS = _bf16_terms(LOG2E, 3)

LANES = 128
SUBLANES = 8
VMEM_LIMIT_BYTES = 56 * 1024 * 1024
ROUTE_ROWS = SUBLANES

ROW_TILE = 512
ATTN_TILE = 512
BAND_Q_TILE = WINDOW_B // 2
CACHE_TILE_A = 1024
CACHE_TILE_C = 1024
MOE_ROW_TILE = 512
MOE_F_TILE = 1792
COMBINE_TILE = 256
CAST_ROWS = 512


def _cparams(sem):
    return pltpu.CompilerParams(dimension_semantics=sem, vmem_limit_bytes=VMEM_LIMIT_BYTES)


def _dot(a, b):
    return jnp.dot(a, b, preferred_element_type=F32)


def _dot_nt(a, b):
    return lax.dot_general(a, b, (((1,), (1,)), ((), ())), preferred_element_type=F32)


def _sigmoid(x):
    return 1.0 / (1.0 + jnp.exp(-x))


def _modulate(x, g, shift, scale):
    y = x * lax.rsqrt(jnp.mean(x * x, axis=-1, keepdims=True) + RMS_EPS)
    return (y * g) * (1.0 + scale) + shift


def _split3(x):
    hi = x.astype(BF)
    r1 = x - hi.astype(F32)
    mid = r1.astype(BF)
    lo = (r1 - mid.astype(F32)).astype(BF)
    return hi, mid, lo


def _cumsum_lanes(x, upper):
    hi, mid, lo = _split3(x)
    return _dot(hi, upper) + _dot(mid, upper) + _dot(lo, upper)


def _upper_tri(t):
    r = jnp.arange(t)
    return (r[:, None] <= r[None, :]).astype(BF)


def _osm_update(s, v, m_ref, l_ref, acc_ref, idx):
    m_prev = m_ref[idx]
    m_new = jnp.maximum(m_prev, jnp.max(s, axis=-1, keepdims=True))
    alpha = jnp.exp(m_prev - m_new)
    p = jnp.exp(s - m_new)
    l_ref[idx] = alpha * l_ref[idx] + jnp.sum(p, axis=-1, keepdims=True)
    acc_ref[idx] = alpha * acc_ref[idx] + _dot(p.astype(BF), v)
    m_ref[idx] = m_new


def _osm_init(m_ref, l_ref, acc_ref):
    m_ref[...] = jnp.full(m_ref.shape, NEG_INF, F32)
    l_ref[...] = jnp.zeros(l_ref.shape, F32)
    acc_ref[...] = jnp.zeros(acc_ref.shape, F32)


def _cast_body(x_ref, o_ref):
    o_ref[...] = x_ref[...].astype(BF)


def _cast_bf16_call(w):
    e, k, n = w.shape
    tk = CAST_ROWS if k % CAST_ROWS == 0 else k
    spec = pl.BlockSpec((1, tk, n), lambda e, i: (e, i, 0))
    return pl.pallas_call(
        _cast_body, grid=(e, k // tk), in_specs=[spec], out_specs=spec,
        out_shape=jax.ShapeDtypeStruct(w.shape, BF),
        compiler_params=_cparams(("arbitrary", "arbitrary")),
        name="cast_bf16",
    )(w)


def _mod_body(c_ref, w_ref, b_ref, o_ref):
    c = c_ref[...]
    s = (c * _sigmoid(c)).astype(BF)
    o_ref[0] = _dot(s, w_ref[0].astype(BF)) + b_ref[0]


def _mod_call(c_all, w_mod, b_mod):
    depth, d, n = w_mod.shape
    r = c_all.shape[0]
    tn = 1536 if n % 1536 == 0 else n
    return pl.pallas_call(
        _mod_body,
        grid=(depth, n // tn),
        in_specs=[pl.BlockSpec((r, d), lambda l, j: (0, 0)),
                  pl.BlockSpec((1, d, tn), lambda l, j: (l, 0, j)),
                  pl.BlockSpec((1, 1, tn), lambda l, j: (l, 0, j))],
        out_specs=pl.BlockSpec((1, r, tn), lambda l, j: (l, 0, j)),
        out_shape=jax.ShapeDtypeStruct((depth, r, n), F32),
        compiler_params=_cparams(("arbitrary", "arbitrary")),
        name="mod",
    )(c_all, w_mod, b_mod.reshape(depth, 1, n))


def _mod_spec(m, tm):
    arr, l, k = m
    d = arr.shape[-1]
    if arr.shape[3] == 1:
        return pl.BlockSpec((1, 1, 1, 1, d), lambda b, i: (l, k, b, 0, 0))
    return pl.BlockSpec((1, 1, 1, tm, d), lambda b, i: (l, k, b, i, 0))


def _const_spec(arr):
    nd = arr.ndim
    return pl.BlockSpec(arr.shape, lambda b, i: (0,) * nd, pipeline_mode=pl.Buffered(1))


def _head_blocks(x, augs):
    tm, w = x.shape
    first = lax.broadcasted_iota(jnp.int32, (tm, LANES), 1) < HEAD_DIM
    out = []
    for p in range(w // LANES):
        pair = x[:, p * LANES:(p + 1) * LANES]
        out.append(jnp.where(first, pair, augs[2 * p]).astype(BF))
        out.append(jnp.where(first, pltpu.roll(pair, HEAD_DIM, 1), augs[2 * p + 1]).astype(BF))
    return out


def _lane_range_row(lo, hi, value):
    lane = lax.broadcasted_iota(jnp.int32, (1, LANES), 1)
    return jnp.where((lane >= lo) & (lane < hi), value, 0.0).astype(F32)


def _ones_rows(t):
    return jnp.where(lax.broadcasted_iota(jnp.int32, (16, t), 0) == 0, 1.0, 0.0).astype(BF)


def _log_forget(h, wf_ref, bf_ref):
    z = _dot(h, wf_ref[...]) + bf_ref[...]
    return jnp.minimum(z, 0.0) - jnp.log1p(jnp.exp(-jnp.abs(z)))


def _inproj_even_body(*refs, wa, wb, prompt):
    if prompt:
        (x_ref, g_ref, sh_ref, sc_ref, w_ref, wf_ref, bf_ref, low_ref, place_ref,
         qk_ref, vt_ref, ka_ref, va_ref, kb_ref, vb_ref, logf_ref, carry_ref) = refs
    else:
        (x_ref, g_ref, sh_ref, sc_ref, w_ref, wf_ref, bf_ref,
         qkv_ref, ka_ref, va_ref, kb_ref, vb_ref, logf_ref, lft_ref) = refs
    i = pl.program_id(1)
    h = _modulate(x_ref[0], g_ref[...], sh_ref[0, 0, 0], sc_ref[0, 0, 0]).astype(BF)
    tm = h.shape[0]
    nh = logf_ref.shape[-1]

    qa = _dot(h, w_ref[:, 0:wa])
    ka = _dot(h, w_ref[:, wa:2 * wa])
    va = _dot(h, w_ref[:, 2 * wa:3 * wa])
    o = 3 * wa
    qb = _dot(h, w_ref[:, o:o + wb])
    kb = _dot(h, w_ref[:, o + wb:o + 2 * wb])
    vb = _dot(h, w_ref[:, o + 2 * wb:o + 3 * wb])
    ka_ref[0] = ka
    va_ref[0] = va
    kb_ref[0] = kb
    vb_ref[0] = vb
    logf = _log_forget(h, wf_ref, bf_ref)
    logf_ref[0] = logf[:, :nh]

    if not prompt:
        qkv_ref[0] = jnp.concatenate(
            [(qa * QK_SCALE).astype(BF), ka.astype(BF), va.astype(BF),
             (qb * QK_SCALE).astype(BF), kb.astype(BF), vb.astype(BF)], axis=-1)
        lft_ref[0] = logf.T[:nh, :]
        return

    @pl.when(i == 0)
    def _():
        carry_ref[...] = jnp.zeros(carry_ref.shape, F32)

    terms = _dot(low_ref[...], jnp.concatenate(_split3(logf), axis=1))
    f_cum = terms[:, :LANES] + terms[:, LANES:2 * LANES] + terms[:, 2 * LANES:] + carry_ref[0:1, :]
    carry_ref[...] = jnp.broadcast_to(f_cum[tm - 1:tm, :], carry_ref.shape)
    k_aug = _dot(jnp.concatenate(_split3(-f_cum), axis=1), place_ref[...])
    q_log2e = sum(_lane_range_row(HEAD_DIM + 3 * r, HEAD_DIM + 3 * r + 3, c) for r, c in enumerate(LOG2E_TERMS))
    zero = jnp.zeros((1, LANES), F32)
    nhb = wb // HEAD_DIM
    blocks = (_head_blocks(qa * PROMPT_Q_SCALE, [q_log2e] * nh)
              + _head_blocks(ka, [k_aug[:, n * LANES:(n + 1) * LANES] for n in range(nh)])
              + _head_blocks(qb * PROMPT_Q_SCALE, [zero] * nhb)
              + _head_blocks(kb, [zero] * nhb))
    qk_ref[0] = jnp.concatenate(blocks, axis=-1)
    ones = _ones_rows(tm)
    vt_ref[0] = jnp.concatenate([va.T.astype(BF), ones, vb.T.astype(BF), ones], axis=0)


def _lower_tri(t):
    r = jnp.arange(t)
    return (r[:, None] >= r[None, :]).astype(BF)


def _place_matrices(nh):
    s = jnp.arange(3)[:, None, None]
    r = jnp.arange(LANES)[None, :, None]
    c = jnp.arange(nh * LANES)[None, None, :]
    off = c - r * LANES - HEAD_DIM - s
    return ((r < nh) & ((off == 0) | (off == 3) | (off == 6))).astype(BF).reshape(3 * LANES, nh * LANES)


def _inproj_even_call(x, g, shift, scale, w_main, w_f, b_f, keep, prompt):
    bx, sx, d = x.shape
    tm = min(ROW_TILE, sx)
    nt = sx // tm
    n = w_main.shape[1]
    wa = wb = n // 6
    nh = wa // HEAD_DIM
    nkeep = keep // tm
    tail = lambda b, i: (b, jnp.maximum(i - (nt - nkeep), 0), 0)
    row = lambda b, i: (b, i, 0)
    in_specs = [pl.BlockSpec((1, tm, d), row), _const_spec(g), _mod_spec(shift, tm), _mod_spec(scale, tm),
                _const_spec(w_main), _const_spec(w_f), _const_spec(b_f)]
    args = [x, g, shift[0], scale[0], w_main, w_f, b_f]
    f32_specs = [pl.BlockSpec((1, tm, wa), row), pl.BlockSpec((1, tm, wa), row),
                 pl.BlockSpec((1, tm, wb), tail), pl.BlockSpec((1, tm, wb), tail),
                 pl.BlockSpec((1, tm, nh), row)]
    f32_shapes = [jax.ShapeDtypeStruct((bx, sx, wa), F32), jax.ShapeDtypeStruct((bx, sx, wa), F32),
                  jax.ShapeDtypeStruct((bx, keep, wb), F32), jax.ShapeDtypeStruct((bx, keep, wb), F32),
                  jax.ShapeDtypeStruct((bx, sx, nh), F32)]
    scratch = []
    if prompt:
        low, place = _lower_tri(tm), _place_matrices(nh)
        in_specs += [_const_spec(low), _const_spec(place)]
        args += [low, place]
        nqk = 2 * (wa + wb) // HEAD_DIM * LANES
        nvt = wa + wb + 32
        out_specs = [pl.BlockSpec((1, tm, nqk), row), pl.BlockSpec((1, nvt, tm), lambda b, i: (b, 0, i))] + f32_specs
        out_shape = [jax.ShapeDtypeStruct((bx, sx, nqk), BF), jax.ShapeDtypeStruct((bx, nvt, sx), BF)] + f32_shapes
        scratch.append(pltpu.VMEM((8, LANES), F32))
    else:
        out_specs = [pl.BlockSpec((1, tm, n), row)] + f32_specs + [pl.BlockSpec((1, nh, tm), lambda b, i: (b, 0, i))]
        out_shape = [jax.ShapeDtypeStruct((bx, sx, n), BF)] + f32_shapes + [jax.ShapeDtypeStruct((bx, nh, sx), F32)]
    return pl.pallas_call(
        functools.partial(_inproj_even_body, wa=wa, wb=wb, prompt=prompt),
        grid=(bx, nt), in_specs=in_specs, out_specs=out_specs, out_shape=out_shape,
        scratch_shapes=scratch,
        compiler_params=_cparams(("arbitrary", "arbitrary")),
        name="inproj_even",
    )(*args)


def _inproj_odd_body(*refs, wc, prompt):
    if prompt:
        x_ref, g_ref, sh_ref, sc_ref, w_ref, qk_ref, vt_ref, k_ref, v_ref = refs
    else:
        x_ref, g_ref, sh_ref, sc_ref, w_ref, qkv_ref, k_ref, v_ref = refs
    i = pl.program_id(1)
    h = _modulate(x_ref[0], g_ref[...], sh_ref[0, 0, 0], sc_ref[0, 0, 0]).astype(BF)
    tm = h.shape[0]
    q = _dot(h, w_ref[:, 0:wc])
    k = _dot(h, w_ref[:, wc:2 * wc])
    v = _dot(h, w_ref[:, 2 * wc:3 * wc])
    k_ref[0] = k
    v_ref[0] = v
    if not prompt:
        qkv_ref[0] = jnp.concatenate([(q * QK_SCALE).astype(BF), k.astype(BF), v.astype(BF)], axis=-1)
        return
    nh = wc // LANES
    pos = (i * tm + lax.broadcasted_iota(jnp.int32, (tm, LANES), 0)).astype(F32)
    pos_hi = pos.astype(BF).astype(F32)
    lane = lax.broadcasted_iota(jnp.int32, (tm, LANES), 1) - HEAD_DIM
    in_aug = (lane >= 0) & (lane < 2 * len(LOG2E_TERMS))
    k_aug = jnp.where(in_aug, jnp.where((lane & 1) == 0, pos_hi, pos - pos_hi), 0.0)
    q_augs = []
    for hh in range(nh):
        slope = _alibi_slope(hh, nh)
        row = sum(_lane_range_row(HEAD_DIM + 2 * r, HEAD_DIM + 2 * r + 2, slope * c) for r, c in enumerate(LOG2E_TERMS))
        q_augs += [row] * 2
    blocks = _head_blocks(q * PROMPT_Q_SCALE, q_augs) + _head_blocks(k, [k_aug] * (2 * nh))
    qk_ref[0] = jnp.concatenate(blocks, axis=-1)
    vt_ref[0] = jnp.concatenate([v.T.astype(BF), _ones_rows(tm)], axis=0)


def _alibi_slope(h, nh):
    assert 8 % nh == 0, "slopes must be exact powers of two to ride in bf16"
    return 2.0 ** (-8.0 * (h + 1) / nh)


def _inproj_odd_call(x, g, shift, scale, w, prompt):
    bx, sx, d = x.shape
    tm = min(ROW_TILE, sx)
    n = w.shape[1]
    wc = n // 3
    row = lambda b, i: (b, i, 0)
    f32_specs = [pl.BlockSpec((1, tm, wc), row), pl.BlockSpec((1, tm, wc), row)]
    f32_shapes = [jax.ShapeDtypeStruct((bx, sx, wc), F32), jax.ShapeDtypeStruct((bx, sx, wc), F32)]
    if prompt:
        nqk = 2 * wc // HEAD_DIM * LANES
        out_specs = [pl.BlockSpec((1, tm, nqk), row), pl.BlockSpec((1, wc + 16, tm), lambda b, i: (b, 0, i))]
        out_shape = [jax.ShapeDtypeStruct((bx, sx, nqk), BF), jax.ShapeDtypeStruct((bx, wc + 16, sx), BF)]
    else:
        out_specs = [pl.BlockSpec((1, tm, n), row)]
        out_shape = [jax.ShapeDtypeStruct((bx, sx, n), BF)]
    return pl.pallas_call(
        functools.partial(_inproj_odd_body, wc=wc, prompt=prompt),
        grid=(bx, sx // tm),
        in_specs=[pl.BlockSpec((1, tm, d), row), _const_spec(g), _mod_spec(shift, tm), _mod_spec(scale, tm),
                  _const_spec(w)],
        out_specs=out_specs + f32_specs, out_shape=out_shape + f32_shapes,
        compiler_params=_cparams(("arbitrary", "arbitrary")),
        name="inproj_odd",
    )(x, g, shift[0], scale[0], w)


def _tflash_update(st, v_aug, m_ref, acc_ref, idx):
    m_prev = m_ref[idx]
    m_new = jnp.maximum(m_prev, jnp.max(st, axis=0, keepdims=True))
    alpha = jnp.exp2(m_prev - m_new)
    p = jnp.exp2(st - m_new).astype(BF)
    acc_ref[idx] = alpha * acc_ref[idx] + _dot(v_aug, p)
    m_ref[idx] = m_new


def _diag_scores(st_ref, slot, k_blk, q_blk):
    hf = k_blk.shape[0] // 2
    st_ref[slot, :hf, :] = _dot_nt(k_blk[:hf], q_blk)
    st_ref[slot, hf:, hf:] = _dot_nt(k_blk[hf:], q_blk[hf:])


def _tflash_diag_update(st_ref, slot, v_aug, fix_lo, fix_hi, m_ref, acc_ref, idx):
    hf = st_ref.shape[1] // 2
    lo = fix_lo(st_ref[slot, :hf, :])
    hi = fix_hi(st_ref[slot, hf:, hf:])
    m_prev = m_ref[idx]
    m_lo = jnp.maximum(m_prev, jnp.max(lo, axis=0, keepdims=True))
    m_new = jnp.concatenate([m_lo[:, :hf], jnp.maximum(m_lo[:, hf:], jnp.max(hi, axis=0, keepdims=True))], axis=1)
    alpha = jnp.exp2(m_prev - m_new)
    p_lo = jnp.exp2(lo - m_new).astype(BF)
    p_hi = jnp.exp2(hi - m_new[:, hf:]).astype(BF)
    upd = _dot(v_aug[:, :hf], p_lo)
    upd = jnp.concatenate([upd[:, :hf], upd[:, hf:] + _dot(v_aug[:, hf:], p_hi)], axis=1)
    acc_ref[idx] = alpha * acc_ref[idx] + upd
    m_ref[idx] = m_new


def _tflash_init(m_ref, acc_ref):
    m_ref[...] = jnp.full(m_ref.shape, NEG_INF, F32)
    acc_ref[...] = jnp.zeros(acc_ref.shape, F32)


def _causal_pairs(nq):
    pairs = [(i, j) for i in range(nq) for j in range(i + 1)]
    return jnp.asarray([p[0] for p in pairs], jnp.int32), jnp.asarray([p[1] for p in pairs], jnp.int32)


def _fox_prompt_body(it_ref, jt_ref, q_ref, k_ref, vt_ref, o_ref, m_ref, acc_ref, st_ref, *, nh):
    i = it_ref[pl.program_id(1)]
    j = jt_ref[pl.program_id(1)]
    hd = HEAD_DIM

    @pl.when(j == 0)
    def _():
        _tflash_init(m_ref, acc_ref)

    def step(diag):
        q = q_ref[0]
        k = k_ref[0]
        vt = vt_ref[0]
        t = q.shape[0]
        ones = vt[nh * hd:, :]
        if diag:
            causal = lambda s: jnp.where(lax.broadcasted_iota(jnp.int32, s.shape, 0)
                                         <= lax.broadcasted_iota(jnp.int32, s.shape, 1), s, NEG_INF)

        def scores(h):
            k_blk, q_blk = k[:, h * LANES:(h + 1) * LANES], q[:, h * LANES:(h + 1) * LANES]
            if diag:
                _diag_scores(st_ref, h % 2, k_blk, q_blk)
            else:
                st_ref[h % 2] = _dot_nt(k_blk, q_blk)

        scores(0)
        for h in range(nh):
            if h + 1 < nh:
                scores(h + 1)
            v_aug = jnp.concatenate([vt[h * hd:(h + 1) * hd, :], ones], axis=0)
            if diag:
                _tflash_diag_update(st_ref, h % 2, v_aug, causal, causal, m_ref, acc_ref, h)
            else:
                _tflash_update(st_ref[h % 2], v_aug, m_ref, acc_ref, h)

    @pl.when(j < i)
    def _():
        step(False)

    @pl.when(j == i)
    def _():
        step(True)
        ot = jnp.concatenate([acc_ref[h][:hd] / acc_ref[h][hd:hd + 1] for h in range(nh)], axis=0)
        o_ref[0] = ot.T.astype(BF)


def _fox_prompt_call(qk, vt, wa):
    b, s, _ = qk.shape
    nh = wa // HEAD_DIM
    t = min(ATTN_TILE, s)
    it, jt = _causal_pairs(s // t)
    wq = nh * LANES
    grid_spec = pltpu.PrefetchScalarGridSpec(
        num_scalar_prefetch=2,
        grid=(b, it.shape[0]),
        in_specs=[pl.BlockSpec((1, t, wq), lambda b, s, it, jt: (b, it[s], 0)),
                  pl.BlockSpec((1, t, wq), lambda b, s, it, jt: (b, jt[s], 1)),
                  pl.BlockSpec((1, wa + 16, t), lambda b, s, it, jt: (b, 0, jt[s]))],
        out_specs=pl.BlockSpec((1, t, wa), lambda b, s, it, jt: (b, it[s], 0)),
        scratch_shapes=[pltpu.VMEM((nh, 1, t), F32), pltpu.VMEM((nh, HEAD_DIM + 16, t), F32),
                        pltpu.VMEM((2, t, t), F32)],
    )
    return pl.pallas_call(
        functools.partial(_fox_prompt_body, nh=nh),
        grid_spec=grid_spec,
        out_shape=jax.ShapeDtypeStruct((b, s, wa), BF),
        compiler_params=_cparams(("arbitrary", "arbitrary")),
        name="fox_prompt",
    )(it, jt, qk, qk, vt)


def _band_prompt_body(q_ref, k0_ref, k1_ref, k2_ref, v0_ref, v1_ref, v2_ref, bias_ref, o_ref, st_ref, *, nh):
    i = pl.program_id(1)
    hd = HEAD_DIM
    q = q_ref[0]
    tq = q.shape[0]
    k = jnp.concatenate([k0_ref[0], k1_ref[0], k2_ref[0]], axis=0)
    vts = [v0_ref[0], v1_ref[0], v2_ref[0]]

    def scores(h):
        st_ref[h % 2] = _dot_nt(k[:, h * LANES:(h + 1) * LANES], q[:, h * LANES:(h + 1) * LANES])

    def attend(at_start):
        if at_start:
            in_seq = lax.broadcasted_iota(jnp.int32, (3 * tq, tq), 0) >= (2 - i) * tq
        outs = []
        scores(0)
        for h in range(nh):
            if h + 1 < nh:
                scores(h + 1)
            st = st_ref[h % 2] + bias_ref[h]
            if at_start:
                st = jnp.where(in_seq, st, NEG_INF)
            p = jnp.exp2(st - jnp.max(st, axis=0, keepdims=True)).astype(BF)
            acc = None
            for w, vt in enumerate(vts):
                v_aug = jnp.concatenate([vt[h * hd:(h + 1) * hd, :], vt[nh * hd:, :]], axis=0)
                part = _dot(v_aug, p[w * tq:(w + 1) * tq, :])
                acc = part if acc is None else acc + part
            outs.append(acc[:hd] / acc[hd:hd + 1])
        o_ref[0] = jnp.concatenate(outs, axis=0).T.astype(BF)

    @pl.when(i < 2)
    def _():
        attend(True)

    @pl.when(i >= 2)
    def _():
        attend(False)


def _band_bias_prompt(rel_bias, tq):
    nh = rel_bias.shape[0]
    period = 4 * tq
    t = jnp.arange(period)
    t = jnp.where(t < 3 * tq, t, t - period)
    u = rel_bias[:, jnp.clip(2 * tq - t, -MAX_REL, MAX_REL) + MAX_REL].astype(F32)
    skew = jnp.tile(u, (1, tq))[:, :tq * (period - 1)].reshape(nh, tq, period - 1)
    bias = skew[:, :, :3 * tq]
    r = jnp.arange(tq)[:, None]
    c = jnp.arange(3 * tq)[None, :]
    valid = (c // CHUNK >= r // CHUNK) & (c // CHUNK <= r // CHUNK + BAND_CHUNKS)
    return jnp.where(valid[None], bias, NEG_INF)


def _band_prompt_call(qk, vt, rel_bias, wa, wb):
    b, s, _ = qk.shape
    nh = wb // HEAD_DIM
    tq = BAND_Q_TILE
    assert s % tq == 0 and wa == wb
    bias_t = (_band_bias_prompt(rel_bias, tq) * LOG2E).transpose(0, 2, 1)
    wq = nh * LANES
    cq, ck = 2, 3

    def k_spec(back):
        return pl.BlockSpec((1, tq, wq), lambda b, i: (b, jnp.maximum(i - back, 0), ck))

    def v_spec(back):
        return pl.BlockSpec((1, wb + 16, tq), lambda b, i: (b, 1, jnp.maximum(i - back, 0)))

    return pl.pallas_call(
        functools.partial(_band_prompt_body, nh=nh),
        grid=(b, s // tq),
        in_specs=[pl.BlockSpec((1, tq, wq), lambda b, i: (b, i, cq)),
                  k_spec(2), k_spec(1), k_spec(0), v_spec(2), v_spec(1), v_spec(0),
                  _const_spec(bias_t)],
        out_specs=pl.BlockSpec((1, tq, wb), lambda b, i: (b, i, 0)),
        out_shape=jax.ShapeDtypeStruct((b, s, wb), BF),
        scratch_shapes=[pltpu.VMEM((2, 3 * tq, tq), F32)],
        compiler_params=_cparams(("arbitrary", "arbitrary")),
        name="band_prompt",
    )(qk, qk, qk, qk, vt, vt, vt, bias_t)


def _lambda(lam_ref, lam_init):
    lp = lam_ref[...]
    a = jnp.sum(lp[0:1] * lp[1:2], axis=-1, keepdims=True)
    b = jnp.sum(lp[2:3] * lp[3:4], axis=-1, keepdims=True)
    return jnp.exp(a) - jnp.exp(b) + lam_init


def _diff_prompt_body(it_ref, jt_ref, q_ref, k_ref, vt_ref, lam_ref, gcol_ref, o_ref, m_ref, acc_ref, st_ref,
                      *, nh, lam_init):
    i = it_ref[pl.program_id(1)]
    j = jt_ref[pl.program_id(1)]
    dv = 2 * HEAD_DIM

    @pl.when(j == 0)
    def _():
        _tflash_init(m_ref, acc_ref)

    def step(diag):
        q = q_ref[0]
        k = k_ref[0]
        vt = vt_ref[0]
        t = q.shape[0]
        ones = vt[nh * dv:, :]
        if diag:
            def penalty(shape):
                key = lax.broadcasted_iota(jnp.int32, shape, 0)
                qry = lax.broadcasted_iota(jnp.int32, shape, 1)
                visible = (key >> CHUNK_SHIFT) <= (qry >> CHUNK_SHIFT)
                return jnp.where(visible, 2.0 * jnp.maximum(key - qry, 0).astype(F32), -NEG_INF)
            pen_lo, pen_hi = penalty((t // 2, t)), penalty((t // 2, t // 2))

        def scores(n):
            k_blk, q_blk = k[:, n * LANES:(n + 1) * LANES], q[:, n * LANES:(n + 1) * LANES]
            if diag:
                _diag_scores(st_ref, n % 2, k_blk, q_blk)
            else:
                st_ref[n % 2] = _dot_nt(k_blk, q_blk)

        scores(0)
        for h in range(nh):
            v_aug = jnp.concatenate([vt[h * dv:(h + 1) * dv, :], ones], axis=0)
            if diag:
                c = LOG2E * _alibi_slope(h, nh)
                ex_lo, ex_hi = c * pen_lo, c * pen_hi
            for u in range(2):
                n = 2 * h + u
                if n + 1 < 2 * nh:
                    scores(n + 1)
                if diag:
                    _tflash_diag_update(st_ref, n % 2, v_aug, lambda s: s - ex_lo, lambda s: s - ex_hi,
                                        m_ref, acc_ref, n)
                else:
                    _tflash_update(st_ref[n % 2], v_aug, m_ref, acc_ref, n)

    @pl.when(j < i)
    def _():
        step(False)

    @pl.when(j == i)
    def _():
        step(True)
        lam = _lambda(lam_ref, lam_init)
        g = gcol_ref[...]
        outs = []
        for h in range(nh):
            a1 = acc_ref[2 * h]
            a2 = acc_ref[2 * h + 1]
            o = a1[:dv] / a1[dv:dv + 1] - lam * (a2[:dv] / a2[dv:dv + 1])
            o = o * lax.rsqrt(jnp.mean(o * o, axis=0, keepdims=True) + RMS_EPS)
            outs.append((o * g) * (1.0 - lam_init))
        o_ref[0] = jnp.concatenate(outs, axis=0).T.astype(BF)


def _diff_prompt_call(qk, vt, lam_par, g_col, wc, lam_init):
    b, s, _ = qk.shape
    nh = wc // (2 * HEAD_DIM)
    t = min(ATTN_TILE, s)
    it, jt = _causal_pairs(s // t)
    wq = 2 * nh * LANES
    grid_spec = pltpu.PrefetchScalarGridSpec(
        num_scalar_prefetch=2,
        grid=(b, it.shape[0]),
        in_specs=[pl.BlockSpec((1, t, wq), lambda b, s, it, jt: (b, it[s], 0)),
                  pl.BlockSpec((1, t, wq), lambda b, s, it, jt: (b, jt[s], 1)),
                  pl.BlockSpec((1, wc + 16, t), lambda b, s, it, jt: (b, 0, jt[s])),
                  pl.BlockSpec(lam_par.shape, lambda b, s, it, jt: (0, 0)),
                  pl.BlockSpec(g_col.shape, lambda b, s, it, jt: (0, 0))],
        out_specs=pl.BlockSpec((1, t, wc), lambda b, s, it, jt: (b, it[s], 0)),
        scratch_shapes=[pltpu.VMEM((2 * nh, 1, t), F32), pltpu.VMEM((2 * nh, 2 * HEAD_DIM + 16, t), F32),
                        pltpu.VMEM((2, t, t), F32)],
    )
    return pl.pallas_call(
        functools.partial(_diff_prompt_body, nh=nh, lam_init=lam_init),
        grid_spec=grid_spec,
        out_shape=jax.ShapeDtypeStruct((b, s, wc), BF),
        compiler_params=_cparams(("arbitrary", "arbitrary")),
        name="diff_prompt",
    )(it, jt, qk, qk, vt, lam_par, g_col)


def _cumsum_body(x_ref, up_ref, o_ref, carry_ref):
    @pl.when(pl.program_id(1) == 0)
    def _():
        carry_ref[...] = jnp.zeros(carry_ref.shape, F32)
    ft = _cumsum_lanes(x_ref[0], up_ref[...]) + carry_ref[:, 0:1]
    o_ref[0] = ft
    carry_ref[...] = jnp.broadcast_to(ft[:, -1:], carry_ref.shape)


def _cumsum_call(x):
    b, nh, length = x.shape
    t = 512
    up = _upper_tri(t)
    return pl.pallas_call(
        _cumsum_body,
        grid=(b, length // t),
        in_specs=[pl.BlockSpec((1, nh, t), lambda b, j: (b, 0, j)), pl.BlockSpec((t, t), lambda b, j: (0, 0))],
        out_specs=pl.BlockSpec((1, nh, t), lambda b, j: (b, 0, j)),
        out_shape=jax.ShapeDtypeStruct(x.shape, F32),
        scratch_shapes=[pltpu.VMEM((nh, LANES), F32)],
        compiler_params=_cparams(("arbitrary", "arbitrary")),
        name="logf_cumsum",
    )(x, up)


def _fox_sample_body(q_ref, kn_ref, vn_ref, ck_ref, cv_ref, ftc_ref, ftn_ref, o_ref, m_ref, l_ref, acc_ref, *, nh):
    j = pl.program_id(1)
    hd = HEAD_DIM
    q = q_ref[0]
    t = q.shape[0]

    @pl.when(j == 0)
    def _():
        _osm_init(m_ref, l_ref, acc_ref)

    k = ck_ref[0].astype(BF)
    v = cv_ref[0].astype(BF)
    ft = ftc_ref[0]
    for h in range(nh):
        sl = slice(h * hd, (h + 1) * hd)
        s = _dot_nt(q[:, sl], k[:, sl]) - ft[h:h + 1, :]
        _osm_update(s, v[:, sl], m_ref, l_ref, acc_ref, h)

    @pl.when(j == pl.num_programs(1) - 1)
    def _():
        kn = kn_ref[0]
        vn = vn_ref[0]
        ftn = ftn_ref[0][:, :t]
        visible = lax.broadcasted_iota(jnp.int32, (t, t), 1) <= lax.broadcasted_iota(jnp.int32, (t, t), 0)
        outs = []
        for h in range(nh):
            sl = slice(h * hd, (h + 1) * hd)
            s = _dot_nt(q[:, sl], kn[:, sl]) - ftn[h:h + 1, :]
            s = jnp.where(visible, s, NEG_INF)
            _osm_update(s, vn[:, sl], m_ref, l_ref, acc_ref, h)
            outs.append((acc_ref[h] / l_ref[h]).astype(BF))
        o_ref[0] = jnp.concatenate(outs, axis=-1)


def _fox_sample_call(qkv, cache_k, cache_v, ft_pad, wa):
    b, t, _ = qkv.shape
    p_len = cache_k.shape[1]
    nh = wa // HEAD_DIM
    tk = min(CACHE_TILE_A, p_len)
    assert p_len % tk == 0 and p_len % LANES == 0 and t <= LANES
    return pl.pallas_call(
        functools.partial(_fox_sample_body, nh=nh),
        grid=(b, p_len // tk),
        in_specs=[pl.BlockSpec((1, t, wa), lambda b, j: (b, 0, 0)),
                  pl.BlockSpec((1, t, wa), lambda b, j: (b, 0, 1)),
                  pl.BlockSpec((1, t, wa), lambda b, j: (b, 0, 2)),
                  pl.BlockSpec((1, tk, wa), lambda b, j: (b, j, 0)),
                  pl.BlockSpec((1, tk, wa), lambda b, j: (b, j, 0)),
                  pl.BlockSpec((1, nh, tk), lambda b, j: (b, 0, j)),
                  pl.BlockSpec((1, nh, LANES), lambda b, j: (b, 0, p_len // LANES))],
        out_specs=pl.BlockSpec((1, t, wa), lambda b, j: (b, 0, 0)),
        out_shape=jax.ShapeDtypeStruct((b, t, wa), BF),
        scratch_shapes=[pltpu.VMEM((nh, t, 1), F32), pltpu.VMEM((nh, t, 1), F32),
                        pltpu.VMEM((nh, t, HEAD_DIM), F32)],
        compiler_params=_cparams(("arbitrary", "arbitrary")),
        name="fox_sample",
    )(qkv, qkv, qkv, cache_k, cache_v, ft_pad, ft_pad)


def _band_sample_body(q_ref, kn_ref, vn_ref, bk_ref, bv_ref, bias_b_ref, bias_n_ref, o_ref, *, nh):
    hd = HEAD_DIM
    q = q_ref[0]
    kn = kn_ref[0]
    vn = vn_ref[0]
    kb = bk_ref[0].astype(BF)
    vb = bv_ref[0].astype(BF)
    outs = []
    for h in range(nh):
        sl = slice(h * hd, (h + 1) * hd)
        sb = _dot_nt(q[:, sl], kb[:, sl]) + bias_b_ref[h]
        sn = _dot_nt(q[:, sl], kn[:, sl]) + bias_n_ref[h]
        m = jnp.maximum(jnp.max(sb, axis=-1, keepdims=True), jnp.max(sn, axis=-1, keepdims=True))
        pb = jnp.exp(sb - m)
        pn = jnp.exp(sn - m)
        l = jnp.sum(pb, axis=-1, keepdims=True) + jnp.sum(pn, axis=-1, keepdims=True)
        o = _dot(pb.astype(BF), vb[:, sl]) + _dot(pn.astype(BF), vn[:, sl])
        outs.append((o / l).astype(BF))
    o_ref[0] = jnp.concatenate(outs, axis=-1)


def _band_sample_call(qkv, buf_k, buf_v, rel_bias, wb):
    b, t, _ = qkv.shape
    lb = buf_k.shape[1]
    nh = wb // HEAD_DIM
    k_pos = jnp.concatenate([jnp.arange(-lb, 0), jnp.arange(t)])
    rel = jnp.arange(t)[:, None] - k_pos[None, :]
    bias = rel_bias[:, jnp.clip(rel, -MAX_REL, MAX_REL) + MAX_REL].astype(F32)
    bias_b, bias_n = bias[:, :, :lb], bias[:, :, lb:]
    return pl.pallas_call(
        functools.partial(_band_sample_body, nh=nh),
        grid=(b,),
        in_specs=[pl.BlockSpec((1, t, wb), lambda b: (b, 0, 3)),
                  pl.BlockSpec((1, t, wb), lambda b: (b, 0, 4)),
                  pl.BlockSpec((1, t, wb), lambda b: (b, 0, 5)),
                  pl.BlockSpec((1, lb, wb), lambda b: (b, 0, 0)),
                  pl.BlockSpec((1, lb, wb), lambda b: (b, 0, 0)),
                  pl.BlockSpec(bias_b.shape, lambda b: (0, 0, 0)),
                  pl.BlockSpec(bias_n.shape, lambda b: (0, 0, 0))],
        out_specs=pl.BlockSpec((1, t, wb), lambda b: (b, 0, 0)),
        out_shape=jax.ShapeDtypeStruct((b, t, wb), BF),
        compiler_params=_cparams(("arbitrary",)),
        name="band_sample",
    )(qkv, qkv, qkv, buf_k, buf_v, bias_b, bias_n)


def _diff_sample_body(q_ref, kn_ref, vn_ref, ck_ref, cv_ref, lam_ref, gsub_ref, o_ref, m_ref, l_ref, acc_ref,
                      *, nh, lam_init, p_len):
    j = pl.program_id(1)
    q = q_ref[0]
    t = q.shape[0]
    dv = 2 * HEAD_DIM
    lane = lax.broadcasted_iota(jnp.int32, (t, dv), 1)

    @pl.when(j == 0)
    def _():
        _osm_init(m_ref, l_ref, acc_ref)

    def attend(k_of, v_of, dist):
        dist2 = jnp.concatenate([dist, dist], axis=0)
        for h in range(nh):
            x = q[:, h * dv:(h + 1) * dv]
            zero = jnp.zeros_like(x)
            q2 = jnp.concatenate([jnp.where(lane < HEAD_DIM, x, zero), jnp.where(lane >= HEAD_DIM, x, zero)], axis=0)
            s = _dot_nt(q2, k_of(h)) - _alibi_slope(h, nh) * dist2
            _osm_update(s, v_of(h), m_ref, l_ref, acc_ref, h)

    tk = ck_ref.shape[1] // nh
    row = lax.broadcasted_iota(jnp.int32, (t, tk), 0)
    col = lax.broadcasted_iota(jnp.int32, (t, tk), 1)
    attend(lambda h: ck_ref[0, pl.ds(h, tk, stride=nh), :].astype(BF),
           lambda h: cv_ref[0, pl.ds(h, tk, stride=nh), :].astype(BF),
           (p_len + row - j * tk - col).astype(F32))

    @pl.when(j == pl.num_programs(1) - 1)
    def _():
        kn = kn_ref[0]
        vn = vn_ref[0]
        rown = lax.broadcasted_iota(jnp.int32, (t, t), 0)
        coln = lax.broadcasted_iota(jnp.int32, (t, t), 1)
        attend(lambda h: kn[:, h * dv:(h + 1) * dv], lambda h: vn[:, h * dv:(h + 1) * dv],
               jnp.abs(rown - coln).astype(F32))
        lam = _lambda(lam_ref, lam_init)
        g = gsub_ref[...]
        outs = []
        for h in range(nh):
            on = acc_ref[h] / l_ref[h]
            o = on[:t] - lam * on[t:]
            o = o * lax.rsqrt(jnp.mean(o * o, axis=-1, keepdims=True) + RMS_EPS)
            outs.append(((o * g) * (1.0 - lam_init)).astype(BF))
        o_ref[0] = jnp.concatenate(outs, axis=-1)


def _diff_sample_call(qkv, cache_k, cache_v, lam_par, g_sub, wc, lam_init):
    b, t, _ = qkv.shape
    nh = wc // (2 * HEAD_DIM)
    p_len = cache_k.shape[1] // nh
    assert (p_len // CHUNK) * CHUNK == p_len and t <= CHUNK
    tk = min(CACHE_TILE_C, p_len)
    return pl.pallas_call(
        functools.partial(_diff_sample_body, nh=nh, lam_init=lam_init, p_len=p_len),
        grid=(b, p_len // tk),
        in_specs=[pl.BlockSpec((1, t, wc), lambda b, j: (b, 0, 0)),
                  pl.BlockSpec((1, t, wc), lambda b, j: (b, 0, 1)),
                  pl.BlockSpec((1, t, wc), lambda b, j: (b, 0, 2)),
                  pl.BlockSpec((1, tk * nh, 2 * HEAD_DIM), lambda b, j: (b, j, 0)),
                  pl.BlockSpec((1, tk * nh, 2 * HEAD_DIM), lambda b, j: (b, j, 0)),
                  pl.BlockSpec(lam_par.shape, lambda b, j: (0, 0)),
                  pl.BlockSpec(g_sub.shape, lambda b, j: (0, 0))],
        out_specs=pl.BlockSpec((1, t, wc), lambda b, j: (b, 0, 0)),
        out_shape=jax.ShapeDtypeStruct((b, t, wc), BF),
        scratch_shapes=[pltpu.VMEM((nh, 2 * t, 1), F32), pltpu.VMEM((nh, 2 * t, 1), F32),
                        pltpu.VMEM((nh, 2 * t, 2 * HEAD_DIM), F32)],
        compiler_params=_cparams(("arbitrary", "arbitrary")),
        name="diff_sample",
    )(qkv, qkv, qkv, cache_k, cache_v, lam_par, g_sub)


def _out_ffn_body(x_ref, oa_ref, ob_ref, gm_ref, g_ref, sh_ref, sc_ref, gf_ref, wo_ref, w1_ref, w3_ref, w2_ref, y_ref,
                  *, f_chunk):
    wa = oa_ref.shape[-1]
    attn = _dot(oa_ref[0], wo_ref[:wa, :]) + _dot(ob_ref[0], wo_ref[wa:, :])
    x1 = x_ref[0] + gm_ref[0, 0, 0] * attn
    h = _modulate(x1, g_ref[...], sh_ref[0, 0, 0], sc_ref[0, 0, 0]).astype(BF)
    f_total = w1_ref.shape[1]
    f = None
    for c0 in range(0, f_total, f_chunk):
        a = _dot(h, w1_ref[:, c0:c0 + f_chunk])
        g = _dot(h, w3_ref[:, c0:c0 + f_chunk])
        u = ((a * _sigmoid(a)) * g).astype(BF)
        part = _dot(u, w2_ref[c0:c0 + f_chunk, :])
        f = part if f is None else f + part
    y_ref[0] = x1 + gf_ref[0, 0, 0] * f


def _out_ffn_call(x, o_a, o_b, gate_m, g, shift, scale, gate_f, wo, w1, w3, w2):
    bx, sx, d = x.shape
    tm = min(ROW_TILE, sx)
    f_total = w1.shape[1]
    f_chunk = f_total // 2 if (f_total // 2) % LANES == 0 else f_total
    row = lambda b, i: (b, i, 0)
    return pl.pallas_call(
        functools.partial(_out_ffn_body, f_chunk=f_chunk),
        grid=(bx, sx // tm),
        in_specs=[pl.BlockSpec((1, tm, d), row),
                  pl.BlockSpec((1, tm, o_a.shape[-1]), row), pl.BlockSpec((1, tm, o_b.shape[-1]), row),
                  _mod_spec(gate_m, tm), _const_spec(g), _mod_spec(shift, tm), _mod_spec(scale, tm),
                  _mod_spec(gate_f, tm), _const_spec(wo), _const_spec(w1), _const_spec(w3), _const_spec(w2)],
        out_specs=pl.BlockSpec((1, tm, d), row),
        out_shape=jax.ShapeDtypeStruct((bx, sx, d), F32),
        compiler_params=_cparams(("arbitrary", "arbitrary")),
        name="out_ffn",
    )(x, o_a, o_b, gate_m[0], g, shift[0], scale[0], gate_f[0], wo, w1, w3, w2)


def _store_tile_rows(ref, x):
    rows, d = x.shape
    nseg = d // LANES
    for s in range(nseg):
        ref[pl.ds(s, rows, stride=nseg), :] = x[:, s * LANES:(s + 1) * LANES]


def _load_tile_rows(ref, rows, d):
    nseg = d // LANES
    return jnp.concatenate([ref[pl.ds(s, rows, stride=nseg), :] for s in range(nseg)], axis=-1)


def _out_router_body(x_ref, o_ref, gm_ref, g_ref, sh_ref, sc_ref, wo_ref, wr_ref, h_in_ref,
                     x1_ref, h_ref, route_ref, *, n_exp):
    del h_in_ref
    x1 = x_ref[0] + gm_ref[0, 0, 0] * _dot(o_ref[0], wo_ref[...])
    x1_ref[0] = x1
    h = _modulate(x1, g_ref[...], sh_ref[0, 0, 0], sc_ref[0, 0, 0])
    hb = h.astype(BF)
    _store_tile_rows(h_ref, hb.astype(F32))
    terms = _dot(hb, wr_ref[...])
    logits = (terms + pltpu.roll(terms, LANES - ROUTE_ROWS, 1)) + pltpu.roll(terms, LANES - 2 * ROUTE_ROWS, 1)
    lt = logits.T[:ROUTE_ROWS, :]
    row = lax.broadcasted_iota(jnp.int32, lt.shape, 0)
    lg = jnp.where(row < n_exp, lt, NEG_INF)
    e = jnp.exp(lg - jnp.max(lg, axis=0, keepdims=True))
    probs = e / jnp.sum(e, axis=0, keepdims=True)
    p1 = jnp.max(probs, axis=0, keepdims=True)
    i1 = jnp.min(jnp.where(probs == p1, row, ROUTE_ROWS), axis=0, keepdims=True)
    rest = jnp.where(row == i1, -1.0, probs)
    p2 = jnp.max(rest, axis=0, keepdims=True)
    i2 = jnp.min(jnp.where(rest == p2, row, ROUTE_ROWS), axis=0, keepdims=True)
    tot = p1 + p2
    route_ref[0] = jnp.where(row == 0, i1.astype(F32),
                             jnp.where(row == 1, i2.astype(F32),
                                       jnp.where(row == 2, p1 / tot, jnp.where(row == 3, p2 / tot, 0.0))))


def _out_router_call(x, o, gate_m, g, shift, scale, wo, wr3, n_exp, tok_offset, h_all):
    bx, sx, d = x.shape
    tm = min(ROW_TILE, sx)
    nt = sx // tm
    nseg = d // LANES
    assert tok_offset % tm == 0 and n_exp <= ROUTE_ROWS
    row = lambda b, i: (b, i, 0)
    return pl.pallas_call(
        functools.partial(_out_router_body, n_exp=n_exp),
        grid=(bx, nt),
        in_specs=[pl.BlockSpec((1, tm, d), row), pl.BlockSpec((1, tm, o.shape[-1]), row),
                  _mod_spec(gate_m, tm), _const_spec(g), _mod_spec(shift, tm), _mod_spec(scale, tm),
                  _const_spec(wo), _const_spec(wr3), pl.BlockSpec(memory_space=pl.ANY)],
        out_specs=[pl.BlockSpec((1, tm, d), row),
                   pl.BlockSpec((tm * nseg, LANES), lambda b, i: (tok_offset // tm + b * nt + i, 0)),
                   pl.BlockSpec((1, ROUTE_ROWS, tm), lambda b, i: (b, 0, i))],
        out_shape=[jax.ShapeDtypeStruct((bx, sx, d), F32),
                   jax.ShapeDtypeStruct(h_all.shape, F32),
                   jax.ShapeDtypeStruct((bx, ROUTE_ROWS, sx), F32)],
        input_output_aliases={8: 1},
        compiler_params=_cparams(("arbitrary", "arbitrary")),
        name="out_router",
    )(x, o, gate_m[0], g, shift[0], scale[0], wo, wr3, h_all)


def _moe_ffn_body(te_ref, nv_ref, idx_ref, idx_next_ref, h_ref, w1_ref, w3_ref, w2_ref, y_ref, xbuf, acc, sem,
                  *, nseg, nf):
    t = pl.program_id(0)
    j = pl.program_id(1)
    nv = nv_ref[0]
    tm, d = acc.shape
    slot = lax.rem(t, 2)

    def issue(rows_ref, s):
        def body(r, c):
            src = pl.multiple_of(rows_ref[0, 0, r], nseg)
            dst = pl.multiple_of(r * nseg, nseg)
            pltpu.make_async_copy(h_ref.at[pl.ds(src, nseg)], xbuf.at[s, pl.ds(dst, nseg)], sem.at[s]).start()
            return c
        lax.fori_loop(0, tm, body, 0, unroll=8)

    @pl.when((j == 0) & (t == 0))
    def _():
        issue(idx_ref, 0)

    @pl.when((j == 0) & (t + 1 < nv))
    def _():
        issue(idx_next_ref, 1 - slot)

    @pl.when((j == 0) & (t < nv))
    def _():
        pltpu.make_async_copy(h_ref.at[pl.ds(0, tm * nseg)], xbuf.at[slot], sem.at[slot]).wait()

    @pl.when(t < nv)
    def _():
        xb = _load_tile_rows(xbuf.at[slot], tm, d).astype(BF)
        a = _dot(xb, w1_ref[0])
        g = _dot(xb, w3_ref[0])
        u = ((a * _sigmoid(a)) * g).astype(BF)
        part = _dot(u, w2_ref[0])

        if nf == 1:
            _store_tile_rows(y_ref, part)
        else:
            @pl.when(j == 0)
            def _():
                acc[...] = part

            @pl.when((j > 0) & (j < nf - 1))
            def _():
                acc[...] += part

            @pl.when(j == nf - 1)
            def _():
                _store_tile_rows(y_ref, acc[...] + part)

    @pl.when((t >= nv) & (j == nf - 1))
    def _():
        y_ref[...] = jnp.zeros(y_ref.shape, F32)


def _moe_ffn_call(h_all, row_tok, tile_e, n_valid, w1, w3, w2):
    n_rows = row_tok.shape[0]
    d = w1.shape[1]
    nseg = d // LANES
    f_total = w1.shape[2]
    tf = MOE_F_TILE if f_total % MOE_F_TILE == 0 else f_total
    nf = f_total // tf
    tm = MOE_ROW_TILE
    n_tiles = n_rows // tm
    idx3 = (row_tok * nseg).reshape(n_tiles, 1, tm)

    def fcol(t, j, te, nv):
        return jnp.where(t < nv[0], j, nf - 1)

    grid_spec = pltpu.PrefetchScalarGridSpec(
        num_scalar_prefetch=2,
        grid=(n_tiles, nf),
        in_specs=[pl.BlockSpec((1, 1, tm), lambda t, j, te, nv: (t, 0, 0), memory_space=pltpu.SMEM),
                  pl.BlockSpec((1, 1, tm), lambda t, j, te, nv: (jnp.minimum(t + 1, n_tiles - 1), 0, 0),
                               memory_space=pltpu.SMEM),
                  pl.BlockSpec(memory_space=pl.ANY),
                  pl.BlockSpec((1, d, tf), lambda t, j, te, nv: (te[t], 0, fcol(t, j, te, nv))),
                  pl.BlockSpec((1, d, tf), lambda t, j, te, nv: (te[t], 0, fcol(t, j, te, nv))),
                  pl.BlockSpec((1, tf, d), lambda t, j, te, nv: (te[t], fcol(t, j, te, nv), 0))],
        out_specs=pl.BlockSpec((tm * nseg, LANES), lambda t, j, te, nv: (t, 0)),
        scratch_shapes=[pltpu.VMEM((2, tm * nseg, LANES), F32), pltpu.VMEM((tm, d), F32),
                        pltpu.SemaphoreType.DMA((2,))],
    )
    return pl.pallas_call(
        functools.partial(_moe_ffn_body, nseg=nseg, nf=nf),
        grid_spec=grid_spec,
        out_shape=jax.ShapeDtypeStruct((n_rows * nseg, LANES), F32),
        compiler_params=_cparams(("arbitrary", "arbitrary")),
        name="moe_ffn",
    )(tile_e, n_valid, idx3, idx3, h_all, w1, w3, w2)


def _combine_body(d0_ref, d1_ref, x_ref, gf_ref, w_ref, g_ref, yb_ref, o_ref, buf, sem):
    t = pl.program_id(0)
    nt = pl.num_programs(0)
    _, tc, d = x_ref.shape
    nseg = d // LANES

    def row_copy(idx_ref, r, slot, k):
        src = pl.multiple_of(idx_ref[0, 0, 2 * r + k], nseg)
        dst = pl.multiple_of(r * nseg, nseg)
        return pltpu.make_async_copy(yb_ref.at[pl.ds(src, nseg)], buf.at[slot, k, pl.ds(dst, nseg)], sem.at[slot])

    def issue(idx_ref, slot):
        def body(r, c):
            row_copy(idx_ref, r, slot, 0).start()
            row_copy(idx_ref, r, slot, 1).start()
            return c
        lax.fori_loop(0, tc, body, 0, unroll=4)

    slot = lax.rem(t, 2)

    @pl.when(t == 0)
    def _():
        issue(d0_ref, 0)

    @pl.when(t + 1 < nt)
    def _():
        issue(d1_ref, 1 - slot)

    for k in range(TOP_K):
        pltpu.make_async_copy(yb_ref.at[pl.ds(0, tc * nseg)], buf.at[slot, k], sem.at[slot]).wait()
    w = jnp.concatenate([w_ref[0], jnp.zeros((LANES - ROUTE_ROWS, tc), F32)], axis=0).T
    f = (w[:, 2:3] * _load_tile_rows(buf.at[slot, 0], tc, d)
         + w[:, 3:4] * _load_tile_rows(buf.at[slot, 1], tc, d))
    y = x_ref[0] + gf_ref[0, 0, 0] * f
    y = y * lax.rsqrt(jnp.mean(y * y, axis=-1, keepdims=True) + RMS_EPS)
    o_ref[0] = y * g_ref[...]


def _combine_call(x1, gate_f, route, g_final, yb, dest):
    bx, sx, d = x1.shape
    nseg = d // LANES
    tc = min(COMBINE_TILE, sx)
    nps = sx // tc
    nt = bx * nps
    dest3 = (dest * nseg).reshape(nt, 1, 2 * tc)
    row = lambda t: (t // nps, t % nps, 0)
    gate_arr, gl, gk = gate_f
    if gate_arr.shape[3] == 1:
        gate_spec = pl.BlockSpec((1, 1, 1, 1, d), lambda t: (gl, gk, t // nps, 0, 0))
    else:
        gate_spec = pl.BlockSpec((1, 1, 1, tc, d), lambda t: (gl, gk, t // nps, t % nps, 0))
    return pl.pallas_call(
        _combine_body,
        grid=(nt,),
        in_specs=[pl.BlockSpec((1, 1, 2 * tc), lambda t: (t, 0, 0), memory_space=pltpu.SMEM),
                  pl.BlockSpec((1, 1, 2 * tc), lambda t: (jnp.minimum(t + 1, nt - 1), 0, 0),
                               memory_space=pltpu.SMEM),
                  pl.BlockSpec((1, tc, d), row), gate_spec,
                  pl.BlockSpec((1, ROUTE_ROWS, tc), lambda t: (t // nps, 0, t % nps)),
                  pl.BlockSpec(g_final.shape, lambda t: (0, 0)),
                  pl.BlockSpec(memory_space=pl.ANY)],
        out_specs=pl.BlockSpec((1, tc, d), row),
        out_shape=jax.ShapeDtypeStruct((bx, sx, d), F32),
        scratch_shapes=[pltpu.VMEM((2, 2, tc * nseg, LANES), F32), pltpu.SemaphoreType.DMA((2,))],
        compiler_params=_cparams(("arbitrary",)),
        name="moe_combine",
    )(dest3, dest3, x1, gate_arr, route, g_final, yb)


def _route_plan(slot_e, n_exp, tm):
    n_slots = slot_e.shape[0]
    onehot = (slot_e[:, None] == jnp.arange(n_exp, dtype=jnp.int32)[None, :]).astype(jnp.int32)
    csum = jnp.cumsum(onehot, axis=0)
    rank = jnp.sum(csum * onehot, axis=1) - 1
    counts = csum[-1]
    padded = ((counts + tm - 1) // tm) * tm
    pad_end = jnp.cumsum(padded)
    pad_start = pad_end - padded
    dest = jnp.sum(onehot * pad_start[None, :], axis=1) + rank
    n_tiles = -(-n_slots // tm) + n_exp
    n_rows = n_tiles * tm
    n_valid = (pad_end[-1] // tm).astype(jnp.int32)
    tile_start = jnp.arange(n_tiles, dtype=jnp.int32) * tm
    tile_e = jnp.minimum(jnp.searchsorted(pad_end, tile_start, side='right'), n_exp - 1).astype(jnp.int32)
    last_e = tile_e[jnp.maximum(n_valid - 1, 0)]
    tile_e = jnp.where(jnp.arange(n_tiles) < n_valid, tile_e, last_e)
    sorted_tok = (jnp.argsort(slot_e, stable=True) // TOP_K).astype(jnp.int32)
    sorted_tok = jnp.concatenate([sorted_tok, jnp.zeros((tm,), jnp.int32)])
    first = jnp.cumsum(counts) - counts - pad_start
    tile_first = jnp.minimum(tile_start + first[tile_e], n_slots)
    row_tok = sorted_tok[(tile_first[:, None] + jnp.arange(tm, dtype=jnp.int32)[None, :]).reshape(n_rows)]
    return dest, row_tok, tile_e, n_valid.reshape(1)


def _moe(parts, h_all, g_final, w1, w3, w2):
    n_exp = w1.shape[0]
    slot_e = jnp.concatenate([r[:, :TOP_K, :].transpose(0, 2, 1).reshape(-1) for _, r, _ in parts], axis=0)
    dest, row_tok, tile_e, n_valid = _route_plan(slot_e.astype(jnp.int32), n_exp, MOE_ROW_TILE)
    yb = _moe_ffn_call(h_all, row_tok, tile_e, n_valid, w1, w3, w2)
    outs = []
    first = 0
    for x1, route, gate_f in parts:
        n_slots = x1.shape[0] * x1.shape[1] * TOP_K
        outs.append(_combine_call(x1, gate_f, route, g_final, yb, dest[first:first + n_slots]))
        first += n_slots
    return outs


def _trunk(x, mods, p, cache, tok_offset, h_all):
    sh_m0, sc_m0, gt_m0, sh_f0, sc_f0, gt_f0 = mods[0]
    sh_m1, sc_m1, gt_m1, sh_f1, sc_f1, gt_f1 = mods[1]
    wa = p['wa']
    wb = p['wb']
    wc = p['wc']
    bx, sx, d = x.shape

    if cache is None:
        keep = min(WINDOW_B, sx)
        qk, vt, ka, va, kb, vb, logf = _inproj_even_call(
            x, p['g_mix0'], sh_m0, sc_m0, p['w_in_even'], p['w_f'], p['b_f'], keep, True)
        o_a = _fox_prompt_call(qk, vt, wa)
        o_b = _band_prompt_call(qk, vt, p['rel_bias'], wa, wb)
        x_l0_state = (ka, va, logf, kb, vb)
    else:
        ck, cv, clogf, bk, bv, cck, ccv, b, t = cache
        qkv, ka, va, kb, vb, logf, lft, = _inproj_even_call(
            x, p['g_mix0'], sh_m0, sc_m0, p['w_in_even'], p['w_f'], p['b_f'], sx, False)
        nh = wa // HEAD_DIM
        p_len = ck.shape[1]
        lft_new = lft.reshape(nh, b, t).transpose(1, 0, 2)
        lp = -(-(p_len + t) // 512) * 512
        lcat = jnp.concatenate([clogf.transpose(0, 2, 1), lft_new,
                                jnp.zeros((b, nh, lp - p_len - t), F32)], axis=-1)
        ft_pad = _cumsum_call(lcat)
        qkv_b = qkv.reshape(b, t, qkv.shape[-1])
        o_a = _fox_sample_call(qkv_b, ck, cv, ft_pad, wa).reshape(1, b * t, wa)
        o_b = _band_sample_call(qkv_b, bk, bv, p['rel_bias'], wb).reshape(1, b * t, wb)
        x_l0_state = (ka, va, logf, kb, vb)
    x = _out_ffn_call(x, o_a, o_b, gt_m0, p['g_ffn0'], sh_f0, sc_f0, gt_f0,
                      p['w_out_even'], p['w1_dense'], p['w3_dense'], p['w2_dense'])

    if cache is None:
        qk, vt, kc, vc = _inproj_odd_call(x, p['g_mix1'], sh_m1, sc_m1, p['w_in_odd'], True)
        o = _diff_prompt_call(qk, vt, p['lam_par'], p['g_sub'].reshape(-1, 1), wc, p['lam_init'])
    else:
        qkv, kc, vc = _inproj_odd_call(x, p['g_mix1'], sh_m1, sc_m1, p['w_in_odd'], False)
        qkv_b = qkv.reshape(b, t, qkv.shape[-1])
        o = _diff_sample_call(qkv_b, cck, ccv, p['lam_par'], p['g_sub'], wc, p['lam_init']).reshape(1, b * t, wc)
    x1, h_all, route = _out_router_call(x, o, gt_m1, p['g_ffn1'], sh_f1, sc_f1, p['w_out_odd'], p['w_router3'],
                                        p['n_exp'], tok_offset, h_all)
    return (x1, route, gt_f1), h_all, x_l0_state, (kc, vc)


def kernel(x_prompt, x_sample, cache_a_k, cache_a_v, cache_a_logf, cache_b_k, cache_b_v, cache_c_k, cache_c_v,
           c_prompt, c_sample, w_mod, b_mod, g_mix, g_ffn, g_final, w_in_even, b_forget, rel_bias, w_out_even,
           w_in_odd, lambda_q1, lambda_k1, lambda_q2, lambda_k2, g_subln, w_out_odd, w1_dense, w3_dense, w2_dense,
           w_router, w1_moe, w3_moe, w2_moe):
    bp, sp, d = x_prompt.shape
    bs, ts, _ = x_sample.shape
    h_a = cache_a_k.shape[3]
    h_b = cache_b_k.shape[3]
    h_c = cache_c_k.shape[3]
    wa, wb, wc = h_a * HEAD_DIM, h_b * HEAD_DIM, h_c * 2 * HEAD_DIM
    n_exp = w_router.shape[-1]
    assert w_mod.shape[0] == 2, "kernel is written for the 2-layer trunk"

    we = w_in_even[0]
    w_main = jnp.concatenate([we[:, :3 * wa], we[:, 3 * wa + h_a:]], axis=1).astype(BF)
    w_f = jnp.pad(we[:, 3 * wa:3 * wa + h_a], ((0, 0), (0, LANES - h_a))).astype(BF)
    b_f = jnp.pad(b_forget[0], (0, LANES - h_a)).reshape(1, LANES).astype(F32)
    wr = jnp.pad(w_router[0], ((0, 0), (0, ROUTE_ROWS - n_exp)))
    wr_hi = wr.astype(BF)
    wr_r1 = wr - wr_hi.astype(F32)
    wr_mid = wr_r1.astype(BF)
    wr_lo = (wr_r1 - wr_mid.astype(F32)).astype(BF)
    w_router3 = jnp.pad(jnp.concatenate([wr_hi, wr_mid, wr_lo], axis=1), ((0, 0), (0, LANES - 3 * ROUTE_ROWS)))
    params = dict(
        wa=wa, wb=wb, wc=wc, n_exp=n_exp, lam_init=0.8 - 0.6 * math.exp(-0.3 * 1),
        g_mix0=g_mix[0:1], g_mix1=g_mix[1:2], g_ffn0=g_ffn[0:1], g_ffn1=g_ffn[1:2],
        g_final=g_final.reshape(1, d),
        w_in_even=w_main, w_f=w_f, b_f=b_f, rel_bias=rel_bias[0],
        w_out_even=w_out_even[0].astype(BF), w_in_odd=w_in_odd[0].astype(BF),
        lam_par=jnp.concatenate([lambda_q1, lambda_k1, lambda_q2, lambda_k2], axis=0),
        g_sub=g_subln[0:1], w_out_odd=w_out_odd[0].astype(BF),
        w1_dense=w1_dense[0].astype(BF), w3_dense=w3_dense[0].astype(BF), w2_dense=w2_dense[0].astype(BF),
        w_router3=w_router3,
        w1_moe=_cast_bf16_call(w1_moe[0]), w3_moe=_cast_bf16_call(w3_moe[0]), w2_moe=_cast_bf16_call(w2_moe[0]),
    )

    mod = _mod_call(jnp.concatenate([c_prompt, c_sample], axis=0), w_mod, b_mod)

    def mods_for(rows, per_row_len):
        m = mod[:, rows].reshape(2, -1, 6, d).transpose(0, 2, 1, 3)
        if per_row_len:
            m = jnp.repeat(m, per_row_len, axis=2)[:, :, None]
        else:
            m = m[:, :, :, None]
        return [[(m, l, k) for k in range(6)] for l in range(2)]

    h_all = jnp.zeros(((bp * sp + bs * ts) * (d // LANES), LANES), F32)
    moe_p, h_all, ev_p, od_p = _trunk(x_prompt, mods_for(slice(0, bp), 0), params, None, 0, h_all)
    cache = (cache_a_k[0].reshape(bs, -1, wa), cache_a_v[0].reshape(bs, -1, wa), cache_a_logf[0],
             cache_b_k[0].reshape(bs, -1, wb), cache_b_v[0].reshape(bs, -1, wb),
             cache_c_k[0].reshape(bs, -1, 2 * HEAD_DIM), cache_c_v[0].reshape(bs, -1, 2 * HEAD_DIM), bs, ts)
    moe_s, h_all, ev_s, od_s = _trunk(x_sample.reshape(1, bs * ts, d), mods_for(slice(bp, bp + bs), ts), params,
                                      cache, bp * sp, h_all)
    y_p, y_s = _moe([moe_p, moe_s], h_all, params['g_final'], params['w1_moe'], params['w3_moe'], params['w2_moe'])

    ka, va, logf, kb, vb = ev_p
    keep = kb.shape[1]
    out_p = (ka.reshape(1, bp, sp, h_a, HEAD_DIM), va.reshape(1, bp, sp, h_a, HEAD_DIM),
             logf.reshape(1, bp, sp, h_a),
             kb.reshape(1, bp, keep, h_b, HEAD_DIM), vb.reshape(1, bp, keep, h_b, HEAD_DIM),
             od_p[0].reshape(1, bp, sp, h_c, 2 * HEAD_DIM), od_p[1].reshape(1, bp, sp, h_c, 2 * HEAD_DIM))
    ka, va, logf, kb, vb = ev_s
    new_bk = jnp.concatenate([cache_b_k[0], kb.reshape(bs, ts, h_b, HEAD_DIM)], axis=1)[:, ts:]
    new_bv = jnp.concatenate([cache_b_v[0], vb.reshape(bs, ts, h_b, HEAD_DIM)], axis=1)[:, ts:]
    out_s = (ka.reshape(1, bs, ts, h_a, HEAD_DIM), va.reshape(1, bs, ts, h_a, HEAD_DIM),
             logf.reshape(1, bs, ts, h_a), new_bk[None], new_bv[None],
             od_s[0].reshape(1, bs, ts, h_c, 2 * HEAD_DIM), od_s[1].reshape(1, bs, ts, h_c, 2 * HEAD_DIM))
    return (y_p, y_s.reshape(bs, ts, d)) + out_p + out_s
```

```python
import functools
import math

import jax
import jax.numpy as jnp
import numpy as np
from jax import lax
from jax.experimental import pallas as pl
from jax.experimental.pallas import tpu as pltpu

BF = jnp.bfloat16
F32 = jnp.float32

CHUNK = 64
CHUNK_SHIFT = 6
HEAD_DIM = 64
BAND_CHUNKS = 8
WINDOW_B = BAND_CHUNKS * CHUNK
MAX_REL = 128
TOP_K = 2
RMS_EPS = 1e-6
NEG_INF = -1e30
QK_SCALE = HEAD_DIM ** -0.5
LOG2E = math.log2(math.e)
PROMPT_Q_SCALE = QK_SCALE * LOG2E


def _bf16_terms(c, n):
    terms = []
    for _ in range(n):
        t = float(np.asarray(c, dtype=jnp.bfloat16))
        terms.append(t)
        c -= t
    return terms


LOG2E_TERMS = _bf16_terms(LOG2E, 3)

LANES = 128
SUBLANES = 8
VMEM_LIMIT_BYTES = 56 * 1024 * 1024
ROUTE_ROWS = SUBLANES

ROW_TILE = 512
ATTN_TILE = 512
BAND_Q_TILE = WINDOW_B // 2
CACHE_TILE_A = 2048
CACHE_TILE_C = 1024
MOE_ROW_TILE = 512
MOE_F_TILE = 1792
COMBINE_TILE = 512
CAST_ROWS = 1024


def _cparams(sem):
    return pltpu.CompilerParams(dimension_semantics=sem, vmem_limit_bytes=VMEM_LIMIT_BYTES)


def _dot(a, b):
    return jnp.dot(a, b, preferred_element_type=F32)


def _dot_nt(a, b):
    return lax.dot_general(a, b, (((1,), (1,)), ((), ())), preferred_element_type=F32)


def _sigmoid(x):
    return 1.0 / (1.0 + jnp.exp(-x))


def _modulate(x, g, shift, scale):
    y = x * lax.rsqrt(jnp.mean(x * x, axis=-1, keepdims=True) + RMS_EPS)
    return (y * g) * (1.0 + scale) + shift


def _split3(x):
    hi = x.astype(BF)
    r1 = x - hi.astype(F32)
    mid = r1.astype(BF)
    lo = (r1 - mid.astype(F32)).astype(BF)
    return hi, mid, lo


def _cumsum_lanes(x, upper):
    hi, mid, lo = _split3(x)
    return _dot(hi, upper) + _dot(mid, upper) + _dot(lo, upper)


def _upper_tri(t):
    r = jnp.arange(t)
    return (r[:, None] <= r[None, :]).astype(BF)


def _osm_update(s, v, m_ref, l_ref, acc_ref, idx):
    m_prev = m_ref[idx]
    m_new = jnp.maximum(m_prev, jnp.max(s, axis=-1, keepdims=True))
    alpha = jnp.exp(m_prev - m_new)
    p = jnp.exp(s - m_new)
    l_ref[idx] = alpha * l_ref[idx] + jnp.sum(p, axis=-1, keepdims=True)
    acc_ref[idx] = alpha * acc_ref[idx] + _dot(p.astype(BF), v)
    m_ref[idx] = m_new


def _osm_init(m_ref, l_ref, acc_ref):
    m_ref[...] = jnp.full(m_ref.shape, NEG_INF, F32)
    l_ref[...] = jnp.zeros(l_ref.shape, F32)
    acc_ref[...] = jnp.zeros(acc_ref.shape, F32)


def _cast_body(x_ref, o_ref):
    o_ref[...] = x_ref[...].astype(BF)


def _cast_bf16_call(w):
    e, k, n = w.shape
    tk = CAST_ROWS if k % CAST_ROWS == 0 else k
    spec = pl.BlockSpec((1, tk, n), lambda e, i: (e, i, 0))
    return pl.pallas_call(
        _cast_body, grid=(e, k // tk), in_specs=[spec], out_specs=spec,
        out_shape=jax.ShapeDtypeStruct(w.shape, BF),
        compiler_params=_cparams(("arbitrary", "arbitrary")),
        name="cast_bf16",
    )(w)


def _mod_body(c_ref, w_ref, b_ref, o_ref):
    c = c_ref[...]
    s = (c * _sigmoid(c)).astype(BF)
    o_ref[0] = _dot(s, w_ref[0].astype(BF)) + b_ref[0]


def _mod_call(c_all, w_mod, b_mod):
    depth, d, n = w_mod.shape
    r = c_all.shape[0]
    tn = 1536 if n % 1536 == 0 else n
    return pl.pallas_call(
        _mod_body,
        grid=(depth, n // tn),
        in_specs=[pl.BlockSpec((r, d), lambda l, j: (0, 0)),
                  pl.BlockSpec((1, d, tn), lambda l, j: (l, 0, j)),
                  pl.BlockSpec((1, 1, tn), lambda l, j: (l, 0, j))],
        out_specs=pl.BlockSpec((1, r, tn), lambda l, j: (l, 0, j)),
        out_shape=jax.ShapeDtypeStruct((depth, r, n), F32),
        compiler_params=_cparams(("arbitrary", "arbitrary")),
        name="mod",
    )(c_all, w_mod, b_mod.reshape(depth, 1, n))


def _mod_spec(m, tm):
    arr, l, k = m
    d = arr.shape[-1]
    if arr.shape[3] == 1:
        return pl.BlockSpec((1, 1, 1, 1, d), lambda b, i: (l, k, b, 0, 0))
    return pl.BlockSpec((1, 1, 1, tm, d), lambda b, i: (l, k, b, i, 0))


def _const_spec(arr):
    nd = arr.ndim
    return pl.BlockSpec(arr.shape, lambda b, i: (0,) * nd, pipeline_mode=pl.Buffered(1))


def _head_blocks(x, augs):
    tm, w = x.shape
    first = lax.broadcasted_iota(jnp.int32, (tm, LANES), 1) < HEAD_DIM
    out = []
    for p in range(w // LANES):
        pair = x[:, p * LANES:(p + 1) * LANES]
        out.append(jnp.where(first, pair, augs[2 * p]).astype(BF))
        out.append(jnp.where(first, pltpu.roll(pair, HEAD_DIM, 1), augs[2 * p + 1]).astype(BF))
    return out


def _lane_range_row(lo, hi, value):
    lane = lax.broadcasted_iota(jnp.int32, (1, LANES), 1)
    return jnp.where((lane >= lo) & (lane < hi), value, 0.0).astype(F32)


def _ones_rows(t):
    return jnp.where(lax.broadcasted_iota(jnp.int32, (16, t), 0) == 0, 1.0, 0.0).astype(BF)


def _log_forget(h, wf_ref, bf_ref):
    z = _dot(h, wf_ref[...]) + bf_ref[...]
    return jnp.minimum(z, 0.0) - jnp.log1p(jnp.exp(-jnp.abs(z)))


def _inproj_even_body(*refs, wa, wb, prompt):
    if prompt:
        (x_ref, g_ref, sh_ref, sc_ref, w_ref, wf_ref, bf_ref, low_ref, place_ref,
         qk_ref, vt_ref, ka_ref, va_ref, kb_ref, vb_ref, logf_ref, carry_ref) = refs
    else:
        (x_ref, g_ref, sh_ref, sc_ref, w_ref, wf_ref, bf_ref,
         qkv_ref, ka_ref, va_ref, kb_ref, vb_ref, logf_ref, lft_ref) = refs
    i = pl.program_id(1)
    h = _modulate(x_ref[0], g_ref[...], sh_ref[0, 0, 0], sc_ref[0, 0, 0]).astype(BF)
    tm = h.shape[0]
    nh = logf_ref.shape[-1]

    qa = _dot(h, w_ref[:, 0:wa])
    ka = _dot(h, w_ref[:, wa:2 * wa])
    va = _dot(h, w_ref[:, 2 * wa:3 * wa])
    o = 3 * wa
    qb = _dot(h, w_ref[:, o:o + wb])
    kb = _dot(h, w_ref[:, o + wb:o + 2 * wb])
    vb = _dot(h, w_ref[:, o + 2 * wb:o + 3 * wb])
    ka_ref[0] = ka
    va_ref[0] = va
    kb_ref[0] = kb
    vb_ref[0] = vb
    logf = _log_forget(h, wf_ref, bf_ref)
    logf_ref[0] = logf[:, :nh]

    if not prompt:
        qkv_ref[0] = jnp.concatenate(
            [(qa * QK_SCALE).astype(BF), ka.astype(BF), va.astype(BF),
             (qb * QK_SCALE).astype(BF), kb.astype(BF), vb.astype(BF)], axis=-1)
        lft_ref[0] = logf.T[:nh, :]
        return

    @pl.when(i == 0)
    def _():
        carry_ref[...] = jnp.zeros(carry_ref.shape, F32)

    terms = _dot(low_ref[...], jnp.concatenate(_split3(logf), axis=1))
    f_cum = terms[:, :LANES] + terms[:, LANES:2 * LANES] + terms[:, 2 * LANES:] + carry_ref[0:1, :]
    carry_ref[...] = jnp.broadcast_to(f_cum[tm - 1:tm, :], carry_ref.shape)
    k_aug = _dot(jnp.concatenate(_split3(-f_cum), axis=1), place_ref[...])
    q_log2e = sum(_lane_range_row(HEAD_DIM + 3 * r, HEAD_DIM + 3 * r + 3, c) for r, c in enumerate(LOG2E_TERMS))
    zero = jnp.zeros((1, LANES), F32)
    nhb = wb // HEAD_DIM
    blocks = (_head_blocks(qa * PROMPT_Q_SCALE, [q_log2e] * nh)
              + _head_blocks(ka, [k_aug[:, n * LANES:(n + 1) * LANES] for n in range(nh)])
              + _head_blocks(qb * PROMPT_Q_SCALE, [zero] * nhb)
              + _head_blocks(kb, [zero] * nhb))
    qk_ref[0] = jnp.concatenate(blocks, axis=-1)
    ones = _ones_rows(tm)
    vt_ref[0] = jnp.concatenate([va.T.astype(BF), ones, vb.T.astype(BF), ones], axis=0)


def _lower_tri(t):
    r = jnp.arange(t)
    return (r[:, None] >= r[None, :]).astype(BF)


def _place_matrices(nh):
    s = jnp.arange(3)[:, None, None]
    r = jnp.arange(LANES)[None, :, None]
    c = jnp.arange(nh * LANES)[None, None, :]
    off = c - r * LANES - HEAD_DIM - s
    return ((r < nh) & ((off == 0) | (off == 3) | (off == 6))).astype(BF).reshape(3 * LANES, nh * LANES)


def _inproj_even_call(x, g, shift, scale, w_main, w_f, b_f, keep, prompt):
    bx, sx, d = x.shape
    tm = min(ROW_TILE, sx)
    nt = sx // tm
    n = w_main.shape[1]
    wa = wb = n // 6
    nh = wa // HEAD_DIM
    nkeep = keep // tm
    tail = lambda b, i: (b, jnp.maximum(i - (nt - nkeep), 0), 0)
    row = lambda b, i: (b, i, 0)
    in_specs = [pl.BlockSpec((1, tm, d), row), _const_spec(g), _mod_spec(shift, tm), _mod_spec(scale, tm),
                _const_spec(w_main), _const_spec(w_f), _const_spec(b_f)]
    args = [x, g, shift[0], scale[0], w_main, w_f, b_f]
    f32_specs = [pl.BlockSpec((1, tm, wa), row), pl.BlockSpec((1, tm, wa), row),
                 pl.BlockSpec((1, tm, wb), tail), pl.BlockSpec((1, tm, wb), tail),
                 pl.BlockSpec((1, tm, nh), row)]
    f32_shapes = [jax.ShapeDtypeStruct((bx, sx, wa), F32), jax.ShapeDtypeStruct((bx, sx, wa), F32),
                  jax.ShapeDtypeStruct((bx, keep, wb), F32), jax.ShapeDtypeStruct((bx, keep, wb), F32),
                  jax.ShapeDtypeStruct((bx, sx, nh), F32)]
    scratch = []
    if prompt:
        low, place = _lower_tri(tm), _place_matrices(nh)
        in_specs += [_const_spec(low), _const_spec(place)]
        args += [low, place]
        nqk = 2 * (wa + wb) // HEAD_DIM * LANES
        nvt = wa + wb + 32
        out_specs = [pl.BlockSpec((1, tm, nqk), row), pl.BlockSpec((1, nvt, tm), lambda b, i: (b, 0, i))] + f32_specs
        out_shape = [jax.ShapeDtypeStruct((bx, sx, nqk), BF), jax.ShapeDtypeStruct((bx, nvt, sx), BF)] + f32_shapes
        scratch.append(pltpu.VMEM((8, LANES), F32))
    else:
        out_specs = [pl.BlockSpec((1, tm, n), row)] + f32_specs + [pl.BlockSpec((1, nh, tm), lambda b, i: (b, 0, i))]
        out_shape = [jax.ShapeDtypeStruct((bx, sx, n), BF)] + f32_shapes + [jax.ShapeDtypeStruct((bx, nh, sx), F32)]
    return pl.pallas_call(
        functools.partial(_inproj_even_body, wa=wa, wb=wb, prompt=prompt),
        grid=(bx, nt), in_specs=in_specs, out_specs=out_specs, out_shape=out_shape,
        scratch_shapes=scratch,
        compiler_params=_cparams(("arbitrary", "arbitrary")),
        name="inproj_even",
    )(*args)


def _inproj_odd_body(*refs, wc, prompt):
    if prompt:
        x_ref, g_ref, sh_ref, sc_ref, w_ref, qk_ref, vt_ref, k_ref, v_ref = refs
    else:
        x_ref, g_ref, sh_ref, sc_ref, w_ref, qkv_ref, k_ref, v_ref = refs
    i = pl.program_id(1)
    h = _modulate(x_ref[0], g_ref[...], sh_ref[0, 0, 0], sc_ref[0, 0, 0]).astype(BF)
    tm = h.shape[0]
    q = _dot(h, w_ref[:, 0:wc])
    k = _dot(h, w_ref[:, wc:2 * wc])
    v = _dot(h, w_ref[:, 2 * wc:3 * wc])
    k_ref[0] = k
    v_ref[0] = v
    if not prompt:
        qkv_ref[0] = jnp.concatenate([(q * QK_SCALE).astype(BF), k.astype(BF), v.astype(BF)], axis=-1)
        return
    nh = wc // LANES
    pos = (i * tm + lax.broadcasted_iota(jnp.int32, (tm, LANES), 0)).astype(F32)
    pos_hi = pos.astype(BF).astype(F32)
    lane = lax.broadcasted_iota(jnp.int32, (tm, LANES), 1) - HEAD_DIM
    in_aug = (lane >= 0) & (lane < 2 * len(LOG2E_TERMS))
    k_aug = jnp.where(in_aug, jnp.where((lane & 1) == 0, pos_hi, pos - pos_hi), 0.0)
    q_augs = []
    for hh in range(nh):
        slope = _alibi_slope(hh, nh)
        row = sum(_lane_range_row(HEAD_DIM + 2 * r, HEAD_DIM + 2 * r + 2, slope * c) for r, c in enumerate(LOG2E_TERMS))
        q_augs += [row] * 2
    blocks = _head_blocks(q * PROMPT_Q_SCALE, q_augs) + _head_blocks(k, [k_aug] * (2 * nh))
    qk_ref[0] = jnp.concatenate(blocks, axis=-1)
    vt_ref[0] = jnp.concatenate([v.T.astype(BF), _ones_rows(tm)], axis=0)


def _alibi_slope(h, nh):
    assert 8 % nh == 0, "slopes must be exact powers of two to ride in bf16"
    return 2.0 ** (-8.0 * (h + 1) / nh)


def _inproj_odd_call(x, g, shift, scale, w, prompt):
    bx, sx, d = x.shape
    tm = min(ROW_TILE, sx)
    n = w.shape[1]
    wc = n // 3
    row = lambda b, i: (b, i, 0)
    f32_specs = [pl.BlockSpec((1, tm, wc), row), pl.BlockSpec((1, tm, wc), row)]
    f32_shapes = [jax.ShapeDtypeStruct((bx, sx, wc), F32), jax.ShapeDtypeStruct((bx, sx, wc), F32)]
    if prompt:
        nqk = 2 * wc // HEAD_DIM * LANES
        out_specs = [pl.BlockSpec((1, tm, nqk), row), pl.BlockSpec((1, wc + 16, tm), lambda b, i: (b, 0, i))]
        out_shape = [jax.ShapeDtypeStruct((bx, sx, nqk), BF), jax.ShapeDtypeStruct((bx, wc + 16, sx), BF)]
    else:
        out_specs = [pl.BlockSpec((1, tm, n), row)]
        out_shape = [jax.ShapeDtypeStruct((bx, sx, n), BF)]
    return pl.pallas_call(
        functools.partial(_inproj_odd_body, wc=wc, prompt=prompt),
        grid=(bx, sx // tm),
        in_specs=[pl.BlockSpec((1, tm, d), row), _const_spec(g), _mod_spec(shift, tm), _mod_spec(scale, tm),
                  _const_spec(w)],
        out_specs=out_specs + f32_specs, out_shape=out_shape + f32_shapes,
        compiler_params=_cparams(("arbitrary", "arbitrary")),
        name="inproj_odd",
    )(x, g, shift[0], scale[0], w)


def _tflash_update(st, v_aug, m_ref, acc_ref, idx):
    m_prev = m_ref[idx]
    m_new = jnp.maximum(m_prev, jnp.max(st, axis=0, keepdims=True))
    alpha = jnp.exp2(m_prev - m_new)
    p = jnp.exp2(st - m_new).astype(BF)
    acc_ref[idx] = alpha * acc_ref[idx] + _dot(v_aug, p)
    m_ref[idx] = m_new


def _diag_scores(st_ref, slot, k_blk, q_blk):
    hf = k_blk.shape[0] // 2
    st_ref[slot, :hf, :] = _dot_nt(k_blk[:hf], q_blk)
    st_ref[slot, hf:, hf:] = _dot_nt(k_blk[hf:], q_blk[hf:])


def _tflash_diag_update(st_ref, slot, v_aug, fix_lo, fix_hi, m_ref, acc_ref, idx):
    hf = st_ref.shape[1] // 2
    lo = fix_lo(st_ref[slot, :hf, :])
    hi = fix_hi(st_ref[slot, hf:, hf:])
    m_prev = m_ref[idx]
    m_lo = jnp.maximum(m_prev, jnp.max(lo, axis=0, keepdims=True))
    m_new = jnp.concatenate([m_lo[:, :hf], jnp.maximum(m_lo[:, hf:], jnp.max(hi, axis=0, keepdims=True))], axis=1)
    alpha = jnp.exp2(m_prev - m_new)
    p_lo = jnp.exp2(lo - m_new).astype(BF)
    p_hi = jnp.exp2(hi - m_new[:, hf:]).astype(BF)
    upd = _dot(v_aug[:, :hf], p_lo)
    upd = jnp.concatenate([upd[:, :hf], upd[:, hf:] + _dot(v_aug[:, hf:], p_hi)], axis=1)
    acc_ref[idx] = alpha * acc_ref[idx] + upd
    m_ref[idx] = m_new


def _tflash_init(m_ref, acc_ref):
    m_ref[...] = jnp.full(m_ref.shape, NEG_INF, F32)
    acc_ref[...] = jnp.zeros(acc_ref.shape, F32)


def _causal_pairs(nq):
    pairs = [(i, j) for i in range(nq) for j in range(i + 1)]
    return jnp.asarray([p[0] for p in pairs], jnp.int32), jnp.asarray([p[1] for p in pairs], jnp.int32)


def _fox_prompt_body(it_ref, jt_ref, q_ref, k_ref, vt_ref, o_ref, m_ref, acc_ref, st_ref, *, nh):
    i = it_ref[pl.program_id(1)]
    j = jt_ref[pl.program_id(1)]
    hd = HEAD_DIM

    @pl.when(j == 0)
    def _():
        _tflash_init(m_ref, acc_ref)

    def step(diag):
        q = q_ref[0]
        k = k_ref[0]
        vt = vt_ref[0]
        t = q.shape[0]
        ones = vt[nh * hd:, :]
        if diag:
            causal = lambda s: jnp.where(lax.broadcasted_iota(jnp.int32, s.shape, 0)
                                         <= lax.broadcasted_iota(jnp.int32, s.shape, 1), s, NEG_INF)

        def scores(h):
            k_blk, q_blk = k[:, h * LANES:(h + 1) * LANES], q[:, h * LANES:(h + 1) * LANES]
            if diag:
                _diag_scores(st_ref, h % 2, k_blk, q_blk)
            else:
                st_ref[h % 2] = _dot_nt(k_blk, q_blk)

        scores(0)
        for h in range(nh):
            if h + 1 < nh:
                scores(h + 1)
            v_aug = jnp.concatenate([vt[h * hd:(h + 1) * hd, :], ones], axis=0)
            if diag:
                _tflash_diag_update(st_ref, h % 2, v_aug, causal, causal, m_ref, acc_ref, h)
            else:
                _tflash_update(st_ref[h % 2], v_aug, m_ref, acc_ref, h)

    @pl.when(j < i)
    def _():
        step(False)

    @pl.when(j == i)
    def _():
        step(True)
        ot = jnp.concatenate([acc_ref[h][:hd] / acc_ref[h][hd:hd + 1] for h in range(nh)], axis=0)
        o_ref[0] = ot.T.astype(BF)


def _fox_prompt_call(qk, vt, wa):
    b, s, _ = qk.shape
    nh = wa // HEAD_DIM
    t = min(ATTN_TILE, s)
    it, jt = _causal_pairs(s // t)
    wq = nh * LANES
    grid_spec = pltpu.PrefetchScalarGridSpec(
        num_scalar_prefetch=2,
        grid=(b, it.shape[0]),
        in_specs=[pl.BlockSpec((1, t, wq), lambda b, s, it, jt: (b, it[s], 0)),
                  pl.BlockSpec((1, t, wq), lambda b, s, it, jt: (b, jt[s], 1)),
                  pl.BlockSpec((1, wa + 16, t), lambda b, s, it, jt: (b, 0, jt[s]))],
        out_specs=pl.BlockSpec((1, t, wa), lambda b, s, it, jt: (b, it[s], 0)),
        scratch_shapes=[pltpu.VMEM((nh, 1, t), F32), pltpu.VMEM((nh, HEAD_DIM + 16, t), F32),
                        pltpu.VMEM((2, t, t), F32)],
    )
    return pl.pallas_call(
        functools.partial(_fox_prompt_body, nh=nh),
        grid_spec=grid_spec,
        out_shape=jax.ShapeDtypeStruct((b, s, wa), BF),
        compiler_params=_cparams(("arbitrary", "arbitrary")),
        name="fox_prompt",
    )(it, jt, qk, qk, vt)


def _band_prompt_body(q_ref, k0_ref, k1_ref, k2_ref, v0_ref, v1_ref, v2_ref, bias_ref, o_ref, st_ref, *, nh):
    i = pl.program_id(1)
    hd = HEAD_DIM
    q = q_ref[0]
    tq = q.shape[0]
    k = jnp.concatenate([k0_ref[0], k1_ref[0], k2_ref[0]], axis=0)
    vts = [v0_ref[0], v1_ref[0], v2_ref[0]]

    def scores(h):
        st_ref[h % 2] = _dot_nt(k[:, h * LANES:(h + 1) * LANES], q[:, h * LANES:(h + 1) * LANES])

    def attend(at_start):
        if at_start:
            in_seq = lax.broadcasted_iota(jnp.int32, (3 * tq, tq), 0) >= (2 - i) * tq
        outs = []
        scores(0)
        for h in range(nh):
            if h + 1 < nh:
                scores(h + 1)
            st = st_ref[h % 2] + bias_ref[h]
            if at_start:
                st = jnp.where(in_seq, st, NEG_INF)
            p = jnp.exp2(st - jnp.max(st, axis=0, keepdims=True)).astype(BF)
            acc = None
            for w, vt in enumerate(vts):
                v_aug = jnp.concatenate([vt[h * hd:(h + 1) * hd, :], vt[nh * hd:, :]], axis=0)
                part = _dot(v_aug, p[w * tq:(w + 1) * tq, :])
                acc = part if acc is None else acc + part
            outs.append(acc[:hd] / acc[hd:hd + 1])
        o_ref[0] = jnp.concatenate(outs, axis=0).T.astype(BF)

    @pl.when(i < 2)
    def _():
        attend(True)

    @pl.when(i >= 2)
    def _():
        attend(False)


def _band_bias_prompt(rel_bias, tq):
    nh = rel_bias.shape[0]
    period = 4 * tq
    t = jnp.arange(period)
    t = jnp.where(t < 3 * tq, t, t - period)
    u = rel_bias[:, jnp.clip(2 * tq - t, -MAX_REL, MAX_REL) + MAX_REL].astype(F32)
    skew = jnp.tile(u, (1, tq))[:, :tq * (period - 1)].reshape(nh, tq, period - 1)
    bias = skew[:, :, :3 * tq]
    r = jnp.arange(tq)[:, None]
    c = jnp.arange(3 * tq)[None, :]
    valid = (c // CHUNK >= r // CHUNK) & (c // CHUNK <= r // CHUNK + BAND_CHUNKS)
    return jnp.where(valid[None], bias, NEG_INF)


def _band_prompt_call(qk, vt, rel_bias, wa, wb):
    b, s, _ = qk.shape
    nh = wb // HEAD_DIM
    tq = BAND_Q_TILE
    assert s % tq == 0 and wa == wb
    bias_t = (_band_bias_prompt(rel_bias, tq) * LOG2E).transpose(0, 2, 1)
    wq = nh * LANES
    cq, ck = 2, 3

    def k_spec(back):
        return pl.BlockSpec((1, tq, wq), lambda b, i: (b, jnp.maximum(i - back, 0), ck))

    def v_spec(back):
        return pl.BlockSpec((1, wb + 16, tq), lambda b, i: (b, 1, jnp.maximum(i - back, 0)))

    return pl.pallas_call(
        functools.partial(_band_prompt_body, nh=nh),
        grid=(b, s // tq),
        in_specs=[pl.BlockSpec((1, tq, wq), lambda b, i: (b, i, cq)),
                  k_spec(2), k_spec(1), k_spec(0), v_spec(2), v_spec(1), v_spec(0),
                  _const_spec(bias_t)],
        out_specs=pl.BlockSpec((1, tq, wb), lambda b, i: (b, i, 0)),
        out_shape=jax.ShapeDtypeStruct((b, s, wb), BF),
        scratch_shapes=[pltpu.VMEM((2, 3 * tq, tq), F32)],
        compiler_params=_cparams(("arbitrary", "arbitrary")),
        name="band_prompt",
    )(qk, qk, qk, qk, vt, vt, vt, bias_t)


def _lambda(lam_ref, lam_init):
    lp = lam_ref[...]
    a = jnp.sum(lp[0:1] * lp[1:2], axis=-1, keepdims=True)
    b = jnp.sum(lp[2:3] * lp[3:4], axis=-1, keepdims=True)
    return jnp.exp(a) - jnp.exp(b) + lam_init


def _diff_prompt_body(it_ref, jt_ref, q_ref, k_ref, vt_ref, lam_ref, gcol_ref, o_ref, m_ref, acc_ref, st_ref,
                      *, nh, lam_init):
    i = it_ref[pl.program_id(1)]
    j = jt_ref[pl.program_id(1)]
    dv = 2 * HEAD_DIM

    @pl.when(j == 0)
    def _():
        _tflash_init(m_ref, acc_ref)

    def step(diag):
        q = q_ref[0]
        k = k_ref[0]
        vt = vt_ref[0]
        t = q.shape[0]
        ones = vt[nh * dv:, :]
        if diag:
            def penalty(shape):
                key = lax.broadcasted_iota(jnp.int32, shape, 0)
                qry = lax.broadcasted_iota(jnp.int32, shape, 1)
                visible = (key >> CHUNK_SHIFT) <= (qry >> CHUNK_SHIFT)
                return jnp.where(visible, 2.0 * jnp.maximum(key - qry, 0).astype(F32), -NEG_INF)
            pen_lo, pen_hi = penalty((t // 2, t)), penalty((t // 2, t // 2))

        def scores(n):
            k_blk, q_blk = k[:, n * LANES:(n + 1) * LANES], q[:, n * LANES:(n + 1) * LANES]
            if diag:
                _diag_scores(st_ref, n % 2, k_blk, q_blk)
            else:
                st_ref[n % 2] = _dot_nt(k_blk, q_blk)

        scores(0)
        for h in range(nh):
            v_aug = jnp.concatenate([vt[h * dv:(h + 1) * dv, :], ones], axis=0)
            if diag:
                c = LOG2E * _alibi_slope(h, nh)
                ex_lo, ex_hi = c * pen_lo, c * pen_hi
            for u in range(2):
                n = 2 * h + u
                if n + 1 < 2 * nh:
                    scores(n + 1)
                if diag:
                    _tflash_diag_update(st_ref, n % 2, v_aug, lambda s: s - ex_lo, lambda s: s - ex_hi,
                                        m_ref, acc_ref, n)
                else:
                    _tflash_update(st_ref[n % 2], v_aug, m_ref, acc_ref, n)

    @pl.when(j < i)
    def _():
        step(False)

    @pl.when(j == i)
    def _():
        step(True)
        lam = _lambda(lam_ref, lam_init)
        g = gcol_ref[...]
        outs = []
        for h in range(nh):
            a1 = acc_ref[2 * h]
            a2 = acc_ref[2 * h + 1]
            o = a1[:dv] / a1[dv:dv + 1] - lam * (a2[:dv] / a2[dv:dv + 1])
            o = o * lax.rsqrt(jnp.mean(o * o, axis=0, keepdims=True) + RMS_EPS)
            outs.append((o * g) * (1.0 - lam_init))
        o_ref[0] = jnp.concatenate(outs, axis=0).T.astype(BF)


def _diff_prompt_call(qk, vt, lam_par, g_col, wc, lam_init):
    b, s, _ = qk.shape
    nh = wc // (2 * HEAD_DIM)
    t = min(ATTN_TILE, s)
    it, jt = _causal_pairs(s // t)
    wq = 2 * nh * LANES
    grid_spec = pltpu.PrefetchScalarGridSpec(
        num_scalar_prefetch=2,
        grid=(b, it.shape[0]),
        in_specs=[pl.BlockSpec((1, t, wq), lambda b, s, it, jt: (b, it[s], 0)),
                  pl.BlockSpec((1, t, wq), lambda b, s, it, jt: (b, jt[s], 1)),
                  pl.BlockSpec((1, wc + 16, t), lambda b, s, it, jt: (b, 0, jt[s])),
                  pl.BlockSpec(lam_par.shape, lambda b, s, it, jt: (0, 0)),
                  pl.BlockSpec(g_col.shape, lambda b, s, it, jt: (0, 0))],
        out_specs=pl.BlockSpec((1, t, wc), lambda b, s, it, jt: (b, it[s], 0)),
        scratch_shapes=[pltpu.VMEM((2 * nh, 1, t), F32), pltpu.VMEM((2 * nh, 2 * HEAD_DIM + 16, t), F32),
                        pltpu.VMEM((2, t, t), F32)],
    )
    return pl.pallas_call(
        functools.partial(_diff_prompt_body, nh=nh, lam_init=lam_init),
        grid_spec=grid_spec,
        out_shape=jax.ShapeDtypeStruct((b, s, wc), BF),
        compiler_params=_cparams(("arbitrary", "arbitrary")),
        name="diff_prompt",
    )(it, jt, qk, qk, vt, lam_par, g_col)


def _cumsum_body(x_ref, up_ref, o_ref, carry_ref):
    @pl.when(pl.program_id(1) == 0)
    def _():
        carry_ref[...] = jnp.zeros(carry_ref.shape, F32)
    ft = _cumsum_lanes(x_ref[0], up_ref[...]) + carry_ref[:, 0:1]
    o_ref[0] = ft
    carry_ref[...] = jnp.broadcast_to(ft[:, -1:], carry_ref.shape)


def _cumsum_call(x):
    b, nh, length = x.shape
    t = 512
    up = _upper_tri(t)
    return pl.pallas_call(
        _cumsum_body,
        grid=(b, length // t),
        in_specs=[pl.BlockSpec((1, nh, t), lambda b, j: (b, 0, j)), pl.BlockSpec((t, t), lambda b, j: (0, 0))],
        out_specs=pl.BlockSpec((1, nh, t), lambda b, j: (b, 0, j)),
        out_shape=jax.ShapeDtypeStruct(x.shape, F32),
        scratch_shapes=[pltpu.VMEM((nh, LANES), F32)],
        compiler_params=_cparams(("arbitrary", "arbitrary")),
        name="logf_cumsum",
    )(x, up)


def _fox_sample_body(q_ref, kn_ref, vn_ref, ck_ref, cv_ref, ftc_ref, ftn_ref, o_ref, m_ref, l_ref, acc_ref, *, nh):
    j = pl.program_id(1)
    hd = HEAD_DIM
    q = q_ref[0]
    t = q.shape[0]

    @pl.when(j == 0)
    def _():
        _osm_init(m_ref, l_ref, acc_ref)

    k = ck_ref[0].astype(BF)
    v = cv_ref[0].astype(BF)
    ft = ftc_ref[0]
    for h in range(nh):
        sl = slice(h * hd, (h + 1) * hd)
        s = _dot_nt(q[:, sl], k[:, sl]) - ft[h:h + 1, :]
        _osm_update(s, v[:, sl], m_ref, l_ref, acc_ref, h)

    @pl.when(j == pl.num_programs(1) - 1)
    def _():
        kn = kn_ref[0]
        vn = vn_ref[0]
        ftn = ftn_ref[0][:, :t]
        visible = lax.broadcasted_iota(jnp.int32, (t, t), 1) <= lax.broadcasted_iota(jnp.int32, (t, t), 0)
        outs = []
        for h in range(nh):
            sl = slice(h * hd, (h + 1) * hd)
            s = _dot_nt(q[:, sl], kn[:, sl]) - ftn[h:h + 1, :]
            s = jnp.where(visible, s, NEG_INF)
            _osm_update(s, vn[:, sl], m_ref, l_ref, acc_ref, h)
            outs.append((acc_ref[h] / l_ref[h]).astype(BF))
        o_ref[0] = jnp.concatenate(outs, axis=-1)


def _fox_sample_call(qkv, cache_k, cache_v, ft_pad, wa):
    b, t, _ = qkv.shape
    p_len = cache_k.shape[1]
    nh = wa // HEAD_DIM
    tk = min(CACHE_TILE_A, p_len)
    assert p_len % tk == 0 and p_len % LANES == 0 and t <= LANES
    return pl.pallas_call(
        functools.partial(_fox_sample_body, nh=nh),
        grid=(b, p_len // tk),
        in_specs=[pl.BlockSpec((1, t, wa), lambda b, j: (b, 0, 0)),
                  pl.BlockSpec((1, t, wa), lambda b, j: (b, 0, 1)),
                  pl.BlockSpec((1, t, wa), lambda b, j: (b, 0, 2)),
                  pl.BlockSpec((1, tk, wa), lambda b, j: (b, j, 0)),
                  pl.BlockSpec((1, tk, wa), lambda b, j: (b, j, 0)),
                  pl.BlockSpec((1, nh, tk), lambda b, j: (b, 0, j)),
                  pl.BlockSpec((1, nh, LANES), lambda b, j: (b, 0, p_len // LANES))],
        out_specs=pl.BlockSpec((1, t, wa), lambda b, j: (b, 0, 0)),
        out_shape=jax.ShapeDtypeStruct((b, t, wa), BF),
        scratch_shapes=[pltpu.VMEM((nh, t, 1), F32), pltpu.VMEM((nh, t, 1), F32),
                        pltpu.VMEM((nh, t, HEAD_DIM), F32)],
        compiler_params=_cparams(("arbitrary", "arbitrary")),
        name="fox_sample",
    )(qkv, qkv, qkv, cache_k, cache_v, ft_pad, ft_pad)


def _band_sample_body(q_ref, kn_ref, vn_ref, bk_ref, bv_ref, bias_b_ref, bias_n_ref, o_ref, *, nh):
    hd = HEAD_DIM
    q = q_ref[0]
    kn = kn_ref[0]
    vn = vn_ref[0]
    kb = bk_ref[0].astype(BF)
    vb = bv_ref[0].astype(BF)
    outs = []
    for h in range(nh):
        sl = slice(h * hd, (h + 1) * hd)
        sb = _dot_nt(q[:, sl], kb[:, sl]) + bias_b_ref[h]
        sn = _dot_nt(q[:, sl], kn[:, sl]) + bias_n_ref[h]
        m = jnp.maximum(jnp.max(sb, axis=-1, keepdims=True), jnp.max(sn, axis=-1, keepdims=True))
        pb = jnp.exp(sb - m)
        pn = jnp.exp(sn - m)
        l = jnp.sum(pb, axis=-1, keepdims=True) + jnp.sum(pn, axis=-1, keepdims=True)
        o = _dot(pb.astype(BF), vb[:, sl]) + _dot(pn.astype(BF), vn[:, sl])
        outs.append((o / l).astype(BF))
    o_ref[0] = jnp.concatenate(outs, axis=-1)


def _band_sample_call(qkv, buf_k, buf_v, rel_bias, wb):
    b, t, _ = qkv.shape
    lb = buf_k.shape[1]
    nh = wb // HEAD_DIM
    k_pos = jnp.concatenate([jnp.arange(-lb, 0), jnp.arange(t)])
    rel = jnp.arange(t)[:, None] - k_pos[None, :]
    bias = rel_bias[:, jnp.clip(rel, -MAX_REL, MAX_REL) + MAX_REL].astype(F32)
    bias_b, bias_n = bias[:, :, :lb], bias[:, :, lb:]
    return pl.pallas_call(
        functools.partial(_band_sample_body, nh=nh),
        grid=(b,),
        in_specs=[pl.BlockSpec((1, t, wb), lambda b: (b, 0, 3)),
                  pl.BlockSpec((1, t, wb), lambda b: (b, 0, 4)),
                  pl.BlockSpec((1, t, wb), lambda b: (b, 0, 5)),
                  pl.BlockSpec((1, lb, wb), lambda b: (b, 0, 0)),
                  pl.BlockSpec((1, lb, wb), lambda b: (b, 0, 0)),
                  pl.BlockSpec(bias_b.shape, lambda b: (0, 0, 0)),
                  pl.BlockSpec(bias_n.shape, lambda b: (0, 0, 0))],
        out_specs=pl.BlockSpec((1, t, wb), lambda b: (b, 0, 0)),
        out_shape=jax.ShapeDtypeStruct((b, t, wb), BF),
        compiler_params=_cparams(("arbitrary",)),
        name="band_sample",
    )(qkv, qkv, qkv, buf_k, buf_v, bias_b, bias_n)


def _diff_sample_body(q_ref, kn_ref, vn_ref, ck_ref, cv_ref, lam_ref, gsub_ref, o_ref, m_ref, l_ref, acc_ref,
                      *, nh, lam_init, p_len):
    j = pl.program_id(1)
    q = q_ref[0]
    t = q.shape[0]
    dv = 2 * HEAD_DIM
    lane = lax.broadcasted_iota(jnp.int32, (t, dv), 1)

    @pl.when(j == 0)
    def _():
        _osm_init(m_ref, l_ref, acc_ref)

    def attend(k_of, v_of, dist):
        dist2 = jnp.concatenate([dist, dist], axis=0)
        for h in range(nh):
            x = q[:, h * dv:(h + 1) * dv]
            zero = jnp.zeros_like(x)
            q2 = jnp.concatenate([jnp.where(lane < HEAD_DIM, x, zero), jnp.where(lane >= HEAD_DIM, x, zero)], axis=0)
            s = _dot_nt(q2, k_of(h)) - _alibi_slope(h, nh) * dist2
            _osm_update(s, v_of(h), m_ref, l_ref, acc_ref, h)

    tk = ck_ref.shape[1] // nh
    row = lax.broadcasted_iota(jnp.int32, (t, tk), 0)
    col = lax.broadcasted_iota(jnp.int32, (t, tk), 1)
    attend(lambda h: ck_ref[0, pl.ds(h, tk, stride=nh), :].astype(BF),
           lambda h: cv_ref[0, pl.ds(h, tk, stride=nh), :].astype(BF),
           (p_len + row - j * tk - col).astype(F32))

    @pl.when(j == pl.num_programs(1) - 1)
    def _():
        kn = kn_ref[0]
        vn = vn_ref[0]
        rown = lax.broadcasted_iota(jnp.int32, (t, t), 0)
        coln = lax.broadcasted_iota(jnp.int32, (t, t), 1)
        attend(lambda h: kn[:, h * dv:(h + 1) * dv], lambda h: vn[:, h * dv:(h + 1) * dv],
               jnp.abs(rown - coln).astype(F32))
        lam = _lambda(lam_ref, lam_init)
        g = gsub_ref[...]
        outs = []
        for h in range(nh):
            on = acc_ref[h] / l_ref[h]
            o = on[:t] - lam * on[t:]
            o = o * lax.rsqrt(jnp.mean(o * o, axis=-1, keepdims=True) + RMS_EPS)
            outs.append(((o * g) * (1.0 - lam_init)).astype(BF))
        o_ref[0] = jnp.concatenate(outs, axis=-1)


def _diff_sample_call(qkv, cache_k, cache_v, lam_par, g_sub, wc, lam_init):
    b, t, _ = qkv.shape
    nh = wc // (2 * HEAD_DIM)
    p_len = cache_k.shape[1] // nh
    assert (p_len // CHUNK) * CHUNK == p_len and t <= CHUNK
    tk = min(CACHE_TILE_C, p_len)
    return pl.pallas_call(
        functools.partial(_diff_sample_body, nh=nh, lam_init=lam_init, p_len=p_len),
        grid=(b, p_len // tk),
        in_specs=[pl.BlockSpec((1, t, wc), lambda b, j: (b, 0, 0)),
                  pl.BlockSpec((1, t, wc), lambda b, j: (b, 0, 1)),
                  pl.BlockSpec((1, t, wc), lambda b, j: (b, 0, 2)),
                  pl.BlockSpec((1, tk * nh, 2 * HEAD_DIM), lambda b, j: (b, j, 0)),
                  pl.BlockSpec((1, tk * nh, 2 * HEAD_DIM), lambda b, j: (b, j, 0)),
                  pl.BlockSpec(lam_par.shape, lambda b, j: (0, 0)),
                  pl.BlockSpec(g_sub.shape, lambda b, j: (0, 0))],
        out_specs=pl.BlockSpec((1, t, wc), lambda b, j: (b, 0, 0)),
        out_shape=jax.ShapeDtypeStruct((b, t, wc), BF),
        scratch_shapes=[pltpu.VMEM((nh, 2 * t, 1), F32), pltpu.VMEM((nh, 2 * t, 1), F32),
                        pltpu.VMEM((nh, 2 * t, 2 * HEAD_DIM), F32)],
        compiler_params=_cparams(("arbitrary", "arbitrary")),
        name="diff_sample",
    )(qkv, qkv, qkv, cache_k, cache_v, lam_par, g_sub)


def _out_ffn_body(x_ref, oa_ref, ob_ref, gm_ref, g_ref, sh_ref, sc_ref, gf_ref, wo_ref, w1_ref, w3_ref, w2_ref, y_ref,
                  *, f_chunk):
    wa = oa_ref.shape[-1]
    attn = _dot(oa_ref[0], wo_ref[:wa, :]) + _dot(ob_ref[0], wo_ref[wa:, :])
    x1 = x_ref[0] + gm_ref[0, 0, 0] * attn
    h = _modulate(x1, g_ref[...], sh_ref[0, 0, 0], sc_ref[0, 0, 0]).astype(BF)
    f_total = w1_ref.shape[1]
    f = None
    for c0 in range(0, f_total, f_chunk):
        a = _dot(h, w1_ref[:, c0:c0 + f_chunk])
        g = _dot(h, w3_ref[:, c0:c0 + f_chunk])
        u = ((a * _sigmoid(a)) * g).astype(BF)
        part = _dot(u, w2_ref[c0:c0 + f_chunk, :])
        f = part if f is None else f + part
    y_ref[0] = x1 + gf_ref[0, 0, 0] * f


def _out_ffn_call(x, o_a, o_b, gate_m, g, shift, scale, gate_f, wo, w1, w3, w2):
    bx, sx, d = x.shape
    tm = min(ROW_TILE, sx)
    f_total = w1.shape[1]
    f_chunk = f_total // 2 if (f_total // 2) % LANES == 0 else f_total
    row = lambda b, i: (b, i, 0)
    return pl.pallas_call(
        functools.partial(_out_ffn_body, f_chunk=f_chunk),
        grid=(bx, sx // tm),
        in_specs=[pl.BlockSpec((1, tm, d), row),
                  pl.BlockSpec((1, tm, o_a.shape[-1]), row), pl.BlockSpec((1, tm, o_b.shape[-1]), row),
                  _mod_spec(gate_m, tm), _const_spec(g), _mod_spec(shift, tm), _mod_spec(scale, tm),
                  _mod_spec(gate_f, tm), _const_spec(wo), _const_spec(w1), _const_spec(w3), _const_spec(w2)],
        out_specs=pl.BlockSpec((1, tm, d), row),
        out_shape=jax.ShapeDtypeStruct((bx, sx, d), F32),
        compiler_params=_cparams(("arbitrary", "arbitrary")),
        name="out_ffn",
    )(x, o_a, o_b, gate_m[0], g, shift[0], scale[0], gate_f[0], wo, w1, w3, w2)


def _store_tile_rows(ref, x):
    rows, d = x.shape
    nseg = d // LANES
    for s in range(nseg):
        ref[pl.ds(s, rows, stride=nseg), :] = x[:, s * LANES:(s + 1) * LANES]


def _load_tile_rows(ref, rows, d):
    nseg = d // LANES
    return jnp.concatenate([ref[pl.ds(s, rows, stride=nseg), :] for s in range(nseg)], axis=-1)


def _out_router_body(x_ref, o_ref, gm_ref, g_ref, sh_ref, sc_ref, wo_ref, wr_ref, h_in_ref,
                     x1_ref, h_ref, route_ref, *, n_exp):
    del h_in_ref
    x1 = x_ref[0] + gm_ref[0, 0, 0] * _dot(o_ref[0], wo_ref[...])
    x1_ref[0] = x1
    h = _modulate(x1, g_ref[...], sh_ref[0, 0, 0], sc_ref[0, 0, 0])
    hb = h.astype(BF)
    _store_tile_rows(h_ref, hb.astype(F32))
    terms = _dot(hb, wr_ref[...])
    logits = (terms + pltpu.roll(terms, LANES - ROUTE_ROWS, 1)) + pltpu.roll(terms, LANES - 2 * ROUTE_ROWS, 1)
    lt = logits.T[:ROUTE_ROWS, :]
    row = lax.broadcasted_iota(jnp.int32, lt.shape, 0)
    lg = jnp.where(row < n_exp, lt, NEG_INF)
    e = jnp.exp(lg - jnp.max(lg, axis=0, keepdims=True))
    probs = e / jnp.sum(e, axis=0, keepdims=True)
    p1 = jnp.max(probs, axis=0, keepdims=True)
    i1 = jnp.min(jnp.where(probs == p1, row, ROUTE_ROWS), axis=0, keepdims=True)
    rest = jnp.where(row == i1, -1.0, probs)
    p2 = jnp.max(rest, axis=0, keepdims=True)
    i2 = jnp.min(jnp.where(rest == p2, row, ROUTE_ROWS), axis=0, keepdims=True)
    tot = p1 + p2
    route_ref[0] = jnp.where(row == 0, i1.astype(F32),
                             jnp.where(row == 1, i2.astype(F32),
                                       jnp.where(row == 2, p1 / tot, jnp.where(row == 3, p2 / tot, 0.0))))


def _out_router_call(x, o, gate_m, g, shift, scale, wo, wr3, n_exp, tok_offset, h_all):
    bx, sx, d = x.shape
    tm = min(ROW_TILE, sx)
    nt = sx // tm
    nseg = d // LANES
    assert tok_offset % tm == 0 and n_exp <= ROUTE_ROWS
    row = lambda b, i: (b, i, 0)
    return pl.pallas_call(
        functools.partial(_out_router_body, n_exp=n_exp),
        grid=(bx, nt),
        in_specs=[pl.BlockSpec((1, tm, d), row), pl.BlockSpec((1, tm, o.shape[-1]), row),
                  _mod_spec(gate_m, tm), _const_spec(g), _mod_spec(shift, tm), _mod_spec(scale, tm),
                  _const_spec(wo), _const_spec(wr3), pl.BlockSpec(memory_space=pl.ANY)],
        out_specs=[pl.BlockSpec((1, tm, d), row),
                   pl.BlockSpec((tm * nseg, LANES), lambda b, i: (tok_offset // tm + b * nt + i, 0)),
                   pl.BlockSpec((1, ROUTE_ROWS, tm), lambda b, i: (b, 0, i))],
        out_shape=[jax.ShapeDtypeStruct((bx, sx, d), F32),
                   jax.ShapeDtypeStruct(h_all.shape, F32),
                   jax.ShapeDtypeStruct((bx, ROUTE_ROWS, sx), F32)],
        input_output_aliases={8: 1},
        compiler_params=_cparams(("arbitrary", "arbitrary")),
        name="out_router",
    )(x, o, gate_m[0], g, shift[0], scale[0], wo, wr3, h_all)


def _moe_ffn_body(te_ref, nv_ref, idx_ref, idx_next_ref, h_ref, w1_ref, w3_ref, w2_ref, y_ref, xbuf, acc, sem,
                  *, nseg, nf):
    t = pl.program_id(0)
    j = pl.program_id(1)
    nv = nv_ref[0]
    tm, d = acc.shape
    slot = lax.rem(t, 2)

    def issue(rows_ref, s):
        def body(r, c):
            src = pl.multiple_of(rows_ref[0, 0, r], nseg)
            dst = pl.multiple_of(r * nseg, nseg)
            pltpu.make_async_copy(h_ref.at[pl.ds(src, nseg)], xbuf.at[s, pl.ds(dst, nseg)], sem.at[s]).start()
            return c
        lax.fori_loop(0, tm, body, 0, unroll=8)

    @pl.when((j == 0) & (t == 0))
    def _():
        issue(idx_ref, 0)

    @pl.when((j == 0) & (t + 1 < nv))
    def _():
        issue(idx_next_ref, 1 - slot)

    @pl.when((j == 0) & (t < nv))
    def _():
        pltpu.make_async_copy(h_ref.at[pl.ds(0, tm * nseg)], xbuf.at[slot], sem.at[slot]).wait()

    @pl.when(t < nv)
    def _():
        xb = _load_tile_rows(xbuf.at[slot], tm, d).astype(BF)
        a = _dot(xb, w1_ref[0])
        g = _dot(xb, w3_ref[0])
        u = ((a * _sigmoid(a)) * g).astype(BF)
        part = _dot(u, w2_ref[0])

        if nf == 1:
            _store_tile_rows(y_ref, part)
        else:
            @pl.when(j == 0)
            def _():
                acc[...] = part

            @pl.when((j > 0) & (j < nf - 1))
            def _():
                acc[...] += part

            @pl.when(j == nf - 1)
            def _():
                _store_tile_rows(y_ref, acc[...] + part)

    @pl.when((t >= nv) & (j == nf - 1))
    def _():
        y_ref[...] = jnp.zeros(y_ref.shape, F32)


def _moe_ffn_call(h_all, row_tok, tile_e, n_valid, w1, w3, w2):
    n_rows = row_tok.shape[0]
    d = w1.shape[1]
    nseg = d // LANES
    f_total = w1.shape[2]
    tf = MOE_F_TILE if f_total % MOE_F_TILE == 0 else f_total
    nf = f_total // tf
    tm = MOE_ROW_TILE
    n_tiles = n_rows // tm
    idx3 = (row_tok * nseg).reshape(n_tiles, 1, tm)

    def fcol(t, j, te, nv):
        return jnp.where(t < nv[0], j, nf - 1)

    grid_spec = pltpu.PrefetchScalarGridSpec(
        num_scalar_prefetch=2,
        grid=(n_tiles, nf),
        in_specs=[pl.BlockSpec((1, 1, tm), lambda t, j, te, nv: (t, 0, 0), memory_space=pltpu.SMEM),
                  pl.BlockSpec((1, 1, tm), lambda t, j, te, nv: (jnp.minimum(t + 1, n_tiles - 1), 0, 0),
                               memory_space=pltpu.SMEM),
                  pl.BlockSpec(memory_space=pl.ANY),
                  pl.BlockSpec((1, d, tf), lambda t, j, te, nv: (te[t], 0, fcol(t, j, te, nv))),
                  pl.BlockSpec((1, d, tf), lambda t, j, te, nv: (te[t], 0, fcol(t, j, te, nv))),
                  pl.BlockSpec((1, tf, d), lambda t, j, te, nv: (te[t], fcol(t, j, te, nv), 0))],
        out_specs=pl.BlockSpec((tm * nseg, LANES), lambda t, j, te, nv: (t, 0)),
        scratch_shapes=[pltpu.VMEM((2, tm * nseg, LANES), F32), pltpu.VMEM((tm, d), F32),
                        pltpu.SemaphoreType.DMA((2,))],
    )
    return pl.pallas_call(
        functools.partial(_moe_ffn_body, nseg=nseg, nf=nf),
        grid_spec=grid_spec,
        out_shape=jax.ShapeDtypeStruct((n_rows * nseg, LANES), F32),
        compiler_params=_cparams(("arbitrary", "arbitrary")),
        name="moe_ffn",
    )(tile_e, n_valid, idx3, idx3, h_all, w1, w3, w2)


def _combine_body(d0_ref, d1_ref, x_ref, gf_ref, w_ref, g_ref, yb_ref, o_ref, buf, sem):
    t = pl.program_id(0)
    nt = pl.num_programs(0)
    _, tc, d = x_ref.shape
    nseg = d // LANES

    def row_copy(idx_ref, r, slot, k):
        src = pl.multiple_of(idx_ref[0, 0, 2 * r + k], nseg)
        dst = pl.multiple_of(r * nseg, nseg)
        return pltpu.make_async_copy(yb_ref.at[pl.ds(src, nseg)], buf.at[slot, k, pl.ds(dst, nseg)], sem.at[slot])

    def issue(idx_ref, slot):
        def body(r, c):
            row_copy(idx_ref, r, slot, 0).start()
            row_copy(idx_ref, r, slot, 1).start()
            return c
        lax.fori_loop(0, tc, body, 0, unroll=4)

    slot = lax.rem(t, 2)

    @pl.when(t == 0)
    def _():
        issue(d0_ref, 0)

    @pl.when(t + 1 < nt)
    def _():
        issue(d1_ref, 1 - slot)

    for k in range(TOP_K):
        pltpu.make_async_copy(yb_ref.at[pl.ds(0, tc * nseg)], buf.at[slot, k], sem.at[slot]).wait()
    w = jnp.concatenate([w_ref[0], jnp.zeros((LANES - ROUTE_ROWS, tc), F32)], axis=0).T
    f = (w[:, 2:3] * _load_tile_rows(buf.at[slot, 0], tc, d)
         + w[:, 3:4] * _load_tile_rows(buf.at[slot, 1], tc, d))
    y = x_ref[0] + gf_ref[0, 0, 0] * f
    y = y * lax.rsqrt(jnp.mean(y * y, axis=-1, keepdims=True) + RMS_EPS)
    o_ref[0] = y * g_ref[...]


def _combine_call(x1, gate_f, route, g_final, yb, dest):
    bx, sx, d = x1.shape
    nseg = d // LANES
    tc = min(COMBINE_TILE, sx)
    nps = sx // tc
    nt = bx * nps
    dest3 = (dest * nseg).reshape(nt, 1, 2 * tc)
    row = lambda t: (t // nps, t % nps, 0)
    gate_arr, gl, gk = gate_f
    if gate_arr.shape[3] == 1:
        gate_spec = pl.BlockSpec((1, 1, 1, 1, d), lambda t: (gl, gk, t // nps, 0, 0))
    else:
        gate_spec = pl.BlockSpec((1, 1, 1, tc, d), lambda t: (gl, gk, t // nps, t % nps, 0))
    return pl.pallas_call(
        _combine_body,
        grid=(nt,),
        in_specs=[pl.BlockSpec((1, 1, 2 * tc), lambda t: (t, 0, 0), memory_space=pltpu.SMEM),
                  pl.BlockSpec((1, 1, 2 * tc), lambda t: (jnp.minimum(t + 1, nt - 1), 0, 0),
                               memory_space=pltpu.SMEM),
                  pl.BlockSpec((1, tc, d), row), gate_spec,
                  pl.BlockSpec((1, ROUTE_ROWS, tc), lambda t: (t // nps, 0, t % nps)),
                  pl.BlockSpec(g_final.shape, lambda t: (0, 0)),
                  pl.BlockSpec(memory_space=pl.ANY)],
        out_specs=pl.BlockSpec((1, tc, d), row),
        out_shape=jax.ShapeDtypeStruct((bx, sx, d), F32),
        scratch_shapes=[pltpu.VMEM((2, 2, tc * nseg, LANES), F32), pltpu.SemaphoreType.DMA((2,))],
        compiler_params=_cparams(("arbitrary",)),
        name="moe_combine",
    )(dest3, dest3, x1, gate_arr, route, g_final, yb)


def _route_plan(slot_e, n_exp, tm):
    n_slots = slot_e.shape[0]
    onehot = (slot_e[:, None] == jnp.arange(n_exp, dtype=jnp.int32)[None, :]).astype(jnp.int32)
    csum = jnp.cumsum(onehot, axis=0)
    rank = jnp.sum(csum * onehot, axis=1) - 1
    counts = csum[-1]
    padded = ((counts + tm - 1) // tm) * tm
    pad_end = jnp.cumsum(padded)
    pad_start = pad_end - padded
    dest = jnp.sum(onehot * pad_start[None, :], axis=1) + rank
    n_tiles = -(-n_slots // tm) + n_exp
    n_rows = n_tiles * tm
    n_valid = (pad_end[-1] // tm).astype(jnp.int32)
    tile_start = jnp.arange(n_tiles, dtype=jnp.int32) * tm
    tile_e = jnp.minimum(jnp.searchsorted(pad_end, tile_start, side='right'), n_exp - 1).astype(jnp.int32)
    last_e = tile_e[jnp.maximum(n_valid - 1, 0)]
    tile_e = jnp.where(jnp.arange(n_tiles) < n_valid, tile_e, last_e)
    sorted_tok = (jnp.argsort(slot_e, stable=True) // TOP_K).astype(jnp.int32)
    sorted_tok = jnp.concatenate([sorted_tok, jnp.zeros((tm,), jnp.int32)])
    first = jnp.cumsum(counts) - counts - pad_start
    tile_first = jnp.minimum(tile_start + first[tile_e], n_slots)
    row_tok = sorted_tok[(tile_first[:, None] + jnp.arange(tm, dtype=jnp.int32)[None, :]).reshape(n_rows)]
    return dest, row_tok, tile_e, n_valid.reshape(1)


def _moe(parts, h_all, g_final, w1, w3, w2):
    n_exp = w1.shape[0]
    slot_e = jnp.concatenate([r[:, :TOP_K, :].transpose(0, 2, 1).reshape(-1) for _, r, _ in parts], axis=0)
    dest, row_tok, tile_e, n_valid = _route_plan(slot_e.astype(jnp.int32), n_exp, MOE_ROW_TILE)
    yb = _moe_ffn_call(h_all, row_tok, tile_e, n_valid, w1, w3, w2)
    outs = []
    first = 0
    for x1, route, gate_f in parts:
        n_slots = x1.shape[0] * x1.shape[1] * TOP_K
        outs.append(_combine_call(x1, gate_f, route, g_final, yb, dest[first:first + n_slots]))
        first += n_slots
    return outs


def _trunk(x, mods, p, cache, tok_offset, h_all):
    sh_m0, sc_m0, gt_m0, sh_f0, sc_f0, gt_f0 = mods[0]
    sh_m1, sc_m1, gt_m1, sh_f1, sc_f1, gt_f1 = mods[1]
    wa = p['wa']
    wb = p['wb']
    wc = p['wc']
    bx, sx, d = x.shape

    if cache is None:
        keep = min(WINDOW_B, sx)
        qk, vt, ka, va, kb, vb, logf = _inproj_even_call(
            x, p['g_mix0'], sh_m0, sc_m0, p['w_in_even'], p['w_f'], p['b_f'], keep, True)
        o_a = _fox_prompt_call(qk, vt, wa)
        o_b = _band_prompt_call(qk, vt, p['rel_bias'], wa, wb)
        x_l0_state = (ka, va, logf, kb, vb)
    else:
        ck, cv, clogf, bk, bv, cck, ccv, b, t = cache
        qkv, ka, va, kb, vb, logf, lft, = _inproj_even_call(
            x, p['g_mix0'], sh_m0, sc_m0, p['w_in_even'], p['w_f'], p['b_f'], sx, False)
        nh = wa // HEAD_DIM
        p_len = ck.shape[1]
        lft_new = lft.reshape(nh, b, t).transpose(1, 0, 2)
        lp = -(-(p_len + t) // 512) * 512
        lcat = jnp.concatenate([clogf.transpose(0, 2, 1), lft_new,
                                jnp.zeros((b, nh, lp - p_len - t), F32)], axis=-1)
        ft_pad = _cumsum_call(lcat)
        qkv_b = qkv.reshape(b, t, qkv.shape[-1])
        o_a = _fox_sample_call(qkv_b, ck, cv, ft_pad, wa).reshape(1, b * t, wa)
        o_b = _band_sample_call(qkv_b, bk, bv, p['rel_bias'], wb).reshape(1, b * t, wb)
        x_l0_state = (ka, va, logf, kb, vb)
    x = _out_ffn_call(x, o_a, o_b, gt_m0, p['g_ffn0'], sh_f0, sc_f0, gt_f0,
                      p['w_out_even'], p['w1_dense'], p['w3_dense'], p['w2_dense'])

    if cache is None:
        qk, vt, kc, vc = _inproj_odd_call(x, p['g_mix1'], sh_m1, sc_m1, p['w_in_odd'], True)
        o = _diff_prompt_call(qk, vt, p['lam_par'], p['g_sub'].reshape(-1, 1), wc, p['lam_init'])
    else:
        qkv, kc, vc = _inproj_odd_call(x, p['g_mix1'], sh_m1, sc_m1, p['w_in_odd'], False)
        qkv_b = qkv.reshape(b, t, qkv.shape[-1])
        o = _diff_sample_call(qkv_b, cck, ccv, p['lam_par'], p['g_sub'], wc, p['lam_init']).reshape(1, b * t, wc)
    x1, h_all, route = _out_router_call(x, o, gt_m1, p['g_ffn1'], sh_f1, sc_f1, p['w_out_odd'], p['w_router3'],
                                        p['n_exp'], tok_offset, h_all)
    return (x1, route, gt_f1), h_all, x_l0_state, (kc, vc)


def kernel(x_prompt, x_sample, cache_a_k, cache_a_v, cache_a_logf, cache_b_k, cache_b_v, cache_c_k, cache_c_v,
           c_prompt, c_sample, w_mod, b_mod, g_mix, g_ffn, g_final, w_in_even, b_forget, rel_bias, w_out_even,
           w_in_odd, lambda_q1, lambda_k1, lambda_q2, lambda_k2, g_subln, w_out_odd, w1_dense, w3_dense, w2_dense,
           w_router, w1_moe, w3_moe, w2_moe):
    bp, sp, d = x_prompt.shape
    bs, ts, _ = x_sample.shape
    h_a = cache_a_k.shape[3]
    h_b = cache_b_k.shape[3]
    h_c = cache_c_k.shape[3]
    wa, wb, wc = h_a * HEAD_DIM, h_b * HEAD_DIM, h_c * 2 * HEAD_DIM
    n_exp = w_router.shape[-1]
    assert w_mod.shape[0] == 2, "kernel is written for the 2-layer trunk"

    we = w_in_even[0]
    w_main = jnp.concatenate([we[:, :3 * wa], we[:, 3 * wa + h_a:]], axis=1).astype(BF)
    w_f = jnp.pad(we[:, 3 * wa:3 * wa + h_a], ((0, 0), (0, LANES - h_a))).astype(BF)
    b_f = jnp.pad(b_forget[0], (0, LANES - h_a)).reshape(1, LANES).astype(F32)
    wr = jnp.pad(w_router[0], ((0, 0), (0, ROUTE_ROWS - n_exp)))
    wr_hi = wr.astype(BF)
    wr_r1 = wr - wr_hi.astype(F32)
    wr_mid = wr_r1.astype(BF)
    wr_lo = (wr_r1 - wr_mid.astype(F32)).astype(BF)
    w_router3 = jnp.pad(jnp.concatenate([wr_hi, wr_mid, wr_lo], axis=1), ((0, 0), (0, LANES - 3 * ROUTE_ROWS)))
    params = dict(
        wa=wa, wb=wb, wc=wc, n_exp=n_exp, lam_init=0.8 - 0.6 * math.exp(-0.3 * 1),
        g_mix0=g_mix[0:1], g_mix1=g_mix[1:2], g_ffn0=g_ffn[0:1], g_ffn1=g_ffn[1:2],
        g_final=g_final.reshape(1, d),
        w_in_even=w_main, w_f=w_f, b_f=b_f, rel_bias=rel_bias[0],
        w_out_even=w_out_even[0].astype(BF), w_in_odd=w_in_odd[0].astype(BF),
        lam_par=jnp.concatenate([lambda_q1, lambda_k1, lambda_q2, lambda_k2], axis=0),
        g_sub=g_subln[0:1], w_out_odd=w_out_odd[0].astype(BF),
        w1_dense=w1_dense[0].astype(BF), w3_dense=w3_dense[0].astype(BF), w2_dense=w2_dense[0].astype(BF),
        w_router3=w_router3,
        w1_moe=_cast_bf16_call(w1_moe[0]), w3_moe=_cast_bf16_call(w3_moe[0]), w2_moe=_cast_bf16_call(w2_moe[0]),
    )

    mod = _mod_call(jnp.concatenate([c_prompt, c_sample], axis=0), w_mod, b_mod)

    def mods_for(rows, per_row_len):
        m = mod[:, rows].reshape(2, -1, 6, d).transpose(0, 2, 1, 3)
        if per_row_len:
            m = jnp.repeat(m, per_row_len, axis=2)[:, :, None]
        else:
            m = m[:, :, :, None]
        return [[(m, l, k) for k in range(6)] for l in range(2)]

    h_all = jnp.zeros(((bp * sp + bs * ts) * (d // LANES), LANES), F32)
    moe_p, h_all, ev_p, od_p = _trunk(x_prompt, mods_for(slice(0, bp), 0), params, None, 0, h_all)
    cache = (cache_a_k[0].reshape(bs, -1, wa), cache_a_v[0].reshape(bs, -1, wa), cache_a_logf[0],
             cache_b_k[0].reshape(bs, -1, wb), cache_b_v[0].reshape(bs, -1, wb),
             cache_c_k[0].reshape(bs, -1, 2 * HEAD_DIM), cache_c_v[0].reshape(bs, -1, 2 * HEAD_DIM), bs, ts)
    moe_s, h_all, ev_s, od_s = _trunk(x_sample.reshape(1, bs * ts, d), mods_for(slice(bp, bp + bs), ts), params,
                                      cache, bp * sp, h_all)
    y_p, y_s = _moe([moe_p, moe_s], h_all, params['g_final'], params['w1_moe'], params['w3_moe'], params['w2_moe'])

    ka, va, logf, kb, vb = ev_p
    keep = kb.shape[1]
    out_p = (ka.reshape(1, bp, sp, h_a, HEAD_DIM), va.reshape(1, bp, sp, h_a, HEAD_DIM),
             logf.reshape(1, bp, sp, h_a),
             kb.reshape(1, bp, keep, h_b, HEAD_DIM), vb.reshape(1, bp, keep, h_b, HEAD_DIM),
             od_p[0].reshape(1, bp, sp, h_c, 2 * HEAD_DIM), od_p[1].reshape(1, bp, sp, h_c, 2 * HEAD_DIM))
    ka, va, logf, kb, vb = ev_s
    new_bk = jnp.concatenate([cache_b_k[0], kb.reshape(bs, ts, h_b, HEAD_DIM)], axis=1)[:, ts:]
    new_bv = jnp.concatenate([cache_b_v[0], vb.reshape(bs, ts, h_b, HEAD_DIM)], axis=1)[:, ts:]
    out_s = (ka.reshape(1, bs, ts, h_a, HEAD_DIM), va.reshape(1, bs, ts, h_a, HEAD_DIM),
             logf.reshape(1, bs, ts, h_a), new_bk[None], new_bv[None],
             od_s[0].reshape(1, bs, ts, h_c, 2 * HEAD_DIM), od_s[1].reshape(1, bs, ts, h_c, 2 * HEAD_DIM))
    return (y_p, y_s.reshape(bs, ts, d)) + out_p + out_s
```

```python
import functools
import math

import jax
import jax.numpy as jnp
import numpy as np
from jax import lax
from jax.experimental import pallas as pl
from jax.experimental.pallas import tpu as pltpu

BF = jnp.bfloat16
F32 = jnp.float32

CHUNK = 64
CHUNK_SHIFT = 6
HEAD_DIM = 64
BAND_CHUNKS = 8
WINDOW_B = BAND_CHUNKS * CHUNK
MAX_REL = 128
TOP_K = 2
RMS_EPS = 1e-6
NEG_INF = -1e30
QK_SCALE = HEAD_DIM ** -0.5
LOG2E = math.log2(math.e)
PROMPT_Q_SCALE = QK_SCALE * LOG2E


def _bf16_terms(c, n):
    terms = []
    for _ in range(n):
        t = float(np.asarray(c, dtype=jnp.bfloat16))
        terms.append(t)
        c -= t
    return terms


LOG2E_TERMS = _bf16_terms(LOG2E, 3)

LANES = 128
SUBLANES = 8
VMEM_LIMIT_BYTES = 56 * 1024 * 1024
ROUTE_ROWS = SUBLANES

ROW_TILE = 512
ATTN_TILE = 512
BAND_Q_TILE = WINDOW_B // 2
CACHE_TILE_A = 2048
CACHE_TILE_C = 1024
MOE_ROW_TILE = 512
MOE_F_TILE = 1792
COMBINE_TILE = 512
CAST_ROWS = 1024


def _cparams(sem):
    return pltpu.CompilerParams(dimension_semantics=sem, vmem_limit_bytes=VMEM_LIMIT_BYTES)


def _dot(a, b):
    return jnp.dot(a, b, preferred_element_type=F32)


def _dot_nt(a, b):
    return lax.dot_general(a, b, (((1,), (1,)), ((), ())), preferred_element_type=F32)


def _sigmoid(x):
    return 1.0 / (1.0 + jnp.exp(-x))


def _modulate(x, g, shift, scale):
    y = x * lax.rsqrt(jnp.mean(x * x, axis=-1, keepdims=True) + RMS_EPS)
    return (y * g) * (1.0 + scale) + shift


def _split3(x):
    hi = x.astype(BF)
    r1 = x - hi.astype(F32)
    mid = r1.astype(BF)
    lo = (r1 - mid.astype(F32)).astype(BF)
    return hi, mid, lo


def _cumsum_lanes(x, upper):
    hi, mid, lo = _split3(x)
    return _dot(hi, upper) + _dot(mid, upper) + _dot(lo, upper)


def _upper_tri(t):
    r = jnp.arange(t)
    return (r[:, None] <= r[None, :]).astype(BF)


def _osm_update(s, v, m_ref, l_ref, acc_ref, idx):
    m_prev = m_ref[idx]
    m_new = jnp.maximum(m_prev, jnp.max(s, axis=-1, keepdims=True))
    alpha = jnp.exp(m_prev - m_new)
    p = jnp.exp(s - m_new)
    l_ref[idx] = alpha * l_ref[idx] + jnp.sum(p, axis=-1, keepdims=True)
    acc_ref[idx] = alpha * acc_ref[idx] + _dot(p.astype(BF), v)
    m_ref[idx] = m_new


def _osm_init(m_ref, l_ref, acc_ref):
    m_ref[...] = jnp.full(m_ref.shape, NEG_INF, F32)
    l_ref[...] = jnp.zeros(l_ref.shape, F32)
    acc_ref[...] = jnp.zeros(acc_ref.shape, F32)


def _cast_body(x_ref, o_ref):
    o_ref[...] = x_ref[...].astype(BF)


def _cast_bf16_call(w):
    e, k, n = w.shape
    tk = CAST_ROWS if k % CAST_ROWS == 0 else k
    spec = pl.BlockSpec((1, tk, n), lambda e, i: (e, i, 0))
    return pl.pallas_call(
        _cast_body, grid=(e, k // tk), in_specs=[spec], out_specs=spec,
        out_shape=jax.ShapeDtypeStruct(w.shape, BF),
        compiler_params=_cparams(("arbitrary", "arbitrary")),
        name="cast_bf16",
    )(w)


def _mod_body(c_ref, w_ref, b_ref, o_ref):
    c = c_ref[...]
    s = (c * _sigmoid(c)).astype(BF)
    o_ref[0] = _dot(s, w_ref[0].astype(BF)) + b_ref[0]


def _mod_call(c_all, w_mod, b_mod):
    depth, d, n = w_mod.shape
    r = c_all.shape[0]
    tn = 1536 if n % 1536 == 0 else n
    return pl.pallas_call(
        _mod_body,
        grid=(depth, n // tn),
        in_specs=[pl.BlockSpec((r, d), lambda l, j: (0, 0)),
                  pl.BlockSpec((1, d, tn), lambda l, j: (l, 0, j)),
                  pl.BlockSpec((1, 1, tn), lambda l, j: (l, 0, j))],
        out_specs=pl.BlockSpec((1, r, tn), lambda l, j: (l, 0, j)),
        out_shape=jax.ShapeDtypeStruct((depth, r, n), F32),
        compiler_params=_cparams(("arbitrary", "arbitrary")),
        name="mod",
    )(c_all, w_mod, b_mod.reshape(depth, 1, n))


def _mod_spec(m, tm):
    arr, l, k = m
    d = arr.shape[-1]
    if arr.shape[3] == 1:
        return pl.BlockSpec((1, 1, 1, 1, d), lambda b, i: (l, k, b, 0, 0))
    return pl.BlockSpec((1, 1, 1, tm, d), lambda b, i: (l, k, b, i, 0))


def _const_spec(arr):
    nd = arr.ndim
    return pl.BlockSpec(arr.shape, lambda b, i: (0,) * nd, pipeline_mode=pl.Buffered(1))


def _head_blocks(x, augs):
    tm, w = x.shape
    first = lax.broadcasted_iota(jnp.int32, (tm, LANES), 1) < HEAD_DIM
    out = []
    for p in range(w // LANES):
        pair = x[:, p * LANES:(p + 1) * LANES]
        out.append(jnp.where(first, pair, augs[2 * p]).astype(BF))
        out.append(jnp.where(first, pltpu.roll(pair, HEAD_DIM, 1), augs[2 * p + 1]).astype(BF))
    return out


def _lane_range_row(lo, hi, value):
    lane = lax.broadcasted_iota(jnp.int32, (1, LANES), 1)
    return jnp.where((lane >= lo) & (lane < hi), value, 0.0).astype(F32)


def _ones_rows(t):
    return jnp.where(lax.broadcasted_iota(jnp.int32, (16, t), 0) == 0, 1.0, 0.0).astype(BF)


def _log_forget(h, wf_ref, bf_ref):
    z = _dot(h, wf_ref[...]) + bf_ref[...]
    return jnp.minimum(z, 0.0) - jnp.log1p(jnp.exp(-jnp.abs(z)))


def _inproj_even_body(*refs, wa, wb, prompt):
    if prompt:
        (x_ref, g_ref, sh_ref, sc_ref, w_ref, wf_ref, bf_ref, low_ref, place_ref,
         qk_ref, vt_ref, ka_ref, va_ref, kb_ref, vb_ref, logf_ref, carry_ref) = refs
    else:
        (x_ref, g_ref, sh_ref, sc_ref, w_ref, wf_ref, bf_ref,
         qkv_ref, ka_ref, va_ref, kb_ref, vb_ref, logf_ref, lft_ref) = refs
    i = pl.program_id(1)
    h = _modulate(x_ref[0], g_ref[...], sh_ref[0, 0, 0], sc_ref[0, 0, 0]).astype(BF)
    tm = h.shape[0]
    nh = logf_ref.shape[-1]

    qa = _dot(h, w_ref[:, 0:wa])
    ka = _dot(h, w_ref[:, wa:2 * wa])
    va = _dot(h, w_ref[:, 2 * wa:3 * wa])
    o = 3 * wa
    qb = _dot(h, w_ref[:, o:o + wb])
    kb = _dot(h, w_ref[:, o + wb:o + 2 * wb])
    vb = _dot(h, w_ref[:, o + 2 * wb:o + 3 * wb])
    ka_ref[0] = ka
    va_ref[0] = va
    kb_ref[0] = kb
    vb_ref[0] = vb
    logf = _log_forget(h, wf_ref, bf_ref)
    logf_ref[0] = logf[:, :nh]

    if not prompt:
        qkv_ref[0] = jnp.concatenate(
            [(qa * QK_SCALE).astype(BF), ka.astype(BF), va.astype(BF),
             (qb * QK_SCALE).astype(BF), kb.astype(BF), vb.astype(BF)], axis=-1)
        lft_ref[0] = logf.T[:nh, :]
        return

    @pl.when(i == 0)
    def _():
        carry_ref[...] = jnp.zeros(carry_ref.shape, F32)

    terms = _dot(low_ref[...], jnp.concatenate(_split3(logf), axis=1))
    f_cum = terms[:, :LANES] + terms[:, LANES:2 * LANES] + terms[:, 2 * LANES:] + carry_ref[0:1, :]
    carry_ref[...] = jnp.broadcast_to(f_cum[tm - 1:tm, :], carry_ref.shape)
    k_aug = _dot(jnp.concatenate(_split3(-f_cum), axis=1), place_ref[...])
    q_log2e = sum(_lane_range_row(HEAD_DIM + 3 * r, HEAD_DIM + 3 * r + 3, c) for r, c in enumerate(LOG2E_TERMS))
    zero = jnp.zeros((1, LANES), F32)
    nhb = wb // HEAD_DIM
    blocks = (_head_blocks(qa * PROMPT_Q_SCALE, [q_log2e] * nh)
              + _head_blocks(ka, [k_aug[:, n * LANES:(n + 1) * LANES] for n in range(nh)])
              + _head_blocks(qb * PROMPT_Q_SCALE, [zero] * nhb)
              + _head_blocks(kb, [zero] * nhb))
    qk_ref[0] = jnp.concatenate(blocks, axis=-1)
    ones = _ones_rows(tm)
    vt_ref[0] = jnp.concatenate([va.T.astype(BF), ones, vb.T.astype(BF), ones], axis=0)


def _lower_tri(t):
    r = jnp.arange(t)
    return (r[:, None] >= r[None, :]).astype(BF)


def _place_matrices(nh):
    s = jnp.arange(3)[:, None, None]
    r = jnp.arange(LANES)[None, :, None]
    c = jnp.arange(nh * LANES)[None, None, :]
    off = c - r * LANES - HEAD_DIM - s
    return ((r < nh) & ((off == 0) | (off == 3) | (off == 6))).astype(BF).reshape(3 * LANES, nh * LANES)


def _inproj_even_call(x, g, shift, scale, w_main, w_f, b_f, keep, prompt):
    bx, sx, d = x.shape
    tm = min(ROW_TILE, sx)
    nt = sx // tm
    n = w_main.shape[1]
    wa = wb = n // 6
    nh = wa // HEAD_DIM
    nkeep = keep // tm
    tail = lambda b, i: (b, jnp.maximum(i - (nt - nkeep), 0), 0)
    row = lambda b, i: (b, i, 0)
    in_specs = [pl.BlockSpec((1, tm, d), row), _const_spec(g), _mod_spec(shift, tm), _mod_spec(scale, tm),
                _const_spec(w_main), _const_spec(w_f), _const_spec(b_f)]
    args = [x, g, shift[0], scale[0], w_main, w_f, b_f]
    f32_specs = [pl.BlockSpec((1, tm, wa), row), pl.BlockSpec((1, tm, wa), row),
                 pl.BlockSpec((1, tm, wb), tail), pl.BlockSpec((1, tm, wb), tail),
                 pl.BlockSpec((1, tm, nh), row)]
    f32_shapes = [jax.ShapeDtypeStruct((bx, sx, wa), F32), jax.ShapeDtypeStruct((bx, sx, wa), F32),
                  jax.ShapeDtypeStruct((bx, keep, wb), F32), jax.ShapeDtypeStruct((bx, keep, wb), F32),
                  jax.ShapeDtypeStruct((bx, sx, nh), F32)]
    scratch = []
    if prompt:
        low, place = _lower_tri(tm), _place_matrices(nh)
        in_specs += [_const_spec(low), _const_spec(place)]
        args += [low, place]
        nqk = 2 * (wa + wb) // HEAD_DIM * LANES
        nvt = wa + wb + 32
        out_specs = [pl.BlockSpec((1, tm, nqk), row), pl.BlockSpec((1, nvt, tm), lambda b, i: (b, 0, i))] + f32_specs
        out_shape = [jax.ShapeDtypeStruct((bx, sx, nqk), BF), jax.ShapeDtypeStruct((bx, nvt, sx), BF)] + f32_shapes
        scratch.append(pltpu.VMEM((8, LANES), F32))
    else:
        out_specs = [pl.BlockSpec((1, tm, n), row)] + f32_specs + [pl.BlockSpec((1, nh, tm), lambda b, i: (b, 0, i))]
        out_shape = [jax.ShapeDtypeStruct((bx, sx, n), BF)] + f32_shapes + [jax.ShapeDtypeStruct((bx, nh, sx), F32)]
    return pl.pallas_call(
        functools.partial(_inproj_even_body, wa=wa, wb=wb, prompt=prompt),
        grid=(bx, nt), in_specs=in_specs, out_specs=out_specs, out_shape=out_shape,
        scratch_shapes=scratch,
        compiler_params=_cparams(("arbitrary", "arbitrary")),
        name="inproj_even",
    )(*args)


def _inproj_odd_body(*refs, wc, prompt):
    if prompt:
        x_ref, g_ref, sh_ref, sc_ref, w_ref, qk_ref, vt_ref, k_ref, v_ref = refs
    else:
        x_ref, g_ref, sh_ref, sc_ref, w_ref, qkv_ref, k_ref, v_ref = refs
    i = pl.program_id(1)
    h = _modulate(x_ref[0], g_ref[...], sh_ref[0, 0, 0], sc_ref[0, 0, 0]).astype(BF)
    tm = h.shape[0]
    q = _dot(h, w_ref[:, 0:wc])
    k = _dot(h, w_ref[:, wc:2 * wc])
    v = _dot(h, w_ref[:, 2 * wc:3 * wc])
    k_ref[0] = k
    v_ref[0] = v
    if not prompt:
        qkv_ref[0] = jnp.concatenate([(q * QK_SCALE).astype(BF), k.astype(BF), v.astype(BF)], axis=-1)
        return
    nh = wc // LANES
    pos = (i * tm + lax.broadcasted_iota(jnp.int32, (tm, LANES), 0)).astype(F32)
    pos_hi = pos.astype(BF).astype(F32)
    lane = lax.broadcasted_iota(jnp.int32, (tm, LANES), 1) - HEAD_DIM
    in_aug = (lane >= 0) & (lane < 2 * len(LOG2E_TERMS))
    k_aug = jnp.where(in_aug, jnp.where((lane & 1) == 0, pos_hi, pos - pos_hi), 0.0)
    q_augs = []
    for hh in range(nh):
        slope = _alibi_slope(hh, nh)
        row = sum(_lane_range_row(HEAD_DIM + 2 * r, HEAD_DIM + 2 * r + 2, slope * c) for r, c in enumerate(LOG2E_TERMS))
        q_augs += [row] * 2
    blocks = _head_blocks(q * PROMPT_Q_SCALE, q_augs) + _head_blocks(k, [k_aug] * (2 * nh))
    qk_ref[0] = jnp.concatenate(blocks, axis=-1)
    vt_ref[0] = jnp.concatenate([v.T.astype(BF), _ones_rows(tm)], axis=0)


def _alibi_slope(h, nh):
    assert 8 % nh == 0, "slopes must be exact powers of two to ride in bf16"
    return 2.0 ** (-8.0 * (h + 1) / nh)


def _inproj_odd_call(x, g, shift, scale, w, prompt):
    bx, sx, d = x.shape
    tm = min(ROW_TILE, sx)
    n = w.shape[1]
    wc = n // 3
    row = lambda b, i: (b, i, 0)
    f32_specs = [pl.BlockSpec((1, tm, wc), row), pl.BlockSpec((1, tm, wc), row)]
    f32_shapes = [jax.ShapeDtypeStruct((bx, sx, wc), F32), jax.ShapeDtypeStruct((bx, sx, wc), F32)]
    if prompt:
        nqk = 2 * wc // HEAD_DIM * LANES
        out_specs = [pl.BlockSpec((1, tm, nqk), row), pl.BlockSpec((1, wc + 16, tm), lambda b, i: (b, 0, i))]
        out_shape = [jax.ShapeDtypeStruct((bx, sx, nqk), BF), jax.ShapeDtypeStruct((bx, wc + 16, sx), BF)]
    else:
        out_specs = [pl.BlockSpec((1, tm, n), row)]
        out_shape = [jax.ShapeDtypeStruct((bx, sx, n), BF)]
    return pl.pallas_call(
        functools.partial(_inproj_odd_body, wc=wc, prompt=prompt),
        grid=(bx, sx // tm),
        in_specs=[pl.BlockSpec((1, tm, d), row), _const_spec(g), _mod_spec(shift, tm), _mod_spec(scale, tm),
                  _const_spec(w)],
        out_specs=out_specs + f32_specs, out_shape=out_shape + f32_shapes,
        compiler_params=_cparams(("arbitrary", "arbitrary")),
        name="inproj_odd",
    )(x, g, shift[0], scale[0], w)


def _tflash_update(st, v_aug, m_ref, acc_ref, idx):
    m_prev = m_ref[idx]
    m_new = jnp.maximum(m_prev, jnp.max(st, axis=0, keepdims=True))
    alpha = jnp.exp2(m_prev - m_new)
    p = jnp.exp2(st - m_new).astype(BF)
    acc_ref[idx] = alpha * acc_ref[idx] + _dot(v_aug, p)
    m_ref[idx] = m_new


def _diag_scores(st_ref, slot, k_blk, q_blk):
    hf = k_blk.shape[0] // 2
    st_ref[slot, :hf, :] = _dot_nt(k_blk[:hf], q_blk)
    st_ref[slot, hf:, hf:] = _dot_nt(k_blk[hf:], q_blk[hf:])


def _tflash_diag_update(st_ref, slot, v_aug, fix_lo, fix_hi, m_ref, acc_ref, idx):
    hf = st_ref.shape[1] // 2
    lo = fix_lo(st_ref[slot, :hf, :])
    hi = fix_hi(st_ref[slot, hf:, hf:])
    m_prev = m_ref[idx]
    m_lo = jnp.maximum(m_prev, jnp.max(lo, axis=0, keepdims=True))
    m_new = jnp.concatenate([m_lo[:, :hf], jnp.maximum(m_lo[:, hf:], jnp.max(hi, axis=0, keepdims=True))], axis=1)
    alpha = jnp.exp2(m_prev - m_new)
    p_lo = jnp.exp2(lo - m_new).astype(BF)
    p_hi = jnp.exp2(hi - m_new[:, hf:]).astype(BF)
    upd = _dot(v_aug[:, :hf], p_lo)
    upd = jnp.concatenate([upd[:, :hf], upd[:, hf:] + _dot(v_aug[:, hf:], p_hi)], axis=1)
    acc_ref[idx] = alpha * acc_ref[idx] + upd
    m_ref[idx] = m_new


def _tflash_init(m_ref, acc_ref):
    m_ref[...] = jnp.full(m_ref.shape, NEG_INF, F32)
    acc_ref[...] = jnp.zeros(acc_ref.shape, F32)


def _causal_pairs(nq):
    pairs = [(i, j) for i in range(nq) for j in range(i + 1)]
    return jnp.asarray([p[0] for p in pairs], jnp.int32), jnp.asarray([p[1] for p in pairs], jnp.int32)


def _fox_prompt_body(it_ref, jt_ref, q_ref, k_ref, vt_ref, o_ref, m_ref, acc_ref, st_ref, *, nh):
    i = it_ref[pl.program_id(1)]
    j = jt_ref[pl.program_id(1)]
    hd = HEAD_DIM

    @pl.when(j == 0)
    def _():
        _tflash_init(m_ref, acc_ref)

    def step(diag):
        q = q_ref[0]
        k = k_ref[0]
        vt = vt_ref[0]
        t = q.shape[0]
        ones = vt[nh * hd:, :]
        if diag:
            causal = lambda s: jnp.where(lax.broadcasted_iota(jnp.int32, s.shape, 0)
                                         <= lax.broadcasted_iota(jnp.int32, s.shape, 1), s, NEG_INF)

        def scores(h):
            k_blk, q_blk = k[:, h * LANES:(h + 1) * LANES], q[:, h * LANES:(h + 1) * LANES]
            if diag:
                _diag_scores(st_ref, h % 2, k_blk, q_blk)
            else:
                st_ref[h % 2] = _dot_nt(k_blk, q_blk)

        scores(0)
        for h in range(nh):
            if h + 1 < nh:
                scores(h + 1)
            v_aug = jnp.concatenate([vt[h * hd:(h + 1) * hd, :], ones], axis=0)
            if diag:
                _tflash_diag_update(st_ref, h % 2, v_aug, causal, causal, m_ref, acc_ref, h)
            else:
                _tflash_update(st_ref[h % 2], v_aug, m_ref, acc_ref, h)

    @pl.when(j < i)
    def _():
        step(False)

    @pl.when(j == i)
    def _():
        step(True)
        ot = jnp.concatenate([acc_ref[h][:hd] / acc_ref[h][hd:hd + 1] for h in range(nh)], axis=0)
        o_ref[0] = ot.T.astype(BF)


def _fox_prompt_call(qk, vt, wa):
    b, s, _ = qk.shape
    nh = wa // HEAD_DIM
    t = min(ATTN_TILE, s)
    it, jt = _causal_pairs(s // t)
    wq = nh * LANES
    grid_spec = pltpu.PrefetchScalarGridSpec(
        num_scalar_prefetch=2,
        grid=(b, it.shape[0]),
        in_specs=[pl.BlockSpec((1, t, wq), lambda b, s, it, jt: (b, it[s], 0)),
                  pl.BlockSpec((1, t, wq), lambda b, s, it, jt: (b, jt[s], 1)),
                  pl.BlockSpec((1, wa + 16, t), lambda b, s, it, jt: (b, 0, jt[s]))],
        out_specs=pl.BlockSpec((1, t, wa), lambda b, s, it, jt: (b, it[s], 0)),
        scratch_shapes=[pltpu.VMEM((nh, 1, t), F32), pltpu.VMEM((nh, HEAD_DIM + 16, t), F32),
                        pltpu.VMEM((2, t, t), F32)],
    )
    return pl.pallas_call(
        functools.partial(_fox_prompt_body, nh=nh),
        grid_spec=grid_spec,
        out_shape=jax.ShapeDtypeStruct((b, s, wa), BF),
        compiler_params=_cparams(("arbitrary", "arbitrary")),
        name="fox_prompt",
    )(it, jt, qk, qk, vt)


def _band_prompt_body(q_ref, k0_ref, k1_ref, k2_ref, v0_ref, v1_ref, v2_ref, bias_ref, o_ref, st_ref, *, nh):
    i = pl.program_id(1)
    hd = HEAD_DIM
    q = q_ref[0]
    tq = q.shape[0]
    k = jnp.concatenate([k0_ref[0], k1_ref[0], k2_ref[0]], axis=0)
    vts = [v0_ref[0], v1_ref[0], v2_ref[0]]

    def scores(h):
        st_ref[h % 2] = _dot_nt(k[:, h * LANES:(h + 1) * LANES], q[:, h * LANES:(h + 1) * LANES])

    def attend(at_start):
        if at_start:
            in_seq = lax.broadcasted_iota(jnp.int32, (3 * tq, tq), 0) >= (2 - i) * tq
        outs = []
        scores(0)
        for h in range(nh):
            if h + 1 < nh:
                scores(h + 1)
            st = st_ref[h % 2] + bias_ref[h]
            if at_start:
                st = jnp.where(in_seq, st, NEG_INF)
            p = jnp.exp2(st - jnp.max(st, axis=0, keepdims=True)).astype(BF)
            acc = None
            for w, vt in enumerate(vts):
                v_aug = jnp.concatenate([vt[h * hd:(h + 1) * hd, :], vt[nh * hd:, :]], axis=0)
                part = _dot(v_aug, p[w * tq:(w + 1) * tq, :])
                acc = part if acc is None else acc + part
            outs.append(acc[:hd] / acc[hd:hd + 1])
        o_ref[0] = jnp.concatenate(outs, axis=0).T.astype(BF)

    @pl.when(i < 2)
    def _():
        attend(True)

    @pl.when(i >= 2)
    def _():
        attend(False)


def _band_bias_prompt(rel_bias, tq):
    nh = rel_bias.shape[0]
    period = 4 * tq
    t = jnp.arange(period)
    t = jnp.where(t < 3 * tq, t, t - period)
    u = rel_bias[:, jnp.clip(2 * tq - t, -MAX_REL, MAX_REL) + MAX_REL].astype(F32)
    skew = jnp.tile(u, (1, tq))[:, :tq * (period - 1)].reshape(nh, tq, period - 1)
    bias = skew[:, :, :3 * tq]
    r = jnp.arange(tq)[:, None]
    c = jnp.arange(3 * tq)[None, :]
    valid = (c // CHUNK >= r // CHUNK) & (c // CHUNK <= r // CHUNK + BAND_CHUNKS)
    return jnp.where(valid[None], bias, NEG_INF)


def _band_prompt_call(qk, vt, rel_bias, wa, wb):
    b, s, _ = qk.shape
    nh = wb // HEAD_DIM
    tq = BAND_Q_TILE
    assert s % tq == 0 and wa == wb
    bias_t = (_band_bias_prompt(rel_bias, tq) * LOG2E).transpose(0, 2, 1)
    wq = nh * LANES
    cq, ck = 2, 3

    def k_spec(back):
        return pl.BlockSpec((1, tq, wq), lambda b, i: (b, jnp.maximum(i - back, 0), ck))

    def v_spec(back):
        return pl.BlockSpec((1, wb + 16, tq), lambda b, i: (b, 1, jnp.maximum(i - back, 0)))

    return pl.pallas_call(
        functools.partial(_band_prompt_body, nh=nh),
        grid=(b, s // tq),
        in_specs=[pl.BlockSpec((1, tq, wq), lambda b, i: (b, i, cq)),
                  k_spec(2), k_spec(1), k_spec(0), v_spec(2), v_spec(1), v_spec(0),
                  _const_spec(bias_t)],
        out_specs=pl.BlockSpec((1, tq, wb), lambda b, i: (b, i, 0)),
        out_shape=jax.ShapeDtypeStruct((b, s, wb), BF),
        scratch_shapes=[pltpu.VMEM((2, 3 * tq, tq), F32)],
        compiler_params=_cparams(("arbitrary", "arbitrary")),
        name="band_prompt",
    )(qk, qk, qk, qk, vt, vt, vt, bias_t)


def _lambda(lam_ref, lam_init):
    lp = lam_ref[...]
    a = jnp.sum(lp[0:1] * lp[1:2], axis=-1, keepdims=True)
    b = jnp.sum(lp[2:3] * lp[3:4], axis=-1, keepdims=True)
    return jnp.exp(a) - jnp.exp(b) + lam_init


def _diff_prompt_body(it_ref, jt_ref, q_ref, k_ref, vt_ref, lam_ref, gcol_ref, o_ref, m_ref, acc_ref, st_ref,
                      *, nh, lam_init):
    i = it_ref[pl.program_id(1)]
    j = jt_ref[pl.program_id(1)]
    dv = 2 * HEAD_DIM

    @pl.when(j == 0)
    def _():
        _tflash_init(m_ref, acc_ref)

    def step(diag):
        q = q_ref[0]
        k = k_ref[0]
        vt = vt_ref[0]
        t = q.shape[0]
        ones = vt[nh * dv:, :]
        if diag:
            def penalty(shape):
                key = lax.broadcasted_iota(jnp.int32, shape, 0)
                qry = lax.broadcasted_iota(jnp.int32, shape, 1)
                visible = (key >> CHUNK_SHIFT) <= (qry >> CHUNK_SHIFT)
                return jnp.where(visible, 2.0 * jnp.maximum(key - qry, 0).astype(F32), -NEG_INF)
            pen_lo, pen_hi = penalty((t // 2, t)), penalty((t // 2, t // 2))

        def scores(n):
            k_blk, q_blk = k[:, n * LANES:(n + 1) * LANES], q[:, n * LANES:(n + 1) * LANES]
            if diag:
                _diag_scores(st_ref, n % 2, k_blk, q_blk)
            else:
                st_ref[n % 2] = _dot_nt(k_blk, q_blk)

        scores(0)
        for h in range(nh):
            v_aug = jnp.concatenate([vt[h * dv:(h + 1) * dv, :], ones], axis=0)
            if diag:
                c = LOG2E * _alibi_slope(h, nh)
                ex_lo, ex_hi = c * pen_lo, c * pen_hi
            for u in range(2):
                n = 2 * h + u
                if n + 1 < 2 * nh:
                    scores(n + 1)
                if diag:
                    _tflash_diag_update(st_ref, n % 2, v_aug, lambda s: s - ex_lo, lambda s: s - ex_hi,
                                        m_ref, acc_ref, n)
                else:
                    _tflash_update(st_ref[n % 2], v_aug, m_ref, acc_ref, n)

    @pl.when(j < i)
    def _():
        step(False)

    @pl.when(j == i)
    def _():
        step(True)
        lam = _lambda(lam_ref, lam_init)
        g = gcol_ref[...]
        outs = []
        for h in range(nh):
            a1 = acc_ref[2 * h]
            a2 = acc_ref[2 * h + 1]
            o = a1[:dv] / a1[dv:dv + 1] - lam * (a2[:dv] / a2[dv:dv + 1])
            o = o * lax.rsqrt(jnp.mean(o * o, axis=0, keepdims=True) + RMS_EPS)
            outs.append((o * g) * (1.0 - lam_init))
        o_ref[0] = jnp.concatenate(outs, axis=0).T.astype(BF)


def _diff_prompt_call(qk, vt, lam_par, g_col, wc, lam_init):
    b, s, _ = qk.shape
    nh = wc // (2 * HEAD_DIM)
    t = min(ATTN_TILE, s)
    it, jt = _causal_pairs(s // t)
    wq = 2 * nh * LANES
    grid_spec = pltpu.PrefetchScalarGridSpec(
        num_scalar_prefetch=2,
        grid=(b, it.shape[0]),
        in_specs=[pl.BlockSpec((1, t, wq), lambda b, s, it, jt: (b, it[s], 0)),
                  pl.BlockSpec((1, t, wq), lambda b, s, it, jt: (b, jt[s], 1)),
                  pl.BlockSpec((1, wc + 16, t), lambda b, s, it, jt: (b, 0, jt[s])),
                  pl.BlockSpec(lam_par.shape, lambda b, s, it, jt: (0, 0)),
                  pl.BlockSpec(g_col.shape, lambda b, s, it, jt: (0, 0))],
        out_specs=pl.BlockSpec((1, t, wc), lambda b, s, it, jt: (b, it[s], 0)),
        scratch_shapes=[pltpu.VMEM((2 * nh, 1, t), F32), pltpu.VMEM((2 * nh, 2 * HEAD_DIM + 16, t), F32),
                        pltpu.VMEM((2, t, t), F32)],
    )
    return pl.pallas_call(
        functools.partial(_diff_prompt_body, nh=nh, lam_init=lam_init),
        grid_spec=grid_spec,
        out_shape=jax.ShapeDtypeStruct((b, s, wc), BF),
        compiler_params=_cparams(("arbitrary", "arbitrary")),
        name="diff_prompt",
    )(it, jt, qk, qk, vt, lam_par, g_col)


def _cumsum_body(x_ref, up_ref, o_ref, carry_ref):
    @pl.when(pl.program_id(1) == 0)
    def _():
        carry_ref[...] = jnp.zeros(carry_ref.shape, F32)
    ft = _cumsum_lanes(x_ref[0], up_ref[...]) + carry_ref[:, 0:1]
    o_ref[0] = ft
    carry_ref[...] = jnp.broadcast_to(ft[:, -1:], carry_ref.shape)


def _cumsum_call(x):
    b, nh, length = x.shape
    t = 512
    up = _upper_tri(t)
    return pl.pallas_call(
        _cumsum_body,
        grid=(b, length // t),
        in_specs=[pl.BlockSpec((1, nh, t), lambda b, j: (b, 0, j)), pl.BlockSpec((t, t), lambda b, j: (0, 0))],
        out_specs=pl.BlockSpec((1, nh, t), lambda b, j: (b, 0, j)),
        out_shape=jax.ShapeDtypeStruct(x.shape, F32),
        scratch_shapes=[pltpu.VMEM((nh, LANES), F32)],
        compiler_params=_cparams(("arbitrary", "arbitrary")),
        name="logf_cumsum",
    )(x, up)


def _fox_sample_body(q_ref, kn_ref, vn_ref, ck_ref, cv_ref, ftc_ref, ftn_ref, o_ref, m_ref, l_ref, acc_ref, *, nh):
    j = pl.program_id(1)
    hd = HEAD_DIM
    q = q_ref[0]
    t = q.shape[0]

    @pl.when(j == 0)
    def _():
        _osm_init(m_ref, l_ref, acc_ref)

    k = ck_ref[0].astype(BF)
    v = cv_ref[0].astype(BF)
    ft = ftc_ref[0]
    for h in range(nh):
        sl = slice(h * hd, (h + 1) * hd)
        s = _dot_nt(q[:, sl], k[:, sl]) - ft[h:h + 1, :]
        _osm_update(s, v[:, sl], m_ref, l_ref, acc_ref, h)

    @pl.when(j == pl.num_programs(1) - 1)
    def _():
        kn = kn_ref[0]
        vn = vn_ref[0]
        ftn = ftn_ref[0][:, :t]
        visible = lax.broadcasted_iota(jnp.int32, (t, t), 1) <= lax.broadcasted_iota(jnp.int32, (t, t), 0)
        outs = []
        for h in range(nh):
            sl = slice(h * hd, (h + 1) * hd)
            s = _dot_nt(q[:, sl], kn[:, sl]) - ftn[h:h + 1, :]
            s = jnp.where(visible, s, NEG_INF)
            _osm_update(s, vn[:, sl], m_ref, l_ref, acc_ref, h)
            outs.append((acc_ref[h] / l_ref[h]).astype(BF))
        o_ref[0] = jnp.concatenate(outs, axis=-1)


def _fox_sample_call(qkv, cache_k, cache_v, ft_pad, wa):
    b, t, _ = qkv.shape
    p_len = cache_k.shape[1]
    nh = wa // HEAD_DIM
    tk = min(CACHE_TILE_A, p_len)
    assert p_len % tk == 0 and p_len % LANES == 0 and t <= LANES
    return pl.pallas_call(
        functools.partial(_fox_sample_body, nh=nh),
        grid=(b, p_len // tk),
        in_specs=[pl.BlockSpec((1, t, wa), lambda b, j: (b, 0, 0)),
                  pl.BlockSpec((1, t, wa), lambda b, j: (b, 0, 1)),
                  pl.BlockSpec((1, t, wa), lambda b, j: (b, 0, 2)),
                  pl.BlockSpec((1, tk, wa), lambda b, j: (b, j, 0)),
                  pl.BlockSpec((1, tk, wa), lambda b, j: (b, j, 0)),
                  pl.BlockSpec((1, nh, tk), lambda b, j: (b, 0, j)),
                  pl.BlockSpec((1, nh, LANES), lambda b, j: (b, 0, p_len // LANES))],
        out_specs=pl.BlockSpec((1, t, wa), lambda b, j: (b, 0, 0)),
        out_shape=jax.ShapeDtypeStruct((b, t, wa), BF),
        scratch_shapes=[pltpu.VMEM((nh, t, 1), F32), pltpu.VMEM((nh, t, 1), F32),
                        pltpu.VMEM((nh, t, HEAD_DIM), F32)],
        compiler_params=_cparams(("arbitrary", "arbitrary")),
        name="fox_sample",
    )(qkv, qkv, qkv, cache_k, cache_v, ft_pad, ft_pad)


def _band_sample_body(q_ref, kn_ref, vn_ref, bk_ref, bv_ref, bias_b_ref, bias_n_ref, o_ref, *, nh):
    hd = HEAD_DIM
    q = q_ref[0]
    kn = kn_ref[0]
    vn = vn_ref[0]
    kb = bk_ref[0].astype(BF)
    vb = bv_ref[0].astype(BF)
    outs = []
    for h in range(nh):
        sl = slice(h * hd, (h + 1) * hd)
        sb = _dot_nt(q[:, sl], kb[:, sl]) + bias_b_ref[h]
        sn = _dot_nt(q[:, sl], kn[:, sl]) + bias_n_ref[h]
        m = jnp.maximum(jnp.max(sb, axis=-1, keepdims=True), jnp.max(sn, axis=-1, keepdims=True))
        pb = jnp.exp(sb - m)
        pn = jnp.exp(sn - m)
        l = jnp.sum(pb, axis=-1, keepdims=True) + jnp.sum(pn, axis=-1, keepdims=True)
        o = _dot(pb.astype(BF), vb[:, sl]) + _dot(pn.astype(BF), vn[:, sl])
        outs.append((o / l).astype(BF))
    o_ref[0] = jnp.concatenate(outs, axis=-1)


def _band_sample_call(qkv, buf_k, buf_v, rel_bias, wb):
    b, t, _ = qkv.shape
    lb = buf_k.shape[1]
    nh = wb // HEAD_DIM
    k_pos = jnp.concatenate([jnp.arange(-lb, 0), jnp.arange(t)])
    rel = jnp.arange(t)[:, None] - k_pos[None, :]
    bias = rel_bias[:, jnp.clip(rel, -MAX_REL, MAX_REL) + MAX_REL].astype(F32)
    bias_b, bias_n = bias[:, :, :lb], bias[:, :, lb:]
    return pl.pallas_call(
        functools.partial(_band_sample_body, nh=nh),
        grid=(b,),
        in_specs=[pl.BlockSpec((1, t, wb), lambda b: (b, 0, 3)),
                  pl.BlockSpec((1, t, wb), lambda b: (b, 0, 4)),
                  pl.BlockSpec((1, t, wb), lambda b: (b, 0, 5)),
                  pl.BlockSpec((1, lb, wb), lambda b: (b, 0, 0)),
                  pl.BlockSpec((1, lb, wb), lambda b: (b, 0, 0)),
                  pl.BlockSpec(bias_b.shape, lambda b: (0, 0, 0)),
                  pl.BlockSpec(bias_n.shape, lambda b: (0, 0, 0))],
        out_specs=pl.BlockSpec((1, t, wb), lambda b: (b, 0, 0)),
        out_shape=jax.ShapeDtypeStruct((b, t, wb), BF),
        compiler_params=_cparams(("arbitrary",)),
        name="band_sample",
    )(qkv, qkv, qkv, buf_k, buf_v, bias_b, bias_n)


def _diff_sample_body(q_ref, kn_ref, vn_ref, ck_ref, cv_ref, lam_ref, gsub_ref, o_ref, m_ref, l_ref, acc_ref,
                      *, nh, lam_init, p_len):
    j = pl.program_id(1)
    q = q_ref[0]
    t = q.shape[0]
    dv = 2 * HEAD_DIM
    lane = lax.broadcasted_iota(jnp.int32, (t, dv), 1)

    @pl.when(j == 0)
    def _():
        _osm_init(m_ref, l_ref, acc_ref)

    def attend(k_of, v_of, dist):
        dist2 = jnp.concatenate([dist, dist], axis=0)
        for h in range(nh):
            x = q[:, h * dv:(h + 1) * dv]
            zero = jnp.zeros_like(x)
            q2 = jnp.concatenate([jnp.where(lane < HEAD_DIM, x, zero), jnp.where(lane >= HEAD_DIM, x, zero)], axis=0)
            s = _dot_nt(q2, k_of(h)) - _alibi_slope(h, nh) * dist2
            _osm_update(s, v_of(h), m_ref, l_ref, acc_ref, h)

    tk = ck_ref.shape[1] // nh
    row = lax.broadcasted_iota(jnp.int32, (t, tk), 0)
    col = lax.broadcasted_iota(jnp.int32, (t, tk), 1)
    attend(lambda h: ck_ref[0, pl.ds(h, tk, stride=nh), :].astype(BF),
           lambda h: cv_ref[0, pl.ds(h, tk, stride=nh), :].astype(BF),
           (p_len + row - j * tk - col).astype(F32))

    @pl.when(j == pl.num_programs(1) - 1)
    def _():
        kn = kn_ref[0]
        vn = vn_ref[0]
        rown = lax.broadcasted_iota(jnp.int32, (t, t), 0)
        coln = lax.broadcasted_iota(jnp.int32, (t, t), 1)
        attend(lambda h: kn[:, h * dv:(h + 1) * dv], lambda h: vn[:, h * dv:(h + 1) * dv],
               jnp.abs(rown - coln).astype(F32))
        lam = _lambda(lam_ref, lam_init)
        g = gsub_ref[...]
        outs = []
        for h in range(nh):
            on = acc_ref[h] / l_ref[h]
            o = on[:t] - lam * on[t:]
            o = o * lax.rsqrt(jnp.mean(o * o, axis=-1, keepdims=True) + RMS_EPS)
            outs.append(((o * g) * (1.0 - lam_init)).astype(BF))
        o_ref[0] = jnp.concatenate(outs, axis=-1)


def _diff_sample_call(qkv, cache_k, cache_v, lam_par, g_sub, wc, lam_init):
    b, t, _ = qkv.shape
    nh = wc // (2 * HEAD_DIM)
    p_len = cache_k.shape[1] // nh
    assert (p_len // CHUNK) * CHUNK == p_len and t <= CHUNK
    tk = min(CACHE_TILE_C, p_len)
    return pl.pallas_call(
        functools.partial(_diff_sample_body, nh=nh, lam_init=lam_init, p_len=p_len),
        grid=(b, p_len // tk),
        in_specs=[pl.BlockSpec((1, t, wc), lambda b, j: (b, 0, 0)),
                  pl.BlockSpec((1, t, wc), lambda b, j: (b, 0, 1)),
                  pl.BlockSpec((1, t, wc), lambda b, j: (b, 0, 2)),
                  pl.BlockSpec((1, tk * nh, 2 * HEAD_DIM), lambda b, j: (b, j, 0)),
                  pl.BlockSpec((1, tk * nh, 2 * HEAD_DIM), lambda b, j: (b, j, 0)),
                  pl.BlockSpec(lam_par.shape, lambda b, j: (0, 0)),
                  pl.BlockSpec(g_sub.shape, lambda b, j: (0, 0))],
        out_specs=pl.BlockSpec((1, t, wc), lambda b, j: (b, 0, 0)),
        out_shape=jax.ShapeDtypeStruct((b, t, wc), BF),
        scratch_shapes=[pltpu.VMEM((nh, 2 * t, 1), F32), pltpu.VMEM((nh, 2 * t, 1), F32),
                        pltpu.VMEM((nh, 2 * t, 2 * HEAD_DIM), F32)],
        compiler_params=_cparams(("arbitrary", "arbitrary")),
        name="diff_sample",
    )(qkv, qkv, qkv, cache_k, cache_v, lam_par, g_sub)


def _out_ffn_body(x_ref, oa_ref, ob_ref, gm_ref, g_ref, sh_ref, sc_ref, gf_ref, wo_ref, w1_ref, w3_ref, w2_ref, y_ref,
                  *, f_chunk):
    wa = oa_ref.shape[-1]
    attn = _dot(oa_ref[0], wo_ref[:wa, :]) + _dot(ob_ref[0], wo_ref[wa:, :])
    x1 = x_ref[0] + gm_ref[0, 0, 0] * attn
    h = _modulate(x1, g_ref[...], sh_ref[0, 0, 0], sc_ref[0, 0, 0]).astype(BF)
    f_total = w1_ref.shape[1]
    f = None
    for c0 in range(0, f_total, f_chunk):
        a = _dot(h, w1_ref[:, c0:c0 + f_chunk])
        g = _dot(h, w3_ref[:, c0:c0 + f_chunk])
        u = ((a * _sigmoid(a)) * g).astype(BF)
        part = _dot(u, w2_ref[c0:c0 + f_chunk, :])
        f = part if f is None else f + part
    y_ref[0] = x1 + gf_ref[0, 0, 0] * f


def _out_ffn_call(x, o_a, o_b, gate_m, g, shift, scale, gate_f, wo, w1, w3, w2):
    bx, sx, d = x.shape
    tm = min(ROW_TILE, sx)
    f_total = w1.shape[1]
    f_chunk = f_total // 2 if (f_total // 2) % LANES == 0 else f_total
    row = lambda b, i: (b, i, 0)
    return pl.pallas_call(
        functools.partial(_out_ffn_body, f_chunk=f_chunk),
        grid=(bx, sx // tm),
        in_specs=[pl.BlockSpec((1, tm, d), row),
                  pl.BlockSpec((1, tm, o_a.shape[-1]), row), pl.BlockSpec((1, tm, o_b.shape[-1]), row),
                  _mod_spec(gate_m, tm), _const_spec(g), _mod_spec(shift, tm), _mod_spec(scale, tm),
                  _mod_spec(gate_f, tm), _const_spec(wo), _const_spec(w1), _const_spec(w3), _const_spec(w2)],
        out_specs=pl.BlockSpec((1, tm, d), row),
        out_shape=jax.ShapeDtypeStruct((bx, sx, d), F32),
        compiler_params=_cparams(("arbitrary", "arbitrary")),
        name="out_ffn",
    )(x, o_a, o_b, gate_m[0], g, shift[0], scale[0], gate_f[0], wo, w1, w3, w2)


def _store_tile_rows(ref, x):
    rows, d = x.shape
    nseg = d // LANES
    for s in range(nseg):
        ref[pl.ds(s, rows, stride=nseg), :] = x[:, s * LANES:(s + 1) * LANES]


def _load_tile_rows(ref, rows, d):
    nseg = d // LANES
    return jnp.concatenate([ref[pl.ds(s, rows, stride=nseg), :] for s in range(nseg)], axis=-1)


def _out_router_body(x_ref, o_ref, gm_ref, g_ref, sh_ref, sc_ref, wo_ref, wr_ref, h_in_ref,
                     x1_ref, h_ref, route_ref, *, n_exp):
    del h_in_ref
    x1 = x_ref[0] + gm_ref[0, 0, 0] * _dot(o_ref[0], wo_ref[...])
    x1_ref[0] = x1
    h = _modulate(x1, g_ref[...], sh_ref[0, 0, 0], sc_ref[0, 0, 0])
    hb = h.astype(BF)
    _store_tile_rows(h_ref, hb.astype(F32))
    terms = _dot(hb, wr_ref[...])
    logits = (terms + pltpu.roll(terms, LANES - ROUTE_ROWS, 1)) + pltpu.roll(terms, LANES - 2 * ROUTE_ROWS, 1)
    lt = logits.T[:ROUTE_ROWS, :]
    row = lax.broadcasted_iota(jnp.int32, lt.shape, 0)
    lg = jnp.where(row < n_exp, lt, NEG_INF)
    e = jnp.exp(lg - jnp.max(lg, axis=0, keepdims=True))
    probs = e / jnp.sum(e, axis=0, keepdims=True)
    p1 = jnp.max(probs, axis=0, keepdims=True)
    i1 = jnp.min(jnp.where(probs == p1, row, ROUTE_ROWS), axis=0, keepdims=True)
    rest = jnp.where(row == i1, -1.0, probs)
    p2 = jnp.max(rest, axis=0, keepdims=True)
    i2 = jnp.min(jnp.where(rest == p2, row, ROUTE_ROWS), axis=0, keepdims=True)
    tot = p1 + p2
    route_ref[0] = jnp.where(row == 0, i1.astype(F32),
                             jnp.where(row == 1, i2.astype(F32),
                                       jnp.where(row == 2, p1 / tot, jnp.where(row == 3, p2 / tot, 0.0))))


def _out_router_call(x, o, gate_m, g, shift, scale, wo, wr3, n_exp, tok_offset, h_all):
    bx, sx, d = x.shape
    tm = min(ROW_TILE, sx)
    nt = sx // tm
    nseg = d // LANES
    assert tok_offset % tm == 0 and n_exp <= ROUTE_ROWS
    row = lambda b, i: (b, i, 0)
    return pl.pallas_call(
        functools.partial(_out_router_body, n_exp=n_exp),
        grid=(bx, nt),
        in_specs=[pl.BlockSpec((1, tm, d), row), pl.BlockSpec((1, tm, o.shape[-1]), row),
                  _mod_spec(gate_m, tm), _const_spec(g), _mod_spec(shift, tm), _mod_spec(scale, tm),
                  _const_spec(wo), _const_spec(wr3), pl.BlockSpec(memory_space=pl.ANY)],
        out_specs=[pl.BlockSpec((1, tm, d), row),
                   pl.BlockSpec((tm * nseg, LANES), lambda b, i: (tok_offset // tm + b * nt + i, 0)),
                   pl.BlockSpec((1, ROUTE_ROWS, tm), lambda b, i: (b, 0, i))],
        out_shape=[jax.ShapeDtypeStruct((bx, sx, d), F32),
                   jax.ShapeDtypeStruct(h_all.shape, F32),
                   jax.ShapeDtypeStruct((bx, ROUTE_ROWS, sx), F32)],
        input_output_aliases={8: 1},
        compiler_params=_cparams(("arbitrary", "arbitrary")),
        name="out_router",
    )(x, o, gate_m[0], g, shift[0], scale[0], wo, wr3, h_all)


def _moe_ffn_body(te_ref, nv_ref, idx_ref, idx_next_ref, h_ref, w1_ref, w3_ref, w2_ref, y_ref, xbuf, acc, sem,
                  *, nseg, nf):
    t = pl.program_id(0)
    j = pl.program_id(1)
    nv = nv_ref[0]
    tm, d = acc.shape
    slot = lax.rem(t, 2)

    def issue(rows_ref, s):
        def body(r2, c):
            for u in range(2):
                r = 2 * r2 + u
                src = pl.multiple_of(rows_ref[0, 0, r], nseg)
                dst = pl.multiple_of(r * nseg, nseg)
                pltpu.make_async_copy(h_ref.at[pl.ds(src, nseg)], xbuf.at[s, pl.ds(dst, nseg)],
                                      sem.at[s]).start(priority=u)
            return c
        lax.fori_loop(0, tm // 2, body, 0, unroll=4)

    @pl.when((j == 0) & (t == 0))
    def _():
        issue(idx_ref, 0)

    @pl.when((j == 0) & (t + 1 < nv))
    def _():
        issue(idx_next_ref, 1 - slot)

    @pl.when((j == 0) & (t < nv))
    def _():
        pltpu.make_async_copy(h_ref.at[pl.ds(0, tm * nseg)], xbuf.at[slot], sem.at[slot]).wait()

    @pl.when(t < nv)
    def _():
        xb = _load_tile_rows(xbuf.at[slot], tm, d).astype(BF)
        a = _dot(xb, w1_ref[0])
        g = _dot(xb, w3_ref[0])
        u = ((a * _sigmoid(a)) * g).astype(BF)
        part = _dot(u, w2_ref[0])

        if nf == 1:
            _store_tile_rows(y_ref, part)
        else:
            @pl.when(j == 0)
            def _():
                acc[...] = part

            @pl.when((j > 0) & (j < nf - 1))
            def _():
                acc[...] += part

            @pl.when(j == nf - 1)
            def _():
                _store_tile_rows(y_ref, acc[...] + part)

    @pl.when((t >= nv) & (j == nf - 1))
    def _():
        y_ref[...] = jnp.zeros(y_ref.shape, F32)


def _moe_ffn_call(h_all, row_tok, tile_e, n_valid, w1, w3, w2):
    n_rows = row_tok.shape[0]
    d = w1.shape[1]
    nseg = d // LANES
    f_total = w1.shape[2]
    tf = MOE_F_TILE if f_total % MOE_F_TILE == 0 else f_total
    nf = f_total // tf
    tm = MOE_ROW_TILE
    n_tiles = n_rows // tm
    idx3 = (row_tok * nseg).reshape(n_tiles, 1, tm)

    def fcol(t, j, te, nv):
        return jnp.where(t < nv[0], j, nf - 1)

    grid_spec = pltpu.PrefetchScalarGridSpec(
        num_scalar_prefetch=2,
        grid=(n_tiles, nf),
        in_specs=[pl.BlockSpec((1, 1, tm), lambda t, j, te, nv: (t, 0, 0), memory_space=pltpu.SMEM),
                  pl.BlockSpec((1, 1, tm), lambda t, j, te, nv: (jnp.minimum(t + 1, n_tiles - 1), 0, 0),
                               memory_space=pltpu.SMEM),
                  pl.BlockSpec(memory_space=pl.ANY),
                  pl.BlockSpec((1, d, tf), lambda t, j, te, nv: (te[t], 0, fcol(t, j, te, nv))),
                  pl.BlockSpec((1, d, tf), lambda t, j, te, nv: (te[t], 0, fcol(t, j, te, nv))),
                  pl.BlockSpec((1, tf, d), lambda t, j, te, nv: (te[t], fcol(t, j, te, nv), 0))],
        out_specs=pl.BlockSpec((tm * nseg, LANES), lambda t, j, te, nv: (t, 0)),
        scratch_shapes=[pltpu.VMEM((2, tm * nseg, LANES), F32), pltpu.VMEM((tm, d), F32),
                        pltpu.SemaphoreType.DMA((2,))],
    )
    return pl.pallas_call(
        functools.partial(_moe_ffn_body, nseg=nseg, nf=nf),
        grid_spec=grid_spec,
        out_shape=jax.ShapeDtypeStruct((n_rows * nseg, LANES), F32),
        compiler_params=_cparams(("arbitrary", "arbitrary")),
        name="moe_ffn",
    )(tile_e, n_valid, idx3, idx3, h_all, w1, w3, w2)


def _combine_body(d0_ref, d1_ref, x_ref, gf_ref, w_ref, g_ref, yb_ref, o_ref, buf, sem):
    t = pl.program_id(0)
    nt = pl.num_programs(0)
    _, tc, d = x_ref.shape
    nseg = d // LANES

    def row_copy(idx_ref, r, slot, k):
        src = pl.multiple_of(idx_ref[0, 0, 2 * r + k], nseg)
        dst = pl.multiple_of(r * nseg, nseg)
        return pltpu.make_async_copy(yb_ref.at[pl.ds(src, nseg)], buf.at[slot, k, pl.ds(dst, nseg)], sem.at[slot])

    def issue(idx_ref, slot):
        def body(r, c):
            row_copy(idx_ref, r, slot, 0).start(priority=0)
            row_copy(idx_ref, r, slot, 1).start(priority=1)
            return c
        lax.fori_loop(0, tc, body, 0, unroll=4)

    slot = lax.rem(t, 2)

    @pl.when(t == 0)
    def _():
        issue(d0_ref, 0)

    @pl.when(t + 1 < nt)
    def _():
        issue(d1_ref, 1 - slot)

    for k in range(TOP_K):
        pltpu.make_async_copy(yb_ref.at[pl.ds(0, tc * nseg)], buf.at[slot, k], sem.at[slot]).wait()
    w = jnp.concatenate([w_ref[0], jnp.zeros((LANES - ROUTE_ROWS, tc), F32)], axis=0).T
    f = (w[:, 2:3] * _load_tile_rows(buf.at[slot, 0], tc, d)
         + w[:, 3:4] * _load_tile_rows(buf.at[slot, 1], tc, d))
    y = x_ref[0] + gf_ref[0, 0, 0] * f
    y = y * lax.rsqrt(jnp.mean(y * y, axis=-1, keepdims=True) + RMS_EPS)
    o_ref[0] = y * g_ref[...]


def _combine_call(x1, gate_f, route, g_final, yb, dest):
    bx, sx, d = x1.shape
    nseg = d // LANES
    tc = min(COMBINE_TILE, sx)
    nps = sx // tc
    nt = bx * nps
    dest3 = (dest * nseg).reshape(nt, 1, 2 * tc)
    row = lambda t: (t // nps, t % nps, 0)
    gate_arr, gl, gk = gate_f
    if gate_arr.shape[3] == 1:
        gate_spec = pl.BlockSpec((1, 1, 1, 1, d), lambda t: (gl, gk, t // nps, 0, 0))
    else:
        gate_spec = pl.BlockSpec((1, 1, 1, tc, d), lambda t: (gl, gk, t // nps, t % nps, 0))
    return pl.pallas_call(
        _combine_body,
        grid=(nt,),
        in_specs=[pl.BlockSpec((1, 1, 2 * tc), lambda t: (t, 0, 0), memory_space=pltpu.SMEM),
                  pl.BlockSpec((1, 1, 2 * tc), lambda t: (jnp.minimum(t + 1, nt - 1), 0, 0),
                               memory_space=pltpu.SMEM),
                  pl.BlockSpec((1, tc, d), row), gate_spec,
                  pl.BlockSpec((1, ROUTE_ROWS, tc), lambda t: (t // nps, 0, t % nps)),
                  pl.BlockSpec(g_final.shape, lambda t: (0, 0)),
                  pl.BlockSpec(memory_space=pl.ANY)],
        out_specs=pl.BlockSpec((1, tc, d), row),
        out_shape=jax.ShapeDtypeStruct((bx, sx, d), F32),
        scratch_shapes=[pltpu.VMEM((2, 2, tc * nseg, LANES), F32), pltpu.SemaphoreType.DMA((2,))],
        compiler_params=_cparams(("arbitrary",)),
        name="moe_combine",
    )(dest3, dest3, x1, gate_arr, route, g_final, yb)


def _route_plan(slot_e, n_exp, tm):
    n_slots = slot_e.shape[0]
    onehot = (slot_e[:, None] == jnp.arange(n_exp, dtype=jnp.int32)[None, :]).astype(jnp.int32)
    csum = jnp.cumsum(onehot, axis=0)
    rank = jnp.sum(csum * onehot, axis=1) - 1
    counts = csum[-1]
    padded = ((counts + tm - 1) // tm) * tm
    pad_end = jnp.cumsum(padded)
    pad_start = pad_end - padded
    dest = jnp.sum(onehot * pad_start[None, :], axis=1) + rank
    n_tiles = -(-n_slots // tm) + n_exp
    n_rows = n_tiles * tm
    n_valid = (pad_end[-1] // tm).astype(jnp.int32)
    tile_start = jnp.arange(n_tiles, dtype=jnp.int32) * tm
    tile_e = jnp.minimum(jnp.searchsorted(pad_end, tile_start, side='right'), n_exp - 1).astype(jnp.int32)
    last_e = tile_e[jnp.maximum(n_valid - 1, 0)]
    tile_e = jnp.where(jnp.arange(n_tiles) < n_valid, tile_e, last_e)
    sorted_tok = (jnp.argsort(slot_e, stable=True) // TOP_K).astype(jnp.int32)
    sorted_tok = jnp.concatenate([sorted_tok, jnp.zeros((tm,), jnp.int32)])
    first = jnp.cumsum(counts) - counts - pad_start
    tile_first = jnp.minimum(tile_start + first[tile_e], n_slots)
    row_tok = sorted_tok[(tile_first[:, None] + jnp.arange(tm, dtype=jnp.int32)[None, :]).reshape(n_rows)]
    return dest, row_tok, tile_e, n_valid.reshape(1)


def _moe(parts, h_all, g_final, w1, w3, w2):
    n_exp = w1.shape[0]
    slot_e = jnp.concatenate([r[:, :TOP_K, :].transpose(0, 2, 1).reshape(-1) for _, r, _ in parts], axis=0)
    dest, row_tok, tile_e, n_valid = _route_plan(slot_e.astype(jnp.int32), n_exp, MOE_ROW_TILE)
    yb = _moe_ffn_call(h_all, row_tok, tile_e, n_valid, w1, w3, w2)
    outs = []
    first = 0
    for x1, route, gate_f in parts:
        n_slots = x1.shape[0] * x1.shape[1] * TOP_K
        outs.append(_combine_call(x1, gate_f, route, g_final, yb, dest[first:first + n_slots]))
        first += n_slots
    return outs


def _trunk(x, mods, p, cache, tok_offset, h_all):
    sh_m0, sc_m0, gt_m0, sh_f0, sc_f0, gt_f0 = mods[0]
    sh_m1, sc_m1, gt_m1, sh_f1, sc_f1, gt_f1 = mods[1]
    wa = p['wa']
    wb = p['wb']
    wc = p['wc']
    bx, sx, d = x.shape

    if cache is None:
        keep = min(WINDOW_B, sx)
        qk, vt, ka, va, kb, vb, logf = _inproj_even_call(
            x, p['g_mix0'], sh_m0, sc_m0, p['w_in_even'], p['w_f'], p['b_f'], keep, True)
        o_a = _fox_prompt_call(qk, vt, wa)
        o_b = _band_prompt_call(qk, vt, p['rel_bias'], wa, wb)
        x_l0_state = (ka, va, logf, kb, vb)
    else:
        ck, cv, clogf, bk, bv, cck, ccv, b, t = cache
        qkv, ka, va, kb, vb, logf, lft, = _inproj_even_call(
            x, p['g_mix0'], sh_m0, sc_m0, p['w_in_even'], p['w_f'], p['b_f'], sx, False)
        nh = wa // HEAD_DIM
        p_len = ck.shape[1]
        lft_new = lft.reshape(nh, b, t).transpose(1, 0, 2)
        lp = -(-(p_len + t) // 512) * 512
        lcat = jnp.concatenate([clogf.transpose(0, 2, 1), lft_new,
                                jnp.zeros((b, nh, lp - p_len - t), F32)], axis=-1)
        ft_pad = _cumsum_call(lcat)
        qkv_b = qkv.reshape(b, t, qkv.shape[-1])
        o_a = _fox_sample_call(qkv_b, ck, cv, ft_pad, wa).reshape(1, b * t, wa)
        o_b = _band_sample_call(qkv_b, bk, bv, p['rel_bias'], wb).reshape(1, b * t, wb)
        x_l0_state = (ka, va, logf, kb, vb)
    x = _out_ffn_call(x, o_a, o_b, gt_m0, p['g_ffn0'], sh_f0, sc_f0, gt_f0,
                      p['w_out_even'], p['w1_dense'], p['w3_dense'], p['w2_dense'])

    if cache is None:
        qk, vt, kc, vc = _inproj_odd_call(x, p['g_mix1'], sh_m1, sc_m1, p['w_in_odd'], True)
        o = _diff_prompt_call(qk, vt, p['lam_par'], p['g_sub'].reshape(-1, 1), wc, p['lam_init'])
    else:
        qkv, kc, vc = _inproj_odd_call(x, p['g_mix1'], sh_m1, sc_m1, p['w_in_odd'], False)
        qkv_b = qkv.reshape(b, t, qkv.shape[-1])
        o = _diff_sample_call(qkv_b, cck, ccv, p['lam_par'], p['g_sub'], wc, p['lam_init']).reshape(1, b * t, wc)
    x1, h_all, route = _out_router_call(x, o, gt_m1, p['g_ffn1'], sh_f1, sc_f1, p['w_out_odd'], p['w_router3'],
                                        p['n_exp'], tok_offset, h_all)
    return (x1, route, gt_f1), h_all, x_l0_state, (kc, vc)


def kernel(x_prompt, x_sample, cache_a_k, cache_a_v, cache_a_logf, cache_b_k, cache_b_v, cache_c_k, cache_c_v,
           c_prompt, c_sample, w_mod, b_mod, g_mix, g_ffn, g_final, w_in_even, b_forget, rel_bias, w_out_even,
           w_in_odd, lambda_q1, lambda_k1, lambda_q2, lambda_k2, g_subln, w_out_odd, w1_dense, w3_dense, w2_dense,
           w_router, w1_moe, w3_moe, w2_moe):
    bp, sp, d = x_prompt.shape
    bs, ts, _ = x_sample.shape
    h_a = cache_a_k.shape[3]
    h_b = cache_b_k.shape[3]
    h_c = cache_c_k.shape[3]
    wa, wb, wc = h_a * HEAD_DIM, h_b * HEAD_DIM, h_c * 2 * HEAD_DIM
    n_exp = w_router.shape[-1]
    assert w_mod.shape[0] == 2, "kernel is written for the 2-layer trunk"

    we = w_in_even[0]
    w_main = jnp.concatenate([we[:, :3 * wa], we[:, 3 * wa + h_a:]], axis=1).astype(BF)
    w_f = jnp.pad(we[:, 3 * wa:3 * wa + h_a], ((0, 0), (0, LANES - h_a))).astype(BF)
    b_f = jnp.pad(b_forget[0], (0, LANES - h_a)).reshape(1, LANES).astype(F32)
    wr = jnp.pad(w_router[0], ((0, 0), (0, ROUTE_ROWS - n_exp)))
    wr_hi = wr.astype(BF)
    wr_r1 = wr - wr_hi.astype(F32)
    wr_mid = wr_r1.astype(BF)
    wr_lo = (wr_r1 - wr_mid.astype(F32)).astype(BF)
    w_router3 = jnp.pad(jnp.concatenate([wr_hi, wr_mid, wr_lo], axis=1), ((0, 0), (0, LANES - 3 * ROUTE_ROWS)))
    params = dict(
        wa=wa, wb=wb, wc=wc, n_exp=n_exp, lam_init=0.8 - 0.6 * math.exp(-0.3 * 1),
        g_mix0=g_mix[0:1], g_mix1=g_mix[1:2], g_ffn0=g_ffn[0:1], g_ffn1=g_ffn[1:2],
        g_final=g_final.reshape(1, d),
        w_in_even=w_main, w_f=w_f, b_f=b_f, rel_bias=rel_bias[0],
        w_out_even=w_out_even[0].astype(BF), w_in_odd=w_in_odd[0].astype(BF),
        lam_par=jnp.concatenate([lambda_q1, lambda_k1, lambda_q2, lambda_k2], axis=0),
        g_sub=g_subln[0:1], w_out_odd=w_out_odd[0].astype(BF),
        w1_dense=w1_dense[0].astype(BF), w3_dense=w3_dense[0].astype(BF), w2_dense=w2_dense[0].astype(BF),
        w_router3=w_router3,
        w1_moe=_cast_bf16_call(w1_moe[0]), w3_moe=_cast_bf16_call(w3_moe[0]), w2_moe=_cast_bf16_call(w2_moe[0]),
    )

    mod = _mod_call(jnp.concatenate([c_prompt, c_sample], axis=0), w_mod, b_mod)

    def mods_for(rows, per_row_len):
        m = mod[:, rows].reshape(2, -1, 6, d).transpose(0, 2, 1, 3)
        if per_row_len:
            m = jnp.repeat(m, per_row_len, axis=2)[:, :, None]
        else:
            m = m[:, :, :, None]
        return [[(m, l, k) for k in range(6)] for l in range(2)]

    h_all = jnp.zeros(((bp * sp + bs * ts) * (d // LANES), LANES), F32)
    moe_p, h_all, ev_p, od_p = _trunk(x_prompt, mods_for(slice(0, bp), 0), params, None, 0, h_all)
    cache = (cache_a_k[0].reshape(bs, -1, wa), cache_a_v[0].reshape(bs, -1, wa), cache_a_logf[0],
             cache_b_k[0].reshape(bs, -1, wb), cache_b_v[0].reshape(bs, -1, wb),
             cache_c_k[0].reshape(bs, -1, 2 * HEAD_DIM), cache_c_v[0].reshape(bs, -1, 2 * HEAD_DIM), bs, ts)
    moe_s, h_all, ev_s, od_s = _trunk(x_sample.reshape(1, bs * ts, d), mods_for(slice(bp, bp + bs), ts), params,
                                      cache, bp * sp, h_all)
    y_p, y_s = _moe([moe_p, moe_s], h_all, params['g_final'], params['w1_moe'], params['w3_moe'], params['w2_moe'])

    ka, va, logf, kb, vb = ev_p
    keep = kb.shape[1]
    out_p = (ka.reshape(1, bp, sp, h_a, HEAD_DIM), va.reshape(1, bp, sp, h_a, HEAD_DIM),
             logf.reshape(1, bp, sp, h_a),
             kb.reshape(1, bp, keep, h_b, HEAD_DIM), vb.reshape(1, bp, keep, h_b, HEAD_DIM),
             od_p[0].reshape(1, bp, sp, h_c, 2 * HEAD_DIM), od_p[1].reshape(1, bp, sp, h_c, 2 * HEAD_DIM))
    ka, va, logf, kb, vb = ev_s
    new_bk = jnp.concatenate([cache_b_k[0], kb.reshape(bs, ts, h_b, HEAD_DIM)], axis=1)[:, ts:]
    new_bv = jnp.concatenate([cache_b_v[0], vb.reshape(bs, ts, h_b, HEAD_DIM)], axis=1)[:, ts:]
    out_s = (ka.reshape(1, bs, ts, h_a, HEAD_DIM), va.reshape(1, bs, ts, h_a, HEAD_DIM),
             logf.reshape(1, bs, ts, h_a), new_bk[None], new_bv[None],
             od_s[0].reshape(1, bs, ts, h_c, 2 * HEAD_DIM), od_s[1].reshape(1, bs, ts, h_c, 2 * HEAD_DIM))
    return (y_p, y_s.reshape(bs, ts, d)) + out_p + out_s
```

```python
import functools
import math

import jax
import jax.numpy as jnp
import numpy as np
from jax import lax
from jax.experimental import pallas as pl
from jax.experimental.pallas import tpu as pltpu

BF = jnp.bfloat16
F32 = jnp.float32

CHUNK = 64
CHUNK_SHIFT = 6
HEAD_DIM = 64
BAND_CHUNKS = 8
WINDOW_B = BAND_CHUNKS * CHUNK
MAX_REL = 128
TOP_K = 2
RMS_EPS = 1e-6
NEG_INF = -1e30
QK_SCALE = HEAD_DIM ** -0.5
LOG2E = math.log2(math.e)
PROMPT_Q_SCALE = QK_SCALE * LOG2E


def _bf16_terms(c, n):
    terms = []
    for _ in range(n):
        t = float(np.asarray(c, dtype=jnp.bfloat16))
        terms.append(t)
        c -= t
    return terms


LOG2E_TERMS = _bf16_terms(LOG2E, 3)

LANES = 128
SUBLANES = 8
ONES_ROWS = 16
VMEM_LIMIT_BYTES = 56 * 1024 * 1024
ROUTE_ROWS = SUBLANES

ROW_TILE = 512
ATTN_TILE = 512
BAND_Q_TILE = WINDOW_B // 2
CACHE_TILE_A = 2048
CACHE_TILE_C = 1024
MOE_ROW_TILE = 512
MOE_F_TILE = 1792
COMBINE_TILE = 512
CAST_ROWS = 1024
MOD_COL_TILE = 1536
CUMSUM_TILE = 512


def _cparams(sem):
    return pltpu.CompilerParams(dimension_semantics=sem, vmem_limit_bytes=VMEM_LIMIT_BYTES)


def _dot(a, b):
    return jnp.dot(a, b, preferred_element_type=F32)


def _dot_nt(a, b):
    return lax.dot_general(a, b, (((1,), (1,)), ((), ())), preferred_element_type=F32)


def _sigmoid(x):
    return 1.0 / (1.0 + jnp.exp(-x))


def _modulate(x, g, shift, scale):
    y = x * lax.rsqrt(jnp.mean(x * x, axis=-1, keepdims=True) + RMS_EPS)
    return (y * g) * (1.0 + scale) + shift


def _split3(x):
    hi = x.astype(BF)
    r1 = x - hi.astype(F32)
    mid = r1.astype(BF)
    lo = (r1 - mid.astype(F32)).astype(BF)
    return hi, mid, lo


def _cumsum_lanes(x, upper):
    hi, mid, lo = _split3(x)
    return _dot(hi, upper) + _dot(mid, upper) + _dot(lo, upper)


def _upper_tri(t):
    r = jnp.arange(t)
    return (r[:, None] <= r[None, :]).astype(BF)


def _osm_update(s, v, m_ref, l_ref, acc_ref, idx):
    m_prev = m_ref[idx]
    m_new = jnp.maximum(m_prev, jnp.max(s, axis=-1, keepdims=True))
    alpha = jnp.exp(m_prev - m_new)
    p = jnp.exp(s - m_new)
    l_ref[idx] = alpha * l_ref[idx] + jnp.sum(p, axis=-1, keepdims=True)
    acc_ref[idx] = alpha * acc_ref[idx] + _dot(p.astype(BF), v)
    m_ref[idx] = m_new


def _osm_init(m_ref, l_ref, acc_ref):
    m_ref[...] = jnp.full(m_ref.shape, NEG_INF, F32)
    l_ref[...] = jnp.zeros(l_ref.shape, F32)
    acc_ref[...] = jnp.zeros(acc_ref.shape, F32)


def _cast_body(x_ref, o_ref):
    o_ref[...] = x_ref[...].astype(BF)


def _cast_bf16_call(w):
    e, k, n = w.shape
    tk = CAST_ROWS if k % CAST_ROWS == 0 else k
    spec = pl.BlockSpec((1, tk, n), lambda e, i: (e, i, 0))
    return pl.pallas_call(
        _cast_body, grid=(e, k // tk), in_specs=[spec], out_specs=spec,
        out_shape=jax.ShapeDtypeStruct(w.shape, BF),
        compiler_params=_cparams(("arbitrary", "arbitrary")),
        name="cast_bf16",
    )(w)


def _mod_body(c_ref, w_ref, b_ref, o_ref):
    c = c_ref[...]
    s = (c * _sigmoid(c)).astype(BF)
    o_ref[0] = _dot(s, w_ref[0].astype(BF)) + b_ref[0]


def _mod_call(c_all, w_mod, b_mod):
    depth, d, n = w_mod.shape
    r = c_all.shape[0]
    tn = MOD_COL_TILE if n % MOD_COL_TILE == 0 else n
    return pl.pallas_call(
        _mod_body,
        grid=(depth, n // tn),
        in_specs=[pl.BlockSpec((r, d), lambda l, j: (0, 0)),
                  pl.BlockSpec((1, d, tn), lambda l, j: (l, 0, j)),
                  pl.BlockSpec((1, 1, tn), lambda l, j: (l, 0, j))],
        out_specs=pl.BlockSpec((1, r, tn), lambda l, j: (l, 0, j)),
        out_shape=jax.ShapeDtypeStruct((depth, r, n), F32),
        compiler_params=_cparams(("arbitrary", "arbitrary")),
        name="mod",
    )(c_all, w_mod, b_mod.reshape(depth, 1, n))


def _mod_spec(m, tm):
    arr, l, k = m
    d = arr.shape[-1]
    if arr.shape[3] == 1:
        return pl.BlockSpec((1, 1, 1, 1, d), lambda b, i: (l, k, b, 0, 0))
    return pl.BlockSpec((1, 1, 1, tm, d), lambda b, i: (l, k, b, i, 0))


def _const_spec(arr):
    nd = arr.ndim
    return pl.BlockSpec(arr.shape, lambda b, i: (0,) * nd, pipeline_mode=pl.Buffered(1))


def _head_blocks(x, augs):
    tm, w = x.shape
    first = lax.broadcasted_iota(jnp.int32, (tm, LANES), 1) < HEAD_DIM
    out = []
    for p in range(w // LANES):
        pair = x[:, p * LANES:(p + 1) * LANES]
        out.append(jnp.where(first, pair, augs[2 * p]).astype(BF))
        out.append(jnp.where(first, pltpu.roll(pair, HEAD_DIM, 1), augs[2 * p + 1]).astype(BF))
    return out


def _lane_range_row(lo, hi, value):
    lane = lax.broadcasted_iota(jnp.int32, (1, LANES), 1)
    return jnp.where((lane >= lo) & (lane < hi), value, 0.0).astype(F32)


def _ones_rows(t):
    return jnp.where(lax.broadcasted_iota(jnp.int32, (ONES_ROWS, t), 0) == 0, 1.0, 0.0).astype(BF)


def _log_forget(h, wf_ref, bf_ref):
    z = _dot(h, wf_ref[...]) + bf_ref[...]
    return jnp.minimum(z, 0.0) - jnp.log1p(jnp.exp(-jnp.abs(z)))


def _inproj_even_body(*refs, wa, wb, prompt):
    if prompt:
        (x_ref, g_ref, sh_ref, sc_ref, w_ref, wf_ref, bf_ref, low_ref, place_ref,
         qk_ref, vt_ref, ka_ref, va_ref, kb_ref, vb_ref, logf_ref, carry_ref) = refs
    else:
        (x_ref, g_ref, sh_ref, sc_ref, w_ref, wf_ref, bf_ref,
         qkv_ref, ka_ref, va_ref, kb_ref, vb_ref, logf_ref, lft_ref) = refs
    i = pl.program_id(1)
    h = _modulate(x_ref[0], g_ref[...], sh_ref[0, 0, 0], sc_ref[0, 0, 0]).astype(BF)
    tm = h.shape[0]
    nh = logf_ref.shape[-1]

    qa = _dot(h, w_ref[:, 0:wa])
    ka = _dot(h, w_ref[:, wa:2 * wa])
    va = _dot(h, w_ref[:, 2 * wa:3 * wa])
    o = 3 * wa
    qb = _dot(h, w_ref[:, o:o + wb])
    kb = _dot(h, w_ref[:, o + wb:o + 2 * wb])
    vb = _dot(h, w_ref[:, o + 2 * wb:o + 3 * wb])
    ka_ref[0] = ka
    va_ref[0] = va
    kb_ref[0] = kb
    vb_ref[0] = vb
    logf = _log_forget(h, wf_ref, bf_ref)
    logf_ref[0] = logf[:, :nh]

    if not prompt:
        qkv_ref[0] = jnp.concatenate(
            [(qa * QK_SCALE).astype(BF), ka.astype(BF), va.astype(BF),
             (qb * QK_SCALE).astype(BF), kb.astype(BF), vb.astype(BF)], axis=-1)
        lft_ref[0] = logf.T[:nh, :]
        return

    @pl.when(i == 0)
    def _():
        carry_ref[...] = jnp.zeros(carry_ref.shape, F32)

    terms = _dot(low_ref[...], jnp.concatenate(_split3(logf), axis=1))
    f_cum = terms[:, :LANES] + terms[:, LANES:2 * LANES] + terms[:, 2 * LANES:] + carry_ref[0:1, :]
    carry_ref[...] = jnp.broadcast_to(f_cum[tm - 1:tm, :], carry_ref.shape)
    k_aug = _dot(jnp.concatenate(_split3(-f_cum), axis=1), place_ref[...])
    q_log2e = sum(_lane_range_row(HEAD_DIM + 3 * r, HEAD_DIM + 3 * r + 3, c) for r, c in enumerate(LOG2E_TERMS))
    zero = jnp.zeros((1, LANES), F32)
    nhb = wb // HEAD_DIM
    blocks = (_head_blocks(qa * PROMPT_Q_SCALE, [q_log2e] * nh)
              + _head_blocks(ka, [k_aug[:, n * LANES:(n + 1) * LANES] for n in range(nh)])
              + _head_blocks(qb * PROMPT_Q_SCALE, [zero] * nhb)
              + _head_blocks(kb, [zero] * nhb))
    qk_ref[0] = jnp.concatenate(blocks, axis=-1)
    ones = _ones_rows(tm)
    vt_ref[0] = jnp.concatenate([va.T.astype(BF), ones, vb.T.astype(BF), ones], axis=0)


def _lower_tri(t):
    r = jnp.arange(t)
    return (r[:, None] >= r[None, :]).astype(BF)


def _place_matrices(nh):
    s = jnp.arange(3)[:, None, None]
    r = jnp.arange(LANES)[None, :, None]
    c = jnp.arange(nh * LANES)[None, None, :]
    off = c - r * LANES - HEAD_DIM - s
    return ((r < nh) & ((off == 0) | (off == 3) | (off == 6))).astype(BF).reshape(3 * LANES, nh * LANES)


def _inproj_even_call(x, g, shift, scale, w_main, w_f, b_f, keep, prompt):
    bx, sx, d = x.shape
    tm = min(ROW_TILE, sx)
    nt = sx // tm
    n = w_main.shape[1]
    wa = wb = n // 6
    nh = wa // HEAD_DIM
    nkeep = keep // tm
    tail = lambda b, i: (b, jnp.maximum(i - (nt - nkeep), 0), 0)
    row = lambda b, i: (b, i, 0)
    in_specs = [pl.BlockSpec((1, tm, d), row), _const_spec(g), _mod_spec(shift, tm), _mod_spec(scale, tm),
                _const_spec(w_main), _const_spec(w_f), _const_spec(b_f)]
    args = [x, g, shift[0], scale[0], w_main, w_f, b_f]
    f32_specs = [pl.BlockSpec((1, tm, wa), row), pl.BlockSpec((1, tm, wa), row),
                 pl.BlockSpec((1, tm, wb), tail), pl.BlockSpec((1, tm, wb), tail),
                 pl.BlockSpec((1, tm, nh), row)]
    f32_shapes = [jax.ShapeDtypeStruct((bx, sx, wa), F32), jax.ShapeDtypeStruct((bx, sx, wa), F32),
                  jax.ShapeDtypeStruct((bx, keep, wb), F32), jax.ShapeDtypeStruct((bx, keep, wb), F32),
                  jax.ShapeDtypeStruct((bx, sx, nh), F32)]
    scratch = []
    if prompt:
        low, place = _lower_tri(tm), _place_matrices(nh)
        in_specs += [_const_spec(low), _const_spec(place)]
        args += [low, place]
        nqk = 2 * (wa + wb) // HEAD_DIM * LANES
        nvt = wa + wb + 2 * ONES_ROWS
        out_specs = [pl.BlockSpec((1, tm, nqk), row), pl.BlockSpec((1, nvt, tm), lambda b, i: (b, 0, i))] + f32_specs
        out_shape = [jax.ShapeDtypeStruct((bx, sx, nqk), BF), jax.ShapeDtypeStruct((bx, nvt, sx), BF)] + f32_shapes
        scratch.append(pltpu.VMEM((8, LANES), F32))
    else:
        out_specs = [pl.BlockSpec((1, tm, n), row)] + f32_specs + [pl.BlockSpec((1, nh, tm), lambda b, i: (b, 0, i))]
        out_shape = [jax.ShapeDtypeStruct((bx, sx, n), BF)] + f32_shapes + [jax.ShapeDtypeStruct((bx, nh, sx), F32)]
    return pl.pallas_call(
        functools.partial(_inproj_even_body, wa=wa, wb=wb, prompt=prompt),
        grid=(bx, nt), in_specs=in_specs, out_specs=out_specs, out_shape=out_shape,
        scratch_shapes=scratch,
        compiler_params=_cparams(("arbitrary", "arbitrary")),
        name="inproj_even",
    )(*args)


def _inproj_odd_body(*refs, wc, prompt):
    if prompt:
        x_ref, g_ref, sh_ref, sc_ref, w_ref, qk_ref, vt_ref, k_ref, v_ref = refs
    else:
        x_ref, g_ref, sh_ref, sc_ref, w_ref, qkv_ref, k_ref, v_ref = refs
    i = pl.program_id(1)
    h = _modulate(x_ref[0], g_ref[...], sh_ref[0, 0, 0], sc_ref[0, 0, 0]).astype(BF)
    tm = h.shape[0]
    q = _dot(h, w_ref[:, 0:wc])
    k = _dot(h, w_ref[:, wc:2 * wc])
    v = _dot(h, w_ref[:, 2 * wc:3 * wc])
    k_ref[0] = k
    v_ref[0] = v
    if not prompt:
        qkv_ref[0] = jnp.concatenate([(q * QK_SCALE).astype(BF), k.astype(BF), v.astype(BF)], axis=-1)
        return
    nh = wc // LANES
    pos = (i * tm + lax.broadcasted_iota(jnp.int32, (tm, LANES), 0)).astype(F32)
    pos_hi = pos.astype(BF).astype(F32)
    lane = lax.broadcasted_iota(jnp.int32, (tm, LANES), 1) - HEAD_DIM
    in_aug = (lane >= 0) & (lane < 2 * len(LOG2E_TERMS))
    k_aug = jnp.where(in_aug, jnp.where((lane & 1) == 0, pos_hi, pos - pos_hi), 0.0)
    q_augs = []
    for hh in range(nh):
        slope = _alibi_slope(hh, nh)
        row = sum(_lane_range_row(HEAD_DIM + 2 * r, HEAD_DIM + 2 * r + 2, slope * c) for r, c in enumerate(LOG2E_TERMS))
        q_augs += [row] * 2
    blocks = _head_blocks(q * PROMPT_Q_SCALE, q_augs) + _head_blocks(k, [k_aug] * (2 * nh))
    qk_ref[0] = jnp.concatenate(blocks, axis=-1)
    vt_ref[0] = jnp.concatenate([v.T.astype(BF), _ones_rows(tm)], axis=0)


def _alibi_slope(h, nh):
    assert 8 % nh == 0, "slopes must be exact powers of two to ride in bf16"
    return 2.0 ** (-8.0 * (h + 1) / nh)


def _inproj_odd_call(x, g, shift, scale, w, prompt):
    bx, sx, d = x.shape
    tm = min(ROW_TILE, sx)
    n = w.shape[1]
    wc = n // 3
    row = lambda b, i: (b, i, 0)
    f32_specs = [pl.BlockSpec((1, tm, wc), row), pl.BlockSpec((1, tm, wc), row)]
    f32_shapes = [jax.ShapeDtypeStruct((bx, sx, wc), F32), jax.ShapeDtypeStruct((bx, sx, wc), F32)]
    if prompt:
        nqk = 2 * wc // HEAD_DIM * LANES
        out_specs = [pl.BlockSpec((1, tm, nqk), row), pl.BlockSpec((1, wc + ONES_ROWS, tm), lambda b, i: (b, 0, i))]
        out_shape = [jax.ShapeDtypeStruct((bx, sx, nqk), BF), jax.ShapeDtypeStruct((bx, wc + ONES_ROWS, sx), BF)]
    else:
        out_specs = [pl.BlockSpec((1, tm, n), row)]
        out_shape = [jax.ShapeDtypeStruct((bx, sx, n), BF)]
    return pl.pallas_call(
        functools.partial(_inproj_odd_body, wc=wc, prompt=prompt),
        grid=(bx, sx // tm),
        in_specs=[pl.BlockSpec((1, tm, d), row), _const_spec(g), _mod_spec(shift, tm), _mod_spec(scale, tm),
                  _const_spec(w)],
        out_specs=out_specs + f32_specs, out_shape=out_shape + f32_shapes,
        compiler_params=_cparams(("arbitrary", "arbitrary")),
        name="inproj_odd",
    )(x, g, shift[0], scale[0], w)


def _tflash_update(st, v_aug, m_ref, acc_ref, idx):
    m_prev = m_ref[idx]
    m_new = jnp.maximum(m_prev, jnp.max(st, axis=0, keepdims=True))
    alpha = jnp.exp2(m_prev - m_new)
    p = jnp.exp2(st - m_new).astype(BF)
    acc_ref[idx] = alpha * acc_ref[idx] + _dot(v_aug, p)
    m_ref[idx] = m_new


def _diag_scores(st_ref, slot, k_blk, q_blk):
    hf = k_blk.shape[0] // 2
    st_ref[slot, :hf, :] = _dot_nt(k_blk[:hf], q_blk)
    st_ref[slot, hf:, hf:] = _dot_nt(k_blk[hf:], q_blk[hf:])


def _tflash_diag_update(st_ref, slot, v_aug, fix_lo, fix_hi, m_ref, acc_ref, idx):
    hf = st_ref.shape[1] // 2
    lo = fix_lo(st_ref[slot, :hf, :])
    hi = fix_hi(st_ref[slot, hf:, hf:])
    m_prev = m_ref[idx]
    m_lo = jnp.maximum(m_prev, jnp.max(lo, axis=0, keepdims=True))
    m_new = jnp.concatenate([m_lo[:, :hf], jnp.maximum(m_lo[:, hf:], jnp.max(hi, axis=0, keepdims=True))], axis=1)
    alpha = jnp.exp2(m_prev - m_new)
    p_lo = jnp.exp2(lo - m_new).astype(BF)
    p_hi = jnp.exp2(hi - m_new[:, hf:]).astype(BF)
    upd = _dot(v_aug[:, :hf], p_lo)
    upd = jnp.concatenate([upd[:, :hf], upd[:, hf:] + _dot(v_aug[:, hf:], p_hi)], axis=1)
    acc_ref[idx] = alpha * acc_ref[idx] + upd
    m_ref[idx] = m_new


def _tflash_init(m_ref, acc_ref):
    m_ref[...] = jnp.full(m_ref.shape, NEG_INF, F32)
    acc_ref[...] = jnp.zeros(acc_ref.shape, F32)


def _causal_pairs(nq):
    pairs = [(i, j) for i in range(nq) for j in range(i + 1)]
    return jnp.asarray([p[0] for p in pairs], jnp.int32), jnp.asarray([p[1] for p in pairs], jnp.int32)


def _fox_prompt_body(it_ref, jt_ref, q_ref, k_ref, vt_ref, o_ref, m_ref, acc_ref, st_ref, *, nh):
    i = it_ref[pl.program_id(1)]
    j = jt_ref[pl.program_id(1)]
    hd = HEAD_DIM

    @pl.when(j == 0)
    def _():
        _tflash_init(m_ref, acc_ref)

    def step(diag):
        q = q_ref[0]
        k = k_ref[0]
        vt = vt_ref[0]
        t = q.shape[0]
        ones = vt[nh * hd:, :]
        if diag:
            causal = lambda s: jnp.where(lax.broadcasted_iota(jnp.int32, s.shape, 0)
                                         <= lax.broadcasted_iota(jnp.int32, s.shape, 1), s, NEG_INF)

        def scores(h):
            k_blk, q_blk = k[:, h * LANES:(h + 1) * LANES], q[:, h * LANES:(h + 1) * LANES]
            if diag:
                _diag_scores(st_ref, h % 2, k_blk, q_blk)
            else:
                st_ref[h % 2] = _dot_nt(k_blk, q_blk)

        scores(0)
        for h in range(nh):
            if h + 1 < nh:
                scores(h + 1)
            v_aug = jnp.concatenate([vt[h * hd:(h + 1) * hd, :], ones], axis=0)
            if diag:
                _tflash_diag_update(st_ref, h % 2, v_aug, causal, causal, m_ref, acc_ref, h)
            else:
                _tflash_update(st_ref[h % 2], v_aug, m_ref, acc_ref, h)

    @pl.when(j < i)
    def _():
        step(False)

    @pl.when(j == i)
    def _():
        step(True)
        ot = jnp.concatenate([acc_ref[h][:hd] / acc_ref[h][hd:hd + 1] for h in range(nh)], axis=0)
        o_ref[0] = ot.T.astype(BF)


def _fox_prompt_call(qk, vt, wa):
    b, s, _ = qk.shape
    nh = wa // HEAD_DIM
    t = min(ATTN_TILE, s)
    it, jt = _causal_pairs(s // t)
    wq = nh * LANES
    grid_spec = pltpu.PrefetchScalarGridSpec(
        num_scalar_prefetch=2,
        grid=(b, it.shape[0]),
        in_specs=[pl.BlockSpec((1, t, wq), lambda b, s, it, jt: (b, it[s], 0)),
                  pl.BlockSpec((1, t, wq), lambda b, s, it, jt: (b, jt[s], 1)),
                  pl.BlockSpec((1, wa + ONES_ROWS, t), lambda b, s, it, jt: (b, 0, jt[s]))],
        out_specs=pl.BlockSpec((1, t, wa), lambda b, s, it, jt: (b, it[s], 0)),
        scratch_shapes=[pltpu.VMEM((nh, 1, t), F32), pltpu.VMEM((nh, HEAD_DIM + ONES_ROWS, t), F32),
                        pltpu.VMEM((2, t, t), F32)],
    )
    return pl.pallas_call(
        functools.partial(_fox_prompt_body, nh=nh),
        grid_spec=grid_spec,
        out_shape=jax.ShapeDtypeStruct((b, s, wa), BF),
        compiler_params=_cparams(("arbitrary", "arbitrary")),
        name="fox_prompt",
    )(it, jt, qk, qk, vt)


def _band_prompt_body(q_ref, k0_ref, k1_ref, k2_ref, v0_ref, v1_ref, v2_ref, bias_ref, o_ref, st_ref, *, nh):
    i = pl.program_id(1)
    hd = HEAD_DIM
    q = q_ref[0]
    tq = q.shape[0]
    k = jnp.concatenate([k0_ref[0], k1_ref[0], k2_ref[0]], axis=0)
    vts = [v0_ref[0], v1_ref[0], v2_ref[0]]

    def scores(h):
        st_ref[h % 2] = _dot_nt(k[:, h * LANES:(h + 1) * LANES], q[:, h * LANES:(h + 1) * LANES])

    def attend(at_start):
        if at_start:
            in_seq = lax.broadcasted_iota(jnp.int32, (3 * tq, tq), 0) >= (2 - i) * tq
        outs = []
        scores(0)
        for h in range(nh):
            if h + 1 < nh:
                scores(h + 1)
            st = st_ref[h % 2] + bias_ref[h]
            if at_start:
                st = jnp.where(in_seq, st, NEG_INF)
            p = jnp.exp2(st - jnp.max(st, axis=0, keepdims=True)).astype(BF)
            acc = None
            for w, vt in enumerate(vts):
                v_aug = jnp.concatenate([vt[h * hd:(h + 1) * hd, :], vt[nh * hd:, :]], axis=0)
                part = _dot(v_aug, p[w * tq:(w + 1) * tq, :])
                acc = part if acc is None else acc + part
            outs.append(acc[:hd] / acc[hd:hd + 1])
        o_ref[0] = jnp.concatenate(outs, axis=0).T.astype(BF)

    @pl.when(i < 2)
    def _():
        attend(True)

    @pl.when(i >= 2)
    def _():
        attend(False)


def _band_bias_prompt(rel_bias, tq):
    nh = rel_bias.shape[0]
    period = 4 * tq
    t = jnp.arange(period)
    t = jnp.where(t < 3 * tq, t, t - period)
    u = rel_bias[:, jnp.clip(2 * tq - t, -MAX_REL, MAX_REL) + MAX_REL].astype(F32)
    skew = jnp.tile(u, (1, tq))[:, :tq * (period - 1)].reshape(nh, tq, period - 1)
    bias = skew[:, :, :3 * tq]
    r = jnp.arange(tq)[:, None]
    c = jnp.arange(3 * tq)[None, :]
    valid = (c // CHUNK >= r // CHUNK) & (c // CHUNK <= r // CHUNK + BAND_CHUNKS)
    return jnp.where(valid[None], bias, NEG_INF)


def _band_prompt_call(qk, vt, rel_bias, wa, wb):
    b, s, _ = qk.shape
    nh = wb // HEAD_DIM
    tq = BAND_Q_TILE
    assert s % tq == 0 and wa == wb
    bias_t = (_band_bias_prompt(rel_bias, tq) * LOG2E).transpose(0, 2, 1)
    wq = nh * LANES
    cq, ck = 2, 3

    def k_spec(back):
        return pl.BlockSpec((1, tq, wq), lambda b, i: (b, jnp.maximum(i - back, 0), ck))

    def v_spec(back):
        return pl.BlockSpec((1, wb + ONES_ROWS, tq), lambda b, i: (b, 1, jnp.maximum(i - back, 0)))

    return pl.pallas_call(
        functools.partial(_band_prompt_body, nh=nh),
        grid=(b, s // tq),
        in_specs=[pl.BlockSpec((1, tq, wq), lambda b, i: (b, i, cq)),
                  k_spec(2), k_spec(1), k_spec(0), v_spec(2), v_spec(1), v_spec(0),
                  _const_spec(bias_t)],
        out_specs=pl.BlockSpec((1, tq, wb), lambda b, i: (b, i, 0)),
        out_shape=jax.ShapeDtypeStruct((b, s, wb), BF),
        scratch_shapes=[pltpu.VMEM((2, 3 * tq, tq), F32)],
        compiler_params=_cparams(("arbitrary", "arbitrary")),
        name="band_prompt",
    )(qk, qk, qk, qk, vt, vt, vt, bias_t)


def _lambda(lam_ref, lam_init):
    lp = lam_ref[...]
    a = jnp.sum(lp[0:1] * lp[1:2], axis=-1, keepdims=True)
    b = jnp.sum(lp[2:3] * lp[3:4], axis=-1, keepdims=True)
    return jnp.exp(a) - jnp.exp(b) + lam_init


def _diff_prompt_body(it_ref, jt_ref, q_ref, k_ref, vt_ref, lam_ref, gcol_ref, o_ref, m_ref, acc_ref, st_ref,
                      *, nh, lam_init):
    i = it_ref[pl.program_id(1)]
    j = jt_ref[pl.program_id(1)]
    dv = 2 * HEAD_DIM

    @pl.when(j == 0)
    def _():
        _tflash_init(m_ref, acc_ref)

    def step(diag):
        q = q_ref[0]
        k = k_ref[0]
        vt = vt_ref[0]
        t = q.shape[0]
        ones = vt[nh * dv:, :]
        if diag:
            def penalty(shape):
                key = lax.broadcasted_iota(jnp.int32, shape, 0)
                qry = lax.broadcasted_iota(jnp.int32, shape, 1)
                visible = (key >> CHUNK_SHIFT) <= (qry >> CHUNK_SHIFT)
                return jnp.where(visible, 2.0 * jnp.maximum(key - qry, 0).astype(F32), -NEG_INF)
            pen_lo, pen_hi = penalty((t // 2, t)), penalty((t // 2, t // 2))

        def scores(n):
            k_blk, q_blk = k[:, n * LANES:(n + 1) * LANES], q[:, n * LANES:(n + 1) * LANES]
            if diag:
                _diag_scores(st_ref, n % 2, k_blk, q_blk)
            else:
                st_ref[n % 2] = _dot_nt(k_blk, q_blk)

        scores(0)
        for h in range(nh):
            v_aug = jnp.concatenate([vt[h * dv:(h + 1) * dv, :], ones], axis=0)
            if diag:
                c = LOG2E * _alibi_slope(h, nh)
                ex_lo, ex_hi = c * pen_lo, c * pen_hi
            for u in range(2):
                n = 2 * h + u
                if n + 1 < 2 * nh:
                    scores(n + 1)
                if diag:
                    _tflash_diag_update(st_ref, n % 2, v_aug, lambda s: s - ex_lo, lambda s: s - ex_hi,
                                        m_ref, acc_ref, n)
                else:
                    _tflash_update(st_ref[n % 2], v_aug, m_ref, acc_ref, n)

    @pl.when(j < i)
    def _():
        step(False)

    @pl.when(j == i)
    def _():
        step(True)
        lam = _lambda(lam_ref, lam_init)
        g = gcol_ref[...]
        outs = []
        for h in range(nh):
            a1 = acc_ref[2 * h]
            a2 = acc_ref[2 * h + 1]
            o = a1[:dv] / a1[dv:dv + 1] - lam * (a2[:dv] / a2[dv:dv + 1])
            o = o * lax.rsqrt(jnp.mean(o * o, axis=0, keepdims=True) + RMS_EPS)
            outs.append((o * g) * (1.0 - lam_init))
        o_ref[0] = jnp.concatenate(outs, axis=0).T.astype(BF)


def _diff_prompt_call(qk, vt, lam_par, g_col, wc, lam_init):
    b, s, _ = qk.shape
    nh = wc // (2 * HEAD_DIM)
    t = min(ATTN_TILE, s)
    it, jt = _causal_pairs(s // t)
    wq = 2 * nh * LANES
    grid_spec = pltpu.PrefetchScalarGridSpec(
        num_scalar_prefetch=2,
        grid=(b, it.shape[0]),
        in_specs=[pl.BlockSpec((1, t, wq), lambda b, s, it, jt: (b, it[s], 0)),
                  pl.BlockSpec((1, t, wq), lambda b, s, it, jt: (b, jt[s], 1)),
                  pl.BlockSpec((1, wc + ONES_ROWS, t), lambda b, s, it, jt: (b, 0, jt[s])),
                  pl.BlockSpec(lam_par.shape, lambda b, s, it, jt: (0, 0)),
                  pl.BlockSpec(g_col.shape, lambda b, s, it, jt: (0, 0))],
        out_specs=pl.BlockSpec((1, t, wc), lambda b, s, it, jt: (b, it[s], 0)),
        scratch_shapes=[pltpu.VMEM((2 * nh, 1, t), F32), pltpu.VMEM((2 * nh, 2 * HEAD_DIM + ONES_ROWS, t), F32),
                        pltpu.VMEM((2, t, t), F32)],
    )
    return pl.pallas_call(
        functools.partial(_diff_prompt_body, nh=nh, lam_init=lam_init),
        grid_spec=grid_spec,
        out_shape=jax.ShapeDtypeStruct((b, s, wc), BF),
        compiler_params=_cparams(("arbitrary", "arbitrary")),
        name="diff_prompt",
    )(it, jt, qk, qk, vt, lam_par, g_col)


def _cumsum_body(x_ref, up_ref, o_ref, carry_ref):
    @pl.when(pl.program_id(1) == 0)
    def _():
        carry_ref[...] = jnp.zeros(carry_ref.shape, F32)
    ft = _cumsum_lanes(x_ref[0], up_ref[...]) + carry_ref[:, 0:1]
    o_ref[0] = ft
    carry_ref[...] = jnp.broadcast_to(ft[:, -1:], carry_ref.shape)


def _cumsum_call(x):
    b, nh, length = x.shape
    t = CUMSUM_TILE
    up = _upper_tri(t)
    return pl.pallas_call(
        _cumsum_body,
        grid=(b, length // t),
        in_specs=[pl.BlockSpec((1, nh, t), lambda b, j: (b, 0, j)), pl.BlockSpec((t, t), lambda b, j: (0, 0))],
        out_specs=pl.BlockSpec((1, nh, t), lambda b, j: (b, 0, j)),
        out_shape=jax.ShapeDtypeStruct(x.shape, F32),
        scratch_shapes=[pltpu.VMEM((nh, LANES), F32)],
        compiler_params=_cparams(("arbitrary", "arbitrary")),
        name="logf_cumsum",
    )(x, up)


def _fox_sample_body(q_ref, kn_ref, vn_ref, ck_ref, cv_ref, ftc_ref, ftn_ref, o_ref, m_ref, l_ref, acc_ref, *, nh):
    j = pl.program_id(1)
    hd = HEAD_DIM
    q = q_ref[0]
    t = q.shape[0]

    @pl.when(j == 0)
    def _():
        _osm_init(m_ref, l_ref, acc_ref)

    k = ck_ref[0].astype(BF)
    v = cv_ref[0].astype(BF)
    ft = ftc_ref[0]
    for h in range(nh):
        sl = slice(h * hd, (h + 1) * hd)
        s = _dot_nt(q[:, sl], k[:, sl]) - ft[h:h + 1, :]
        _osm_update(s, v[:, sl], m_ref, l_ref, acc_ref, h)

    @pl.when(j == pl.num_programs(1) - 1)
    def _():
        kn = kn_ref[0]
        vn = vn_ref[0]
        ftn = ftn_ref[0][:, :t]
        visible = lax.broadcasted_iota(jnp.int32, (t, t), 1) <= lax.broadcasted_iota(jnp.int32, (t, t), 0)
        outs = []
        for h in range(nh):
            sl = slice(h * hd, (h + 1) * hd)
            s = _dot_nt(q[:, sl], kn[:, sl]) - ftn[h:h + 1, :]
            s = jnp.where(visible, s, NEG_INF)
            _osm_update(s, vn[:, sl], m_ref, l_ref, acc_ref, h)
            outs.append((acc_ref[h] / l_ref[h]).astype(BF))
        o_ref[0] = jnp.concatenate(outs, axis=-1)


def _fox_sample_call(qkv, cache_k, cache_v, ft_pad, wa):
    b, t, _ = qkv.shape
    p_len = cache_k.shape[1]
    nh = wa // HEAD_DIM
    tk = min(CACHE_TILE_A, p_len)
    assert p_len % tk == 0 and p_len % LANES == 0 and t <= LANES
    return pl.pallas_call(
        functools.partial(_fox_sample_body, nh=nh),
        grid=(b, p_len // tk),
        in_specs=[pl.BlockSpec((1, t, wa), lambda b, j: (b, 0, 0)),
                  pl.BlockSpec((1, t, wa), lambda b, j: (b, 0, 1)),
                  pl.BlockSpec((1, t, wa), lambda b, j: (b, 0, 2)),
                  pl.BlockSpec((1, tk, wa), lambda b, j: (b, j, 0)),
                  pl.BlockSpec((1, tk, wa), lambda b, j: (b, j, 0)),
                  pl.BlockSpec((1, nh, tk), lambda b, j: (b, 0, j)),
                  pl.BlockSpec((1, nh, LANES), lambda b, j: (b, 0, p_len // LANES))],
        out_specs=pl.BlockSpec((1, t, wa), lambda b, j: (b, 0, 0)),
        out_shape=jax.ShapeDtypeStruct((b, t, wa), BF),
        scratch_shapes=[pltpu.VMEM((nh, t, 1), F32), pltpu.VMEM((nh, t, 1), F32),
                        pltpu.VMEM((nh, t, HEAD_DIM), F32)],
        compiler_params=_cparams(("arbitrary", "arbitrary")),
        name="fox_sample",
    )(qkv, qkv, qkv, cache_k, cache_v, ft_pad, ft_pad)


def _band_sample_body(q_ref, kn_ref, vn_ref, bk_ref, bv_ref, bias_b_ref, bias_n_ref, o_ref, *, nh):
    hd = HEAD_DIM
    q = q_ref[0]
    kn = kn_ref[0]
    vn = vn_ref[0]
    kb = bk_ref[0].astype(BF)
    vb = bv_ref[0].astype(BF)
    outs = []
    for h in range(nh):
        sl = slice(h * hd, (h + 1) * hd)
        sb = _dot_nt(q[:, sl], kb[:, sl]) + bias_b_ref[h]
        sn = _dot_nt(q[:, sl], kn[:, sl]) + bias_n_ref[h]
        m = jnp.maximum(jnp.max(sb, axis=-1, keepdims=True), jnp.max(sn, axis=-1, keepdims=True))
        pb = jnp.exp(sb - m)
        pn = jnp.exp(sn - m)
        l = jnp.sum(pb, axis=-1, keepdims=True) + jnp.sum(pn, axis=-1, keepdims=True)
        o = _dot(pb.astype(BF), vb[:, sl]) + _dot(pn.astype(BF), vn[:, sl])
        outs.append((o / l).astype(BF))
    o_ref[0] = jnp.concatenate(outs, axis=-1)


def _band_sample_call(qkv, buf_k, buf_v, rel_bias, wb):
    b, t, _ = qkv.shape
    lb = buf_k.shape[1]
    nh = wb // HEAD_DIM
    k_pos = jnp.concatenate([jnp.arange(-lb, 0), jnp.arange(t)])
    rel = jnp.arange(t)[:, None] - k_pos[None, :]
    bias = rel_bias[:, jnp.clip(rel, -MAX_REL, MAX_REL) + MAX_REL].astype(F32)
    bias_b, bias_n = bias[:, :, :lb], bias[:, :, lb:]
    return pl.pallas_call(
        functools.partial(_band_sample_body, nh=nh),
        grid=(b,),
        in_specs=[pl.BlockSpec((1, t, wb), lambda b: (b, 0, 3)),
                  pl.BlockSpec((1, t, wb), lambda b: (b, 0, 4)),
                  pl.BlockSpec((1, t, wb), lambda b: (b, 0, 5)),
                  pl.BlockSpec((1, lb, wb), lambda b: (b, 0, 0)),
                  pl.BlockSpec((1, lb, wb), lambda b: (b, 0, 0)),
                  pl.BlockSpec(bias_b.shape, lambda b: (0, 0, 0)),
                  pl.BlockSpec(bias_n.shape, lambda b: (0, 0, 0))],
        out_specs=pl.BlockSpec((1, t, wb), lambda b: (b, 0, 0)),
        out_shape=jax.ShapeDtypeStruct((b, t, wb), BF),
        compiler_params=_cparams(("arbitrary",)),
        name="band_sample",
    )(qkv, qkv, qkv, buf_k, buf_v, bias_b, bias_n)


def _diff_sample_body(q_ref, kn_ref, vn_ref, ck_ref, cv_ref, lam_ref, gsub_ref, o_ref, m_ref, l_ref, acc_ref,
                      *, nh, lam_init, p_len):
    j = pl.program_id(1)
    q = q_ref[0]
    t = q.shape[0]
    dv = 2 * HEAD_DIM
    lane = lax.broadcasted_iota(jnp.int32, (t, dv), 1)

    @pl.when(j == 0)
    def _():
        _osm_init(m_ref, l_ref, acc_ref)

    def attend(k_of, v_of, dist):
        dist2 = jnp.concatenate([dist, dist], axis=0)
        for h in range(nh):
            x = q[:, h * dv:(h + 1) * dv]
            zero = jnp.zeros_like(x)
            q2 = jnp.concatenate([jnp.where(lane < HEAD_DIM, x, zero), jnp.where(lane >= HEAD_DIM, x, zero)], axis=0)
            s = _dot_nt(q2, k_of(h)) - _alibi_slope(h, nh) * dist2
            _osm_update(s, v_of(h), m_ref, l_ref, acc_ref, h)

    tk = ck_ref.shape[1] // nh
    row = lax.broadcasted_iota(jnp.int32, (t, tk), 0)
    col = lax.broadcasted_iota(jnp.int32, (t, tk), 1)
    attend(lambda h: ck_ref[0, pl.ds(h, tk, stride=nh), :].astype(BF),
           lambda h: cv_ref[0, pl.ds(h, tk, stride=nh), :].astype(BF),
           (p_len + row - j * tk - col).astype(F32))

    @pl.when(j == pl.num_programs(1) - 1)
    def _():
        kn = kn_ref[0]
        vn = vn_ref[0]
        rown = lax.broadcasted_iota(jnp.int32, (t, t), 0)
        coln = lax.broadcasted_iota(jnp.int32, (t, t), 1)
        attend(lambda h: kn[:, h * dv:(h + 1) * dv], lambda h: vn[:, h * dv:(h + 1) * dv],
               jnp.abs(rown - coln).astype(F32))
        lam = _lambda(lam_ref, lam_init)
        g = gsub_ref[...]
        outs = []
        for h in range(nh):
            on = acc_ref[h] / l_ref[h]
            o = on[:t] - lam * on[t:]
            o = o * lax.rsqrt(jnp.mean(o * o, axis=-1, keepdims=True) + RMS_EPS)
            outs.append(((o * g) * (1.0 - lam_init)).astype(BF))
        o_ref[0] = jnp.concatenate(outs, axis=-1)


def _diff_sample_call(qkv, cache_k, cache_v, lam_par, g_sub, wc, lam_init):
    b, t, _ = qkv.shape
    nh = wc // (2 * HEAD_DIM)
    p_len = cache_k.shape[1] // nh
    assert (p_len // CHUNK) * CHUNK == p_len and t <= CHUNK
    tk = min(CACHE_TILE_C, p_len)
    return pl.pallas_call(
        functools.partial(_diff_sample_body, nh=nh, lam_init=lam_init, p_len=p_len),
        grid=(b, p_len // tk),
        in_specs=[pl.BlockSpec((1, t, wc), lambda b, j: (b, 0, 0)),
                  pl.BlockSpec((1, t, wc), lambda b, j: (b, 0, 1)),
                  pl.BlockSpec((1, t, wc), lambda b, j: (b, 0, 2)),
                  pl.BlockSpec((1, tk * nh, 2 * HEAD_DIM), lambda b, j: (b, j, 0)),
                  pl.BlockSpec((1, tk * nh, 2 * HEAD_DIM), lambda b, j: (b, j, 0)),
                  pl.BlockSpec(lam_par.shape, lambda b, j: (0, 0)),
                  pl.BlockSpec(g_sub.shape, lambda b, j: (0, 0))],
        out_specs=pl.BlockSpec((1, t, wc), lambda b, j: (b, 0, 0)),
        out_shape=jax.ShapeDtypeStruct((b, t, wc), BF),
        scratch_shapes=[pltpu.VMEM((nh, 2 * t, 1), F32), pltpu.VMEM((nh, 2 * t, 1), F32),
                        pltpu.VMEM((nh, 2 * t, 2 * HEAD_DIM), F32)],
        compiler_params=_cparams(("arbitrary", "arbitrary")),
        name="diff_sample",
    )(qkv, qkv, qkv, cache_k, cache_v, lam_par, g_sub)


def _out_ffn_body(x_ref, oa_ref, ob_ref, gm_ref, g_ref, sh_ref, sc_ref, gf_ref, wo_ref, w1_ref, w3_ref, w2_ref, y_ref,
                  *, f_chunk):
    wa = oa_ref.shape[-1]
    attn = _dot(oa_ref[0], wo_ref[:wa, :]) + _dot(ob_ref[0], wo_ref[wa:, :])
    x1 = x_ref[0] + gm_ref[0, 0, 0] * attn
    h = _modulate(x1, g_ref[...], sh_ref[0, 0, 0], sc_ref[0, 0, 0]).astype(BF)
    f_total = w1_ref.shape[1]
    f = None
    for c0 in range(0, f_total, f_chunk):
        a = _dot(h, w1_ref[:, c0:c0 + f_chunk])
        g = _dot(h, w3_ref[:, c0:c0 + f_chunk])
        u = ((a * _sigmoid(a)) * g).astype(BF)
        part = _dot(u, w2_ref[c0:c0 + f_chunk, :])
        f = part if f is None else f + part
    y_ref[0] = x1 + gf_ref[0, 0, 0] * f


def _out_ffn_call(x, o_a, o_b, gate_m, g, shift, scale, gate_f, wo, w1, w3, w2):
    bx, sx, d = x.shape
    tm = min(ROW_TILE, sx)
    f_total = w1.shape[1]
    f_chunk = f_total // 2 if (f_total // 2) % LANES == 0 else f_total
    row = lambda b, i: (b, i, 0)
    return pl.pallas_call(
        functools.partial(_out_ffn_body, f_chunk=f_chunk),
        grid=(bx, sx // tm),
        in_specs=[pl.BlockSpec((1, tm, d), row),
                  pl.BlockSpec((1, tm, o_a.shape[-1]), row), pl.BlockSpec((1, tm, o_b.shape[-1]), row),
                  _mod_spec(gate_m, tm), _const_spec(g), _mod_spec(shift, tm), _mod_spec(scale, tm),
                  _mod_spec(gate_f, tm), _const_spec(wo), _const_spec(w1), _const_spec(w3), _const_spec(w2)],
        out_specs=pl.BlockSpec((1, tm, d), row),
        out_shape=jax.ShapeDtypeStruct((bx, sx, d), F32),
        compiler_params=_cparams(("arbitrary", "arbitrary")),
        name="out_ffn",
    )(x, o_a, o_b, gate_m[0], g, shift[0], scale[0], gate_f[0], wo, w1, w3, w2)


def _store_tile_rows(ref, x):
    rows, d = x.shape
    nseg = d // LANES
    for s in range(nseg):
        ref[pl.ds(s, rows, stride=nseg), :] = x[:, s * LANES:(s + 1) * LANES]


def _load_tile_rows(ref, rows, d):
    nseg = d // LANES
    return jnp.concatenate([ref[pl.ds(s, rows, stride=nseg), :] for s in range(nseg)], axis=-1)


def _out_router_body(x_ref, o_ref, gm_ref, g_ref, sh_ref, sc_ref, wo_ref, wr_ref, h_in_ref,
                     x1_ref, h_ref, route_ref, *, n_exp):
    del h_in_ref
    x1 = x_ref[0] + gm_ref[0, 0, 0] * _dot(o_ref[0], wo_ref[...])
    x1_ref[0] = x1
    h = _modulate(x1, g_ref[...], sh_ref[0, 0, 0], sc_ref[0, 0, 0])
    hb = h.astype(BF)
    _store_tile_rows(h_ref, hb.astype(F32))
    terms = _dot(hb, wr_ref[...])
    logits = (terms + pltpu.roll(terms, LANES - ROUTE_ROWS, 1)) + pltpu.roll(terms, LANES - 2 * ROUTE_ROWS, 1)
    lt = logits.T[:ROUTE_ROWS, :]
    row = lax.broadcasted_iota(jnp.int32, lt.shape, 0)
    lg = jnp.where(row < n_exp, lt, NEG_INF)
    e = jnp.exp(lg - jnp.max(lg, axis=0, keepdims=True))
    probs = e / jnp.sum(e, axis=0, keepdims=True)
    p1 = jnp.max(probs, axis=0, keepdims=True)
    i1 = jnp.min(jnp.where(probs == p1, row, ROUTE_ROWS), axis=0, keepdims=True)
    rest = jnp.where(row == i1, -1.0, probs)
    p2 = jnp.max(rest, axis=0, keepdims=True)
    i2 = jnp.min(jnp.where(rest == p2, row, ROUTE_ROWS), axis=0, keepdims=True)
    tot = p1 + p2
    route_ref[0] = jnp.where(row == 0, i1.astype(F32),
                             jnp.where(row == 1, i2.astype(F32),
                                       jnp.where(row == 2, p1 / tot, jnp.where(row == 3, p2 / tot, 0.0))))


def _out_router_call(x, o, gate_m, g, shift, scale, wo, wr3, n_exp, tok_offset, h_all):
    bx, sx, d = x.shape
    tm = min(ROW_TILE, sx)
    nt = sx // tm
    nseg = d // LANES
    assert tok_offset % tm == 0 and n_exp <= ROUTE_ROWS
    row = lambda b, i: (b, i, 0)
    return pl.pallas_call(
        functools.partial(_out_router_body, n_exp=n_exp),
        grid=(bx, nt),
        in_specs=[pl.BlockSpec((1, tm, d), row), pl.BlockSpec((1, tm, o.shape[-1]), row),
                  _mod_spec(gate_m, tm), _const_spec(g), _mod_spec(shift, tm), _mod_spec(scale, tm),
                  _const_spec(wo), _const_spec(wr3), pl.BlockSpec(memory_space=pl.ANY)],
        out_specs=[pl.BlockSpec((1, tm, d), row),
                   pl.BlockSpec((tm * nseg, LANES), lambda b, i: (tok_offset // tm + b * nt + i, 0)),
                   pl.BlockSpec((1, ROUTE_ROWS, tm), lambda b, i: (b, 0, i))],
        out_shape=[jax.ShapeDtypeStruct((bx, sx, d), F32),
                   jax.ShapeDtypeStruct(h_all.shape, F32),
                   jax.ShapeDtypeStruct((bx, ROUTE_ROWS, sx), F32)],
        input_output_aliases={8: 1},
        compiler_params=_cparams(("arbitrary", "arbitrary")),
        name="out_router",
    )(x, o, gate_m[0], g, shift[0], scale[0], wo, wr3, h_all)


def _moe_ffn_body(te_ref, nv_ref, idx_ref, idx_next_ref, h_ref, w1_ref, w3_ref, w2_ref, y_ref, xbuf, acc, sem,
                  *, nseg, nf):
    t = pl.program_id(0)
    j = pl.program_id(1)
    nv = nv_ref[0]
    tm, d = acc.shape
    slot = lax.rem(t, 2)

    def issue(rows_ref, s):
        def body(r2, c):
            for u in range(2):
                r = 2 * r2 + u
                src = pl.multiple_of(rows_ref[0, 0, r], nseg)
                dst = pl.multiple_of(r * nseg, nseg)
                pltpu.make_async_copy(h_ref.at[pl.ds(src, nseg)], xbuf.at[s, pl.ds(dst, nseg)],
                                      sem.at[s]).start(priority=u)
            return c
        lax.fori_loop(0, tm // 2, body, 0, unroll=4)

    @pl.when((j == 0) & (t == 0))
    def _():
        issue(idx_ref, 0)

    @pl.when((j == 0) & (t + 1 < nv))
    def _():
        issue(idx_next_ref, 1 - slot)

    @pl.when((j == 0) & (t < nv))
    def _():
        pltpu.make_async_copy(h_ref.at[pl.ds(0, tm * nseg)], xbuf.at[slot], sem.at[slot]).wait()

    @pl.when(t < nv)
    def _():
        xb = _load_tile_rows(xbuf.at[slot], tm, d).astype(BF)
        a = _dot(xb, w1_ref[0])
        g = _dot(xb, w3_ref[0])
        u = ((a * _sigmoid(a)) * g).astype(BF)
        part = _dot(u, w2_ref[0])

        if nf == 1:
            _store_tile_rows(y_ref, part)
        else:
            @pl.when(j == 0)
            def _():
                acc[...] = part

            @pl.when((j > 0) & (j < nf - 1))
            def _():
                acc[...] += part

            @pl.when(j == nf - 1)
            def _():
                _store_tile_rows(y_ref, acc[...] + part)

    @pl.when((t >= nv) & (j == nf - 1))
    def _():
        y_ref[...] = jnp.zeros(y_ref.shape, F32)


def _moe_ffn_call(h_all, row_tok, tile_e, n_valid, w1, w3, w2):
    n_rows = row_tok.shape[0]
    d = w1.shape[1]
    nseg = d // LANES
    f_total = w1.shape[2]
    tf = MOE_F_TILE if f_total % MOE_F_TILE == 0 else f_total
    nf = f_total // tf
    tm = MOE_ROW_TILE
    n_tiles = n_rows // tm
    idx3 = (row_tok * nseg).reshape(n_tiles, 1, tm)

    def fcol(t, j, te, nv):
        return jnp.where(t < nv[0], j, nf - 1)

    grid_spec = pltpu.PrefetchScalarGridSpec(
        num_scalar_prefetch=2,
        grid=(n_tiles, nf),
        in_specs=[pl.BlockSpec((1, 1, tm), lambda t, j, te, nv: (t, 0, 0), memory_space=pltpu.SMEM),
                  pl.BlockSpec((1, 1, tm), lambda t, j, te, nv: (jnp.minimum(t + 1, n_tiles - 1), 0, 0),
                               memory_space=pltpu.SMEM),
                  pl.BlockSpec(memory_space=pl.ANY),
                  pl.BlockSpec((1, d, tf), lambda t, j, te, nv: (te[t], 0, fcol(t, j, te, nv))),
                  pl.BlockSpec((1, d, tf), lambda t, j, te, nv: (te[t], 0, fcol(t, j, te, nv))),
                  pl.BlockSpec((1, tf, d), lambda t, j, te, nv: (te[t], fcol(t, j, te, nv), 0))],
        out_specs=pl.BlockSpec((tm * nseg, LANES), lambda t, j, te, nv: (t, 0)),
        scratch_shapes=[pltpu.VMEM((2, tm * nseg, LANES), F32), pltpu.VMEM((tm, d), F32),
                        pltpu.SemaphoreType.DMA((2,))],
    )
    return pl.pallas_call(
        functools.partial(_moe_ffn_body, nseg=nseg, nf=nf),
        grid_spec=grid_spec,
        out_shape=jax.ShapeDtypeStruct((n_rows * nseg, LANES), F32),
        compiler_params=_cparams(("arbitrary", "arbitrary")),
        name="moe_ffn",
    )(tile_e, n_valid, idx3, idx3, h_all, w1, w3, w2)


def _combine_body(d0_ref, d1_ref, x_ref, gf_ref, w_ref, g_ref, yb_ref, o_ref, buf, sem):
    t = pl.program_id(0)
    nt = pl.num_programs(0)
    _, tc, d = x_ref.shape
    nseg = d // LANES

    def row_copy(idx_ref, r, slot, k):
        src = pl.multiple_of(idx_ref[0, 0, 2 * r + k], nseg)
        dst = pl.multiple_of(r * nseg, nseg)
        return pltpu.make_async_copy(yb_ref.at[pl.ds(src, nseg)], buf.at[slot, k, pl.ds(dst, nseg)], sem.at[slot])

    def issue(idx_ref, slot):
        def body(r, c):
            row_copy(idx_ref, r, slot, 0).start(priority=0)
            row_copy(idx_ref, r, slot, 1).start(priority=1)
            return c
        lax.fori_loop(0, tc, body, 0, unroll=4)

    slot = lax.rem(t, 2)

    @pl.when(t == 0)
    def _():
        issue(d0_ref, 0)

    @pl.when(t + 1 < nt)
    def _():
        issue(d1_ref, 1 - slot)

    for k in range(TOP_K):
        pltpu.make_async_copy(yb_ref.at[pl.ds(0, tc * nseg)], buf.at[slot, k], sem.at[slot]).wait()
    w = jnp.concatenate([w_ref[0], jnp.zeros((LANES - ROUTE_ROWS, tc), F32)], axis=0).T
    f = (w[:, 2:3] * _load_tile_rows(buf.at[slot, 0], tc, d)
         + w[:, 3:4] * _load_tile_rows(buf.at[slot, 1], tc, d))
    y = x_ref[0] + gf_ref[0, 0, 0] * f
    y = y * lax.rsqrt(jnp.mean(y * y, axis=-1, keepdims=True) + RMS_EPS)
    o_ref[0] = y * g_ref[...]


def _combine_call(x1, gate_f, route, g_final, yb, dest):
    bx, sx, d = x1.shape
    nseg = d // LANES
    tc = min(COMBINE_TILE, sx)
    nps = sx // tc
    nt = bx * nps
    dest3 = (dest * nseg).reshape(nt, 1, 2 * tc)
    row = lambda t: (t // nps, t % nps, 0)
    gate_arr, gl, gk = gate_f
    if gate_arr.shape[3] == 1:
        gate_spec = pl.BlockSpec((1, 1, 1, 1, d), lambda t: (gl, gk, t // nps, 0, 0))
    else:
        gate_spec = pl.BlockSpec((1, 1, 1, tc, d), lambda t: (gl, gk, t // nps, t % nps, 0))
    return pl.pallas_call(
        _combine_body,
        grid=(nt,),
        in_specs=[pl.BlockSpec((1, 1, 2 * tc), lambda t: (t, 0, 0), memory_space=pltpu.SMEM),
                  pl.BlockSpec((1, 1, 2 * tc), lambda t: (jnp.minimum(t + 1, nt - 1), 0, 0),
                               memory_space=pltpu.SMEM),
                  pl.BlockSpec((1, tc, d), row), gate_spec,
                  pl.BlockSpec((1, ROUTE_ROWS, tc), lambda t: (t // nps, 0, t % nps)),
                  pl.BlockSpec(g_final.shape, lambda t: (0, 0)),
                  pl.BlockSpec(memory_space=pl.ANY)],
        out_specs=pl.BlockSpec((1, tc, d), row),
        out_shape=jax.ShapeDtypeStruct((bx, sx, d), F32),
        scratch_shapes=[pltpu.VMEM((2, 2, tc * nseg, LANES), F32), pltpu.SemaphoreType.DMA((2,))],
        compiler_params=_cparams(("arbitrary",)),
        name="moe_combine",
    )(dest3, dest3, x1, gate_arr, route, g_final, yb)


def _route_plan(slot_e, n_exp, tm):
    n_slots = slot_e.shape[0]
    onehot = (slot_e[:, None] == jnp.arange(n_exp, dtype=jnp.int32)[None, :]).astype(jnp.int32)
    csum = jnp.cumsum(onehot, axis=0)
    rank = jnp.sum(csum * onehot, axis=1) - 1
    counts = csum[-1]
    padded = ((counts + tm - 1) // tm) * tm
    pad_end = jnp.cumsum(padded)
    pad_start = pad_end - padded
    dest = jnp.sum(onehot * pad_start[None, :], axis=1) + rank
    n_tiles = -(-n_slots // tm) + n_exp
    n_rows = n_tiles * tm
    n_valid = (pad_end[-1] // tm).astype(jnp.int32)
    tile_start = jnp.arange(n_tiles, dtype=jnp.int32) * tm
    tile_e = jnp.minimum(jnp.searchsorted(pad_end, tile_start, side='right'), n_exp - 1).astype(jnp.int32)
    last_e = tile_e[jnp.maximum(n_valid - 1, 0)]
    tile_e = jnp.where(jnp.arange(n_tiles) < n_valid, tile_e, last_e)
    sorted_tok = (jnp.argsort(slot_e, stable=True) // TOP_K).astype(jnp.int32)
    sorted_tok = jnp.concatenate([sorted_tok, jnp.zeros((tm,), jnp.int32)])
    first = jnp.cumsum(counts) - counts - pad_start
    tile_first = jnp.minimum(tile_start + first[tile_e], n_slots)
    row_tok = sorted_tok[(tile_first[:, None] + jnp.arange(tm, dtype=jnp.int32)[None, :]).reshape(n_rows)]
    return dest, row_tok, tile_e, n_valid.reshape(1)


def _moe(parts, h_all, g_final, w1, w3, w2):
    n_exp = w1.shape[0]
    slot_e = jnp.concatenate([r[:, :TOP_K, :].transpose(0, 2, 1).reshape(-1) for _, r, _ in parts], axis=0)
    dest, row_tok, tile_e, n_valid = _route_plan(slot_e.astype(jnp.int32), n_exp, MOE_ROW_TILE)
    yb = _moe_ffn_call(h_all, row_tok, tile_e, n_valid, w1, w3, w2)
    outs = []
    first = 0
    for x1, route, gate_f in parts:
        n_slots = x1.shape[0] * x1.shape[1] * TOP_K
        outs.append(_combine_call(x1, gate_f, route, g_final, yb, dest[first:first + n_slots]))
        first += n_slots
    return outs


def _trunk(x, mods, p, cache, tok_offset, h_all):
    sh_m0, sc_m0, gt_m0, sh_f0, sc_f0, gt_f0 = mods[0]
    sh_m1, sc_m1, gt_m1, sh_f1, sc_f1, gt_f1 = mods[1]
    wa = p['wa']
    wb = p['wb']
    wc = p['wc']
    bx, sx, d = x.shape

    if cache is None:
        keep = min(WINDOW_B, sx)
        qk, vt, ka, va, kb, vb, logf = _inproj_even_call(
            x, p['g_mix0'], sh_m0, sc_m0, p['w_in_even'], p['w_f'], p['b_f'], keep, True)
        o_a = _fox_prompt_call(qk, vt, wa)
        o_b = _band_prompt_call(qk, vt, p['rel_bias'], wa, wb)
        x_l0_state = (ka, va, logf, kb, vb)
    else:
        ck, cv, clogf, bk, bv, cck, ccv, b, t = cache
        qkv, ka, va, kb, vb, logf, lft, = _inproj_even_call(
            x, p['g_mix0'], sh_m0, sc_m0, p['w_in_even'], p['w_f'], p['b_f'], sx, False)
        nh = wa // HEAD_DIM
        p_len = ck.shape[1]
        lft_new = lft.reshape(nh, b, t).transpose(1, 0, 2)
        lp = -(-(p_len + t) // CUMSUM_TILE) * CUMSUM_TILE
        lcat = jnp.concatenate([clogf.transpose(0, 2, 1), lft_new,
                                jnp.zeros((b, nh, lp - p_len - t), F32)], axis=-1)
        ft_pad = _cumsum_call(lcat)
        qkv_b = qkv.reshape(b, t, qkv.shape[-1])
        o_a = _fox_sample_call(qkv_b, ck, cv, ft_pad, wa).reshape(1, b * t, wa)
        o_b = _band_sample_call(qkv_b, bk, bv, p['rel_bias'], wb).reshape(1, b * t, wb)
        x_l0_state = (ka, va, logf, kb, vb)
    x = _out_ffn_call(x, o_a, o_b, gt_m0, p['g_ffn0'], sh_f0, sc_f0, gt_f0,
                      p['w_out_even'], p['w1_dense'], p['w3_dense'], p['w2_dense'])

    if cache is None:
        qk, vt, kc, vc = _inproj_odd_call(x, p['g_mix1'], sh_m1, sc_m1, p['w_in_odd'], True)
        o = _diff_prompt_call(qk, vt, p['lam_par'], p['g_sub'].reshape(-1, 1), wc, p['lam_init'])
    else:
        qkv, kc, vc = _inproj_odd_call(x, p['g_mix1'], sh_m1, sc_m1, p['w_in_odd'], False)
        qkv_b = qkv.reshape(b, t, qkv.shape[-1])
        o = _diff_sample_call(qkv_b, cck, ccv, p['lam_par'], p['g_sub'], wc, p['lam_init']).reshape(1, b * t, wc)
    x1, h_all, route = _out_router_call(x, o, gt_m1, p['g_ffn1'], sh_f1, sc_f1, p['w_out_odd'], p['w_router3'],
                                        p['n_exp'], tok_offset, h_all)
    return (x1, route, gt_f1), h_all, x_l0_state, (kc, vc)


def kernel(x_prompt, x_sample, cache_a_k, cache_a_v, cache_a_logf, cache_b_k, cache_b_v, cache_c_k, cache_c_v,
           c_prompt, c_sample, w_mod, b_mod, g_mix, g_ffn, g_final, w_in_even, b_forget, rel_bias, w_out_even,
           w_in_odd, lambda_q1, lambda_k1, lambda_q2, lambda_k2, g_subln, w_out_odd, w1_dense, w3_dense, w2_dense,
           w_router, w1_moe, w3_moe, w2_moe):
    bp, sp, d = x_prompt.shape
    bs, ts, _ = x_sample.shape
    h_a = cache_a_k.shape[3]
    h_b = cache_b_k.shape[3]
    h_c = cache_c_k.shape[3]
    wa, wb, wc = h_a * HEAD_DIM, h_b * HEAD_DIM, h_c * 2 * HEAD_DIM
    n_exp = w_router.shape[-1]
    assert w_mod.shape[0] == 2, "kernel is written for the 2-layer trunk"

    we = w_in_even[0]
    w_main = jnp.concatenate([we[:, :3 * wa], we[:, 3 * wa + h_a:]], axis=1).astype(BF)
    w_f = jnp.pad(we[:, 3 * wa:3 * wa + h_a], ((0, 0), (0, LANES - h_a))).astype(BF)
    b_f = jnp.pad(b_forget[0], (0, LANES - h_a)).reshape(1, LANES).astype(F32)
    wr = jnp.pad(w_router[0], ((0, 0), (0, ROUTE_ROWS - n_exp)))
    wr_hi = wr.astype(BF)
    wr_r1 = wr - wr_hi.astype(F32)
    wr_mid = wr_r1.astype(BF)
    wr_lo = (wr_r1 - wr_mid.astype(F32)).astype(BF)
    w_router3 = jnp.pad(jnp.concatenate([wr_hi, wr_mid, wr_lo], axis=1), ((0, 0), (0, LANES - 3 * ROUTE_ROWS)))
    params = dict(
        wa=wa, wb=wb, wc=wc, n_exp=n_exp, lam_init=0.8 - 0.6 * math.exp(-0.3 * 1),
        g_mix0=g_mix[0:1], g_mix1=g_mix[1:2], g_ffn0=g_ffn[0:1], g_ffn1=g_ffn[1:2],
        g_final=g_final.reshape(1, d),
        w_in_even=w_main, w_f=w_f, b_f=b_f, rel_bias=rel_bias[0],
        w_out_even=w_out_even[0].astype(BF), w_in_odd=w_in_odd[0].astype(BF),
        lam_par=jnp.concatenate([lambda_q1, lambda_k1, lambda_q2, lambda_k2], axis=0),
        g_sub=g_subln[0:1], w_out_odd=w_out_odd[0].astype(BF),
        w1_dense=w1_dense[0].astype(BF), w3_dense=w3_dense[0].astype(BF), w2_dense=w2_dense[0].astype(BF),
        w_router3=w_router3,
        w1_moe=_cast_bf16_call(w1_moe[0]), w3_moe=_cast_bf16_call(w3_moe[0]), w2_moe=_cast_bf16_call(w2_moe[0]),
    )

    mod = _mod_call(jnp.concatenate([c_prompt, c_sample], axis=0), w_mod, b_mod)

    def mods_for(rows, per_row_len):
        m = mod[:, rows].reshape(2, -1, 6, d).transpose(0, 2, 1, 3)
        if per_row_len:
            m = jnp.repeat(m, per_row_len, axis=2)[:, :, None]
        else:
            m = m[:, :, :, None]
        return [[(m, l, k) for k in range(6)] for l in range(2)]

    h_all = jnp.zeros(((bp * sp + bs * ts) * (d // LANES), LANES), F32)
    moe_p, h_all, ev_p, od_p = _trunk(x_prompt, mods_for(slice(0, bp), 0), params, None, 0, h_all)
    cache = (cache_a_k[0].reshape(bs, -1, wa), cache_a_v[0].reshape(bs, -1, wa), cache_a_logf[0],
             cache_b_k[0].reshape(bs, -1, wb), cache_b_v[0].reshape(bs, -1, wb),
             cache_c_k[0].reshape(bs, -1, 2 * HEAD_DIM), cache_c_v[0].reshape(bs, -1, 2 * HEAD_DIM), bs, ts)
    moe_s, h_all, ev_s, od_s = _trunk(x_sample.reshape(1, bs * ts, d), mods_for(slice(bp, bp + bs), ts), params,
                                      cache, bp * sp, h_all)
    y_p, y_s = _moe([moe_p, moe_s], h_all, params['g_final'], params['w1_moe'], params['w3_moe'], params['w2_moe'])

    ka, va, logf, kb, vb = ev_p
    keep = kb.shape[1]
    out_p = (ka.reshape(1, bp, sp, h_a, HEAD_DIM), va.reshape(1, bp, sp, h_a, HEAD_DIM),
             logf.reshape(1, bp, sp, h_a),
             kb.reshape(1, bp, keep, h_b, HEAD_DIM), vb.reshape(1, bp, keep, h_b, HEAD_DIM),
             od_p[0].reshape(1, bp, sp, h_c, 2 * HEAD_DIM), od_p[1].reshape(1, bp, sp, h_c, 2 * HEAD_DIM))
    ka, va, logf, kb, vb = ev_s
    new_bk = jnp.concatenate([cache_b_k[0], kb.reshape(bs, ts, h_b, HEAD_DIM)], axis=1)[:, ts:]
    new_bv = jnp.concatenate([cache_b_v[0], vb.reshape(bs, ts, h_b, HEAD_DIM)], axis=1)[:, ts:]
    out_s = (ka.reshape(1, bs, ts, h_a, HEAD_DIM), va.reshape(1, bs, ts, h_a, HEAD_DIM),
             logf.reshape(1, bs, ts, h_a), new_bk[None], new_bv[None],
             od_s[0].reshape(1, bs, ts, h_c, 2 * HEAD_DIM), od_s[1].reshape(1, bs, ts, h_c, 2 * HEAD_DIM))
    return (y_p, y_s.reshape(bs, ts, d)) + out_p + out_s
```
